```python
import math
import jax, jax.numpy as jnp
from jax import lax
import numpy as np

D_MODEL = 1024
BATCH = 8
SEQ = 8192
DEPTH = 1

N_META = 16
MLA_HEADS = 8
QK_NOPE_DIM = 128
QK_ROPE_DIM = 64
V_HEAD_DIM = 128
Q_LORA_RANK = 384
KV_LORA_RANK = 256
ROPE_THETA = 10000.0
Q_BLOCK = 128
SOFTMAX_SCALE = (QK_NOPE_DIM + QK_ROPE_DIM) ** -0.5
D_ATTN = MLA_HEADS * V_HEAD_DIM
SSM_HEADS = 16
SSM_HEAD_DIM = 64
SSM_GROUPS = 2
HEADS_PER_GROUP = SSM_HEADS // SSM_GROUPS
SSM_STATE = 128
SSM_CONV = 4
CHUNK = 128
D_SSM = SSM_HEADS * SSM_HEAD_DIM
D_XBC = D_SSM + 2 * SSM_GROUPS * SSM_STATE
D_MIX = D_ATTN + D_SSM
D_FF = 2816
FFN_CONV = 3
EPS = 1e-6
IN_SPLITS = (Q_LORA_RANK, KV_LORA_RANK, QK_ROPE_DIM, D_SSM, D_XBC, SSM_HEADS)
D_IN = sum(IN_SPLITS)

kernel_name = 'hymba_mla_ssd_convffn_layer'


def rms_norm(x, gain):
    xf = x.astype(jnp.float32)
    y = xf * lax.rsqrt(jnp.mean(xf * xf, axis=-1, keepdims=True) + EPS)
    return (y * gain.astype(jnp.float32)).astype(x.dtype)


def causal_dwconv(x, w, b):
    k = w.shape[0]
    y = lax.conv_general_dilated(x, w[:, None, :].astype(x.dtype), window_strides=(1,),
                                 padding=[(k - 1, 0)], dimension_numbers=('NWC', 'WIO', 'NWC'),
                                 feature_group_count=x.shape[-1])
    return y + b.astype(x.dtype)


def split_cols(t, sizes):
    idx = np.cumsum(sizes)[:-1].tolist()
    return jnp.split(t, idx, axis=-1)


def rope_tables(n):
    inv = ROPE_THETA ** (-jnp.arange(0, QK_ROPE_DIM, 2, dtype=jnp.float32) / QK_ROPE_DIM)
    ang = jnp.arange(n, dtype=jnp.float32)[:, None] * inv[None, :]
    return jnp.cos(ang), jnp.sin(ang)


def apply_rope(x, cos, sin):
    x1, x2 = jnp.split(x, 2, axis=-1)
    cos = cos.astype(x.dtype)
    sin = sin.astype(x.dtype)
    return jnp.concatenate([x1 * cos - x2 * sin, x1 * sin + x2 * cos], axis=-1)


def attend_block(q_nope, q_pe, q_pos, k_nope, k_pe, v, k_pos):
    s = jnp.einsum('bqhd,bkhd->bhqk', q_nope, k_nope) + jnp.einsum('bqhr,bkr->bhqk', q_pe, k_pe)
    s = s.astype(jnp.float32) * SOFTMAX_SCALE
    s = jnp.where(k_pos[None, :] <= q_pos[:, None], s, -jnp.inf)
    p = jax.nn.softmax(s, axis=-1).astype(v.dtype)
    return jnp.einsum('bhqk,bkhd->bqhd', p, v)


def mla_group(q_c, kv_c, k_pe_raw, q_a_norm, w_uq, kv_a_norm, w_ukv, cos, sin):
    bsz, L = q_c.shape[0], q_c.shape[1]
    S = L - N_META
    q = jnp.einsum('blr,rhd->blhd', rms_norm(q_c, q_a_norm), w_uq)
    q_nope = q[..., :QK_NOPE_DIM]
    q_pe = apply_rope(q[..., QK_NOPE_DIM:], cos[:, None, :], sin[:, None, :])
    kv = jnp.einsum('blc,chd->blhd', rms_norm(kv_c, kv_a_norm), w_ukv)
    k_nope, v = kv[..., :QK_NOPE_DIM], kv[..., QK_NOPE_DIM:]
    k_pe = apply_rope(k_pe_raw, cos, sin)
    pos = jnp.arange(L)
    o_meta = attend_block(q_nope[:, :N_META], q_pe[:, :N_META], pos[:N_META],
                          k_nope[:, :N_META], k_pe[:, :N_META], v[:, :N_META], pos[:N_META])
    nb = S // Q_BLOCK

    def to_blocks(t):
        t = t[:, N_META:].reshape((bsz, nb, Q_BLOCK) + t.shape[2:])
        return jnp.moveaxis(t, 1, 0)

    o_real = lax.map(lambda blk: attend_block(blk[0], blk[1], blk[2], k_nope, k_pe, v, pos),
                     (to_blocks(q_nope), to_blocks(q_pe), pos[N_META:].reshape(nb, Q_BLOCK)))
    o_real = jnp.moveaxis(o_real, 0, 1).reshape(bsz, S, MLA_HEADS, V_HEAD_DIM)
    o = jnp.concatenate([o_meta, o_real], axis=1)
    return o.reshape(bsz, L, D_ATTN)


def ssd_chunks(xs, dt, A, Bm, Cm, init):
    Acs = jnp.cumsum(dt * A, axis=2)
    Q = xs.shape[2]
    causal = jnp.tril(jnp.ones((Q, Q), dtype=bool))[:, :, None, None]
    seg = Acs[:, :, :, None] - Acs[:, :, None, :]
    decay_in = jnp.exp(jnp.where(causal, seg, -jnp.inf))
    xdt = xs * dt[..., None]
    cb = jnp.einsum('bclgn,bcsgn->bclsg', Cm, Bm)
    y_diag = jnp.einsum('bclsgh,bcsghp->bclghp', cb[..., None] * decay_in, xdt)
    decay_to_end = jnp.exp(Acs[:, :, -1:] - Acs)
    states = jnp.einsum('bcsgn,bcsghp->bcghpn', Bm, xdt * decay_to_end[..., None])
    chunk_decay = jnp.exp(Acs[:, :, -1])

    def step(carry, inp):
        st, dec = inp
        return carry * dec[..., None, None] + st, carry

    final, prev = lax.scan(step, init, (jnp.moveaxis(states, 1, 0), jnp.moveaxis(chunk_decay, 1, 0)))
    prev = jnp.moveaxis(prev, 0, 1)
    y_off = jnp.einsum('bclgn,bcghpn->bclghp', Cm, prev) * jnp.exp(Acs)[..., None]
    return y_diag + y_off, final


def ssd_group(z, xbc, dt_raw, conv_w, conv_b, dt_bias, A_log, D, norm_gain):
    bsz, L = z.shape[0], z.shape[1]
    S = L - N_META
    nc = S // CHUNK
    f32 = jnp.float32
    xbc = jax.nn.silu(causal_dwconv(xbc, conv_w, conv_b)).astype(f32)
    xs, Bm, Cm = split_cols(xbc, (D_SSM, SSM_GROUPS * SSM_STATE, SSM_GROUPS * SSM_STATE))
    xs = xs.reshape(bsz, L, SSM_GROUPS, HEADS_PER_GROUP, SSM_HEAD_DIM)
    Bm = Bm.reshape(bsz, L, SSM_GROUPS, SSM_STATE)
    Cm = Cm.reshape(bsz, L, SSM_GROUPS, SSM_STATE)
    dt = jax.nn.softplus(dt_raw.astype(f32) + dt_bias.astype(f32)).reshape(bsz, L, SSM_GROUPS, HEADS_PER_GROUP)
    A = -jnp.exp(A_log.astype(f32)).reshape(SSM_GROUPS, HEADS_PER_GROUP)

    def meta_part(t):
        return t[:, None, :N_META]

    def real_part(t):
        return t[:, N_META:].reshape((bsz, nc, CHUNK) + t.shape[2:])

    init = jnp.zeros((bsz, SSM_GROUPS, HEADS_PER_GROUP, SSM_HEAD_DIM, SSM_STATE), f32)
    y_meta, h_meta = ssd_chunks(meta_part(xs), meta_part(dt), A, meta_part(Bm), meta_part(Cm), init)
    y_real, _ = ssd_chunks(real_part(xs), real_part(dt), A, real_part(Bm), real_part(Cm), h_meta)
    y = jnp.concatenate([y_meta.reshape(bsz, N_META, SSM_GROUPS, HEADS_PER_GROUP, SSM_HEAD_DIM),
                         y_real.reshape(bsz, S, SSM_GROUPS, HEADS_PER_GROUP, SSM_HEAD_DIM)], axis=1)
    y = y + D.astype(f32).reshape(SSM_GROUPS, HEADS_PER_GROUP)[..., None] * xs
    gsz = HEADS_PER_GROUP * SSM_HEAD_DIM
    y = y.reshape(bsz, L, SSM_GROUPS, gsz) * jax.nn.silu(z.astype(f32)).reshape(bsz, L, SSM_GROUPS, gsz)
    y = y * lax.rsqrt(jnp.mean(y * y, axis=-1, keepdims=True) + EPS)
    return (y.reshape(bsz, L, D_SSM) * norm_gain.astype(f32)).astype(z.dtype)


def conv_ffn(h, w_up, conv_w, conv_b, w_down):
    u = causal_dwconv(h @ w_up, conv_w, conv_b)
    g, v = jnp.split(u, 2, axis=-1)
    return (jax.nn.silu(g) * v) @ w_down


def _fwd_setup_inputs(seed: int = 0) -> dict:
    key = jax.random.key(seed)
    ks = jax.random.split(key, 26)
    f32 = jnp.float32

    def nrm(k, shape, scale):
        return scale * jax.random.normal(k, shape, f32)

    def gain(k, n):
        return 1.0 + 0.05 * jax.random.normal(k, (DEPTH, n), f32)

    dt0 = jnp.exp(jax.random.uniform(ks[17], (DEPTH, SSM_HEADS), f32, math.log(1e-3), math.log(1e-1)))
    return {
        'x': nrm(ks[0], (BATCH, SEQ, D_MODEL), 1.0),
        'meta_tokens': nrm(ks[1], (N_META, D_MODEL), 1.0),
        'norm_mix_pre': gain(ks[2], D_MODEL),
        'norm_mix_post': gain(ks[3], D_MODEL),
        'norm_ffn_pre': gain(ks[4], D_MODEL),
        'norm_ffn_post': gain(ks[5], D_MODEL),
        'w_in': nrm(ks[6], (DEPTH, D_MODEL, D_IN), D_MODEL ** -0.5),
        'q_a_norm': gain(ks[7], Q_LORA_RANK),
        'w_uq': nrm(ks[8], (DEPTH, Q_LORA_RANK, MLA_HEADS, QK_NOPE_DIM + QK_ROPE_DIM), Q_LORA_RANK ** -0.5),
        'kv_a_norm': gain(ks[9], KV_LORA_RANK),
        'w_ukv': nrm(ks[10], (DEPTH, KV_LORA_RANK, MLA_HEADS, QK_NOPE_DIM + V_HEAD_DIM), KV_LORA_RANK ** -0.5),
        'attn_out_norm': gain(ks[11], D_ATTN),
        'ssm_conv_w': nrm(ks[12], (DEPTH, SSM_CONV, D_XBC), SSM_CONV ** -0.5),
        'ssm_conv_b': nrm(ks[13], (DEPTH, D_XBC), 0.02),
        'ssm_dt_bias': dt0 + jnp.log(-jnp.expm1(-dt0)),
        'ssm_A_log': jnp.log(jax.random.uniform(ks[14], (DEPTH, SSM_HEADS), f32, 1.0, 16.0)),
        'ssm_D': 1.0 + 0.1 * jax.random.normal(ks[15], (DEPTH, SSM_HEADS), f32),
        'ssm_norm': gain(ks[16], D_SSM),
        'w_out': nrm(ks[18], (DEPTH, D_MIX, D_MODEL), D_MIX ** -0.5),
        'w_up': nrm(ks[19], (DEPTH, D_MODEL, 2 * D_FF), D_MODEL ** -0.5),
        'ffn_conv_w': nrm(ks[20], (DEPTH, FFN_CONV, 2 * D_FF), FFN_CONV ** -0.5),
        'ffn_conv_b': nrm(ks[21], (DEPTH, 2 * D_FF), 0.02),
        'w_down': nrm(ks[22], (DEPTH, D_FF, D_MODEL), D_FF ** -0.5),
    }


def _fwd_reference(x, meta_tokens, norm_mix_pre, norm_mix_post, norm_ffn_pre, norm_ffn_post, w_in,
              q_a_norm, w_uq, kv_a_norm, w_ukv, attn_out_norm, ssm_conv_w, ssm_conv_b, ssm_dt_bias,
              ssm_A_log, ssm_D, ssm_norm, w_out, w_up, ffn_conv_w, ffn_conv_b, w_down):
    bsz = x.shape[0]
    L = N_META + x.shape[1]
    meta = jnp.broadcast_to(meta_tokens[None].astype(x.dtype), (bsz, N_META, D_MODEL))
    h = jnp.concatenate([meta, x], axis=1)
    cos, sin = rope_tables(L)
    for l in range(DEPTH):
        hn = rms_norm(h, norm_mix_pre[l])
        q_c, kv_c, k_pe, z, xbc, dt_raw = split_cols(hn @ w_in[l], IN_SPLITS)
        attn = mla_group(q_c, kv_c, k_pe, q_a_norm[l], w_uq[l], kv_a_norm[l], w_ukv[l], cos, sin)
        ssm = ssd_group(z, xbc, dt_raw, ssm_conv_w[l], ssm_conv_b[l], ssm_dt_bias[l], ssm_A_log[l],
                        ssm_D[l], ssm_norm[l])
        mix = jnp.concatenate([rms_norm(attn, attn_out_norm[l]), ssm], axis=-1) @ w_out[l]
        h = h + rms_norm(mix, norm_mix_post[l])
        hn = rms_norm(h, norm_ffn_pre[l])
        h = h + rms_norm(conv_ffn(hn, w_up[l], ffn_conv_w[l], ffn_conv_b[l], w_down[l]), norm_ffn_post[l])
    return h[:, N_META:]


import jax as _jax
import jax.numpy as _jnp

TWIN_FORMAT = 'train_step'
FWD_PARAMS = ['x', 'meta_tokens', 'norm_mix_pre', 'norm_mix_post', 'norm_ffn_pre', 'norm_ffn_post', 'w_in', 'q_a_norm', 'w_uq', 'kv_a_norm', 'w_ukv', 'attn_out_norm', 'ssm_conv_w', 'ssm_conv_b', 'ssm_dt_bias', 'ssm_A_log', 'ssm_D', 'ssm_norm', 'w_out', 'w_up', 'ffn_conv_w', 'ffn_conv_b', 'w_down']
TWIN_WEIGHTS = ['meta_tokens', 'norm_mix_pre', 'norm_mix_post', 'norm_ffn_pre', 'norm_ffn_post', 'w_in', 'q_a_norm', 'w_uq', 'kv_a_norm', 'w_ukv', 'attn_out_norm', 'ssm_conv_w', 'ssm_conv_b', 'ssm_dt_bias', 'ssm_A_log', 'ssm_D', 'ssm_norm', 'w_out', 'w_up', 'ffn_conv_w', 'ffn_conv_b', 'w_down']
TWIN_DIFF_INPUT = 'x'
TWIN_INPUTS = ['x', 'meta_tokens', 'norm_mix_pre', 'norm_mix_post', 'norm_ffn_pre', 'norm_ffn_post', 'w_in', 'q_a_norm', 'w_uq', 'kv_a_norm', 'w_ukv', 'attn_out_norm', 'ssm_conv_w', 'ssm_conv_b', 'ssm_dt_bias', 'ssm_A_log', 'ssm_D', 'ssm_norm', 'w_out', 'w_up', 'ffn_conv_w', 'ffn_conv_b', 'w_down', 'loss_target', 'm_meta_tokens', 'm_norm_mix_pre', 'm_norm_mix_post', 'm_norm_ffn_pre', 'm_norm_ffn_post', 'm_w_in', 'm_q_a_norm', 'm_w_uq', 'm_kv_a_norm', 'm_w_ukv', 'm_attn_out_norm', 'm_ssm_conv_w', 'm_ssm_conv_b', 'm_ssm_dt_bias', 'm_ssm_A_log', 'm_ssm_D', 'm_ssm_norm', 'm_w_out', 'm_w_up', 'm_ffn_conv_w', 'm_ffn_conv_b', 'm_w_down', 'v_meta_tokens', 'v_norm_mix_pre', 'v_norm_mix_post', 'v_norm_ffn_pre', 'v_norm_ffn_post', 'v_w_in', 'v_q_a_norm', 'v_w_uq', 'v_kv_a_norm', 'v_w_ukv', 'v_attn_out_norm', 'v_ssm_conv_w', 'v_ssm_conv_b', 'v_ssm_dt_bias', 'v_ssm_A_log', 'v_ssm_D', 'v_ssm_norm', 'v_w_out', 'v_w_up', 'v_ffn_conv_w', 'v_ffn_conv_b', 'v_w_down']
TWIN_OUTPUTS = ['loss', 'grad_x', 'grad_meta_tokens', 'grad_norm_mix_pre', 'grad_norm_mix_post', 'grad_norm_ffn_pre', 'grad_norm_ffn_post', 'grad_w_in', 'grad_q_a_norm', 'grad_w_uq', 'grad_kv_a_norm', 'grad_w_ukv', 'grad_attn_out_norm', 'grad_ssm_conv_w', 'grad_ssm_conv_b', 'grad_ssm_dt_bias', 'grad_ssm_A_log', 'grad_ssm_D', 'grad_ssm_norm', 'grad_w_out', 'grad_w_up', 'grad_ffn_conv_w', 'grad_ffn_conv_b', 'grad_w_down', 'delta_meta_tokens', 'delta_norm_mix_pre', 'delta_norm_mix_post', 'delta_norm_ffn_pre', 'delta_norm_ffn_post', 'delta_w_in', 'delta_q_a_norm', 'delta_w_uq', 'delta_kv_a_norm', 'delta_w_ukv', 'delta_attn_out_norm', 'delta_ssm_conv_w', 'delta_ssm_conv_b', 'delta_ssm_dt_bias', 'delta_ssm_A_log', 'delta_ssm_D', 'delta_ssm_norm', 'delta_w_out', 'delta_w_up', 'delta_ffn_conv_w', 'delta_ffn_conv_b', 'delta_w_down', 'new_m_meta_tokens', 'new_m_norm_mix_pre', 'new_m_norm_mix_post', 'new_m_norm_ffn_pre', 'new_m_norm_ffn_post', 'new_m_w_in', 'new_m_q_a_norm', 'new_m_w_uq', 'new_m_kv_a_norm', 'new_m_w_ukv', 'new_m_attn_out_norm', 'new_m_ssm_conv_w', 'new_m_ssm_conv_b', 'new_m_ssm_dt_bias', 'new_m_ssm_A_log', 'new_m_ssm_D', 'new_m_ssm_norm', 'new_m_w_out', 'new_m_w_up', 'new_m_ffn_conv_w', 'new_m_ffn_conv_b', 'new_m_w_down', 'new_v_meta_tokens', 'new_v_norm_mix_pre', 'new_v_norm_mix_post', 'new_v_norm_ffn_pre', 'new_v_norm_ffn_post', 'new_v_w_in', 'new_v_q_a_norm', 'new_v_w_uq', 'new_v_kv_a_norm', 'new_v_w_ukv', 'new_v_attn_out_norm', 'new_v_ssm_conv_w', 'new_v_ssm_conv_b', 'new_v_ssm_dt_bias', 'new_v_ssm_A_log', 'new_v_ssm_D', 'new_v_ssm_norm', 'new_v_w_out', 'new_v_w_up', 'new_v_ffn_conv_w', 'new_v_ffn_conv_b', 'new_v_w_down']
TWIN_LEAF_KINDS = {'loss': 'loss', 'grad_x': 'grad_x', 'grad_meta_tokens': 'grad_w', 'grad_norm_mix_pre': 'grad_w', 'grad_norm_mix_post': 'grad_w', 'grad_norm_ffn_pre': 'grad_w', 'grad_norm_ffn_post': 'grad_w', 'grad_w_in': 'grad_w', 'grad_q_a_norm': 'grad_w', 'grad_w_uq': 'grad_w', 'grad_kv_a_norm': 'grad_w', 'grad_w_ukv': 'grad_w', 'grad_attn_out_norm': 'grad_w', 'grad_ssm_conv_w': 'grad_w', 'grad_ssm_conv_b': 'grad_w', 'grad_ssm_dt_bias': 'grad_w', 'grad_ssm_A_log': 'grad_w', 'grad_ssm_D': 'grad_w', 'grad_ssm_norm': 'grad_w', 'grad_w_out': 'grad_w', 'grad_w_up': 'grad_w', 'grad_ffn_conv_w': 'grad_w', 'grad_ffn_conv_b': 'grad_w', 'grad_w_down': 'grad_w', 'delta_meta_tokens': 'delta_w', 'delta_norm_mix_pre': 'delta_w', 'delta_norm_mix_post': 'delta_w', 'delta_norm_ffn_pre': 'delta_w', 'delta_norm_ffn_post': 'delta_w', 'delta_w_in': 'delta_w', 'delta_q_a_norm': 'delta_w', 'delta_w_uq': 'delta_w', 'delta_kv_a_norm': 'delta_w', 'delta_w_ukv': 'delta_w', 'delta_attn_out_norm': 'delta_w', 'delta_ssm_conv_w': 'delta_w', 'delta_ssm_conv_b': 'delta_w', 'delta_ssm_dt_bias': 'delta_w', 'delta_ssm_A_log': 'delta_w', 'delta_ssm_D': 'delta_w', 'delta_ssm_norm': 'delta_w', 'delta_w_out': 'delta_w', 'delta_w_up': 'delta_w', 'delta_ffn_conv_w': 'delta_w', 'delta_ffn_conv_b': 'delta_w', 'delta_w_down': 'delta_w', 'new_m_meta_tokens': 'new_m', 'new_m_norm_mix_pre': 'new_m', 'new_m_norm_mix_post': 'new_m', 'new_m_norm_ffn_pre': 'new_m', 'new_m_norm_ffn_post': 'new_m', 'new_m_w_in': 'new_m', 'new_m_q_a_norm': 'new_m', 'new_m_w_uq': 'new_m', 'new_m_kv_a_norm': 'new_m', 'new_m_w_ukv': 'new_m', 'new_m_attn_out_norm': 'new_m', 'new_m_ssm_conv_w': 'new_m', 'new_m_ssm_conv_b': 'new_m', 'new_m_ssm_dt_bias': 'new_m', 'new_m_ssm_A_log': 'new_m', 'new_m_ssm_D': 'new_m', 'new_m_ssm_norm': 'new_m', 'new_m_w_out': 'new_m', 'new_m_w_up': 'new_m', 'new_m_ffn_conv_w': 'new_m', 'new_m_ffn_conv_b': 'new_m', 'new_m_w_down': 'new_m', 'new_v_meta_tokens': 'new_v', 'new_v_norm_mix_pre': 'new_v', 'new_v_norm_mix_post': 'new_v', 'new_v_norm_ffn_pre': 'new_v', 'new_v_norm_ffn_post': 'new_v', 'new_v_w_in': 'new_v', 'new_v_q_a_norm': 'new_v', 'new_v_w_uq': 'new_v', 'new_v_kv_a_norm': 'new_v', 'new_v_w_ukv': 'new_v', 'new_v_attn_out_norm': 'new_v', 'new_v_ssm_conv_w': 'new_v', 'new_v_ssm_conv_b': 'new_v', 'new_v_ssm_dt_bias': 'new_v', 'new_v_ssm_A_log': 'new_v', 'new_v_ssm_D': 'new_v', 'new_v_ssm_norm': 'new_v', 'new_v_w_out': 'new_v', 'new_v_w_up': 'new_v', 'new_v_ffn_conv_w': 'new_v', 'new_v_ffn_conv_b': 'new_v', 'new_v_w_down': 'new_v'}


def _forward(args):
    return _fwd_reference(*[args[k] for k in FWD_PARAMS])


def _output_shape():
    def fwd():
        inp = _fwd_setup_inputs(0)
        return _fwd_reference(*[inp[k] for k in FWD_PARAMS])
    out = _jax.eval_shape(fwd)
    return out.shape, out.dtype

N_MICROBATCH = 1
ADAM_LR = 0.001
ADAM_B1 = 0.9
ADAM_B2 = 0.999
ADAM_EPS = 1e-08
ADAM_WD = 0.01
ADAM_STEP = 10
PER_EXAMPLE_BATCH_AXIS = {'x': 0, 'loss_target': 0}
SHARED_INPUTS = []
_WEIGHT_DTYPES = {'meta_tokens': _jnp.float32, 'norm_mix_pre': _jnp.float32, 'norm_mix_post': _jnp.float32, 'norm_ffn_pre': _jnp.float32, 'norm_ffn_post': _jnp.float32, 'w_in': _jnp.float32, 'q_a_norm': _jnp.float32, 'w_uq': _jnp.float32, 'kv_a_norm': _jnp.float32, 'w_ukv': _jnp.float32, 'attn_out_norm': _jnp.float32, 'ssm_conv_w': _jnp.float32, 'ssm_conv_b': _jnp.float32, 'ssm_dt_bias': _jnp.float32, 'ssm_A_log': _jnp.float32, 'ssm_D': _jnp.float32, 'ssm_norm': _jnp.float32, 'w_out': _jnp.float32, 'w_up': _jnp.float32, 'ffn_conv_w': _jnp.float32, 'ffn_conv_b': _jnp.float32, 'w_down': _jnp.float32}
MOMENT_SCALE = {'meta_tokens': 8.162817e-02, 'norm_mix_pre': 1.338995e+00, 'norm_mix_post': 6.384720e+01, 'norm_ffn_pre': 9.113013e-01, 'norm_ffn_post': 6.412429e+01, 'w_in': 7.223229e-01, 'q_a_norm': 1.189949e+00, 'w_uq': 6.143168e-01, 'kv_a_norm': 3.177940e+00, 'w_ukv': 8.095654e-01, 'attn_out_norm': 9.123046e-01, 'ssm_conv_w': 5.006006e-01, 'ssm_conv_b': 1.317852e+00, 'ssm_dt_bias': 8.612073e-01, 'ssm_A_log': 4.003465e+00, 'ssm_D': 4.398379e+00, 'ssm_norm': 7.523361e-01, 'w_out': 1.220171e+00, 'w_up': 3.988278e-01, 'ffn_conv_w': 4.858442e-01, 'ffn_conv_b': 1.798750e+00, 'w_down': 7.817818e-01}


def _to_microbatches(a, axis):
    t = _jnp.moveaxis(a, axis, 0)
    t = t.reshape((N_MICROBATCH, t.shape[0] // N_MICROBATCH) + t.shape[1:])
    return _jnp.moveaxis(t, 1, axis + 1)


def setup_inputs(seed: int = 0) -> dict:
    inp = _fwd_setup_inputs(seed)
    key = _jax.random.fold_in(_jax.random.key(seed), 7919)
    shape, _ = _output_shape()
    out = dict(inp)
    out["loss_target"] = _jax.random.normal(_jax.random.fold_in(key, 0), shape, _jnp.float32)
    for i, name in enumerate(TWIN_WEIGHTS):
        w = inp[name].astype(_jnp.float32)
        if MOMENT_SCALE is None:
            s = _jnp.sqrt(_jnp.mean(_jnp.square(w)) + 1e-30)
        else:
            s = MOMENT_SCALE[name]
        km, kv = _jax.random.split(_jax.random.fold_in(key, i + 1))
        out[name] = w
        out["m_" + name] = s * _jax.random.normal(km, w.shape, _jnp.float32)
        out["v_" + name] = (s * s) * _jax.random.uniform(kv, w.shape, _jnp.float32, 0.5, 1.5)
    if N_MICROBATCH > 1:
        for name, axis in PER_EXAMPLE_BATCH_AXIS.items():
            out[name] = _to_microbatches(out[name], axis)
    return {'x': out['x'], 'meta_tokens': out['meta_tokens'], 'norm_mix_pre': out['norm_mix_pre'], 'norm_mix_post': out['norm_mix_post'], 'norm_ffn_pre': out['norm_ffn_pre'], 'norm_ffn_post': out['norm_ffn_post'], 'w_in': out['w_in'], 'q_a_norm': out['q_a_norm'], 'w_uq': out['w_uq'], 'kv_a_norm': out['kv_a_norm'], 'w_ukv': out['w_ukv'], 'attn_out_norm': out['attn_out_norm'], 'ssm_conv_w': out['ssm_conv_w'], 'ssm_conv_b': out['ssm_conv_b'], 'ssm_dt_bias': out['ssm_dt_bias'], 'ssm_A_log': out['ssm_A_log'], 'ssm_D': out['ssm_D'], 'ssm_norm': out['ssm_norm'], 'w_out': out['w_out'], 'w_up': out['w_up'], 'ffn_conv_w': out['ffn_conv_w'], 'ffn_conv_b': out['ffn_conv_b'], 'w_down': out['w_down'], 'loss_target': out['loss_target'], 'm_meta_tokens': out['m_meta_tokens'], 'm_norm_mix_pre': out['m_norm_mix_pre'], 'm_norm_mix_post': out['m_norm_mix_post'], 'm_norm_ffn_pre': out['m_norm_ffn_pre'], 'm_norm_ffn_post': out['m_norm_ffn_post'], 'm_w_in': out['m_w_in'], 'm_q_a_norm': out['m_q_a_norm'], 'm_w_uq': out['m_w_uq'], 'm_kv_a_norm': out['m_kv_a_norm'], 'm_w_ukv': out['m_w_ukv'], 'm_attn_out_norm': out['m_attn_out_norm'], 'm_ssm_conv_w': out['m_ssm_conv_w'], 'm_ssm_conv_b': out['m_ssm_conv_b'], 'm_ssm_dt_bias': out['m_ssm_dt_bias'], 'm_ssm_A_log': out['m_ssm_A_log'], 'm_ssm_D': out['m_ssm_D'], 'm_ssm_norm': out['m_ssm_norm'], 'm_w_out': out['m_w_out'], 'm_w_up': out['m_w_up'], 'm_ffn_conv_w': out['m_ffn_conv_w'], 'm_ffn_conv_b': out['m_ffn_conv_b'], 'm_w_down': out['m_w_down'], 'v_meta_tokens': out['v_meta_tokens'], 'v_norm_mix_pre': out['v_norm_mix_pre'], 'v_norm_mix_post': out['v_norm_mix_post'], 'v_norm_ffn_pre': out['v_norm_ffn_pre'], 'v_norm_ffn_post': out['v_norm_ffn_post'], 'v_w_in': out['v_w_in'], 'v_q_a_norm': out['v_q_a_norm'], 'v_w_uq': out['v_w_uq'], 'v_kv_a_norm': out['v_kv_a_norm'], 'v_w_ukv': out['v_w_ukv'], 'v_attn_out_norm': out['v_attn_out_norm'], 'v_ssm_conv_w': out['v_ssm_conv_w'], 'v_ssm_conv_b': out['v_ssm_conv_b'], 'v_ssm_dt_bias': out['v_ssm_dt_bias'], 'v_ssm_A_log': out['v_ssm_A_log'], 'v_ssm_D': out['v_ssm_D'], 'v_ssm_norm': out['v_ssm_norm'], 'v_w_out': out['v_w_out'], 'v_w_up': out['v_w_up'], 'v_ffn_conv_w': out['v_ffn_conv_w'], 'v_ffn_conv_b': out['v_ffn_conv_b'], 'v_w_down': out['v_w_down']}


def _loss(weights, diff, rest, loss_target):
    with _jax.named_scope("forward"):
        args = {**rest, TWIN_DIFF_INPUT: diff, **{k: w.astype(_WEIGHT_DTYPES[k]) for k, w in weights.items()}}
        y = _forward(args)
    with _jax.named_scope("loss_head"):
        err = _jnp.square(y.astype(_jnp.float32) - loss_target)
        return 0.5 * _jnp.sum(_jnp.mean(err, axis=-1)) if err.ndim else 0.5 * err


def _adamw(w, g, m, v):
    m = ADAM_B1 * m + (1.0 - ADAM_B1) * g
    v = ADAM_B2 * v + (1.0 - ADAM_B2) * _jnp.square(g)
    m_hat = m / (1.0 - ADAM_B1 ** ADAM_STEP)
    v_hat = v / (1.0 - ADAM_B2 ** ADAM_STEP)
    delta = -ADAM_LR * (m_hat / (_jnp.sqrt(v_hat) + ADAM_EPS) + ADAM_WD * w)
    return delta, m, v


def reference(x, meta_tokens, norm_mix_pre, norm_mix_post, norm_ffn_pre, norm_ffn_post, w_in, q_a_norm, w_uq, kv_a_norm, w_ukv, attn_out_norm, ssm_conv_w, ssm_conv_b, ssm_dt_bias, ssm_A_log, ssm_D, ssm_norm, w_out, w_up, ffn_conv_w, ffn_conv_b, w_down, loss_target, m_meta_tokens, m_norm_mix_pre, m_norm_mix_post, m_norm_ffn_pre, m_norm_ffn_post, m_w_in, m_q_a_norm, m_w_uq, m_kv_a_norm, m_w_ukv, m_attn_out_norm, m_ssm_conv_w, m_ssm_conv_b, m_ssm_dt_bias, m_ssm_A_log, m_ssm_D, m_ssm_norm, m_w_out, m_w_up, m_ffn_conv_w, m_ffn_conv_b, m_w_down, v_meta_tokens, v_norm_mix_pre, v_norm_mix_post, v_norm_ffn_pre, v_norm_ffn_post, v_w_in, v_q_a_norm, v_w_uq, v_kv_a_norm, v_w_ukv, v_attn_out_norm, v_ssm_conv_w, v_ssm_conv_b, v_ssm_dt_bias, v_ssm_A_log, v_ssm_D, v_ssm_norm, v_w_out, v_w_up, v_ffn_conv_w, v_ffn_conv_b, v_w_down):
    given = dict(x=x, meta_tokens=meta_tokens, norm_mix_pre=norm_mix_pre, norm_mix_post=norm_mix_post, norm_ffn_pre=norm_ffn_pre, norm_ffn_post=norm_ffn_post, w_in=w_in, q_a_norm=q_a_norm, w_uq=w_uq, kv_a_norm=kv_a_norm, w_ukv=w_ukv, attn_out_norm=attn_out_norm, ssm_conv_w=ssm_conv_w, ssm_conv_b=ssm_conv_b, ssm_dt_bias=ssm_dt_bias, ssm_A_log=ssm_A_log, ssm_D=ssm_D, ssm_norm=ssm_norm, w_out=w_out, w_up=w_up, ffn_conv_w=ffn_conv_w, ffn_conv_b=ffn_conv_b, w_down=w_down, loss_target=loss_target, m_meta_tokens=m_meta_tokens, m_norm_mix_pre=m_norm_mix_pre, m_norm_mix_post=m_norm_mix_post, m_norm_ffn_pre=m_norm_ffn_pre, m_norm_ffn_post=m_norm_ffn_post, m_w_in=m_w_in, m_q_a_norm=m_q_a_norm, m_w_uq=m_w_uq, m_kv_a_norm=m_kv_a_norm, m_w_ukv=m_w_ukv, m_attn_out_norm=m_attn_out_norm, m_ssm_conv_w=m_ssm_conv_w, m_ssm_conv_b=m_ssm_conv_b, m_ssm_dt_bias=m_ssm_dt_bias, m_ssm_A_log=m_ssm_A_log, m_ssm_D=m_ssm_D, m_ssm_norm=m_ssm_norm, m_w_out=m_w_out, m_w_up=m_w_up, m_ffn_conv_w=m_ffn_conv_w, m_ffn_conv_b=m_ffn_conv_b, m_w_down=m_w_down, v_meta_tokens=v_meta_tokens, v_norm_mix_pre=v_norm_mix_pre, v_norm_mix_post=v_norm_mix_post, v_norm_ffn_pre=v_norm_ffn_pre, v_norm_ffn_post=v_norm_ffn_post, v_w_in=v_w_in, v_q_a_norm=v_q_a_norm, v_w_uq=v_w_uq, v_kv_a_norm=v_kv_a_norm, v_w_ukv=v_w_ukv, v_attn_out_norm=v_attn_out_norm, v_ssm_conv_w=v_ssm_conv_w, v_ssm_conv_b=v_ssm_conv_b, v_ssm_dt_bias=v_ssm_dt_bias, v_ssm_A_log=v_ssm_A_log, v_ssm_D=v_ssm_D, v_ssm_norm=v_ssm_norm, v_w_out=v_w_out, v_w_up=v_w_up, v_ffn_conv_w=v_ffn_conv_w, v_ffn_conv_b=v_ffn_conv_b, v_w_down=v_w_down)
    weights = {n: given[n] for n in TWIN_WEIGHTS}
    shared = {n: given[n] for n in SHARED_INPUTS}
    per_example = {n: given[n] for n in ['x']}
    grad_fn = _jax.value_and_grad(_loss, argnums=(0, 1))

    def one_microbatch(ex, loss_target):
        ex = dict(ex)
        diff = ex.pop(TWIN_DIFF_INPUT)
        return grad_fn(weights, diff, {**shared, **ex}, loss_target)

    if N_MICROBATCH == 1:
        loss, (grad_w, grad_x) = one_microbatch(per_example, given["loss_target"])
    else:
        def body(carry, xs):
            loss_sum, grad_sum = carry
            l_k, (gw_k, gx_k) = one_microbatch(xs[0], xs[1])
            with _jax.named_scope("update"):
                return (loss_sum + l_k, _jax.tree.map(_jnp.add, grad_sum, gw_k)), gx_k

        init = (_jnp.zeros((), _jnp.float32), _jax.tree.map(_jnp.zeros_like, weights))
        (loss, grad_w), grad_x = _jax.lax.scan(body, init, (per_example, given["loss_target"]))
    with _jax.named_scope("update"):
        delta_w, new_m, new_v = {}, {}, {}
        for n in TWIN_WEIGHTS:
            delta_w[n], new_m[n], new_v[n] = _adamw(weights[n], grad_w[n], given["m_" + n], given["v_" + n])
    return (loss, grad_x, *[grad_w[n] for n in TWIN_WEIGHTS], *[delta_w[n] for n in TWIN_WEIGHTS],
            *[new_m[n] for n in TWIN_WEIGHTS], *[new_v[n] for n in TWIN_WEIGHTS])
```

```python
import functools

import jax
import jax.numpy as jnp
from jax import lax
from jax.experimental import pallas as pl
from jax.experimental.pallas import tpu as pltpu

F32 = jnp.float32
BF16 = jnp.bfloat16
HI = lax.Precision.HIGHEST

D = 1024
N_META = 16
FRONT = 128
PAD_ROWS = FRONT - N_META
MLA_H = 8
DN, DR, DV = 128, 64, 128
QR, KVR = 384, 256
SOFTMAX_SCALE = (DN + DR) ** -0.5
ROPE_THETA = 10000.0
SSM_H, SSM_P, SSM_G, SSM_N, SSM_K = 16, 64, 2, 128, 4
CHUNK = 128
D_SSM = SSM_H * SSM_P
D_XBC = D_SSM + 2 * SSM_G * SSM_N
GSZ = D_SSM // SSM_G
D_FF = 2816
FFN_K = 3
EPS = 1e-6
IN_SPLITS = (QR, KVR, DR, D_SSM, D_XBC, SSM_H)
D_IN = sum(IN_SPLITS)
LAT_W = 768
IN_P = LAT_W + D_SSM + D_XBC + 128
NEG = -1e30

ADAM_LR, ADAM_B1, ADAM_B2, ADAM_EPS, ADAM_WD, ADAM_STEP = 0.001, 0.9, 0.999, 1e-08, 0.01, 10

VMEM_LIMIT = 56 * 1024 * 1024
MESH = pl.DeviceIdType.MESH


def _sds(shape, dtype):
    return jax.ShapeDtypeStruct(shape, dtype)


def _cp(*sem):
    return pltpu.CompilerParams(dimension_semantics=sem, vmem_limit_bytes=VMEM_LIMIT)


def _rt(n, cands):
    for c in cands:
        if n % c == 0:
            return c
    raise ValueError((n, cands))


def _full(shape):
    nd = len(shape)
    return pl.BlockSpec(shape, lambda *_: (0,) * nd)


def _rows(tr, c):
    return pl.BlockSpec((tr, c), lambda i: (i, 0))


def _sigmoid(x):
    return 1.0 / (1.0 + jnp.exp(-x))


def _silu(x):
    return x * _sigmoid(x)


def _dsilu(x):
    s = _sigmoid(x)
    return s * (1.0 + x * (1.0 - s))


def _softplus(x):
    return jnp.maximum(x, 0.0) + jnp.log(1.0 + jnp.exp(-jnp.abs(x)))


def _rms(x, g):
    r = lax.rsqrt(jnp.mean(x * x, axis=-1, keepdims=True) + EPS)
    return x * r * g


def _rms_bwd(x, g, dy):
    r = lax.rsqrt(jnp.mean(x * x, axis=-1, keepdims=True) + EPS)
    xh = x * r
    dxh = dy * g
    dx = r * (dxh - xh * jnp.mean(dxh * xh, axis=-1, keepdims=True))
    return dx, jnp.sum(dy * xh, axis=0, keepdims=True)


def _dot(a, b):
    return jnp.dot(a, b, preferred_element_type=F32)


def _dot_nt(a, b):
    return lax.dot_general(a, b, (((1,), (1,)), ((), ())), preferred_element_type=F32)


def _dot_tn(a, b):
    return lax.dot_general(a, b, (((0,), (0,)), ((), ())), preferred_element_type=F32)


def _dot_hi(a, b):
    return jnp.dot(a, b, precision=HI, preferred_element_type=F32)


def _dot_nt_hi(a, b):
    return lax.dot_general(a, b, (((1,), (1,)), ((), ())), precision=HI, preferred_element_type=F32)


def _shift_down(x, halo, j):
    xr = pltpu.roll(x, j, axis=0)
    hr = pltpu.roll(halo, j, axis=0)
    row = lax.broadcasted_iota(jnp.int32, (8, x.shape[1]), 0)
    first = jnp.where(row < j, hr, xr[:8])
    return jnp.concatenate([first, xr[8:]], axis=0)


def _shift_up(x, nxt, j):
    t = x.shape[0]
    xr = pltpu.roll(x, t - j, axis=0)
    nr = pltpu.roll(nxt, 8 - j, axis=0)
    row = lax.broadcasted_iota(jnp.int32, (8, x.shape[1]), 0)
    last = jnp.where(row + j >= 8, nr, xr[t - 8:])
    return jnp.concatenate([xr[:t - 8], last], axis=0)


def _acc_rows(ref, val, first):
    @pl.when(first)
    def _():
        ref[...] = val

    @pl.when(jnp.logical_not(first))
    def _():
        ref[...] += val


def _mm_nt(a, b, name, out_dtype=F32, tms=(640, 320, 128)):
    m, k = a.shape
    n = b.shape[0]
    tm = _rt(m, tms)

    def body(a_ref, b_ref, o_ref):
        o_ref[...] = _dot_nt(a_ref[...].astype(BF16), b_ref[...]).astype(out_dtype)

    return pl.pallas_call(
        body, name=name, grid=(m // tm,), in_specs=[_rows(tm, k), _full(b.shape)], out_specs=_rows(tm, n),
        out_shape=_sds((m, n), out_dtype), compiler_params=_cp("parallel"))(a, b)


def _mm_tn(a, b, name, tn=None, trs=(640, 128)):
    r, m = a.shape
    n = b.shape[1]
    tn = n if tn is None else tn
    tr = _rt(r, trs)

    def body(a_ref, b_ref, o_ref):
        part = _dot_tn(a_ref[...].astype(BF16), b_ref[...].astype(BF16))
        _acc_rows(o_ref, part, pl.program_id(1) == 0)

    return pl.pallas_call(
        body, name=name, grid=(n // tn, r // tr),
        in_specs=[pl.BlockSpec((tr, m), lambda j, i: (i, 0)), pl.BlockSpec((tr, tn), lambda j, i: (i, j))],
        out_specs=pl.BlockSpec((m, tn), lambda j, i: (0, j)),
        out_shape=_sds((m, n), F32), compiler_params=_cp("parallel", "arbitrary"))(a, b)


def _inproj(h0, g, w):
    lp = h0.shape[0]
    tr = _rt(lp, (320, 128))
    segs = ((0, LAT_W), (LAT_W, LAT_W + D_SSM), (LAT_W + D_SSM, LAT_W + D_SSM + D_XBC), (IN_P - 128, IN_P))

    def body(h_ref, g_ref, w_ref, hn_ref, lat_ref, z_ref, xbc_ref, dt_ref):
        hn = _rms(h_ref[...], g_ref[...]).astype(BF16)
        hn_ref[...] = hn
        for ref, (a, b) in zip((lat_ref, z_ref, xbc_ref, dt_ref), segs):
            ref[...] = _dot(hn, w_ref[:, a:b])

    return pl.pallas_call(
        body, name="inproj", grid=(lp // tr,), in_specs=[_rows(tr, D), _full((1, D)), _full(w.shape)],
        out_specs=[_rows(tr, D), _rows(tr, LAT_W), _rows(tr, D_SSM), _rows(tr, D_XBC), _rows(tr, 128)],
        out_shape=[_sds((lp, D), BF16), _sds((lp, LAT_W), F32), _sds((lp, D_SSM), F32), _sds((lp, D_XBC), F32),
                   _sds((lp, 128), F32)],
        compiler_params=_cp("parallel"))(h0, g, w)


def _rope(x, cos, sa, sb):
    return x * cos + pltpu.roll(x, 96, axis=1) * sa + pltpu.roll(x, 32, axis=1) * sb


def _rope_t(g, cos, sa, sb):
    return g * cos + pltpu.roll(g * sa, 32, axis=1) + pltpu.roll(g * sb, 96, axis=1)


def _mla_prep(lat, qg, kvg, wq, wkv, cos, sa, sb):
    lp = lat.shape[0]
    tr = _rt(lp, (320, 128))

    def body(lat_ref, qg_ref, kvg_ref, wq_ref, wkv_ref, cos_ref, sa_ref, sb_ref, q_ref, k_ref, v_ref, ql_ref, kl_ref):
        lat_v = lat_ref[...]
        ql = _rms(lat_v[:, :QR], qg_ref[...]).astype(BF16)
        kl = _rms(lat_v[:, QR:QR + KVR], kvg_ref[...]).astype(BF16)
        ql_ref[...] = ql
        kl_ref[...] = kl
        cos_v, sa_v, sb_v = cos_ref[...], sa_ref[...], sb_ref[...]
        kpe = _rope(lat_v[:, QR + KVR:LAT_W], cos_v, sa_v, sb_v).astype(BF16)
        for h in range(MLA_H):
            q_ref[h, :, 0:DN] = _dot(ql, wq_ref[:, h * DN:(h + 1) * DN]).astype(BF16)
            qpe = _dot(ql, wq_ref[:, D + h * 128:D + (h + 1) * 128])
            q_ref[h, :, DN:2 * DN] = _rope(qpe, cos_v, sa_v, sb_v).astype(BF16)
            k_ref[h, :, 0:DN] = _dot(kl, wkv_ref[:, h * DN:(h + 1) * DN]).astype(BF16)
            k_ref[h, :, DN:2 * DN] = kpe
            v_ref[h] = _dot(kl, wkv_ref[:, D + h * DV:D + (h + 1) * DV]).astype(BF16)

    hb = lambda w: pl.BlockSpec((MLA_H, tr, w), lambda i: (0, i, 0))
    return pl.pallas_call(
        body, name="mla_prep", grid=(lp // tr,),
        in_specs=[_rows(tr, LAT_W), _full((1, QR)), _full((1, KVR)), _full(wq.shape), _full(wkv.shape),
                  _rows(tr, 128), _rows(tr, 128), _rows(tr, 128)],
        out_specs=[hb(256), hb(256), hb(128), _rows(tr, QR), _rows(tr, KVR)],
        out_shape=[_sds((MLA_H, lp, 256), BF16), _sds((MLA_H, lp, 256), BF16), _sds((MLA_H, lp, 128), BF16),
                   _sds((lp, QR), BF16), _sds((lp, KVR), BF16)],
        compiler_params=_cp("parallel"))(lat, qg, kvg, wq, wkv, cos, sa, sb)


def _attn_mask(r0, c0, tq, tk, transposed=False):
    if transposed:
        kk = c0 + lax.broadcasted_iota(jnp.int32, (tk, tq), 0)
        qq = r0 + lax.broadcasted_iota(jnp.int32, (tk, tq), 1)
    else:
        qq = r0 + lax.broadcasted_iota(jnp.int32, (tq, tk), 0)
        kk = c0 + lax.broadcasted_iota(jnp.int32, (tq, tk), 1)
    return jnp.logical_and(kk <= qq, kk >= PAD_ROWS)


def _attn_fwd(q, k, v):
    lp = q.shape[1]
    t = _rt(lp, (640, 128))
    nq = lp // t

    def body(q_ref, k_ref, v_ref, o_ref, lse_ref):
        qi = pl.program_id(1)
        qv = q_ref[0]

        def step(kj, carry):
            m, l, acc = carry
            kv_rows = pl.ds(pl.multiple_of(kj * t, t), t)
            kk = k_ref[0, kv_rows, :]
            vv = v_ref[0, kv_rows, :]
            s = _dot_nt(qv, kk) * SOFTMAX_SCALE
            s = jnp.where(_attn_mask(qi * t, kj * t, t, t), s, NEG)
            m_new = jnp.maximum(m, jnp.max(s, axis=-1, keepdims=True))
            alpha = jnp.exp(m - m_new)
            p = jnp.exp(s - m_new)
            l = alpha * l + jnp.sum(p, axis=-1, keepdims=True)
            acc = alpha * acc + _dot(p.astype(BF16), vv)
            return m_new, l, acc

        init = (jnp.full((t, 1), NEG, F32), jnp.zeros((t, 1), F32), jnp.zeros((t, DV), F32))
        m, l, acc = lax.fori_loop(0, qi + 1, step, init)
        o_ref[...] = acc / l
        lse_ref[0] = jnp.broadcast_to(m + jnp.log(l), (t, 128))

    return pl.pallas_call(
        body, name="attn_fwd", grid=(MLA_H, nq),
        in_specs=[pl.BlockSpec((1, t, 256), lambda h, i: (h, i, 0)), pl.BlockSpec((1, lp, 256), lambda h, i: (h, 0, 0)),
                  pl.BlockSpec((1, lp, 128), lambda h, i: (h, 0, 0))],
        out_specs=[pl.BlockSpec((t, DV), lambda h, i: (i, h)), pl.BlockSpec((1, t, 128), lambda h, i: (h, i, 0))],
        out_shape=[_sds((lp, MLA_H * DV), F32), _sds((MLA_H, lp, 128), F32)],
        compiler_params=_cp("parallel", "parallel"))(q, k, v)


def _attn_bwd_dq(q, k, v, o, do, lse):
    lp = q.shape[1]
    t = _rt(lp, (640, 128))
    nq = lp // t

    def body(q_ref, k_ref, v_ref, o_ref, do_ref, lse_ref, dq_ref, dl_ref):
        qi = pl.program_id(1)
        qv = q_ref[0]
        dov = do_ref[...]
        delta = jnp.sum(dov * o_ref[...], axis=-1, keepdims=True)
        dl_ref[0] = jnp.broadcast_to(delta, (t, 128))
        lse_v = lse_ref[0][:, :1]
        dob = dov.astype(BF16)

        def step(kj, dq):
            kv_rows = pl.ds(pl.multiple_of(kj * t, t), t)
            kk = k_ref[0, kv_rows, :]
            vv = v_ref[0, kv_rows, :]
            s = _dot_nt(qv, kk) * SOFTMAX_SCALE
            s = jnp.where(_attn_mask(qi * t, kj * t, t, t), s, NEG)
            p = jnp.exp(s - lse_v)
            dp = _dot_nt(dob, vv)
            ds = p * (dp - delta) * SOFTMAX_SCALE
            return dq + _dot(ds.astype(BF16), kk)

        dq_ref[0] = lax.fori_loop(0, qi + 1, step, jnp.zeros((t, 256), F32))

    return pl.pallas_call(
        body, name="attn_bwd_dq", grid=(MLA_H, nq),
        in_specs=[pl.BlockSpec((1, t, 256), lambda h, i: (h, i, 0)), pl.BlockSpec((1, lp, 256), lambda h, i: (h, 0, 0)),
                  pl.BlockSpec((1, lp, 128), lambda h, i: (h, 0, 0)), pl.BlockSpec((t, DV), lambda h, i: (i, h)),
                  pl.BlockSpec((t, DV), lambda h, i: (i, h)), pl.BlockSpec((1, t, 128), lambda h, i: (h, i, 0))],
        out_specs=[pl.BlockSpec((1, t, 256), lambda h, i: (h, i, 0)), pl.BlockSpec((1, t, 128), lambda h, i: (h, i, 0))],
        out_shape=[_sds((MLA_H, lp, 256), F32), _sds((MLA_H, lp, 128), F32)],
        compiler_params=_cp("parallel", "parallel"))(q, k, v, o, do, lse)


def _attn_bwd_dkv(q, k, v, do, lse_row, delta_row):
    lp = q.shape[1]
    t = _rt(lp, (640, 128))
    nq = lp // t

    def body(q_ref, k_ref, v_ref, do_ref, lse_ref, dl_ref, dk_ref, dv_ref):
        kj = pl.program_id(1)
        kk = k_ref[0]
        vv = v_ref[0]

        def step(qi, carry):
            dk, dv = carry
            q_rows = pl.ds(pl.multiple_of(qi * t, t), t)
            qv = q_ref[0, q_rows, :]
            dob = do_ref[q_rows, :].astype(BF16)
            st = _dot_nt(kk, qv) * SOFTMAX_SCALE
            st = jnp.where(_attn_mask(qi * t, kj * t, t, t, transposed=True), st, NEG)
            pt = jnp.exp(st - lse_ref[0, qi])
            dpt = _dot_nt(vv, dob)
            dst = pt * (dpt - dl_ref[0, qi]) * SOFTMAX_SCALE
            dv = dv + _dot(pt.astype(BF16), dob)
            dk = dk + _dot(dst.astype(BF16), qv)
            return dk, dv

        dk, dv = lax.fori_loop(kj, nq, step, (jnp.zeros((t, 256), F32), jnp.zeros((t, DV), F32)))
        dk_ref[0] = dk
        dv_ref[0] = dv

    stat = pl.BlockSpec((1, nq, 1, t), lambda h, j: (h, 0, 0, 0))
    return pl.pallas_call(
        body, name="attn_bwd_dkv", grid=(MLA_H, nq),
        in_specs=[pl.BlockSpec((1, lp, 256), lambda h, j: (h, 0, 0)), pl.BlockSpec((1, t, 256), lambda h, j: (h, j, 0)),
                  pl.BlockSpec((1, t, 128), lambda h, j: (h, j, 0)), pl.BlockSpec((lp, DV), lambda h, j: (0, h)), stat, stat],
        out_specs=[pl.BlockSpec((1, t, 256), lambda h, j: (h, j, 0)), pl.BlockSpec((1, t, 128), lambda h, j: (h, j, 0))],
        out_shape=[_sds((MLA_H, lp, 256), F32), _sds((MLA_H, lp, 128), F32)],
        compiler_params=_cp("parallel", "parallel"))(q, k, v, do, lse_row, delta_row)


def _ssd_consts():
    ri = lax.broadcasted_iota(jnp.int32, (CHUNK, CHUNK), 0)
    ci = lax.broadcasted_iota(jnp.int32, (CHUNK, CHUNK), 1)
    expand = (lax.broadcasted_iota(jnp.int32, (128, D_SSM), 0)
              == lax.broadcasted_iota(jnp.int32, (128, D_SSM), 1) // SSM_P).astype(F32)
    return ri, ci, expand


def _ssd_chunk(c, x_ref, xh_ref, dt_ref, dtT_ref, cw_ref, cb_ref, dtb_ref, dtbT_ref, al_ref, alT_ref):
    ri, ci, expand = _ssd_consts()
    x = x_ref[...]
    halo = jnp.where(c > 0, xh_ref[...], 0.0)
    sh = [x] + [_shift_down(x, halo, j) for j in range(1, SSM_K)]
    cv = cb_ref[...]
    for kk in range(SSM_K):
        cv = cv + cw_ref[kk:kk + 1, :] * sh[SSM_K - 1 - kk]
    xa = _silu(cv)
    grow = c * CHUNK + ri
    gcol = c * CHUNK + lax.broadcasted_iota(jnp.int32, (SSM_H, CHUNK), 1)
    sp = dt_ref[...] + dtb_ref[...]
    spT = dtT_ref[...] + dtbT_ref[...]
    dtc = jnp.where(grow >= PAD_ROWS, _softplus(sp), 0.0)
    dtr = jnp.where(gcol >= PAD_ROWS, _softplus(spT), 0.0)
    arow = -jnp.exp(al_ref[...])
    acolT = -jnp.exp(alT_ref[...])
    ltri = (ci <= ri).astype(F32)
    acs = _dot_hi(ltri, dtc * arow)
    acsT = _dot_hi(dtr * acolT, (ri <= ci).astype(F32))
    return dict(x=x, sh=sh, cv=cv, xa=xa, sp=sp, dtc=dtc, arow=arow, acs=acs, acsT=acsT, ri=ri, ci=ci, expand=expand,
                grow=grow)


def _ssd_mats(k, s_prev):
    xa, acs, acsT, expand, ri, ci = k["xa"], k["acs"], k["acsT"], k["expand"], k["ri"], k["ci"]
    xs = xa[:, :D_SSM]
    dt_e = _dot_hi(k["dtc"], expand)
    acs_e = _dot_hi(acs, expand)
    last_e = acs_e[CHUNK - 1:CHUNK, :]
    ea = jnp.exp(acs_e)
    f = jnp.exp(last_e - acs_e)
    cd = jnp.exp(last_e)
    xdt = xs * dt_e
    bm = [xa[:, D_SSM + g * SSM_N:D_SSM + (g + 1) * SSM_N] for g in range(SSM_G)]
    cm = [xa[:, D_SSM + (SSM_G + g) * SSM_N:D_SSM + (SSM_G + g + 1) * SSM_N] for g in range(SSM_G)]
    bmb = [b.astype(BF16) for b in bm]
    cmb = [cc.astype(BF16) for cc in cm]
    cb = [_dot_nt(cmb[g], bmb[g]) for g in range(SSM_G)]
    lam, mm = [], []
    for h in range(SSM_H):
        diff = acs[:, h:h + 1] - acsT[h:h + 1, :]
        lam_h = jnp.exp(jnp.where(ci <= ri, diff, NEG))
        lam.append(lam_h)
        mm.append(cb[h // (SSM_H // SSM_G)] * lam_h)
    lo = lax.broadcasted_iota(jnp.int32, (CHUNK, 128), 1) < SSM_P
    xdt_h = []
    for h in range(SSM_H):
        pair = xdt[:, (h // 2) * 128:(h // 2 + 1) * 128]
        xdt_h.append(jnp.where(lo if h % 2 == 0 else jnp.logical_not(lo), pair, 0.0).astype(BF16))
    ydiag = jnp.concatenate(
        [_dot(mm[2 * j].astype(BF16), xdt_h[2 * j]) + _dot(mm[2 * j + 1].astype(BF16), xdt_h[2 * j + 1])
         for j in range(SSM_H // 2)], axis=1)
    t_off = [_dot(cmb[g], s_prev[g].astype(BF16)) for g in range(SSM_G)]
    yoff = jnp.concatenate(t_off, axis=1) * ea
    return dict(xs=xs, dt_e=dt_e, acs_e=acs_e, ea=ea, f=f, cd=cd, xdt=xdt, bm=bm, cm=cm, bmb=bmb, cmb=cmb, cb=cb, lam=lam,
                mm=mm, lo=lo, xdt_h=xdt_h, ydiag=ydiag, t_off=t_off, yoff=yoff)


def _ssd_specs(nc, rev):
    ix = (lambda i: nc - 1 - i) if rev else (lambda i: i)
    return [
        pl.BlockSpec((CHUNK, D_XBC), lambda i: (ix(i), 0)),
        pl.BlockSpec((8, D_XBC), lambda i: (jnp.maximum(ix(i) * (CHUNK // 8) - 1, 0), 0)),
        pl.BlockSpec((CHUNK, D_SSM), lambda i: (ix(i), 0)),
        pl.BlockSpec((CHUNK, 128), lambda i: (ix(i), 0)),
        pl.BlockSpec((SSM_H, CHUNK), lambda i: (0, ix(i))),
        _full((8, D_XBC)), _full((1, D_XBC)), _full((1, 128)), _full((SSM_H, 1)), _full((1, 128)), _full((SSM_H, 1)),
        _full((1, D_SSM)), _full((1, D_SSM)),
    ]


def _ssd_fwd(xbc, z, dtr, dtrT, cw, cb, dtb, dtbT, alog, alogT, d_e, ng):
    lp = xbc.shape[0]
    nc = lp // CHUNK

    def body(x_ref, xh_ref, z_ref, dt_ref, dtT_ref, cw_ref, cb_ref, dtb_ref, dtbT_ref, al_ref, alT_ref, de_ref, ng_ref,
             y_ref, st_ref, s_scr):
        c = pl.program_id(0)

        @pl.when(c == 0)
        def _():
            s_scr[...] = jnp.zeros_like(s_scr)

        k = _ssd_chunk(c, x_ref, xh_ref, dt_ref, dtT_ref, cw_ref, cb_ref, dtb_ref, dtbT_ref, al_ref, alT_ref)
        s_prev = [s_scr[g] for g in range(SSM_G)]
        st_ref[0] = s_scr[...]
        m = _ssd_mats(k, s_prev)
        xd = (m["xdt"] * m["f"]).astype(BF16)
        for g in range(SSM_G):
            sl = slice(g * GSZ, (g + 1) * GSZ)
            s_scr[g] = m["cd"][:, sl] * s_prev[g] + _dot(m["bm"][g].T.astype(BF16), xd[:, sl])
        y = m["ydiag"] + m["yoff"] + de_ref[...] * m["xs"]
        u = y * _silu(z_ref[...])
        outs = []
        for g in range(SSM_G):
            ug = u[:, g * GSZ:(g + 1) * GSZ]
            outs.append(ug * lax.rsqrt(jnp.mean(ug * ug, axis=-1, keepdims=True) + EPS))
        y_ref[...] = jnp.concatenate(outs, axis=1) * ng_ref[...]

    return pl.pallas_call(
        body, name="ssd_fwd", grid=(nc,), in_specs=_ssd_specs(nc, False),
        out_specs=[_rows(CHUNK, D_SSM), pl.BlockSpec((1, SSM_G, SSM_N, GSZ), lambda i: (i, 0, 0, 0))],
        out_shape=[_sds((lp, D_SSM), F32), _sds((nc, SSM_G, SSM_N, GSZ), F32)],
        scratch_shapes=[pltpu.VMEM((SSM_G, SSM_N, GSZ), F32)],
        compiler_params=_cp("arbitrary"))(xbc, xbc, z, dtr, dtrT, cw, cb, dtb, dtbT, alog, alogT, d_e, ng)


def _ssd_bwd(dmixin, xbc, z, dtr, dtrT, st, cw, cb, dtb, dtbT, alog, alogT, d_e, ng):
    lp = xbc.shape[0]
    nc = lp // CHUNK
    hpg = SSM_H // SSM_G

    def body(dy_ref, x_ref, xh_ref, z_ref, dt_ref, dtT_ref, st_ref, cw_ref, cb_ref, dtb_ref, dtbT_ref, al_ref, alT_ref,
             de_ref, ng_ref, dz_ref, dx_ref, ddt_ref, dcw_ref, dcb_ref, ddtb_ref, dal_ref, dd_ref, dng_ref, ds_scr, nx_scr):
        i = pl.program_id(0)
        c = nc - 1 - i
        first = i == 0

        @pl.when(first)
        def _():
            ds_scr[...] = jnp.zeros_like(ds_scr)
            nx_scr[...] = jnp.zeros_like(nx_scr)

        k = _ssd_chunk(c, x_ref, xh_ref, dt_ref, dtT_ref, cw_ref, cb_ref, dtb_ref, dtbT_ref, al_ref, alT_ref)
        s_prev = [st_ref[0, g] for g in range(SSM_G)]
        m = _ssd_mats(k, s_prev)
        ri, ci, expand = k["ri"], k["ci"], k["expand"]
        xs, acs, acsT = m["xs"], k["acs"], k["acsT"]
        zv = z_ref[...]
        dout = dy_ref[...]
        ngv = ng_ref[...]
        y = m["ydiag"] + m["yoff"] + de_ref[...] * xs
        sz = _silu(zv)
        u = y * sz
        du_parts, dng_parts = [], []
        for g in range(SSM_G):
            sl = slice(g * GSZ, (g + 1) * GSZ)
            dug, dngg = _rms_bwd(u[:, sl], ngv[:, sl], dout[:, sl])
            du_parts.append(dug)
            dng_parts.append(dngg)
        du = jnp.concatenate(du_parts, axis=1)
        _acc_rows(dng_ref, jnp.concatenate(dng_parts, axis=1), first)
        dy = du * sz
        dz_ref[...] = du * y * _dsilu(zv)
        dd_e = jnp.sum(dy * xs, axis=0, keepdims=True)
        _acc_rows(dd_ref, _dot_nt_hi(dd_e, expand), first)
        dxs = de_ref[...] * dy
        dacs_e = dy * m["yoff"]
        dtg = (dy * m["ea"]).astype(BF16)
        dxdt = jnp.zeros_like(xs)
        dlast_e = []
        db, dc, ds_prev = [], [], []
        xd = m["xdt"] * m["f"]
        dxd_all = []
        for g in range(SSM_G):
            sl = slice(g * GSZ, (g + 1) * GSZ)
            dsg = ds_scr[g]
            spb = s_prev[g].astype(BF16)
            dc.append(_dot_nt(dtg[:, sl], spb))
            dsp = _dot(m["cm"][g].T.astype(BF16), dtg[:, sl]) + m["cd"][:, sl] * dsg
            ds_prev.append(dsp)
            dlast_e.append(jnp.sum(dsg * s_prev[g], axis=0, keepdims=True) * m["cd"][:, sl])
            dsb = dsg.astype(BF16)
            db.append(_dot_nt(xd[:, sl].astype(BF16), dsb))
            dxd_all.append(_dot(m["bmb"][g], dsb))
        dxd = jnp.concatenate(dxd_all, axis=1)
        dxdt = dxd * m["f"]
        dff = dxd * xd
        dacs_e = dacs_e - dff
        dlast_row = jnp.concatenate(dlast_e, axis=1) + jnp.sum(dff, axis=0, keepdims=True)
        dacs = jnp.zeros((CHUNK, 128), F32)
        lane = lax.broadcasted_iota(jnp.int32, (1, 128), 1)
        cbT = [_dot_nt(m["bmb"][g], m["cmb"][g]) for g in range(SSM_G)]
        dgs = [jnp.zeros((CHUNK, CHUNK), F32) for _ in range(SSM_G)]
        dgTs = [jnp.zeros((CHUNK, CHUNK), F32) for _ in range(SSM_G)]
        dxdt_pairs = []
        for h in range(SSM_H):
            g = h // hpg
            pr = slice((h // 2) * 128, (h // 2 + 1) * 128)
            lo_h = m["lo"] if h % 2 == 0 else jnp.logical_not(m["lo"])
            dyp = jnp.where(lo_h, dy[:, pr], 0.0).astype(BF16)
            xdp = m["xdt"][:, pr].astype(BF16)
            dm = _dot_nt(dyp, xdp)
            dmT = _dot_nt(xdp, dyp)
            lamT = jnp.exp(jnp.where(ri <= ci, acsT[h:h + 1, :] - acs[:, h:h + 1], NEG))
            mT = cbT[g] * lamT
            dgs[g] = dgs[g] + dm * m["lam"][h]
            dgTs[g] = dgTs[g] + dmT * lamT
            v1 = jnp.sum(dm * m["mm"][h], axis=1, keepdims=True)
            v2 = jnp.sum(dmT * mT, axis=1, keepdims=True)
            dacs = dacs + (v1 - v2) * (lane == h).astype(F32)
            part = _dot(mT.astype(BF16), dyp)
            if h % 2 == 0:
                dxdt_pairs.append(part)
            else:
                dxdt_pairs[-1] = dxdt_pairs[-1] + part
        dxdt = dxdt + jnp.concatenate(dxdt_pairs, axis=1)
        for g in range(SSM_G):
            dc[g] = dc[g] + _dot(dgs[g].astype(BF16), m["bmb"][g])
            db[g] = db[g] + _dot(dgTs[g].astype(BF16), m["cmb"][g])
        dacs = dacs + _dot_nt_hi(dacs_e, expand)
        dlast = _dot_nt_hi(dlast_row, expand)
        dacs = dacs + jnp.where(ri == CHUNK - 1, dlast, 0.0)
        dxs = dxs + dxdt * m["dt_e"]
        ddt = _dot_nt_hi(dxdt * xs, expand)
        da = _dot_hi((ri <= ci).astype(F32), dacs)
        ddt = ddt + da * k["arow"]
        dA = jnp.sum(da * k["dtc"], axis=0, keepdims=True)
        _acc_rows(dal_ref, dA * k["arow"], first)
        ddtr = jnp.where(k["grow"] >= PAD_ROWS, ddt * _sigmoid(k["sp"]), 0.0)
        ddt_ref[...] = ddtr
        _acc_rows(ddtb_ref, jnp.sum(ddtr, axis=0, keepdims=True), first)
        for g in range(SSM_G):
            ds_scr[g] = ds_prev[g]
        dxa = jnp.concatenate([dxs] + db + dc, axis=1)
        dcv = dxa * _dsilu(k["cv"])
        _acc_rows(dcb_ref, jnp.sum(dcv, axis=0, keepdims=True), first)
        dcw_rows = [jnp.sum(dcv * k["sh"][SSM_K - 1 - kk], axis=0, keepdims=True) for kk in range(SSM_K)]
        dcw_rows.append(jnp.zeros((8 - SSM_K, D_XBC), F32))
        _acc_rows(dcw_ref, jnp.concatenate(dcw_rows, axis=0), first)
        nxt = nx_scr[...]
        dx = cw_ref[SSM_K - 1:SSM_K, :] * dcv
        for j in range(1, SSM_K):
            dx = dx + cw_ref[SSM_K - 1 - j:SSM_K - j, :] * _shift_up(dcv, nxt, j)
        grow_x = c * CHUNK + lax.broadcasted_iota(jnp.int32, (CHUNK, D_XBC), 0)
        dx_ref[...] = jnp.where(grow_x >= PAD_ROWS, dx, 0.0)
        nx_scr[...] = dcv[:8]

    specs = _ssd_specs(nc, True)
    in_specs = [pl.BlockSpec((CHUNK, D_SSM), lambda i: (nc - 1 - i, 1))] + specs[:5] + [
        pl.BlockSpec((1, SSM_G, SSM_N, GSZ), lambda i: (nc - 1 - i, 0, 0, 0))] + specs[5:]
    rv = lambda w: pl.BlockSpec((CHUNK, w), lambda i: (nc - 1 - i, 0))
    return pl.pallas_call(
        body, name="ssd_bwd", grid=(nc,), in_specs=in_specs,
        out_specs=[rv(D_SSM), rv(D_XBC), rv(128), _full((8, D_XBC)), _full((1, D_XBC)), _full((1, 128)), _full((1, 128)),
                   _full((1, 128)), _full((1, D_SSM))],
        out_shape=[_sds((lp, D_SSM), F32), _sds((lp, D_XBC), F32), _sds((lp, 128), F32), _sds((8, D_XBC), F32),
                   _sds((1, D_XBC), F32), _sds((1, 128), F32), _sds((1, 128), F32), _sds((1, 128), F32), _sds((1, D_SSM), F32)],
        scratch_shapes=[pltpu.VMEM((SSM_G, SSM_N, GSZ), F32), pltpu.VMEM((8, D_XBC), F32)],
        compiler_params=_cp("arbitrary"))(dmixin, xbc, xbc, z, dtr, dtrT, st, cw, cb, dtb, dtbT, alog, alogT, d_e, ng)


def _mixout_fwd(o, ssm, h0, g_ao, g_post, w):
    lp = o.shape[0]
    tr = _rt(lp, (320, 128))

    def body(o_ref, s_ref, h_ref, ga_ref, gp_ref, w_ref, mi_ref, mix_ref, h1_ref):
        mixin = jnp.concatenate([_rms(o_ref[...], ga_ref[...]), s_ref[...]], axis=1).astype(BF16)
        mi_ref[...] = mixin
        mix = _dot(mixin, w_ref[...])
        mix_ref[...] = mix
        grow = pl.program_id(0) * tr + lax.broadcasted_iota(jnp.int32, (tr, D), 0)
        h1_ref[...] = h_ref[...] + jnp.where(grow >= PAD_ROWS, _rms(mix, gp_ref[...]), 0.0)

    return pl.pallas_call(
        body, name="mixout_fwd", grid=(lp // tr,),
        in_specs=[_rows(tr, D), _rows(tr, D), _rows(tr, D), _full((1, D)), _full((1, D)), _full(w.shape)],
        out_specs=[_rows(tr, 2 * D), _rows(tr, D), _rows(tr, D)],
        out_shape=[_sds((lp, 2 * D), BF16), _sds((lp, D), F32), _sds((lp, D), F32)],
        compiler_params=_cp("parallel"))(o, ssm, h0, g_ao, g_post, w)


def _ffn_up(h1, g, w):
    lp = h1.shape[0]
    tr = _rt(lp, (320, 128))
    tn = D_FF // 2

    def body(h_ref, g_ref, w_ref, hn_ref, u_ref):
        hn = _rms(h_ref[...], g_ref[...]).astype(BF16)
        hn_ref[...] = hn
        u_ref[...] = _dot(hn, w_ref[...])

    return pl.pallas_call(
        body, name="ffn_up", grid=(lp // tr, 2 * D_FF // tn),
        in_specs=[pl.BlockSpec((tr, D), lambda i, j: (i, 0)), _full((1, D)), pl.BlockSpec((D, tn), lambda i, j: (0, j))],
        out_specs=[pl.BlockSpec((tr, D), lambda i, j: (i, 0)), pl.BlockSpec((tr, tn), lambda i, j: (i, j))],
        out_shape=[_sds((lp, D), BF16), _sds((lp, 2 * D_FF), F32)],
        compiler_params=_cp("parallel", "arbitrary"))(h1, g, w)


FFN_CB = 256


def _ffn_gate(u, cw, cb):
    lp = u.shape[0]
    tr = _rt(lp, (320, 128))

    def body(u_ref, uh_ref, cw_ref, cb_ref, a_ref):
        i = pl.program_id(0)
        for j in range(D_FF // FFN_CB):
            halves = []
            for off in (0, D_FF):
                sl = slice(off + j * FFN_CB, off + (j + 1) * FFN_CB)
                x = u_ref[:, sl]
                halo = jnp.where(i > 0, uh_ref[:, sl], 0.0)
                cv = cb_ref[:, sl] + cw_ref[FFN_K - 1:FFN_K, sl] * x
                for s in range(1, FFN_K):
                    cv = cv + cw_ref[FFN_K - 1 - s:FFN_K - s, sl] * _shift_down(x, halo, s)
                halves.append(cv)
            a_ref[:, j * FFN_CB:(j + 1) * FFN_CB] = (_silu(halves[0]) * halves[1]).astype(BF16)

    return pl.pallas_call(
        body, name="ffn_gate", grid=(lp // tr,),
        in_specs=[_rows(tr, 2 * D_FF), pl.BlockSpec((8, 2 * D_FF), lambda i: (jnp.maximum(i * (tr // 8) - 1, 0), 0)),
                  _full((8, 2 * D_FF)), _full((1, 2 * D_FF))],
        out_specs=_rows(tr, D_FF), out_shape=_sds((lp, D_FF), BF16),
        compiler_params=_cp("parallel"))(u, u, cw, cb)


def _ffn_down(a, w, h1, tgt, g_post):
    lp = a.shape[0]
    tr = _rt(lp, (320, 128))

    def body(a_ref, w_ref, h_ref, t_ref, g_ref, dh2_ref, dd_ref, dg_ref, loss_ref):
        i = pl.program_id(0)
        d = _dot(a_ref[...], w_ref[...])
        gv = g_ref[...]
        h2 = h_ref[...] + _rms(d, gv)
        grow = i * tr + lax.broadcasted_iota(jnp.int32, (tr, D), 0)
        err = jnp.where(grow >= FRONT, h2 - t_ref[...], 0.0)
        dh2 = err * (1.0 / D)
        dh2_ref[...] = dh2
        dd, dg = _rms_bwd(d, gv, dh2)
        dd_ref[...] = dd.astype(BF16)
        _acc_rows(dg_ref, dg, i == 0)
        part = 0.5 * jnp.sum(jnp.sum(err * err, axis=1, keepdims=True), axis=0, keepdims=True) * (1.0 / D)
        _acc_rows(loss_ref, jnp.broadcast_to(part, (8, 128)), i == 0)

    return pl.pallas_call(
        body, name="ffn_down", grid=(lp // tr,),
        in_specs=[_rows(tr, D_FF), _full(w.shape), _rows(tr, D), _rows(tr, D), _full((1, D))],
        out_specs=[_rows(tr, D), _rows(tr, D), _full((1, D)), _full((8, 128))],
        out_shape=[_sds((lp, D), F32), _sds((lp, D), BF16), _sds((1, D), F32), _sds((8, 128), F32)],
        compiler_params=_cp("arbitrary"))(a, w, h1, tgt, g_post)


def _ffn_gate_bwd(u, da, cw, cb):
    lp = u.shape[0]
    tr = _rt(lp, (320, 128))
    n = lp // tr

    def body(u_ref, uh_ref, da_ref, cw_ref, cb_ref, du_ref, dcw_ref, dcb_ref, nx_scr):
        i = pl.program_id(0)
        t = n - 1 - i
        first = i == 0

        @pl.when(first)
        def _():
            nx_scr[...] = jnp.zeros_like(nx_scr)

        grow = t * tr + lax.broadcasted_iota(jnp.int32, (tr, FFN_CB), 0)
        for j in range(D_FF // FFN_CB):
            cvs, shs, sls = [], [], []
            for off in (0, D_FF):
                sl = slice(off + j * FFN_CB, off + (j + 1) * FFN_CB)
                x = u_ref[:, sl]
                halo = jnp.where(t > 0, uh_ref[:, sl], 0.0)
                sh = [x] + [_shift_down(x, halo, s) for s in range(1, FFN_K)]
                cv = cb_ref[:, sl]
                for kk in range(FFN_K):
                    cv = cv + cw_ref[kk:kk + 1, sl] * sh[FFN_K - 1 - kk]
                cvs.append(cv)
                shs.append(sh)
                sls.append(sl)
            dav = da_ref[:, j * FFN_CB:(j + 1) * FFN_CB]
            dcv = (dav * cvs[1] * _dsilu(cvs[0]), dav * _silu(cvs[0]))
            for hf in range(2):
                sl = sls[hf]
                g = dcv[hf]
                rows = [jnp.sum(g * shs[hf][FFN_K - 1 - kk], axis=0, keepdims=True) for kk in range(FFN_K)]
                rows.append(jnp.zeros((8 - FFN_K, FFN_CB), F32))
                upd_w = jnp.concatenate(rows, axis=0)
                upd_b = jnp.sum(g, axis=0, keepdims=True)

                @pl.when(first)
                def _():
                    dcw_ref[:, sl] = upd_w
                    dcb_ref[:, sl] = upd_b

                @pl.when(jnp.logical_not(first))
                def _():
                    dcw_ref[:, sl] += upd_w
                    dcb_ref[:, sl] += upd_b

                nxt = nx_scr[:, sl]
                du = cw_ref[FFN_K - 1:FFN_K, sl] * g
                for s in range(1, FFN_K):
                    du = du + cw_ref[FFN_K - 1 - s:FFN_K - s, sl] * _shift_up(g, nxt, s)
                du_ref[:, sl] = jnp.where(grow >= PAD_ROWS, du, 0.0).astype(BF16)
                nx_scr[:, sl] = g[:8]

    return pl.pallas_call(
        body, name="ffn_gate_bwd", grid=(n,),
        in_specs=[pl.BlockSpec((tr, 2 * D_FF), lambda i: (n - 1 - i, 0)),
                  pl.BlockSpec((8, 2 * D_FF), lambda i: (jnp.maximum((n - 1 - i) * (tr // 8) - 1, 0), 0)),
                  pl.BlockSpec((tr, D_FF), lambda i: (n - 1 - i, 0)), _full((8, 2 * D_FF)), _full((1, 2 * D_FF))],
        out_specs=[pl.BlockSpec((tr, 2 * D_FF), lambda i: (n - 1 - i, 0)), _full((8, 2 * D_FF)), _full((1, 2 * D_FF))],
        out_shape=[_sds((lp, 2 * D_FF), BF16), _sds((8, 2 * D_FF), F32), _sds((1, 2 * D_FF), F32)],
        scratch_shapes=[pltpu.VMEM((8, 2 * D_FF), F32)],
        compiler_params=_cp("arbitrary"))(u, u, da, cw, cb)


def _norm_bwd_res(x, g, dy, res, name, mask_pad=False, out_dtype=F32):
    lp, c = x.shape
    tr = _rt(lp, (640, 128))

    def body(*refs):
        if res is None:
            x_ref, g_ref, dy_ref, o_ref, dg_ref = refs
        else:
            x_ref, g_ref, dy_ref, r_ref, o_ref, dg_ref = refs
        i = pl.program_id(0)
        dyv = dy_ref[...].astype(F32)
        if mask_pad:
            grow = i * tr + lax.broadcasted_iota(jnp.int32, (tr, c), 0)
            dyv = jnp.where(grow >= PAD_ROWS, dyv, 0.0)
        dx, dg = _rms_bwd(x_ref[...].astype(F32), g_ref[...], dyv)
        if res is not None:
            dx = dx + r_ref[...]
        o_ref[...] = dx.astype(out_dtype)
        _acc_rows(dg_ref, dg, i == 0)

    args = [x, g, dy] + ([] if res is None else [res])
    in_specs = [_rows(tr, c), _full((1, c)), pl.BlockSpec((tr, c), lambda i: (i, 0))] + ([] if res is None else [_rows(tr, c)])
    return pl.pallas_call(
        body, name=name, grid=(lp // tr,), in_specs=in_specs, out_specs=[_rows(tr, c), _full((1, c))],
        out_shape=[_sds((lp, c), out_dtype), _sds((1, c), F32)], compiler_params=_cp("arbitrary"))(*args)


def _mla_bwd(dq, dk, dv, lat, qg, kvg, wq, wkv, cos, sa, sb):
    lp = lat.shape[0]
    tr = _rt(lp, (320, 128))

    def body(dq_ref, dk_ref, dv_ref, lat_ref, qg_ref, kvg_ref, wq_ref, wkv_ref, cos_ref, sa_ref, sb_ref,
             dqf_ref, dkvf_ref, dlat_ref, dqg_ref, dkvg_ref):
        i = pl.program_id(0)
        cos_v, sa_v, sb_v = cos_ref[...], sa_ref[...], sb_ref[...]
        dkpe = jnp.zeros((tr, 128), F32)
        for h in range(MLA_H):
            dqh = dq_ref[h]
            dqf_ref[:, h * DN:(h + 1) * DN] = dqh[:, :DN].astype(BF16)
            dqf_ref[:, D + h * 128:D + (h + 1) * 128] = _rope_t(dqh[:, DN:], cos_v, sa_v, sb_v).astype(BF16)
            dkh = dk_ref[h]
            dkvf_ref[:, h * DN:(h + 1) * DN] = dkh[:, :DN].astype(BF16)
            dkpe = dkpe + dkh[:, DN:]
            dkvf_ref[:, D + h * DV:D + (h + 1) * DV] = dv_ref[h].astype(BF16)
        dql = _dot_nt(dqf_ref[...], wq_ref[...])
        dkl = _dot_nt(dkvf_ref[...], wkv_ref[...])
        lat_v = lat_ref[...]
        dqc, dqg = _rms_bwd(lat_v[:, :QR], qg_ref[...], dql)
        dkc, dkg = _rms_bwd(lat_v[:, QR:QR + KVR], kvg_ref[...], dkl)
        dlat_ref[:, :QR] = dqc
        dlat_ref[:, QR:QR + KVR] = dkc
        dlat_ref[:, QR + KVR:] = _rope_t(dkpe, cos_v, sa_v, sb_v)
        _acc_rows(dqg_ref, dqg, i == 0)
        _acc_rows(dkvg_ref, dkg, i == 0)

    hb = lambda w: pl.BlockSpec((MLA_H, tr, w), lambda i: (0, i, 0))
    return pl.pallas_call(
        body, name="mla_bwd", grid=(lp // tr,),
        in_specs=[hb(256), hb(256), hb(128), _rows(tr, LAT_W), _full((1, QR)), _full((1, KVR)), _full(wq.shape),
                  _full(wkv.shape), _rows(tr, 128), _rows(tr, 128), _rows(tr, 128)],
        out_specs=[_rows(tr, 2 * D), _rows(tr, 2 * D), _rows(tr, LAT_W), _full((1, QR)), _full((1, KVR))],
        out_shape=[_sds((lp, 2 * D), BF16), _sds((lp, 2 * D), BF16), _sds((lp, LAT_W), F32), _sds((1, QR), F32),
                   _sds((1, KVR), F32)],
        compiler_params=_cp("arbitrary"))(dq, dk, dv, lat, qg, kvg, wq, wkv, cos, sa, sb)


def _inproj_bwd(dlat, dz, dxbc, ddt, w, h0, g, dh1):
    lp = h0.shape[0]
    tr = _rt(lp, (320, 128))
    segs = ((0, LAT_W), (LAT_W, LAT_W + D_SSM), (LAT_W + D_SSM, LAT_W + D_SSM + D_XBC), (IN_P - 128, IN_P))

    def body(dl_ref, dz_ref, dx_ref, dt_ref, w_ref, h_ref, g_ref, r_ref, o_ref, dg_ref):
        dhn = jnp.zeros((tr, D), F32)
        for ref, (a, b) in zip((dl_ref, dz_ref, dx_ref, dt_ref), segs):
            dhn = dhn + _dot_nt(ref[...].astype(BF16), w_ref[:, a:b])
        dx, dg = _rms_bwd(h_ref[...], g_ref[...], dhn)
        o_ref[...] = dx + r_ref[...]
        _acc_rows(dg_ref, dg, pl.program_id(0) == 0)

    return pl.pallas_call(
        body, name="inproj_bwd", grid=(lp // tr,),
        in_specs=[_rows(tr, LAT_W), _rows(tr, D_SSM), _rows(tr, D_XBC), _rows(tr, 128), _full(w.shape), _rows(tr, D),
                  _full((1, D)), _rows(tr, D)],
        out_specs=[_rows(tr, D), _full((1, D))], out_shape=[_sds((lp, D), F32), _sds((1, D), F32)],
        compiler_params=_cp("arbitrary"))(dlat, dz, dxbc, ddt, w, h0, g, dh1)


def _rope_tables(lp):
    pos = (jnp.arange(lp, dtype=jnp.int32) - PAD_ROWS).astype(F32)
    inv = ROPE_THETA ** (-jnp.arange(0, DR, 2, dtype=F32) / DR)
    ang = pos[:, None] * inv[None, :]
    cos, sin = jnp.cos(ang), jnp.sin(ang)
    zero = jnp.zeros_like(sin)
    cos128 = jnp.concatenate([cos, cos, cos, cos], axis=1)
    sa128 = jnp.concatenate([-sin, zero, -sin, zero], axis=1)
    sb128 = jnp.concatenate([zero, sin, zero, sin], axis=1)
    return cos128, sa128, sb128


def _pad_rows8(w):
    return jnp.concatenate([w, jnp.zeros((8 - w.shape[0], w.shape[1]), w.dtype)], axis=0)


def _lane_pad(v):
    return jnp.concatenate([v, jnp.zeros((v.shape[0], 128 - v.shape[1]), v.dtype)], axis=1)


def _device_step(x, tgt, meta, p):
    s = x.shape[0]
    lp = s + FRONT
    zpad = jnp.zeros((PAD_ROWS, D), F32)
    h0 = jnp.concatenate([zpad, meta, x], axis=0)
    tgt_p = jnp.concatenate([jnp.zeros((FRONT, D), F32), tgt], axis=0)
    cos, sa, sb = _rope_tables(lp)

    w_in = p["w_in"]
    w_in_p = jnp.concatenate([w_in[:, :QR + KVR + DR], jnp.zeros((D, 64), BF16), w_in[:, QR + KVR + DR:],
                              jnp.zeros((D, 128 - SSM_H), BF16)], axis=1)
    w_uq = p["w_uq"]
    wq_p = jnp.concatenate([w_uq[:, :, :DN].reshape(QR, MLA_H * DN),
                            jnp.concatenate([w_uq[:, :, DN:], jnp.zeros((QR, MLA_H, 128 - DR), BF16)], axis=2).reshape(QR, MLA_H * 128)],
                           axis=1)
    w_ukv = p["w_ukv"]
    wkv_p = jnp.concatenate([w_ukv[:, :, :DN].reshape(KVR, MLA_H * DN), w_ukv[:, :, DN:].reshape(KVR, MLA_H * DV)], axis=1)
    scw = _pad_rows8(p["ssm_conv_w"])
    fcw = _pad_rows8(p["ffn_conv_w"])
    dtb, alog = _lane_pad(p["ssm_dt_bias"]), _lane_pad(p["ssm_A_log"])
    dtbT, alogT = p["ssm_dt_bias"].reshape(SSM_H, 1), p["ssm_A_log"].reshape(SSM_H, 1)
    d_e = jnp.repeat(p["ssm_D"], SSM_P, axis=1)

    hn, lat, z, xbc, dtr = _inproj(h0, p["norm_mix_pre"], w_in_p)
    dtrT = dtr[:, :SSM_H].T
    q, k, v, qlat, kvlat = _mla_prep(lat, p["q_a_norm"], p["kv_a_norm"], wq_p, wkv_p, cos, sa, sb)
    o, lse = _attn_fwd(q, k, v)
    ssm, st = _ssd_fwd(xbc, z, dtr, dtrT, scw, p["ssm_conv_b"], dtb, dtbT, alog, alogT, d_e, p["ssm_norm"])
    mixin, mix, h1 = _mixout_fwd(o, ssm, h0, p["attn_out_norm"], p["norm_mix_post"], p["w_out"])
    hn2, u = _ffn_up(h1, p["norm_ffn_pre"], p["w_up"])
    a = _ffn_gate(u, fcw, p["ffn_conv_b"])
    dh2, dd, g_ffn_post, loss = _ffn_down(a, p["w_down"], h1, tgt_p, p["norm_ffn_post"])

    da = _mm_nt(dd, p["w_down"], "ffn_da")
    g_w_down = _mm_tn(a, dd, "ffn_dw_down", tn=512)
    du, g_fcw, g_fcb = _ffn_gate_bwd(u, da, fcw, p["ffn_conv_b"])
    dhn2 = _mm_nt(du, p["w_up"], "ffn_dhn", tms=(320, 128))
    g_w_up = _mm_tn(hn2, du, "ffn_dw_up", tn=D_FF // 2)
    dh1, g_ffn_pre = _norm_bwd_res(h1, p["norm_ffn_pre"], dhn2, dh2, "ffn_norm_bwd")
    dmix, g_mix_post = _norm_bwd_res(mix, p["norm_mix_post"], dh1, None, "mix_post_bwd", mask_pad=True, out_dtype=BF16)
    dmixin = _mm_nt(dmix, p["w_out"], "mix_dmixin")
    g_w_out = _mm_tn(mixin, dmix, "mix_dw_out", tn=512)
    do, g_ao = _norm_bwd_res(o, p["attn_out_norm"], dmixin, None, "attn_out_norm_bwd")
    dq, delta = _attn_bwd_dq(q, k, v, o, do, lse)
    t = _rt(lp, (640, 128))
    to_row = lambda st_: st_[:, :, 0].reshape(MLA_H, lp // t, 1, t)
    dk, dv = _attn_bwd_dkv(q, k, v, do, to_row(lse), to_row(delta))
    dqf, dkvf, dlat, g_qa, g_kva = _mla_bwd(dq, dk, dv, lat, p["q_a_norm"], p["kv_a_norm"], wq_p, wkv_p, cos, sa, sb)
    g_wq_p = _mm_tn(qlat, dqf, "mla_dw_uq")
    g_wkv_p = _mm_tn(kvlat, dkvf, "mla_dw_ukv")
    dz, dxbc, ddtr, g_scw, g_scb, g_dtb, g_alog, g_dd, g_ssm_norm = _ssd_bwd(
        dmixin, xbc, z, dtr, dtrT, st, scw, p["ssm_conv_b"], dtb, dtbT, alog, alogT, d_e, p["ssm_norm"])
    dh0, g_mix_pre = _inproj_bwd(dlat, dz, dxbc, ddtr, w_in_p, h0, p["norm_mix_pre"], dh1)
    g_in_p = jnp.concatenate([_mm_tn(hn, dlat, "in_dw_lat"), _mm_tn(hn, dz, "in_dw_z"), _mm_tn(hn, dxbc, "in_dw_xbc"),
                              _mm_tn(hn, ddtr, "in_dw_dt")], axis=1)

    g_w_in = jnp.concatenate([g_in_p[:, :QR + KVR + DR], g_in_p[:, LAT_W:LAT_W + D_SSM + D_XBC + SSM_H]], axis=1)
    g_w_uq = jnp.concatenate([g_wq_p[:, :D].reshape(QR, MLA_H, DN), g_wq_p[:, D:].reshape(QR, MLA_H, 128)[:, :, :DR]], axis=2)
    g_w_ukv = jnp.concatenate([g_wkv_p[:, :D].reshape(KVR, MLA_H, DN), g_wkv_p[:, D:].reshape(KVR, MLA_H, DV)], axis=2)
    grads = dict(
        norm_mix_pre=g_mix_pre, norm_mix_post=g_mix_post, norm_ffn_pre=g_ffn_pre, norm_ffn_post=g_ffn_post, w_in=g_w_in,
        q_a_norm=g_qa, w_uq=g_w_uq, kv_a_norm=g_kva, w_ukv=g_w_ukv, attn_out_norm=g_ao, ssm_conv_w=g_scw[:SSM_K],
        ssm_conv_b=g_scb, ssm_dt_bias=g_dtb[:, :SSM_H], ssm_A_log=g_alog[:, :SSM_H], ssm_D=g_dd[:, :SSM_H],
        ssm_norm=g_ssm_norm, w_out=g_w_out, w_up=g_w_up, ffn_conv_w=g_fcw[:FFN_K], ffn_conv_b=g_fcb, w_down=g_w_down)
    return loss, dh0[FRONT:], dh0[PAD_ROWS:FRONT], grads


N_CHIPS = 4
BIG = (("w_in", (D, D_IN // N_CHIPS)), ("w_uq", (QR // N_CHIPS, MLA_H, DN + DR)), ("w_ukv", (KVR // N_CHIPS, MLA_H, DN + DV)),
       ("w_out", (2 * D // N_CHIPS, D)), ("w_up", (D, 2 * D_FF // N_CHIPS)), ("w_down", (D_FF // N_CHIPS, D)))
BIG_AXIS = dict(w_in=1, w_uq=0, w_ukv=0, w_out=0, w_up=1, w_down=0)
SHARD_ELEMS = sum(functools.reduce(lambda a, b: a * b, s) for _, s in BIG)
HALF_ROWS = SHARD_ELEMS // 256
SMALL_SHARDED = (("meta_tokens", (N_META, D // N_CHIPS)), ("ssm_conv_w", (SSM_K, D_XBC // N_CHIPS)),
                 ("ffn_conv_w", (FFN_K, 2 * D_FF // N_CHIPS)))
SMALL_REPL = (("norm_mix_pre", D), ("norm_mix_post", D), ("norm_ffn_pre", D), ("norm_ffn_post", D), ("q_a_norm", QR),
              ("kv_a_norm", KVR), ("attn_out_norm", D), ("ssm_conv_b", D_XBC), ("ssm_dt_bias", SSM_H), ("ssm_A_log", SSM_H),
              ("ssm_D", SSM_H), ("ssm_norm", D_SSM), ("ffn_conv_b", 2 * D_FF))
ANY = pl.BlockSpec(memory_space=pl.ANY)


def _pad128(v):
    n = v.shape[0]
    return jnp.concatenate([v, jnp.zeros(((-n) % 128,), v.dtype)]) if n % 128 else v


def _pack_rows(vs, rows):
    flat = jnp.concatenate([_pad128(v.reshape(-1)) for v in vs])
    flat = jnp.concatenate([flat, jnp.zeros((rows * 128 - flat.shape[0],), flat.dtype)])
    return flat.reshape(rows, 128)


def _unpack_rows(pack, sizes):
    flat = pack.reshape(-1)
    out, off = [], 0
    for n in sizes:
        out.append(flat[off:off + n])
        off += n + (-n) % 128
    return out


def _my_place():
    return lax.axis_index("x"), lax.axis_index("y"), lax.axis_index("c")


def _other_chips(x, y):
    return [(1 - x, y), (x, 1 - y), (1 - x, 1 - y)]


def _remote(src, dst, send, recv, dev):
    return pltpu.make_async_remote_copy(src_ref=src, dst_ref=dst, send_sem=send, recv_sem=recv, device_id=dev,
                                        device_id_type=MESH)


SMALL_AG_ROWS = 80


def _allgather_weights(pack, small):
    def body(pack_ref, small_ref, out_ref, osm_ref, send, recv, lsem):
        x, y, c = _my_place()
        me = 2 * x + y
        chips = _other_chips(x, y)
        loc = [pltpu.make_async_copy(pack_ref, out_ref.at[me], lsem.at[0]),
               pltpu.make_async_copy(small_ref, osm_ref.at[me], lsem.at[1])]
        for cp in loc:
            cp.start()
        sends = []
        for kk, (cx, cy) in enumerate(chips):
            sends.append(_remote(pack_ref.at[c], out_ref.at[me, c], send.at[kk], recv.at[kk], (cx, cy, c)))
            sends.append(_remote(small_ref, osm_ref.at[me], send.at[6 + kk], recv.at[6 + kk], (cx, cy, c)))
        for cp in sends:
            cp.start()
        for kk, (cx, cy) in enumerate(chips):
            src = 2 * cx + cy
            _remote(pack_ref.at[c], out_ref.at[src, c], send.at[kk], recv.at[kk], (cx, cy, c)).wait_recv()
            fwd = _remote(out_ref.at[src, c], out_ref.at[src, c], send.at[3 + kk], recv.at[3 + kk], (x, y, 1 - c))
            fwd.start()
            sends.append(fwd)
        for kk, (cx, cy) in enumerate(chips):
            src = 2 * cx + cy
            _remote(pack_ref.at[c], out_ref.at[src, 1 - c], send.at[3 + kk], recv.at[3 + kk], (x, y, 1 - c)).wait_recv()
            _remote(small_ref, osm_ref.at[src], send.at[6 + kk], recv.at[6 + kk], (cx, cy, c)).wait_recv()
        for cp in sends:
            cp.wait_send()
        for cp in loc:
            cp.wait()

    return pl.pallas_call(
        body, name="allgather_weights", in_specs=[ANY, ANY], out_specs=[ANY, ANY],
        out_shape=[_sds((N_CHIPS, 2, HALF_ROWS, 128), BF16), _sds((N_CHIPS, SMALL_AG_ROWS, 128), F32)],
        scratch_shapes=[pltpu.SemaphoreType.DMA((9,)), pltpu.SemaphoreType.DMA((9,)), pltpu.SemaphoreType.DMA((2,))])(pack, small)


def _swap_sibling(v, name):
    def body(v_ref, o_ref, send, recv):
        x, y, c = _my_place()
        cp = _remote(v_ref, o_ref, send, recv, (x, y, 1 - c))
        cp.start()
        cp.wait()

    return pl.pallas_call(
        body, name=name, in_specs=[ANY], out_specs=ANY, out_shape=_sds(v.shape, v.dtype),
        scratch_shapes=[pltpu.SemaphoreType.DMA, pltpu.SemaphoreType.DMA])(v)


def _share_sibling(half):
    def body(h_ref, o_ref, send, recv, lsem):
        x, y, c = _my_place()
        loc = pltpu.make_async_copy(h_ref, o_ref.at[c], lsem)
        loc.start()
        cp = _remote(h_ref, o_ref.at[c], send, recv, (x, y, 1 - c))
        cp.start()
        cp.wait_send()
        _remote(h_ref, o_ref.at[1 - c], send, recv, (x, y, 1 - c)).wait_recv()
        loc.wait()

    return pl.pallas_call(
        body, name="share_sibling", in_specs=[ANY], out_specs=ANY, out_shape=_sds((2,) + half.shape, half.dtype),
        scratch_shapes=[pltpu.SemaphoreType.DMA, pltpu.SemaphoreType.DMA, pltpu.SemaphoreType.DMA])(half)


def _exchange_chips(p):
    def body(p_ref, o_ref, send, recv):
        x, y, c = _my_place()
        cps = [_remote(p_ref.at[2 * cx + cy], o_ref.at[kk], send.at[kk], recv.at[kk], (cx, cy, c))
               for kk, (cx, cy) in enumerate(_other_chips(x, y))]
        for cp in cps:
            cp.start()
        for cp in cps:
            cp.wait()

    return pl.pallas_call(
        body, name="exchange_chips", in_specs=[ANY], out_specs=ANY, out_shape=_sds((3,) + p.shape[1:], p.dtype),
        scratch_shapes=[pltpu.SemaphoreType.DMA((3,)), pltpu.SemaphoreType.DMA((3,))])(p)


def _add_n(arrs, name):
    r = arrs[0].shape[0]
    tr = _rt(r, (HALF_ROWS // 2, 8))

    def body(*refs):
        acc = refs[0][...]
        for ref in refs[1:-1]:
            acc = acc + ref[...]
        refs[-1][...] = acc

    return pl.pallas_call(
        body, name=name, grid=(r // tr,), in_specs=[_rows(tr, 128)] * len(arrs), out_specs=_rows(tr, 128),
        out_shape=_sds((r, 128), F32), compiler_params=_cp("parallel"))(*arrs)


SMALL_AR_ROWS = 424


def _allreduce_small(v):
    def body(v_ref, o_ref, gath, send, recv):
        x, y, c = _my_place()
        me = 4 * x + 2 * y + c
        gath[me] = v_ref[...]
        cps = []
        for dd in range(1, 8):
            dx, dy, dc = dd >> 2, (dd >> 1) & 1, dd & 1
            peer = (1 - x if dx else x, 1 - y if dy else y, 1 - c if dc else c)
            cps.append(_remote(v_ref, gath.at[me], send.at[dd - 1], recv.at[dd - 1], peer))
        for cp in cps:
            cp.start()
        for cp in cps:
            cp.wait()
        acc = gath[0]
        for dev in range(1, 8):
            acc = acc + gath[dev]
        o_ref[...] = acc

    vm = pl.BlockSpec(memory_space=pltpu.VMEM)
    return pl.pallas_call(
        body, name="allreduce_small", in_specs=[vm], out_specs=vm, out_shape=_sds(v.shape, F32),
        scratch_shapes=[pltpu.VMEM((8,) + v.shape, F32), pltpu.SemaphoreType.DMA((7,)), pltpu.SemaphoreType.DMA((7,))])(v)


def _adamw(w, g, m, v, name):
    r, c = w.shape
    tr = next(t for t in range(r, 0, -1) if r % t == 0 and (t % 8 == 0 or t == r) and t * c * 4 <= (1 << 20))

    def body(w_ref, g_ref, m_ref, v_ref, d_ref, m2_ref, v2_ref):
        gv = g_ref[...]
        m2 = ADAM_B1 * m_ref[...] + (1.0 - ADAM_B1) * gv
        v2 = ADAM_B2 * v_ref[...] + (1.0 - ADAM_B2) * jnp.square(gv)
        m_hat = m2 / (1.0 - ADAM_B1 ** ADAM_STEP)
        v_hat = v2 / (1.0 - ADAM_B2 ** ADAM_STEP)
        d_ref[...] = -ADAM_LR * (m_hat / (jnp.sqrt(v_hat) + ADAM_EPS) + ADAM_WD * w_ref[...])
        m2_ref[...] = m2
        v2_ref[...] = v2

    return pl.pallas_call(
        body, name=name, grid=(r // tr,), in_specs=[_rows(tr, c)] * 4, out_specs=[_rows(tr, c)] * 3,
        out_shape=[_sds((r, c), F32)] * 3, compiler_params=_cp("parallel"))(w, g, m, v)


WEIGHT_NAMES = ("meta_tokens", "norm_mix_pre", "norm_mix_post", "norm_ffn_pre", "norm_ffn_post", "w_in", "q_a_norm", "w_uq",
                "kv_a_norm", "w_ukv", "attn_out_norm", "ssm_conv_w", "ssm_conv_b", "ssm_dt_bias", "ssm_A_log", "ssm_D",
                "ssm_norm", "w_out", "w_up", "ffn_conv_w", "ffn_conv_b", "w_down")
SMALL_ADAM_ROWS = 192


def kernel(x, meta_tokens, norm_mix_pre, norm_mix_post, norm_ffn_pre, norm_ffn_post, w_in, q_a_norm, w_uq, kv_a_norm, w_ukv, attn_out_norm, ssm_conv_w, ssm_conv_b, ssm_dt_bias, ssm_A_log, ssm_D, ssm_norm, w_out, w_up, ffn_conv_w, ffn_conv_b, w_down, loss_target, m_meta_tokens, m_norm_mix_pre, m_norm_mix_post, m_norm_ffn_pre, m_norm_ffn_post, m_w_in, m_q_a_norm, m_w_uq, m_kv_a_norm, m_w_ukv, m_attn_out_norm, m_ssm_conv_w, m_ssm_conv_b, m_ssm_dt_bias, m_ssm_A_log, m_ssm_D, m_ssm_norm, m_w_out, m_w_up, m_ffn_conv_w, m_ffn_conv_b, m_w_down, v_meta_tokens, v_norm_mix_pre, v_norm_mix_post, v_norm_ffn_pre, v_norm_ffn_post, v_w_in, v_q_a_norm, v_w_uq, v_kv_a_norm, v_w_ukv, v_attn_out_norm, v_ssm_conv_w, v_ssm_conv_b, v_ssm_dt_bias, v_ssm_A_log, v_ssm_D, v_ssm_norm, v_w_out, v_w_up, v_ffn_conv_w, v_ffn_conv_b, v_w_down):
    args = locals()
    w = {n: args[n] for n in WEIGHT_NAMES}
    mom = {n: args["m_" + n] for n in WEIGHT_NAMES}
    var = {n: args["v_" + n] for n in WEIGHT_NAMES}
    cx, cy, cc = _my_place()
    chip = 2 * cx + cy

    pack = jnp.concatenate([w[n].reshape(-1) for n, _ in BIG]).astype(BF16).reshape(2, HALF_ROWS, 128)
    small = _pack_rows([w[n] for n, _ in SMALL_SHARDED], SMALL_AG_ROWS)
    big_all, small_all = _allgather_weights(pack, small)
    big_flat = big_all.reshape(N_CHIPS, SHARD_ELEMS)
    p, off = {}, 0
    for n, shp in BIG:
        size = functools.reduce(lambda a, b: a * b, shp)
        parts = [big_flat[j, off:off + size].reshape(shp) for j in range(N_CHIPS)]
        p[n] = jnp.concatenate(parts, axis=BIG_AXIS[n])
        off += size
    sm_parts = [_unpack_rows(small_all[j], [a * b for _, (a, b) in SMALL_SHARDED]) for j in range(N_CHIPS)]
    for i, (n, shp) in enumerate(SMALL_SHARDED):
        p[n] = jnp.concatenate([sm_parts[j][i].reshape(shp) for j in range(N_CHIPS)], axis=1)
    for n, _ in SMALL_REPL:
        p[n] = w[n]
    meta_full = p.pop("meta_tokens")

    loss_part, gx, gmeta, g = _device_step(x[0], loss_target[0], meta_full, p)

    small_names = [n for n, _ in SMALL_REPL] + ["ssm_conv_w", "ffn_conv_w"]
    small_sizes = [128] + [sz for _, sz in SMALL_REPL] + [N_META * D, SSM_K * D_XBC, FFN_K * 2 * D_FF]
    order = [n for n, _ in SMALL_REPL]
    sp = _pack_rows([loss_part[0]] + [g[n] for n in order] + [gmeta, g["ssm_conv_w"], g["ffn_conv_w"]], SMALL_AR_ROWS)
    red = _unpack_rows(_allreduce_small(sp), small_sizes)
    loss = red[0][0]
    gfull = {n: red[1 + i].reshape(1, -1) for i, n in enumerate(order)}
    n_r = len(order)
    gfull["meta_tokens"] = lax.dynamic_slice_in_dim(red[1 + n_r].reshape(N_META, D), chip * (D // N_CHIPS), D // N_CHIPS, axis=1)
    gfull["ssm_conv_w"] = lax.dynamic_slice_in_dim(red[2 + n_r].reshape(SSM_K, D_XBC), chip * (D_XBC // N_CHIPS),
                                                   D_XBC // N_CHIPS, axis=1)[None]
    gfull["ffn_conv_w"] = lax.dynamic_slice_in_dim(red[3 + n_r].reshape(FFN_K, 2 * D_FF), chip * (2 * D_FF // N_CHIPS),
                                                   2 * D_FF // N_CHIPS, axis=1)[None]

    shards = []
    for j in range(N_CHIPS):
        pieces = []
        for n, shp in BIG:
            ax = BIG_AXIS[n]
            pieces.append(lax.slice_in_dim(g[n], j * shp[ax], (j + 1) * shp[ax], axis=ax).reshape(-1))
        shards.append(jnp.concatenate(pieces).reshape(2, HALF_ROWS, 128))
    gp = jnp.stack(shards, axis=1)
    mine = lax.dynamic_index_in_dim(gp, cc, axis=0, keepdims=False)
    theirs = lax.dynamic_index_in_dim(gp, 1 - cc, axis=0, keepdims=False)
    from_sib = _swap_sibling(theirs, "reduce_sibling")
    pair = _add_n([mine.reshape(-1, 128), from_sib.reshape(-1, 128)], "reduce_pair").reshape(N_CHIPS, HALF_ROWS, 128)
    got = _exchange_chips(pair)
    own = lax.dynamic_index_in_dim(pair, chip, axis=0, keepdims=False)
    half = _add_n([own, got[0], got[1], got[2]], "reduce_chips")
    flat = _share_sibling(half).reshape(SHARD_ELEMS)
    off = 0
    for n, shp in BIG:
        size = functools.reduce(lambda a, b: a * b, shp)
        gfull[n] = flat[off:off + size].reshape((1,) + shp)
        off += size

    delta, new_m, new_v = {}, {}, {}
    for n, shp in BIG:
        two_d = (shp[0], functools.reduce(lambda a, b: a * b, shp[1:]))
        outs = _adamw(w[n].reshape(two_d), gfull[n].reshape(two_d), mom[n].reshape(two_d), var[n].reshape(two_d), "adamw_" + n)
        delta[n], new_m[n], new_v[n] = (o.reshape((1,) + shp) for o in outs)
    snames = order + ["meta_tokens", "ssm_conv_w", "ffn_conv_w"]
    ssizes = [functools.reduce(lambda a, b: a * b, w[n].shape) for n in snames]
    packs = [_pack_rows([d[n] for n in snames], SMALL_ADAM_ROWS) for d in (w, gfull, mom, var)]
    outs = _adamw(*packs, "adamw_small")
    for d, o in zip((delta, new_m, new_v), outs):
        for n, piece in zip(snames, _unpack_rows(o, ssizes)):
            d[n] = piece.reshape(w[n].shape)
    gout = {n: gfull[n].reshape(w[n].shape) for n in WEIGHT_NAMES}
    return (loss, gx[None], *[gout[n] for n in WEIGHT_NAMES], *[delta[n] for n in WEIGHT_NAMES],
            *[new_m[n] for n in WEIGHT_NAMES], *[new_v[n] for n in WEIGHT_NAMES])
```

```python
import functools

import jax
import jax.numpy as jnp
from jax import lax
from jax.experimental import pallas as pl
from jax.experimental.pallas import tpu as pltpu

F32 = jnp.float32
BF16 = jnp.bfloat16
HI = lax.Precision.HIGHEST

D = 1024
N_META = 16
FRONT = 128
PAD_ROWS = FRONT - N_META
MLA_H = 8
DN, DR, DV = 128, 64, 128
QR, KVR = 384, 256
SOFTMAX_SCALE = (DN + DR) ** -0.5
ROPE_THETA = 10000.0
SSM_H, SSM_P, SSM_G, SSM_N, SSM_K = 16, 64, 2, 128, 4
CHUNK = 128
D_SSM = SSM_H * SSM_P
D_XBC = D_SSM + 2 * SSM_G * SSM_N
GSZ = D_SSM // SSM_G
D_FF = 2816
FFN_K = 3
EPS = 1e-6
IN_SPLITS = (QR, KVR, DR, D_SSM, D_XBC, SSM_H)
D_IN = sum(IN_SPLITS)
LAT_W = 768
IN_P = LAT_W + D_SSM + D_XBC + 128
NEG = -1e30
LOG2E = 1.4426950408889634
LN2 = 0.6931471805599453
Q_SCALE = SOFTMAX_SCALE * LOG2E

ADAM_LR, ADAM_B1, ADAM_B2, ADAM_EPS, ADAM_WD, ADAM_STEP = 0.001, 0.9, 0.999, 1e-08, 0.01, 10

VMEM_LIMIT = 56 * 1024 * 1024
MESH = pl.DeviceIdType.MESH


def _sds(shape, dtype):
    return jax.ShapeDtypeStruct(shape, dtype)


def _cp(*sem):
    return pltpu.CompilerParams(dimension_semantics=sem, vmem_limit_bytes=VMEM_LIMIT)


def _rt(n, cands):
    for c in cands:
        if n % c == 0:
            return c
    raise ValueError((n, cands))


def _full(shape):
    nd = len(shape)
    return pl.BlockSpec(shape, lambda *_: (0,) * nd)


def _rows(tr, c):
    return pl.BlockSpec((tr, c), lambda i: (i, 0))


def _sigmoid(x):
    return 1.0 / (1.0 + jnp.exp(-x))


def _silu(x):
    return x * _sigmoid(x)


def _dsilu(x):
    s = _sigmoid(x)
    return s * (1.0 + x * (1.0 - s))


def _softplus(x):
    return jnp.maximum(x, 0.0) + jnp.log(1.0 + jnp.exp(-jnp.abs(x)))


def _rms(x, g):
    r = lax.rsqrt(jnp.mean(x * x, axis=-1, keepdims=True) + EPS)
    return x * r * g


def _rms_bwd(x, g, dy):
    r = lax.rsqrt(jnp.mean(x * x, axis=-1, keepdims=True) + EPS)
    xh = x * r
    dxh = dy * g
    dx = r * (dxh - xh * jnp.mean(dxh * xh, axis=-1, keepdims=True))
    return dx, jnp.sum(dy * xh, axis=0, keepdims=True)


def _dot(a, b):
    return jnp.dot(a, b, preferred_element_type=F32)


def _dot_nt(a, b):
    return lax.dot_general(a, b, (((1,), (1,)), ((), ())), preferred_element_type=F32)


def _dot_tn(a, b):
    return lax.dot_general(a, b, (((0,), (0,)), ((), ())), preferred_element_type=F32)


def _dot_hi(a, b):
    return jnp.dot(a, b, precision=HI, preferred_element_type=F32)


def _dot_nt_hi(a, b):
    return lax.dot_general(a, b, (((1,), (1,)), ((), ())), precision=HI, preferred_element_type=F32)


def _shift_down(x, halo, j):
    xr = pltpu.roll(x, j, axis=0)
    hr = pltpu.roll(halo, j, axis=0)
    row = lax.broadcasted_iota(jnp.int32, (8, x.shape[1]), 0)
    first = jnp.where(row < j, hr, xr[:8])
    return jnp.concatenate([first, xr[8:]], axis=0)


def _shift_up(x, nxt, j):
    t = x.shape[0]
    xr = pltpu.roll(x, t - j, axis=0)
    nr = pltpu.roll(nxt, 8 - j, axis=0)
    row = lax.broadcasted_iota(jnp.int32, (8, x.shape[1]), 0)
    last = jnp.where(row + j >= 8, nr, xr[t - 8:])
    return jnp.concatenate([xr[:t - 8], last], axis=0)


def _acc_rows(ref, val, first):
    @pl.when(first)
    def _():
        ref[...] = val

    @pl.when(jnp.logical_not(first))
    def _():
        ref[...] += val


def _mm_nt(a, b, name, out_dtype=F32, tms=(640, 320, 128)):
    m, k = a.shape
    n = b.shape[0]
    tm = _rt(m, tms)

    def body(a_ref, b_ref, o_ref):
        o_ref[...] = _dot_nt(a_ref[...].astype(BF16), b_ref[...]).astype(out_dtype)

    return pl.pallas_call(
        body, name=name, grid=(m // tm,), in_specs=[_rows(tm, k), _full(b.shape)], out_specs=_rows(tm, n),
        out_shape=_sds((m, n), out_dtype), compiler_params=_cp("parallel"))(a, b)


def _mm_tn(a, b, name, tn=None, trs=(640, 128)):
    r, m = a.shape
    n = b.shape[1]
    tn = n if tn is None else tn
    tr = _rt(r, trs)

    def body(a_ref, b_ref, o_ref):
        part = _dot_tn(a_ref[...].astype(BF16), b_ref[...].astype(BF16))
        _acc_rows(o_ref, part, pl.program_id(1) == 0)

    return pl.pallas_call(
        body, name=name, grid=(n // tn, r // tr),
        in_specs=[pl.BlockSpec((tr, m), lambda j, i: (i, 0)), pl.BlockSpec((tr, tn), lambda j, i: (i, j))],
        out_specs=pl.BlockSpec((m, tn), lambda j, i: (0, j)),
        out_shape=_sds((m, n), F32), compiler_params=_cp("parallel", "arbitrary"))(a, b)


def _inproj(h0, g, w):
    lp = h0.shape[0]
    tr = _rt(lp, (320, 128))
    segs = ((0, LAT_W), (LAT_W, LAT_W + D_SSM), (LAT_W + D_SSM, LAT_W + D_SSM + D_XBC), (IN_P - 128, IN_P))

    def body(h_ref, g_ref, w_ref, hn_ref, lat_ref, z_ref, xbc_ref, dt_ref):
        hn = _rms(h_ref[...], g_ref[...]).astype(BF16)
        hn_ref[...] = hn
        for ref, (a, b) in zip((lat_ref, z_ref, xbc_ref, dt_ref), segs):
            ref[...] = _dot(hn, w_ref[:, a:b])

    return pl.pallas_call(
        body, name="inproj", grid=(lp // tr,), in_specs=[_rows(tr, D), _full((1, D)), _full(w.shape)],
        out_specs=[_rows(tr, D), _rows(tr, LAT_W), _rows(tr, D_SSM), _rows(tr, D_XBC), _rows(tr, 128)],
        out_shape=[_sds((lp, D), BF16), _sds((lp, LAT_W), F32), _sds((lp, D_SSM), F32), _sds((lp, D_XBC), F32),
                   _sds((lp, 128), F32)],
        compiler_params=_cp("parallel"))(h0, g, w)


def _rope(x, cos, sa, sb):
    return x * cos + pltpu.roll(x, 96, axis=1) * sa + pltpu.roll(x, 32, axis=1) * sb


def _rope_t(g, cos, sa, sb):
    return g * cos + pltpu.roll(g * sa, 32, axis=1) + pltpu.roll(g * sb, 96, axis=1)


def _mla_prep(lat, qg, kvg, wq, wkv, cos, sa, sb):
    lp = lat.shape[0]
    tr = _rt(lp, (320, 128))

    def body(lat_ref, qg_ref, kvg_ref, wq_ref, wkv_ref, cos_ref, sa_ref, sb_ref, q_ref, k_ref, v_ref, ql_ref, kl_ref):
        lat_v = lat_ref[...]
        ql = _rms(lat_v[:, :QR], qg_ref[...]).astype(BF16)
        kl = _rms(lat_v[:, QR:QR + KVR], kvg_ref[...]).astype(BF16)
        ql_ref[...] = ql
        kl_ref[...] = kl
        cos_v, sa_v, sb_v = cos_ref[...], sa_ref[...], sb_ref[...]
        kpe = _rope(lat_v[:, QR + KVR:LAT_W], cos_v, sa_v, sb_v).astype(BF16)
        for h in range(MLA_H):
            q_ref[h, :, 0:DN] = (_dot(ql, wq_ref[:, h * DN:(h + 1) * DN]) * Q_SCALE).astype(BF16)
            qpe = _dot(ql, wq_ref[:, D + h * 128:D + (h + 1) * 128])
            q_ref[h, :, DN:2 * DN] = (_rope(qpe, cos_v, sa_v, sb_v) * Q_SCALE).astype(BF16)
            k_ref[h, :, 0:DN] = _dot(kl, wkv_ref[:, h * DN:(h + 1) * DN]).astype(BF16)
            k_ref[h, :, DN:2 * DN] = kpe
            v_ref[h] = _dot(kl, wkv_ref[:, D + h * DV:D + (h + 1) * DV]).astype(BF16)

    hb = lambda w: pl.BlockSpec((MLA_H, tr, w), lambda i: (0, i, 0))
    return pl.pallas_call(
        body, name="mla_prep", grid=(lp // tr,),
        in_specs=[_rows(tr, LAT_W), _full((1, QR)), _full((1, KVR)), _full(wq.shape), _full(wkv.shape),
                  _rows(tr, 128), _rows(tr, 128), _rows(tr, 128)],
        out_specs=[hb(256), hb(256), hb(128), _rows(tr, QR), _rows(tr, KVR)],
        out_shape=[_sds((MLA_H, lp, 256), BF16), _sds((MLA_H, lp, 256), BF16), _sds((MLA_H, lp, 128), BF16),
                   _sds((lp, QR), BF16), _sds((lp, KVR), BF16)],
        compiler_params=_cp("parallel"))(lat, qg, kvg, wq, wkv, cos, sa, sb)


def _attn_mask(r0, c0, tq, tk, transposed=False):
    if transposed:
        kk = c0 + lax.broadcasted_iota(jnp.int32, (tk, tq), 0)
        qq = r0 + lax.broadcasted_iota(jnp.int32, (tk, tq), 1)
    else:
        qq = r0 + lax.broadcasted_iota(jnp.int32, (tq, tk), 0)
        kk = c0 + lax.broadcasted_iota(jnp.int32, (tq, tk), 1)
    return jnp.logical_and(kk <= qq, kk >= PAD_ROWS)


def _attn_fwd(q, k, v):
    lp = q.shape[1]
    t = _rt(lp, (640, 128))
    nq = lp // t

    hp = 2

    def body(q_ref, k_ref, v_ref, o_ref, lse_ref):
        qi = pl.program_id(1)
        qv = [q_ref[a] for a in range(hp)]

        def tile(kj, carries, masked, live=None):
            kv_rows = pl.ds(pl.multiple_of(kj * t, t), t)
            out = []
            for a in range(hp):
                m, l, acc = carries[a]
                kk = k_ref[a, kv_rows, :]
                vv = v_ref[a, kv_rows, :]
                s = _dot_nt(qv[a], kk)
                if masked:
                    keep = _attn_mask(qi * t, kj * t, t, t)
                    if live is not None:
                        keep = jnp.logical_and(keep, live)
                    s = jnp.where(keep, s, NEG)
                m_new = jnp.maximum(m, jnp.max(s, axis=-1, keepdims=True))
                alpha = jnp.exp2(m - m_new)
                p = jnp.exp2(s - m_new)
                l = alpha * l + jnp.sum(p, axis=-1, keepdims=True)
                acc = alpha * acc + _dot(p.astype(BF16), vv)
                out.append((m_new, l, acc))
            return tuple(out)

        init = tuple((jnp.full((t, 1), NEG, F32), jnp.zeros((t, 1), F32), jnp.zeros((t, DV), F32)) for _ in range(hp))
        carries = tile(0, init, True)
        carries = lax.fori_loop(1, qi, lambda kj, c: tile(kj, c, False), carries)
        carries = tile(qi, carries, True, live=qi > 0)
        for a in range(hp):
            m, l, acc = carries[a]
            o_ref[:, a * DV:(a + 1) * DV] = acc / l
            lse_ref[a] = jnp.broadcast_to(m + jnp.log(l) * LOG2E, (t, 128)).T[:8]

    return pl.pallas_call(
        body, name="attn_fwd", grid=(MLA_H // hp, nq),
        in_specs=[pl.BlockSpec((hp, t, 256), lambda h, i: (h, i, 0)), pl.BlockSpec((hp, lp, 256), lambda h, i: (h, 0, 0)),
                  pl.BlockSpec((hp, lp, 128), lambda h, i: (h, 0, 0))],
        out_specs=[pl.BlockSpec((t, hp * DV), lambda h, i: (i, h)), pl.BlockSpec((hp, 8, t), lambda h, i: (h, 0, i))],
        out_shape=[_sds((lp, MLA_H * DV), F32), _sds((MLA_H, 8, lp), F32)],
        compiler_params=_cp("parallel", "parallel"))(q, k, v)


def _attn_out_bwd(o, g, dmixin):
    lp = o.shape[0]
    tr = _rt(lp, (640, 128))

    def body(o_ref, g_ref, dy_ref, do_ref, dg_ref, dl_ref):
        i = pl.program_id(0)
        ov = o_ref[...]
        do, dg = _rms_bwd(ov, g_ref[...], dy_ref[...])
        do_ref[...] = do
        _acc_rows(dg_ref, dg, i == 0)
        prod = do * ov
        lane = lax.broadcasted_iota(jnp.int32, (1, 128), 1)
        cols = jnp.zeros((tr, 128), F32)
        for h in range(MLA_H):
            cols = cols + jnp.sum(prod[:, h * DV:(h + 1) * DV], axis=-1, keepdims=True) * (lane == h).astype(F32)
        dl_ref[...] = cols.T[:MLA_H]

    return pl.pallas_call(
        body, name="attn_out_bwd", grid=(lp // tr,),
        in_specs=[_rows(tr, D), _full((1, D)), pl.BlockSpec((tr, D), lambda i: (i, 0))],
        out_specs=[_rows(tr, D), _full((1, D)), pl.BlockSpec((MLA_H, tr), lambda i: (0, i))],
        out_shape=[_sds((lp, D), F32), _sds((1, D), F32), _sds((MLA_H, lp), F32)],
        compiler_params=_cp("arbitrary"))(o, g, dmixin)


def _attn_bwd(q, k, v, do, lse_row, delta_row):
    lp = q.shape[1]
    t = _rt(lp, (640, 128))
    nq = lp // t

    def body(q_ref, k_ref, v_ref, do_ref, lse_ref, dl_ref, dq_ref, dk_ref, dv_ref):
        kj = pl.program_id(1)
        kk = k_ref[0]
        vv = v_ref[0]

        @pl.when(kj == 0)
        def _():
            dq_ref[...] = jnp.zeros_like(dq_ref)

        def tile(qi, carry, masked):
            dk, dv = carry
            q_rows = pl.ds(pl.multiple_of(qi * t, t), t)
            qv = q_ref[0, q_rows, :]
            dob = do_ref[q_rows, :].astype(BF16)
            st = _dot_nt(kk, qv)
            if masked:
                st = jnp.where(_attn_mask(qi * t, kj * t, t, t, transposed=True), st, NEG)
            pt = jnp.exp2(st - lse_ref[0, qi])
            dpt = _dot_nt(vv, dob)
            dst = (pt * (dpt - dl_ref[0, qi])).astype(BF16)
            dv = dv + _dot(pt.astype(BF16), dob)
            dk = dk + _dot(dst, qv)
            dq_ref[0, q_rows, :] += _dot_tn(dst, kk)
            return dk, dv

        carry = tile(kj, (jnp.zeros((t, 256), F32), jnp.zeros((t, DV), F32)), True)
        split = jnp.where(kj == 0, nq, kj + 1)
        carry = lax.fori_loop(kj + 1, split, lambda qi, c: tile(qi, c, True), carry)
        dk, dv = lax.fori_loop(split, nq, lambda qi, c: tile(qi, c, False), carry)
        dk_ref[0] = dk * LN2
        dv_ref[0] = dv

    stat = pl.BlockSpec((1, nq, 1, t), lambda h, j: (h, 0, 0, 0))
    return pl.pallas_call(
        body, name="attn_bwd", grid=(MLA_H, nq),
        in_specs=[pl.BlockSpec((1, lp, 256), lambda h, j: (h, 0, 0)), pl.BlockSpec((1, t, 256), lambda h, j: (h, j, 0)),
                  pl.BlockSpec((1, t, 128), lambda h, j: (h, j, 0)), pl.BlockSpec((lp, DV), lambda h, j: (0, h)), stat, stat],
        out_specs=[pl.BlockSpec((1, lp, 256), lambda h, j: (h, 0, 0)), pl.BlockSpec((1, t, 256), lambda h, j: (h, j, 0)),
                   pl.BlockSpec((1, t, 128), lambda h, j: (h, j, 0))],
        out_shape=[_sds((MLA_H, lp, 256), F32), _sds((MLA_H, lp, 256), F32), _sds((MLA_H, lp, 128), F32)],
        compiler_params=_cp("parallel", "arbitrary"))(q, k, v, do, lse_row, delta_row)


def _ssd_consts():
    ri = lax.broadcasted_iota(jnp.int32, (CHUNK, CHUNK), 0)
    ci = lax.broadcasted_iota(jnp.int32, (CHUNK, CHUNK), 1)
    expand = (lax.broadcasted_iota(jnp.int32, (128, D_SSM), 0)
              == lax.broadcasted_iota(jnp.int32, (128, D_SSM), 1) // SSM_P).astype(F32)
    return ri, ci, expand


def _ssd_chunk(c, x_ref, xh_ref, dt_ref, dtT_ref, cw_ref, cb_ref, dtb_ref, dtbT_ref, al_ref, alT_ref):
    ri, ci, expand = _ssd_consts()
    x = x_ref[...]
    halo = jnp.where(c > 0, xh_ref[...], 0.0)
    sh = [x] + [_shift_down(x, halo, j) for j in range(1, SSM_K)]
    cv = cb_ref[...]
    for kk in range(SSM_K):
        cv = cv + cw_ref[kk:kk + 1, :] * sh[SSM_K - 1 - kk]
    xa = _silu(cv)
    grow = c * CHUNK + ri
    gcol = c * CHUNK + lax.broadcasted_iota(jnp.int32, (SSM_H, CHUNK), 1)
    sp = dt_ref[...] + dtb_ref[...]
    spT = dtT_ref[...] + dtbT_ref[...]
    dtc = jnp.where(grow >= PAD_ROWS, _softplus(sp), 0.0)
    dtr = jnp.where(gcol >= PAD_ROWS, _softplus(spT), 0.0)
    arow = -jnp.exp(al_ref[...])
    acolT = -jnp.exp(alT_ref[...])
    ltri = (ci <= ri).astype(F32)
    acs = _dot_hi(ltri, dtc * arow)
    acsT = _dot_hi(dtr * acolT, (ri <= ci).astype(F32))
    return dict(x=x, sh=sh, cv=cv, xa=xa, sp=sp, dtc=dtc, arow=arow, acs=acs, acsT=acsT, ri=ri, ci=ci, expand=expand,
                grow=grow)


def _ssd_mats(k, s_prev):
    xa, acs, acsT, expand, ri, ci = k["xa"], k["acs"], k["acsT"], k["expand"], k["ri"], k["ci"]
    xs = xa[:, :D_SSM]
    dt_e = _dot_hi(k["dtc"], expand)
    acs_e = _dot_hi(acs, expand)
    last_e = acs_e[CHUNK - 1:CHUNK, :]
    ea = jnp.exp(acs_e)
    f = jnp.exp(last_e - acs_e)
    cd = jnp.exp(last_e)
    xdt = xs * dt_e
    bm = [xa[:, D_SSM + g * SSM_N:D_SSM + (g + 1) * SSM_N] for g in range(SSM_G)]
    cm = [xa[:, D_SSM + (SSM_G + g) * SSM_N:D_SSM + (SSM_G + g + 1) * SSM_N] for g in range(SSM_G)]
    bmb = [b.astype(BF16) for b in bm]
    cmb = [cc.astype(BF16) for cc in cm]
    cb = [_dot_nt(cmb[g], bmb[g]) for g in range(SSM_G)]
    lam, mm = [], []
    for h in range(SSM_H):
        diff = acs[:, h:h + 1] - acsT[h:h + 1, :]
        lam_h = jnp.exp(jnp.where(ci <= ri, diff, NEG))
        lam.append(lam_h)
        mm.append(cb[h // (SSM_H // SSM_G)] * lam_h)
    lo = lax.broadcasted_iota(jnp.int32, (CHUNK, 128), 1) < SSM_P
    xdt_h = []
    for h in range(SSM_H):
        pair = xdt[:, (h // 2) * 128:(h // 2 + 1) * 128]
        xdt_h.append(jnp.where(lo if h % 2 == 0 else jnp.logical_not(lo), pair, 0.0).astype(BF16))
    ydiag = jnp.concatenate(
        [_dot(mm[2 * j].astype(BF16), xdt_h[2 * j]) + _dot(mm[2 * j + 1].astype(BF16), xdt_h[2 * j + 1])
         for j in range(SSM_H // 2)], axis=1)
    t_off = [_dot(cmb[g], s_prev[g].astype(BF16)) for g in range(SSM_G)]
    yoff = jnp.concatenate(t_off, axis=1) * ea
    return dict(xs=xs, dt_e=dt_e, acs_e=acs_e, ea=ea, f=f, cd=cd, xdt=xdt, bm=bm, cm=cm, bmb=bmb, cmb=cmb, cb=cb, lam=lam,
                mm=mm, lo=lo, xdt_h=xdt_h, ydiag=ydiag, t_off=t_off, yoff=yoff)


def _ssd_specs(nc, rev):
    ix = (lambda i: nc - 1 - i) if rev else (lambda i: i)
    return [
        pl.BlockSpec((CHUNK, D_XBC), lambda i: (ix(i), 0)),
        pl.BlockSpec((8, D_XBC), lambda i: (jnp.maximum(ix(i) * (CHUNK // 8) - 1, 0), 0)),
        pl.BlockSpec((CHUNK, D_SSM), lambda i: (ix(i), 0)),
        pl.BlockSpec((CHUNK, 128), lambda i: (ix(i), 0)),
        pl.BlockSpec((SSM_H, CHUNK), lambda i: (0, ix(i))),
        _full((8, D_XBC)), _full((1, D_XBC)), _full((1, 128)), _full((SSM_H, 1)), _full((1, 128)), _full((SSM_H, 1)),
        _full((1, D_SSM)), _full((1, D_SSM)),
    ]


def _ssd_fwd(xbc, z, dtr, dtrT, cw, cb, dtb, dtbT, alog, alogT, d_e, ng):
    lp = xbc.shape[0]
    nc = lp // CHUNK

    def body(x_ref, xh_ref, z_ref, dt_ref, dtT_ref, cw_ref, cb_ref, dtb_ref, dtbT_ref, al_ref, alT_ref, de_ref, ng_ref,
             y_ref, st_ref, s_scr):
        c = pl.program_id(0)

        @pl.when(c == 0)
        def _():
            s_scr[...] = jnp.zeros_like(s_scr)

        k = _ssd_chunk(c, x_ref, xh_ref, dt_ref, dtT_ref, cw_ref, cb_ref, dtb_ref, dtbT_ref, al_ref, alT_ref)
        s_prev = [s_scr[g] for g in range(SSM_G)]
        st_ref[0] = s_scr[...]
        m = _ssd_mats(k, s_prev)
        xd = (m["xdt"] * m["f"]).astype(BF16)
        for g in range(SSM_G):
            sl = slice(g * GSZ, (g + 1) * GSZ)
            s_scr[g] = m["cd"][:, sl] * s_prev[g] + _dot(m["bm"][g].T.astype(BF16), xd[:, sl])
        y = m["ydiag"] + m["yoff"] + de_ref[...] * m["xs"]
        u = y * _silu(z_ref[...])
        outs = []
        for g in range(SSM_G):
            ug = u[:, g * GSZ:(g + 1) * GSZ]
            outs.append(ug * lax.rsqrt(jnp.mean(ug * ug, axis=-1, keepdims=True) + EPS))
        y_ref[...] = jnp.concatenate(outs, axis=1) * ng_ref[...]

    return pl.pallas_call(
        body, name="ssd_fwd", grid=(nc,), in_specs=_ssd_specs(nc, False),
        out_specs=[_rows(CHUNK, D_SSM), pl.BlockSpec((1, SSM_G, SSM_N, GSZ), lambda i: (i, 0, 0, 0))],
        out_shape=[_sds((lp, D_SSM), F32), _sds((nc, SSM_G, SSM_N, GSZ), F32)],
        scratch_shapes=[pltpu.VMEM((SSM_G, SSM_N, GSZ), F32)],
        compiler_params=_cp("arbitrary"))(xbc, xbc, z, dtr, dtrT, cw, cb, dtb, dtbT, alog, alogT, d_e, ng)


def _ssd_bwd(dmixin, xbc, z, dtr, dtrT, st, cw, cb, dtb, dtbT, alog, alogT, d_e, ng):
    lp = xbc.shape[0]
    nc = lp // CHUNK
    hpg = SSM_H // SSM_G

    def body(dy_ref, x_ref, xh_ref, z_ref, dt_ref, dtT_ref, st_ref, cw_ref, cb_ref, dtb_ref, dtbT_ref, al_ref, alT_ref,
             de_ref, ng_ref, dz_ref, dx_ref, ddt_ref, dcw_ref, dcb_ref, ddtb_ref, dal_ref, dd_ref, dng_ref, ds_scr, nx_scr):
        i = pl.program_id(0)
        c = nc - 1 - i
        first = i == 0

        @pl.when(first)
        def _():
            ds_scr[...] = jnp.zeros_like(ds_scr)
            nx_scr[...] = jnp.zeros_like(nx_scr)

        k = _ssd_chunk(c, x_ref, xh_ref, dt_ref, dtT_ref, cw_ref, cb_ref, dtb_ref, dtbT_ref, al_ref, alT_ref)
        s_prev = [st_ref[0, g] for g in range(SSM_G)]
        m = _ssd_mats(k, s_prev)
        ri, ci, expand = k["ri"], k["ci"], k["expand"]
        xs, acs, acsT = m["xs"], k["acs"], k["acsT"]
        zv = z_ref[...]
        dout = dy_ref[...]
        ngv = ng_ref[...]
        y = m["ydiag"] + m["yoff"] + de_ref[...] * xs
        sz = _silu(zv)
        u = y * sz
        du_parts, dng_parts = [], []
        for g in range(SSM_G):
            sl = slice(g * GSZ, (g + 1) * GSZ)
            dug, dngg = _rms_bwd(u[:, sl], ngv[:, sl], dout[:, sl])
            du_parts.append(dug)
            dng_parts.append(dngg)
        du = jnp.concatenate(du_parts, axis=1)
        _acc_rows(dng_ref, jnp.concatenate(dng_parts, axis=1), first)
        dy = du * sz
        dz_ref[...] = du * y * _dsilu(zv)
        dd_e = jnp.sum(dy * xs, axis=0, keepdims=True)
        _acc_rows(dd_ref, _dot_nt_hi(dd_e, expand), first)
        dxs = de_ref[...] * dy
        dacs_e = dy * m["yoff"]
        dtg = (dy * m["ea"]).astype(BF16)
        dxdt = jnp.zeros_like(xs)
        dlast_e = []
        db, dc, ds_prev = [], [], []
        xd = m["xdt"] * m["f"]
        dxd_all = []
        for g in range(SSM_G):
            sl = slice(g * GSZ, (g + 1) * GSZ)
            dsg = ds_scr[g]
            spb = s_prev[g].astype(BF16)
            dc.append(_dot_nt(dtg[:, sl], spb))
            dsp = _dot(m["cm"][g].T.astype(BF16), dtg[:, sl]) + m["cd"][:, sl] * dsg
            ds_prev.append(dsp)
            dlast_e.append(jnp.sum(dsg * s_prev[g], axis=0, keepdims=True) * m["cd"][:, sl])
            dsb = dsg.astype(BF16)
            db.append(_dot_nt(xd[:, sl].astype(BF16), dsb))
            dxd_all.append(_dot(m["bmb"][g], dsb))
        dxd = jnp.concatenate(dxd_all, axis=1)
        dxdt = dxd * m["f"]
        dff = dxd * xd
        dacs_e = dacs_e - dff
        dlast_row = jnp.concatenate(dlast_e, axis=1) + jnp.sum(dff, axis=0, keepdims=True)
        dacs = jnp.zeros((CHUNK, 128), F32)
        lane = lax.broadcasted_iota(jnp.int32, (1, 128), 1)
        cbT = [_dot_nt(m["bmb"][g], m["cmb"][g]) for g in range(SSM_G)]
        dgs = [jnp.zeros((CHUNK, CHUNK), F32) for _ in range(SSM_G)]
        dgTs = [jnp.zeros((CHUNK, CHUNK), F32) for _ in range(SSM_G)]
        dxdt_pairs = []
        for h in range(SSM_H):
            g = h // hpg
            pr = slice((h // 2) * 128, (h // 2 + 1) * 128)
            lo_h = m["lo"] if h % 2 == 0 else jnp.logical_not(m["lo"])
            dyp = jnp.where(lo_h, dy[:, pr], 0.0).astype(BF16)
            xdp = m["xdt"][:, pr].astype(BF16)
            dm = _dot_nt(dyp, xdp)
            dmT = _dot_nt(xdp, dyp)
            lamT = jnp.exp(jnp.where(ri <= ci, acsT[h:h + 1, :] - acs[:, h:h + 1], NEG))
            mT = cbT[g] * lamT
            dgs[g] = dgs[g] + dm * m["lam"][h]
            dgTs[g] = dgTs[g] + dmT * lamT
            v1 = jnp.sum(dm * m["mm"][h], axis=1, keepdims=True)
            v2 = jnp.sum(dmT * mT, axis=1, keepdims=True)
            dacs = dacs + (v1 - v2) * (lane == h).astype(F32)
            part = _dot(mT.astype(BF16), dyp)
            if h % 2 == 0:
                dxdt_pairs.append(part)
            else:
                dxdt_pairs[-1] = dxdt_pairs[-1] + part
        dxdt = dxdt + jnp.concatenate(dxdt_pairs, axis=1)
        for g in range(SSM_G):
            dc[g] = dc[g] + _dot(dgs[g].astype(BF16), m["bmb"][g])
            db[g] = db[g] + _dot(dgTs[g].astype(BF16), m["cmb"][g])
        dacs = dacs + _dot_nt_hi(dacs_e, expand)
        dlast = _dot_nt_hi(dlast_row, expand)
        dacs = dacs + jnp.where(ri == CHUNK - 1, dlast, 0.0)
        dxs = dxs + dxdt * m["dt_e"]
        ddt = _dot_nt_hi(dxdt * xs, expand)
        da = _dot_hi((ri <= ci).astype(F32), dacs)
        ddt = ddt + da * k["arow"]
        dA = jnp.sum(da * k["dtc"], axis=0, keepdims=True)
        _acc_rows(dal_ref, dA * k["arow"], first)
        ddtr = jnp.where(k["grow"] >= PAD_ROWS, ddt * _sigmoid(k["sp"]), 0.0)
        ddt_ref[...] = ddtr
        _acc_rows(ddtb_ref, jnp.sum(ddtr, axis=0, keepdims=True), first)
        for g in range(SSM_G):
            ds_scr[g] = ds_prev[g]
        dxa = jnp.concatenate([dxs] + db + dc, axis=1)
        dcv = dxa * _dsilu(k["cv"])
        _acc_rows(dcb_ref, jnp.sum(dcv, axis=0, keepdims=True), first)
        dcw_rows = [jnp.sum(dcv * k["sh"][SSM_K - 1 - kk], axis=0, keepdims=True) for kk in range(SSM_K)]
        dcw_rows.append(jnp.zeros((8 - SSM_K, D_XBC), F32))
        _acc_rows(dcw_ref, jnp.concatenate(dcw_rows, axis=0), first)
        nxt = nx_scr[...]
        dx = cw_ref[SSM_K - 1:SSM_K, :] * dcv
        for j in range(1, SSM_K):
            dx = dx + cw_ref[SSM_K - 1 - j:SSM_K - j, :] * _shift_up(dcv, nxt, j)
        grow_x = c * CHUNK + lax.broadcasted_iota(jnp.int32, (CHUNK, D_XBC), 0)
        dx_ref[...] = jnp.where(grow_x >= PAD_ROWS, dx, 0.0)
        nx_scr[...] = dcv[:8]

    specs = _ssd_specs(nc, True)
    in_specs = [pl.BlockSpec((CHUNK, D_SSM), lambda i: (nc - 1 - i, 1))] + specs[:5] + [
        pl.BlockSpec((1, SSM_G, SSM_N, GSZ), lambda i: (nc - 1 - i, 0, 0, 0))] + specs[5:]
    rv = lambda w: pl.BlockSpec((CHUNK, w), lambda i: (nc - 1 - i, 0))
    return pl.pallas_call(
        body, name="ssd_bwd", grid=(nc,), in_specs=in_specs,
        out_specs=[rv(D_SSM), rv(D_XBC), rv(128), _full((8, D_XBC)), _full((1, D_XBC)), _full((1, 128)), _full((1, 128)),
                   _full((1, 128)), _full((1, D_SSM))],
        out_shape=[_sds((lp, D_SSM), F32), _sds((lp, D_XBC), F32), _sds((lp, 128), F32), _sds((8, D_XBC), F32),
                   _sds((1, D_XBC), F32), _sds((1, 128), F32), _sds((1, 128), F32), _sds((1, 128), F32), _sds((1, D_SSM), F32)],
        scratch_shapes=[pltpu.VMEM((SSM_G, SSM_N, GSZ), F32), pltpu.VMEM((8, D_XBC), F32)],
        compiler_params=_cp("arbitrary"))(dmixin, xbc, xbc, z, dtr, dtrT, st, cw, cb, dtb, dtbT, alog, alogT, d_e, ng)


def _mixout_fwd(o, ssm, h0, g_ao, g_post, w):
    lp = o.shape[0]
    tr = _rt(lp, (320, 128))

    def body(o_ref, s_ref, h_ref, ga_ref, gp_ref, w_ref, mi_ref, mix_ref, h1_ref):
        mixin = jnp.concatenate([_rms(o_ref[...], ga_ref[...]), s_ref[...]], axis=1).astype(BF16)
        mi_ref[...] = mixin
        mix = _dot(mixin, w_ref[...])
        mix_ref[...] = mix
        grow = pl.program_id(0) * tr + lax.broadcasted_iota(jnp.int32, (tr, D), 0)
        h1_ref[...] = h_ref[...] + jnp.where(grow >= PAD_ROWS, _rms(mix, gp_ref[...]), 0.0)

    return pl.pallas_call(
        body, name="mixout_fwd", grid=(lp // tr,),
        in_specs=[_rows(tr, D), _rows(tr, D), _rows(tr, D), _full((1, D)), _full((1, D)), _full(w.shape)],
        out_specs=[_rows(tr, 2 * D), _rows(tr, D), _rows(tr, D)],
        out_shape=[_sds((lp, 2 * D), BF16), _sds((lp, D), F32), _sds((lp, D), F32)],
        compiler_params=_cp("parallel"))(o, ssm, h0, g_ao, g_post, w)


def _ffn_up(h1, g, w):
    lp = h1.shape[0]
    tr = _rt(lp, (320, 128))
    tn = D_FF // 2

    def body(h_ref, g_ref, w_ref, hn_ref, u_ref):
        hn = _rms(h_ref[...], g_ref[...]).astype(BF16)
        hn_ref[...] = hn
        u_ref[...] = _dot(hn, w_ref[...])

    return pl.pallas_call(
        body, name="ffn_up", grid=(lp // tr, 2 * D_FF // tn),
        in_specs=[pl.BlockSpec((tr, D), lambda i, j: (i, 0)), _full((1, D)), pl.BlockSpec((D, tn), lambda i, j: (0, j))],
        out_specs=[pl.BlockSpec((tr, D), lambda i, j: (i, 0)), pl.BlockSpec((tr, tn), lambda i, j: (i, j))],
        out_shape=[_sds((lp, D), BF16), _sds((lp, 2 * D_FF), F32)],
        compiler_params=_cp("parallel", "arbitrary"))(h1, g, w)


FFN_CB = 256


def _ffn_gate(u, cw, cb):
    lp = u.shape[0]
    tr = _rt(lp, (320, 128))

    def body(u_ref, uh_ref, cw_ref, cb_ref, a_ref):
        i = pl.program_id(0)
        for j in range(D_FF // FFN_CB):
            halves = []
            for off in (0, D_FF):
                sl = slice(off + j * FFN_CB, off + (j + 1) * FFN_CB)
                x = u_ref[:, sl]
                halo = jnp.where(i > 0, uh_ref[:, sl], 0.0)
                cv = cb_ref[:, sl] + cw_ref[FFN_K - 1:FFN_K, sl] * x
                for s in range(1, FFN_K):
                    cv = cv + cw_ref[FFN_K - 1 - s:FFN_K - s, sl] * _shift_down(x, halo, s)
                halves.append(cv)
            a_ref[:, j * FFN_CB:(j + 1) * FFN_CB] = (_silu(halves[0]) * halves[1]).astype(BF16)

    return pl.pallas_call(
        body, name="ffn_gate", grid=(lp // tr,),
        in_specs=[_rows(tr, 2 * D_FF), pl.BlockSpec((8, 2 * D_FF), lambda i: (jnp.maximum(i * (tr // 8) - 1, 0), 0)),
                  _full((8, 2 * D_FF)), _full((1, 2 * D_FF))],
        out_specs=_rows(tr, D_FF), out_shape=_sds((lp, D_FF), BF16),
        compiler_params=_cp("parallel"))(u, u, cw, cb)


def _ffn_down(a, w, h1, tgt, g_post):
    lp = a.shape[0]
    tr = _rt(lp, (320, 128))

    def body(a_ref, w_ref, h_ref, t_ref, g_ref, dh2_ref, dd_ref, dg_ref, loss_ref):
        i = pl.program_id(0)
        d = _dot(a_ref[...], w_ref[...])
        gv = g_ref[...]
        h2 = h_ref[...] + _rms(d, gv)
        grow = i * tr + lax.broadcasted_iota(jnp.int32, (tr, D), 0)
        err = jnp.where(grow >= FRONT, h2 - t_ref[...], 0.0)
        dh2 = err * (1.0 / D)
        dh2_ref[...] = dh2
        dd, dg = _rms_bwd(d, gv, dh2)
        dd_ref[...] = dd.astype(BF16)
        _acc_rows(dg_ref, dg, i == 0)
        part = 0.5 * jnp.sum(jnp.sum(err * err, axis=1, keepdims=True), axis=0, keepdims=True) * (1.0 / D)
        _acc_rows(loss_ref, jnp.broadcast_to(part, (8, 128)), i == 0)

    return pl.pallas_call(
        body, name="ffn_down", grid=(lp // tr,),
        in_specs=[_rows(tr, D_FF), _full(w.shape), _rows(tr, D), _rows(tr, D), _full((1, D))],
        out_specs=[_rows(tr, D), _rows(tr, D), _full((1, D)), _full((8, 128))],
        out_shape=[_sds((lp, D), F32), _sds((lp, D), BF16), _sds((1, D), F32), _sds((8, 128), F32)],
        compiler_params=_cp("arbitrary"))(a, w, h1, tgt, g_post)


def _ffn_gate_bwd(u, da, cw, cb):
    lp = u.shape[0]
    tr = _rt(lp, (320, 128))
    n = lp // tr

    def body(u_ref, uh_ref, da_ref, cw_ref, cb_ref, du_ref, dcw_ref, dcb_ref, nx_scr):
        i = pl.program_id(0)
        t = n - 1 - i
        first = i == 0

        @pl.when(first)
        def _():
            nx_scr[...] = jnp.zeros_like(nx_scr)

        grow = t * tr + lax.broadcasted_iota(jnp.int32, (tr, FFN_CB), 0)
        for j in range(D_FF // FFN_CB):
            cvs, shs, sls = [], [], []
            for off in (0, D_FF):
                sl = slice(off + j * FFN_CB, off + (j + 1) * FFN_CB)
                x = u_ref[:, sl]
                halo = jnp.where(t > 0, uh_ref[:, sl], 0.0)
                sh = [x] + [_shift_down(x, halo, s) for s in range(1, FFN_K)]
                cv = cb_ref[:, sl]
                for kk in range(FFN_K):
                    cv = cv + cw_ref[kk:kk + 1, sl] * sh[FFN_K - 1 - kk]
                cvs.append(cv)
                shs.append(sh)
                sls.append(sl)
            dav = da_ref[:, j * FFN_CB:(j + 1) * FFN_CB]
            dcv = (dav * cvs[1] * _dsilu(cvs[0]), dav * _silu(cvs[0]))
            for hf in range(2):
                sl = sls[hf]
                g = dcv[hf]
                rows = [jnp.sum(g * shs[hf][FFN_K - 1 - kk], axis=0, keepdims=True) for kk in range(FFN_K)]
                rows.append(jnp.zeros((8 - FFN_K, FFN_CB), F32))
                upd_w = jnp.concatenate(rows, axis=0)
                upd_b = jnp.sum(g, axis=0, keepdims=True)

                @pl.when(first)
                def _():
                    dcw_ref[:, sl] = upd_w
                    dcb_ref[:, sl] = upd_b

                @pl.when(jnp.logical_not(first))
                def _():
                    dcw_ref[:, sl] += upd_w
                    dcb_ref[:, sl] += upd_b

                nxt = nx_scr[:, sl]
                du = cw_ref[FFN_K - 1:FFN_K, sl] * g
                for s in range(1, FFN_K):
                    du = du + cw_ref[FFN_K - 1 - s:FFN_K - s, sl] * _shift_up(g, nxt, s)
                du_ref[:, sl] = jnp.where(grow >= PAD_ROWS, du, 0.0).astype(BF16)
                nx_scr[:, sl] = g[:8]

    return pl.pallas_call(
        body, name="ffn_gate_bwd", grid=(n,),
        in_specs=[pl.BlockSpec((tr, 2 * D_FF), lambda i: (n - 1 - i, 0)),
                  pl.BlockSpec((8, 2 * D_FF), lambda i: (jnp.maximum((n - 1 - i) * (tr // 8) - 1, 0), 0)),
                  pl.BlockSpec((tr, D_FF), lambda i: (n - 1 - i, 0)), _full((8, 2 * D_FF)), _full((1, 2 * D_FF))],
        out_specs=[pl.BlockSpec((tr, 2 * D_FF), lambda i: (n - 1 - i, 0)), _full((8, 2 * D_FF)), _full((1, 2 * D_FF))],
        out_shape=[_sds((lp, 2 * D_FF), BF16), _sds((8, 2 * D_FF), F32), _sds((1, 2 * D_FF), F32)],
        scratch_shapes=[pltpu.VMEM((8, 2 * D_FF), F32)],
        compiler_params=_cp("arbitrary"))(u, u, da, cw, cb)


def _norm_bwd_res(x, g, dy, res, name, mask_pad=False, out_dtype=F32):
    lp, c = x.shape
    tr = _rt(lp, (640, 128))

    def body(*refs):
        if res is None:
            x_ref, g_ref, dy_ref, o_ref, dg_ref = refs
        else:
            x_ref, g_ref, dy_ref, r_ref, o_ref, dg_ref = refs
        i = pl.program_id(0)
        dyv = dy_ref[...].astype(F32)
        if mask_pad:
            grow = i * tr + lax.broadcasted_iota(jnp.int32, (tr, c), 0)
            dyv = jnp.where(grow >= PAD_ROWS, dyv, 0.0)
        dx, dg = _rms_bwd(x_ref[...].astype(F32), g_ref[...], dyv)
        if res is not None:
            dx = dx + r_ref[...]
        o_ref[...] = dx.astype(out_dtype)
        _acc_rows(dg_ref, dg, i == 0)

    args = [x, g, dy] + ([] if res is None else [res])
    in_specs = [_rows(tr, c), _full((1, c)), pl.BlockSpec((tr, c), lambda i: (i, 0))] + ([] if res is None else [_rows(tr, c)])
    return pl.pallas_call(
        body, name=name, grid=(lp // tr,), in_specs=in_specs, out_specs=[_rows(tr, c), _full((1, c))],
        out_shape=[_sds((lp, c), out_dtype), _sds((1, c), F32)], compiler_params=_cp("arbitrary"))(*args)


def _mla_bwd(dq, dk, dv, lat, qg, kvg, wq, wkv, cos, sa, sb):
    lp = lat.shape[0]
    tr = _rt(lp, (320, 128))

    def body(dq_ref, dk_ref, dv_ref, lat_ref, qg_ref, kvg_ref, wq_ref, wkv_ref, cos_ref, sa_ref, sb_ref,
             dqf_ref, dkvf_ref, dlat_ref, dqg_ref, dkvg_ref):
        i = pl.program_id(0)
        cos_v, sa_v, sb_v = cos_ref[...], sa_ref[...], sb_ref[...]
        dkpe = jnp.zeros((tr, 128), F32)
        for h in range(MLA_H):
            dqh = dq_ref[h] * SOFTMAX_SCALE
            dqf_ref[:, h * DN:(h + 1) * DN] = dqh[:, :DN].astype(BF16)
            dqf_ref[:, D + h * 128:D + (h + 1) * 128] = _rope_t(dqh[:, DN:], cos_v, sa_v, sb_v).astype(BF16)
            dkh = dk_ref[h]
            dkvf_ref[:, h * DN:(h + 1) * DN] = dkh[:, :DN].astype(BF16)
            dkpe = dkpe + dkh[:, DN:]
            dkvf_ref[:, D + h * DV:D + (h + 1) * DV] = dv_ref[h].astype(BF16)
        dql = _dot_nt(dqf_ref[...], wq_ref[...])
        dkl = _dot_nt(dkvf_ref[...], wkv_ref[...])
        lat_v = lat_ref[...]
        dqc, dqg = _rms_bwd(lat_v[:, :QR], qg_ref[...], dql)
        dkc, dkg = _rms_bwd(lat_v[:, QR:QR + KVR], kvg_ref[...], dkl)
        dlat_ref[:, :QR] = dqc
        dlat_ref[:, QR:QR + KVR] = dkc
        dlat_ref[:, QR + KVR:] = _rope_t(dkpe, cos_v, sa_v, sb_v)
        _acc_rows(dqg_ref, dqg, i == 0)
        _acc_rows(dkvg_ref, dkg, i == 0)

    hb = lambda w: pl.BlockSpec((MLA_H, tr, w), lambda i: (0, i, 0))
    return pl.pallas_call(
        body, name="mla_bwd", grid=(lp // tr,),
        in_specs=[hb(256), hb(256), hb(128), _rows(tr, LAT_W), _full((1, QR)), _full((1, KVR)), _full(wq.shape),
                  _full(wkv.shape), _rows(tr, 128), _rows(tr, 128), _rows(tr, 128)],
        out_specs=[_rows(tr, 2 * D), _rows(tr, 2 * D), _rows(tr, LAT_W), _full((1, QR)), _full((1, KVR))],
        out_shape=[_sds((lp, 2 * D), BF16), _sds((lp, 2 * D), BF16), _sds((lp, LAT_W), F32), _sds((1, QR), F32),
                   _sds((1, KVR), F32)],
        compiler_params=_cp("arbitrary"))(dq, dk, dv, lat, qg, kvg, wq, wkv, cos, sa, sb)


def _inproj_bwd(dlat, dz, dxbc, ddt, w, h0, g, dh1):
    lp = h0.shape[0]
    tr = _rt(lp, (320, 128))
    segs = ((0, LAT_W), (LAT_W, LAT_W + D_SSM), (LAT_W + D_SSM, LAT_W + D_SSM + D_XBC), (IN_P - 128, IN_P))

    def body(dl_ref, dz_ref, dx_ref, dt_ref, w_ref, h_ref, g_ref, r_ref, o_ref, dg_ref):
        dhn = jnp.zeros((tr, D), F32)
        for ref, (a, b) in zip((dl_ref, dz_ref, dx_ref, dt_ref), segs):
            dhn = dhn + _dot_nt(ref[...].astype(BF16), w_ref[:, a:b])
        dx, dg = _rms_bwd(h_ref[...], g_ref[...], dhn)
        o_ref[...] = dx + r_ref[...]
        _acc_rows(dg_ref, dg, pl.program_id(0) == 0)

    return pl.pallas_call(
        body, name="inproj_bwd", grid=(lp // tr,),
        in_specs=[_rows(tr, LAT_W), _rows(tr, D_SSM), _rows(tr, D_XBC), _rows(tr, 128), _full(w.shape), _rows(tr, D),
                  _full((1, D)), _rows(tr, D)],
        out_specs=[_rows(tr, D), _full((1, D))], out_shape=[_sds((lp, D), F32), _sds((1, D), F32)],
        compiler_params=_cp("arbitrary"))(dlat, dz, dxbc, ddt, w, h0, g, dh1)


def _rope_tables(lp):
    pos = (jnp.arange(lp, dtype=jnp.int32) - PAD_ROWS).astype(F32)
    inv = ROPE_THETA ** (-jnp.arange(0, DR, 2, dtype=F32) / DR)
    ang = pos[:, None] * inv[None, :]
    cos, sin = jnp.cos(ang), jnp.sin(ang)
    zero = jnp.zeros_like(sin)
    cos128 = jnp.concatenate([cos, cos, cos, cos], axis=1)
    sa128 = jnp.concatenate([-sin, zero, -sin, zero], axis=1)
    sb128 = jnp.concatenate([zero, sin, zero, sin], axis=1)
    return cos128, sa128, sb128


def _pad_rows8(w):
    return jnp.concatenate([w, jnp.zeros((8 - w.shape[0], w.shape[1]), w.dtype)], axis=0)


def _lane_pad(v):
    return jnp.concatenate([v, jnp.zeros((v.shape[0], 128 - v.shape[1]), v.dtype)], axis=1)


def _device_step(x, tgt, meta, p):
    s = x.shape[0]
    lp = s + FRONT
    zpad = jnp.zeros((PAD_ROWS, D), F32)
    h0 = jnp.concatenate([zpad, meta, x], axis=0)
    tgt_p = jnp.concatenate([jnp.zeros((FRONT, D), F32), tgt], axis=0)
    cos, sa, sb = _rope_tables(lp)

    w_in = p["w_in"]
    w_in_p = jnp.concatenate([w_in[:, :QR + KVR + DR], jnp.zeros((D, 64), BF16), w_in[:, QR + KVR + DR:],
                              jnp.zeros((D, 128 - SSM_H), BF16)], axis=1)
    w_uq = p["w_uq"]
    wq_p = jnp.concatenate([w_uq[:, :, :DN].reshape(QR, MLA_H * DN),
                            jnp.concatenate([w_uq[:, :, DN:], jnp.zeros((QR, MLA_H, 128 - DR), BF16)], axis=2).reshape(QR, MLA_H * 128)],
                           axis=1)
    w_ukv = p["w_ukv"]
    wkv_p = jnp.concatenate([w_ukv[:, :, :DN].reshape(KVR, MLA_H * DN), w_ukv[:, :, DN:].reshape(KVR, MLA_H * DV)], axis=1)
    scw = _pad_rows8(p["ssm_conv_w"])
    fcw = _pad_rows8(p["ffn_conv_w"])
    dtb, alog = _lane_pad(p["ssm_dt_bias"]), _lane_pad(p["ssm_A_log"])
    dtbT, alogT = p["ssm_dt_bias"].reshape(SSM_H, 1), p["ssm_A_log"].reshape(SSM_H, 1)
    d_e = jnp.repeat(p["ssm_D"], SSM_P, axis=1)

    hn, lat, z, xbc, dtr = _inproj(h0, p["norm_mix_pre"], w_in_p)
    dtrT = dtr[:, :SSM_H].T
    q, k, v, qlat, kvlat = _mla_prep(lat, p["q_a_norm"], p["kv_a_norm"], wq_p, wkv_p, cos, sa, sb)
    o, lse = _attn_fwd(q, k, v)
    ssm, st = _ssd_fwd(xbc, z, dtr, dtrT, scw, p["ssm_conv_b"], dtb, dtbT, alog, alogT, d_e, p["ssm_norm"])
    mixin, mix, h1 = _mixout_fwd(o, ssm, h0, p["attn_out_norm"], p["norm_mix_post"], p["w_out"])
    hn2, u = _ffn_up(h1, p["norm_ffn_pre"], p["w_up"])
    a = _ffn_gate(u, fcw, p["ffn_conv_b"])
    dh2, dd, g_ffn_post, loss = _ffn_down(a, p["w_down"], h1, tgt_p, p["norm_ffn_post"])

    da = _mm_nt(dd, p["w_down"], "ffn_da")
    g_w_down = _mm_tn(a, dd, "ffn_dw_down", tn=512)
    du, g_fcw, g_fcb = _ffn_gate_bwd(u, da, fcw, p["ffn_conv_b"])
    dhn2 = _mm_nt(du, p["w_up"], "ffn_dhn", tms=(320, 128))
    g_w_up = _mm_tn(hn2, du, "ffn_dw_up", tn=D_FF // 2)
    dh1, g_ffn_pre = _norm_bwd_res(h1, p["norm_ffn_pre"], dhn2, dh2, "ffn_norm_bwd")
    dmix, g_mix_post = _norm_bwd_res(mix, p["norm_mix_post"], dh1, None, "mix_post_bwd", mask_pad=True, out_dtype=BF16)
    dmixin = _mm_nt(dmix, p["w_out"], "mix_dmixin")
    g_w_out = _mm_tn(mixin, dmix, "mix_dw_out", tn=512)
    do, g_ao, delta = _attn_out_bwd(o, p["attn_out_norm"], dmixin)
    t = _rt(lp, (640, 128))
    dq, dk, dv = _attn_bwd(q, k, v, do, lse[:, 0, :].reshape(MLA_H, lp // t, 1, t), delta.reshape(MLA_H, lp // t, 1, t))
    dqf, dkvf, dlat, g_qa, g_kva = _mla_bwd(dq, dk, dv, lat, p["q_a_norm"], p["kv_a_norm"], wq_p, wkv_p, cos, sa, sb)
    g_wq_p = _mm_tn(qlat, dqf, "mla_dw_uq")
    g_wkv_p = _mm_tn(kvlat, dkvf, "mla_dw_ukv")
    dz, dxbc, ddtr, g_scw, g_scb, g_dtb, g_alog, g_dd, g_ssm_norm = _ssd_bwd(
        dmixin, xbc, z, dtr, dtrT, st, scw, p["ssm_conv_b"], dtb, dtbT, alog, alogT, d_e, p["ssm_norm"])
    dh0, g_mix_pre = _inproj_bwd(dlat, dz, dxbc, ddtr, w_in_p, h0, p["norm_mix_pre"], dh1)
    g_in_p = jnp.concatenate([_mm_tn(hn, dlat, "in_dw_lat"), _mm_tn(hn, dz, "in_dw_z"), _mm_tn(hn, dxbc, "in_dw_xbc"),
                              _mm_tn(hn, ddtr, "in_dw_dt")], axis=1)

    g_w_in = jnp.concatenate([g_in_p[:, :QR + KVR + DR], g_in_p[:, LAT_W:LAT_W + D_SSM + D_XBC + SSM_H]], axis=1)
    g_w_uq = jnp.concatenate([g_wq_p[:, :D].reshape(QR, MLA_H, DN), g_wq_p[:, D:].reshape(QR, MLA_H, 128)[:, :, :DR]], axis=2)
    g_w_ukv = jnp.concatenate([g_wkv_p[:, :D].reshape(KVR, MLA_H, DN), g_wkv_p[:, D:].reshape(KVR, MLA_H, DV)], axis=2)
    grads = dict(
        norm_mix_pre=g_mix_pre, norm_mix_post=g_mix_post, norm_ffn_pre=g_ffn_pre, norm_ffn_post=g_ffn_post, w_in=g_w_in,
        q_a_norm=g_qa, w_uq=g_w_uq, kv_a_norm=g_kva, w_ukv=g_w_ukv, attn_out_norm=g_ao, ssm_conv_w=g_scw[:SSM_K],
        ssm_conv_b=g_scb, ssm_dt_bias=g_dtb[:, :SSM_H], ssm_A_log=g_alog[:, :SSM_H], ssm_D=g_dd[:, :SSM_H],
        ssm_norm=g_ssm_norm, w_out=g_w_out, w_up=g_w_up, ffn_conv_w=g_fcw[:FFN_K], ffn_conv_b=g_fcb, w_down=g_w_down)
    return loss, dh0[FRONT:], dh0[PAD_ROWS:FRONT], grads


N_CHIPS = 4
BIG = (("w_in", (D, D_IN // N_CHIPS)), ("w_uq", (QR // N_CHIPS, MLA_H, DN + DR)), ("w_ukv", (KVR // N_CHIPS, MLA_H, DN + DV)),
       ("w_out", (2 * D // N_CHIPS, D)), ("w_up", (D, 2 * D_FF // N_CHIPS)), ("w_down", (D_FF // N_CHIPS, D)))
BIG_AXIS = dict(w_in=1, w_uq=0, w_ukv=0, w_out=0, w_up=1, w_down=0)
SHARD_ELEMS = sum(functools.reduce(lambda a, b: a * b, s) for _, s in BIG)
HALF_ROWS = SHARD_ELEMS // 256
SMALL_SHARDED = (("meta_tokens", (N_META, D // N_CHIPS)), ("ssm_conv_w", (SSM_K, D_XBC // N_CHIPS)),
                 ("ffn_conv_w", (FFN_K, 2 * D_FF // N_CHIPS)))
SMALL_REPL = (("norm_mix_pre", D), ("norm_mix_post", D), ("norm_ffn_pre", D), ("norm_ffn_post", D), ("q_a_norm", QR),
              ("kv_a_norm", KVR), ("attn_out_norm", D), ("ssm_conv_b", D_XBC), ("ssm_dt_bias", SSM_H), ("ssm_A_log", SSM_H),
              ("ssm_D", SSM_H), ("ssm_norm", D_SSM), ("ffn_conv_b", 2 * D_FF))
ANY = pl.BlockSpec(memory_space=pl.ANY)


def _pad128(v):
    n = v.shape[0]
    return jnp.concatenate([v, jnp.zeros(((-n) % 128,), v.dtype)]) if n % 128 else v


def _pack_rows(vs, rows):
    flat = jnp.concatenate([_pad128(v.reshape(-1)) for v in vs])
    flat = jnp.concatenate([flat, jnp.zeros((rows * 128 - flat.shape[0],), flat.dtype)])
    return flat.reshape(rows, 128)


def _unpack_rows(pack, sizes):
    flat = pack.reshape(-1)
    out, off = [], 0
    for n in sizes:
        out.append(flat[off:off + n])
        off += n + (-n) % 128
    return out


def _my_place():
    return lax.axis_index("x"), lax.axis_index("y"), lax.axis_index("c")


def _other_chips(x, y):
    return [(1 - x, y), (x, 1 - y), (1 - x, 1 - y)]


def _remote(src, dst, send, recv, dev):
    return pltpu.make_async_remote_copy(src_ref=src, dst_ref=dst, send_sem=send, recv_sem=recv, device_id=dev,
                                        device_id_type=MESH)


SMALL_AG_ROWS = 80


def _allgather_weights(pack, small):
    def body(pack_ref, small_ref, out_ref, osm_ref, send, recv, lsem):
        x, y, c = _my_place()
        me = 2 * x + y
        chips = _other_chips(x, y)
        loc = [pltpu.make_async_copy(pack_ref, out_ref.at[me], lsem.at[0]),
               pltpu.make_async_copy(small_ref, osm_ref.at[me], lsem.at[1])]
        for cp in loc:
            cp.start()
        sends = []
        for kk, (cx, cy) in enumerate(chips):
            sends.append(_remote(pack_ref.at[c], out_ref.at[me, c], send.at[kk], recv.at[kk], (cx, cy, c)))
            sends.append(_remote(small_ref, osm_ref.at[me], send.at[6 + kk], recv.at[6 + kk], (cx, cy, c)))
        for cp in sends:
            cp.start()
        for kk, (cx, cy) in enumerate(chips):
            src = 2 * cx + cy
            _remote(pack_ref.at[c], out_ref.at[src, c], send.at[kk], recv.at[kk], (cx, cy, c)).wait_recv()
            fwd = _remote(out_ref.at[src, c], out_ref.at[src, c], send.at[3 + kk], recv.at[3 + kk], (x, y, 1 - c))
            fwd.start()
            sends.append(fwd)
        for kk, (cx, cy) in enumerate(chips):
            src = 2 * cx + cy
            _remote(pack_ref.at[c], out_ref.at[src, 1 - c], send.at[3 + kk], recv.at[3 + kk], (x, y, 1 - c)).wait_recv()
            _remote(small_ref, osm_ref.at[src], send.at[6 + kk], recv.at[6 + kk], (cx, cy, c)).wait_recv()
        for cp in sends:
            cp.wait_send()
        for cp in loc:
            cp.wait()

    return pl.pallas_call(
        body, name="allgather_weights", in_specs=[ANY, ANY], out_specs=[ANY, ANY],
        out_shape=[_sds((N_CHIPS, 2, HALF_ROWS, 128), BF16), _sds((N_CHIPS, SMALL_AG_ROWS, 128), F32)],
        scratch_shapes=[pltpu.SemaphoreType.DMA((9,)), pltpu.SemaphoreType.DMA((9,)), pltpu.SemaphoreType.DMA((2,))])(pack, small)


def _swap_sibling(v, name):
    def body(v_ref, o_ref, send, recv):
        x, y, c = _my_place()
        cp = _remote(v_ref, o_ref, send, recv, (x, y, 1 - c))
        cp.start()
        cp.wait()

    return pl.pallas_call(
        body, name=name, in_specs=[ANY], out_specs=ANY, out_shape=_sds(v.shape, v.dtype),
        scratch_shapes=[pltpu.SemaphoreType.DMA, pltpu.SemaphoreType.DMA])(v)


def _share_sibling(half):
    def body(h_ref, o_ref, send, recv, lsem):
        x, y, c = _my_place()
        loc = pltpu.make_async_copy(h_ref, o_ref.at[c], lsem)
        loc.start()
        cp = _remote(h_ref, o_ref.at[c], send, recv, (x, y, 1 - c))
        cp.start()
        cp.wait_send()
        _remote(h_ref, o_ref.at[1 - c], send, recv, (x, y, 1 - c)).wait_recv()
        loc.wait()

    return pl.pallas_call(
        body, name="share_sibling", in_specs=[ANY], out_specs=ANY, out_shape=_sds((2,) + half.shape, half.dtype),
        scratch_shapes=[pltpu.SemaphoreType.DMA, pltpu.SemaphoreType.DMA, pltpu.SemaphoreType.DMA])(half)


def _exchange_chips(p):
    def body(p_ref, o_ref, send, recv):
        x, y, c = _my_place()
        cps = [_remote(p_ref.at[2 * cx + cy], o_ref.at[kk], send.at[kk], recv.at[kk], (cx, cy, c))
               for kk, (cx, cy) in enumerate(_other_chips(x, y))]
        for cp in cps:
            cp.start()
        for cp in cps:
            cp.wait()

    return pl.pallas_call(
        body, name="exchange_chips", in_specs=[ANY], out_specs=ANY, out_shape=_sds((3,) + p.shape[1:], p.dtype),
        scratch_shapes=[pltpu.SemaphoreType.DMA((3,)), pltpu.SemaphoreType.DMA((3,))])(p)


def _add_n(arrs, name):
    r = arrs[0].shape[0]
    tr = _rt(r, (HALF_ROWS // 2, 8))

    def body(*refs):
        acc = refs[0][...]
        for ref in refs[1:-1]:
            acc = acc + ref[...]
        refs[-1][...] = acc

    return pl.pallas_call(
        body, name=name, grid=(r // tr,), in_specs=[_rows(tr, 128)] * len(arrs), out_specs=_rows(tr, 128),
        out_shape=_sds((r, 128), F32), compiler_params=_cp("parallel"))(*arrs)


SMALL_AR_ROWS = 424


def _allreduce_small(v):
    def body(v_ref, o_ref, gath, send, recv):
        x, y, c = _my_place()
        me = 4 * x + 2 * y + c
        gath[me] = v_ref[...]
        cps = []
        for dd in range(1, 8):
            dx, dy, dc = dd >> 2, (dd >> 1) & 1, dd & 1
            peer = (1 - x if dx else x, 1 - y if dy else y, 1 - c if dc else c)
            cps.append(_remote(v_ref, gath.at[me], send.at[dd - 1], recv.at[dd - 1], peer))
        for cp in cps:
            cp.start()
        for cp in cps:
            cp.wait()
        acc = gath[0]
        for dev in range(1, 8):
            acc = acc + gath[dev]
        o_ref[...] = acc

    vm = pl.BlockSpec(memory_space=pltpu.VMEM)
    return pl.pallas_call(
        body, name="allreduce_small", in_specs=[vm], out_specs=vm, out_shape=_sds(v.shape, F32),
        scratch_shapes=[pltpu.VMEM((8,) + v.shape, F32), pltpu.SemaphoreType.DMA((7,)), pltpu.SemaphoreType.DMA((7,))])(v)


def _adamw(w, g, m, v, name):
    r, c = w.shape
    tr = next(t for t in range(r, 0, -1) if r % t == 0 and (t % 8 == 0 or t == r) and t * c * 4 <= (1 << 20))

    def body(w_ref, g_ref, m_ref, v_ref, d_ref, m2_ref, v2_ref):
        gv = g_ref[...]
        m2 = ADAM_B1 * m_ref[...] + (1.0 - ADAM_B1) * gv
        v2 = ADAM_B2 * v_ref[...] + (1.0 - ADAM_B2) * jnp.square(gv)
        m_hat = m2 / (1.0 - ADAM_B1 ** ADAM_STEP)
        v_hat = v2 / (1.0 - ADAM_B2 ** ADAM_STEP)
        d_ref[...] = -ADAM_LR * (m_hat / (jnp.sqrt(v_hat) + ADAM_EPS) + ADAM_WD * w_ref[...])
        m2_ref[...] = m2
        v2_ref[...] = v2

    return pl.pallas_call(
        body, name=name, grid=(r // tr,), in_specs=[_rows(tr, c)] * 4, out_specs=[_rows(tr, c)] * 3,
        out_shape=[_sds((r, c), F32)] * 3, compiler_params=_cp("parallel"))(w, g, m, v)


WEIGHT_NAMES = ("meta_tokens", "norm_mix_pre", "norm_mix_post", "norm_ffn_pre", "norm_ffn_post", "w_in", "q_a_norm", "w_uq",
                "kv_a_norm", "w_ukv", "attn_out_norm", "ssm_conv_w", "ssm_conv_b", "ssm_dt_bias", "ssm_A_log", "ssm_D",
                "ssm_norm", "w_out", "w_up", "ffn_conv_w", "ffn_conv_b", "w_down")
SMALL_ADAM_ROWS = 192


def kernel(x, meta_tokens, norm_mix_pre, norm_mix_post, norm_ffn_pre, norm_ffn_post, w_in, q_a_norm, w_uq, kv_a_norm, w_ukv, attn_out_norm, ssm_conv_w, ssm_conv_b, ssm_dt_bias, ssm_A_log, ssm_D, ssm_norm, w_out, w_up, ffn_conv_w, ffn_conv_b, w_down, loss_target, m_meta_tokens, m_norm_mix_pre, m_norm_mix_post, m_norm_ffn_pre, m_norm_ffn_post, m_w_in, m_q_a_norm, m_w_uq, m_kv_a_norm, m_w_ukv, m_attn_out_norm, m_ssm_conv_w, m_ssm_conv_b, m_ssm_dt_bias, m_ssm_A_log, m_ssm_D, m_ssm_norm, m_w_out, m_w_up, m_ffn_conv_w, m_ffn_conv_b, m_w_down, v_meta_tokens, v_norm_mix_pre, v_norm_mix_post, v_norm_ffn_pre, v_norm_ffn_post, v_w_in, v_q_a_norm, v_w_uq, v_kv_a_norm, v_w_ukv, v_attn_out_norm, v_ssm_conv_w, v_ssm_conv_b, v_ssm_dt_bias, v_ssm_A_log, v_ssm_D, v_ssm_norm, v_w_out, v_w_up, v_ffn_conv_w, v_ffn_conv_b, v_w_down):
    args = locals()
    w = {n: args[n] for n in WEIGHT_NAMES}
    mom = {n: args["m_" + n] for n in WEIGHT_NAMES}
    var = {n: args["v_" + n] for n in WEIGHT_NAMES}
    cx, cy, cc = _my_place()
    chip = 2 * cx + cy

    pack = jnp.concatenate([w[n].reshape(-1) for n, _ in BIG]).astype(BF16).reshape(2, HALF_ROWS, 128)
    small = _pack_rows([w[n] for n, _ in SMALL_SHARDED], SMALL_AG_ROWS)
    big_all, small_all = _allgather_weights(pack, small)
    big_flat = big_all.reshape(N_CHIPS, SHARD_ELEMS)
    p, off = {}, 0
    for n, shp in BIG:
        size = functools.reduce(lambda a, b: a * b, shp)
        parts = [big_flat[j, off:off + size].reshape(shp) for j in range(N_CHIPS)]
        p[n] = jnp.concatenate(parts, axis=BIG_AXIS[n])
        off += size
    sm_parts = [_unpack_rows(small_all[j], [a * b for _, (a, b) in SMALL_SHARDED]) for j in range(N_CHIPS)]
    for i, (n, shp) in enumerate(SMALL_SHARDED):
        p[n] = jnp.concatenate([sm_parts[j][i].reshape(shp) for j in range(N_CHIPS)], axis=1)
    for n, _ in SMALL_REPL:
        p[n] = w[n]
    meta_full = p.pop("meta_tokens")

    loss_part, gx, gmeta, g = _device_step(x[0], loss_target[0], meta_full, p)

    small_names = [n for n, _ in SMALL_REPL] + ["ssm_conv_w", "ffn_conv_w"]
    small_sizes = [128] + [sz for _, sz in SMALL_REPL] + [N_META * D, SSM_K * D_XBC, FFN_K * 2 * D_FF]
    order = [n for n, _ in SMALL_REPL]
    sp = _pack_rows([loss_part[0]] + [g[n] for n in order] + [gmeta, g["ssm_conv_w"], g["ffn_conv_w"]], SMALL_AR_ROWS)
    red = _unpack_rows(_allreduce_small(sp), small_sizes)
    loss = red[0][0]
    gfull = {n: red[1 + i].reshape(1, -1) for i, n in enumerate(order)}
    n_r = len(order)
    gfull["meta_tokens"] = lax.dynamic_slice_in_dim(red[1 + n_r].reshape(N_META, D), chip * (D // N_CHIPS), D // N_CHIPS, axis=1)
    gfull["ssm_conv_w"] = lax.dynamic_slice_in_dim(red[2 + n_r].reshape(SSM_K, D_XBC), chip * (D_XBC // N_CHIPS),
                                                   D_XBC // N_CHIPS, axis=1)[None]
    gfull["ffn_conv_w"] = lax.dynamic_slice_in_dim(red[3 + n_r].reshape(FFN_K, 2 * D_FF), chip * (2 * D_FF // N_CHIPS),
                                                   2 * D_FF // N_CHIPS, axis=1)[None]

    shards = []
    for j in range(N_CHIPS):
        pieces = []
        for n, shp in BIG:
            ax = BIG_AXIS[n]
            pieces.append(lax.slice_in_dim(g[n], j * shp[ax], (j + 1) * shp[ax], axis=ax).reshape(-1))
        shards.append(jnp.concatenate(pieces).reshape(2, HALF_ROWS, 128))
    gp = jnp.stack(shards, axis=1)
    mine = lax.dynamic_index_in_dim(gp, cc, axis=0, keepdims=False)
    theirs = lax.dynamic_index_in_dim(gp, 1 - cc, axis=0, keepdims=False)
    from_sib = _swap_sibling(theirs, "reduce_sibling")
    pair = _add_n([mine.reshape(-1, 128), from_sib.reshape(-1, 128)], "reduce_pair").reshape(N_CHIPS, HALF_ROWS, 128)
    got = _exchange_chips(pair)
    own = lax.dynamic_index_in_dim(pair, chip, axis=0, keepdims=False)
    half = _add_n([own, got[0], got[1], got[2]], "reduce_chips")
    flat = _share_sibling(half).reshape(SHARD_ELEMS)
    off = 0
    for n, shp in BIG:
        size = functools.reduce(lambda a, b: a * b, shp)
        gfull[n] = flat[off:off + size].reshape((1,) + shp)
        off += size

    delta, new_m, new_v = {}, {}, {}
    for n, shp in BIG:
        two_d = (shp[0], functools.reduce(lambda a, b: a * b, shp[1:]))
        outs = _adamw(w[n].reshape(two_d), gfull[n].reshape(two_d), mom[n].reshape(two_d), var[n].reshape(two_d), "adamw_" + n)
        delta[n], new_m[n], new_v[n] = (o.reshape((1,) + shp) for o in outs)
    snames = order + ["meta_tokens", "ssm_conv_w", "ffn_conv_w"]
    ssizes = [functools.reduce(lambda a, b: a * b, w[n].shape) for n in snames]
    packs = [_pack_rows([d[n] for n in snames], SMALL_ADAM_ROWS) for d in (w, gfull, mom, var)]
    outs = _adamw(*packs, "adamw_small")
    for d, o in zip((delta, new_m, new_v), outs):
        for n, piece in zip(snames, _unpack_rows(o, ssizes)):
            d[n] = piece.reshape(w[n].shape)
    gout = {n: gfull[n].reshape(w[n].shape) for n in WEIGHT_NAMES}
    return (loss, gx[None], *[gout[n] for n in WEIGHT_NAMES], *[delta[n] for n in WEIGHT_NAMES],
            *[new_m[n] for n in WEIGHT_NAMES], *[new_v[n] for n in WEIGHT_NAMES])
```

```python
import functools

import jax
import jax.numpy as jnp
from jax import lax
from jax.experimental import pallas as pl
from jax.experimental.pallas import tpu as pltpu

F32 = jnp.float32
BF16 = jnp.bfloat16
HI = lax.Precision.HIGHEST

D = 1024
N_META = 16
FRONT = 128
PAD_ROWS = FRONT - N_META
MLA_H = 8
DN, DR, DV = 128, 64, 128
QR, KVR = 384, 256
SOFTMAX_SCALE = (DN + DR) ** -0.5
ROPE_THETA = 10000.0
SSM_H, SSM_P, SSM_G, SSM_N, SSM_K = 16, 64, 2, 128, 4
CHUNK = 128
D_SSM = SSM_H * SSM_P
D_XBC = D_SSM + 2 * SSM_G * SSM_N
GSZ = D_SSM // SSM_G
D_FF = 2816
FFN_K = 3
EPS = 1e-6
IN_SPLITS = (QR, KVR, DR, D_SSM, D_XBC, SSM_H)
D_IN = sum(IN_SPLITS)
LAT_W = 768
IN_P = LAT_W + D_SSM + D_XBC + 128
NEG = -1e30
LOG2E = 1.4426950408889634
LN2 = 0.6931471805599453
Q_SCALE = SOFTMAX_SCALE * LOG2E

ADAM_LR, ADAM_B1, ADAM_B2, ADAM_EPS, ADAM_WD, ADAM_STEP = 0.001, 0.9, 0.999, 1e-08, 0.01, 10

VMEM_LIMIT = 56 * 1024 * 1024
MESH = pl.DeviceIdType.MESH


def _sds(shape, dtype):
    return jax.ShapeDtypeStruct(shape, dtype)


def _cp(*sem):
    return pltpu.CompilerParams(dimension_semantics=sem, vmem_limit_bytes=VMEM_LIMIT)


def _rt(n, cands):
    for c in cands:
        if n % c == 0:
            return c
    raise ValueError((n, cands))


def _full(shape):
    nd = len(shape)
    return pl.BlockSpec(shape, lambda *_: (0,) * nd)


def _rows(tr, c):
    return pl.BlockSpec((tr, c), lambda i: (i, 0))


def _sigmoid(x):
    return 1.0 / (1.0 + jnp.exp(-x))


def _silu(x):
    return x * _sigmoid(x)


def _dsilu(x):
    s = _sigmoid(x)
    return s * (1.0 + x * (1.0 - s))


def _softplus(x):
    return jnp.maximum(x, 0.0) + jnp.log(1.0 + jnp.exp(-jnp.abs(x)))


def _rms(x, g):
    r = lax.rsqrt(jnp.mean(x * x, axis=-1, keepdims=True) + EPS)
    return x * r * g


def _rms_bwd(x, g, dy):
    r = lax.rsqrt(jnp.mean(x * x, axis=-1, keepdims=True) + EPS)
    xh = x * r
    dxh = dy * g
    dx = r * (dxh - xh * jnp.mean(dxh * xh, axis=-1, keepdims=True))
    return dx, jnp.sum(dy * xh, axis=0, keepdims=True)


def _dot(a, b):
    return jnp.dot(a, b, preferred_element_type=F32)


def _dot_nt(a, b):
    return lax.dot_general(a, b, (((1,), (1,)), ((), ())), preferred_element_type=F32)


def _dot_tn(a, b):
    return lax.dot_general(a, b, (((0,), (0,)), ((), ())), preferred_element_type=F32)


def _dot_hi(a, b):
    return jnp.dot(a, b, precision=HI, preferred_element_type=F32)


def _dot_nt_hi(a, b):
    return lax.dot_general(a, b, (((1,), (1,)), ((), ())), precision=HI, preferred_element_type=F32)


def _shift_down(x, halo, j):
    xr = pltpu.roll(x, j, axis=0)
    hr = pltpu.roll(halo, j, axis=0)
    row = lax.broadcasted_iota(jnp.int32, (8, x.shape[1]), 0)
    first = jnp.where(row < j, hr, xr[:8])
    return jnp.concatenate([first, xr[8:]], axis=0)


def _shift_up(x, nxt, j):
    t = x.shape[0]
    xr = pltpu.roll(x, t - j, axis=0)
    nr = pltpu.roll(nxt, 8 - j, axis=0)
    row = lax.broadcasted_iota(jnp.int32, (8, x.shape[1]), 0)
    last = jnp.where(row + j >= 8, nr, xr[t - 8:])
    return jnp.concatenate([xr[:t - 8], last], axis=0)


def _acc_rows(ref, val, first):
    @pl.when(first)
    def _():
        ref[...] = val

    @pl.when(jnp.logical_not(first))
    def _():
        ref[...] += val


def _mm_nt(a, b, name, out_dtype=F32, tms=(640, 320, 128)):
    m, k = a.shape
    n = b.shape[0]
    tm = _rt(m, tms)

    def body(a_ref, b_ref, o_ref):
        o_ref[...] = _dot_nt(a_ref[...].astype(BF16), b_ref[...]).astype(out_dtype)

    return pl.pallas_call(
        body, name=name, grid=(m // tm,), in_specs=[_rows(tm, k), _full(b.shape)], out_specs=_rows(tm, n),
        out_shape=_sds((m, n), out_dtype), compiler_params=_cp("parallel"))(a, b)


def _mm_tn(a, b, name, tn=None, trs=(640, 128), chunked=False):
    r, m = a.shape
    n = b.shape[1]
    tn = n if tn is None else tn
    tr = _rt(r, trs)

    def body(a_ref, b_ref, o_ref):
        part = _dot_tn(a_ref[...].astype(BF16), b_ref[...].astype(BF16))
        _acc_rows(o_ref, part, pl.program_id(1) == 0)

    if chunked:
        out_specs, out_shape = pl.BlockSpec((None, m, tn), lambda j, i: (j, 0, 0)), _sds((n // tn, m, tn), F32)
    else:
        out_specs, out_shape = pl.BlockSpec((m, tn), lambda j, i: (0, j)), _sds((m, n), F32)
    return pl.pallas_call(
        body, name=name, grid=(n // tn, r // tr),
        in_specs=[pl.BlockSpec((tr, m), lambda j, i: (i, 0)), pl.BlockSpec((tr, tn), lambda j, i: (i, j))],
        out_specs=out_specs, out_shape=out_shape, compiler_params=_cp("parallel", "arbitrary"))(a, b)


def _inproj(h0, g, w):
    lp = h0.shape[0]
    tr = _rt(lp, (320, 128))
    segs = ((0, LAT_W), (LAT_W, LAT_W + D_SSM), (LAT_W + D_SSM, LAT_W + D_SSM + D_XBC), (IN_P - 128, IN_P))

    def body(h_ref, g_ref, w_ref, hn_ref, lat_ref, z_ref, xbc_ref, dt_ref):
        hn = _rms(h_ref[...], g_ref[...]).astype(BF16)
        hn_ref[...] = hn
        for ref, (a, b) in zip((lat_ref, z_ref, xbc_ref, dt_ref), segs):
            ref[...] = _dot(hn, w_ref[:, a:b])

    return pl.pallas_call(
        body, name="inproj", grid=(lp // tr,), in_specs=[_rows(tr, D), _full((1, D)), _full(w.shape)],
        out_specs=[_rows(tr, D), _rows(tr, LAT_W), _rows(tr, D_SSM), _rows(tr, D_XBC), _rows(tr, 128)],
        out_shape=[_sds((lp, D), BF16), _sds((lp, LAT_W), F32), _sds((lp, D_SSM), F32), _sds((lp, D_XBC), F32),
                   _sds((lp, 128), F32)],
        compiler_params=_cp("parallel"))(h0, g, w)


def _rope(x, cos, sa, sb):
    return x * cos + pltpu.roll(x, 96, axis=1) * sa + pltpu.roll(x, 32, axis=1) * sb


def _rope_t(g, cos, sa, sb):
    return g * cos + pltpu.roll(g * sa, 32, axis=1) + pltpu.roll(g * sb, 96, axis=1)


def _mla_prep(lat, qg, kvg, wq, wkv, cos, sa, sb):
    lp = lat.shape[0]
    tr = _rt(lp, (320, 128))

    def body(lat_ref, qg_ref, kvg_ref, wq_ref, wkv_ref, cos_ref, sa_ref, sb_ref, q_ref, k_ref, v_ref, ql_ref, kl_ref):
        lat_v = lat_ref[...]
        ql = _rms(lat_v[:, :QR], qg_ref[...]).astype(BF16)
        kl = _rms(lat_v[:, QR:QR + KVR], kvg_ref[...]).astype(BF16)
        ql_ref[...] = ql
        kl_ref[...] = kl
        cos_v, sa_v, sb_v = cos_ref[...], sa_ref[...], sb_ref[...]
        kpe = _rope(lat_v[:, QR + KVR:LAT_W], cos_v, sa_v, sb_v).astype(BF16)
        for h in range(MLA_H):
            q_ref[h, :, 0:DN] = (_dot(ql, wq_ref[:, h * DN:(h + 1) * DN]) * Q_SCALE).astype(BF16)
            qpe = _dot(ql, wq_ref[:, D + h * 128:D + (h + 1) * 128])
            q_ref[h, :, DN:2 * DN] = (_rope(qpe, cos_v, sa_v, sb_v) * Q_SCALE).astype(BF16)
            k_ref[h, :, 0:DN] = _dot(kl, wkv_ref[:, h * DN:(h + 1) * DN]).astype(BF16)
            k_ref[h, :, DN:2 * DN] = kpe
            v_ref[h] = _dot(kl, wkv_ref[:, D + h * DV:D + (h + 1) * DV]).astype(BF16)

    hb = lambda w: pl.BlockSpec((MLA_H, tr, w), lambda i: (0, i, 0))
    return pl.pallas_call(
        body, name="mla_prep", grid=(lp // tr,),
        in_specs=[_rows(tr, LAT_W), _full((1, QR)), _full((1, KVR)), _full(wq.shape), _full(wkv.shape),
                  _rows(tr, 128), _rows(tr, 128), _rows(tr, 128)],
        out_specs=[hb(256), hb(256), hb(128), _rows(tr, QR), _rows(tr, KVR)],
        out_shape=[_sds((MLA_H, lp, 256), BF16), _sds((MLA_H, lp, 256), BF16), _sds((MLA_H, lp, 128), BF16),
                   _sds((lp, QR), BF16), _sds((lp, KVR), BF16)],
        compiler_params=_cp("parallel"))(lat, qg, kvg, wq, wkv, cos, sa, sb)


def _attn_mask(r0, c0, tq, tk, transposed=False):
    if transposed:
        kk = c0 + lax.broadcasted_iota(jnp.int32, (tk, tq), 0)
        qq = r0 + lax.broadcasted_iota(jnp.int32, (tk, tq), 1)
    else:
        qq = r0 + lax.broadcasted_iota(jnp.int32, (tq, tk), 0)
        kk = c0 + lax.broadcasted_iota(jnp.int32, (tq, tk), 1)
    return jnp.logical_and(kk <= qq, kk >= PAD_ROWS)


def _attn_fwd(q, k, v):
    lp = q.shape[1]
    t = _rt(lp, (640, 128))
    nq = lp // t

    hp = 2

    def body(q_ref, k_ref, v_ref, o_ref, lse_ref):
        qi = pl.program_id(1)
        qv = [q_ref[a] for a in range(hp)]

        def tile(kj, carries, masked, live=None):
            kv_rows = pl.ds(pl.multiple_of(kj * t, t), t)
            out = []
            for a in range(hp):
                m, l, acc = carries[a]
                kk = k_ref[a, kv_rows, :]
                vv = v_ref[a, kv_rows, :]
                s = _dot_nt(qv[a], kk)
                if masked:
                    keep = _attn_mask(qi * t, kj * t, t, t)
                    if live is not None:
                        keep = jnp.logical_and(keep, live)
                    s = jnp.where(keep, s, NEG)
                m_new = jnp.maximum(m, jnp.max(s, axis=-1, keepdims=True))
                alpha = jnp.exp2(m - m_new)
                p = jnp.exp2(s - m_new)
                l = alpha * l + jnp.sum(p, axis=-1, keepdims=True)
                acc = alpha * acc + _dot(p.astype(BF16), vv)
                out.append((m_new, l, acc))
            return tuple(out)

        init = tuple((jnp.full((t, 1), NEG, F32), jnp.zeros((t, 1), F32), jnp.zeros((t, DV), F32)) for _ in range(hp))
        carries = tile(0, init, True)
        carries = lax.fori_loop(1, qi, lambda kj, c: tile(kj, c, False), carries)
        carries = tile(qi, carries, True, live=qi > 0)
        for a in range(hp):
            m, l, acc = carries[a]
            o_ref[:, a * DV:(a + 1) * DV] = acc / l
            lse_ref[a] = jnp.broadcast_to(m + jnp.log(l) * LOG2E, (t, 128)).T[:8]

    return pl.pallas_call(
        body, name="attn_fwd", grid=(MLA_H // hp, nq),
        in_specs=[pl.BlockSpec((hp, t, 256), lambda h, i: (h, i, 0)), pl.BlockSpec((hp, lp, 256), lambda h, i: (h, 0, 0)),
                  pl.BlockSpec((hp, lp, 128), lambda h, i: (h, 0, 0))],
        out_specs=[pl.BlockSpec((t, hp * DV), lambda h, i: (i, h)), pl.BlockSpec((hp, 8, t), lambda h, i: (h, 0, i))],
        out_shape=[_sds((lp, MLA_H * DV), F32), _sds((MLA_H, 8, lp), F32)],
        compiler_params=_cp("parallel", "parallel"))(q, k, v)


def _attn_out_bwd(o, g, dmixin):
    lp = o.shape[0]
    tr = _rt(lp, (640, 128))

    def body(o_ref, g_ref, dy_ref, do_ref, dg_ref, dl_ref):
        i = pl.program_id(0)
        ov = o_ref[...]
        do, dg = _rms_bwd(ov, g_ref[...], dy_ref[...])
        do_ref[...] = do
        _acc_rows(dg_ref, dg, i == 0)
        prod = do * ov
        lane = lax.broadcasted_iota(jnp.int32, (1, 128), 1)
        cols = jnp.zeros((tr, 128), F32)
        for h in range(MLA_H):
            cols = cols + jnp.sum(prod[:, h * DV:(h + 1) * DV], axis=-1, keepdims=True) * (lane == h).astype(F32)
        dl_ref[...] = cols.T[:MLA_H]

    return pl.pallas_call(
        body, name="attn_out_bwd", grid=(lp // tr,),
        in_specs=[_rows(tr, D), _full((1, D)), pl.BlockSpec((tr, D), lambda i: (i, 0))],
        out_specs=[_rows(tr, D), _full((1, D)), pl.BlockSpec((MLA_H, tr), lambda i: (0, i))],
        out_shape=[_sds((lp, D), F32), _sds((1, D), F32), _sds((MLA_H, lp), F32)],
        compiler_params=_cp("arbitrary"))(o, g, dmixin)


def _attn_bwd(q, k, v, do, lse_row, delta_row):
    lp = q.shape[1]
    t = _rt(lp, (640, 128))
    nq = lp // t

    def body(q_ref, k_ref, v_ref, do_ref, lse_ref, dl_ref, dq_ref, dk_ref, dv_ref):
        kj = pl.program_id(1)
        kk = k_ref[0]
        vv = v_ref[0]

        @pl.when(kj == 0)
        def _():
            dq_ref[...] = jnp.zeros_like(dq_ref)

        def tile(qi, carry, masked):
            dk, dv = carry
            q_rows = pl.ds(pl.multiple_of(qi * t, t), t)
            qv = q_ref[0, q_rows, :]
            dob = do_ref[q_rows, :].astype(BF16)
            st = _dot_nt(kk, qv)
            if masked:
                st = jnp.where(_attn_mask(qi * t, kj * t, t, t, transposed=True), st, NEG)
            pt = jnp.exp2(st - lse_ref[0, qi])
            dpt = _dot_nt(vv, dob)
            dst = (pt * (dpt - dl_ref[0, qi])).astype(BF16)
            dv = dv + _dot(pt.astype(BF16), dob)
            dk = dk + _dot(dst, qv)
            dq_ref[0, q_rows, :] += _dot_tn(dst, kk)
            return dk, dv

        carry = tile(kj, (jnp.zeros((t, 256), F32), jnp.zeros((t, DV), F32)), True)
        split = jnp.where(kj == 0, nq, kj + 1)
        carry = lax.fori_loop(kj + 1, split, lambda qi, c: tile(qi, c, True), carry)
        dk, dv = lax.fori_loop(split, nq, lambda qi, c: tile(qi, c, False), carry)
        dk_ref[0] = dk * LN2
        dv_ref[0] = dv

    stat = pl.BlockSpec((1, nq, 1, t), lambda h, j: (h, 0, 0, 0))
    return pl.pallas_call(
        body, name="attn_bwd", grid=(MLA_H, nq),
        in_specs=[pl.BlockSpec((1, lp, 256), lambda h, j: (h, 0, 0)), pl.BlockSpec((1, t, 256), lambda h, j: (h, j, 0)),
                  pl.BlockSpec((1, t, 128), lambda h, j: (h, j, 0)), pl.BlockSpec((lp, DV), lambda h, j: (0, h)), stat, stat],
        out_specs=[pl.BlockSpec((1, lp, 256), lambda h, j: (h, 0, 0)), pl.BlockSpec((1, t, 256), lambda h, j: (h, j, 0)),
                   pl.BlockSpec((1, t, 128), lambda h, j: (h, j, 0))],
        out_shape=[_sds((MLA_H, lp, 256), F32), _sds((MLA_H, lp, 256), F32), _sds((MLA_H, lp, 128), F32)],
        compiler_params=_cp("parallel", "arbitrary"))(q, k, v, do, lse_row, delta_row)


def _ssd_consts():
    ri = lax.broadcasted_iota(jnp.int32, (CHUNK, CHUNK), 0)
    ci = lax.broadcasted_iota(jnp.int32, (CHUNK, CHUNK), 1)
    expand = (lax.broadcasted_iota(jnp.int32, (128, D_SSM), 0)
              == lax.broadcasted_iota(jnp.int32, (128, D_SSM), 1) // SSM_P).astype(F32)
    return ri, ci, expand


def _ssd_chunk(c, x_ref, xh_ref, dt_ref, dtT_ref, cw_ref, cb_ref, dtb_ref, dtbT_ref, al_ref, alT_ref):
    ri, ci, expand = _ssd_consts()
    x = x_ref[...]
    halo = jnp.where(c > 0, xh_ref[...], 0.0)
    sh = [x] + [_shift_down(x, halo, j) for j in range(1, SSM_K)]
    cv = cb_ref[...]
    for kk in range(SSM_K):
        cv = cv + cw_ref[kk:kk + 1, :] * sh[SSM_K - 1 - kk]
    xa = _silu(cv)
    grow = c * CHUNK + ri
    gcol = c * CHUNK + lax.broadcasted_iota(jnp.int32, (SSM_H, CHUNK), 1)
    sp = dt_ref[...] + dtb_ref[...]
    spT = dtT_ref[...] + dtbT_ref[...]
    dtc = jnp.where(grow >= PAD_ROWS, _softplus(sp), 0.0)
    dtr = jnp.where(gcol >= PAD_ROWS, _softplus(spT), 0.0)
    arow = -jnp.exp(al_ref[...])
    acolT = -jnp.exp(alT_ref[...])
    ltri = (ci <= ri).astype(F32)
    acs = _dot_hi(ltri, dtc * arow)
    acsT = _dot_hi(dtr * acolT, (ri <= ci).astype(F32))
    return dict(x=x, sh=sh, cv=cv, xa=xa, sp=sp, dtc=dtc, arow=arow, acs=acs, acsT=acsT, ri=ri, ci=ci, expand=expand,
                grow=grow)


def _ssd_mats(k, s_prev):
    xa, acs, acsT, expand, ri, ci = k["xa"], k["acs"], k["acsT"], k["expand"], k["ri"], k["ci"]
    xs = xa[:, :D_SSM]
    dt_e = _dot_hi(k["dtc"], expand)
    acs_e = _dot_hi(acs, expand)
    last_e = acs_e[CHUNK - 1:CHUNK, :]
    ea = jnp.exp(acs_e)
    f = jnp.exp(last_e - acs_e)
    cd = jnp.exp(last_e)
    xdt = xs * dt_e
    bm = [xa[:, D_SSM + g * SSM_N:D_SSM + (g + 1) * SSM_N] for g in range(SSM_G)]
    cm = [xa[:, D_SSM + (SSM_G + g) * SSM_N:D_SSM + (SSM_G + g + 1) * SSM_N] for g in range(SSM_G)]
    bmb = [b.astype(BF16) for b in bm]
    cmb = [cc.astype(BF16) for cc in cm]
    cb = [_dot_nt(cmb[g], bmb[g]) for g in range(SSM_G)]
    lam, mm = [], []
    for h in range(SSM_H):
        diff = acs[:, h:h + 1] - acsT[h:h + 1, :]
        lam_h = jnp.exp(jnp.where(ci <= ri, diff, NEG))
        lam.append(lam_h)
        mm.append(cb[h // (SSM_H // SSM_G)] * lam_h)
    lo = lax.broadcasted_iota(jnp.int32, (CHUNK, 128), 1) < SSM_P
    xdt_h = []
    for h in range(SSM_H):
        pair = xdt[:, (h // 2) * 128:(h // 2 + 1) * 128]
        xdt_h.append(jnp.where(lo if h % 2 == 0 else jnp.logical_not(lo), pair, 0.0).astype(BF16))
    ydiag = jnp.concatenate(
        [_dot(mm[2 * j].astype(BF16), xdt_h[2 * j]) + _dot(mm[2 * j + 1].astype(BF16), xdt_h[2 * j + 1])
         for j in range(SSM_H // 2)], axis=1)
    t_off = [_dot(cmb[g], s_prev[g].astype(BF16)) for g in range(SSM_G)]
    yoff = jnp.concatenate(t_off, axis=1) * ea
    return dict(xs=xs, dt_e=dt_e, acs_e=acs_e, ea=ea, f=f, cd=cd, xdt=xdt, bm=bm, cm=cm, bmb=bmb, cmb=cmb, cb=cb, lam=lam,
                mm=mm, lo=lo, xdt_h=xdt_h, ydiag=ydiag, t_off=t_off, yoff=yoff)


def _ssd_specs(nc, rev):
    ix = (lambda i: nc - 1 - i) if rev else (lambda i: i)
    return [
        pl.BlockSpec((CHUNK, D_XBC), lambda i: (ix(i), 0)),
        pl.BlockSpec((8, D_XBC), lambda i: (jnp.maximum(ix(i) * (CHUNK // 8) - 1, 0), 0)),
        pl.BlockSpec((CHUNK, D_SSM), lambda i: (ix(i), 0)),
        pl.BlockSpec((CHUNK, 128), lambda i: (ix(i), 0)),
        pl.BlockSpec((SSM_H, CHUNK), lambda i: (0, ix(i))),
        _full((8, D_XBC)), _full((1, D_XBC)), _full((1, 128)), _full((SSM_H, 1)), _full((1, 128)), _full((SSM_H, 1)),
        _full((1, D_SSM)), _full((1, D_SSM)),
    ]


def _ssd_fwd(xbc, z, dtr, dtrT, cw, cb, dtb, dtbT, alog, alogT, d_e, ng):
    lp = xbc.shape[0]
    nc = lp // CHUNK

    def body(x_ref, xh_ref, z_ref, dt_ref, dtT_ref, cw_ref, cb_ref, dtb_ref, dtbT_ref, al_ref, alT_ref, de_ref, ng_ref,
             y_ref, st_ref, s_scr):
        c = pl.program_id(0)

        @pl.when(c == 0)
        def _():
            s_scr[...] = jnp.zeros_like(s_scr)

        k = _ssd_chunk(c, x_ref, xh_ref, dt_ref, dtT_ref, cw_ref, cb_ref, dtb_ref, dtbT_ref, al_ref, alT_ref)
        s_prev = [s_scr[g] for g in range(SSM_G)]
        st_ref[0] = s_scr[...]
        m = _ssd_mats(k, s_prev)
        xd = (m["xdt"] * m["f"]).astype(BF16)
        for g in range(SSM_G):
            sl = slice(g * GSZ, (g + 1) * GSZ)
            s_scr[g] = m["cd"][:, sl] * s_prev[g] + _dot(m["bm"][g].T.astype(BF16), xd[:, sl])
        y = m["ydiag"] + m["yoff"] + de_ref[...] * m["xs"]
        u = y * _silu(z_ref[...])
        outs = []
        for g in range(SSM_G):
            ug = u[:, g * GSZ:(g + 1) * GSZ]
            outs.append(ug * lax.rsqrt(jnp.mean(ug * ug, axis=-1, keepdims=True) + EPS))
        y_ref[...] = jnp.concatenate(outs, axis=1) * ng_ref[...]

    return pl.pallas_call(
        body, name="ssd_fwd", grid=(nc,), in_specs=_ssd_specs(nc, False),
        out_specs=[_rows(CHUNK, D_SSM), pl.BlockSpec((1, SSM_G, SSM_N, GSZ), lambda i: (i, 0, 0, 0))],
        out_shape=[_sds((lp, D_SSM), F32), _sds((nc, SSM_G, SSM_N, GSZ), F32)],
        scratch_shapes=[pltpu.VMEM((SSM_G, SSM_N, GSZ), F32)],
        compiler_params=_cp("arbitrary"))(xbc, xbc, z, dtr, dtrT, cw, cb, dtb, dtbT, alog, alogT, d_e, ng)


def _ssd_bwd(dmixin, xbc, z, dtr, dtrT, st, cw, cb, dtb, dtbT, alog, alogT, d_e, ng):
    lp = xbc.shape[0]
    nc = lp // CHUNK
    hpg = SSM_H // SSM_G

    def body(dy_ref, x_ref, xh_ref, z_ref, dt_ref, dtT_ref, st_ref, cw_ref, cb_ref, dtb_ref, dtbT_ref, al_ref, alT_ref,
             de_ref, ng_ref, dz_ref, dx_ref, ddt_ref, dcw_ref, dcb_ref, ddtb_ref, dal_ref, dd_ref, dng_ref, ds_scr, nx_scr):
        i = pl.program_id(0)
        c = nc - 1 - i
        first = i == 0

        @pl.when(first)
        def _():
            ds_scr[...] = jnp.zeros_like(ds_scr)
            nx_scr[...] = jnp.zeros_like(nx_scr)

        k = _ssd_chunk(c, x_ref, xh_ref, dt_ref, dtT_ref, cw_ref, cb_ref, dtb_ref, dtbT_ref, al_ref, alT_ref)
        s_prev = [st_ref[0, g] for g in range(SSM_G)]
        m = _ssd_mats(k, s_prev)
        ri, ci, expand = k["ri"], k["ci"], k["expand"]
        xs, acs, acsT = m["xs"], k["acs"], k["acsT"]
        zv = z_ref[...]
        dout = dy_ref[...]
        ngv = ng_ref[...]
        y = m["ydiag"] + m["yoff"] + de_ref[...] * xs
        sz = _silu(zv)
        u = y * sz
        du_parts, dng_parts = [], []
        for g in range(SSM_G):
            sl = slice(g * GSZ, (g + 1) * GSZ)
            dug, dngg = _rms_bwd(u[:, sl], ngv[:, sl], dout[:, sl])
            du_parts.append(dug)
            dng_parts.append(dngg)
        du = jnp.concatenate(du_parts, axis=1)
        _acc_rows(dng_ref, jnp.concatenate(dng_parts, axis=1), first)
        dy = du * sz
        dz_ref[...] = du * y * _dsilu(zv)
        dd_e = jnp.sum(dy * xs, axis=0, keepdims=True)
        _acc_rows(dd_ref, _dot_nt_hi(dd_e, expand), first)
        dxs = de_ref[...] * dy
        dacs_e = dy * m["yoff"]
        dtg = (dy * m["ea"]).astype(BF16)
        dxdt = jnp.zeros_like(xs)
        dlast_e = []
        db, dc, ds_prev = [], [], []
        xd = m["xdt"] * m["f"]
        dxd_all = []
        for g in range(SSM_G):
            sl = slice(g * GSZ, (g + 1) * GSZ)
            dsg = ds_scr[g]
            spb = s_prev[g].astype(BF16)
            dc.append(_dot_nt(dtg[:, sl], spb))
            dsp = _dot(m["cm"][g].T.astype(BF16), dtg[:, sl]) + m["cd"][:, sl] * dsg
            ds_prev.append(dsp)
            dlast_e.append(jnp.sum(dsg * s_prev[g], axis=0, keepdims=True) * m["cd"][:, sl])
            dsb = dsg.astype(BF16)
            db.append(_dot_nt(xd[:, sl].astype(BF16), dsb))
            dxd_all.append(_dot(m["bmb"][g], dsb))
        dxd = jnp.concatenate(dxd_all, axis=1)
        dxdt = dxd * m["f"]
        dff = dxd * xd
        dacs_e = dacs_e - dff
        dlast_row = jnp.concatenate(dlast_e, axis=1) + jnp.sum(dff, axis=0, keepdims=True)
        dacs = jnp.zeros((CHUNK, 128), F32)
        lane = lax.broadcasted_iota(jnp.int32, (1, 128), 1)
        cbT = [_dot_nt(m["bmb"][g], m["cmb"][g]) for g in range(SSM_G)]
        dgs = [jnp.zeros((CHUNK, CHUNK), F32) for _ in range(SSM_G)]
        dgTs = [jnp.zeros((CHUNK, CHUNK), F32) for _ in range(SSM_G)]
        dxdt_pairs = []
        for h in range(SSM_H):
            g = h // hpg
            pr = slice((h // 2) * 128, (h // 2 + 1) * 128)
            lo_h = m["lo"] if h % 2 == 0 else jnp.logical_not(m["lo"])
            dyp = jnp.where(lo_h, dy[:, pr], 0.0).astype(BF16)
            xdp = m["xdt"][:, pr].astype(BF16)
            dm = _dot_nt(dyp, xdp)
            dmT = _dot_nt(xdp, dyp)
            lamT = jnp.exp(jnp.where(ri <= ci, acsT[h:h + 1, :] - acs[:, h:h + 1], NEG))
            mT = cbT[g] * lamT
            dgs[g] = dgs[g] + dm * m["lam"][h]
            dgTs[g] = dgTs[g] + dmT * lamT
            v1 = jnp.sum(dm * m["mm"][h], axis=1, keepdims=True)
            v2 = jnp.sum(dmT * mT, axis=1, keepdims=True)
            dacs = dacs + (v1 - v2) * (lane == h).astype(F32)
            part = _dot(mT.astype(BF16), dyp)
            if h % 2 == 0:
                dxdt_pairs.append(part)
            else:
                dxdt_pairs[-1] = dxdt_pairs[-1] + part
        dxdt = dxdt + jnp.concatenate(dxdt_pairs, axis=1)
        for g in range(SSM_G):
            dc[g] = dc[g] + _dot(dgs[g].astype(BF16), m["bmb"][g])
            db[g] = db[g] + _dot(dgTs[g].astype(BF16), m["cmb"][g])
        dacs = dacs + _dot_nt_hi(dacs_e, expand)
        dlast = _dot_nt_hi(dlast_row, expand)
        dacs = dacs + jnp.where(ri == CHUNK - 1, dlast, 0.0)
        dxs = dxs + dxdt * m["dt_e"]
        ddt = _dot_nt_hi(dxdt * xs, expand)
        da = _dot_hi((ri <= ci).astype(F32), dacs)
        ddt = ddt + da * k["arow"]
        dA = jnp.sum(da * k["dtc"], axis=0, keepdims=True)
        _acc_rows(dal_ref, dA * k["arow"], first)
        ddtr = jnp.where(k["grow"] >= PAD_ROWS, ddt * _sigmoid(k["sp"]), 0.0)
        ddt_ref[...] = ddtr
        _acc_rows(ddtb_ref, jnp.sum(ddtr, axis=0, keepdims=True), first)
        for g in range(SSM_G):
            ds_scr[g] = ds_prev[g]
        dxa = jnp.concatenate([dxs] + db + dc, axis=1)
        dcv = dxa * _dsilu(k["cv"])
        _acc_rows(dcb_ref, jnp.sum(dcv, axis=0, keepdims=True), first)
        dcw_rows = [jnp.sum(dcv * k["sh"][SSM_K - 1 - kk], axis=0, keepdims=True) for kk in range(SSM_K)]
        dcw_rows.append(jnp.zeros((8 - SSM_K, D_XBC), F32))
        _acc_rows(dcw_ref, jnp.concatenate(dcw_rows, axis=0), first)
        nxt = nx_scr[...]
        dx = cw_ref[SSM_K - 1:SSM_K, :] * dcv
        for j in range(1, SSM_K):
            dx = dx + cw_ref[SSM_K - 1 - j:SSM_K - j, :] * _shift_up(dcv, nxt, j)
        grow_x = c * CHUNK + lax.broadcasted_iota(jnp.int32, (CHUNK, D_XBC), 0)
        dx_ref[...] = jnp.where(grow_x >= PAD_ROWS, dx, 0.0)
        nx_scr[...] = dcv[:8]

    specs = _ssd_specs(nc, True)
    in_specs = [pl.BlockSpec((CHUNK, D_SSM), lambda i: (nc - 1 - i, 1))] + specs[:5] + [
        pl.BlockSpec((1, SSM_G, SSM_N, GSZ), lambda i: (nc - 1 - i, 0, 0, 0))] + specs[5:]
    rv = lambda w: pl.BlockSpec((CHUNK, w), lambda i: (nc - 1 - i, 0))
    return pl.pallas_call(
        body, name="ssd_bwd", grid=(nc,), in_specs=in_specs,
        out_specs=[rv(D_SSM), rv(D_XBC), rv(128), _full((8, D_XBC)), _full((1, D_XBC)), _full((1, 128)), _full((1, 128)),
                   _full((1, 128)), _full((1, D_SSM))],
        out_shape=[_sds((lp, D_SSM), F32), _sds((lp, D_XBC), F32), _sds((lp, 128), F32), _sds((8, D_XBC), F32),
                   _sds((1, D_XBC), F32), _sds((1, 128), F32), _sds((1, 128), F32), _sds((1, 128), F32), _sds((1, D_SSM), F32)],
        scratch_shapes=[pltpu.VMEM((SSM_G, SSM_N, GSZ), F32), pltpu.VMEM((8, D_XBC), F32)],
        compiler_params=_cp("arbitrary"))(dmixin, xbc, xbc, z, dtr, dtrT, st, cw, cb, dtb, dtbT, alog, alogT, d_e, ng)


def _mixout_fwd(o, ssm, h0, g_ao, g_post, w):
    lp = o.shape[0]
    tr = _rt(lp, (320, 128))

    def body(o_ref, s_ref, h_ref, ga_ref, gp_ref, w_ref, mi_ref, mix_ref, h1_ref):
        mixin = jnp.concatenate([_rms(o_ref[...], ga_ref[...]), s_ref[...]], axis=1).astype(BF16)
        mi_ref[...] = mixin
        mix = _dot(mixin, w_ref[...])
        mix_ref[...] = mix
        grow = pl.program_id(0) * tr + lax.broadcasted_iota(jnp.int32, (tr, D), 0)
        h1_ref[...] = h_ref[...] + jnp.where(grow >= PAD_ROWS, _rms(mix, gp_ref[...]), 0.0)

    return pl.pallas_call(
        body, name="mixout_fwd", grid=(lp // tr,),
        in_specs=[_rows(tr, D), _rows(tr, D), _rows(tr, D), _full((1, D)), _full((1, D)), _full(w.shape)],
        out_specs=[_rows(tr, 2 * D), _rows(tr, D), _rows(tr, D)],
        out_shape=[_sds((lp, 2 * D), BF16), _sds((lp, D), F32), _sds((lp, D), F32)],
        compiler_params=_cp("parallel"))(o, ssm, h0, g_ao, g_post, w)


def _ffn_up(h1, g, w):
    lp = h1.shape[0]
    tr = _rt(lp, (320, 128))
    tn = D_FF // 2

    def body(h_ref, g_ref, w_ref, hn_ref, u_ref):
        hn = _rms(h_ref[...], g_ref[...]).astype(BF16)
        hn_ref[...] = hn
        u_ref[...] = _dot(hn, w_ref[...])

    return pl.pallas_call(
        body, name="ffn_up", grid=(lp // tr, 2 * D_FF // tn),
        in_specs=[pl.BlockSpec((tr, D), lambda i, j: (i, 0)), _full((1, D)), pl.BlockSpec((None, D, tn), lambda i, j: (j, 0, 0))],
        out_specs=[pl.BlockSpec((tr, D), lambda i, j: (i, 0)), pl.BlockSpec((tr, tn), lambda i, j: (i, j))],
        out_shape=[_sds((lp, D), BF16), _sds((lp, 2 * D_FF), F32)],
        compiler_params=_cp("parallel", "arbitrary"))(h1, g, w)


def _ffn_dhn(du, w4):
    lp = du.shape[0]
    nch, _, tn = w4.shape
    tr = _rt(lp, (320, 128))

    def body(du_ref, w_ref, o_ref):
        acc = _dot_nt(du_ref[:, 0:tn], w_ref[0])
        for j in range(1, nch):
            acc = acc + _dot_nt(du_ref[:, j * tn:(j + 1) * tn], w_ref[j])
        o_ref[...] = acc

    return pl.pallas_call(
        body, name="ffn_dhn", grid=(lp // tr,), in_specs=[_rows(tr, nch * tn), _full(w4.shape)], out_specs=_rows(tr, D),
        out_shape=_sds((lp, D), F32), compiler_params=_cp("parallel"))(du, w4)


FFN_CB = 256


def _ffn_gate(u, cw, cb):
    lp = u.shape[0]
    tr = _rt(lp, (320, 128))

    def body(u_ref, uh_ref, cw_ref, cb_ref, a_ref):
        i = pl.program_id(0)
        for j in range(D_FF // FFN_CB):
            halves = []
            for off in (0, D_FF):
                sl = slice(off + j * FFN_CB, off + (j + 1) * FFN_CB)
                x = u_ref[:, sl]
                halo = jnp.where(i > 0, uh_ref[:, sl], 0.0)
                cv = cb_ref[:, sl] + cw_ref[FFN_K - 1:FFN_K, sl] * x
                for s in range(1, FFN_K):
                    cv = cv + cw_ref[FFN_K - 1 - s:FFN_K - s, sl] * _shift_down(x, halo, s)
                halves.append(cv)
            a_ref[:, j * FFN_CB:(j + 1) * FFN_CB] = (_silu(halves[0]) * halves[1]).astype(BF16)

    return pl.pallas_call(
        body, name="ffn_gate", grid=(lp // tr,),
        in_specs=[_rows(tr, 2 * D_FF), pl.BlockSpec((8, 2 * D_FF), lambda i: (jnp.maximum(i * (tr // 8) - 1, 0), 0)),
                  _full((8, 2 * D_FF)), _full((1, 2 * D_FF))],
        out_specs=_rows(tr, D_FF), out_shape=_sds((lp, D_FF), BF16),
        compiler_params=_cp("parallel"))(u, u, cw, cb)


def _ffn_down(a, w, h1, tgt, g_post):
    lp = a.shape[0]
    tr = _rt(lp, (320, 128))

    def body(a_ref, w_ref, h_ref, t_ref, g_ref, dh2_ref, dd_ref, dg_ref, loss_ref):
        i = pl.program_id(0)
        d = _dot(a_ref[...], w_ref[...])
        gv = g_ref[...]
        h2 = h_ref[...] + _rms(d, gv)
        grow = i * tr + lax.broadcasted_iota(jnp.int32, (tr, D), 0)
        err = jnp.where(grow >= FRONT, h2 - t_ref[...], 0.0)
        dh2 = err * (1.0 / D)
        dh2_ref[...] = dh2
        dd, dg = _rms_bwd(d, gv, dh2)
        dd_ref[...] = dd.astype(BF16)
        _acc_rows(dg_ref, dg, i == 0)
        part = 0.5 * jnp.sum(jnp.sum(err * err, axis=1, keepdims=True), axis=0, keepdims=True) * (1.0 / D)
        _acc_rows(loss_ref, jnp.broadcast_to(part, (8, 128)), i == 0)

    return pl.pallas_call(
        body, name="ffn_down", grid=(lp // tr,),
        in_specs=[_rows(tr, D_FF), _full(w.shape), _rows(tr, D), _rows(tr, D), _full((1, D))],
        out_specs=[_rows(tr, D), _rows(tr, D), _full((1, D)), _full((8, 128))],
        out_shape=[_sds((lp, D), F32), _sds((lp, D), BF16), _sds((1, D), F32), _sds((8, 128), F32)],
        compiler_params=_cp("arbitrary"))(a, w, h1, tgt, g_post)


def _ffn_gate_bwd(u, da, cw, cb):
    lp = u.shape[0]
    tr = _rt(lp, (320, 128))
    n = lp // tr

    def body(u_ref, uh_ref, da_ref, cw_ref, cb_ref, du_ref, dcw_ref, dcb_ref, nx_scr):
        i = pl.program_id(0)
        t = n - 1 - i
        first = i == 0

        @pl.when(first)
        def _():
            nx_scr[...] = jnp.zeros_like(nx_scr)

        grow = t * tr + lax.broadcasted_iota(jnp.int32, (tr, FFN_CB), 0)
        for j in range(D_FF // FFN_CB):
            cvs, shs, sls = [], [], []
            for off in (0, D_FF):
                sl = slice(off + j * FFN_CB, off + (j + 1) * FFN_CB)
                x = u_ref[:, sl]
                halo = jnp.where(t > 0, uh_ref[:, sl], 0.0)
                sh = [x] + [_shift_down(x, halo, s) for s in range(1, FFN_K)]
                cv = cb_ref[:, sl]
                for kk in range(FFN_K):
                    cv = cv + cw_ref[kk:kk + 1, sl] * sh[FFN_K - 1 - kk]
                cvs.append(cv)
                shs.append(sh)
                sls.append(sl)
            dav = da_ref[:, j * FFN_CB:(j + 1) * FFN_CB]
            dcv = (dav * cvs[1] * _dsilu(cvs[0]), dav * _silu(cvs[0]))
            for hf in range(2):
                sl = sls[hf]
                g = dcv[hf]
                rows = [jnp.sum(g * shs[hf][FFN_K - 1 - kk], axis=0, keepdims=True) for kk in range(FFN_K)]
                rows.append(jnp.zeros((8 - FFN_K, FFN_CB), F32))
                upd_w = jnp.concatenate(rows, axis=0)
                upd_b = jnp.sum(g, axis=0, keepdims=True)

                @pl.when(first)
                def _():
                    dcw_ref[:, sl] = upd_w
                    dcb_ref[:, sl] = upd_b

                @pl.when(jnp.logical_not(first))
                def _():
                    dcw_ref[:, sl] += upd_w
                    dcb_ref[:, sl] += upd_b

                nxt = nx_scr[:, sl]
                du = cw_ref[FFN_K - 1:FFN_K, sl] * g
                for s in range(1, FFN_K):
                    du = du + cw_ref[FFN_K - 1 - s:FFN_K - s, sl] * _shift_up(g, nxt, s)
                du_ref[:, sl] = jnp.where(grow >= PAD_ROWS, du, 0.0).astype(BF16)
                nx_scr[:, sl] = g[:8]

    return pl.pallas_call(
        body, name="ffn_gate_bwd", grid=(n,),
        in_specs=[pl.BlockSpec((tr, 2 * D_FF), lambda i: (n - 1 - i, 0)),
                  pl.BlockSpec((8, 2 * D_FF), lambda i: (jnp.maximum((n - 1 - i) * (tr // 8) - 1, 0), 0)),
                  pl.BlockSpec((tr, D_FF), lambda i: (n - 1 - i, 0)), _full((8, 2 * D_FF)), _full((1, 2 * D_FF))],
        out_specs=[pl.BlockSpec((tr, 2 * D_FF), lambda i: (n - 1 - i, 0)), _full((8, 2 * D_FF)), _full((1, 2 * D_FF))],
        out_shape=[_sds((lp, 2 * D_FF), BF16), _sds((8, 2 * D_FF), F32), _sds((1, 2 * D_FF), F32)],
        scratch_shapes=[pltpu.VMEM((8, 2 * D_FF), F32)],
        compiler_params=_cp("arbitrary"))(u, u, da, cw, cb)


def _norm_bwd_res(x, g, dy, res, name, mask_pad=False, out_dtype=F32):
    lp, c = x.shape
    tr = _rt(lp, (640, 128))

    def body(*refs):
        if res is None:
            x_ref, g_ref, dy_ref, o_ref, dg_ref = refs
        else:
            x_ref, g_ref, dy_ref, r_ref, o_ref, dg_ref = refs
        i = pl.program_id(0)
        dyv = dy_ref[...].astype(F32)
        if mask_pad:
            grow = i * tr + lax.broadcasted_iota(jnp.int32, (tr, c), 0)
            dyv = jnp.where(grow >= PAD_ROWS, dyv, 0.0)
        dx, dg = _rms_bwd(x_ref[...].astype(F32), g_ref[...], dyv)
        if res is not None:
            dx = dx + r_ref[...]
        o_ref[...] = dx.astype(out_dtype)
        _acc_rows(dg_ref, dg, i == 0)

    args = [x, g, dy] + ([] if res is None else [res])
    in_specs = [_rows(tr, c), _full((1, c)), pl.BlockSpec((tr, c), lambda i: (i, 0))] + ([] if res is None else [_rows(tr, c)])
    return pl.pallas_call(
        body, name=name, grid=(lp // tr,), in_specs=in_specs, out_specs=[_rows(tr, c), _full((1, c))],
        out_shape=[_sds((lp, c), out_dtype), _sds((1, c), F32)], compiler_params=_cp("arbitrary"))(*args)


def _mla_bwd(dq, dk, dv, lat, qg, kvg, wq, wkv, cos, sa, sb):
    lp = lat.shape[0]
    tr = _rt(lp, (320, 128))

    def body(dq_ref, dk_ref, dv_ref, lat_ref, qg_ref, kvg_ref, wq_ref, wkv_ref, cos_ref, sa_ref, sb_ref,
             dqf_ref, dkvf_ref, dlat_ref, dqg_ref, dkvg_ref):
        i = pl.program_id(0)
        cos_v, sa_v, sb_v = cos_ref[...], sa_ref[...], sb_ref[...]
        dkpe = jnp.zeros((tr, 128), F32)
        for h in range(MLA_H):
            dqh = dq_ref[h] * SOFTMAX_SCALE
            dqf_ref[:, h * DN:(h + 1) * DN] = dqh[:, :DN].astype(BF16)
            dqf_ref[:, D + h * 128:D + (h + 1) * 128] = _rope_t(dqh[:, DN:], cos_v, sa_v, sb_v).astype(BF16)
            dkh = dk_ref[h]
            dkvf_ref[:, h * DN:(h + 1) * DN] = dkh[:, :DN].astype(BF16)
            dkpe = dkpe + dkh[:, DN:]
            dkvf_ref[:, D + h * DV:D + (h + 1) * DV] = dv_ref[h].astype(BF16)
        dql = _dot_nt(dqf_ref[...], wq_ref[...])
        dkl = _dot_nt(dkvf_ref[...], wkv_ref[...])
        lat_v = lat_ref[...]
        dqc, dqg = _rms_bwd(lat_v[:, :QR], qg_ref[...], dql)
        dkc, dkg = _rms_bwd(lat_v[:, QR:QR + KVR], kvg_ref[...], dkl)
        dlat_ref[:, :QR] = dqc
        dlat_ref[:, QR:QR + KVR] = dkc
        dlat_ref[:, QR + KVR:] = _rope_t(dkpe, cos_v, sa_v, sb_v)
        _acc_rows(dqg_ref, dqg, i == 0)
        _acc_rows(dkvg_ref, dkg, i == 0)

    hb = lambda w: pl.BlockSpec((MLA_H, tr, w), lambda i: (0, i, 0))
    return pl.pallas_call(
        body, name="mla_bwd", grid=(lp // tr,),
        in_specs=[hb(256), hb(256), hb(128), _rows(tr, LAT_W), _full((1, QR)), _full((1, KVR)), _full(wq.shape),
                  _full(wkv.shape), _rows(tr, 128), _rows(tr, 128), _rows(tr, 128)],
        out_specs=[_rows(tr, 2 * D), _rows(tr, 2 * D), _rows(tr, LAT_W), _full((1, QR)), _full((1, KVR))],
        out_shape=[_sds((lp, 2 * D), BF16), _sds((lp, 2 * D), BF16), _sds((lp, LAT_W), F32), _sds((1, QR), F32),
                   _sds((1, KVR), F32)],
        compiler_params=_cp("arbitrary"))(dq, dk, dv, lat, qg, kvg, wq, wkv, cos, sa, sb)


def _inproj_bwd(dlat, dz, dxbc, ddt, w, h0, g, dh1):
    lp = h0.shape[0]
    tr = _rt(lp, (320, 128))
    segs = ((0, LAT_W), (LAT_W, LAT_W + D_SSM), (LAT_W + D_SSM, LAT_W + D_SSM + D_XBC), (IN_P - 128, IN_P))

    def body(dl_ref, dz_ref, dx_ref, dt_ref, w_ref, h_ref, g_ref, r_ref, o_ref, dg_ref):
        dhn = jnp.zeros((tr, D), F32)
        for ref, (a, b) in zip((dl_ref, dz_ref, dx_ref, dt_ref), segs):
            dhn = dhn + _dot_nt(ref[...].astype(BF16), w_ref[:, a:b])
        dx, dg = _rms_bwd(h_ref[...], g_ref[...], dhn)
        o_ref[...] = dx + r_ref[...]
        _acc_rows(dg_ref, dg, pl.program_id(0) == 0)

    return pl.pallas_call(
        body, name="inproj_bwd", grid=(lp // tr,),
        in_specs=[_rows(tr, LAT_W), _rows(tr, D_SSM), _rows(tr, D_XBC), _rows(tr, 128), _full(w.shape), _rows(tr, D),
                  _full((1, D)), _rows(tr, D)],
        out_specs=[_rows(tr, D), _full((1, D))], out_shape=[_sds((lp, D), F32), _sds((1, D), F32)],
        compiler_params=_cp("arbitrary"))(dlat, dz, dxbc, ddt, w, h0, g, dh1)


def _rope_tables(lp):
    pos = (jnp.arange(lp, dtype=jnp.int32) - PAD_ROWS).astype(F32)
    inv = ROPE_THETA ** (-jnp.arange(0, DR, 2, dtype=F32) / DR)
    ang = pos[:, None] * inv[None, :]
    cos, sin = jnp.cos(ang), jnp.sin(ang)
    zero = jnp.zeros_like(sin)
    cos128 = jnp.concatenate([cos, cos, cos, cos], axis=1)
    sa128 = jnp.concatenate([-sin, zero, -sin, zero], axis=1)
    sb128 = jnp.concatenate([zero, sin, zero, sin], axis=1)
    return cos128, sa128, sb128


def _pad_rows8(w):
    return jnp.concatenate([w, jnp.zeros((8 - w.shape[0], w.shape[1]), w.dtype)], axis=0)


def _lane_pad(v):
    return jnp.concatenate([v, jnp.zeros((v.shape[0], 128 - v.shape[1]), v.dtype)], axis=1)


def _device_step(x, tgt, meta, p):
    s = x.shape[0]
    lp = s + FRONT
    zpad = jnp.zeros((PAD_ROWS, D), F32)
    h0 = jnp.concatenate([zpad, meta, x], axis=0)
    tgt_p = jnp.concatenate([jnp.zeros((FRONT, D), F32), tgt], axis=0)
    cos, sa, sb = _rope_tables(lp)

    w_in = p["w_in"]
    w_in_p = jnp.concatenate([w_in[:, :QR + KVR + DR], jnp.zeros((D, 64), BF16), w_in[:, QR + KVR + DR:],
                              jnp.zeros((D, 128 - SSM_H), BF16)], axis=1)
    w_uq = p["w_uq"]
    wq_p = jnp.concatenate([w_uq[:, :, :DN].reshape(QR, MLA_H * DN),
                            jnp.concatenate([w_uq[:, :, DN:], jnp.zeros((QR, MLA_H, 128 - DR), BF16)], axis=2).reshape(QR, MLA_H * 128)],
                           axis=1)
    w_ukv = p["w_ukv"]
    wkv_p = jnp.concatenate([w_ukv[:, :, :DN].reshape(KVR, MLA_H * DN), w_ukv[:, :, DN:].reshape(KVR, MLA_H * DV)], axis=1)
    scw = _pad_rows8(p["ssm_conv_w"])
    fcw = _pad_rows8(p["ffn_conv_w"])
    dtb, alog = _lane_pad(p["ssm_dt_bias"]), _lane_pad(p["ssm_A_log"])
    dtbT, alogT = p["ssm_dt_bias"].reshape(SSM_H, 1), p["ssm_A_log"].reshape(SSM_H, 1)
    d_e = jnp.repeat(p["ssm_D"], SSM_P, axis=1)

    hn, lat, z, xbc, dtr = _inproj(h0, p["norm_mix_pre"], w_in_p)
    dtrT = dtr[:, :SSM_H].T
    q, k, v, qlat, kvlat = _mla_prep(lat, p["q_a_norm"], p["kv_a_norm"], wq_p, wkv_p, cos, sa, sb)
    o, lse = _attn_fwd(q, k, v)
    ssm, st = _ssd_fwd(xbc, z, dtr, dtrT, scw, p["ssm_conv_b"], dtb, dtbT, alog, alogT, d_e, p["ssm_norm"])
    mixin, mix, h1 = _mixout_fwd(o, ssm, h0, p["attn_out_norm"], p["norm_mix_post"], p["w_out"])
    hn2, u = _ffn_up(h1, p["norm_ffn_pre"], p["w_up"])
    a = _ffn_gate(u, fcw, p["ffn_conv_b"])
    dh2, dd, g_ffn_post, loss = _ffn_down(a, p["w_down"], h1, tgt_p, p["norm_ffn_post"])

    da = _mm_nt(dd, p["w_down"], "ffn_da")
    g_w_down = _mm_tn(a, dd, "ffn_dw_down", tn=512)
    du, g_fcw, g_fcb = _ffn_gate_bwd(u, da, fcw, p["ffn_conv_b"])
    dhn2 = _ffn_dhn(du, p["w_up"])
    g_w_up = _mm_tn(hn2, du, "ffn_dw_up", tn=D_FF // 2, chunked=True)
    dh1, g_ffn_pre = _norm_bwd_res(h1, p["norm_ffn_pre"], dhn2, dh2, "ffn_norm_bwd")
    dmix, g_mix_post = _norm_bwd_res(mix, p["norm_mix_post"], dh1, None, "mix_post_bwd", mask_pad=True, out_dtype=BF16)
    dmixin = _mm_nt(dmix, p["w_out"], "mix_dmixin")
    g_w_out = _mm_tn(mixin, dmix, "mix_dw_out", tn=512)
    do, g_ao, delta = _attn_out_bwd(o, p["attn_out_norm"], dmixin)
    t = _rt(lp, (640, 128))
    dq, dk, dv = _attn_bwd(q, k, v, do, lse[:, 0, :].reshape(MLA_H, lp // t, 1, t), delta.reshape(MLA_H, lp // t, 1, t))
    dqf, dkvf, dlat, g_qa, g_kva = _mla_bwd(dq, dk, dv, lat, p["q_a_norm"], p["kv_a_norm"], wq_p, wkv_p, cos, sa, sb)
    g_wq_p = _mm_tn(qlat, dqf, "mla_dw_uq")
    g_wkv_p = _mm_tn(kvlat, dkvf, "mla_dw_ukv")
    dz, dxbc, ddtr, g_scw, g_scb, g_dtb, g_alog, g_dd, g_ssm_norm = _ssd_bwd(
        dmixin, xbc, z, dtr, dtrT, st, scw, p["ssm_conv_b"], dtb, dtbT, alog, alogT, d_e, p["ssm_norm"])
    dh0, g_mix_pre = _inproj_bwd(dlat, dz, dxbc, ddtr, w_in_p, h0, p["norm_mix_pre"], dh1)
    g_in_p = jnp.concatenate([_mm_tn(hn, dlat, "in_dw_lat"), _mm_tn(hn, dz, "in_dw_z"), _mm_tn(hn, dxbc, "in_dw_xbc"),
                              _mm_tn(hn, ddtr, "in_dw_dt")], axis=1)

    g_w_in = jnp.concatenate([g_in_p[:, :QR + KVR + DR], g_in_p[:, LAT_W:LAT_W + D_SSM + D_XBC + SSM_H]], axis=1)
    g_w_uq = jnp.concatenate([g_wq_p[:, :D].reshape(QR, MLA_H, DN), g_wq_p[:, D:].reshape(QR, MLA_H, 128)[:, :, :DR]], axis=2)
    g_w_ukv = jnp.concatenate([g_wkv_p[:, :D].reshape(KVR, MLA_H, DN), g_wkv_p[:, D:].reshape(KVR, MLA_H, DV)], axis=2)
    grads = dict(
        norm_mix_pre=g_mix_pre, norm_mix_post=g_mix_post, norm_ffn_pre=g_ffn_pre, norm_ffn_post=g_ffn_post, w_in=g_w_in,
        q_a_norm=g_qa, w_uq=g_w_uq, kv_a_norm=g_kva, w_ukv=g_w_ukv, attn_out_norm=g_ao, ssm_conv_w=g_scw[:SSM_K],
        ssm_conv_b=g_scb, ssm_dt_bias=g_dtb[:, :SSM_H], ssm_A_log=g_alog[:, :SSM_H], ssm_D=g_dd[:, :SSM_H],
        ssm_norm=g_ssm_norm, w_out=g_w_out, w_up=g_w_up, ffn_conv_w=g_fcw[:FFN_K], ffn_conv_b=g_fcb, w_down=g_w_down)
    return loss, dh0[FRONT:], dh0[PAD_ROWS:FRONT], grads


N_CHIPS = 4
BIG = (("w_in", (D, D_IN // N_CHIPS)), ("w_uq", (QR // N_CHIPS, MLA_H, DN + DR)), ("w_ukv", (KVR // N_CHIPS, MLA_H, DN + DV)),
       ("w_out", (2 * D // N_CHIPS, D)), ("w_up", (D, 2 * D_FF // N_CHIPS)), ("w_down", (D_FF // N_CHIPS, D)))
BIG_AXIS = dict(w_in=1, w_uq=0, w_ukv=0, w_out=0, w_up=1, w_down=0)
SHARD_ELEMS = sum(functools.reduce(lambda a, b: a * b, s) for _, s in BIG)
HALF_ROWS = SHARD_ELEMS // 256
SMALL_SHARDED = (("meta_tokens", (N_META, D // N_CHIPS)), ("ssm_conv_w", (SSM_K, D_XBC // N_CHIPS)),
                 ("ffn_conv_w", (FFN_K, 2 * D_FF // N_CHIPS)))
SMALL_REPL = (("norm_mix_pre", D), ("norm_mix_post", D), ("norm_ffn_pre", D), ("norm_ffn_post", D), ("q_a_norm", QR),
              ("kv_a_norm", KVR), ("attn_out_norm", D), ("ssm_conv_b", D_XBC), ("ssm_dt_bias", SSM_H), ("ssm_A_log", SSM_H),
              ("ssm_D", SSM_H), ("ssm_norm", D_SSM), ("ffn_conv_b", 2 * D_FF))
ANY = pl.BlockSpec(memory_space=pl.ANY)


def _pad128(v):
    n = v.shape[0]
    return jnp.concatenate([v, jnp.zeros(((-n) % 128,), v.dtype)]) if n % 128 else v


def _pack_rows(vs, rows):
    flat = jnp.concatenate([_pad128(v.reshape(-1)) for v in vs])
    flat = jnp.concatenate([flat, jnp.zeros((rows * 128 - flat.shape[0],), flat.dtype)])
    return flat.reshape(rows, 128)


def _unpack_rows(pack, sizes):
    flat = pack.reshape(-1)
    out, off = [], 0
    for n in sizes:
        out.append(flat[off:off + n])
        off += n + (-n) % 128
    return out


def _my_place():
    return lax.axis_index("x"), lax.axis_index("y"), lax.axis_index("c")


def _other_chips(x, y):
    return [(1 - x, y), (x, 1 - y), (1 - x, 1 - y)]


def _remote(src, dst, send, recv, dev):
    return pltpu.make_async_remote_copy(src_ref=src, dst_ref=dst, send_sem=send, recv_sem=recv, device_id=dev,
                                        device_id_type=MESH)


SMALL_AG_ROWS = 80


def _gather_weights(shards, small, name):
    arrs = list(shards) + ([] if small is None else [small])
    n, nb = len(arrs), len(shards)

    def body(*refs):
        ins, outs = refs[:n], refs[n:2 * n]
        send, recv, lsem = refs[2 * n:]
        x, y, c = _my_place()
        me = 2 * x + y
        chips = _other_chips(x, y)
        half_of = lambda w, ref: ref.at[c] if w < nb else ref
        slot = lambda w, chip, cc: outs[w].at[chip, cc] if w < nb else outs[w].at[chip]
        loc = [pltpu.make_async_copy(ins[w], outs[w].at[me], lsem.at[w]) for w in range(n)]
        for cp in loc:
            cp.start()
        sends = []
        for w in range(n):
            for kk, (cx, cy) in enumerate(chips):
                sends.append(_remote(half_of(w, ins[w]), slot(w, me, c), send.at[3 * w + kk], recv.at[3 * w + kk], (cx, cy, c)))
        for cp in sends:
            cp.start()
        for w in range(nb):
            for kk, (cx, cy) in enumerate(chips):
                src = 2 * cx + cy
                _remote(ins[w].at[c], slot(w, src, c), send.at[3 * w + kk], recv.at[3 * w + kk], (cx, cy, c)).wait_recv()
                fwd = _remote(slot(w, src, c), slot(w, src, c), send.at[3 * (n + w) + kk], recv.at[3 * (n + w) + kk], (x, y, 1 - c))
                fwd.start()
                sends.append(fwd)
        for w in range(n):
            for kk, (cx, cy) in enumerate(chips):
                src = 2 * cx + cy
                if w < nb:
                    _remote(ins[w].at[c], slot(w, src, 1 - c), send.at[3 * (n + w) + kk], recv.at[3 * (n + w) + kk],
                            (x, y, 1 - c)).wait_recv()
                else:
                    _remote(ins[w], slot(w, src, c), send.at[3 * w + kk], recv.at[3 * w + kk], (cx, cy, c)).wait_recv()
        for cp in sends:
            cp.wait_send()
        for cp in loc:
            cp.wait()

    return pl.pallas_call(
        body, name=name, in_specs=[ANY] * n, out_specs=[ANY] * n,
        out_shape=[_sds((N_CHIPS,) + a.shape, a.dtype) for a in arrs],
        scratch_shapes=[pltpu.SemaphoreType.DMA((3 * (n + nb),)), pltpu.SemaphoreType.DMA((3 * (n + nb),)),
                        pltpu.SemaphoreType.DMA((n,))])(*arrs)


def _send_sibling_halves(gs):
    n = len(gs)

    def body(*refs):
        ins, outs, send, recv = refs[:n], refs[n:2 * n], refs[2 * n], refs[2 * n + 1]
        x, y, c = _my_place()
        cps = [_remote(ins[w].at[:, 1 - c], outs[w], send.at[w], recv.at[w], (x, y, 1 - c)) for w in range(n)]
        for cp in cps:
            cp.start()
        for cp in cps:
            cp.wait()

    return pl.pallas_call(
        body, name="reduce_sibling", in_specs=[ANY] * n, out_specs=[ANY] * n,
        out_shape=[_sds((g.shape[0],) + g.shape[2:], g.dtype) for g in gs],
        scratch_shapes=[pltpu.SemaphoreType.DMA((n,)), pltpu.SemaphoreType.DMA((n,))])(*gs)


def _exchange_chips(ps):
    n = len(ps)

    def body(*refs):
        ins, outs, send, recv = refs[:n], refs[n:2 * n], refs[2 * n], refs[2 * n + 1]
        x, y, c = _my_place()
        cps = [_remote(ins[w].at[2 * cx + cy], outs[w].at[kk], send.at[3 * w + kk], recv.at[3 * w + kk], (cx, cy, c))
               for w in range(n) for kk, (cx, cy) in enumerate(_other_chips(x, y))]
        for cp in cps:
            cp.start()
        for cp in cps:
            cp.wait()

    return pl.pallas_call(
        body, name="exchange_chips", in_specs=[ANY] * n, out_specs=[ANY] * n,
        out_shape=[_sds((3,) + p.shape[1:], p.dtype) for p in ps],
        scratch_shapes=[pltpu.SemaphoreType.DMA((3 * n,)), pltpu.SemaphoreType.DMA((3 * n,))])(*ps)


def _share_sibling(halves):
    n = len(halves)

    def body(*refs):
        ins, outs, send, recv, lsem = refs[:n], refs[n:2 * n], refs[2 * n], refs[2 * n + 1], refs[2 * n + 2]
        x, y, c = _my_place()
        loc = [pltpu.make_async_copy(ins[w], outs[w].at[c], lsem.at[w]) for w in range(n)]
        cps = [_remote(ins[w], outs[w].at[c], send.at[w], recv.at[w], (x, y, 1 - c)) for w in range(n)]
        for cp in loc + cps:
            cp.start()
        for w in range(n):
            cps[w].wait_send()
            _remote(ins[w], outs[w].at[1 - c], send.at[w], recv.at[w], (x, y, 1 - c)).wait_recv()
            loc[w].wait()

    return pl.pallas_call(
        body, name="share_sibling", in_specs=[ANY] * n, out_specs=[ANY] * n,
        out_shape=[_sds((2,) + h.shape, h.dtype) for h in halves],
        scratch_shapes=[pltpu.SemaphoreType.DMA((n,)), pltpu.SemaphoreType.DMA((n,)), pltpu.SemaphoreType.DMA((n,))])(*halves)


def _row_tile(r, c, cap=1 << 20):
    return next(t for t in range(r, 0, -1) if r % t == 0 and (t % 8 == 0 or t == r) and t * c * 4 <= cap)


def _add_pair(g, t, core, name):
    _, _, r, c = g.shape
    tr = _row_tile(r, c)

    def body(c_ref, g_ref, t_ref, o_ref):
        o_ref[...] = (g_ref[...] + t_ref[...]).astype(BF16)

    return pl.pallas_call(
        body, name=name, out_shape=_sds(t.shape, BF16),
        grid_spec=pltpu.PrefetchScalarGridSpec(
            num_scalar_prefetch=1, grid=(N_CHIPS, r // tr),
            in_specs=[pl.BlockSpec((None, None, tr, c), lambda j, i, cr: (j, cr[0], i, 0)),
                      pl.BlockSpec((None, tr, c), lambda j, i, cr: (j, i, 0))],
            out_specs=pl.BlockSpec((None, tr, c), lambda j, i, cr: (j, i, 0))),
        compiler_params=_cp("parallel", "parallel"))(core, g, t)


def _add_chips(p, got, chip, name):
    _, r, c = p.shape
    tr = _row_tile(r, c)

    def body(c_ref, p_ref, g_ref, o_ref):
        o_ref[...] = ((p_ref[...].astype(F32) + g_ref[0].astype(F32)) + g_ref[1].astype(F32)) + g_ref[2].astype(F32)

    return pl.pallas_call(
        body, name=name, out_shape=_sds((r, c), F32),
        grid_spec=pltpu.PrefetchScalarGridSpec(
            num_scalar_prefetch=1, grid=(r // tr,),
            in_specs=[pl.BlockSpec((None, tr, c), lambda i, cr: (cr[0], i, 0)), pl.BlockSpec((3, tr, c), lambda i, cr: (0, i, 0))],
            out_specs=pl.BlockSpec((tr, c), lambda i, cr: (i, 0))),
        compiler_params=_cp("parallel"))(chip, p, got)


SMALL_AR_ROWS = 424


def _allreduce_small(v):
    def body(v_ref, o_ref, gath, send, recv):
        x, y, c = _my_place()
        me = 4 * x + 2 * y + c
        gath[me] = v_ref[...]
        cps = []
        for dd in range(1, 8):
            dx, dy, dc = dd >> 2, (dd >> 1) & 1, dd & 1
            peer = (1 - x if dx else x, 1 - y if dy else y, 1 - c if dc else c)
            cps.append(_remote(v_ref, gath.at[me], send.at[dd - 1], recv.at[dd - 1], peer))
        for cp in cps:
            cp.start()
        for cp in cps:
            cp.wait()
        acc = gath[0]
        for dev in range(1, 8):
            acc = acc + gath[dev]
        o_ref[...] = acc

    vm = pl.BlockSpec(memory_space=pltpu.VMEM)
    return pl.pallas_call(
        body, name="allreduce_small", in_specs=[vm], out_specs=vm, out_shape=_sds(v.shape, F32),
        scratch_shapes=[pltpu.VMEM((8,) + v.shape, F32), pltpu.SemaphoreType.DMA((7,)), pltpu.SemaphoreType.DMA((7,))])(v)


def _adamw(w, g, m, v, name):
    r, c = w.shape
    tr = _row_tile(r, c)

    def body(w_ref, g_ref, m_ref, v_ref, d_ref, m2_ref, v2_ref):
        gv = g_ref[...]
        m2 = ADAM_B1 * m_ref[...] + (1.0 - ADAM_B1) * gv
        v2 = ADAM_B2 * v_ref[...] + (1.0 - ADAM_B2) * jnp.square(gv)
        m_hat = m2 / (1.0 - ADAM_B1 ** ADAM_STEP)
        v_hat = v2 / (1.0 - ADAM_B2 ** ADAM_STEP)
        d_ref[...] = -ADAM_LR * (m_hat / (jnp.sqrt(v_hat) + ADAM_EPS) + ADAM_WD * w_ref[...])
        m2_ref[...] = m2
        v2_ref[...] = v2

    return pl.pallas_call(
        body, name=name, grid=(r // tr,), in_specs=[_rows(tr, c)] * 4, out_specs=[_rows(tr, c)] * 3,
        out_shape=[_sds((r, c), F32)] * 3, compiler_params=_cp("parallel"))(w, g, m, v)


WEIGHT_NAMES = ("meta_tokens", "norm_mix_pre", "norm_mix_post", "norm_ffn_pre", "norm_ffn_post", "w_in", "q_a_norm", "w_uq",
                "kv_a_norm", "w_ukv", "attn_out_norm", "ssm_conv_w", "ssm_conv_b", "ssm_dt_bias", "ssm_A_log", "ssm_D",
                "ssm_norm", "w_out", "w_up", "ffn_conv_w", "ffn_conv_b", "w_down")
SMALL_ADAM_ROWS = 192


def kernel(x, meta_tokens, norm_mix_pre, norm_mix_post, norm_ffn_pre, norm_ffn_post, w_in, q_a_norm, w_uq, kv_a_norm, w_ukv, attn_out_norm, ssm_conv_w, ssm_conv_b, ssm_dt_bias, ssm_A_log, ssm_D, ssm_norm, w_out, w_up, ffn_conv_w, ffn_conv_b, w_down, loss_target, m_meta_tokens, m_norm_mix_pre, m_norm_mix_post, m_norm_ffn_pre, m_norm_ffn_post, m_w_in, m_q_a_norm, m_w_uq, m_kv_a_norm, m_w_ukv, m_attn_out_norm, m_ssm_conv_w, m_ssm_conv_b, m_ssm_dt_bias, m_ssm_A_log, m_ssm_D, m_ssm_norm, m_w_out, m_w_up, m_ffn_conv_w, m_ffn_conv_b, m_w_down, v_meta_tokens, v_norm_mix_pre, v_norm_mix_post, v_norm_ffn_pre, v_norm_ffn_post, v_w_in, v_q_a_norm, v_w_uq, v_kv_a_norm, v_w_ukv, v_attn_out_norm, v_ssm_conv_w, v_ssm_conv_b, v_ssm_dt_bias, v_ssm_A_log, v_ssm_D, v_ssm_norm, v_w_out, v_w_up, v_ffn_conv_w, v_ffn_conv_b, v_w_down):
    args = locals()
    w = {n: args[n] for n in WEIGHT_NAMES}
    mom = {n: args["m_" + n] for n in WEIGHT_NAMES}
    var = {n: args["v_" + n] for n in WEIGHT_NAMES}
    cx, cy, cc = _my_place()
    chip = 2 * cx + cy

    two_d = {n: (shp[0], functools.reduce(lambda a, b: a * b, shp[1:])) for n, shp in BIG}
    names = [n for n, _ in BIG]
    halves = [w[n].astype(BF16).reshape(2, two_d[n][0] // 2, two_d[n][1]) for n in names]
    small = _pack_rows([w[n] for n, _ in SMALL_SHARDED], SMALL_AG_ROWS)
    *gathered, small_all = _gather_weights(halves, small, "allgather_weights")
    gath = {n: a.reshape((N_CHIPS,) + two_d[n]) for n, a in zip(names, gathered)}
    p = dict(w_in=gath["w_in"].transpose(1, 0, 2).reshape(D, D_IN), w_uq=gath["w_uq"].reshape(QR, MLA_H, DN + DR),
             w_ukv=gath["w_ukv"].reshape(KVR, MLA_H, DN + DV), w_out=gath["w_out"].reshape(2 * D, D), w_up=gath["w_up"],
             w_down=gath["w_down"].reshape(D_FF, D))
    sm_parts = [_unpack_rows(small_all[j], [a * b for _, (a, b) in SMALL_SHARDED]) for j in range(N_CHIPS)]
    for i, (n, shp) in enumerate(SMALL_SHARDED):
        p[n] = jnp.concatenate([sm_parts[j][i].reshape(shp) for j in range(N_CHIPS)], axis=1)
    for n, _ in SMALL_REPL:
        p[n] = w[n]
    meta_full = p.pop("meta_tokens")

    loss_part, gx, gmeta, g = _device_step(x[0], loss_target[0], meta_full, p)

    small_names = [n for n, _ in SMALL_REPL] + ["ssm_conv_w", "ffn_conv_w"]
    small_sizes = [128] + [sz for _, sz in SMALL_REPL] + [N_META * D, SSM_K * D_XBC, FFN_K * 2 * D_FF]
    order = [n for n, _ in SMALL_REPL]
    sp = _pack_rows([loss_part[0]] + [g[n] for n in order] + [gmeta, g["ssm_conv_w"], g["ffn_conv_w"]], SMALL_AR_ROWS)
    red = _unpack_rows(_allreduce_small(sp), small_sizes)
    loss = red[0][0]
    gfull = {n: red[1 + i].reshape(1, -1) for i, n in enumerate(order)}
    n_r = len(order)
    gfull["meta_tokens"] = lax.dynamic_slice_in_dim(red[1 + n_r].reshape(N_META, D), chip * (D // N_CHIPS), D // N_CHIPS, axis=1)
    gfull["ssm_conv_w"] = lax.dynamic_slice_in_dim(red[2 + n_r].reshape(SSM_K, D_XBC), chip * (D_XBC // N_CHIPS),
                                                   D_XBC // N_CHIPS, axis=1)[None]
    gfull["ffn_conv_w"] = lax.dynamic_slice_in_dim(red[3 + n_r].reshape(FFN_K, 2 * D_FF), chip * (2 * D_FF // N_CHIPS),
                                                   2 * D_FF // N_CHIPS, axis=1)[None]

    g["w_in"] = g["w_in"].reshape(D, N_CHIPS, D_IN // N_CHIPS).transpose(1, 0, 2)
    gs = [g[n].reshape(N_CHIPS, 2, two_d[n][0] // 2, two_d[n][1]) for n in names]
    core_i = cc.astype(jnp.int32).reshape(1)
    chip_i = chip.astype(jnp.int32).reshape(1)
    from_sib = _send_sibling_halves(gs)
    pairs = [_add_pair(gg, tt, core_i, "reduce_pair_" + n) for n, gg, tt in zip(names, gs, from_sib)]
    got = _exchange_chips(pairs)
    mine = [_add_chips(pp, gg, chip_i, "reduce_chips_" + n) for n, pp, gg in zip(names, pairs, got)]
    for n, both in zip(names, _share_sibling(mine)):
        gfull[n] = both.reshape(two_d[n])

    delta, new_m, new_v = {}, {}, {}
    for n, shp in BIG:
        outs = _adamw(w[n].reshape(two_d[n]), gfull[n], mom[n].reshape(two_d[n]), var[n].reshape(two_d[n]), "adamw_" + n)
        delta[n], new_m[n], new_v[n] = (o.reshape((1,) + shp) for o in outs)
    snames = order + ["meta_tokens", "ssm_conv_w", "ffn_conv_w"]
    ssizes = [functools.reduce(lambda a, b: a * b, w[n].shape) for n in snames]
    packs = [_pack_rows([d[n] for n in snames], SMALL_ADAM_ROWS) for d in (w, gfull, mom, var)]
    outs = _adamw(*packs, "adamw_small")
    for d, o in zip((delta, new_m, new_v), outs):
        for n, piece in zip(snames, _unpack_rows(o, ssizes)):
            d[n] = piece.reshape(w[n].shape)
    gout = {n: gfull[n].reshape(w[n].shape) for n in WEIGHT_NAMES}
    return (loss, gx[None], *[gout[n] for n in WEIGHT_NAMES], *[delta[n] for n in WEIGHT_NAMES],
            *[new_m[n] for n in WEIGHT_NAMES], *[new_v[n] for n in WEIGHT_NAMES])
```

```python
import functools

import jax
import jax.numpy as jnp
from jax import lax
from jax.experimental import pallas as pl
from jax.experimental.pallas import tpu as pltpu

F32 = jnp.float32
BF16 = jnp.bfloat16

D = 1024
N_META = 16
FRONT = 128
PAD_ROWS = FRONT - N_META
MLA_H = 8
DN, DR, DV = 128, 64, 128
QR, KVR = 384, 256
SOFTMAX_SCALE = (DN + DR) ** -0.5
ROPE_THETA = 10000.0
SSM_H, SSM_P, SSM_G, SSM_N, SSM_K = 16, 64, 2, 128, 4
CHUNK = 128
D_SSM = SSM_H * SSM_P
D_XBC = D_SSM + 2 * SSM_G * SSM_N
GSZ = D_SSM // SSM_G
D_FF = 2816
FFN_K = 3
EPS = 1e-6
IN_SPLITS = (QR, KVR, DR, D_SSM, D_XBC, SSM_H)
D_IN = sum(IN_SPLITS)
LAT_W = 768
IN_P = LAT_W + D_SSM + D_XBC + 128
NEG = -1e30
LOG2E = 1.4426950408889634
LN2 = 0.6931471805599453
Q_SCALE = SOFTMAX_SCALE * LOG2E

ADAM_LR, ADAM_B1, ADAM_B2, ADAM_EPS, ADAM_WD, ADAM_STEP = 0.001, 0.9, 0.999, 1e-08, 0.01, 10

VMEM_LIMIT = 56 * 1024 * 1024
MM_ROWS = (640, 320, 128)
MESH = pl.DeviceIdType.MESH


def _sds(shape, dtype):
    return jax.ShapeDtypeStruct(shape, dtype)


def _cp(*sem):
    return pltpu.CompilerParams(dimension_semantics=sem, vmem_limit_bytes=VMEM_LIMIT)


def _rt(n, cands):
    for c in cands:
        if n % c == 0:
            return c
    raise ValueError((n, cands))


def _full(shape):
    nd = len(shape)
    return pl.BlockSpec(shape, lambda *_: (0,) * nd)


def _rows(tr, c):
    return pl.BlockSpec((tr, c), lambda i: (i, 0))


def _sigmoid(x):
    return 1.0 / (1.0 + jnp.exp(-x))


def _silu(x):
    return x * _sigmoid(x)


def _dsilu(x):
    s = _sigmoid(x)
    return s * (1.0 + x * (1.0 - s))


def _softplus(x):
    return jnp.maximum(x, 0.0) + jnp.log(1.0 + jnp.exp(-jnp.abs(x)))


def _rms(x, g):
    r = lax.rsqrt(jnp.mean(x * x, axis=-1, keepdims=True) + EPS)
    return x * r * g


def _rms_bwd(x, g, dy):
    r = lax.rsqrt(jnp.mean(x * x, axis=-1, keepdims=True) + EPS)
    xh = x * r
    dxh = dy * g
    dx = r * (dxh - xh * jnp.mean(dxh * xh, axis=-1, keepdims=True))
    return dx, jnp.sum(dy * xh, axis=0, keepdims=True)


def _dot(a, b):
    return jnp.dot(a, b, preferred_element_type=F32)


def _dot_nt(a, b):
    return lax.dot_general(a, b, (((1,), (1,)), ((), ())), preferred_element_type=F32)


def _dot_tn(a, b):
    return lax.dot_general(a, b, (((0,), (0,)), ((), ())), preferred_element_type=F32)


def _split3(x):
    hi = x.astype(BF16)
    r = x - hi.astype(F32)
    mid = r.astype(BF16)
    return hi, mid, (r - mid.astype(F32)).astype(BF16)


def _dot_hi(a, b, split="a"):
    if split == "a":
        bb = b.astype(BF16)
        return sum(_dot(t, bb) for t in _split3(a))
    ab = a.astype(BF16)
    return sum(_dot(ab, t) for t in _split3(b))


def _dot_nt_hi(a, b):
    bb = b.astype(BF16)
    return sum(_dot_nt(t, bb) for t in _split3(a))


def _shift_down(x, halo, j):
    xr = pltpu.roll(x, j, axis=0)
    hr = pltpu.roll(halo, j, axis=0)
    row = lax.broadcasted_iota(jnp.int32, (8, x.shape[1]), 0)
    first = jnp.where(row < j, hr, xr[:8])
    return jnp.concatenate([first, xr[8:]], axis=0)


def _shift_up(x, nxt, j):
    t = x.shape[0]
    xr = pltpu.roll(x, t - j, axis=0)
    nr = pltpu.roll(nxt, 8 - j, axis=0)
    row = lax.broadcasted_iota(jnp.int32, (8, x.shape[1]), 0)
    last = jnp.where(row + j >= 8, nr, xr[t - 8:])
    return jnp.concatenate([xr[:t - 8], last], axis=0)


def _acc_rows(ref, val, first):
    @pl.when(first)
    def _():
        ref[...] = val

    @pl.when(jnp.logical_not(first))
    def _():
        ref[...] += val


def _mm_nt(a, b, name, out_dtype=F32, tms=(640, 320, 128)):
    m, k = a.shape
    n = b.shape[0]
    tm = _rt(m, tms)

    def body(a_ref, b_ref, o_ref):
        o_ref[...] = _dot_nt(a_ref[...].astype(BF16), b_ref[...]).astype(out_dtype)

    return pl.pallas_call(
        body, name=name, grid=(m // tm,), in_specs=[_rows(tm, k), _full(b.shape)], out_specs=_rows(tm, n),
        out_shape=_sds((m, n), out_dtype), compiler_params=_cp("parallel"))(a, b)


def _mm_tn(a, b, name, tn=None, trs=(1664, 640, 128), chunked=False):
    r, m = a.shape
    n = b.shape[1]
    tn = n if tn is None else tn
    tr = _rt(r, trs)

    def body(a_ref, b_ref, o_ref):
        part = _dot_tn(a_ref[...].astype(BF16), b_ref[...].astype(BF16))
        _acc_rows(o_ref, part, pl.program_id(1) == 0)

    if chunked:
        out_specs, out_shape = pl.BlockSpec((None, m, tn), lambda j, i: (j, 0, 0)), _sds((n // tn, m, tn), F32)
    else:
        out_specs, out_shape = pl.BlockSpec((m, tn), lambda j, i: (0, j)), _sds((m, n), F32)
    return pl.pallas_call(
        body, name=name, grid=(n // tn, r // tr),
        in_specs=[pl.BlockSpec((tr, m), lambda j, i: (i, 0)), pl.BlockSpec((tr, tn), lambda j, i: (i, j))],
        out_specs=out_specs, out_shape=out_shape, compiler_params=_cp("parallel", "arbitrary"))(a, b)


def _inproj(h0, g, w):
    lp = h0.shape[0]
    tr = _rt(lp, MM_ROWS)
    segs = ((0, LAT_W), (LAT_W, LAT_W + D_SSM), (LAT_W + D_SSM, LAT_W + D_SSM + D_XBC), (IN_P - 128, IN_P))

    def body(h_ref, g_ref, w_ref, hn_ref, lat_ref, z_ref, xbc_ref, dt_ref):
        hn = _rms(h_ref[...], g_ref[...]).astype(BF16)
        hn_ref[...] = hn
        for ref, (a, b) in zip((lat_ref, z_ref, xbc_ref, dt_ref), segs):
            ref[...] = _dot(hn, w_ref[:, a:b])

    return pl.pallas_call(
        body, name="inproj", grid=(lp // tr,), in_specs=[_rows(tr, D), _full((1, D)), _full(w.shape)],
        out_specs=[_rows(tr, D), _rows(tr, LAT_W), _rows(tr, D_SSM), _rows(tr, D_XBC), _rows(tr, 128)],
        out_shape=[_sds((lp, D), BF16), _sds((lp, LAT_W), F32), _sds((lp, D_SSM), F32), _sds((lp, D_XBC), F32),
                   _sds((lp, 128), F32)],
        compiler_params=_cp("parallel"))(h0, g, w)


def _rope(x, cos, sa, sb):
    return x * cos + pltpu.roll(x, 96, axis=1) * sa + pltpu.roll(x, 32, axis=1) * sb


def _rope_t(g, cos, sa, sb):
    return g * cos + pltpu.roll(g * sa, 32, axis=1) + pltpu.roll(g * sb, 96, axis=1)


def _mla_prep(lat, qg, kvg, wq, wkv, cos, sa, sb):
    lp = lat.shape[0]
    tr = _rt(lp, MM_ROWS)

    def body(lat_ref, qg_ref, kvg_ref, wq_ref, wkv_ref, cos_ref, sa_ref, sb_ref, q_ref, k_ref, v_ref, ql_ref, kl_ref):
        lat_v = lat_ref[...]
        ql = _rms(lat_v[:, :QR], qg_ref[...]).astype(BF16)
        kl = _rms(lat_v[:, QR:QR + KVR], kvg_ref[...]).astype(BF16)
        ql_ref[...] = ql
        kl_ref[...] = kl
        cos_v, sa_v, sb_v = cos_ref[...], sa_ref[...], sb_ref[...]
        kpe = _rope(lat_v[:, QR + KVR:LAT_W], cos_v, sa_v, sb_v).astype(BF16)
        for h in range(MLA_H):
            q_ref[h, :, 0:DN] = (_dot(ql, wq_ref[:, h * DN:(h + 1) * DN]) * Q_SCALE).astype(BF16)
            qpe = _dot(ql, wq_ref[:, D + h * 128:D + (h + 1) * 128])
            q_ref[h, :, DN:2 * DN] = (_rope(qpe, cos_v, sa_v, sb_v) * Q_SCALE).astype(BF16)
            k_ref[h, :, 0:DN] = _dot(kl, wkv_ref[:, h * DN:(h + 1) * DN]).astype(BF16)
            k_ref[h, :, DN:2 * DN] = kpe
            v_ref[h] = _dot(kl, wkv_ref[:, D + h * DV:D + (h + 1) * DV]).astype(BF16)

    hb = lambda w: pl.BlockSpec((MLA_H, tr, w), lambda i: (0, i, 0))
    return pl.pallas_call(
        body, name="mla_prep", grid=(lp // tr,),
        in_specs=[_rows(tr, LAT_W), _full((1, QR)), _full((1, KVR)), _full(wq.shape), _full(wkv.shape),
                  _rows(tr, 128), _rows(tr, 128), _rows(tr, 128)],
        out_specs=[hb(256), hb(256), hb(128), _rows(tr, QR), _rows(tr, KVR)],
        out_shape=[_sds((MLA_H, lp, 256), BF16), _sds((MLA_H, lp, 256), BF16), _sds((MLA_H, lp, 128), BF16),
                   _sds((lp, QR), BF16), _sds((lp, KVR), BF16)],
        compiler_params=_cp("parallel"))(lat, qg, kvg, wq, wkv, cos, sa, sb)


def _attn_mask(r0, c0, tq, tk, transposed=False):
    if transposed:
        kk = c0 + lax.broadcasted_iota(jnp.int32, (tk, tq), 0)
        qq = r0 + lax.broadcasted_iota(jnp.int32, (tk, tq), 1)
    else:
        qq = r0 + lax.broadcasted_iota(jnp.int32, (tq, tk), 0)
        kk = c0 + lax.broadcasted_iota(jnp.int32, (tq, tk), 1)
    return jnp.logical_and(kk <= qq, kk >= PAD_ROWS)


def _attn_fwd(q, k, v):
    lp = q.shape[1]
    t = _rt(lp, (640, 128))
    nq = lp // t

    hp = 2

    def body(q_ref, k_ref, v_ref, o_ref, lse_ref):
        qi = pl.program_id(1)
        qv = [q_ref[a] for a in range(hp)]

        def tile(kj, carries, masked, live=None):
            kv_rows = pl.ds(pl.multiple_of(kj * t, t), t)
            out = []
            for a in range(hp):
                m, l, acc = carries[a]
                kk = k_ref[a, kv_rows, :]
                vv = v_ref[a, kv_rows, :]
                s = _dot_nt(qv[a], kk)
                if masked:
                    keep = _attn_mask(qi * t, kj * t, t, t)
                    if live is not None:
                        keep = jnp.logical_and(keep, live)
                    s = jnp.where(keep, s, NEG)
                m_new = jnp.maximum(m, jnp.max(s, axis=-1, keepdims=True))
                alpha = jnp.exp2(m - m_new)
                p = jnp.exp2(s - m_new)
                l = alpha * l + jnp.sum(p, axis=-1, keepdims=True)
                acc = alpha * acc + _dot(p.astype(BF16), vv)
                out.append((m_new, l, acc))
            return tuple(out)

        init = tuple((jnp.full((t, 1), NEG, F32), jnp.zeros((t, 1), F32), jnp.zeros((t, DV), F32)) for _ in range(hp))
        carries = tile(0, init, True)
        carries = lax.fori_loop(1, qi, lambda kj, c: tile(kj, c, False), carries)
        carries = tile(qi, carries, True, live=qi > 0)
        for a in range(hp):
            m, l, acc = carries[a]
            o_ref[:, a * DV:(a + 1) * DV] = acc / l
            lse_ref[a] = jnp.broadcast_to(m + jnp.log(l) * LOG2E, (t, 128)).T[:8]

    return pl.pallas_call(
        body, name="attn_fwd", grid=(MLA_H // hp, nq),
        in_specs=[pl.BlockSpec((hp, t, 256), lambda h, i: (h, i, 0)), pl.BlockSpec((hp, lp, 256), lambda h, i: (h, 0, 0)),
                  pl.BlockSpec((hp, lp, 128), lambda h, i: (h, 0, 0))],
        out_specs=[pl.BlockSpec((t, hp * DV), lambda h, i: (i, h)), pl.BlockSpec((hp, 8, t), lambda h, i: (h, 0, i))],
        out_shape=[_sds((lp, MLA_H * DV), F32), _sds((MLA_H, 8, lp), F32)],
        compiler_params=_cp("parallel", "parallel"))(q, k, v)


def _attn_out_bwd(o, g, dmixin):
    lp = o.shape[0]
    tr = _rt(lp, (640, 128))

    def body(o_ref, g_ref, dy_ref, do_ref, dg_ref, dl_ref):
        i = pl.program_id(0)
        ov = o_ref[...]
        do, dg = _rms_bwd(ov, g_ref[...], dy_ref[...])
        do_ref[...] = do
        _acc_rows(dg_ref, dg, i == 0)
        prod = do * ov
        lane = lax.broadcasted_iota(jnp.int32, (1, 128), 1)
        cols = jnp.zeros((tr, 128), F32)
        for h in range(MLA_H):
            cols = cols + jnp.sum(prod[:, h * DV:(h + 1) * DV], axis=-1, keepdims=True) * (lane == h).astype(F32)
        dl_ref[...] = cols.T[:MLA_H]

    return pl.pallas_call(
        body, name="attn_out_bwd", grid=(lp // tr,),
        in_specs=[_rows(tr, D), _full((1, D)), pl.BlockSpec((tr, D), lambda i: (i, 0))],
        out_specs=[_rows(tr, D), _full((1, D)), pl.BlockSpec((MLA_H, tr), lambda i: (0, i))],
        out_shape=[_sds((lp, D), F32), _sds((1, D), F32), _sds((MLA_H, lp), F32)],
        compiler_params=_cp("arbitrary"))(o, g, dmixin)


def _attn_bwd(q, k, v, do, lse_row, delta_row):
    lp = q.shape[1]
    t = _rt(lp, (640, 128))
    nq = lp // t

    def body(q_ref, k_ref, v_ref, do_ref, lse_ref, dl_ref, dq_ref, dk_ref, dv_ref):
        kj = pl.program_id(1)
        kk = k_ref[0]
        vv = v_ref[0]

        @pl.when(kj == 0)
        def _():
            dq_ref[...] = jnp.zeros_like(dq_ref)

        def tile(qi, carry, masked):
            dk, dv = carry
            q_rows = pl.ds(pl.multiple_of(qi * t, t), t)
            qv = q_ref[0, q_rows, :]
            dob = do_ref[q_rows, :].astype(BF16)
            st = _dot_nt(kk, qv)
            if masked:
                st = jnp.where(_attn_mask(qi * t, kj * t, t, t, transposed=True), st, NEG)
            pt = jnp.exp2(st - lse_ref[0, qi])
            dpt = _dot_nt(vv, dob)
            dst = (pt * (dpt - dl_ref[0, qi])).astype(BF16)
            dv = dv + _dot(pt.astype(BF16), dob)
            dk = dk + _dot(dst, qv)
            dq_ref[0, q_rows, :] += _dot_tn(dst, kk)
            return dk, dv

        carry = tile(kj, (jnp.zeros((t, 256), F32), jnp.zeros((t, DV), F32)), True)
        split = jnp.where(kj == 0, nq, kj + 1)
        carry = lax.fori_loop(kj + 1, split, lambda qi, c: tile(qi, c, True), carry)
        dk, dv = lax.fori_loop(split, nq, lambda qi, c: tile(qi, c, False), carry)
        dk_ref[0] = dk * LN2
        dv_ref[0] = dv

    stat = pl.BlockSpec((1, nq, 1, t), lambda h, j: (h, 0, 0, 0))
    return pl.pallas_call(
        body, name="attn_bwd", grid=(MLA_H, nq),
        in_specs=[pl.BlockSpec((1, lp, 256), lambda h, j: (h, 0, 0)), pl.BlockSpec((1, t, 256), lambda h, j: (h, j, 0)),
                  pl.BlockSpec((1, t, 128), lambda h, j: (h, j, 0)), pl.BlockSpec((lp, DV), lambda h, j: (0, h)), stat, stat],
        out_specs=[pl.BlockSpec((1, lp, 256), lambda h, j: (h, 0, 0)), pl.BlockSpec((1, t, 256), lambda h, j: (h, j, 0)),
                   pl.BlockSpec((1, t, 128), lambda h, j: (h, j, 0))],
        out_shape=[_sds((MLA_H, lp, 256), F32), _sds((MLA_H, lp, 256), F32), _sds((MLA_H, lp, 128), F32)],
        compiler_params=_cp("parallel", "arbitrary"))(q, k, v, do, lse_row, delta_row)


def _ssd_consts():
    ri = lax.broadcasted_iota(jnp.int32, (CHUNK, CHUNK), 0)
    ci = lax.broadcasted_iota(jnp.int32, (CHUNK, CHUNK), 1)
    expand = (lax.broadcasted_iota(jnp.int32, (128, D_SSM), 0)
              == lax.broadcasted_iota(jnp.int32, (128, D_SSM), 1) // SSM_P).astype(F32)
    return ri, ci, expand


def _ssd_chunk(c, x_ref, xh_ref, dt_ref, dtT_ref, cw_ref, cb_ref, dtb_ref, dtbT_ref, al_ref, alT_ref):
    ri, ci, expand = _ssd_consts()
    x = x_ref[...]
    halo = jnp.where(c > 0, xh_ref[...], 0.0)
    sh = [x] + [_shift_down(x, halo, j) for j in range(1, SSM_K)]
    cv = cb_ref[...]
    for kk in range(SSM_K):
        cv = cv + cw_ref[kk:kk + 1, :] * sh[SSM_K - 1 - kk]
    xa = _silu(cv)
    grow = c * CHUNK + ri
    gcol = c * CHUNK + lax.broadcasted_iota(jnp.int32, (SSM_H, CHUNK), 1)
    sp = dt_ref[...] + dtb_ref[...]
    spT = dtT_ref[...] + dtbT_ref[...]
    dtc = jnp.where(grow >= PAD_ROWS, _softplus(sp), 0.0)
    dtr = jnp.where(gcol >= PAD_ROWS, _softplus(spT), 0.0)
    arow = -jnp.exp(al_ref[...])
    acolT = -jnp.exp(alT_ref[...])
    ltri = (ci <= ri).astype(F32)
    acs = _dot_hi(ltri, dtc * arow, split="b")
    acsT = _dot_hi(dtr * acolT, (ri <= ci).astype(F32))
    return dict(x=x, sh=sh, cv=cv, xa=xa, sp=sp, dtc=dtc, arow=arow, acs=acs, acsT=acsT, ri=ri, ci=ci, expand=expand,
                grow=grow)


def _ssd_mats(k, s_prev):
    xa, acs, acsT, expand, ri, ci = k["xa"], k["acs"], k["acsT"], k["expand"], k["ri"], k["ci"]
    xs = xa[:, :D_SSM]
    dt_e = _dot_hi(k["dtc"], expand)
    acs_e = _dot_hi(acs, expand)
    last_e = acs_e[CHUNK - 1:CHUNK, :]
    ea = jnp.exp(acs_e)
    f = jnp.exp(last_e - acs_e)
    cd = jnp.exp(last_e)
    xdt = xs * dt_e
    bm = [xa[:, D_SSM + g * SSM_N:D_SSM + (g + 1) * SSM_N] for g in range(SSM_G)]
    cm = [xa[:, D_SSM + (SSM_G + g) * SSM_N:D_SSM + (SSM_G + g + 1) * SSM_N] for g in range(SSM_G)]
    bmb = [b.astype(BF16) for b in bm]
    cmb = [cc.astype(BF16) for cc in cm]
    cb = [_dot_nt(cmb[g], bmb[g]) for g in range(SSM_G)]
    lam, mm = [], []
    for h in range(SSM_H):
        diff = acs[:, h:h + 1] - acsT[h:h + 1, :]
        lam_h = jnp.exp(jnp.where(ci <= ri, diff, NEG))
        lam.append(lam_h)
        mm.append(cb[h // (SSM_H // SSM_G)] * lam_h)
    lo = lax.broadcasted_iota(jnp.int32, (CHUNK, 128), 1) < SSM_P
    xdt_h = []
    for h in range(SSM_H):
        pair = xdt[:, (h // 2) * 128:(h // 2 + 1) * 128]
        xdt_h.append(jnp.where(lo if h % 2 == 0 else jnp.logical_not(lo), pair, 0.0).astype(BF16))
    ydiag = jnp.concatenate(
        [_dot(mm[2 * j].astype(BF16), xdt_h[2 * j]) + _dot(mm[2 * j + 1].astype(BF16), xdt_h[2 * j + 1])
         for j in range(SSM_H // 2)], axis=1)
    t_off = [_dot(cmb[g], s_prev[g].astype(BF16)) for g in range(SSM_G)]
    yoff = jnp.concatenate(t_off, axis=1) * ea
    return dict(xs=xs, dt_e=dt_e, acs_e=acs_e, ea=ea, f=f, cd=cd, xdt=xdt, bm=bm, cm=cm, bmb=bmb, cmb=cmb, cb=cb, lam=lam,
                mm=mm, lo=lo, xdt_h=xdt_h, ydiag=ydiag, t_off=t_off, yoff=yoff)


def _ssd_specs(nc, rev):
    ix = (lambda i: nc - 1 - i) if rev else (lambda i: i)
    return [
        pl.BlockSpec((CHUNK, D_XBC), lambda i: (ix(i), 0)),
        pl.BlockSpec((8, D_XBC), lambda i: (jnp.maximum(ix(i) * (CHUNK // 8) - 1, 0), 0)),
        pl.BlockSpec((CHUNK, D_SSM), lambda i: (ix(i), 0)),
        pl.BlockSpec((CHUNK, 128), lambda i: (ix(i), 0)),
        pl.BlockSpec((SSM_H, CHUNK), lambda i: (0, ix(i))),
        _full((8, D_XBC)), _full((1, D_XBC)), _full((1, 128)), _full((SSM_H, 1)), _full((1, 128)), _full((SSM_H, 1)),
        _full((1, D_SSM)), _full((1, D_SSM)),
    ]


def _ssd_fwd(xbc, z, dtr, dtrT, cw, cb, dtb, dtbT, alog, alogT, d_e, ng):
    lp = xbc.shape[0]
    nc = lp // CHUNK

    def body(x_ref, xh_ref, z_ref, dt_ref, dtT_ref, cw_ref, cb_ref, dtb_ref, dtbT_ref, al_ref, alT_ref, de_ref, ng_ref,
             y_ref, st_ref, s_scr):
        c = pl.program_id(0)

        @pl.when(c == 0)
        def _():
            s_scr[...] = jnp.zeros_like(s_scr)

        k = _ssd_chunk(c, x_ref, xh_ref, dt_ref, dtT_ref, cw_ref, cb_ref, dtb_ref, dtbT_ref, al_ref, alT_ref)
        s_prev = [s_scr[g] for g in range(SSM_G)]
        st_ref[0] = s_scr[...]
        m = _ssd_mats(k, s_prev)
        xd = (m["xdt"] * m["f"]).astype(BF16)
        for g in range(SSM_G):
            sl = slice(g * GSZ, (g + 1) * GSZ)
            s_scr[g] = m["cd"][:, sl] * s_prev[g] + _dot(m["bm"][g].T.astype(BF16), xd[:, sl])
        y = m["ydiag"] + m["yoff"] + de_ref[...] * m["xs"]
        u = y * _silu(z_ref[...])
        outs = []
        for g in range(SSM_G):
            ug = u[:, g * GSZ:(g + 1) * GSZ]
            outs.append(ug * lax.rsqrt(jnp.mean(ug * ug, axis=-1, keepdims=True) + EPS))
        y_ref[...] = jnp.concatenate(outs, axis=1) * ng_ref[...]

    return pl.pallas_call(
        body, name="ssd_fwd", grid=(nc,), in_specs=_ssd_specs(nc, False),
        out_specs=[_rows(CHUNK, D_SSM), pl.BlockSpec((1, SSM_G, SSM_N, GSZ), lambda i: (i, 0, 0, 0))],
        out_shape=[_sds((lp, D_SSM), F32), _sds((nc, SSM_G, SSM_N, GSZ), F32)],
        scratch_shapes=[pltpu.VMEM((SSM_G, SSM_N, GSZ), F32)],
        compiler_params=_cp("arbitrary"))(xbc, xbc, z, dtr, dtrT, cw, cb, dtb, dtbT, alog, alogT, d_e, ng)


def _ssd_bwd(dmixin, xbc, z, dtr, dtrT, st, cw, cb, dtb, dtbT, alog, alogT, d_e, ng):
    lp = xbc.shape[0]
    nc = lp // CHUNK
    hpg = SSM_H // SSM_G

    def body(dy_ref, x_ref, xh_ref, z_ref, dt_ref, dtT_ref, st_ref, cw_ref, cb_ref, dtb_ref, dtbT_ref, al_ref, alT_ref,
             de_ref, ng_ref, dz_ref, dx_ref, ddt_ref, dcw_ref, dcb_ref, ddtb_ref, dal_ref, dd_ref, dng_ref, ds_scr, nx_scr):
        i = pl.program_id(0)
        c = nc - 1 - i
        first = i == 0

        @pl.when(first)
        def _():
            ds_scr[...] = jnp.zeros_like(ds_scr)
            nx_scr[...] = jnp.zeros_like(nx_scr)

        k = _ssd_chunk(c, x_ref, xh_ref, dt_ref, dtT_ref, cw_ref, cb_ref, dtb_ref, dtbT_ref, al_ref, alT_ref)
        s_prev = [st_ref[0, g] for g in range(SSM_G)]
        m = _ssd_mats(k, s_prev)
        ri, ci, expand = k["ri"], k["ci"], k["expand"]
        xs, acs, acsT = m["xs"], k["acs"], k["acsT"]
        zv = z_ref[...]
        dout = dy_ref[...]
        ngv = ng_ref[...]
        y = m["ydiag"] + m["yoff"] + de_ref[...] * xs
        sz = _silu(zv)
        u = y * sz
        du_parts, dng_parts = [], []
        for g in range(SSM_G):
            sl = slice(g * GSZ, (g + 1) * GSZ)
            dug, dngg = _rms_bwd(u[:, sl], ngv[:, sl], dout[:, sl])
            du_parts.append(dug)
            dng_parts.append(dngg)
        du = jnp.concatenate(du_parts, axis=1)
        _acc_rows(dng_ref, jnp.concatenate(dng_parts, axis=1), first)
        dy = du * sz
        dz_ref[...] = du * y * _dsilu(zv)
        dd_e = jnp.sum(dy * xs, axis=0, keepdims=True)
        _acc_rows(dd_ref, _dot_nt_hi(dd_e, expand), first)
        dxs = de_ref[...] * dy
        dacs_e = dy * m["yoff"]
        dtg = (dy * m["ea"]).astype(BF16)
        dxdt = jnp.zeros_like(xs)
        dlast_e = []
        db, dc, ds_prev = [], [], []
        xd = m["xdt"] * m["f"]
        dxd_all = []
        for g in range(SSM_G):
            sl = slice(g * GSZ, (g + 1) * GSZ)
            dsg = ds_scr[g]
            spb = s_prev[g].astype(BF16)
            dc.append(_dot_nt(dtg[:, sl], spb))
            dsp = _dot(m["cm"][g].T.astype(BF16), dtg[:, sl]) + m["cd"][:, sl] * dsg
            ds_prev.append(dsp)
            dlast_e.append(jnp.sum(dsg * s_prev[g], axis=0, keepdims=True) * m["cd"][:, sl])
            dsb = dsg.astype(BF16)
            db.append(_dot_nt(xd[:, sl].astype(BF16), dsb))
            dxd_all.append(_dot(m["bmb"][g], dsb))
        dxd = jnp.concatenate(dxd_all, axis=1)
        dxdt = dxd * m["f"]
        dff = dxd * xd
        dacs_e = dacs_e - dff
        dlast_row = jnp.concatenate(dlast_e, axis=1) + jnp.sum(dff, axis=0, keepdims=True)
        dacs = jnp.zeros((CHUNK, 128), F32)
        lane = lax.broadcasted_iota(jnp.int32, (1, 128), 1)
        cbT = [_dot_nt(m["bmb"][g], m["cmb"][g]) for g in range(SSM_G)]
        dgs = [jnp.zeros((CHUNK, CHUNK), F32) for _ in range(SSM_G)]
        dgTs = [jnp.zeros((CHUNK, CHUNK), F32) for _ in range(SSM_G)]
        dxdt_pairs = []
        for h in range(SSM_H):
            g = h // hpg
            pr = slice((h // 2) * 128, (h // 2 + 1) * 128)
            lo_h = m["lo"] if h % 2 == 0 else jnp.logical_not(m["lo"])
            dyp = jnp.where(lo_h, dy[:, pr], 0.0).astype(BF16)
            xdp = m["xdt"][:, pr].astype(BF16)
            dm = _dot_nt(dyp, xdp)
            dmT = _dot_nt(xdp, dyp)
            lamT = jnp.exp(jnp.where(ri <= ci, acsT[h:h + 1, :] - acs[:, h:h + 1], NEG))
            mT = cbT[g] * lamT
            dgs[g] = dgs[g] + dm * m["lam"][h]
            dgTs[g] = dgTs[g] + dmT * lamT
            v1 = jnp.sum(dm * m["mm"][h], axis=1, keepdims=True)
            v2 = jnp.sum(dmT * mT, axis=1, keepdims=True)
            dacs = dacs + (v1 - v2) * (lane == h).astype(F32)
            part = _dot(mT.astype(BF16), dyp)
            if h % 2 == 0:
                dxdt_pairs.append(part)
            else:
                dxdt_pairs[-1] = dxdt_pairs[-1] + part
        dxdt = dxdt + jnp.concatenate(dxdt_pairs, axis=1)
        for g in range(SSM_G):
            dc[g] = dc[g] + _dot(dgs[g].astype(BF16), m["bmb"][g])
            db[g] = db[g] + _dot(dgTs[g].astype(BF16), m["cmb"][g])
        dacs = dacs + _dot_nt_hi(dacs_e, expand)
        dlast = _dot_nt_hi(dlast_row, expand)
        dacs = dacs + jnp.where(ri == CHUNK - 1, dlast, 0.0)
        dxs = dxs + dxdt * m["dt_e"]
        ddt = _dot_nt_hi(dxdt * xs, expand)
        da = _dot_hi((ri <= ci).astype(F32), dacs, split="b")
        ddt = ddt + da * k["arow"]
        dA = jnp.sum(da * k["dtc"], axis=0, keepdims=True)
        _acc_rows(dal_ref, dA * k["arow"], first)
        ddtr = jnp.where(k["grow"] >= PAD_ROWS, ddt * _sigmoid(k["sp"]), 0.0)
        ddt_ref[...] = ddtr
        _acc_rows(ddtb_ref, jnp.sum(ddtr, axis=0, keepdims=True), first)
        for g in range(SSM_G):
            ds_scr[g] = ds_prev[g]
        dxa = jnp.concatenate([dxs] + db + dc, axis=1)
        dcv = dxa * _dsilu(k["cv"])
        _acc_rows(dcb_ref, jnp.sum(dcv, axis=0, keepdims=True), first)
        dcw_rows = [jnp.sum(dcv * k["sh"][SSM_K - 1 - kk], axis=0, keepdims=True) for kk in range(SSM_K)]
        dcw_rows.append(jnp.zeros((8 - SSM_K, D_XBC), F32))
        _acc_rows(dcw_ref, jnp.concatenate(dcw_rows, axis=0), first)
        nxt = nx_scr[...]
        dx = cw_ref[SSM_K - 1:SSM_K, :] * dcv
        for j in range(1, SSM_K):
            dx = dx + cw_ref[SSM_K - 1 - j:SSM_K - j, :] * _shift_up(dcv, nxt, j)
        grow_x = c * CHUNK + lax.broadcasted_iota(jnp.int32, (CHUNK, D_XBC), 0)
        dx_ref[...] = jnp.where(grow_x >= PAD_ROWS, dx, 0.0)
        nx_scr[...] = dcv[:8]

    specs = _ssd_specs(nc, True)
    in_specs = [pl.BlockSpec((CHUNK, D_SSM), lambda i: (nc - 1 - i, 1))] + specs[:5] + [
        pl.BlockSpec((1, SSM_G, SSM_N, GSZ), lambda i: (nc - 1 - i, 0, 0, 0))] + specs[5:]
    rv = lambda w: pl.BlockSpec((CHUNK, w), lambda i: (nc - 1 - i, 0))
    return pl.pallas_call(
        body, name="ssd_bwd", grid=(nc,), in_specs=in_specs,
        out_specs=[rv(D_SSM), rv(D_XBC), rv(128), _full((8, D_XBC)), _full((1, D_XBC)), _full((1, 128)), _full((1, 128)),
                   _full((1, 128)), _full((1, D_SSM))],
        out_shape=[_sds((lp, D_SSM), F32), _sds((lp, D_XBC), F32), _sds((lp, 128), F32), _sds((8, D_XBC), F32),
                   _sds((1, D_XBC), F32), _sds((1, 128), F32), _sds((1, 128), F32), _sds((1, 128), F32), _sds((1, D_SSM), F32)],
        scratch_shapes=[pltpu.VMEM((SSM_G, SSM_N, GSZ), F32), pltpu.VMEM((8, D_XBC), F32)],
        compiler_params=_cp("arbitrary"))(dmixin, xbc, xbc, z, dtr, dtrT, st, cw, cb, dtb, dtbT, alog, alogT, d_e, ng)


def _mixout_fwd(o, ssm, h0, g_ao, g_post, w):
    lp = o.shape[0]
    tr = _rt(lp, MM_ROWS)

    def body(o_ref, s_ref, h_ref, ga_ref, gp_ref, w_ref, mi_ref, mix_ref, h1_ref):
        mixin = jnp.concatenate([_rms(o_ref[...], ga_ref[...]), s_ref[...]], axis=1).astype(BF16)
        mi_ref[...] = mixin
        mix = _dot(mixin, w_ref[...])
        mix_ref[...] = mix
        grow = pl.program_id(0) * tr + lax.broadcasted_iota(jnp.int32, (tr, D), 0)
        h1_ref[...] = h_ref[...] + jnp.where(grow >= PAD_ROWS, _rms(mix, gp_ref[...]), 0.0)

    return pl.pallas_call(
        body, name="mixout_fwd", grid=(lp // tr,),
        in_specs=[_rows(tr, D), _rows(tr, D), _rows(tr, D), _full((1, D)), _full((1, D)), _full(w.shape)],
        out_specs=[_rows(tr, 2 * D), _rows(tr, D), _rows(tr, D)],
        out_shape=[_sds((lp, 2 * D), BF16), _sds((lp, D), F32), _sds((lp, D), F32)],
        compiler_params=_cp("parallel"))(o, ssm, h0, g_ao, g_post, w)


def _ffn_up(h1, g, w):
    lp = h1.shape[0]
    tr = _rt(lp, MM_ROWS)
    tn = D_FF // 2

    def body(h_ref, g_ref, w_ref, hn_ref, u_ref):
        hn = _rms(h_ref[...], g_ref[...]).astype(BF16)
        hn_ref[...] = hn
        u_ref[...] = _dot(hn, w_ref[...])

    return pl.pallas_call(
        body, name="ffn_up", grid=(lp // tr, 2 * D_FF // tn),
        in_specs=[pl.BlockSpec((tr, D), lambda i, j: (i, 0)), _full((1, D)), pl.BlockSpec((None, D, tn), lambda i, j: (j, 0, 0))],
        out_specs=[pl.BlockSpec((tr, D), lambda i, j: (i, 0)), pl.BlockSpec((tr, tn), lambda i, j: (i, j))],
        out_shape=[_sds((lp, D), BF16), _sds((lp, 2 * D_FF), F32)],
        compiler_params=_cp("parallel", "arbitrary"))(h1, g, w)


def _ffn_dhn(du, w4):
    lp = du.shape[0]
    nch, _, tn = w4.shape
    tr = _rt(lp, MM_ROWS)

    def body(du_ref, w_ref, o_ref):
        acc = _dot_nt(du_ref[:, 0:tn], w_ref[0])
        for j in range(1, nch):
            acc = acc + _dot_nt(du_ref[:, j * tn:(j + 1) * tn], w_ref[j])
        o_ref[...] = acc

    return pl.pallas_call(
        body, name="ffn_dhn", grid=(lp // tr,), in_specs=[_rows(tr, nch * tn), _full(w4.shape)], out_specs=_rows(tr, D),
        out_shape=_sds((lp, D), F32), compiler_params=_cp("parallel"))(du, w4)


FFN_CB = 256


def _ffn_gate(u, cw, cb):
    lp = u.shape[0]
    tr = _rt(lp, (320, 128))

    def body(u_ref, uh_ref, cw_ref, cb_ref, a_ref):
        i = pl.program_id(0)
        for j in range(D_FF // FFN_CB):
            halves = []
            for off in (0, D_FF):
                sl = slice(off + j * FFN_CB, off + (j + 1) * FFN_CB)
                x = u_ref[:, sl]
                halo = jnp.where(i > 0, uh_ref[:, sl], 0.0)
                cv = cb_ref[:, sl] + cw_ref[FFN_K - 1:FFN_K, sl] * x
                for s in range(1, FFN_K):
                    cv = cv + cw_ref[FFN_K - 1 - s:FFN_K - s, sl] * _shift_down(x, halo, s)
                halves.append(cv)
            a_ref[:, j * FFN_CB:(j + 1) * FFN_CB] = (_silu(halves[0]) * halves[1]).astype(BF16)

    return pl.pallas_call(
        body, name="ffn_gate", grid=(lp // tr,),
        in_specs=[_rows(tr, 2 * D_FF), pl.BlockSpec((8, 2 * D_FF), lambda i: (jnp.maximum(i * (tr // 8) - 1, 0), 0)),
                  _full((8, 2 * D_FF)), _full((1, 2 * D_FF))],
        out_specs=_rows(tr, D_FF), out_shape=_sds((lp, D_FF), BF16),
        compiler_params=_cp("parallel"))(u, u, cw, cb)


def _ffn_down(a, w, h1, tgt, g_post):
    lp = a.shape[0]
    tr = _rt(lp, MM_ROWS)

    def body(a_ref, w_ref, h_ref, t_ref, g_ref, dh2_ref, dd_ref, dg_ref, loss_ref):
        i = pl.program_id(0)
        d = _dot(a_ref[...], w_ref[...])
        gv = g_ref[...]
        h2 = h_ref[...] + _rms(d, gv)
        grow = i * tr + lax.broadcasted_iota(jnp.int32, (tr, D), 0)
        err = jnp.where(grow >= FRONT, h2 - t_ref[...], 0.0)
        dh2 = err * (1.0 / D)
        dh2_ref[...] = dh2
        dd, dg = _rms_bwd(d, gv, dh2)
        dd_ref[...] = dd.astype(BF16)
        _acc_rows(dg_ref, dg, i == 0)
        part = 0.5 * jnp.sum(jnp.sum(err * err, axis=1, keepdims=True), axis=0, keepdims=True) * (1.0 / D)
        _acc_rows(loss_ref, jnp.broadcast_to(part, (8, 128)), i == 0)

    return pl.pallas_call(
        body, name="ffn_down", grid=(lp // tr,),
        in_specs=[_rows(tr, D_FF), _full(w.shape), _rows(tr, D), _rows(tr, D), _full((1, D))],
        out_specs=[_rows(tr, D), _rows(tr, D), _full((1, D)), _full((8, 128))],
        out_shape=[_sds((lp, D), F32), _sds((lp, D), BF16), _sds((1, D), F32), _sds((8, 128), F32)],
        compiler_params=_cp("arbitrary"))(a, w, h1, tgt, g_post)


def _ffn_gate_bwd(u, da, cw, cb):
    lp = u.shape[0]
    tr = _rt(lp, (320, 128))
    n = lp // tr

    def body(u_ref, uh_ref, da_ref, cw_ref, cb_ref, du_ref, dcw_ref, dcb_ref, nx_scr):
        i = pl.program_id(0)
        t = n - 1 - i
        first = i == 0

        @pl.when(first)
        def _():
            nx_scr[...] = jnp.zeros_like(nx_scr)

        grow = t * tr + lax.broadcasted_iota(jnp.int32, (tr, FFN_CB), 0)
        for j in range(D_FF // FFN_CB):
            cvs, shs, sls = [], [], []
            for off in (0, D_FF):
                sl = slice(off + j * FFN_CB, off + (j + 1) * FFN_CB)
                x = u_ref[:, sl]
                halo = jnp.where(t > 0, uh_ref[:, sl], 0.0)
                sh = [x] + [_shift_down(x, halo, s) for s in range(1, FFN_K)]
                cv = cb_ref[:, sl]
                for kk in range(FFN_K):
                    cv = cv + cw_ref[kk:kk + 1, sl] * sh[FFN_K - 1 - kk]
                cvs.append(cv)
                shs.append(sh)
                sls.append(sl)
            dav = da_ref[:, j * FFN_CB:(j + 1) * FFN_CB]
            dcv = (dav * cvs[1] * _dsilu(cvs[0]), dav * _silu(cvs[0]))
            for hf in range(2):
                sl = sls[hf]
                g = dcv[hf]
                rows = [jnp.sum(g * shs[hf][FFN_K - 1 - kk], axis=0, keepdims=True) for kk in range(FFN_K)]
                rows.append(jnp.zeros((8 - FFN_K, FFN_CB), F32))
                upd_w = jnp.concatenate(rows, axis=0)
                upd_b = jnp.sum(g, axis=0, keepdims=True)

                @pl.when(first)
                def _():
                    dcw_ref[:, sl] = upd_w
                    dcb_ref[:, sl] = upd_b

                @pl.when(jnp.logical_not(first))
                def _():
                    dcw_ref[:, sl] += upd_w
                    dcb_ref[:, sl] += upd_b

                nxt = nx_scr[:, sl]
                du = cw_ref[FFN_K - 1:FFN_K, sl] * g
                for s in range(1, FFN_K):
                    du = du + cw_ref[FFN_K - 1 - s:FFN_K - s, sl] * _shift_up(g, nxt, s)
                du_ref[:, sl] = jnp.where(grow >= PAD_ROWS, du, 0.0).astype(BF16)
                nx_scr[:, sl] = g[:8]

    return pl.pallas_call(
        body, name="ffn_gate_bwd", grid=(n,),
        in_specs=[pl.BlockSpec((tr, 2 * D_FF), lambda i: (n - 1 - i, 0)),
                  pl.BlockSpec((8, 2 * D_FF), lambda i: (jnp.maximum((n - 1 - i) * (tr // 8) - 1, 0), 0)),
                  pl.BlockSpec((tr, D_FF), lambda i: (n - 1 - i, 0)), _full((8, 2 * D_FF)), _full((1, 2 * D_FF))],
        out_specs=[pl.BlockSpec((tr, 2 * D_FF), lambda i: (n - 1 - i, 0)), _full((8, 2 * D_FF)), _full((1, 2 * D_FF))],
        out_shape=[_sds((lp, 2 * D_FF), BF16), _sds((8, 2 * D_FF), F32), _sds((1, 2 * D_FF), F32)],
        scratch_shapes=[pltpu.VMEM((8, 2 * D_FF), F32)],
        compiler_params=_cp("arbitrary"))(u, u, da, cw, cb)


def _norm_bwd_res(x, g, dy, res, name, mask_pad=False, out_dtype=F32):
    lp, c = x.shape
    tr = _rt(lp, (640, 128))

    def body(*refs):
        if res is None:
            x_ref, g_ref, dy_ref, o_ref, dg_ref = refs
        else:
            x_ref, g_ref, dy_ref, r_ref, o_ref, dg_ref = refs
        i = pl.program_id(0)
        dyv = dy_ref[...].astype(F32)
        if mask_pad:
            grow = i * tr + lax.broadcasted_iota(jnp.int32, (tr, c), 0)
            dyv = jnp.where(grow >= PAD_ROWS, dyv, 0.0)
        dx, dg = _rms_bwd(x_ref[...].astype(F32), g_ref[...], dyv)
        if res is not None:
            dx = dx + r_ref[...]
        o_ref[...] = dx.astype(out_dtype)
        _acc_rows(dg_ref, dg, i == 0)

    args = [x, g, dy] + ([] if res is None else [res])
    in_specs = [_rows(tr, c), _full((1, c)), pl.BlockSpec((tr, c), lambda i: (i, 0))] + ([] if res is None else [_rows(tr, c)])
    return pl.pallas_call(
        body, name=name, grid=(lp // tr,), in_specs=in_specs, out_specs=[_rows(tr, c), _full((1, c))],
        out_shape=[_sds((lp, c), out_dtype), _sds((1, c), F32)], compiler_params=_cp("arbitrary"))(*args)


def _mla_bwd(dq, dk, dv, lat, qg, kvg, wq, wkv, cos, sa, sb):
    lp = lat.shape[0]
    tr = _rt(lp, (320, 128))

    def body(dq_ref, dk_ref, dv_ref, lat_ref, qg_ref, kvg_ref, wq_ref, wkv_ref, cos_ref, sa_ref, sb_ref,
             dqf_ref, dkvf_ref, dlat_ref, dqg_ref, dkvg_ref):
        i = pl.program_id(0)
        cos_v, sa_v, sb_v = cos_ref[...], sa_ref[...], sb_ref[...]
        dkpe = jnp.zeros((tr, 128), F32)
        for h in range(MLA_H):
            dqh = dq_ref[h] * SOFTMAX_SCALE
            dqf_ref[:, h * DN:(h + 1) * DN] = dqh[:, :DN].astype(BF16)
            dqf_ref[:, D + h * 128:D + (h + 1) * 128] = _rope_t(dqh[:, DN:], cos_v, sa_v, sb_v).astype(BF16)
            dkh = dk_ref[h]
            dkvf_ref[:, h * DN:(h + 1) * DN] = dkh[:, :DN].astype(BF16)
            dkpe = dkpe + dkh[:, DN:]
            dkvf_ref[:, D + h * DV:D + (h + 1) * DV] = dv_ref[h].astype(BF16)
        dql = _dot_nt(dqf_ref[...], wq_ref[...])
        dkl = _dot_nt(dkvf_ref[...], wkv_ref[...])
        lat_v = lat_ref[...]
        dqc, dqg = _rms_bwd(lat_v[:, :QR], qg_ref[...], dql)
        dkc, dkg = _rms_bwd(lat_v[:, QR:QR + KVR], kvg_ref[...], dkl)
        dlat_ref[:, :QR] = dqc
        dlat_ref[:, QR:QR + KVR] = dkc
        dlat_ref[:, QR + KVR:] = _rope_t(dkpe, cos_v, sa_v, sb_v)
        _acc_rows(dqg_ref, dqg, i == 0)
        _acc_rows(dkvg_ref, dkg, i == 0)

    hb = lambda w: pl.BlockSpec((MLA_H, tr, w), lambda i: (0, i, 0))
    return pl.pallas_call(
        body, name="mla_bwd", grid=(lp // tr,),
        in_specs=[hb(256), hb(256), hb(128), _rows(tr, LAT_W), _full((1, QR)), _full((1, KVR)), _full(wq.shape),
                  _full(wkv.shape), _rows(tr, 128), _rows(tr, 128), _rows(tr, 128)],
        out_specs=[_rows(tr, 2 * D), _rows(tr, 2 * D), _rows(tr, LAT_W), _full((1, QR)), _full((1, KVR))],
        out_shape=[_sds((lp, 2 * D), BF16), _sds((lp, 2 * D), BF16), _sds((lp, LAT_W), F32), _sds((1, QR), F32),
                   _sds((1, KVR), F32)],
        compiler_params=_cp("arbitrary"))(dq, dk, dv, lat, qg, kvg, wq, wkv, cos, sa, sb)


def _inproj_bwd(dlat, dz, dxbc, ddt, w, h0, g, dh1):
    lp = h0.shape[0]
    tr = _rt(lp, (320, 128))
    segs = ((0, LAT_W), (LAT_W, LAT_W + D_SSM), (LAT_W + D_SSM, LAT_W + D_SSM + D_XBC), (IN_P - 128, IN_P))

    def body(dl_ref, dz_ref, dx_ref, dt_ref, w_ref, h_ref, g_ref, r_ref, o_ref, dg_ref):
        dhn = jnp.zeros((tr, D), F32)
        for ref, (a, b) in zip((dl_ref, dz_ref, dx_ref, dt_ref), segs):
            dhn = dhn + _dot_nt(ref[...].astype(BF16), w_ref[:, a:b])
        dx, dg = _rms_bwd(h_ref[...], g_ref[...], dhn)
        o_ref[...] = dx + r_ref[...]
        _acc_rows(dg_ref, dg, pl.program_id(0) == 0)

    return pl.pallas_call(
        body, name="inproj_bwd", grid=(lp // tr,),
        in_specs=[_rows(tr, LAT_W), _rows(tr, D_SSM), _rows(tr, D_XBC), _rows(tr, 128), _full(w.shape), _rows(tr, D),
                  _full((1, D)), _rows(tr, D)],
        out_specs=[_rows(tr, D), _full((1, D))], out_shape=[_sds((lp, D), F32), _sds((1, D), F32)],
        compiler_params=_cp("arbitrary"))(dlat, dz, dxbc, ddt, w, h0, g, dh1)


def _rope_tables(lp):
    pos = (jnp.arange(lp, dtype=jnp.int32) - PAD_ROWS).astype(F32)
    inv = ROPE_THETA ** (-jnp.arange(0, DR, 2, dtype=F32) / DR)
    ang = pos[:, None] * inv[None, :]
    cos, sin = jnp.cos(ang), jnp.sin(ang)
    zero = jnp.zeros_like(sin)
    cos128 = jnp.concatenate([cos, cos, cos, cos], axis=1)
    sa128 = jnp.concatenate([-sin, zero, -sin, zero], axis=1)
    sb128 = jnp.concatenate([zero, sin, zero, sin], axis=1)
    return cos128, sa128, sb128


def _pad_rows8(w):
    return jnp.concatenate([w, jnp.zeros((8 - w.shape[0], w.shape[1]), w.dtype)], axis=0)


def _lane_pad(v):
    return jnp.concatenate([v, jnp.zeros((v.shape[0], 128 - v.shape[1]), v.dtype)], axis=1)


def _device_step(x, tgt, meta, p):
    s = x.shape[0]
    lp = s + FRONT
    zpad = jnp.zeros((PAD_ROWS, D), F32)
    h0 = jnp.concatenate([zpad, meta, x], axis=0)
    tgt_p = jnp.concatenate([jnp.zeros((FRONT, D), F32), tgt], axis=0)
    cos, sa, sb = _rope_tables(lp)

    w_in = p["w_in"]
    w_in_p = jnp.concatenate([w_in[:, :QR + KVR + DR], jnp.zeros((D, 64), BF16), w_in[:, QR + KVR + DR:],
                              jnp.zeros((D, 128 - SSM_H), BF16)], axis=1)
    w_uq = p["w_uq"]
    wq_p = jnp.concatenate([w_uq[:, :, :DN].reshape(QR, MLA_H * DN),
                            jnp.concatenate([w_uq[:, :, DN:], jnp.zeros((QR, MLA_H, 128 - DR), BF16)], axis=2).reshape(QR, MLA_H * 128)],
                           axis=1)
    w_ukv = p["w_ukv"]
    wkv_p = jnp.concatenate([w_ukv[:, :, :DN].reshape(KVR, MLA_H * DN), w_ukv[:, :, DN:].reshape(KVR, MLA_H * DV)], axis=1)
    scw = _pad_rows8(p["ssm_conv_w"])
    fcw = _pad_rows8(p["ffn_conv_w"])
    dtb, alog = _lane_pad(p["ssm_dt_bias"]), _lane_pad(p["ssm_A_log"])
    dtbT, alogT = p["ssm_dt_bias"].reshape(SSM_H, 1), p["ssm_A_log"].reshape(SSM_H, 1)
    d_e = jnp.repeat(p["ssm_D"], SSM_P, axis=1)

    hn, lat, z, xbc, dtr = _inproj(h0, p["norm_mix_pre"], w_in_p)
    dtrT = dtr[:, :SSM_H].T
    q, k, v, qlat, kvlat = _mla_prep(lat, p["q_a_norm"], p["kv_a_norm"], wq_p, wkv_p, cos, sa, sb)
    o, lse = _attn_fwd(q, k, v)
    ssm, st = _ssd_fwd(xbc, z, dtr, dtrT, scw, p["ssm_conv_b"], dtb, dtbT, alog, alogT, d_e, p["ssm_norm"])
    mixin, mix, h1 = _mixout_fwd(o, ssm, h0, p["attn_out_norm"], p["norm_mix_post"], p["w_out"])
    hn2, u = _ffn_up(h1, p["norm_ffn_pre"], p["w_up"])
    a = _ffn_gate(u, fcw, p["ffn_conv_b"])
    dh2, dd, g_ffn_post, loss = _ffn_down(a, p["w_down"], h1, tgt_p, p["norm_ffn_post"])

    da = _mm_nt(dd, p["w_down"], "ffn_da")
    g_w_down = _mm_tn(a, dd, "ffn_dw_down", tn=512)
    du, g_fcw, g_fcb = _ffn_gate_bwd(u, da, fcw, p["ffn_conv_b"])
    dhn2 = _ffn_dhn(du, p["w_up"])
    g_w_up = _mm_tn(hn2, du, "ffn_dw_up", tn=D_FF // 2, chunked=True)
    dh1, g_ffn_pre = _norm_bwd_res(h1, p["norm_ffn_pre"], dhn2, dh2, "ffn_norm_bwd")
    dmix, g_mix_post = _norm_bwd_res(mix, p["norm_mix_post"], dh1, None, "mix_post_bwd", mask_pad=True, out_dtype=BF16)
    dmixin = _mm_nt(dmix, p["w_out"], "mix_dmixin")
    g_w_out = _mm_tn(mixin, dmix, "mix_dw_out", tn=512)
    do, g_ao, delta = _attn_out_bwd(o, p["attn_out_norm"], dmixin)
    t = _rt(lp, (640, 128))
    dq, dk, dv = _attn_bwd(q, k, v, do, lse[:, 0, :].reshape(MLA_H, lp // t, 1, t), delta.reshape(MLA_H, lp // t, 1, t))
    dqf, dkvf, dlat, g_qa, g_kva = _mla_bwd(dq, dk, dv, lat, p["q_a_norm"], p["kv_a_norm"], wq_p, wkv_p, cos, sa, sb)
    g_wq_p = _mm_tn(qlat, dqf, "mla_dw_uq")
    g_wkv_p = _mm_tn(kvlat, dkvf, "mla_dw_ukv")
    dz, dxbc, ddtr, g_scw, g_scb, g_dtb, g_alog, g_dd, g_ssm_norm = _ssd_bwd(
        dmixin, xbc, z, dtr, dtrT, st, scw, p["ssm_conv_b"], dtb, dtbT, alog, alogT, d_e, p["ssm_norm"])
    dh0, g_mix_pre = _inproj_bwd(dlat, dz, dxbc, ddtr, w_in_p, h0, p["norm_mix_pre"], dh1)
    g_in_p = jnp.concatenate([_mm_tn(hn, dlat, "in_dw_lat"), _mm_tn(hn, dz, "in_dw_z"), _mm_tn(hn, dxbc, "in_dw_xbc"),
                              _mm_tn(hn, ddtr, "in_dw_dt")], axis=1)

    g_w_in = jnp.concatenate([g_in_p[:, :QR + KVR + DR], g_in_p[:, LAT_W:LAT_W + D_SSM + D_XBC + SSM_H]], axis=1)
    g_w_uq = jnp.concatenate([g_wq_p[:, :D].reshape(QR, MLA_H, DN), g_wq_p[:, D:].reshape(QR, MLA_H, 128)[:, :, :DR]], axis=2)
    g_w_ukv = jnp.concatenate([g_wkv_p[:, :D].reshape(KVR, MLA_H, DN), g_wkv_p[:, D:].reshape(KVR, MLA_H, DV)], axis=2)
    grads = dict(
        norm_mix_pre=g_mix_pre, norm_mix_post=g_mix_post, norm_ffn_pre=g_ffn_pre, norm_ffn_post=g_ffn_post, w_in=g_w_in,
        q_a_norm=g_qa, w_uq=g_w_uq, kv_a_norm=g_kva, w_ukv=g_w_ukv, attn_out_norm=g_ao, ssm_conv_w=g_scw[:SSM_K],
        ssm_conv_b=g_scb, ssm_dt_bias=g_dtb[:, :SSM_H], ssm_A_log=g_alog[:, :SSM_H], ssm_D=g_dd[:, :SSM_H],
        ssm_norm=g_ssm_norm, w_out=g_w_out, w_up=g_w_up, ffn_conv_w=g_fcw[:FFN_K], ffn_conv_b=g_fcb, w_down=g_w_down)
    return loss, dh0[FRONT:], dh0[PAD_ROWS:FRONT], grads


N_CHIPS = 4
BIG = (("w_in", (D, D_IN // N_CHIPS)), ("w_uq", (QR // N_CHIPS, MLA_H, DN + DR)), ("w_ukv", (KVR // N_CHIPS, MLA_H, DN + DV)),
       ("w_out", (2 * D // N_CHIPS, D)), ("w_up", (D, 2 * D_FF // N_CHIPS)), ("w_down", (D_FF // N_CHIPS, D)))
BIG_AXIS = dict(w_in=1, w_uq=0, w_ukv=0, w_out=0, w_up=1, w_down=0)
SHARD_ELEMS = sum(functools.reduce(lambda a, b: a * b, s) for _, s in BIG)
HALF_ROWS = SHARD_ELEMS // 256
SMALL_SHARDED = (("meta_tokens", (N_META, D // N_CHIPS)), ("ssm_conv_w", (SSM_K, D_XBC // N_CHIPS)),
                 ("ffn_conv_w", (FFN_K, 2 * D_FF // N_CHIPS)))
SMALL_REPL = (("norm_mix_pre", D), ("norm_mix_post", D), ("norm_ffn_pre", D), ("norm_ffn_post", D), ("q_a_norm", QR),
              ("kv_a_norm", KVR), ("attn_out_norm", D), ("ssm_conv_b", D_XBC), ("ssm_dt_bias", SSM_H), ("ssm_A_log", SSM_H),
              ("ssm_D", SSM_H), ("ssm_norm", D_SSM), ("ffn_conv_b", 2 * D_FF))
ANY = pl.BlockSpec(memory_space=pl.ANY)


def _pad128(v):
    n = v.shape[0]
    return jnp.concatenate([v, jnp.zeros(((-n) % 128,), v.dtype)]) if n % 128 else v


def _pack_rows(vs, rows):
    flat = jnp.concatenate([_pad128(v.reshape(-1)) for v in vs])
    flat = jnp.concatenate([flat, jnp.zeros((rows * 128 - flat.shape[0],), flat.dtype)])
    return flat.reshape(rows, 128)


def _unpack_rows(pack, sizes):
    flat = pack.reshape(-1)
    out, off = [], 0
    for n in sizes:
        out.append(flat[off:off + n])
        off += n + (-n) % 128
    return out


def _my_place():
    return lax.axis_index("x"), lax.axis_index("y"), lax.axis_index("c")


def _other_chips(x, y):
    return [(1 - x, y), (x, 1 - y), (1 - x, 1 - y)]


def _remote(src, dst, send, recv, dev):
    return pltpu.make_async_remote_copy(src_ref=src, dst_ref=dst, send_sem=send, recv_sem=recv, device_id=dev,
                                        device_id_type=MESH)


SMALL_AG_ROWS = 80


def _gather_weights(shards, small, name):
    arrs = list(shards) + ([] if small is None else [small])
    n, nb = len(arrs), len(shards)

    def body(*refs):
        ins, outs = refs[:n], refs[n:2 * n]
        send, recv, lsem = refs[2 * n:]
        x, y, c = _my_place()
        me = 2 * x + y
        chips = _other_chips(x, y)
        slot = lambda w, chip, cc: outs[w].at[chip, cc] if w < nb else outs[w].at[chip]
        mine = lambda w: slot(w, me, c) if w < nb else ins[w]
        loc = [pltpu.make_async_copy(ins[w], outs[w].at[me], lsem.at[w - nb]) for w in range(nb, n)]
        for cp in loc:
            cp.start()
        sends = []
        for w in range(n):
            for kk, (cx, cy) in enumerate(chips):
                sends.append(_remote(mine(w), slot(w, me, c), send.at[3 * w + kk], recv.at[3 * w + kk], (cx, cy, c)))
        for cp in sends:
            cp.start()
        for w in range(nb):
            for kk, (cx, cy) in enumerate(chips):
                src = 2 * cx + cy
                _remote(mine(w), slot(w, src, c), send.at[3 * w + kk], recv.at[3 * w + kk], (cx, cy, c)).wait_recv()
                fwd = _remote(slot(w, src, c), slot(w, src, c), send.at[3 * (n + w) + kk], recv.at[3 * (n + w) + kk], (x, y, 1 - c))
                fwd.start()
                sends.append(fwd)
        for w in range(n):
            for kk, (cx, cy) in enumerate(chips):
                src = 2 * cx + cy
                if w < nb:
                    _remote(mine(w), slot(w, src, 1 - c), send.at[3 * (n + w) + kk], recv.at[3 * (n + w) + kk],
                            (x, y, 1 - c)).wait_recv()
                else:
                    _remote(ins[w], slot(w, src, c), send.at[3 * w + kk], recv.at[3 * w + kk], (cx, cy, c)).wait_recv()
        for cp in sends:
            cp.wait_send()
        for cp in loc:
            cp.wait()

    return pl.pallas_call(
        body, name=name, in_specs=[ANY] * n, out_specs=[ANY] * n,
        out_shape=[_sds(a.shape, a.dtype) for a in shards] + ([] if small is None else [_sds((N_CHIPS,) + small.shape, small.dtype)]),
        input_output_aliases={w: w for w in range(nb)},
        scratch_shapes=[pltpu.SemaphoreType.DMA((3 * (n + nb),)), pltpu.SemaphoreType.DMA((3 * (n + nb),)),
                        pltpu.SemaphoreType.DMA((max(n - nb, 1),))])(*arrs)


def _place_own(wt, chip, name):
    r, c = wt.shape
    tr = _row_tile(r, c)

    def body(c_ref, w_ref, o_ref):
        o_ref[...] = w_ref[...].astype(BF16)

    return pl.pallas_call(
        body, name=name, out_shape=_sds((N_CHIPS, r, c), BF16),
        grid_spec=pltpu.PrefetchScalarGridSpec(
            num_scalar_prefetch=1, grid=(r // tr,), in_specs=[pl.BlockSpec((tr, c), lambda i, cr: (i, 0))],
            out_specs=pl.BlockSpec((None, tr, c), lambda i, cr: (cr[0], i, 0))),
        compiler_params=_cp("parallel"))(chip, wt)


def _send_sibling_halves(gs):
    n = len(gs)

    def body(*refs):
        ins, outs, send, recv = refs[:n], refs[n:2 * n], refs[2 * n], refs[2 * n + 1]
        x, y, c = _my_place()
        cps = [_remote(ins[w].at[:, 1 - c], outs[w], send.at[w], recv.at[w], (x, y, 1 - c)) for w in range(n)]
        for cp in cps:
            cp.start()
        for cp in cps:
            cp.wait()

    return pl.pallas_call(
        body, name="reduce_sibling", in_specs=[ANY] * n, out_specs=[ANY] * n,
        out_shape=[_sds((g.shape[0],) + g.shape[2:], g.dtype) for g in gs],
        scratch_shapes=[pltpu.SemaphoreType.DMA((n,)), pltpu.SemaphoreType.DMA((n,))])(*gs)


def _exchange_chips(ps):
    n = len(ps)

    def body(*refs):
        ins, outs, send, recv = refs[:n], refs[n:2 * n], refs[2 * n], refs[2 * n + 1]
        x, y, c = _my_place()
        cps = [_remote(ins[w].at[2 * cx + cy], outs[w].at[kk], send.at[3 * w + kk], recv.at[3 * w + kk], (cx, cy, c))
               for w in range(n) for kk, (cx, cy) in enumerate(_other_chips(x, y))]
        for cp in cps:
            cp.start()
        for cp in cps:
            cp.wait()

    return pl.pallas_call(
        body, name="exchange_chips", in_specs=[ANY] * n, out_specs=[ANY] * n,
        out_shape=[_sds((3,) + p.shape[1:], p.dtype) for p in ps],
        scratch_shapes=[pltpu.SemaphoreType.DMA((3 * n,)), pltpu.SemaphoreType.DMA((3 * n,))])(*ps)


def _share_sibling(halves):
    n = len(halves)

    def body(*refs):
        outs, send, recv = refs[n:2 * n], refs[2 * n], refs[2 * n + 1]
        x, y, c = _my_place()
        cps = [_remote(outs[w].at[c], outs[w].at[c], send.at[w], recv.at[w], (x, y, 1 - c)) for w in range(n)]
        for cp in cps:
            cp.start()
        for w in range(n):
            cps[w].wait_send()
            _remote(outs[w].at[c], outs[w].at[1 - c], send.at[w], recv.at[w], (x, y, 1 - c)).wait_recv()

    return pl.pallas_call(
        body, name="share_sibling", in_specs=[ANY] * n, out_specs=[ANY] * n,
        out_shape=[_sds(h.shape, h.dtype) for h in halves], input_output_aliases={w: w for w in range(n)},
        scratch_shapes=[pltpu.SemaphoreType.DMA((n,)), pltpu.SemaphoreType.DMA((n,))])(*halves)


def _row_tile(r, c, cap=1 << 20):
    return next(t for t in range(r, 0, -1) if r % t == 0 and (t % 8 == 0 or t == r) and t * c * 4 <= cap)


def _add_pair(g, t, core, name):
    _, _, r, c = g.shape
    tr = _row_tile(r, c)

    def body(c_ref, g_ref, t_ref, o_ref):
        o_ref[...] = (g_ref[...] + t_ref[...]).astype(BF16)

    return pl.pallas_call(
        body, name=name, out_shape=_sds(t.shape, BF16),
        grid_spec=pltpu.PrefetchScalarGridSpec(
            num_scalar_prefetch=1, grid=(N_CHIPS, r // tr),
            in_specs=[pl.BlockSpec((None, None, tr, c), lambda j, i, cr: (j, cr[0], i, 0)),
                      pl.BlockSpec((None, tr, c), lambda j, i, cr: (j, i, 0))],
            out_specs=pl.BlockSpec((None, tr, c), lambda j, i, cr: (j, i, 0))),
        compiler_params=_cp("parallel", "parallel"))(core, g, t)


def _add_chips(p, got, chip, name):
    _, r, c = p.shape
    tr = _row_tile(r, c)

    def body(c_ref, p_ref, g_ref, o_ref):
        o_ref[...] = ((p_ref[...].astype(F32) + g_ref[0].astype(F32)) + g_ref[1].astype(F32)) + g_ref[2].astype(F32)

    return pl.pallas_call(
        body, name=name, out_shape=_sds((2, r, c), F32),
        grid_spec=pltpu.PrefetchScalarGridSpec(
            num_scalar_prefetch=1, grid=(r // tr,),
            in_specs=[pl.BlockSpec((None, tr, c), lambda i, cr: (cr[0], i, 0)), pl.BlockSpec((3, tr, c), lambda i, cr: (0, i, 0))],
            out_specs=pl.BlockSpec((None, tr, c), lambda i, cr: (cr[1], i, 0))),
        compiler_params=_cp("parallel"))(chip, p, got)


SMALL_AR_ROWS = 424


def _allreduce_small(v):
    def body(v_ref, o_ref, gath, send, recv):
        x, y, c = _my_place()
        me = 4 * x + 2 * y + c
        gath[me] = v_ref[...]
        cps = []
        for dd in range(1, 8):
            dx, dy, dc = dd >> 2, (dd >> 1) & 1, dd & 1
            peer = (1 - x if dx else x, 1 - y if dy else y, 1 - c if dc else c)
            cps.append(_remote(v_ref, gath.at[me], send.at[dd - 1], recv.at[dd - 1], peer))
        for cp in cps:
            cp.start()
        for cp in cps:
            cp.wait()
        acc = gath[0]
        for dev in range(1, 8):
            acc = acc + gath[dev]
        o_ref[...] = acc

    vm = pl.BlockSpec(memory_space=pltpu.VMEM)
    return pl.pallas_call(
        body, name="allreduce_small", in_specs=[vm], out_specs=vm, out_shape=_sds(v.shape, F32),
        scratch_shapes=[pltpu.VMEM((8,) + v.shape, F32), pltpu.SemaphoreType.DMA((7,)), pltpu.SemaphoreType.DMA((7,))])(v)


def _adamw(w, g, m, v, name):
    r, c = w.shape
    tr = _row_tile(r, c)

    def body(w_ref, g_ref, m_ref, v_ref, d_ref, m2_ref, v2_ref):
        gv = g_ref[...]
        m2 = ADAM_B1 * m_ref[...] + (1.0 - ADAM_B1) * gv
        v2 = ADAM_B2 * v_ref[...] + (1.0 - ADAM_B2) * jnp.square(gv)
        m_hat = m2 / (1.0 - ADAM_B1 ** ADAM_STEP)
        v_hat = v2 / (1.0 - ADAM_B2 ** ADAM_STEP)
        d_ref[...] = -ADAM_LR * (m_hat / (jnp.sqrt(v_hat) + ADAM_EPS) + ADAM_WD * w_ref[...])
        m2_ref[...] = m2
        v2_ref[...] = v2

    return pl.pallas_call(
        body, name=name, grid=(r // tr,), in_specs=[_rows(tr, c)] * 4, out_specs=[_rows(tr, c)] * 3,
        out_shape=[_sds((r, c), F32)] * 3, compiler_params=_cp("parallel"))(w, g, m, v)


WEIGHT_NAMES = ("meta_tokens", "norm_mix_pre", "norm_mix_post", "norm_ffn_pre", "norm_ffn_post", "w_in", "q_a_norm", "w_uq",
                "kv_a_norm", "w_ukv", "attn_out_norm", "ssm_conv_w", "ssm_conv_b", "ssm_dt_bias", "ssm_A_log", "ssm_D",
                "ssm_norm", "w_out", "w_up", "ffn_conv_w", "ffn_conv_b", "w_down")
SMALL_ADAM_ROWS = 192


def kernel(x, meta_tokens, norm_mix_pre, norm_mix_post, norm_ffn_pre, norm_ffn_post, w_in, q_a_norm, w_uq, kv_a_norm, w_ukv, attn_out_norm, ssm_conv_w, ssm_conv_b, ssm_dt_bias, ssm_A_log, ssm_D, ssm_norm, w_out, w_up, ffn_conv_w, ffn_conv_b, w_down, loss_target, m_meta_tokens, m_norm_mix_pre, m_norm_mix_post, m_norm_ffn_pre, m_norm_ffn_post, m_w_in, m_q_a_norm, m_w_uq, m_kv_a_norm, m_w_ukv, m_attn_out_norm, m_ssm_conv_w, m_ssm_conv_b, m_ssm_dt_bias, m_ssm_A_log, m_ssm_D, m_ssm_norm, m_w_out, m_w_up, m_ffn_conv_w, m_ffn_conv_b, m_w_down, v_meta_tokens, v_norm_mix_pre, v_norm_mix_post, v_norm_ffn_pre, v_norm_ffn_post, v_w_in, v_q_a_norm, v_w_uq, v_kv_a_norm, v_w_ukv, v_attn_out_norm, v_ssm_conv_w, v_ssm_conv_b, v_ssm_dt_bias, v_ssm_A_log, v_ssm_D, v_ssm_norm, v_w_out, v_w_up, v_ffn_conv_w, v_ffn_conv_b, v_w_down):
    args = locals()
    w = {n: args[n] for n in WEIGHT_NAMES}
    mom = {n: args["m_" + n] for n in WEIGHT_NAMES}
    var = {n: args["v_" + n] for n in WEIGHT_NAMES}
    cx, cy, cc = _my_place()
    chip = 2 * cx + cy

    two_d = {n: (shp[0], functools.reduce(lambda a, b: a * b, shp[1:])) for n, shp in BIG}
    names = [n for n, _ in BIG]
    core_i = cc.astype(jnp.int32).reshape(1)
    chip_i = chip.astype(jnp.int32).reshape(1)
    bufs = [_place_own(w[n].reshape(two_d[n]), chip_i, "place_" + n).reshape(N_CHIPS, 2, two_d[n][0] // 2, two_d[n][1])
            for n in names]
    small = _pack_rows([w[n] for n, _ in SMALL_SHARDED], SMALL_AG_ROWS)
    *gathered, small_all = _gather_weights(bufs, small, "allgather_weights")
    gath = {n: a.reshape((N_CHIPS,) + two_d[n]) for n, a in zip(names, gathered)}
    p = dict(w_in=gath["w_in"].transpose(1, 0, 2).reshape(D, D_IN), w_uq=gath["w_uq"].reshape(QR, MLA_H, DN + DR),
             w_ukv=gath["w_ukv"].reshape(KVR, MLA_H, DN + DV), w_out=gath["w_out"].reshape(2 * D, D), w_up=gath["w_up"],
             w_down=gath["w_down"].reshape(D_FF, D))
    sm_parts = [_unpack_rows(small_all[j], [a * b for _, (a, b) in SMALL_SHARDED]) for j in range(N_CHIPS)]
    for i, (n, shp) in enumerate(SMALL_SHARDED):
        p[n] = jnp.concatenate([sm_parts[j][i].reshape(shp) for j in range(N_CHIPS)], axis=1)
    for n, _ in SMALL_REPL:
        p[n] = w[n]
    meta_full = p.pop("meta_tokens")

    loss_part, gx, gmeta, g = _device_step(x[0], loss_target[0], meta_full, p)

    small_names = [n for n, _ in SMALL_REPL] + ["ssm_conv_w", "ffn_conv_w"]
    small_sizes = [128] + [sz for _, sz in SMALL_REPL] + [N_META * D, SSM_K * D_XBC, FFN_K * 2 * D_FF]
    order = [n for n, _ in SMALL_REPL]
    sp = _pack_rows([loss_part[0]] + [g[n] for n in order] + [gmeta, g["ssm_conv_w"], g["ffn_conv_w"]], SMALL_AR_ROWS)
    red = _unpack_rows(_allreduce_small(sp), small_sizes)
    loss = red[0][0]
    gfull = {n: red[1 + i].reshape(1, -1) for i, n in enumerate(order)}
    n_r = len(order)
    gfull["meta_tokens"] = lax.dynamic_slice_in_dim(red[1 + n_r].reshape(N_META, D), chip * (D // N_CHIPS), D // N_CHIPS, axis=1)
    gfull["ssm_conv_w"] = lax.dynamic_slice_in_dim(red[2 + n_r].reshape(SSM_K, D_XBC), chip * (D_XBC // N_CHIPS),
                                                   D_XBC // N_CHIPS, axis=1)[None]
    gfull["ffn_conv_w"] = lax.dynamic_slice_in_dim(red[3 + n_r].reshape(FFN_K, 2 * D_FF), chip * (2 * D_FF // N_CHIPS),
                                                   2 * D_FF // N_CHIPS, axis=1)[None]

    g["w_in"] = g["w_in"].reshape(D, N_CHIPS, D_IN // N_CHIPS).transpose(1, 0, 2)
    gs = [g[n].reshape(N_CHIPS, 2, two_d[n][0] // 2, two_d[n][1]) for n in names]
    place_i = jnp.stack([chip, cc]).astype(jnp.int32)
    from_sib = _send_sibling_halves(gs)
    pairs = [_add_pair(gg, tt, core_i, "reduce_pair_" + n) for n, gg, tt in zip(names, gs, from_sib)]
    got = _exchange_chips(pairs)
    mine = [_add_chips(pp, gg, place_i, "reduce_chips_" + n) for n, pp, gg in zip(names, pairs, got)]
    for n, both in zip(names, _share_sibling(mine)):
        gfull[n] = both.reshape(two_d[n])

    delta, new_m, new_v = {}, {}, {}
    for n, shp in BIG:
        outs = _adamw(w[n].reshape(two_d[n]), gfull[n], mom[n].reshape(two_d[n]), var[n].reshape(two_d[n]), "adamw_" + n)
        delta[n], new_m[n], new_v[n] = (o.reshape((1,) + shp) for o in outs)
    snames = order + ["meta_tokens", "ssm_conv_w", "ffn_conv_w"]
    ssizes = [functools.reduce(lambda a, b: a * b, w[n].shape) for n in snames]
    packs = [_pack_rows([d[n] for n in snames], SMALL_ADAM_ROWS) for d in (w, gfull, mom, var)]
    outs = _adamw(*packs, "adamw_small")
    for d, o in zip((delta, new_m, new_v), outs):
        for n, piece in zip(snames, _unpack_rows(o, ssizes)):
            d[n] = piece.reshape(w[n].shape)
    gout = {n: gfull[n].reshape(w[n].shape) for n in WEIGHT_NAMES}
    return (loss, gx[None], *[gout[n] for n in WEIGHT_NAMES], *[delta[n] for n in WEIGHT_NAMES],
            *[new_m[n] for n in WEIGHT_NAMES], *[new_v[n] for n in WEIGHT_NAMES])
```

```python
import functools

import jax
import jax.numpy as jnp
from jax import lax
from jax.experimental import pallas as pl
from jax.experimental.pallas import tpu as pltpu

F32 = jnp.float32
BF16 = jnp.bfloat16

D = 1024
N_META = 16
FRONT = 128
PAD_ROWS = FRONT - N_META
MLA_H = 8
DN, DR, DV = 128, 64, 128
QR, KVR = 384, 256
SOFTMAX_SCALE = (DN + DR) ** -0.5
ROPE_THETA = 10000.0
SSM_H, SSM_P, SSM_G, SSM_N, SSM_K = 16, 64, 2, 128, 4
CHUNK = 128
D_SSM = SSM_H * SSM_P
D_XBC = D_SSM + 2 * SSM_G * SSM_N
GSZ = D_SSM // SSM_G
D_FF = 2816
FFN_K = 3
EPS = 1e-6
IN_SPLITS = (QR, KVR, DR, D_SSM, D_XBC, SSM_H)
D_IN = sum(IN_SPLITS)
LAT_W = 768
IN_P = LAT_W + D_SSM + D_XBC + 128
NEG = -1e30
LOG2E = 1.4426950408889634
LN2 = 0.6931471805599453
Q_SCALE = SOFTMAX_SCALE * LOG2E

ADAM_LR, ADAM_B1, ADAM_B2, ADAM_EPS, ADAM_WD, ADAM_STEP = 0.001, 0.9, 0.999, 1e-08, 0.01, 10

VMEM_LIMIT = 56 * 1024 * 1024
MM_ROWS = (640, 320, 128)
MESH = pl.DeviceIdType.MESH


def _sds(shape, dtype):
    return jax.ShapeDtypeStruct(shape, dtype)


def _cp(*sem):
    return pltpu.CompilerParams(dimension_semantics=sem, vmem_limit_bytes=VMEM_LIMIT)


def _rt(n, cands):
    for c in cands:
        if n % c == 0:
            return c
    raise ValueError((n, cands))


def _full(shape):
    nd = len(shape)
    return pl.BlockSpec(shape, lambda *_: (0,) * nd)


def _rows(tr, c):
    return pl.BlockSpec((tr, c), lambda i: (i, 0))


def _sigmoid(x):
    return 1.0 / (1.0 + jnp.exp(-x))


def _silu(x):
    return x * _sigmoid(x)


def _dsilu(x):
    s = _sigmoid(x)
    return s * (1.0 + x * (1.0 - s))


def _softplus(x):
    return jnp.maximum(x, 0.0) + jnp.log(1.0 + jnp.exp(-jnp.abs(x)))


def _rms(x, g):
    r = lax.rsqrt(jnp.mean(x * x, axis=-1, keepdims=True) + EPS)
    return x * r * g


def _rms_bwd(x, g, dy):
    r = lax.rsqrt(jnp.mean(x * x, axis=-1, keepdims=True) + EPS)
    xh = x * r
    dxh = dy * g
    dx = r * (dxh - xh * jnp.mean(dxh * xh, axis=-1, keepdims=True))
    return dx, jnp.sum(dy * xh, axis=0, keepdims=True)


def _dot(a, b):
    return jnp.dot(a, b, preferred_element_type=F32)


def _dot_nt(a, b):
    return lax.dot_general(a, b, (((1,), (1,)), ((), ())), preferred_element_type=F32)


def _dot_tn(a, b):
    return lax.dot_general(a, b, (((0,), (0,)), ((), ())), preferred_element_type=F32)


def _split3(x):
    hi = x.astype(BF16)
    r = x - hi.astype(F32)
    mid = r.astype(BF16)
    return hi, mid, (r - mid.astype(F32)).astype(BF16)


def _dot_hi(a, b, split="a"):
    if split == "a":
        bb = b.astype(BF16)
        return sum(_dot(t, bb) for t in _split3(a))
    ab = a.astype(BF16)
    return sum(_dot(ab, t) for t in _split3(b))


def _dot_nt_hi(a, b):
    bb = b.astype(BF16)
    return sum(_dot_nt(t, bb) for t in _split3(a))


def _shift_down(x, halo, j):
    xr = pltpu.roll(x, j, axis=0)
    hr = pltpu.roll(halo, j, axis=0)
    row = lax.broadcasted_iota(jnp.int32, (8, x.shape[1]), 0)
    first = jnp.where(row < j, hr, xr[:8])
    return jnp.concatenate([first, xr[8:]], axis=0)


def _shift_up(x, nxt, j):
    t = x.shape[0]
    xr = pltpu.roll(x, t - j, axis=0)
    nr = pltpu.roll(nxt, 8 - j, axis=0)
    row = lax.broadcasted_iota(jnp.int32, (8, x.shape[1]), 0)
    last = jnp.where(row + j >= 8, nr, xr[t - 8:])
    return jnp.concatenate([xr[:t - 8], last], axis=0)


def _acc_rows(ref, val, first):
    @pl.when(first)
    def _():
        ref[...] = val

    @pl.when(jnp.logical_not(first))
    def _():
        ref[...] += val


def _mm_nt(a, b, name, out_dtype=F32, tms=(640, 320, 128)):
    m, k = a.shape
    n = b.shape[0]
    tm = _rt(m, tms)

    def body(a_ref, b_ref, o_ref):
        o_ref[...] = _dot_nt(a_ref[...].astype(BF16), b_ref[...]).astype(out_dtype)

    return pl.pallas_call(
        body, name=name, grid=(m // tm,), in_specs=[_rows(tm, k), _full(b.shape)], out_specs=_rows(tm, n),
        out_shape=_sds((m, n), out_dtype), compiler_params=_cp("parallel"))(a, b)


def _mm_tn(a, b, name, tn=None, trs=(1664, 640, 128), chunked=False):
    r, m = a.shape
    n = b.shape[1]
    tn = n if tn is None else tn
    tr = _rt(r, trs)

    def body(a_ref, b_ref, o_ref):
        part = _dot_tn(a_ref[...].astype(BF16), b_ref[...].astype(BF16))
        _acc_rows(o_ref, part, pl.program_id(1) == 0)

    if chunked:
        out_specs, out_shape = pl.BlockSpec((None, m, tn), lambda j, i: (j, 0, 0)), _sds((n // tn, m, tn), F32)
    else:
        out_specs, out_shape = pl.BlockSpec((m, tn), lambda j, i: (0, j)), _sds((m, n), F32)
    return pl.pallas_call(
        body, name=name, grid=(n // tn, r // tr),
        in_specs=[pl.BlockSpec((tr, m), lambda j, i: (i, 0)), pl.BlockSpec((tr, tn), lambda j, i: (i, j))],
        out_specs=out_specs, out_shape=out_shape, compiler_params=_cp("parallel", "arbitrary"))(a, b)


def _inproj(h0, g, w):
    lp = h0.shape[0]
    tr = _rt(lp, MM_ROWS)
    segs = ((0, LAT_W), (LAT_W, LAT_W + D_SSM), (LAT_W + D_SSM, LAT_W + D_SSM + D_XBC), (IN_P - 128, IN_P))

    def body(h_ref, g_ref, w_ref, hn_ref, lat_ref, z_ref, xbc_ref, dt_ref):
        hn = _rms(h_ref[...], g_ref[...]).astype(BF16)
        hn_ref[...] = hn
        for ref, (a, b) in zip((lat_ref, z_ref, xbc_ref, dt_ref), segs):
            ref[...] = _dot(hn, w_ref[:, a:b])

    return pl.pallas_call(
        body, name="inproj", grid=(lp // tr,), in_specs=[_rows(tr, D), _full((1, D)), _full(w.shape)],
        out_specs=[_rows(tr, D), _rows(tr, LAT_W), _rows(tr, D_SSM), _rows(tr, D_XBC), _rows(tr, 128)],
        out_shape=[_sds((lp, D), BF16), _sds((lp, LAT_W), F32), _sds((lp, D_SSM), F32), _sds((lp, D_XBC), F32),
                   _sds((lp, 128), F32)],
        compiler_params=_cp("parallel"))(h0, g, w)


def _rope(x, cos, sa, sb):
    return x * cos + pltpu.roll(x, 96, axis=1) * sa + pltpu.roll(x, 32, axis=1) * sb


def _rope_t(g, cos, sa, sb):
    return g * cos + pltpu.roll(g * sa, 32, axis=1) + pltpu.roll(g * sb, 96, axis=1)


def _mla_prep(lat, qg, kvg, wq, wkv, cos, sa, sb):
    lp = lat.shape[0]
    tr = _rt(lp, MM_ROWS)

    def body(lat_ref, qg_ref, kvg_ref, wq_ref, wkv_ref, cos_ref, sa_ref, sb_ref, q_ref, k_ref, v_ref, ql_ref, kl_ref):
        lat_v = lat_ref[...]
        ql = _rms(lat_v[:, :QR], qg_ref[...]).astype(BF16)
        kl = _rms(lat_v[:, QR:QR + KVR], kvg_ref[...]).astype(BF16)
        ql_ref[...] = ql
        kl_ref[...] = kl
        cos_v, sa_v, sb_v = cos_ref[...], sa_ref[...], sb_ref[...]
        kpe = _rope(lat_v[:, QR + KVR:LAT_W], cos_v, sa_v, sb_v).astype(BF16)
        for h in range(MLA_H):
            q_ref[h, :, 0:DN] = (_dot(ql, wq_ref[:, h * DN:(h + 1) * DN]) * Q_SCALE).astype(BF16)
            qpe = _dot(ql, wq_ref[:, D + h * 128:D + (h + 1) * 128])
            q_ref[h, :, DN:2 * DN] = (_rope(qpe, cos_v, sa_v, sb_v) * Q_SCALE).astype(BF16)
            k_ref[h, :, 0:DN] = _dot(kl, wkv_ref[:, h * DN:(h + 1) * DN]).astype(BF16)
            k_ref[h, :, DN:2 * DN] = kpe
            v_ref[h] = _dot(kl, wkv_ref[:, D + h * DV:D + (h + 1) * DV]).astype(BF16)

    hb = lambda w: pl.BlockSpec((MLA_H, tr, w), lambda i: (0, i, 0))
    return pl.pallas_call(
        body, name="mla_prep", grid=(lp // tr,),
        in_specs=[_rows(tr, LAT_W), _full((1, QR)), _full((1, KVR)), _full(wq.shape), _full(wkv.shape),
                  _rows(tr, 128), _rows(tr, 128), _rows(tr, 128)],
        out_specs=[hb(256), hb(256), hb(128), _rows(tr, QR), _rows(tr, KVR)],
        out_shape=[_sds((MLA_H, lp, 256), BF16), _sds((MLA_H, lp, 256), BF16), _sds((MLA_H, lp, 128), BF16),
                   _sds((lp, QR), BF16), _sds((lp, KVR), BF16)],
        compiler_params=_cp("parallel"))(lat, qg, kvg, wq, wkv, cos, sa, sb)


def _attn_mask(r0, c0, tq, tk, transposed=False):
    if transposed:
        kk = c0 + lax.broadcasted_iota(jnp.int32, (tk, tq), 0)
        qq = r0 + lax.broadcasted_iota(jnp.int32, (tk, tq), 1)
    else:
        qq = r0 + lax.broadcasted_iota(jnp.int32, (tq, tk), 0)
        kk = c0 + lax.broadcasted_iota(jnp.int32, (tq, tk), 1)
    return jnp.logical_and(kk <= qq, kk >= PAD_ROWS)


def _attn_fwd(q, k, v, ride=()):
    lp = q.shape[1]
    t = _rt(lp, (640, 128))
    nq = lp // t

    hp = 2

    nr = len(ride)
    steps = (MLA_H // hp) * nq

    def body(q_ref, k_ref, v_ref, *rest):
        o_ref, lse_ref = rest[nr:nr + 2]
        bufs, sems = rest[nr + 2:2 * nr + 2], rest[2 * nr + 2:]
        qi = pl.program_id(1)
        step = pl.program_id(0) * nq + qi
        if nr:
            pl.when(step == 0)(lambda: _ride_gather(bufs, *sems, 0))
            pl.when(step == steps // 2)(lambda: _ride_gather(bufs, *sems, 1))
        qv = [q_ref[a] for a in range(hp)]

        def tile(kj, carries, masked, live=None):
            kv_rows = pl.ds(pl.multiple_of(kj * t, t), t)
            out = []
            for a in range(hp):
                m, l, acc = carries[a]
                kk = k_ref[a, kv_rows, :]
                vv = v_ref[a, kv_rows, :]
                s = _dot_nt(qv[a], kk)
                if masked:
                    keep = _attn_mask(qi * t, kj * t, t, t)
                    if live is not None:
                        keep = jnp.logical_and(keep, live)
                    s = jnp.where(keep, s, NEG)
                m_new = jnp.maximum(m, jnp.max(s, axis=-1, keepdims=True))
                alpha = jnp.exp2(m - m_new)
                p = jnp.exp2(s - m_new)
                l = alpha * l + jnp.sum(p, axis=-1, keepdims=True)
                acc = alpha * acc + _dot(p.astype(BF16), vv)
                out.append((m_new, l, acc))
            return tuple(out)

        init = tuple((jnp.full((t, 1), NEG, F32), jnp.zeros((t, 1), F32), jnp.zeros((t, DV), F32)) for _ in range(hp))
        carries = tile(0, init, True)
        carries = lax.fori_loop(1, qi, lambda kj, c: tile(kj, c, False), carries)
        carries = tile(qi, carries, True, live=qi > 0)
        for a in range(hp):
            m, l, acc = carries[a]
            o_ref[:, a * DV:(a + 1) * DV] = acc / l
            lse_ref[a] = jnp.broadcast_to(m + jnp.log(l) * LOG2E, (t, 128)).T[:8]
        if nr:
            pl.when(step == steps - 1)(lambda: _ride_gather(bufs, *sems, 2))

    sems = [pltpu.SemaphoreType.DMA((6 * nr,)), pltpu.SemaphoreType.DMA((6 * nr,))] if nr else []
    outs = pl.pallas_call(
        body, name="attn_fwd", grid=(MLA_H // hp, nq),
        in_specs=[pl.BlockSpec((hp, t, 256), lambda h, i: (h, i, 0)), pl.BlockSpec((hp, lp, 256), lambda h, i: (h, 0, 0)),
                  pl.BlockSpec((hp, lp, 128), lambda h, i: (h, 0, 0))] + [ANY] * nr,
        out_specs=[pl.BlockSpec((t, hp * DV), lambda h, i: (i, h)), pl.BlockSpec((hp, 8, t), lambda h, i: (h, 0, i))] + [ANY] * nr,
        out_shape=[_sds((lp, MLA_H * DV), F32), _sds((MLA_H, 8, lp), F32)] + [_sds(b.shape, b.dtype) for b in ride],
        input_output_aliases={3 + w: 2 + w for w in range(nr)}, scratch_shapes=sems,
        compiler_params=_cp("arbitrary", "arbitrary"))(q, k, v, *ride)
    return outs[0], outs[1], list(outs[2:])


def _attn_out_bwd(o, g, dmixin):
    lp = o.shape[0]
    tr = _rt(lp, (640, 128))

    def body(o_ref, g_ref, dy_ref, do_ref, dg_ref, dl_ref):
        i = pl.program_id(0)
        ov = o_ref[...]
        do, dg = _rms_bwd(ov, g_ref[...], dy_ref[...])
        do_ref[...] = do
        _acc_rows(dg_ref, dg, i == 0)
        prod = do * ov
        lane = lax.broadcasted_iota(jnp.int32, (1, 128), 1)
        cols = jnp.zeros((tr, 128), F32)
        for h in range(MLA_H):
            cols = cols + jnp.sum(prod[:, h * DV:(h + 1) * DV], axis=-1, keepdims=True) * (lane == h).astype(F32)
        dl_ref[...] = cols.T[:MLA_H]

    return pl.pallas_call(
        body, name="attn_out_bwd", grid=(lp // tr,),
        in_specs=[_rows(tr, D), _full((1, D)), pl.BlockSpec((tr, D), lambda i: (i, 0))],
        out_specs=[_rows(tr, D), _full((1, D)), pl.BlockSpec((MLA_H, tr), lambda i: (0, i))],
        out_shape=[_sds((lp, D), F32), _sds((1, D), F32), _sds((MLA_H, lp), F32)],
        compiler_params=_cp("arbitrary"))(o, g, dmixin)


def _attn_bwd(q, k, v, do, lse_row, delta_row, ride=()):
    lp = q.shape[1]
    t = _rt(lp, (640, 128))
    nq = lp // t

    nr = len(ride)

    def body(q_ref, k_ref, v_ref, do_ref, lse_ref, dl_ref, *rest):
        ps = rest[:nr]
        dq_ref, dk_ref, dv_ref = rest[nr:nr + 3]
        got, sems = rest[nr + 3:2 * nr + 3], rest[2 * nr + 3:]
        kj = pl.program_id(1)
        step = pl.program_id(0) * nq + kj
        if nr:
            pl.when(step == 0)(lambda: _ride_exchange(ps, got, *sems, 0))
        kk = k_ref[0]
        vv = v_ref[0]

        @pl.when(kj == 0)
        def _():
            dq_ref[...] = jnp.zeros_like(dq_ref)

        def tile(qi, carry, masked):
            dk, dv = carry
            q_rows = pl.ds(pl.multiple_of(qi * t, t), t)
            qv = q_ref[0, q_rows, :]
            dob = do_ref[q_rows, :].astype(BF16)
            st = _dot_nt(kk, qv)
            if masked:
                st = jnp.where(_attn_mask(qi * t, kj * t, t, t, transposed=True), st, NEG)
            pt = jnp.exp2(st - lse_ref[0, qi])
            dpt = _dot_nt(vv, dob)
            dst = (pt * (dpt - dl_ref[0, qi])).astype(BF16)
            dv = dv + _dot(pt.astype(BF16), dob)
            dk = dk + _dot(dst, qv)
            dq_ref[0, q_rows, :] += _dot_tn(dst, kk)
            return dk, dv

        carry = tile(kj, (jnp.zeros((t, 256), F32), jnp.zeros((t, DV), F32)), True)
        split = jnp.where(kj == 0, nq, kj + 1)
        carry = lax.fori_loop(kj + 1, split, lambda qi, c: tile(qi, c, True), carry)
        dk, dv = lax.fori_loop(split, nq, lambda qi, c: tile(qi, c, False), carry)
        dk_ref[0] = dk * LN2
        dv_ref[0] = dv
        if nr:
            pl.when(step == MLA_H * nq - 1)(lambda: _ride_exchange(ps, got, *sems, 1))

    stat = pl.BlockSpec((1, nq, 1, t), lambda h, j: (h, 0, 0, 0))
    sems = [pltpu.SemaphoreType.DMA((3 * nr,)), pltpu.SemaphoreType.DMA((3 * nr,))] if nr else []
    outs = pl.pallas_call(
        body, name="attn_bwd", grid=(MLA_H, nq),
        in_specs=[pl.BlockSpec((1, lp, 256), lambda h, j: (h, 0, 0)), pl.BlockSpec((1, t, 256), lambda h, j: (h, j, 0)),
                  pl.BlockSpec((1, t, 128), lambda h, j: (h, j, 0)), pl.BlockSpec((lp, DV), lambda h, j: (0, h)), stat, stat]
        + [ANY] * nr,
        out_specs=[pl.BlockSpec((1, lp, 256), lambda h, j: (h, 0, 0)), pl.BlockSpec((1, t, 256), lambda h, j: (h, j, 0)),
                   pl.BlockSpec((1, t, 128), lambda h, j: (h, j, 0))] + [ANY] * nr,
        out_shape=[_sds((MLA_H, lp, 256), F32), _sds((MLA_H, lp, 256), F32), _sds((MLA_H, lp, 128), F32)]
        + [_sds((3,) + p.shape[1:], p.dtype) for p in ride],
        scratch_shapes=sems, compiler_params=_cp("arbitrary", "arbitrary"))(q, k, v, do, lse_row, delta_row, *ride)
    return outs[0], outs[1], outs[2], list(outs[3:])


def _ssd_consts():
    ri = lax.broadcasted_iota(jnp.int32, (CHUNK, CHUNK), 0)
    ci = lax.broadcasted_iota(jnp.int32, (CHUNK, CHUNK), 1)
    expand = (lax.broadcasted_iota(jnp.int32, (128, D_SSM), 0)
              == lax.broadcasted_iota(jnp.int32, (128, D_SSM), 1) // SSM_P).astype(F32)
    return ri, ci, expand


def _ssd_chunk(c, x_ref, xh_ref, dt_ref, dtT_ref, cw_ref, cb_ref, dtb_ref, dtbT_ref, al_ref, alT_ref):
    ri, ci, expand = _ssd_consts()
    x = x_ref[...]
    halo = jnp.where(c > 0, xh_ref[...], 0.0)
    sh = [x] + [_shift_down(x, halo, j) for j in range(1, SSM_K)]
    cv = cb_ref[...]
    for kk in range(SSM_K):
        cv = cv + cw_ref[kk:kk + 1, :] * sh[SSM_K - 1 - kk]
    xa = _silu(cv)
    grow = c * CHUNK + ri
    gcol = c * CHUNK + lax.broadcasted_iota(jnp.int32, (SSM_H, CHUNK), 1)
    sp = dt_ref[...] + dtb_ref[...]
    spT = dtT_ref[...] + dtbT_ref[...]
    dtc = jnp.where(grow >= PAD_ROWS, _softplus(sp), 0.0)
    dtr = jnp.where(gcol >= PAD_ROWS, _softplus(spT), 0.0)
    arow = -jnp.exp(al_ref[...])
    acolT = -jnp.exp(alT_ref[...])
    ltri = (ci <= ri).astype(F32)
    acs = _dot_hi(ltri, dtc * arow, split="b")
    acsT = _dot_hi(dtr * acolT, (ri <= ci).astype(F32))
    return dict(x=x, sh=sh, cv=cv, xa=xa, sp=sp, dtc=dtc, arow=arow, acs=acs, acsT=acsT, ri=ri, ci=ci, expand=expand,
                grow=grow)


def _ssd_mats(k, s_prev):
    xa, acs, acsT, expand, ri, ci = k["xa"], k["acs"], k["acsT"], k["expand"], k["ri"], k["ci"]
    xs = xa[:, :D_SSM]
    dt_e = _dot_hi(k["dtc"], expand)
    acs_e = _dot_hi(acs, expand)
    last_e = acs_e[CHUNK - 1:CHUNK, :]
    ea = jnp.exp(acs_e)
    f = jnp.exp(last_e - acs_e)
    cd = jnp.exp(last_e)
    xdt = xs * dt_e
    bm = [xa[:, D_SSM + g * SSM_N:D_SSM + (g + 1) * SSM_N] for g in range(SSM_G)]
    cm = [xa[:, D_SSM + (SSM_G + g) * SSM_N:D_SSM + (SSM_G + g + 1) * SSM_N] for g in range(SSM_G)]
    bmb = [b.astype(BF16) for b in bm]
    cmb = [cc.astype(BF16) for cc in cm]
    cb = [_dot_nt(cmb[g], bmb[g]) for g in range(SSM_G)]
    lam, mm = [], []
    for h in range(SSM_H):
        diff = acs[:, h:h + 1] - acsT[h:h + 1, :]
        lam_h = jnp.exp(jnp.where(ci <= ri, diff, NEG))
        lam.append(lam_h)
        mm.append(cb[h // (SSM_H // SSM_G)] * lam_h)
    lo = lax.broadcasted_iota(jnp.int32, (CHUNK, 128), 1) < SSM_P
    xdt_h = []
    for h in range(SSM_H):
        pair = xdt[:, (h // 2) * 128:(h // 2 + 1) * 128]
        xdt_h.append(jnp.where(lo if h % 2 == 0 else jnp.logical_not(lo), pair, 0.0).astype(BF16))
    ydiag = jnp.concatenate(
        [_dot(mm[2 * j].astype(BF16), xdt_h[2 * j]) + _dot(mm[2 * j + 1].astype(BF16), xdt_h[2 * j + 1])
         for j in range(SSM_H // 2)], axis=1)
    t_off = [_dot(cmb[g], s_prev[g].astype(BF16)) for g in range(SSM_G)]
    yoff = jnp.concatenate(t_off, axis=1) * ea
    return dict(xs=xs, dt_e=dt_e, acs_e=acs_e, ea=ea, f=f, cd=cd, xdt=xdt, bm=bm, cm=cm, bmb=bmb, cmb=cmb, cb=cb, lam=lam,
                mm=mm, lo=lo, xdt_h=xdt_h, ydiag=ydiag, t_off=t_off, yoff=yoff)


def _ssd_specs(nc, rev):
    ix = (lambda i: nc - 1 - i) if rev else (lambda i: i)
    return [
        pl.BlockSpec((CHUNK, D_XBC), lambda i: (ix(i), 0)),
        pl.BlockSpec((8, D_XBC), lambda i: (jnp.maximum(ix(i) * (CHUNK // 8) - 1, 0), 0)),
        pl.BlockSpec((CHUNK, D_SSM), lambda i: (ix(i), 0)),
        pl.BlockSpec((CHUNK, 128), lambda i: (ix(i), 0)),
        pl.BlockSpec((SSM_H, CHUNK), lambda i: (0, ix(i))),
        _full((8, D_XBC)), _full((1, D_XBC)), _full((1, 128)), _full((SSM_H, 1)), _full((1, 128)), _full((SSM_H, 1)),
        _full((1, D_SSM)), _full((1, D_SSM)),
    ]


def _ssd_fwd(xbc, z, dtr, dtrT, cw, cb, dtb, dtbT, alog, alogT, d_e, ng):
    lp = xbc.shape[0]
    nc = lp // CHUNK

    def body(x_ref, xh_ref, z_ref, dt_ref, dtT_ref, cw_ref, cb_ref, dtb_ref, dtbT_ref, al_ref, alT_ref, de_ref, ng_ref,
             y_ref, st_ref, s_scr):
        c = pl.program_id(0)

        @pl.when(c == 0)
        def _():
            s_scr[...] = jnp.zeros_like(s_scr)

        k = _ssd_chunk(c, x_ref, xh_ref, dt_ref, dtT_ref, cw_ref, cb_ref, dtb_ref, dtbT_ref, al_ref, alT_ref)
        s_prev = [s_scr[g] for g in range(SSM_G)]
        st_ref[0] = s_scr[...]
        m = _ssd_mats(k, s_prev)
        xd = (m["xdt"] * m["f"]).astype(BF16)
        for g in range(SSM_G):
            sl = slice(g * GSZ, (g + 1) * GSZ)
            s_scr[g] = m["cd"][:, sl] * s_prev[g] + _dot(m["bm"][g].T.astype(BF16), xd[:, sl])
        y = m["ydiag"] + m["yoff"] + de_ref[...] * m["xs"]
        u = y * _silu(z_ref[...])
        outs = []
        for g in range(SSM_G):
            ug = u[:, g * GSZ:(g + 1) * GSZ]
            outs.append(ug * lax.rsqrt(jnp.mean(ug * ug, axis=-1, keepdims=True) + EPS))
        y_ref[...] = jnp.concatenate(outs, axis=1) * ng_ref[...]

    return pl.pallas_call(
        body, name="ssd_fwd", grid=(nc,), in_specs=_ssd_specs(nc, False),
        out_specs=[_rows(CHUNK, D_SSM), pl.BlockSpec((1, SSM_G, SSM_N, GSZ), lambda i: (i, 0, 0, 0))],
        out_shape=[_sds((lp, D_SSM), F32), _sds((nc, SSM_G, SSM_N, GSZ), F32)],
        scratch_shapes=[pltpu.VMEM((SSM_G, SSM_N, GSZ), F32)],
        compiler_params=_cp("arbitrary"))(xbc, xbc, z, dtr, dtrT, cw, cb, dtb, dtbT, alog, alogT, d_e, ng)


def _ssd_bwd(dmixin, xbc, z, dtr, dtrT, st, cw, cb, dtb, dtbT, alog, alogT, d_e, ng):
    lp = xbc.shape[0]
    nc = lp // CHUNK
    hpg = SSM_H // SSM_G

    def body(dy_ref, x_ref, xh_ref, z_ref, dt_ref, dtT_ref, st_ref, cw_ref, cb_ref, dtb_ref, dtbT_ref, al_ref, alT_ref,
             de_ref, ng_ref, dz_ref, dx_ref, ddt_ref, dcw_ref, dcb_ref, ddtb_ref, dal_ref, dd_ref, dng_ref, ds_scr, nx_scr):
        i = pl.program_id(0)
        c = nc - 1 - i
        first = i == 0

        @pl.when(first)
        def _():
            ds_scr[...] = jnp.zeros_like(ds_scr)
            nx_scr[...] = jnp.zeros_like(nx_scr)

        k = _ssd_chunk(c, x_ref, xh_ref, dt_ref, dtT_ref, cw_ref, cb_ref, dtb_ref, dtbT_ref, al_ref, alT_ref)
        s_prev = [st_ref[0, g] for g in range(SSM_G)]
        m = _ssd_mats(k, s_prev)
        ri, ci, expand = k["ri"], k["ci"], k["expand"]
        xs, acs, acsT = m["xs"], k["acs"], k["acsT"]
        zv = z_ref[...]
        dout = dy_ref[...]
        ngv = ng_ref[...]
        y = m["ydiag"] + m["yoff"] + de_ref[...] * xs
        sz = _silu(zv)
        u = y * sz
        du_parts, dng_parts = [], []
        for g in range(SSM_G):
            sl = slice(g * GSZ, (g + 1) * GSZ)
            dug, dngg = _rms_bwd(u[:, sl], ngv[:, sl], dout[:, sl])
            du_parts.append(dug)
            dng_parts.append(dngg)
        du = jnp.concatenate(du_parts, axis=1)
        _acc_rows(dng_ref, jnp.concatenate(dng_parts, axis=1), first)
        dy = du * sz
        dz_ref[...] = du * y * _dsilu(zv)
        dd_e = jnp.sum(dy * xs, axis=0, keepdims=True)
        _acc_rows(dd_ref, _dot_nt_hi(dd_e, expand), first)
        dxs = de_ref[...] * dy
        dacs_e = dy * m["yoff"]
        dtg = (dy * m["ea"]).astype(BF16)
        dxdt = jnp.zeros_like(xs)
        dlast_e = []
        db, dc, ds_prev = [], [], []
        xd = m["xdt"] * m["f"]
        dxd_all = []
        for g in range(SSM_G):
            sl = slice(g * GSZ, (g + 1) * GSZ)
            dsg = ds_scr[g]
            spb = s_prev[g].astype(BF16)
            dc.append(_dot_nt(dtg[:, sl], spb))
            dsp = _dot(m["cm"][g].T.astype(BF16), dtg[:, sl]) + m["cd"][:, sl] * dsg
            ds_prev.append(dsp)
            dlast_e.append(jnp.sum(dsg * s_prev[g], axis=0, keepdims=True) * m["cd"][:, sl])
            dsb = dsg.astype(BF16)
            db.append(_dot_nt(xd[:, sl].astype(BF16), dsb))
            dxd_all.append(_dot(m["bmb"][g], dsb))
        dxd = jnp.concatenate(dxd_all, axis=1)
        dxdt = dxd * m["f"]
        dff = dxd * xd
        dacs_e = dacs_e - dff
        dlast_row = jnp.concatenate(dlast_e, axis=1) + jnp.sum(dff, axis=0, keepdims=True)
        dacs = jnp.zeros((CHUNK, 128), F32)
        lane = lax.broadcasted_iota(jnp.int32, (1, 128), 1)
        cbT = [_dot_nt(m["bmb"][g], m["cmb"][g]) for g in range(SSM_G)]
        dgs = [jnp.zeros((CHUNK, CHUNK), F32) for _ in range(SSM_G)]
        dgTs = [jnp.zeros((CHUNK, CHUNK), F32) for _ in range(SSM_G)]
        dxdt_pairs = []
        for h in range(SSM_H):
            g = h // hpg
            pr = slice((h // 2) * 128, (h // 2 + 1) * 128)
            lo_h = m["lo"] if h % 2 == 0 else jnp.logical_not(m["lo"])
            dyp = jnp.where(lo_h, dy[:, pr], 0.0).astype(BF16)
            xdp = m["xdt"][:, pr].astype(BF16)
            dm = _dot_nt(dyp, xdp)
            dmT = _dot_nt(xdp, dyp)
            lamT = jnp.exp(jnp.where(ri <= ci, acsT[h:h + 1, :] - acs[:, h:h + 1], NEG))
            mT = cbT[g] * lamT
            dgs[g] = dgs[g] + dm * m["lam"][h]
            dgTs[g] = dgTs[g] + dmT * lamT
            v1 = jnp.sum(dm * m["mm"][h], axis=1, keepdims=True)
            v2 = jnp.sum(dmT * mT, axis=1, keepdims=True)
            dacs = dacs + (v1 - v2) * (lane == h).astype(F32)
            part = _dot(mT.astype(BF16), dyp)
            if h % 2 == 0:
                dxdt_pairs.append(part)
            else:
                dxdt_pairs[-1] = dxdt_pairs[-1] + part
        dxdt = dxdt + jnp.concatenate(dxdt_pairs, axis=1)
        for g in range(SSM_G):
            dc[g] = dc[g] + _dot(dgs[g].astype(BF16), m["bmb"][g])
            db[g] = db[g] + _dot(dgTs[g].astype(BF16), m["cmb"][g])
        dacs = dacs + _dot_nt_hi(dacs_e, expand)
        dlast = _dot_nt_hi(dlast_row, expand)
        dacs = dacs + jnp.where(ri == CHUNK - 1, dlast, 0.0)
        dxs = dxs + dxdt * m["dt_e"]
        ddt = _dot_nt_hi(dxdt * xs, expand)
        da = _dot_hi((ri <= ci).astype(F32), dacs, split="b")
        ddt = ddt + da * k["arow"]
        dA = jnp.sum(da * k["dtc"], axis=0, keepdims=True)
        _acc_rows(dal_ref, dA * k["arow"], first)
        ddtr = jnp.where(k["grow"] >= PAD_ROWS, ddt * _sigmoid(k["sp"]), 0.0)
        ddt_ref[...] = ddtr
        _acc_rows(ddtb_ref, jnp.sum(ddtr, axis=0, keepdims=True), first)
        for g in range(SSM_G):
            ds_scr[g] = ds_prev[g]
        dxa = jnp.concatenate([dxs] + db + dc, axis=1)
        dcv = dxa * _dsilu(k["cv"])
        _acc_rows(dcb_ref, jnp.sum(dcv, axis=0, keepdims=True), first)
        dcw_rows = [jnp.sum(dcv * k["sh"][SSM_K - 1 - kk], axis=0, keepdims=True) for kk in range(SSM_K)]
        dcw_rows.append(jnp.zeros((8 - SSM_K, D_XBC), F32))
        _acc_rows(dcw_ref, jnp.concatenate(dcw_rows, axis=0), first)
        nxt = nx_scr[...]
        dx = cw_ref[SSM_K - 1:SSM_K, :] * dcv
        for j in range(1, SSM_K):
            dx = dx + cw_ref[SSM_K - 1 - j:SSM_K - j, :] * _shift_up(dcv, nxt, j)
        grow_x = c * CHUNK + lax.broadcasted_iota(jnp.int32, (CHUNK, D_XBC), 0)
        dx_ref[...] = jnp.where(grow_x >= PAD_ROWS, dx, 0.0)
        nx_scr[...] = dcv[:8]

    specs = _ssd_specs(nc, True)
    in_specs = [pl.BlockSpec((CHUNK, D_SSM), lambda i: (nc - 1 - i, 1))] + specs[:5] + [
        pl.BlockSpec((1, SSM_G, SSM_N, GSZ), lambda i: (nc - 1 - i, 0, 0, 0))] + specs[5:]
    rv = lambda w: pl.BlockSpec((CHUNK, w), lambda i: (nc - 1 - i, 0))
    return pl.pallas_call(
        body, name="ssd_bwd", grid=(nc,), in_specs=in_specs,
        out_specs=[rv(D_SSM), rv(D_XBC), rv(128), _full((8, D_XBC)), _full((1, D_XBC)), _full((1, 128)), _full((1, 128)),
                   _full((1, 128)), _full((1, D_SSM))],
        out_shape=[_sds((lp, D_SSM), F32), _sds((lp, D_XBC), F32), _sds((lp, 128), F32), _sds((8, D_XBC), F32),
                   _sds((1, D_XBC), F32), _sds((1, 128), F32), _sds((1, 128), F32), _sds((1, 128), F32), _sds((1, D_SSM), F32)],
        scratch_shapes=[pltpu.VMEM((SSM_G, SSM_N, GSZ), F32), pltpu.VMEM((8, D_XBC), F32)],
        compiler_params=_cp("arbitrary"))(dmixin, xbc, xbc, z, dtr, dtrT, st, cw, cb, dtb, dtbT, alog, alogT, d_e, ng)


def _mixout_fwd(o, ssm, h0, g_ao, g_post, w):
    lp = o.shape[0]
    tr = _rt(lp, MM_ROWS)

    def body(o_ref, s_ref, h_ref, ga_ref, gp_ref, w_ref, mi_ref, mix_ref, h1_ref):
        mixin = jnp.concatenate([_rms(o_ref[...], ga_ref[...]), s_ref[...]], axis=1).astype(BF16)
        mi_ref[...] = mixin
        mix = _dot(mixin, w_ref[...])
        mix_ref[...] = mix
        grow = pl.program_id(0) * tr + lax.broadcasted_iota(jnp.int32, (tr, D), 0)
        h1_ref[...] = h_ref[...] + jnp.where(grow >= PAD_ROWS, _rms(mix, gp_ref[...]), 0.0)

    return pl.pallas_call(
        body, name="mixout_fwd", grid=(lp // tr,),
        in_specs=[_rows(tr, D), _rows(tr, D), _rows(tr, D), _full((1, D)), _full((1, D)), _full(w.shape)],
        out_specs=[_rows(tr, 2 * D), _rows(tr, D), _rows(tr, D)],
        out_shape=[_sds((lp, 2 * D), BF16), _sds((lp, D), F32), _sds((lp, D), F32)],
        compiler_params=_cp("parallel"))(o, ssm, h0, g_ao, g_post, w)


def _ffn_up(h1, g, w):
    lp = h1.shape[0]
    tr = _rt(lp, MM_ROWS)
    tn = D_FF // 2

    def body(h_ref, g_ref, w_ref, hn_ref, u_ref):
        hn = _rms(h_ref[...], g_ref[...]).astype(BF16)
        hn_ref[...] = hn
        u_ref[...] = _dot(hn, w_ref[...])

    return pl.pallas_call(
        body, name="ffn_up", grid=(lp // tr, 2 * D_FF // tn),
        in_specs=[pl.BlockSpec((tr, D), lambda i, j: (i, 0)), _full((1, D)), pl.BlockSpec((None, D, tn), lambda i, j: (j, 0, 0))],
        out_specs=[pl.BlockSpec((tr, D), lambda i, j: (i, 0)), pl.BlockSpec((tr, tn), lambda i, j: (i, j))],
        out_shape=[_sds((lp, D), BF16), _sds((lp, 2 * D_FF), F32)],
        compiler_params=_cp("parallel", "arbitrary"))(h1, g, w)


def _ffn_dhn(du, w4):
    lp = du.shape[0]
    nch, _, tn = w4.shape
    tr = _rt(lp, MM_ROWS)

    def body(du_ref, w_ref, o_ref):
        acc = _dot_nt(du_ref[:, 0:tn], w_ref[0])
        for j in range(1, nch):
            acc = acc + _dot_nt(du_ref[:, j * tn:(j + 1) * tn], w_ref[j])
        o_ref[...] = acc

    return pl.pallas_call(
        body, name="ffn_dhn", grid=(lp // tr,), in_specs=[_rows(tr, nch * tn), _full(w4.shape)], out_specs=_rows(tr, D),
        out_shape=_sds((lp, D), F32), compiler_params=_cp("parallel"))(du, w4)


FFN_CB = 256


def _ffn_gate(u, cw, cb):
    lp = u.shape[0]
    tr = _rt(lp, (320, 128))

    def body(u_ref, uh_ref, cw_ref, cb_ref, a_ref):
        i = pl.program_id(0)
        for j in range(D_FF // FFN_CB):
            halves = []
            for off in (0, D_FF):
                sl = slice(off + j * FFN_CB, off + (j + 1) * FFN_CB)
                x = u_ref[:, sl]
                halo = jnp.where(i > 0, uh_ref[:, sl], 0.0)
                cv = cb_ref[:, sl] + cw_ref[FFN_K - 1:FFN_K, sl] * x
                for s in range(1, FFN_K):
                    cv = cv + cw_ref[FFN_K - 1 - s:FFN_K - s, sl] * _shift_down(x, halo, s)
                halves.append(cv)
            a_ref[:, j * FFN_CB:(j + 1) * FFN_CB] = (_silu(halves[0]) * halves[1]).astype(BF16)

    return pl.pallas_call(
        body, name="ffn_gate", grid=(lp // tr,),
        in_specs=[_rows(tr, 2 * D_FF), pl.BlockSpec((8, 2 * D_FF), lambda i: (jnp.maximum(i * (tr // 8) - 1, 0), 0)),
                  _full((8, 2 * D_FF)), _full((1, 2 * D_FF))],
        out_specs=_rows(tr, D_FF), out_shape=_sds((lp, D_FF), BF16),
        compiler_params=_cp("parallel"))(u, u, cw, cb)


def _ffn_down(a, w, h1, tgt, g_post):
    lp = a.shape[0]
    tr = _rt(lp, MM_ROWS)

    def body(a_ref, w_ref, h_ref, t_ref, g_ref, dh2_ref, dd_ref, dg_ref, loss_ref):
        i = pl.program_id(0)
        d = _dot(a_ref[...], w_ref[...])
        gv = g_ref[...]
        h2 = h_ref[...] + _rms(d, gv)
        grow = i * tr + lax.broadcasted_iota(jnp.int32, (tr, D), 0)
        err = jnp.where(grow >= FRONT, h2 - t_ref[...], 0.0)
        dh2 = err * (1.0 / D)
        dh2_ref[...] = dh2
        dd, dg = _rms_bwd(d, gv, dh2)
        dd_ref[...] = dd.astype(BF16)
        _acc_rows(dg_ref, dg, i == 0)
        part = 0.5 * jnp.sum(jnp.sum(err * err, axis=1, keepdims=True), axis=0, keepdims=True) * (1.0 / D)
        _acc_rows(loss_ref, jnp.broadcast_to(part, (8, 128)), i == 0)

    return pl.pallas_call(
        body, name="ffn_down", grid=(lp // tr,),
        in_specs=[_rows(tr, D_FF), _full(w.shape), _rows(tr, D), _rows(tr, D), _full((1, D))],
        out_specs=[_rows(tr, D), _rows(tr, D), _full((1, D)), _full((8, 128))],
        out_shape=[_sds((lp, D), F32), _sds((lp, D), BF16), _sds((1, D), F32), _sds((8, 128), F32)],
        compiler_params=_cp("arbitrary"))(a, w, h1, tgt, g_post)


def _ffn_gate_bwd(u, da, cw, cb):
    lp = u.shape[0]
    tr = _rt(lp, (320, 128))
    n = lp // tr

    def body(u_ref, uh_ref, da_ref, cw_ref, cb_ref, du_ref, dcw_ref, dcb_ref, nx_scr):
        i = pl.program_id(0)
        t = n - 1 - i
        first = i == 0

        @pl.when(first)
        def _():
            nx_scr[...] = jnp.zeros_like(nx_scr)

        grow = t * tr + lax.broadcasted_iota(jnp.int32, (tr, FFN_CB), 0)
        for j in range(D_FF // FFN_CB):
            cvs, shs, sls = [], [], []
            for off in (0, D_FF):
                sl = slice(off + j * FFN_CB, off + (j + 1) * FFN_CB)
                x = u_ref[:, sl]
                halo = jnp.where(t > 0, uh_ref[:, sl], 0.0)
                sh = [x] + [_shift_down(x, halo, s) for s in range(1, FFN_K)]
                cv = cb_ref[:, sl]
                for kk in range(FFN_K):
                    cv = cv + cw_ref[kk:kk + 1, sl] * sh[FFN_K - 1 - kk]
                cvs.append(cv)
                shs.append(sh)
                sls.append(sl)
            dav = da_ref[:, j * FFN_CB:(j + 1) * FFN_CB]
            dcv = (dav * cvs[1] * _dsilu(cvs[0]), dav * _silu(cvs[0]))
            for hf in range(2):
                sl = sls[hf]
                g = dcv[hf]
                rows = [jnp.sum(g * shs[hf][FFN_K - 1 - kk], axis=0, keepdims=True) for kk in range(FFN_K)]
                rows.append(jnp.zeros((8 - FFN_K, FFN_CB), F32))
                upd_w = jnp.concatenate(rows, axis=0)
                upd_b = jnp.sum(g, axis=0, keepdims=True)

                @pl.when(first)
                def _():
                    dcw_ref[:, sl] = upd_w
                    dcb_ref[:, sl] = upd_b

                @pl.when(jnp.logical_not(first))
                def _():
                    dcw_ref[:, sl] += upd_w
                    dcb_ref[:, sl] += upd_b

                nxt = nx_scr[:, sl]
                du = cw_ref[FFN_K - 1:FFN_K, sl] * g
                for s in range(1, FFN_K):
                    du = du + cw_ref[FFN_K - 1 - s:FFN_K - s, sl] * _shift_up(g, nxt, s)
                du_ref[:, sl] = jnp.where(grow >= PAD_ROWS, du, 0.0).astype(BF16)
                nx_scr[:, sl] = g[:8]

    return pl.pallas_call(
        body, name="ffn_gate_bwd", grid=(n,),
        in_specs=[pl.BlockSpec((tr, 2 * D_FF), lambda i: (n - 1 - i, 0)),
                  pl.BlockSpec((8, 2 * D_FF), lambda i: (jnp.maximum((n - 1 - i) * (tr // 8) - 1, 0), 0)),
                  pl.BlockSpec((tr, D_FF), lambda i: (n - 1 - i, 0)), _full((8, 2 * D_FF)), _full((1, 2 * D_FF))],
        out_specs=[pl.BlockSpec((tr, 2 * D_FF), lambda i: (n - 1 - i, 0)), _full((8, 2 * D_FF)), _full((1, 2 * D_FF))],
        out_shape=[_sds((lp, 2 * D_FF), BF16), _sds((8, 2 * D_FF), F32), _sds((1, 2 * D_FF), F32)],
        scratch_shapes=[pltpu.VMEM((8, 2 * D_FF), F32)],
        compiler_params=_cp("arbitrary"))(u, u, da, cw, cb)


def _norm_bwd_res(x, g, dy, res, name, mask_pad=False, out_dtype=F32):
    lp, c = x.shape
    tr = _rt(lp, (640, 128))

    def body(*refs):
        if res is None:
            x_ref, g_ref, dy_ref, o_ref, dg_ref = refs
        else:
            x_ref, g_ref, dy_ref, r_ref, o_ref, dg_ref = refs
        i = pl.program_id(0)
        dyv = dy_ref[...].astype(F32)
        if mask_pad:
            grow = i * tr + lax.broadcasted_iota(jnp.int32, (tr, c), 0)
            dyv = jnp.where(grow >= PAD_ROWS, dyv, 0.0)
        dx, dg = _rms_bwd(x_ref[...].astype(F32), g_ref[...], dyv)
        if res is not None:
            dx = dx + r_ref[...]
        o_ref[...] = dx.astype(out_dtype)
        _acc_rows(dg_ref, dg, i == 0)

    args = [x, g, dy] + ([] if res is None else [res])
    in_specs = [_rows(tr, c), _full((1, c)), pl.BlockSpec((tr, c), lambda i: (i, 0))] + ([] if res is None else [_rows(tr, c)])
    return pl.pallas_call(
        body, name=name, grid=(lp // tr,), in_specs=in_specs, out_specs=[_rows(tr, c), _full((1, c))],
        out_shape=[_sds((lp, c), out_dtype), _sds((1, c), F32)], compiler_params=_cp("arbitrary"))(*args)


def _mla_bwd(dq, dk, dv, lat, qg, kvg, wq, wkv, cos, sa, sb):
    lp = lat.shape[0]
    tr = _rt(lp, (320, 128))

    def body(dq_ref, dk_ref, dv_ref, lat_ref, qg_ref, kvg_ref, wq_ref, wkv_ref, cos_ref, sa_ref, sb_ref,
             dqf_ref, dkvf_ref, dlat_ref, dqg_ref, dkvg_ref):
        i = pl.program_id(0)
        cos_v, sa_v, sb_v = cos_ref[...], sa_ref[...], sb_ref[...]
        dkpe = jnp.zeros((tr, 128), F32)
        for h in range(MLA_H):
            dqh = dq_ref[h] * SOFTMAX_SCALE
            dqf_ref[:, h * DN:(h + 1) * DN] = dqh[:, :DN].astype(BF16)
            dqf_ref[:, D + h * 128:D + (h + 1) * 128] = _rope_t(dqh[:, DN:], cos_v, sa_v, sb_v).astype(BF16)
            dkh = dk_ref[h]
            dkvf_ref[:, h * DN:(h + 1) * DN] = dkh[:, :DN].astype(BF16)
            dkpe = dkpe + dkh[:, DN:]
            dkvf_ref[:, D + h * DV:D + (h + 1) * DV] = dv_ref[h].astype(BF16)
        dql = _dot_nt(dqf_ref[...], wq_ref[...])
        dkl = _dot_nt(dkvf_ref[...], wkv_ref[...])
        lat_v = lat_ref[...]
        dqc, dqg = _rms_bwd(lat_v[:, :QR], qg_ref[...], dql)
        dkc, dkg = _rms_bwd(lat_v[:, QR:QR + KVR], kvg_ref[...], dkl)
        dlat_ref[:, :QR] = dqc
        dlat_ref[:, QR:QR + KVR] = dkc
        dlat_ref[:, QR + KVR:] = _rope_t(dkpe, cos_v, sa_v, sb_v)
        _acc_rows(dqg_ref, dqg, i == 0)
        _acc_rows(dkvg_ref, dkg, i == 0)

    hb = lambda w: pl.BlockSpec((MLA_H, tr, w), lambda i: (0, i, 0))
    return pl.pallas_call(
        body, name="mla_bwd", grid=(lp // tr,),
        in_specs=[hb(256), hb(256), hb(128), _rows(tr, LAT_W), _full((1, QR)), _full((1, KVR)), _full(wq.shape),
                  _full(wkv.shape), _rows(tr, 128), _rows(tr, 128), _rows(tr, 128)],
        out_specs=[_rows(tr, 2 * D), _rows(tr, 2 * D), _rows(tr, LAT_W), _full((1, QR)), _full((1, KVR))],
        out_shape=[_sds((lp, 2 * D), BF16), _sds((lp, 2 * D), BF16), _sds((lp, LAT_W), F32), _sds((1, QR), F32),
                   _sds((1, KVR), F32)],
        compiler_params=_cp("arbitrary"))(dq, dk, dv, lat, qg, kvg, wq, wkv, cos, sa, sb)


def _inproj_bwd(dlat, dz, dxbc, ddt, w, h0, g, dh1):
    lp = h0.shape[0]
    tr = _rt(lp, (320, 128))
    segs = ((0, LAT_W), (LAT_W, LAT_W + D_SSM), (LAT_W + D_SSM, LAT_W + D_SSM + D_XBC), (IN_P - 128, IN_P))

    def body(dl_ref, dz_ref, dx_ref, dt_ref, w_ref, h_ref, g_ref, r_ref, o_ref, dg_ref):
        dhn = jnp.zeros((tr, D), F32)
        for ref, (a, b) in zip((dl_ref, dz_ref, dx_ref, dt_ref), segs):
            dhn = dhn + _dot_nt(ref[...].astype(BF16), w_ref[:, a:b])
        dx, dg = _rms_bwd(h_ref[...], g_ref[...], dhn)
        o_ref[...] = dx + r_ref[...]
        _acc_rows(dg_ref, dg, pl.program_id(0) == 0)

    return pl.pallas_call(
        body, name="inproj_bwd", grid=(lp // tr,),
        in_specs=[_rows(tr, LAT_W), _rows(tr, D_SSM), _rows(tr, D_XBC), _rows(tr, 128), _full(w.shape), _rows(tr, D),
                  _full((1, D)), _rows(tr, D)],
        out_specs=[_rows(tr, D), _full((1, D))], out_shape=[_sds((lp, D), F32), _sds((1, D), F32)],
        compiler_params=_cp("arbitrary"))(dlat, dz, dxbc, ddt, w, h0, g, dh1)


def _rope_tables(lp):
    pos = (jnp.arange(lp, dtype=jnp.int32) - PAD_ROWS).astype(F32)
    inv = ROPE_THETA ** (-jnp.arange(0, DR, 2, dtype=F32) / DR)
    ang = pos[:, None] * inv[None, :]
    cos, sin = jnp.cos(ang), jnp.sin(ang)
    zero = jnp.zeros_like(sin)
    cos128 = jnp.concatenate([cos, cos, cos, cos], axis=1)
    sa128 = jnp.concatenate([-sin, zero, -sin, zero], axis=1)
    sb128 = jnp.concatenate([zero, sin, zero, sin], axis=1)
    return cos128, sa128, sb128


def _pad_rows8(w):
    return jnp.concatenate([w, jnp.zeros((8 - w.shape[0], w.shape[1]), w.dtype)], axis=0)


def _lane_pad(v):
    return jnp.concatenate([v, jnp.zeros((v.shape[0], 128 - v.shape[1]), v.dtype)], axis=1)


def _late_weights(bufs):
    w_out, w_up, w_down = bufs
    return dict(w_out=w_out.reshape(2 * D, D), w_up=w_up.reshape(N_CHIPS, D, 2 * D_FF // N_CHIPS), w_down=w_down.reshape(D_FF, D))


def _device_step(x, tgt, meta, p, late_bufs=(), early_reduce=None):
    s = x.shape[0]
    lp = s + FRONT
    zpad = jnp.zeros((PAD_ROWS, D), F32)
    h0 = jnp.concatenate([zpad, meta, x], axis=0)
    tgt_p = jnp.concatenate([jnp.zeros((FRONT, D), F32), tgt], axis=0)
    cos, sa, sb = _rope_tables(lp)

    w_in = p["w_in"]
    w_in_p = jnp.concatenate([w_in[:, :QR + KVR + DR], jnp.zeros((D, 64), BF16), w_in[:, QR + KVR + DR:],
                              jnp.zeros((D, 128 - SSM_H), BF16)], axis=1)
    w_uq = p["w_uq"]
    wq_p = jnp.concatenate([w_uq[:, :, :DN].reshape(QR, MLA_H * DN),
                            jnp.concatenate([w_uq[:, :, DN:], jnp.zeros((QR, MLA_H, 128 - DR), BF16)], axis=2).reshape(QR, MLA_H * 128)],
                           axis=1)
    w_ukv = p["w_ukv"]
    wkv_p = jnp.concatenate([w_ukv[:, :, :DN].reshape(KVR, MLA_H * DN), w_ukv[:, :, DN:].reshape(KVR, MLA_H * DV)], axis=1)
    scw = _pad_rows8(p["ssm_conv_w"])
    fcw = _pad_rows8(p["ffn_conv_w"])
    dtb, alog = _lane_pad(p["ssm_dt_bias"]), _lane_pad(p["ssm_A_log"])
    dtbT, alogT = p["ssm_dt_bias"].reshape(SSM_H, 1), p["ssm_A_log"].reshape(SSM_H, 1)
    d_e = jnp.repeat(p["ssm_D"], SSM_P, axis=1)

    hn, lat, z, xbc, dtr = _inproj(h0, p["norm_mix_pre"], w_in_p)
    dtrT = dtr[:, :SSM_H].T
    q, k, v, qlat, kvlat = _mla_prep(lat, p["q_a_norm"], p["kv_a_norm"], wq_p, wkv_p, cos, sa, sb)
    o, lse, gathered = _attn_fwd(q, k, v, ride=late_bufs)
    if late_bufs:
        p = dict(p, **_late_weights(gathered))
    ssm, st = _ssd_fwd(xbc, z, dtr, dtrT, scw, p["ssm_conv_b"], dtb, dtbT, alog, alogT, d_e, p["ssm_norm"])
    mixin, mix, h1 = _mixout_fwd(o, ssm, h0, p["attn_out_norm"], p["norm_mix_post"], p["w_out"])
    hn2, u = _ffn_up(h1, p["norm_ffn_pre"], p["w_up"])
    a = _ffn_gate(u, fcw, p["ffn_conv_b"])
    dh2, dd, g_ffn_post, loss = _ffn_down(a, p["w_down"], h1, tgt_p, p["norm_ffn_post"])

    da = _mm_nt(dd, p["w_down"], "ffn_da")
    g_w_down = _mm_tn(a, dd, "ffn_dw_down", tn=512)
    du, g_fcw, g_fcb = _ffn_gate_bwd(u, da, fcw, p["ffn_conv_b"])
    dhn2 = _ffn_dhn(du, p["w_up"])
    g_w_up = _mm_tn(hn2, du, "ffn_dw_up", tn=D_FF // 2, chunked=True)
    dh1, g_ffn_pre = _norm_bwd_res(h1, p["norm_ffn_pre"], dhn2, dh2, "ffn_norm_bwd")
    dmix, g_mix_post = _norm_bwd_res(mix, p["norm_mix_post"], dh1, None, "mix_post_bwd", mask_pad=True, out_dtype=BF16)
    dmixin = _mm_nt(dmix, p["w_out"], "mix_dmixin")
    g_w_out = _mm_tn(mixin, dmix, "mix_dw_out", tn=512)
    do, g_ao, delta = _attn_out_bwd(o, p["attn_out_norm"], dmixin)
    t = _rt(lp, (640, 128))
    pairs = early_reduce(dict(w_out=g_w_out, w_up=g_w_up, w_down=g_w_down)) if early_reduce else ()
    dq, dk, dv, got = _attn_bwd(q, k, v, do, lse[:, 0, :].reshape(MLA_H, lp // t, 1, t), delta.reshape(MLA_H, lp // t, 1, t),
                                ride=pairs)
    dqf, dkvf, dlat, g_qa, g_kva = _mla_bwd(dq, dk, dv, lat, p["q_a_norm"], p["kv_a_norm"], wq_p, wkv_p, cos, sa, sb)
    g_wq_p = _mm_tn(qlat, dqf, "mla_dw_uq")
    g_wkv_p = _mm_tn(kvlat, dkvf, "mla_dw_ukv")
    dz, dxbc, ddtr, g_scw, g_scb, g_dtb, g_alog, g_dd, g_ssm_norm = _ssd_bwd(
        dmixin, xbc, z, dtr, dtrT, st, scw, p["ssm_conv_b"], dtb, dtbT, alog, alogT, d_e, p["ssm_norm"])
    dh0, g_mix_pre = _inproj_bwd(dlat, dz, dxbc, ddtr, w_in_p, h0, p["norm_mix_pre"], dh1)
    g_in_p = jnp.concatenate([_mm_tn(hn, dlat, "in_dw_lat"), _mm_tn(hn, dz, "in_dw_z"), _mm_tn(hn, dxbc, "in_dw_xbc"),
                              _mm_tn(hn, ddtr, "in_dw_dt")], axis=1)

    g_w_in = jnp.concatenate([g_in_p[:, :QR + KVR + DR], g_in_p[:, LAT_W:LAT_W + D_SSM + D_XBC + SSM_H]], axis=1)
    g_w_uq = jnp.concatenate([g_wq_p[:, :D].reshape(QR, MLA_H, DN), g_wq_p[:, D:].reshape(QR, MLA_H, 128)[:, :, :DR]], axis=2)
    g_w_ukv = jnp.concatenate([g_wkv_p[:, :D].reshape(KVR, MLA_H, DN), g_wkv_p[:, D:].reshape(KVR, MLA_H, DV)], axis=2)
    grads = dict(
        norm_mix_pre=g_mix_pre, norm_mix_post=g_mix_post, norm_ffn_pre=g_ffn_pre, norm_ffn_post=g_ffn_post, w_in=g_w_in,
        q_a_norm=g_qa, w_uq=g_w_uq, kv_a_norm=g_kva, w_ukv=g_w_ukv, attn_out_norm=g_ao, ssm_conv_w=g_scw[:SSM_K],
        ssm_conv_b=g_scb, ssm_dt_bias=g_dtb[:, :SSM_H], ssm_A_log=g_alog[:, :SSM_H], ssm_D=g_dd[:, :SSM_H],
        ssm_norm=g_ssm_norm, w_out=g_w_out, w_up=g_w_up, ffn_conv_w=g_fcw[:FFN_K], ffn_conv_b=g_fcb, w_down=g_w_down)
    return loss, dh0[FRONT:], dh0[PAD_ROWS:FRONT], grads, (pairs, got)


N_CHIPS = 4
BIG = (("w_in", (D, D_IN // N_CHIPS)), ("w_uq", (QR // N_CHIPS, MLA_H, DN + DR)), ("w_ukv", (KVR // N_CHIPS, MLA_H, DN + DV)),
       ("w_out", (2 * D // N_CHIPS, D)), ("w_up", (D, 2 * D_FF // N_CHIPS)), ("w_down", (D_FF // N_CHIPS, D)))
BIG_AXIS = dict(w_in=1, w_uq=0, w_ukv=0, w_out=0, w_up=1, w_down=0)
SHARD_ELEMS = sum(functools.reduce(lambda a, b: a * b, s) for _, s in BIG)
HALF_ROWS = SHARD_ELEMS // 256
SMALL_SHARDED = (("meta_tokens", (N_META, D // N_CHIPS)), ("ssm_conv_w", (SSM_K, D_XBC // N_CHIPS)),
                 ("ffn_conv_w", (FFN_K, 2 * D_FF // N_CHIPS)))
SMALL_REPL = (("norm_mix_pre", D), ("norm_mix_post", D), ("norm_ffn_pre", D), ("norm_ffn_post", D), ("q_a_norm", QR),
              ("kv_a_norm", KVR), ("attn_out_norm", D), ("ssm_conv_b", D_XBC), ("ssm_dt_bias", SSM_H), ("ssm_A_log", SSM_H),
              ("ssm_D", SSM_H), ("ssm_norm", D_SSM), ("ffn_conv_b", 2 * D_FF))
ANY = pl.BlockSpec(memory_space=pl.ANY)


def _pad128(v):
    n = v.shape[0]
    return jnp.concatenate([v, jnp.zeros(((-n) % 128,), v.dtype)]) if n % 128 else v


def _pack_rows(vs, rows):
    flat = jnp.concatenate([_pad128(v.reshape(-1)) for v in vs])
    flat = jnp.concatenate([flat, jnp.zeros((rows * 128 - flat.shape[0],), flat.dtype)])
    return flat.reshape(rows, 128)


def _unpack_rows(pack, sizes):
    flat = pack.reshape(-1)
    out, off = [], 0
    for n in sizes:
        out.append(flat[off:off + n])
        off += n + (-n) % 128
    return out


def _my_place():
    return lax.axis_index("x"), lax.axis_index("y"), lax.axis_index("c")


def _other_chips(x, y):
    return [(1 - x, y), (x, 1 - y), (1 - x, 1 - y)]


def _remote(src, dst, send, recv, dev):
    return pltpu.make_async_remote_copy(src_ref=src, dst_ref=dst, send_sem=send, recv_sem=recv, device_id=dev,
                                        device_id_type=MESH)


SMALL_AG_ROWS = 80


def _gather_weights(shards, small, name):
    arrs = list(shards) + ([] if small is None else [small])
    n, nb = len(arrs), len(shards)

    def body(*refs):
        ins, outs = refs[:n], refs[n:2 * n]
        send, recv, lsem = refs[2 * n:]
        x, y, c = _my_place()
        me = 2 * x + y
        chips = _other_chips(x, y)
        slot = lambda w, chip, cc: outs[w].at[chip, cc] if w < nb else outs[w].at[chip]
        mine = lambda w: slot(w, me, c) if w < nb else ins[w]
        loc = [pltpu.make_async_copy(ins[w], outs[w].at[me], lsem.at[w - nb]) for w in range(nb, n)]
        for cp in loc:
            cp.start()
        sends = []
        for w in range(n):
            for kk, (cx, cy) in enumerate(chips):
                sends.append(_remote(mine(w), slot(w, me, c), send.at[3 * w + kk], recv.at[3 * w + kk], (cx, cy, c)))
        for cp in sends:
            cp.start()
        for w in range(nb):
            for kk, (cx, cy) in enumerate(chips):
                src = 2 * cx + cy
                _remote(mine(w), slot(w, src, c), send.at[3 * w + kk], recv.at[3 * w + kk], (cx, cy, c)).wait_recv()
                fwd = _remote(slot(w, src, c), slot(w, src, c), send.at[3 * (n + w) + kk], recv.at[3 * (n + w) + kk], (x, y, 1 - c))
                fwd.start()
                sends.append(fwd)
        for w in range(n):
            for kk, (cx, cy) in enumerate(chips):
                src = 2 * cx + cy
                if w < nb:
                    _remote(mine(w), slot(w, src, 1 - c), send.at[3 * (n + w) + kk], recv.at[3 * (n + w) + kk],
                            (x, y, 1 - c)).wait_recv()
                else:
                    _remote(ins[w], slot(w, src, c), send.at[3 * w + kk], recv.at[3 * w + kk], (cx, cy, c)).wait_recv()
        for cp in sends:
            cp.wait_send()
        for cp in loc:
            cp.wait()

    return pl.pallas_call(
        body, name=name, in_specs=[ANY] * n, out_specs=[ANY] * n,
        out_shape=[_sds(a.shape, a.dtype) for a in shards] + ([] if small is None else [_sds((N_CHIPS,) + small.shape, small.dtype)]),
        input_output_aliases={w: w for w in range(nb)},
        scratch_shapes=[pltpu.SemaphoreType.DMA((3 * (n + nb),)), pltpu.SemaphoreType.DMA((3 * (n + nb),)),
                        pltpu.SemaphoreType.DMA((max(n - nb, 1),))])(*arrs)


def _place_own(wt, chip, name):
    r, c = wt.shape
    tr = _row_tile(r, c)

    def body(c_ref, w_ref, o_ref):
        o_ref[...] = w_ref[...].astype(BF16)

    return pl.pallas_call(
        body, name=name, out_shape=_sds((N_CHIPS, r, c), BF16),
        grid_spec=pltpu.PrefetchScalarGridSpec(
            num_scalar_prefetch=1, grid=(r // tr,), in_specs=[pl.BlockSpec((tr, c), lambda i, cr: (i, 0))],
            out_specs=pl.BlockSpec((None, tr, c), lambda i, cr: (cr[0], i, 0))),
        compiler_params=_cp("parallel"))(chip, wt)


def _send_sibling_halves(gs, name):
    n = len(gs)

    def body(*refs):
        ins, outs, send, recv = refs[:n], refs[n:2 * n], refs[2 * n], refs[2 * n + 1]
        x, y, c = _my_place()
        cps = [_remote(ins[w].at[:, 1 - c], outs[w], send.at[w], recv.at[w], (x, y, 1 - c)) for w in range(n)]
        for cp in cps:
            cp.start()
        for cp in cps:
            cp.wait()

    return pl.pallas_call(
        body, name=name, in_specs=[ANY] * n, out_specs=[ANY] * n,
        out_shape=[_sds((g.shape[0],) + g.shape[2:], g.dtype) for g in gs],
        scratch_shapes=[pltpu.SemaphoreType.DMA((n,)), pltpu.SemaphoreType.DMA((n,))])(*gs)


def _exchange_chips(ps):
    n = len(ps)

    def body(*refs):
        ins, outs, send, recv = refs[:n], refs[n:2 * n], refs[2 * n], refs[2 * n + 1]
        x, y, c = _my_place()
        cps = [_remote(ins[w].at[2 * cx + cy], outs[w].at[kk], send.at[3 * w + kk], recv.at[3 * w + kk], (cx, cy, c))
               for w in range(n) for kk, (cx, cy) in enumerate(_other_chips(x, y))]
        for cp in cps:
            cp.start()
        for cp in cps:
            cp.wait()

    return pl.pallas_call(
        body, name="exchange_chips", in_specs=[ANY] * n, out_specs=[ANY] * n,
        out_shape=[_sds((3,) + p.shape[1:], p.dtype) for p in ps],
        scratch_shapes=[pltpu.SemaphoreType.DMA((3 * n,)), pltpu.SemaphoreType.DMA((3 * n,))])(*ps)


def _ride_gather(bufs, send, recv, phase):
    n = len(bufs)
    x, y, c = _my_place()
    me = 2 * x + y
    for w in range(n):
        for kk, (cx, cy) in enumerate(_other_chips(x, y)):
            src = 2 * cx + cy
            out = _remote(bufs[w].at[me, c], bufs[w].at[me, c], send.at[3 * w + kk], recv.at[3 * w + kk], (cx, cy, c))
            fwd = _remote(bufs[w].at[src, c], bufs[w].at[src, c], send.at[3 * (n + w) + kk], recv.at[3 * (n + w) + kk],
                          (x, y, 1 - c))
            if phase == 0:
                out.start()
            elif phase == 1:
                _remote(bufs[w].at[me, c], bufs[w].at[src, c], send.at[3 * w + kk], recv.at[3 * w + kk], (cx, cy, c)).wait_recv()
                fwd.start()
            else:
                _remote(bufs[w].at[me, c], bufs[w].at[src, 1 - c], send.at[3 * (n + w) + kk], recv.at[3 * (n + w) + kk],
                        (x, y, 1 - c)).wait_recv()
                out.wait_send()
                fwd.wait_send()


def _ride_exchange(ps, outs, send, recv, phase):
    x, y, c = _my_place()
    for w in range(len(ps)):
        for kk, (cx, cy) in enumerate(_other_chips(x, y)):
            cp = _remote(ps[w].at[2 * cx + cy], outs[w].at[kk], send.at[3 * w + kk], recv.at[3 * w + kk], (cx, cy, c))
            if phase == 0:
                cp.start()
            else:
                cp.wait()


def _share_sibling(halves):
    n = len(halves)

    def body(*refs):
        outs, send, recv = refs[n:2 * n], refs[2 * n], refs[2 * n + 1]
        x, y, c = _my_place()
        cps = [_remote(outs[w].at[c], outs[w].at[c], send.at[w], recv.at[w], (x, y, 1 - c)) for w in range(n)]
        for cp in cps:
            cp.start()
        for w in range(n):
            cps[w].wait_send()
            _remote(outs[w].at[c], outs[w].at[1 - c], send.at[w], recv.at[w], (x, y, 1 - c)).wait_recv()

    return pl.pallas_call(
        body, name="share_sibling", in_specs=[ANY] * n, out_specs=[ANY] * n,
        out_shape=[_sds(h.shape, h.dtype) for h in halves], input_output_aliases={w: w for w in range(n)},
        scratch_shapes=[pltpu.SemaphoreType.DMA((n,)), pltpu.SemaphoreType.DMA((n,))])(*halves)


def _row_tile(r, c, cap=1 << 20):
    return next(t for t in range(r, 0, -1) if r % t == 0 and (t % 8 == 0 or t == r) and t * c * 4 <= cap)


def _add_pair(g, t, core, name):
    _, _, r, c = g.shape
    tr = _row_tile(r, c)

    def body(c_ref, g_ref, t_ref, o_ref):
        o_ref[...] = (g_ref[...] + t_ref[...]).astype(BF16)

    return pl.pallas_call(
        body, name=name, out_shape=_sds(t.shape, BF16),
        grid_spec=pltpu.PrefetchScalarGridSpec(
            num_scalar_prefetch=1, grid=(N_CHIPS, r // tr),
            in_specs=[pl.BlockSpec((None, None, tr, c), lambda j, i, cr: (j, cr[0], i, 0)),
                      pl.BlockSpec((None, tr, c), lambda j, i, cr: (j, i, 0))],
            out_specs=pl.BlockSpec((None, tr, c), lambda j, i, cr: (j, i, 0))),
        compiler_params=_cp("parallel", "parallel"))(core, g, t)


def _add_chips(p, got, chip, name):
    _, r, c = p.shape
    tr = _row_tile(r, c)

    def body(c_ref, p_ref, g_ref, o_ref):
        o_ref[...] = ((p_ref[...].astype(F32) + g_ref[0].astype(F32)) + g_ref[1].astype(F32)) + g_ref[2].astype(F32)

    return pl.pallas_call(
        body, name=name, out_shape=_sds((2, r, c), F32),
        grid_spec=pltpu.PrefetchScalarGridSpec(
            num_scalar_prefetch=1, grid=(r // tr,),
            in_specs=[pl.BlockSpec((None, tr, c), lambda i, cr: (cr[0], i, 0)), pl.BlockSpec((3, tr, c), lambda i, cr: (0, i, 0))],
            out_specs=pl.BlockSpec((None, tr, c), lambda i, cr: (cr[1], i, 0))),
        compiler_params=_cp("parallel"))(chip, p, got)


SMALL_AR_ROWS = 424


def _allreduce_small(v):
    def body(v_ref, o_ref, gath, send, recv):
        x, y, c = _my_place()
        me = 4 * x + 2 * y + c
        gath[me] = v_ref[...]
        cps = []
        for dd in range(1, 8):
            dx, dy, dc = dd >> 2, (dd >> 1) & 1, dd & 1
            peer = (1 - x if dx else x, 1 - y if dy else y, 1 - c if dc else c)
            cps.append(_remote(v_ref, gath.at[me], send.at[dd - 1], recv.at[dd - 1], peer))
        for cp in cps:
            cp.start()
        for cp in cps:
            cp.wait()
        acc = gath[0]
        for dev in range(1, 8):
            acc = acc + gath[dev]
        o_ref[...] = acc

    vm = pl.BlockSpec(memory_space=pltpu.VMEM)
    return pl.pallas_call(
        body, name="allreduce_small", in_specs=[vm], out_specs=vm, out_shape=_sds(v.shape, F32),
        scratch_shapes=[pltpu.VMEM((8,) + v.shape, F32), pltpu.SemaphoreType.DMA((7,)), pltpu.SemaphoreType.DMA((7,))])(v)


def _adamw(w, g, m, v, name):
    r, c = w.shape
    tr = _row_tile(r, c)

    def body(w_ref, g_ref, m_ref, v_ref, d_ref, m2_ref, v2_ref):
        gv = g_ref[...]
        m2 = ADAM_B1 * m_ref[...] + (1.0 - ADAM_B1) * gv
        v2 = ADAM_B2 * v_ref[...] + (1.0 - ADAM_B2) * jnp.square(gv)
        m_hat = m2 / (1.0 - ADAM_B1 ** ADAM_STEP)
        v_hat = v2 / (1.0 - ADAM_B2 ** ADAM_STEP)
        d_ref[...] = -ADAM_LR * (m_hat / (jnp.sqrt(v_hat) + ADAM_EPS) + ADAM_WD * w_ref[...])
        m2_ref[...] = m2
        v2_ref[...] = v2

    return pl.pallas_call(
        body, name=name, grid=(r // tr,), in_specs=[_rows(tr, c)] * 4, out_specs=[_rows(tr, c)] * 3,
        out_shape=[_sds((r, c), F32)] * 3, compiler_params=_cp("parallel"))(w, g, m, v)


WEIGHT_NAMES = ("meta_tokens", "norm_mix_pre", "norm_mix_post", "norm_ffn_pre", "norm_ffn_post", "w_in", "q_a_norm", "w_uq",
                "kv_a_norm", "w_ukv", "attn_out_norm", "ssm_conv_w", "ssm_conv_b", "ssm_dt_bias", "ssm_A_log", "ssm_D",
                "ssm_norm", "w_out", "w_up", "ffn_conv_w", "ffn_conv_b", "w_down")
SMALL_ADAM_ROWS = 192


def kernel(x, meta_tokens, norm_mix_pre, norm_mix_post, norm_ffn_pre, norm_ffn_post, w_in, q_a_norm, w_uq, kv_a_norm, w_ukv, attn_out_norm, ssm_conv_w, ssm_conv_b, ssm_dt_bias, ssm_A_log, ssm_D, ssm_norm, w_out, w_up, ffn_conv_w, ffn_conv_b, w_down, loss_target, m_meta_tokens, m_norm_mix_pre, m_norm_mix_post, m_norm_ffn_pre, m_norm_ffn_post, m_w_in, m_q_a_norm, m_w_uq, m_kv_a_norm, m_w_ukv, m_attn_out_norm, m_ssm_conv_w, m_ssm_conv_b, m_ssm_dt_bias, m_ssm_A_log, m_ssm_D, m_ssm_norm, m_w_out, m_w_up, m_ffn_conv_w, m_ffn_conv_b, m_w_down, v_meta_tokens, v_norm_mix_pre, v_norm_mix_post, v_norm_ffn_pre, v_norm_ffn_post, v_w_in, v_q_a_norm, v_w_uq, v_kv_a_norm, v_w_ukv, v_attn_out_norm, v_ssm_conv_w, v_ssm_conv_b, v_ssm_dt_bias, v_ssm_A_log, v_ssm_D, v_ssm_norm, v_w_out, v_w_up, v_ffn_conv_w, v_ffn_conv_b, v_w_down):
    args = locals()
    w = {n: args[n] for n in WEIGHT_NAMES}
    mom = {n: args["m_" + n] for n in WEIGHT_NAMES}
    var = {n: args["v_" + n] for n in WEIGHT_NAMES}
    cx, cy, cc = _my_place()
    chip = 2 * cx + cy

    two_d = {n: (shp[0], functools.reduce(lambda a, b: a * b, shp[1:])) for n, shp in BIG}
    names = [n for n, _ in BIG]
    core_i = cc.astype(jnp.int32).reshape(1)
    chip_i = chip.astype(jnp.int32).reshape(1)
    early, late = names[:3], names[3:]
    halves = lambda n, a: a.reshape(N_CHIPS, 2, two_d[n][0] // 2, two_d[n][1])
    bufs = {n: halves(n, _place_own(w[n].reshape(two_d[n]), chip_i, "place_" + n)) for n in names}
    small = _pack_rows([w[n] for n, _ in SMALL_SHARDED], SMALL_AG_ROWS)
    *gathered, small_all = _gather_weights([bufs[n] for n in early], small, "allgather_weights")
    gath = {n: a.reshape((N_CHIPS,) + two_d[n]) for n, a in zip(early, gathered)}
    p = dict(w_in=gath["w_in"].transpose(1, 0, 2).reshape(D, D_IN), w_uq=gath["w_uq"].reshape(QR, MLA_H, DN + DR),
             w_ukv=gath["w_ukv"].reshape(KVR, MLA_H, DN + DV))
    sm_parts = [_unpack_rows(small_all[j], [a * b for _, (a, b) in SMALL_SHARDED]) for j in range(N_CHIPS)]
    for i, (n, shp) in enumerate(SMALL_SHARDED):
        p[n] = jnp.concatenate([sm_parts[j][i].reshape(shp) for j in range(N_CHIPS)], axis=1)
    for n, _ in SMALL_REPL:
        p[n] = w[n]
    meta_full = p.pop("meta_tokens")

    place_i = jnp.stack([chip, cc]).astype(jnp.int32)

    def pair_sums(gd, group):
        gs = [halves(n, gd[n]) for n in group]
        from_sib = _send_sibling_halves(gs, "reduce_sibling_" + group[0])
        return [_add_pair(gg, tt, core_i, "reduce_pair_" + n) for n, gg, tt in zip(group, gs, from_sib)]

    loss_part, gx, gmeta, g, (late_pairs, late_got) = _device_step(
        x[0], loss_target[0], meta_full, p, late_bufs=[bufs[n] for n in late], early_reduce=lambda gd: pair_sums(gd, late))

    small_names = [n for n, _ in SMALL_REPL] + ["ssm_conv_w", "ffn_conv_w"]
    small_sizes = [128] + [sz for _, sz in SMALL_REPL] + [N_META * D, SSM_K * D_XBC, FFN_K * 2 * D_FF]
    order = [n for n, _ in SMALL_REPL]
    sp = _pack_rows([loss_part[0]] + [g[n] for n in order] + [gmeta, g["ssm_conv_w"], g["ffn_conv_w"]], SMALL_AR_ROWS)
    red = _unpack_rows(_allreduce_small(sp), small_sizes)
    loss = red[0][0]
    gfull = {n: red[1 + i].reshape(1, -1) for i, n in enumerate(order)}
    n_r = len(order)
    gfull["meta_tokens"] = lax.dynamic_slice_in_dim(red[1 + n_r].reshape(N_META, D), chip * (D // N_CHIPS), D // N_CHIPS, axis=1)
    gfull["ssm_conv_w"] = lax.dynamic_slice_in_dim(red[2 + n_r].reshape(SSM_K, D_XBC), chip * (D_XBC // N_CHIPS),
                                                   D_XBC // N_CHIPS, axis=1)[None]
    gfull["ffn_conv_w"] = lax.dynamic_slice_in_dim(red[3 + n_r].reshape(FFN_K, 2 * D_FF), chip * (2 * D_FF // N_CHIPS),
                                                   2 * D_FF // N_CHIPS, axis=1)[None]

    g["w_in"] = g["w_in"].reshape(D, N_CHIPS, D_IN // N_CHIPS).transpose(1, 0, 2)
    early_pairs = pair_sums(g, early)
    pairs, got = list(early_pairs) + list(late_pairs), list(_exchange_chips(early_pairs)) + list(late_got)
    mine = [_add_chips(pp, gg, place_i, "reduce_chips_" + n) for n, pp, gg in zip(names, pairs, got)]
    for n, both in zip(names, _share_sibling(mine)):
        gfull[n] = both.reshape(two_d[n])

    delta, new_m, new_v = {}, {}, {}
    for n, shp in BIG:
        outs = _adamw(w[n].reshape(two_d[n]), gfull[n], mom[n].reshape(two_d[n]), var[n].reshape(two_d[n]), "adamw_" + n)
        delta[n], new_m[n], new_v[n] = (o.reshape((1,) + shp) for o in outs)
    snames = order + ["meta_tokens", "ssm_conv_w", "ffn_conv_w"]
    ssizes = [functools.reduce(lambda a, b: a * b, w[n].shape) for n in snames]
    packs = [_pack_rows([d[n] for n in snames], SMALL_ADAM_ROWS) for d in (w, gfull, mom, var)]
    outs = _adamw(*packs, "adamw_small")
    for d, o in zip((delta, new_m, new_v), outs):
        for n, piece in zip(snames, _unpack_rows(o, ssizes)):
            d[n] = piece.reshape(w[n].shape)
    gout = {n: gfull[n].reshape(w[n].shape) for n in WEIGHT_NAMES}
    return (loss, gx[None], *[gout[n] for n in WEIGHT_NAMES], *[delta[n] for n in WEIGHT_NAMES],
            *[new_m[n] for n in WEIGHT_NAMES], *[new_v[n] for n in WEIGHT_NAMES])
```

```python
import functools

import jax
import jax.numpy as jnp
from jax import lax
from jax.experimental import pallas as pl
from jax.experimental.pallas import tpu as pltpu

F32 = jnp.float32
BF16 = jnp.bfloat16

D = 1024
N_META = 16
FRONT = 128
PAD_ROWS = FRONT - N_META
MLA_H = 8
DN, DR, DV = 128, 64, 128
QR, KVR = 384, 256
SOFTMAX_SCALE = (DN + DR) ** -0.5
ROPE_THETA = 10000.0
SSM_H, SSM_P, SSM_G, SSM_N, SSM_K = 16, 64, 2, 128, 4
CHUNK = 128
D_SSM = SSM_H * SSM_P
D_XBC = D_SSM + 2 * SSM_G * SSM_N
GSZ = D_SSM // SSM_G
D_FF = 2816
FFN_K = 3
EPS = 1e-6
IN_SPLITS = (QR, KVR, DR, D_SSM, D_XBC, SSM_H)
D_IN = sum(IN_SPLITS)
LAT_W = 768
IN_P = LAT_W + D_SSM + D_XBC + 128
NEG = -1e30
LOG2E = 1.4426950408889634
LN2 = 0.6931471805599453
Q_SCALE = SOFTMAX_SCALE * LOG2E

ADAM_LR, ADAM_B1, ADAM_B2, ADAM_EPS, ADAM_WD, ADAM_STEP = 0.001, 0.9, 0.999, 1e-08, 0.01, 10

VMEM_LIMIT = 56 * 1024 * 1024
MM_ROWS = (640, 320, 128)
MESH = pl.DeviceIdType.MESH


def _sds(shape, dtype):
    return jax.ShapeDtypeStruct(shape, dtype)


def _cp(*sem):
    return pltpu.CompilerParams(dimension_semantics=sem, vmem_limit_bytes=VMEM_LIMIT)


def _rt(n, cands):
    for c in cands:
        if n % c == 0:
            return c
    raise ValueError((n, cands))


def _full(shape):
    nd = len(shape)
    return pl.BlockSpec(shape, lambda *_: (0,) * nd)


def _rows(tr, c):
    return pl.BlockSpec((tr, c), lambda i: (i, 0))


def _sigmoid(x):
    return 1.0 / (1.0 + jnp.exp(-x))


def _silu(x):
    return x * _sigmoid(x)


def _dsilu(x):
    s = _sigmoid(x)
    return s * (1.0 + x * (1.0 - s))


def _softplus(x):
    return jnp.maximum(x, 0.0) + jnp.log(1.0 + jnp.exp(-jnp.abs(x)))


def _rms(x, g):
    r = lax.rsqrt(jnp.mean(x * x, axis=-1, keepdims=True) + EPS)
    return x * r * g


def _rms_bwd(x, g, dy):
    r = lax.rsqrt(jnp.mean(x * x, axis=-1, keepdims=True) + EPS)
    xh = x * r
    dxh = dy * g
    dx = r * (dxh - xh * jnp.mean(dxh * xh, axis=-1, keepdims=True))
    return dx, jnp.sum(dy * xh, axis=0, keepdims=True)


def _dot(a, b):
    return jnp.dot(a, b, preferred_element_type=F32)


def _dot_nt(a, b):
    return lax.dot_general(a, b, (((1,), (1,)), ((), ())), preferred_element_type=F32)


def _dot_tn(a, b):
    return lax.dot_general(a, b, (((0,), (0,)), ((), ())), preferred_element_type=F32)


def _split3(x):
    hi = x.astype(BF16)
    r = x - hi.astype(F32)
    mid = r.astype(BF16)
    return hi, mid, (r - mid.astype(F32)).astype(BF16)


def _dot_hi(a, b, split="a"):
    if split == "a":
        bb = b.astype(BF16)
        return sum(_dot(t, bb) for t in _split3(a))
    ab = a.astype(BF16)
    return sum(_dot(ab, t) for t in _split3(b))


def _dot_nt_hi(a, b):
    bb = b.astype(BF16)
    return sum(_dot_nt(t, bb) for t in _split3(a))


def _shift_down(x, halo, j):
    xr = pltpu.roll(x, j, axis=0)
    hr = pltpu.roll(halo, j, axis=0)
    row = lax.broadcasted_iota(jnp.int32, (8, x.shape[1]), 0)
    first = jnp.where(row < j, hr, xr[:8])
    return jnp.concatenate([first, xr[8:]], axis=0)


def _shift_up(x, nxt, j):
    t = x.shape[0]
    xr = pltpu.roll(x, t - j, axis=0)
    nr = pltpu.roll(nxt, 8 - j, axis=0)
    row = lax.broadcasted_iota(jnp.int32, (8, x.shape[1]), 0)
    last = jnp.where(row + j >= 8, nr, xr[t - 8:])
    return jnp.concatenate([xr[:t - 8], last], axis=0)


def _acc_rows(ref, val, first):
    @pl.when(first)
    def _():
        ref[...] = val

    @pl.when(jnp.logical_not(first))
    def _():
        ref[...] += val


def _mm_tn(a, b, name, tn=None, trs=(1664, 640, 128), chunked=False):
    r, m = a.shape
    n = b.shape[1]
    tn = n if tn is None else tn
    tr = _rt(r, trs)

    def body(a_ref, b_ref, o_ref):
        part = _dot_tn(a_ref[...].astype(BF16), b_ref[...].astype(BF16))
        _acc_rows(o_ref, part, pl.program_id(1) == 0)

    if chunked:
        out_specs, out_shape = pl.BlockSpec((None, m, tn), lambda j, i: (j, 0, 0)), _sds((n // tn, m, tn), F32)
    else:
        out_specs, out_shape = pl.BlockSpec((m, tn), lambda j, i: (0, j)), _sds((m, n), F32)
    return pl.pallas_call(
        body, name=name, grid=(n // tn, r // tr),
        in_specs=[pl.BlockSpec((tr, m), lambda j, i: (i, 0)), pl.BlockSpec((tr, tn), lambda j, i: (i, j))],
        out_specs=out_specs, out_shape=out_shape, compiler_params=_cp("parallel", "arbitrary"))(a, b)


def _inproj(h0, g, w):
    lp = h0.shape[0]
    tr = _rt(lp, MM_ROWS)
    segs = ((0, LAT_W), (LAT_W, LAT_W + D_SSM), (LAT_W + D_SSM, LAT_W + D_SSM + D_XBC), (IN_P - 128, IN_P))

    def body(h_ref, g_ref, w_ref, hn_ref, lat_ref, z_ref, xbc_ref, dt_ref):
        hn = _rms(h_ref[...], g_ref[...]).astype(BF16)
        hn_ref[...] = hn
        for ref, (a, b) in zip((lat_ref, z_ref, xbc_ref, dt_ref), segs):
            ref[...] = _dot(hn, w_ref[:, a:b])

    return pl.pallas_call(
        body, name="inproj", grid=(lp // tr,), in_specs=[_rows(tr, D), _full((1, D)), _full(w.shape)],
        out_specs=[_rows(tr, D), _rows(tr, LAT_W), _rows(tr, D_SSM), _rows(tr, D_XBC), _rows(tr, 128)],
        out_shape=[_sds((lp, D), BF16), _sds((lp, LAT_W), F32), _sds((lp, D_SSM), F32), _sds((lp, D_XBC), F32),
                   _sds((lp, 128), F32)],
        compiler_params=_cp("parallel"))(h0, g, w)


def _rope(x, cos, sa, sb):
    return x * cos + pltpu.roll(x, 96, axis=1) * sa + pltpu.roll(x, 32, axis=1) * sb


def _rope_t(g, cos, sa, sb):
    return g * cos + pltpu.roll(g * sa, 32, axis=1) + pltpu.roll(g * sb, 96, axis=1)


def _mla_prep(lat, qg, kvg, wq, wkv, cos, sa, sb):
    lp = lat.shape[0]
    tr = _rt(lp, MM_ROWS)

    def body(lat_ref, qg_ref, kvg_ref, wq_ref, wkv_ref, cos_ref, sa_ref, sb_ref, q_ref, k_ref, v_ref, ql_ref, kl_ref):
        lat_v = lat_ref[...]
        ql = _rms(lat_v[:, :QR], qg_ref[...]).astype(BF16)
        kl = _rms(lat_v[:, QR:QR + KVR], kvg_ref[...]).astype(BF16)
        ql_ref[...] = ql
        kl_ref[...] = kl
        cos_v, sa_v, sb_v = cos_ref[...], sa_ref[...], sb_ref[...]
        kpe = _rope(lat_v[:, QR + KVR:LAT_W], cos_v, sa_v, sb_v).astype(BF16)
        for h in range(MLA_H):
            q_ref[h, :, 0:DN] = (_dot(ql, wq_ref[:, h * DN:(h + 1) * DN]) * Q_SCALE).astype(BF16)
            qpe = _dot(ql, wq_ref[:, D + h * 128:D + (h + 1) * 128])
            q_ref[h, :, DN:2 * DN] = (_rope(qpe, cos_v, sa_v, sb_v) * Q_SCALE).astype(BF16)
            k_ref[h, :, 0:DN] = _dot(kl, wkv_ref[:, h * DN:(h + 1) * DN]).astype(BF16)
            k_ref[h, :, DN:2 * DN] = kpe
            v_ref[h] = _dot(kl, wkv_ref[:, D + h * DV:D + (h + 1) * DV]).astype(BF16)

    hb = lambda w: pl.BlockSpec((MLA_H, tr, w), lambda i: (0, i, 0))
    return pl.pallas_call(
        body, name="mla_prep", grid=(lp // tr,),
        in_specs=[_rows(tr, LAT_W), _full((1, QR)), _full((1, KVR)), _full(wq.shape), _full(wkv.shape),
                  _rows(tr, 128), _rows(tr, 128), _rows(tr, 128)],
        out_specs=[hb(256), hb(256), hb(128), _rows(tr, QR), _rows(tr, KVR)],
        out_shape=[_sds((MLA_H, lp, 256), BF16), _sds((MLA_H, lp, 256), BF16), _sds((MLA_H, lp, 128), BF16),
                   _sds((lp, QR), BF16), _sds((lp, KVR), BF16)],
        compiler_params=_cp("parallel"))(lat, qg, kvg, wq, wkv, cos, sa, sb)


def _attn_mask(r0, c0, tq, tk, transposed=False):
    if transposed:
        kk = c0 + lax.broadcasted_iota(jnp.int32, (tk, tq), 0)
        qq = r0 + lax.broadcasted_iota(jnp.int32, (tk, tq), 1)
    else:
        qq = r0 + lax.broadcasted_iota(jnp.int32, (tq, tk), 0)
        kk = c0 + lax.broadcasted_iota(jnp.int32, (tq, tk), 1)
    return jnp.logical_and(kk <= qq, kk >= PAD_ROWS)


def _attn_fwd(q, k, v, ride=()):
    lp = q.shape[1]
    t = _rt(lp, (640, 128))
    nq = lp // t

    hp = 2

    nr = len(ride)
    steps = (MLA_H // hp) * nq

    def body(q_ref, k_ref, v_ref, *rest):
        o_ref, lse_ref = rest[nr:nr + 2]
        bufs, sems = rest[nr + 2:2 * nr + 2], rest[2 * nr + 2:]
        qi = pl.program_id(1)
        step = pl.program_id(0) * nq + qi
        if nr:
            pl.when(step == 0)(lambda: _ride_gather(bufs, *sems, 0))
            pl.when(step == steps // 2)(lambda: _ride_gather(bufs, *sems, 1))
        qv = [q_ref[a] for a in range(hp)]

        def tile(kj, carries, masked, live=None):
            kv_rows = pl.ds(pl.multiple_of(kj * t, t), t)
            out = []
            for a in range(hp):
                m, l, acc = carries[a]
                kk = k_ref[a, kv_rows, :]
                vv = v_ref[a, kv_rows, :]
                s = _dot_nt(qv[a], kk)
                if masked:
                    keep = _attn_mask(qi * t, kj * t, t, t)
                    if live is not None:
                        keep = jnp.logical_and(keep, live)
                    s = jnp.where(keep, s, NEG)
                m_new = jnp.maximum(m, jnp.max(s, axis=-1, keepdims=True))
                alpha = jnp.exp2(m - m_new)
                p = jnp.exp2(s - m_new)
                l = alpha * l + jnp.sum(p, axis=-1, keepdims=True)
                acc = alpha * acc + _dot(p.astype(BF16), vv)
                out.append((m_new, l, acc))
            return tuple(out)

        init = tuple((jnp.full((t, 1), NEG, F32), jnp.zeros((t, 1), F32), jnp.zeros((t, DV), F32)) for _ in range(hp))
        carries = tile(0, init, True)
        carries = lax.fori_loop(1, qi, lambda kj, c: tile(kj, c, False), carries)
        carries = tile(qi, carries, True, live=qi > 0)
        for a in range(hp):
            m, l, acc = carries[a]
            o_ref[:, a * DV:(a + 1) * DV] = acc / l
            lse_ref[a] = jnp.broadcast_to(m + jnp.log(l) * LOG2E, (t, 128)).T[:8]
        if nr:
            pl.when(step == steps - 1)(lambda: _ride_gather(bufs, *sems, 2))

    sems = [pltpu.SemaphoreType.DMA((6 * nr,)), pltpu.SemaphoreType.DMA((6 * nr,))] if nr else []
    outs = pl.pallas_call(
        body, name="attn_fwd", grid=(MLA_H // hp, nq),
        in_specs=[pl.BlockSpec((hp, t, 256), lambda h, i: (h, i, 0)), pl.BlockSpec((hp, lp, 256), lambda h, i: (h, 0, 0)),
                  pl.BlockSpec((hp, lp, 128), lambda h, i: (h, 0, 0))] + [ANY] * nr,
        out_specs=[pl.BlockSpec((t, hp * DV), lambda h, i: (i, h)), pl.BlockSpec((hp, 8, t), lambda h, i: (h, 0, i))] + [ANY] * nr,
        out_shape=[_sds((lp, MLA_H * DV), F32), _sds((MLA_H, 8, lp), F32)] + [_sds(b.shape, b.dtype) for b in ride],
        input_output_aliases={3 + w: 2 + w for w in range(nr)}, scratch_shapes=sems,
        compiler_params=_cp("arbitrary", "arbitrary"))(q, k, v, *ride)
    return outs[0], outs[1], list(outs[2:])


def _mixout_bwd(mix, g_post, dh1, o, g_ao, w_out):
    lp = o.shape[0]
    tr = _rt(lp, MM_ROWS)

    def body(mix_ref, gp_ref, dh_ref, o_ref, g_ref, w_ref, dmix_ref, dssm_ref, do_ref, dgp_ref, dg_ref, dl_ref):
        i = pl.program_id(0)
        grow = i * tr + lax.broadcasted_iota(jnp.int32, (tr, D), 0)
        dmix, dgp = _rms_bwd(mix_ref[...], gp_ref[...], jnp.where(grow >= PAD_ROWS, dh_ref[...], 0.0))
        dmix = dmix.astype(BF16)
        dmix_ref[...] = dmix
        _acc_rows(dgp_ref, dgp, i == 0)
        dssm_ref[...] = _dot_nt(dmix, w_ref[D:, :])
        ov = o_ref[...]
        do, dg = _rms_bwd(ov, g_ref[...], _dot_nt(dmix, w_ref[:D, :]))
        do_ref[...] = do
        _acc_rows(dg_ref, dg, i == 0)
        prod = do * ov
        lane = lax.broadcasted_iota(jnp.int32, (1, 128), 1)
        cols = jnp.zeros((tr, 128), F32)
        for h in range(MLA_H):
            cols = cols + jnp.sum(prod[:, h * DV:(h + 1) * DV], axis=-1, keepdims=True) * (lane == h).astype(F32)
        dl_ref[...] = cols.T[:MLA_H]

    return pl.pallas_call(
        body, name="mixout_bwd", grid=(lp // tr,),
        in_specs=[_rows(tr, D), _full((1, D)), _rows(tr, D), _rows(tr, D), _full((1, D)), _full(w_out.shape)],
        out_specs=[_rows(tr, D), _rows(tr, D), _rows(tr, D), _full((1, D)), _full((1, D)), pl.BlockSpec((MLA_H, tr), lambda i: (0, i))],
        out_shape=[_sds((lp, D), BF16), _sds((lp, D), F32), _sds((lp, D), F32), _sds((1, D), F32), _sds((1, D), F32),
                   _sds((MLA_H, lp), F32)],
        compiler_params=_cp("arbitrary"))(mix, g_post, dh1, o, g_ao, w_out)


def _attn_bwd(q, k, v, do, lse_row, delta_row, ride=()):
    lp = q.shape[1]
    t = _rt(lp, (640, 128))
    nq = lp // t

    nr = len(ride)

    def body(q_ref, k_ref, v_ref, do_ref, lse_ref, dl_ref, *rest):
        ps = rest[:nr]
        dq_ref, dk_ref, dv_ref = rest[nr:nr + 3]
        got, sems = rest[nr + 3:2 * nr + 3], rest[2 * nr + 3:]
        kj = pl.program_id(1)
        step = pl.program_id(0) * nq + kj
        if nr:
            pl.when(step == 0)(lambda: _ride_exchange(ps, got, *sems, 0))
        kk = k_ref[0]
        vv = v_ref[0]

        @pl.when(kj == 0)
        def _():
            dq_ref[...] = jnp.zeros_like(dq_ref)

        def tile(qi, carry, masked):
            dk, dv = carry
            q_rows = pl.ds(pl.multiple_of(qi * t, t), t)
            qv = q_ref[0, q_rows, :]
            dob = do_ref[q_rows, :].astype(BF16)
            st = _dot_nt(kk, qv)
            if masked:
                st = jnp.where(_attn_mask(qi * t, kj * t, t, t, transposed=True), st, NEG)
            pt = jnp.exp2(st - lse_ref[0, qi])
            dpt = _dot_nt(vv, dob)
            dst = (pt * (dpt - dl_ref[0, qi])).astype(BF16)
            dv = dv + _dot(pt.astype(BF16), dob)
            dk = dk + _dot(dst, qv)
            dq_ref[0, q_rows, :] += _dot_tn(dst, kk)
            return dk, dv

        carry = tile(kj, (jnp.zeros((t, 256), F32), jnp.zeros((t, DV), F32)), True)
        split = jnp.where(kj == 0, nq, kj + 1)
        carry = lax.fori_loop(kj + 1, split, lambda qi, c: tile(qi, c, True), carry)
        dk, dv = lax.fori_loop(split, nq, lambda qi, c: tile(qi, c, False), carry)
        dk_ref[0] = dk * LN2
        dv_ref[0] = dv
        if nr:
            pl.when(step == MLA_H * nq - 1)(lambda: _ride_exchange(ps, got, *sems, 1))

    stat = pl.BlockSpec((1, nq, 1, t), lambda h, j: (h, 0, 0, 0))
    sems = [pltpu.SemaphoreType.DMA((3 * nr,)), pltpu.SemaphoreType.DMA((3 * nr,))] if nr else []
    outs = pl.pallas_call(
        body, name="attn_bwd", grid=(MLA_H, nq),
        in_specs=[pl.BlockSpec((1, lp, 256), lambda h, j: (h, 0, 0)), pl.BlockSpec((1, t, 256), lambda h, j: (h, j, 0)),
                  pl.BlockSpec((1, t, 128), lambda h, j: (h, j, 0)), pl.BlockSpec((lp, DV), lambda h, j: (0, h)), stat, stat]
        + [ANY] * nr,
        out_specs=[pl.BlockSpec((1, lp, 256), lambda h, j: (h, 0, 0)), pl.BlockSpec((1, t, 256), lambda h, j: (h, j, 0)),
                   pl.BlockSpec((1, t, 128), lambda h, j: (h, j, 0))] + [ANY] * nr,
        out_shape=[_sds((MLA_H, lp, 256), F32), _sds((MLA_H, lp, 256), F32), _sds((MLA_H, lp, 128), F32)]
        + [_sds((3,) + p.shape[1:], p.dtype) for p in ride],
        scratch_shapes=sems, compiler_params=_cp("arbitrary", "arbitrary"))(q, k, v, do, lse_row, delta_row, *ride)
    return outs[0], outs[1], outs[2], list(outs[3:])


def _ssd_consts():
    ri = lax.broadcasted_iota(jnp.int32, (CHUNK, CHUNK), 0)
    ci = lax.broadcasted_iota(jnp.int32, (CHUNK, CHUNK), 1)
    expand = (lax.broadcasted_iota(jnp.int32, (128, D_SSM), 0)
              == lax.broadcasted_iota(jnp.int32, (128, D_SSM), 1) // SSM_P).astype(F32)
    return ri, ci, expand


def _ssd_chunk(c, x_ref, xh_ref, dt_ref, dtT_ref, cw_ref, cb_ref, dtb_ref, dtbT_ref, al_ref, alT_ref):
    ri, ci, expand = _ssd_consts()
    x = x_ref[...]
    halo = jnp.where(c > 0, xh_ref[...], 0.0)
    sh = [x] + [_shift_down(x, halo, j) for j in range(1, SSM_K)]
    cv = cb_ref[...]
    for kk in range(SSM_K):
        cv = cv + cw_ref[kk:kk + 1, :] * sh[SSM_K - 1 - kk]
    xa = _silu(cv)
    grow = c * CHUNK + ri
    gcol = c * CHUNK + lax.broadcasted_iota(jnp.int32, (SSM_H, CHUNK), 1)
    sp = dt_ref[...] + dtb_ref[...]
    spT = dtT_ref[...] + dtbT_ref[...]
    dtc = jnp.where(grow >= PAD_ROWS, _softplus(sp), 0.0)
    dtr = jnp.where(gcol >= PAD_ROWS, _softplus(spT), 0.0)
    arow = -jnp.exp(al_ref[...])
    acolT = -jnp.exp(alT_ref[...])
    ltri = (ci <= ri).astype(F32)
    acs = _dot_hi(ltri, dtc * arow, split="b")
    acsT = _dot_hi(dtr * acolT, (ri <= ci).astype(F32))
    return dict(x=x, sh=sh, cv=cv, xa=xa, sp=sp, dtc=dtc, arow=arow, acs=acs, acsT=acsT, ri=ri, ci=ci, expand=expand,
                grow=grow)


def _ssd_mats(k, s_prev):
    xa, acs, acsT, expand, ri, ci = k["xa"], k["acs"], k["acsT"], k["expand"], k["ri"], k["ci"]
    xs = xa[:, :D_SSM]
    dt_e = _dot_hi(k["dtc"], expand)
    acs_e = _dot_hi(acs, expand)
    last_e = acs_e[CHUNK - 1:CHUNK, :]
    ea = jnp.exp(acs_e)
    f = jnp.exp(last_e - acs_e)
    cd = jnp.exp(last_e)
    xdt = xs * dt_e
    bm = [xa[:, D_SSM + g * SSM_N:D_SSM + (g + 1) * SSM_N] for g in range(SSM_G)]
    cm = [xa[:, D_SSM + (SSM_G + g) * SSM_N:D_SSM + (SSM_G + g + 1) * SSM_N] for g in range(SSM_G)]
    bmb = [b.astype(BF16) for b in bm]
    cmb = [cc.astype(BF16) for cc in cm]
    cb = [_dot_nt(cmb[g], bmb[g]) for g in range(SSM_G)]
    lam, mm = [], []
    for h in range(SSM_H):
        diff = acs[:, h:h + 1] - acsT[h:h + 1, :]
        lam_h = jnp.exp(jnp.where(ci <= ri, diff, NEG))
        lam.append(lam_h)
        mm.append(cb[h // (SSM_H // SSM_G)] * lam_h)
    lo = lax.broadcasted_iota(jnp.int32, (CHUNK, 128), 1) < SSM_P
    xdt_h = []
    for h in range(SSM_H):
        pair = xdt[:, (h // 2) * 128:(h // 2 + 1) * 128]
        xdt_h.append(jnp.where(lo if h % 2 == 0 else jnp.logical_not(lo), pair, 0.0).astype(BF16))
    ydiag = jnp.concatenate(
        [_dot(mm[2 * j].astype(BF16), xdt_h[2 * j]) + _dot(mm[2 * j + 1].astype(BF16), xdt_h[2 * j + 1])
         for j in range(SSM_H // 2)], axis=1)
    t_off = [_dot(cmb[g], s_prev[g].astype(BF16)) for g in range(SSM_G)]
    yoff = jnp.concatenate(t_off, axis=1) * ea
    return dict(xs=xs, dt_e=dt_e, acs_e=acs_e, ea=ea, f=f, cd=cd, xdt=xdt, bm=bm, cm=cm, bmb=bmb, cmb=cmb, cb=cb, lam=lam,
                mm=mm, lo=lo, xdt_h=xdt_h, ydiag=ydiag, t_off=t_off, yoff=yoff)


def _ssd_specs(nc, rev):
    ix = (lambda i: nc - 1 - i) if rev else (lambda i: i)
    return [
        pl.BlockSpec((CHUNK, D_XBC), lambda i: (ix(i), 0)),
        pl.BlockSpec((8, D_XBC), lambda i: (jnp.maximum(ix(i) * (CHUNK // 8) - 1, 0), 0)),
        pl.BlockSpec((CHUNK, D_SSM), lambda i: (ix(i), 0)),
        pl.BlockSpec((CHUNK, 128), lambda i: (ix(i), 0)),
        pl.BlockSpec((SSM_H, CHUNK), lambda i: (0, ix(i))),
        _full((8, D_XBC)), _full((1, D_XBC)), _full((1, 128)), _full((SSM_H, 1)), _full((1, 128)), _full((SSM_H, 1)),
        _full((1, D_SSM)), _full((1, D_SSM)),
    ]


def _ssd_fwd(xbc, z, dtr, dtrT, cw, cb, dtb, dtbT, alog, alogT, d_e, ng):
    lp = xbc.shape[0]
    nc = lp // CHUNK

    def body(x_ref, xh_ref, z_ref, dt_ref, dtT_ref, cw_ref, cb_ref, dtb_ref, dtbT_ref, al_ref, alT_ref, de_ref, ng_ref,
             y_ref, st_ref, s_scr):
        c = pl.program_id(0)

        @pl.when(c == 0)
        def _():
            s_scr[...] = jnp.zeros_like(s_scr)

        k = _ssd_chunk(c, x_ref, xh_ref, dt_ref, dtT_ref, cw_ref, cb_ref, dtb_ref, dtbT_ref, al_ref, alT_ref)
        s_prev = [s_scr[g] for g in range(SSM_G)]
        st_ref[0] = s_scr[...]
        m = _ssd_mats(k, s_prev)
        xd = (m["xdt"] * m["f"]).astype(BF16)
        for g in range(SSM_G):
            sl = slice(g * GSZ, (g + 1) * GSZ)
            s_scr[g] = m["cd"][:, sl] * s_prev[g] + _dot(m["bm"][g].T.astype(BF16), xd[:, sl])
        y = m["ydiag"] + m["yoff"] + de_ref[...] * m["xs"]
        u = y * _silu(z_ref[...])
        outs = []
        for g in range(SSM_G):
            ug = u[:, g * GSZ:(g + 1) * GSZ]
            outs.append(ug * lax.rsqrt(jnp.mean(ug * ug, axis=-1, keepdims=True) + EPS))
        y_ref[...] = jnp.concatenate(outs, axis=1) * ng_ref[...]

    return pl.pallas_call(
        body, name="ssd_fwd", grid=(nc,), in_specs=_ssd_specs(nc, False),
        out_specs=[_rows(CHUNK, D_SSM), pl.BlockSpec((1, SSM_G, SSM_N, GSZ), lambda i: (i, 0, 0, 0))],
        out_shape=[_sds((lp, D_SSM), F32), _sds((nc, SSM_G, SSM_N, GSZ), F32)],
        scratch_shapes=[pltpu.VMEM((SSM_G, SSM_N, GSZ), F32)],
        compiler_params=_cp("arbitrary"))(xbc, xbc, z, dtr, dtrT, cw, cb, dtb, dtbT, alog, alogT, d_e, ng)


def _ssd_bwd(dssm, xbc, z, dtr, dtrT, st, cw, cb, dtb, dtbT, alog, alogT, d_e, ng):
    lp = xbc.shape[0]
    nc = lp // CHUNK
    hpg = SSM_H // SSM_G

    def body(dy_ref, x_ref, xh_ref, z_ref, dt_ref, dtT_ref, st_ref, cw_ref, cb_ref, dtb_ref, dtbT_ref, al_ref, alT_ref,
             de_ref, ng_ref, dz_ref, dx_ref, ddt_ref, dcw_ref, dcb_ref, ddtb_ref, dal_ref, dd_ref, dng_ref, ds_scr, nx_scr):
        i = pl.program_id(0)
        c = nc - 1 - i
        first = i == 0

        @pl.when(first)
        def _():
            ds_scr[...] = jnp.zeros_like(ds_scr)
            nx_scr[...] = jnp.zeros_like(nx_scr)

        k = _ssd_chunk(c, x_ref, xh_ref, dt_ref, dtT_ref, cw_ref, cb_ref, dtb_ref, dtbT_ref, al_ref, alT_ref)
        s_prev = [st_ref[0, g] for g in range(SSM_G)]
        m = _ssd_mats(k, s_prev)
        ri, ci, expand = k["ri"], k["ci"], k["expand"]
        xs, acs, acsT = m["xs"], k["acs"], k["acsT"]
        zv = z_ref[...]
        dout = dy_ref[...]
        ngv = ng_ref[...]
        y = m["ydiag"] + m["yoff"] + de_ref[...] * xs
        sz = _silu(zv)
        u = y * sz
        du_parts, dng_parts = [], []
        for g in range(SSM_G):
            sl = slice(g * GSZ, (g + 1) * GSZ)
            dug, dngg = _rms_bwd(u[:, sl], ngv[:, sl], dout[:, sl])
            du_parts.append(dug)
            dng_parts.append(dngg)
        du = jnp.concatenate(du_parts, axis=1)
        _acc_rows(dng_ref, jnp.concatenate(dng_parts, axis=1), first)
        dy = du * sz
        dz_ref[...] = du * y * _dsilu(zv)
        dd_e = jnp.sum(dy * xs, axis=0, keepdims=True)
        _acc_rows(dd_ref, _dot_nt_hi(dd_e, expand), first)
        dxs = de_ref[...] * dy
        dacs_e = dy * m["yoff"]
        dtg = (dy * m["ea"]).astype(BF16)
        dxdt = jnp.zeros_like(xs)
        dlast_e = []
        db, dc, ds_prev = [], [], []
        xd = m["xdt"] * m["f"]
        dxd_all = []
        for g in range(SSM_G):
            sl = slice(g * GSZ, (g + 1) * GSZ)
            dsg = ds_scr[g]
            spb = s_prev[g].astype(BF16)
            dc.append(_dot_nt(dtg[:, sl], spb))
            dsp = _dot(m["cm"][g].T.astype(BF16), dtg[:, sl]) + m["cd"][:, sl] * dsg
            ds_prev.append(dsp)
            dlast_e.append(jnp.sum(dsg * s_prev[g], axis=0, keepdims=True) * m["cd"][:, sl])
            dsb = dsg.astype(BF16)
            db.append(_dot_nt(xd[:, sl].astype(BF16), dsb))
            dxd_all.append(_dot(m["bmb"][g], dsb))
        dxd = jnp.concatenate(dxd_all, axis=1)
        dxdt = dxd * m["f"]
        dff = dxd * xd
        dacs_e = dacs_e - dff
        dlast_row = jnp.concatenate(dlast_e, axis=1) + jnp.sum(dff, axis=0, keepdims=True)
        dacs = jnp.zeros((CHUNK, 128), F32)
        lane = lax.broadcasted_iota(jnp.int32, (1, 128), 1)
        cbT = [_dot_nt(m["bmb"][g], m["cmb"][g]) for g in range(SSM_G)]
        dgs = [jnp.zeros((CHUNK, CHUNK), F32) for _ in range(SSM_G)]
        dgTs = [jnp.zeros((CHUNK, CHUNK), F32) for _ in range(SSM_G)]
        dxdt_pairs = []
        for h in range(SSM_H):
            g = h // hpg
            pr = slice((h // 2) * 128, (h // 2 + 1) * 128)
            lo_h = m["lo"] if h % 2 == 0 else jnp.logical_not(m["lo"])
            dyp = jnp.where(lo_h, dy[:, pr], 0.0).astype(BF16)
            xdp = m["xdt"][:, pr].astype(BF16)
            dm = _dot_nt(dyp, xdp)
            dmT = _dot_nt(xdp, dyp)
            lamT = jnp.exp(jnp.where(ri <= ci, acsT[h:h + 1, :] - acs[:, h:h + 1], NEG))
            mT = cbT[g] * lamT
            dgs[g] = dgs[g] + dm * m["lam"][h]
            dgTs[g] = dgTs[g] + dmT * lamT
            v1 = jnp.sum(dm * m["mm"][h], axis=1, keepdims=True)
            v2 = jnp.sum(dmT * mT, axis=1, keepdims=True)
            dacs = dacs + (v1 - v2) * (lane == h).astype(F32)
            part = _dot(mT.astype(BF16), dyp)
            if h % 2 == 0:
                dxdt_pairs.append(part)
            else:
                dxdt_pairs[-1] = dxdt_pairs[-1] + part
        dxdt = dxdt + jnp.concatenate(dxdt_pairs, axis=1)
        for g in range(SSM_G):
            dc[g] = dc[g] + _dot(dgs[g].astype(BF16), m["bmb"][g])
            db[g] = db[g] + _dot(dgTs[g].astype(BF16), m["cmb"][g])
        dacs = dacs + _dot_nt_hi(dacs_e, expand)
        dlast = _dot_nt_hi(dlast_row, expand)
        dacs = dacs + jnp.where(ri == CHUNK - 1, dlast, 0.0)
        dxs = dxs + dxdt * m["dt_e"]
        ddt = _dot_nt_hi(dxdt * xs, expand)
        da = _dot_hi((ri <= ci).astype(F32), dacs, split="b")
        ddt = ddt + da * k["arow"]
        dA = jnp.sum(da * k["dtc"], axis=0, keepdims=True)
        _acc_rows(dal_ref, dA * k["arow"], first)
        ddtr = jnp.where(k["grow"] >= PAD_ROWS, ddt * _sigmoid(k["sp"]), 0.0)
        ddt_ref[...] = ddtr
        _acc_rows(ddtb_ref, jnp.sum(ddtr, axis=0, keepdims=True), first)
        for g in range(SSM_G):
            ds_scr[g] = ds_prev[g]
        dxa = jnp.concatenate([dxs] + db + dc, axis=1)
        dcv = dxa * _dsilu(k["cv"])
        _acc_rows(dcb_ref, jnp.sum(dcv, axis=0, keepdims=True), first)
        dcw_rows = [jnp.sum(dcv * k["sh"][SSM_K - 1 - kk], axis=0, keepdims=True) for kk in range(SSM_K)]
        dcw_rows.append(jnp.zeros((8 - SSM_K, D_XBC), F32))
        _acc_rows(dcw_ref, jnp.concatenate(dcw_rows, axis=0), first)
        nxt = nx_scr[...]
        dx = cw_ref[SSM_K - 1:SSM_K, :] * dcv
        for j in range(1, SSM_K):
            dx = dx + cw_ref[SSM_K - 1 - j:SSM_K - j, :] * _shift_up(dcv, nxt, j)
        grow_x = c * CHUNK + lax.broadcasted_iota(jnp.int32, (CHUNK, D_XBC), 0)
        dx_ref[...] = jnp.where(grow_x >= PAD_ROWS, dx, 0.0)
        nx_scr[...] = dcv[:8]

    specs = _ssd_specs(nc, True)
    in_specs = [pl.BlockSpec((CHUNK, D_SSM), lambda i: (nc - 1 - i, 0))] + specs[:5] + [
        pl.BlockSpec((1, SSM_G, SSM_N, GSZ), lambda i: (nc - 1 - i, 0, 0, 0))] + specs[5:]
    rv = lambda w: pl.BlockSpec((CHUNK, w), lambda i: (nc - 1 - i, 0))
    return pl.pallas_call(
        body, name="ssd_bwd", grid=(nc,), in_specs=in_specs,
        out_specs=[rv(D_SSM), rv(D_XBC), rv(128), _full((8, D_XBC)), _full((1, D_XBC)), _full((1, 128)), _full((1, 128)),
                   _full((1, 128)), _full((1, D_SSM))],
        out_shape=[_sds((lp, D_SSM), F32), _sds((lp, D_XBC), F32), _sds((lp, 128), F32), _sds((8, D_XBC), F32),
                   _sds((1, D_XBC), F32), _sds((1, 128), F32), _sds((1, 128), F32), _sds((1, 128), F32), _sds((1, D_SSM), F32)],
        scratch_shapes=[pltpu.VMEM((SSM_G, SSM_N, GSZ), F32), pltpu.VMEM((8, D_XBC), F32)],
        compiler_params=_cp("arbitrary"))(dssm, xbc, xbc, z, dtr, dtrT, st, cw, cb, dtb, dtbT, alog, alogT, d_e, ng)


def _mixout_fwd(o, ssm, h0, g_ao, g_post, w):
    lp = o.shape[0]
    tr = _rt(lp, MM_ROWS)

    def body(o_ref, s_ref, h_ref, ga_ref, gp_ref, w_ref, mi_ref, mix_ref, h1_ref):
        mixin = jnp.concatenate([_rms(o_ref[...], ga_ref[...]), s_ref[...]], axis=1).astype(BF16)
        mi_ref[...] = mixin
        mix = _dot(mixin, w_ref[...])
        mix_ref[...] = mix
        grow = pl.program_id(0) * tr + lax.broadcasted_iota(jnp.int32, (tr, D), 0)
        h1_ref[...] = h_ref[...] + jnp.where(grow >= PAD_ROWS, _rms(mix, gp_ref[...]), 0.0)

    return pl.pallas_call(
        body, name="mixout_fwd", grid=(lp // tr,),
        in_specs=[_rows(tr, D), _rows(tr, D), _rows(tr, D), _full((1, D)), _full((1, D)), _full(w.shape)],
        out_specs=[_rows(tr, 2 * D), _rows(tr, D), _rows(tr, D)],
        out_shape=[_sds((lp, 2 * D), BF16), _sds((lp, D), F32), _sds((lp, D), F32)],
        compiler_params=_cp("parallel"))(o, ssm, h0, g_ao, g_post, w)


def _ffn_up(h1, g, w):
    lp = h1.shape[0]
    tr = _rt(lp, MM_ROWS)
    tn = D_FF // 2

    def body(h_ref, g_ref, w_ref, hn_ref, u_ref):
        hn = _rms(h_ref[...], g_ref[...]).astype(BF16)
        hn_ref[...] = hn
        u_ref[...] = _dot(hn, w_ref[...])

    return pl.pallas_call(
        body, name="ffn_up", grid=(lp // tr, 2 * D_FF // tn),
        in_specs=[pl.BlockSpec((tr, D), lambda i, j: (i, 0)), _full((1, D)), pl.BlockSpec((None, D, tn), lambda i, j: (j, 0, 0))],
        out_specs=[pl.BlockSpec((tr, D), lambda i, j: (i, 0)), pl.BlockSpec((tr, tn), lambda i, j: (i, j))],
        out_shape=[_sds((lp, D), BF16), _sds((lp, 2 * D_FF), F32)],
        compiler_params=_cp("parallel", "arbitrary"))(h1, g, w)


def _ffn_dhn(du, w4):
    lp = du.shape[0]
    nch, _, tn = w4.shape
    tr = _rt(lp, MM_ROWS)

    def body(du_ref, w_ref, o_ref):
        acc = _dot_nt(du_ref[:, 0:tn], w_ref[0])
        for j in range(1, nch):
            acc = acc + _dot_nt(du_ref[:, j * tn:(j + 1) * tn], w_ref[j])
        o_ref[...] = acc

    return pl.pallas_call(
        body, name="ffn_dhn", grid=(lp // tr,), in_specs=[_rows(tr, nch * tn), _full(w4.shape)], out_specs=_rows(tr, D),
        out_shape=_sds((lp, D), F32), compiler_params=_cp("parallel"))(du, w4)


FFN_CB = 256


def _ffn_gate(u, cw, cb):
    lp = u.shape[0]
    tr = _rt(lp, (320, 128))

    def body(u_ref, uh_ref, cw_ref, cb_ref, a_ref):
        i = pl.program_id(0)
        for j in range(D_FF // FFN_CB):
            halves = []
            for off in (0, D_FF):
                sl = slice(off + j * FFN_CB, off + (j + 1) * FFN_CB)
                x = u_ref[:, sl]
                halo = jnp.where(i > 0, uh_ref[:, sl], 0.0)
                cv = cb_ref[:, sl] + cw_ref[FFN_K - 1:FFN_K, sl] * x
                for s in range(1, FFN_K):
                    cv = cv + cw_ref[FFN_K - 1 - s:FFN_K - s, sl] * _shift_down(x, halo, s)
                halves.append(cv)
            a_ref[:, j * FFN_CB:(j + 1) * FFN_CB] = (_silu(halves[0]) * halves[1]).astype(BF16)

    return pl.pallas_call(
        body, name="ffn_gate", grid=(lp // tr,),
        in_specs=[_rows(tr, 2 * D_FF), pl.BlockSpec((8, 2 * D_FF), lambda i: (jnp.maximum(i * (tr // 8) - 1, 0), 0)),
                  _full((8, 2 * D_FF)), _full((1, 2 * D_FF))],
        out_specs=_rows(tr, D_FF), out_shape=_sds((lp, D_FF), BF16),
        compiler_params=_cp("parallel"))(u, u, cw, cb)


def _ffn_down(a, w, h1, tgt, g_post):
    lp = a.shape[0]
    tr = _rt(lp, MM_ROWS)

    def body(a_ref, w_ref, h_ref, t_ref, g_ref, dh2_ref, dd_ref, dg_ref, loss_ref):
        i = pl.program_id(0)
        d = _dot(a_ref[...], w_ref[...])
        gv = g_ref[...]
        h2 = h_ref[...] + _rms(d, gv)
        grow = i * tr + lax.broadcasted_iota(jnp.int32, (tr, D), 0)
        err = jnp.where(grow >= FRONT, h2 - t_ref[...], 0.0)
        dh2 = err * (1.0 / D)
        dh2_ref[...] = dh2
        dd, dg = _rms_bwd(d, gv, dh2)
        dd_ref[...] = dd.astype(BF16)
        _acc_rows(dg_ref, dg, i == 0)
        part = 0.5 * jnp.sum(jnp.sum(err * err, axis=1, keepdims=True), axis=0, keepdims=True) * (1.0 / D)
        _acc_rows(loss_ref, jnp.broadcast_to(part, (8, 128)), i == 0)

    return pl.pallas_call(
        body, name="ffn_down", grid=(lp // tr,),
        in_specs=[_rows(tr, D_FF), _full(w.shape), _rows(tr, D), _rows(tr, D), _full((1, D))],
        out_specs=[_rows(tr, D), _rows(tr, D), _full((1, D)), _full((8, 128))],
        out_shape=[_sds((lp, D), F32), _sds((lp, D), BF16), _sds((1, D), F32), _sds((8, 128), F32)],
        compiler_params=_cp("arbitrary"))(a, w, h1, tgt, g_post)


def _ffn_gate_bwd(u, dd, w_down, cw, cb):
    lp = u.shape[0]
    tr = _rt(lp, (320, 128))
    n = lp // tr

    def body(u_ref, uh_ref, dd_ref, wd_ref, cw_ref, cb_ref, du_ref, dcw_ref, dcb_ref, nx_scr):
        i = pl.program_id(0)
        t = n - 1 - i
        first = i == 0

        @pl.when(first)
        def _():
            nx_scr[...] = jnp.zeros_like(nx_scr)

        grow = t * tr + lax.broadcasted_iota(jnp.int32, (tr, FFN_CB), 0)
        ddv = dd_ref[...]
        for j in range(D_FF // FFN_CB):
            cvs, shs, sls = [], [], []
            for off in (0, D_FF):
                sl = slice(off + j * FFN_CB, off + (j + 1) * FFN_CB)
                x = u_ref[:, sl]
                halo = jnp.where(t > 0, uh_ref[:, sl], 0.0)
                sh = [x] + [_shift_down(x, halo, s) for s in range(1, FFN_K)]
                cv = cb_ref[:, sl]
                for kk in range(FFN_K):
                    cv = cv + cw_ref[kk:kk + 1, sl] * sh[FFN_K - 1 - kk]
                cvs.append(cv)
                shs.append(sh)
                sls.append(sl)
            dav = _dot_nt(ddv, wd_ref[j * FFN_CB:(j + 1) * FFN_CB, :])
            dcv = (dav * cvs[1] * _dsilu(cvs[0]), dav * _silu(cvs[0]))
            for hf in range(2):
                sl = sls[hf]
                g = dcv[hf]
                rows = [jnp.sum(g * shs[hf][FFN_K - 1 - kk], axis=0, keepdims=True) for kk in range(FFN_K)]
                rows.append(jnp.zeros((8 - FFN_K, FFN_CB), F32))
                upd_w = jnp.concatenate(rows, axis=0)
                upd_b = jnp.sum(g, axis=0, keepdims=True)

                @pl.when(first)
                def _():
                    dcw_ref[:, sl] = upd_w
                    dcb_ref[:, sl] = upd_b

                @pl.when(jnp.logical_not(first))
                def _():
                    dcw_ref[:, sl] += upd_w
                    dcb_ref[:, sl] += upd_b

                nxt = nx_scr[:, sl]
                du = cw_ref[FFN_K - 1:FFN_K, sl] * g
                for s in range(1, FFN_K):
                    du = du + cw_ref[FFN_K - 1 - s:FFN_K - s, sl] * _shift_up(g, nxt, s)
                du_ref[:, sl] = jnp.where(grow >= PAD_ROWS, du, 0.0).astype(BF16)
                nx_scr[:, sl] = g[:8]

    return pl.pallas_call(
        body, name="ffn_gate_bwd", grid=(n,),
        in_specs=[pl.BlockSpec((tr, 2 * D_FF), lambda i: (n - 1 - i, 0)),
                  pl.BlockSpec((8, 2 * D_FF), lambda i: (jnp.maximum((n - 1 - i) * (tr // 8) - 1, 0), 0)),
                  pl.BlockSpec((tr, D), lambda i: (n - 1 - i, 0)), _full(w_down.shape), _full((8, 2 * D_FF)), _full((1, 2 * D_FF))],
        out_specs=[pl.BlockSpec((tr, 2 * D_FF), lambda i: (n - 1 - i, 0)), _full((8, 2 * D_FF)), _full((1, 2 * D_FF))],
        out_shape=[_sds((lp, 2 * D_FF), BF16), _sds((8, 2 * D_FF), F32), _sds((1, 2 * D_FF), F32)],
        scratch_shapes=[pltpu.VMEM((8, 2 * D_FF), F32)],
        compiler_params=_cp("arbitrary"))(u, u, dd, w_down, cw, cb)


def _norm_bwd_res(x, g, dy, res, name, mask_pad=False, out_dtype=F32):
    lp, c = x.shape
    tr = _rt(lp, (640, 128))

    def body(*refs):
        if res is None:
            x_ref, g_ref, dy_ref, o_ref, dg_ref = refs
        else:
            x_ref, g_ref, dy_ref, r_ref, o_ref, dg_ref = refs
        i = pl.program_id(0)
        dyv = dy_ref[...].astype(F32)
        if mask_pad:
            grow = i * tr + lax.broadcasted_iota(jnp.int32, (tr, c), 0)
            dyv = jnp.where(grow >= PAD_ROWS, dyv, 0.0)
        dx, dg = _rms_bwd(x_ref[...].astype(F32), g_ref[...], dyv)
        if res is not None:
            dx = dx + r_ref[...]
        o_ref[...] = dx.astype(out_dtype)
        _acc_rows(dg_ref, dg, i == 0)

    args = [x, g, dy] + ([] if res is None else [res])
    in_specs = [_rows(tr, c), _full((1, c)), pl.BlockSpec((tr, c), lambda i: (i, 0))] + ([] if res is None else [_rows(tr, c)])
    return pl.pallas_call(
        body, name=name, grid=(lp // tr,), in_specs=in_specs, out_specs=[_rows(tr, c), _full((1, c))],
        out_shape=[_sds((lp, c), out_dtype), _sds((1, c), F32)], compiler_params=_cp("arbitrary"))(*args)


def _mla_bwd(dq, dk, dv, lat, qg, kvg, wq, wkv, cos, sa, sb):
    lp = lat.shape[0]
    tr = _rt(lp, (320, 128))

    def body(dq_ref, dk_ref, dv_ref, lat_ref, qg_ref, kvg_ref, wq_ref, wkv_ref, cos_ref, sa_ref, sb_ref,
             dqf_ref, dkvf_ref, dlat_ref, dqg_ref, dkvg_ref):
        i = pl.program_id(0)
        cos_v, sa_v, sb_v = cos_ref[...], sa_ref[...], sb_ref[...]
        dkpe = jnp.zeros((tr, 128), F32)
        for h in range(MLA_H):
            dqh = dq_ref[h] * SOFTMAX_SCALE
            dqf_ref[:, h * DN:(h + 1) * DN] = dqh[:, :DN].astype(BF16)
            dqf_ref[:, D + h * 128:D + (h + 1) * 128] = _rope_t(dqh[:, DN:], cos_v, sa_v, sb_v).astype(BF16)
            dkh = dk_ref[h]
            dkvf_ref[:, h * DN:(h + 1) * DN] = dkh[:, :DN].astype(BF16)
            dkpe = dkpe + dkh[:, DN:]
            dkvf_ref[:, D + h * DV:D + (h + 1) * DV] = dv_ref[h].astype(BF16)
        dql = _dot_nt(dqf_ref[...], wq_ref[...])
        dkl = _dot_nt(dkvf_ref[...], wkv_ref[...])
        lat_v = lat_ref[...]
        dqc, dqg = _rms_bwd(lat_v[:, :QR], qg_ref[...], dql)
        dkc, dkg = _rms_bwd(lat_v[:, QR:QR + KVR], kvg_ref[...], dkl)
        dlat_ref[:, :QR] = dqc
        dlat_ref[:, QR:QR + KVR] = dkc
        dlat_ref[:, QR + KVR:] = _rope_t(dkpe, cos_v, sa_v, sb_v)
        _acc_rows(dqg_ref, dqg, i == 0)
        _acc_rows(dkvg_ref, dkg, i == 0)

    hb = lambda w: pl.BlockSpec((MLA_H, tr, w), lambda i: (0, i, 0))
    return pl.pallas_call(
        body, name="mla_bwd", grid=(lp // tr,),
        in_specs=[hb(256), hb(256), hb(128), _rows(tr, LAT_W), _full((1, QR)), _full((1, KVR)), _full(wq.shape),
                  _full(wkv.shape), _rows(tr, 128), _rows(tr, 128), _rows(tr, 128)],
        out_specs=[_rows(tr, 2 * D), _rows(tr, 2 * D), _rows(tr, LAT_W), _full((1, QR)), _full((1, KVR))],
        out_shape=[_sds((lp, 2 * D), BF16), _sds((lp, 2 * D), BF16), _sds((lp, LAT_W), F32), _sds((1, QR), F32),
                   _sds((1, KVR), F32)],
        compiler_params=_cp("arbitrary"))(dq, dk, dv, lat, qg, kvg, wq, wkv, cos, sa, sb)


def _inproj_bwd(dlat, dz, dxbc, ddt, w, h0, g, dh1, ride=()):
    lp = h0.shape[0]
    tr = _rt(lp, (320, 128))
    segs = ((0, LAT_W), (LAT_W, LAT_W + D_SSM), (LAT_W + D_SSM, LAT_W + D_SSM + D_XBC), (IN_P - 128, IN_P))
    nr = len(ride)
    steps = lp // tr

    def body(dl_ref, dz_ref, dx_ref, dt_ref, w_ref, h_ref, g_ref, r_ref, *rest):
        ps = rest[:nr]
        o_ref, dg_ref = rest[nr:nr + 2]
        got, sems = rest[nr + 2:2 * nr + 2], rest[2 * nr + 2:]
        step = pl.program_id(0)
        if nr:
            pl.when(step == 0)(lambda: _ride_exchange(ps, got, *sems, 0))
        dhn = jnp.zeros((tr, D), F32)
        for ref, (a, b) in zip((dl_ref, dz_ref, dx_ref, dt_ref), segs):
            dhn = dhn + _dot_nt(ref[...].astype(BF16), w_ref[:, a:b])
        dx, dg = _rms_bwd(h_ref[...], g_ref[...], dhn)
        o_ref[...] = dx + r_ref[...]
        _acc_rows(dg_ref, dg, step == 0)
        if nr:
            pl.when(step == steps - 1)(lambda: _ride_exchange(ps, got, *sems, 1))

    sems = [pltpu.SemaphoreType.DMA((3 * nr,)), pltpu.SemaphoreType.DMA((3 * nr,))] if nr else []
    outs = pl.pallas_call(
        body, name="inproj_bwd", grid=(steps,),
        in_specs=[_rows(tr, LAT_W), _rows(tr, D_SSM), _rows(tr, D_XBC), _rows(tr, 128), _full(w.shape), _rows(tr, D),
                  _full((1, D)), _rows(tr, D)] + [ANY] * nr,
        out_specs=[_rows(tr, D), _full((1, D))] + [ANY] * nr,
        out_shape=[_sds((lp, D), F32), _sds((1, D), F32)] + [_sds((3,) + p.shape[1:], p.dtype) for p in ride],
        scratch_shapes=sems, compiler_params=_cp("arbitrary"))(dlat, dz, dxbc, ddt, w, h0, g, dh1, *ride)
    return outs[0], outs[1], list(outs[2:])


def _rope_tables(lp):
    pos = (jnp.arange(lp, dtype=jnp.int32) - PAD_ROWS).astype(F32)
    inv = ROPE_THETA ** (-jnp.arange(0, DR, 2, dtype=F32) / DR)
    ang = pos[:, None] * inv[None, :]
    cos, sin = jnp.cos(ang), jnp.sin(ang)
    zero = jnp.zeros_like(sin)
    cos128 = jnp.concatenate([cos, cos, cos, cos], axis=1)
    sa128 = jnp.concatenate([-sin, zero, -sin, zero], axis=1)
    sb128 = jnp.concatenate([zero, sin, zero, sin], axis=1)
    return cos128, sa128, sb128


def _pad_rows8(w):
    return jnp.concatenate([w, jnp.zeros((8 - w.shape[0], w.shape[1]), w.dtype)], axis=0)


def _lane_pad(v):
    return jnp.concatenate([v, jnp.zeros((v.shape[0], 128 - v.shape[1]), v.dtype)], axis=1)


def _late_weights(bufs):
    w_out, w_up, w_down = bufs
    return dict(w_out=w_out.reshape(2 * D, D), w_up=w_up.reshape(N_CHIPS, D, 2 * D_FF // N_CHIPS), w_down=w_down.reshape(D_FF, D))


def _device_step(x, tgt, meta, p, late_bufs=(), early_reduce=None, last_reduce=None):
    s = x.shape[0]
    lp = s + FRONT
    zpad = jnp.zeros((PAD_ROWS, D), F32)
    h0 = jnp.concatenate([zpad, meta, x], axis=0)
    tgt_p = jnp.concatenate([jnp.zeros((FRONT, D), F32), tgt], axis=0)
    cos, sa, sb = _rope_tables(lp)

    w_in = p["w_in"]
    w_in_p = jnp.concatenate([w_in[:, :QR + KVR + DR], jnp.zeros((D, 64), BF16), w_in[:, QR + KVR + DR:],
                              jnp.zeros((D, 128 - SSM_H), BF16)], axis=1)
    w_uq = p["w_uq"]
    wq_p = jnp.concatenate([w_uq[:, :, :DN].reshape(QR, MLA_H * DN),
                            jnp.concatenate([w_uq[:, :, DN:], jnp.zeros((QR, MLA_H, 128 - DR), BF16)], axis=2).reshape(QR, MLA_H * 128)],
                           axis=1)
    w_ukv = p["w_ukv"]
    wkv_p = jnp.concatenate([w_ukv[:, :, :DN].reshape(KVR, MLA_H * DN), w_ukv[:, :, DN:].reshape(KVR, MLA_H * DV)], axis=1)
    scw = _pad_rows8(p["ssm_conv_w"])
    fcw = _pad_rows8(p["ffn_conv_w"])
    dtb, alog = _lane_pad(p["ssm_dt_bias"]), _lane_pad(p["ssm_A_log"])
    dtbT, alogT = p["ssm_dt_bias"].reshape(SSM_H, 1), p["ssm_A_log"].reshape(SSM_H, 1)
    d_e = jnp.repeat(p["ssm_D"], SSM_P, axis=1)

    hn, lat, z, xbc, dtr = _inproj(h0, p["norm_mix_pre"], w_in_p)
    dtrT = dtr[:, :SSM_H].T
    q, k, v, qlat, kvlat = _mla_prep(lat, p["q_a_norm"], p["kv_a_norm"], wq_p, wkv_p, cos, sa, sb)
    o, lse, gathered = _attn_fwd(q, k, v, ride=late_bufs)
    if late_bufs:
        p = dict(p, **_late_weights(gathered))
    ssm, st = _ssd_fwd(xbc, z, dtr, dtrT, scw, p["ssm_conv_b"], dtb, dtbT, alog, alogT, d_e, p["ssm_norm"])
    mixin, mix, h1 = _mixout_fwd(o, ssm, h0, p["attn_out_norm"], p["norm_mix_post"], p["w_out"])
    hn2, u = _ffn_up(h1, p["norm_ffn_pre"], p["w_up"])
    a = _ffn_gate(u, fcw, p["ffn_conv_b"])
    dh2, dd, g_ffn_post, loss = _ffn_down(a, p["w_down"], h1, tgt_p, p["norm_ffn_post"])

    g_w_down = _mm_tn(a, dd, "ffn_dw_down", tn=512)
    du, g_fcw, g_fcb = _ffn_gate_bwd(u, dd, p["w_down"], fcw, p["ffn_conv_b"])
    dhn2 = _ffn_dhn(du, p["w_up"])
    g_w_up = _mm_tn(hn2, du, "ffn_dw_up", tn=D_FF // 2, chunked=True)
    dh1, g_ffn_pre = _norm_bwd_res(h1, p["norm_ffn_pre"], dhn2, dh2, "ffn_norm_bwd")
    dmix, dssm, do, g_mix_post, g_ao, delta = _mixout_bwd(mix, p["norm_mix_post"], dh1, o, p["attn_out_norm"], p["w_out"])
    g_w_out = _mm_tn(mixin, dmix, "mix_dw_out", tn=512)
    t = _rt(lp, (640, 128))
    pairs = early_reduce(dict(w_out=g_w_out, w_up=g_w_up, w_down=g_w_down)) if early_reduce else ()
    dq, dk, dv, got = _attn_bwd(q, k, v, do, lse[:, 0, :].reshape(MLA_H, lp // t, 1, t), delta.reshape(MLA_H, lp // t, 1, t),
                                ride=pairs)
    dqf, dkvf, dlat, g_qa, g_kva = _mla_bwd(dq, dk, dv, lat, p["q_a_norm"], p["kv_a_norm"], wq_p, wkv_p, cos, sa, sb)
    g_wq_p = _mm_tn(qlat, dqf, "mla_dw_uq")
    g_wkv_p = _mm_tn(kvlat, dkvf, "mla_dw_ukv")
    dz, dxbc, ddtr, g_scw, g_scb, g_dtb, g_alog, g_dd, g_ssm_norm = _ssd_bwd(
        dssm, xbc, z, dtr, dtrT, st, scw, p["ssm_conv_b"], dtb, dtbT, alog, alogT, d_e, p["ssm_norm"])
    g_in_p = jnp.concatenate([_mm_tn(hn, dlat, "in_dw_lat"), _mm_tn(hn, dz, "in_dw_z"), _mm_tn(hn, dxbc, "in_dw_xbc"),
                              _mm_tn(hn, ddtr, "in_dw_dt")], axis=1)
    g_w_in = jnp.concatenate([g_in_p[:, :QR + KVR + DR], g_in_p[:, LAT_W:LAT_W + D_SSM + D_XBC + SSM_H]], axis=1)
    g_w_uq = jnp.concatenate([g_wq_p[:, :D].reshape(QR, MLA_H, DN), g_wq_p[:, D:].reshape(QR, MLA_H, 128)[:, :, :DR]], axis=2)
    g_w_ukv = jnp.concatenate([g_wkv_p[:, :D].reshape(KVR, MLA_H, DN), g_wkv_p[:, D:].reshape(KVR, MLA_H, DV)], axis=2)
    pairs2 = last_reduce(dict(w_in=g_w_in, w_uq=g_w_uq, w_ukv=g_w_ukv)) if last_reduce else ()
    dh0, g_mix_pre, got2 = _inproj_bwd(dlat, dz, dxbc, ddtr, w_in_p, h0, p["norm_mix_pre"], dh1, ride=pairs2)
    grads = dict(
        norm_mix_pre=g_mix_pre, norm_mix_post=g_mix_post, norm_ffn_pre=g_ffn_pre, norm_ffn_post=g_ffn_post, w_in=g_w_in,
        q_a_norm=g_qa, w_uq=g_w_uq, kv_a_norm=g_kva, w_ukv=g_w_ukv, attn_out_norm=g_ao, ssm_conv_w=g_scw[:SSM_K],
        ssm_conv_b=g_scb, ssm_dt_bias=g_dtb[:, :SSM_H], ssm_A_log=g_alog[:, :SSM_H], ssm_D=g_dd[:, :SSM_H],
        ssm_norm=g_ssm_norm, w_out=g_w_out, w_up=g_w_up, ffn_conv_w=g_fcw[:FFN_K], ffn_conv_b=g_fcb, w_down=g_w_down)
    return loss, dh0[FRONT:], dh0[PAD_ROWS:FRONT], grads, (list(pairs2) + list(pairs), list(got2) + list(got))


N_CHIPS = 4
BIG = (("w_in", (D, D_IN // N_CHIPS)), ("w_uq", (QR // N_CHIPS, MLA_H, DN + DR)), ("w_ukv", (KVR // N_CHIPS, MLA_H, DN + DV)),
       ("w_out", (2 * D // N_CHIPS, D)), ("w_up", (D, 2 * D_FF // N_CHIPS)), ("w_down", (D_FF // N_CHIPS, D)))
SMALL_SHARDED = (("meta_tokens", (N_META, D // N_CHIPS)), ("ssm_conv_w", (SSM_K, D_XBC // N_CHIPS)),
                 ("ffn_conv_w", (FFN_K, 2 * D_FF // N_CHIPS)))
SMALL_REPL = (("norm_mix_pre", D), ("norm_mix_post", D), ("norm_ffn_pre", D), ("norm_ffn_post", D), ("q_a_norm", QR),
              ("kv_a_norm", KVR), ("attn_out_norm", D), ("ssm_conv_b", D_XBC), ("ssm_dt_bias", SSM_H), ("ssm_A_log", SSM_H),
              ("ssm_D", SSM_H), ("ssm_norm", D_SSM), ("ffn_conv_b", 2 * D_FF))
ANY = pl.BlockSpec(memory_space=pl.ANY)


def _pad128(v):
    n = v.shape[0]
    return jnp.concatenate([v, jnp.zeros(((-n) % 128,), v.dtype)]) if n % 128 else v


def _pack_rows(vs, rows):
    flat = jnp.concatenate([_pad128(v.reshape(-1)) for v in vs])
    flat = jnp.concatenate([flat, jnp.zeros((rows * 128 - flat.shape[0],), flat.dtype)])
    return flat.reshape(rows, 128)


def _unpack_rows(pack, sizes):
    flat = pack.reshape(-1)
    out, off = [], 0
    for n in sizes:
        out.append(flat[off:off + n])
        off += n + (-n) % 128
    return out


def _my_place():
    return lax.axis_index("x"), lax.axis_index("y"), lax.axis_index("c")


def _other_chips(x, y):
    return [(1 - x, y), (x, 1 - y), (1 - x, 1 - y)]


def _remote(src, dst, send, recv, dev):
    return pltpu.make_async_remote_copy(src_ref=src, dst_ref=dst, send_sem=send, recv_sem=recv, device_id=dev,
                                        device_id_type=MESH)


SMALL_AG_ROWS = 80


def _gather_weights(shards, small, name):
    arrs = list(shards) + ([] if small is None else [small])
    n, nb = len(arrs), len(shards)

    def body(*refs):
        ins, outs = refs[:n], refs[n:2 * n]
        send, recv, lsem = refs[2 * n:]
        x, y, c = _my_place()
        me = 2 * x + y
        chips = _other_chips(x, y)
        slot = lambda w, chip, cc: outs[w].at[chip, cc] if w < nb else outs[w].at[chip]
        mine = lambda w: slot(w, me, c) if w < nb else ins[w]
        loc = [pltpu.make_async_copy(ins[w], outs[w].at[me], lsem.at[w - nb]) for w in range(nb, n)]
        for cp in loc:
            cp.start()
        sends = []
        for w in range(n):
            for kk, (cx, cy) in enumerate(chips):
                sends.append(_remote(mine(w), slot(w, me, c), send.at[3 * w + kk], recv.at[3 * w + kk], (cx, cy, c)))
        for cp in sends:
            cp.start()
        for w in range(nb):
            for kk, (cx, cy) in enumerate(chips):
                src = 2 * cx + cy
                _remote(mine(w), slot(w, src, c), send.at[3 * w + kk], recv.at[3 * w + kk], (cx, cy, c)).wait_recv()
                fwd = _remote(slot(w, src, c), slot(w, src, c), send.at[3 * (n + w) + kk], recv.at[3 * (n + w) + kk], (x, y, 1 - c))
                fwd.start()
                sends.append(fwd)
        for w in range(n):
            for kk, (cx, cy) in enumerate(chips):
                src = 2 * cx + cy
                if w < nb:
                    _remote(mine(w), slot(w, src, 1 - c), send.at[3 * (n + w) + kk], recv.at[3 * (n + w) + kk],
                            (x, y, 1 - c)).wait_recv()
                else:
                    _remote(ins[w], slot(w, src, c), send.at[3 * w + kk], recv.at[3 * w + kk], (cx, cy, c)).wait_recv()
        for cp in sends:
            cp.wait_send()
        for cp in loc:
            cp.wait()

    return pl.pallas_call(
        body, name=name, in_specs=[ANY] * n, out_specs=[ANY] * n,
        out_shape=[_sds(a.shape, a.dtype) for a in shards] + ([] if small is None else [_sds((N_CHIPS,) + small.shape, small.dtype)]),
        input_output_aliases={w: w for w in range(nb)},
        scratch_shapes=[pltpu.SemaphoreType.DMA((3 * (n + nb),)), pltpu.SemaphoreType.DMA((3 * (n + nb),)),
                        pltpu.SemaphoreType.DMA((max(n - nb, 1),))])(*arrs)


def _place_own(wt, chip, name):
    r, c = wt.shape
    tr = _row_tile(r, c)

    def body(c_ref, w_ref, o_ref):
        o_ref[...] = w_ref[...].astype(BF16)

    return pl.pallas_call(
        body, name=name, out_shape=_sds((N_CHIPS, r, c), BF16),
        grid_spec=pltpu.PrefetchScalarGridSpec(
            num_scalar_prefetch=1, grid=(r // tr,), in_specs=[pl.BlockSpec((tr, c), lambda i, cr: (i, 0))],
            out_specs=pl.BlockSpec((None, tr, c), lambda i, cr: (cr[0], i, 0))),
        compiler_params=_cp("parallel"))(chip, wt)


def _send_sibling_halves(gs, name):
    n = len(gs)

    def body(*refs):
        ins, outs, send, recv = refs[:n], refs[n:2 * n], refs[2 * n], refs[2 * n + 1]
        x, y, c = _my_place()
        cps = [_remote(ins[w].at[:, 1 - c], outs[w], send.at[w], recv.at[w], (x, y, 1 - c)) for w in range(n)]
        for cp in cps:
            cp.start()
        for cp in cps:
            cp.wait()

    return pl.pallas_call(
        body, name=name, in_specs=[ANY] * n, out_specs=[ANY] * n,
        out_shape=[_sds((g.shape[0],) + g.shape[2:], g.dtype) for g in gs],
        scratch_shapes=[pltpu.SemaphoreType.DMA((n,)), pltpu.SemaphoreType.DMA((n,))])(*gs)


def _ride_gather(bufs, send, recv, phase):
    n = len(bufs)
    x, y, c = _my_place()
    me = 2 * x + y
    for w in range(n):
        for kk, (cx, cy) in enumerate(_other_chips(x, y)):
            src = 2 * cx + cy
            out = _remote(bufs[w].at[me, c], bufs[w].at[me, c], send.at[3 * w + kk], recv.at[3 * w + kk], (cx, cy, c))
            fwd = _remote(bufs[w].at[src, c], bufs[w].at[src, c], send.at[3 * (n + w) + kk], recv.at[3 * (n + w) + kk],
                          (x, y, 1 - c))
            if phase == 0:
                out.start()
            elif phase == 1:
                _remote(bufs[w].at[me, c], bufs[w].at[src, c], send.at[3 * w + kk], recv.at[3 * w + kk], (cx, cy, c)).wait_recv()
                fwd.start()
            else:
                _remote(bufs[w].at[me, c], bufs[w].at[src, 1 - c], send.at[3 * (n + w) + kk], recv.at[3 * (n + w) + kk],
                        (x, y, 1 - c)).wait_recv()
                out.wait_send()
                fwd.wait_send()


def _ride_exchange(ps, outs, send, recv, phase):
    x, y, c = _my_place()
    for w in range(len(ps)):
        for kk, (cx, cy) in enumerate(_other_chips(x, y)):
            cp = _remote(ps[w].at[2 * cx + cy], outs[w].at[kk], send.at[3 * w + kk], recv.at[3 * w + kk], (cx, cy, c))
            if phase == 0:
                cp.start()
            else:
                cp.wait()


def _share_sibling(halves):
    n = len(halves)

    def body(*refs):
        outs, send, recv = refs[n:2 * n], refs[2 * n], refs[2 * n + 1]
        x, y, c = _my_place()
        cps = [_remote(outs[w].at[c], outs[w].at[c], send.at[w], recv.at[w], (x, y, 1 - c)) for w in range(n)]
        for cp in cps:
            cp.start()
        for w in range(n):
            cps[w].wait_send()
            _remote(outs[w].at[c], outs[w].at[1 - c], send.at[w], recv.at[w], (x, y, 1 - c)).wait_recv()

    return pl.pallas_call(
        body, name="share_sibling", in_specs=[ANY] * n, out_specs=[ANY] * n,
        out_shape=[_sds(h.shape, h.dtype) for h in halves], input_output_aliases={w: w for w in range(n)},
        scratch_shapes=[pltpu.SemaphoreType.DMA((n,)), pltpu.SemaphoreType.DMA((n,))])(*halves)


def _row_tile(r, c, cap=1 << 20):
    return next(t for t in range(r, 0, -1) if r % t == 0 and (t % 8 == 0 or t == r) and t * c * 4 <= cap)


def _add_pair(g, t, core, name):
    _, _, r, c = g.shape
    tr = _row_tile(r, c)

    def body(c_ref, g_ref, t_ref, o_ref):
        o_ref[...] = (g_ref[...] + t_ref[...]).astype(BF16)

    return pl.pallas_call(
        body, name=name, out_shape=_sds(t.shape, BF16),
        grid_spec=pltpu.PrefetchScalarGridSpec(
            num_scalar_prefetch=1, grid=(N_CHIPS, r // tr),
            in_specs=[pl.BlockSpec((None, None, tr, c), lambda j, i, cr: (j, cr[0], i, 0)),
                      pl.BlockSpec((None, tr, c), lambda j, i, cr: (j, i, 0))],
            out_specs=pl.BlockSpec((None, tr, c), lambda j, i, cr: (j, i, 0))),
        compiler_params=_cp("parallel", "parallel"))(core, g, t)


def _add_chips(p, got, chip, name):
    _, r, c = p.shape
    tr = _row_tile(r, c)

    def body(c_ref, p_ref, g_ref, o_ref):
        o_ref[...] = ((p_ref[...].astype(F32) + g_ref[0].astype(F32)) + g_ref[1].astype(F32)) + g_ref[2].astype(F32)

    return pl.pallas_call(
        body, name=name, out_shape=_sds((2, r, c), F32),
        grid_spec=pltpu.PrefetchScalarGridSpec(
            num_scalar_prefetch=1, grid=(r // tr,),
            in_specs=[pl.BlockSpec((None, tr, c), lambda i, cr: (cr[0], i, 0)), pl.BlockSpec((3, tr, c), lambda i, cr: (0, i, 0))],
            out_specs=pl.BlockSpec((None, tr, c), lambda i, cr: (cr[1], i, 0))),
        compiler_params=_cp("parallel"))(chip, p, got)


SMALL_AR_ROWS = 424


def _allreduce_small(v):
    def body(v_ref, o_ref, gath, send, recv):
        x, y, c = _my_place()
        me = 4 * x + 2 * y + c
        gath[me] = v_ref[...]
        cps = []
        for dd in range(1, 8):
            dx, dy, dc = dd >> 2, (dd >> 1) & 1, dd & 1
            peer = (1 - x if dx else x, 1 - y if dy else y, 1 - c if dc else c)
            cps.append(_remote(v_ref, gath.at[me], send.at[dd - 1], recv.at[dd - 1], peer))
        for cp in cps:
            cp.start()
        for cp in cps:
            cp.wait()
        acc = gath[0]
        for dev in range(1, 8):
            acc = acc + gath[dev]
        o_ref[...] = acc

    vm = pl.BlockSpec(memory_space=pltpu.VMEM)
    return pl.pallas_call(
        body, name="allreduce_small", in_specs=[vm], out_specs=vm, out_shape=_sds(v.shape, F32),
        scratch_shapes=[pltpu.VMEM((8,) + v.shape, F32), pltpu.SemaphoreType.DMA((7,)), pltpu.SemaphoreType.DMA((7,))])(v)


def _adamw(w, g, m, v, name):
    r, c = w.shape
    tr = _row_tile(r, c)

    def body(w_ref, g_ref, m_ref, v_ref, d_ref, m2_ref, v2_ref):
        gv = g_ref[...]
        m2 = ADAM_B1 * m_ref[...] + (1.0 - ADAM_B1) * gv
        v2 = ADAM_B2 * v_ref[...] + (1.0 - ADAM_B2) * jnp.square(gv)
        m_hat = m2 / (1.0 - ADAM_B1 ** ADAM_STEP)
        v_hat = v2 / (1.0 - ADAM_B2 ** ADAM_STEP)
        d_ref[...] = -ADAM_LR * (m_hat / (jnp.sqrt(v_hat) + ADAM_EPS) + ADAM_WD * w_ref[...])
        m2_ref[...] = m2
        v2_ref[...] = v2

    return pl.pallas_call(
        body, name=name, grid=(r // tr,), in_specs=[_rows(tr, c)] * 4, out_specs=[_rows(tr, c)] * 3,
        out_shape=[_sds((r, c), F32)] * 3, compiler_params=_cp("parallel"))(w, g, m, v)


WEIGHT_NAMES = ("meta_tokens", "norm_mix_pre", "norm_mix_post", "norm_ffn_pre", "norm_ffn_post", "w_in", "q_a_norm", "w_uq",
                "kv_a_norm", "w_ukv", "attn_out_norm", "ssm_conv_w", "ssm_conv_b", "ssm_dt_bias", "ssm_A_log", "ssm_D",
                "ssm_norm", "w_out", "w_up", "ffn_conv_w", "ffn_conv_b", "w_down")
SMALL_ADAM_ROWS = 192


def kernel(x, meta_tokens, norm_mix_pre, norm_mix_post, norm_ffn_pre, norm_ffn_post, w_in, q_a_norm, w_uq, kv_a_norm, w_ukv, attn_out_norm, ssm_conv_w, ssm_conv_b, ssm_dt_bias, ssm_A_log, ssm_D, ssm_norm, w_out, w_up, ffn_conv_w, ffn_conv_b, w_down, loss_target, m_meta_tokens, m_norm_mix_pre, m_norm_mix_post, m_norm_ffn_pre, m_norm_ffn_post, m_w_in, m_q_a_norm, m_w_uq, m_kv_a_norm, m_w_ukv, m_attn_out_norm, m_ssm_conv_w, m_ssm_conv_b, m_ssm_dt_bias, m_ssm_A_log, m_ssm_D, m_ssm_norm, m_w_out, m_w_up, m_ffn_conv_w, m_ffn_conv_b, m_w_down, v_meta_tokens, v_norm_mix_pre, v_norm_mix_post, v_norm_ffn_pre, v_norm_ffn_post, v_w_in, v_q_a_norm, v_w_uq, v_kv_a_norm, v_w_ukv, v_attn_out_norm, v_ssm_conv_w, v_ssm_conv_b, v_ssm_dt_bias, v_ssm_A_log, v_ssm_D, v_ssm_norm, v_w_out, v_w_up, v_ffn_conv_w, v_ffn_conv_b, v_w_down):
    args = locals()
    w = {n: args[n] for n in WEIGHT_NAMES}
    mom = {n: args["m_" + n] for n in WEIGHT_NAMES}
    var = {n: args["v_" + n] for n in WEIGHT_NAMES}
    cx, cy, cc = _my_place()
    chip = 2 * cx + cy

    two_d = {n: (shp[0], functools.reduce(lambda a, b: a * b, shp[1:])) for n, shp in BIG}
    names = [n for n, _ in BIG]
    core_i = cc.astype(jnp.int32).reshape(1)
    chip_i = chip.astype(jnp.int32).reshape(1)
    early, late = names[:3], names[3:]
    halves = lambda n, a: a.reshape(N_CHIPS, 2, two_d[n][0] // 2, two_d[n][1])
    bufs = {n: halves(n, _place_own(w[n].reshape(two_d[n]), chip_i, "place_" + n)) for n in names}
    small = _pack_rows([w[n] for n, _ in SMALL_SHARDED], SMALL_AG_ROWS)
    *gathered, small_all = _gather_weights([bufs[n] for n in early], small, "allgather_weights")
    gath = {n: a.reshape((N_CHIPS,) + two_d[n]) for n, a in zip(early, gathered)}
    p = dict(w_in=gath["w_in"].transpose(1, 0, 2).reshape(D, D_IN), w_uq=gath["w_uq"].reshape(QR, MLA_H, DN + DR),
             w_ukv=gath["w_ukv"].reshape(KVR, MLA_H, DN + DV))
    sm_parts = [_unpack_rows(small_all[j], [a * b for _, (a, b) in SMALL_SHARDED]) for j in range(N_CHIPS)]
    for i, (n, shp) in enumerate(SMALL_SHARDED):
        p[n] = jnp.concatenate([sm_parts[j][i].reshape(shp) for j in range(N_CHIPS)], axis=1)
    for n, _ in SMALL_REPL:
        p[n] = w[n]
    meta_full = p.pop("meta_tokens")

    place_i = jnp.stack([chip, cc]).astype(jnp.int32)

    def pair_sums(gd, group):
        gd = dict(gd)
        if "w_in" in gd:
            gd["w_in"] = gd["w_in"].reshape(D, N_CHIPS, D_IN // N_CHIPS).transpose(1, 0, 2)
        gs = [halves(n, gd[n]) for n in group]
        from_sib = _send_sibling_halves(gs, "reduce_sibling_" + group[0])
        return [_add_pair(gg, tt, core_i, "reduce_pair_" + n) for n, gg, tt in zip(group, gs, from_sib)]

    loss_part, gx, gmeta, g, (pairs, got) = _device_step(
        x[0], loss_target[0], meta_full, p, late_bufs=[bufs[n] for n in late], early_reduce=lambda gd: pair_sums(gd, late),
        last_reduce=lambda gd: pair_sums(gd, early))

    small_names = [n for n, _ in SMALL_REPL] + ["ssm_conv_w", "ffn_conv_w"]
    small_sizes = [128] + [sz for _, sz in SMALL_REPL] + [N_META * D, SSM_K * D_XBC, FFN_K * 2 * D_FF]
    order = [n for n, _ in SMALL_REPL]
    sp = _pack_rows([loss_part[0]] + [g[n] for n in order] + [gmeta, g["ssm_conv_w"], g["ffn_conv_w"]], SMALL_AR_ROWS)
    red = _unpack_rows(_allreduce_small(sp), small_sizes)
    loss = red[0][0]
    gfull = {n: red[1 + i].reshape(1, -1) for i, n in enumerate(order)}
    n_r = len(order)
    gfull["meta_tokens"] = lax.dynamic_slice_in_dim(red[1 + n_r].reshape(N_META, D), chip * (D // N_CHIPS), D // N_CHIPS, axis=1)
    gfull["ssm_conv_w"] = lax.dynamic_slice_in_dim(red[2 + n_r].reshape(SSM_K, D_XBC), chip * (D_XBC // N_CHIPS),
                                                   D_XBC // N_CHIPS, axis=1)[None]
    gfull["ffn_conv_w"] = lax.dynamic_slice_in_dim(red[3 + n_r].reshape(FFN_K, 2 * D_FF), chip * (2 * D_FF // N_CHIPS),
                                                   2 * D_FF // N_CHIPS, axis=1)[None]

    mine = [_add_chips(pp, gg, place_i, "reduce_chips_" + n) for n, pp, gg in zip(names, pairs, got)]
    for n, both in zip(names, _share_sibling(mine)):
        gfull[n] = both.reshape(two_d[n])

    delta, new_m, new_v = {}, {}, {}
    for n, shp in BIG:
        outs = _adamw(w[n].reshape(two_d[n]), gfull[n], mom[n].reshape(two_d[n]), var[n].reshape(two_d[n]), "adamw_" + n)
        delta[n], new_m[n], new_v[n] = (o.reshape((1,) + shp) for o in outs)
    snames = order + ["meta_tokens", "ssm_conv_w", "ffn_conv_w"]
    ssizes = [functools.reduce(lambda a, b: a * b, w[n].shape) for n in snames]
    packs = [_pack_rows([d[n] for n in snames], SMALL_ADAM_ROWS) for d in (w, gfull, mom, var)]
    outs = _adamw(*packs, "adamw_small")
    for d, o in zip((delta, new_m, new_v), outs):
        for n, piece in zip(snames, _unpack_rows(o, ssizes)):
            d[n] = piece.reshape(w[n].shape)
    gout = {n: gfull[n].reshape(w[n].shape) for n in WEIGHT_NAMES}
    return (loss, gx[None], *[gout[n] for n in WEIGHT_NAMES], *[delta[n] for n in WEIGHT_NAMES],
            *[new_m[n] for n in WEIGHT_NAMES], *[new_v[n] for n in WEIGHT_NAMES])
```

```python
import functools

import jax
import jax.numpy as jnp
from jax import lax
from jax.experimental import pallas as pl
from jax.experimental.pallas import tpu as pltpu

F32 = jnp.float32
BF16 = jnp.bfloat16

D = 1024
N_META = 16
FRONT = 128
PAD_ROWS = FRONT - N_META
MLA_H = 8
DN, DR, DV = 128, 64, 128
QR, KVR = 384, 256
SOFTMAX_SCALE = (DN + DR) ** -0.5
ROPE_THETA = 10000.0
SSM_H, SSM_P, SSM_G, SSM_N, SSM_K = 16, 64, 2, 128, 4
CHUNK = 128
D_SSM = SSM_H * SSM_P
D_XBC = D_SSM + 2 * SSM_G * SSM_N
GSZ = D_SSM // SSM_G
D_FF = 2816
FFN_K = 3
EPS = 1e-6
IN_SPLITS = (QR, KVR, DR, D_SSM, D_XBC, SSM_H)
D_IN = sum(IN_SPLITS)
LAT_W = 768
IN_P = LAT_W + D_SSM + D_XBC + 128
NEG = -1e30
LOG2E = 1.4426950408889634
LN2 = 0.6931471805599453
Q_SCALE = SOFTMAX_SCALE * LOG2E

ADAM_LR, ADAM_B1, ADAM_B2, ADAM_EPS, ADAM_WD, ADAM_STEP = 0.001, 0.9, 0.999, 1e-08, 0.01, 10

VMEM_LIMIT = 56 * 1024 * 1024
MM_ROWS = (640, 320, 128)
MESH = pl.DeviceIdType.MESH


def _sds(shape, dtype):
    return jax.ShapeDtypeStruct(shape, dtype)


def _cp(*sem):
    return pltpu.CompilerParams(dimension_semantics=sem, vmem_limit_bytes=VMEM_LIMIT)


def _rt(n, cands):
    for c in cands:
        if n % c == 0:
            return c
    raise ValueError((n, cands))


def _full(shape):
    nd = len(shape)
    return pl.BlockSpec(shape, lambda *_: (0,) * nd)


def _rows(tr, c):
    return pl.BlockSpec((tr, c), lambda i: (i, 0))


def _sigmoid(x):
    return 1.0 / (1.0 + jnp.exp(-x))


def _silu(x):
    return x * _sigmoid(x)


def _dsilu(x):
    s = _sigmoid(x)
    return s * (1.0 + x * (1.0 - s))


def _softplus(x):
    return jnp.maximum(x, 0.0) + jnp.log(1.0 + jnp.exp(-jnp.abs(x)))


def _rms(x, g):
    r = lax.rsqrt(jnp.mean(x * x, axis=-1, keepdims=True) + EPS)
    return x * r * g


def _rms_bwd(x, g, dy):
    r = lax.rsqrt(jnp.mean(x * x, axis=-1, keepdims=True) + EPS)
    xh = x * r
    dxh = dy * g
    dx = r * (dxh - xh * jnp.mean(dxh * xh, axis=-1, keepdims=True))
    return dx, jnp.sum(dy * xh, axis=0, keepdims=True)


def _dot(a, b):
    return jnp.dot(a, b, preferred_element_type=F32)


def _dot_nt(a, b):
    return lax.dot_general(a, b, (((1,), (1,)), ((), ())), preferred_element_type=F32)


def _dot_tn(a, b):
    return lax.dot_general(a, b, (((0,), (0,)), ((), ())), preferred_element_type=F32)


def _split3(x):
    hi = x.astype(BF16)
    r = x - hi.astype(F32)
    mid = r.astype(BF16)
    return hi, mid, (r - mid.astype(F32)).astype(BF16)


def _dot_hi(a, b, split="a"):
    if split == "a":
        bb = b.astype(BF16)
        return sum(_dot(t, bb) for t in _split3(a))
    ab = a.astype(BF16)
    return sum(_dot(ab, t) for t in _split3(b))


def _dot_nt_hi(a, b):
    bb = b.astype(BF16)
    return sum(_dot_nt(t, bb) for t in _split3(a))


def _shift_down(x, halo, j):
    xr = pltpu.roll(x, j, axis=0)
    hr = pltpu.roll(halo, j, axis=0)
    row = lax.broadcasted_iota(jnp.int32, (8, x.shape[1]), 0)
    first = jnp.where(row < j, hr, xr[:8])
    return jnp.concatenate([first, xr[8:]], axis=0)


def _shift_up(x, nxt, j):
    t = x.shape[0]
    xr = pltpu.roll(x, t - j, axis=0)
    nr = pltpu.roll(nxt, 8 - j, axis=0)
    row = lax.broadcasted_iota(jnp.int32, (8, x.shape[1]), 0)
    last = jnp.where(row + j >= 8, nr, xr[t - 8:])
    return jnp.concatenate([xr[:t - 8], last], axis=0)


def _acc_rows(ref, val, first):
    @pl.when(first)
    def _():
        ref[...] = val

    @pl.when(jnp.logical_not(first))
    def _():
        ref[...] += val


def _mm_tn(a, b, name, tn=None, trs=(1664, 640, 128), chunked=False):
    r, m = a.shape
    n = b.shape[1]
    tn = n if tn is None else tn
    tr = _rt(r, trs)

    def body(a_ref, b_ref, o_ref):
        part = _dot_tn(a_ref[...].astype(BF16), b_ref[...].astype(BF16))
        _acc_rows(o_ref, part, pl.program_id(1) == 0)

    if chunked:
        out_specs, out_shape = pl.BlockSpec((None, m, tn), lambda j, i: (j, 0, 0)), _sds((n // tn, m, tn), F32)
    else:
        out_specs, out_shape = pl.BlockSpec((m, tn), lambda j, i: (0, j)), _sds((m, n), F32)
    return pl.pallas_call(
        body, name=name, grid=(n // tn, r // tr),
        in_specs=[pl.BlockSpec((tr, m), lambda j, i: (i, 0)), pl.BlockSpec((tr, tn), lambda j, i: (i, j))],
        out_specs=out_specs, out_shape=out_shape, compiler_params=_cp("parallel", "arbitrary"))(a, b)


def _inproj(h0, g, w):
    lp = h0.shape[0]
    tr = _rt(lp, MM_ROWS)
    segs = ((0, LAT_W), (LAT_W, LAT_W + D_SSM), (LAT_W + D_SSM, LAT_W + D_SSM + D_XBC), (IN_P - 128, IN_P))

    def body(h_ref, g_ref, w_ref, hn_ref, lat_ref, z_ref, xbc_ref, dt_ref):
        hn = _rms(h_ref[...], g_ref[...]).astype(BF16)
        hn_ref[...] = hn
        for ref, (a, b) in zip((lat_ref, z_ref, xbc_ref, dt_ref), segs):
            ref[...] = _dot(hn, w_ref[:, a:b])

    return pl.pallas_call(
        body, name="inproj", grid=(lp // tr,), in_specs=[_rows(tr, D), _full((1, D)), _full(w.shape)],
        out_specs=[_rows(tr, D), _rows(tr, LAT_W), _rows(tr, D_SSM), _rows(tr, D_XBC), _rows(tr, 128)],
        out_shape=[_sds((lp, D), BF16), _sds((lp, LAT_W), F32), _sds((lp, D_SSM), F32), _sds((lp, D_XBC), F32),
                   _sds((lp, 128), F32)],
        compiler_params=_cp("parallel"))(h0, g, w)


def _rope(x, cos, sa, sb):
    return x * cos + pltpu.roll(x, 96, axis=1) * sa + pltpu.roll(x, 32, axis=1) * sb


def _rope_t(g, cos, sa, sb):
    return g * cos + pltpu.roll(g * sa, 32, axis=1) + pltpu.roll(g * sb, 96, axis=1)


def _mla_prep(lat, qg, kvg, wq, wkv, cos, sa, sb):
    lp = lat.shape[0]
    tr = _rt(lp, MM_ROWS)

    def body(lat_ref, qg_ref, kvg_ref, wq_ref, wkv_ref, cos_ref, sa_ref, sb_ref, q_ref, k_ref, v_ref, ql_ref, kl_ref):
        lat_v = lat_ref[...]
        ql = _rms(lat_v[:, :QR], qg_ref[...]).astype(BF16)
        kl = _rms(lat_v[:, QR:QR + KVR], kvg_ref[...]).astype(BF16)
        ql_ref[...] = ql
        kl_ref[...] = kl
        cos_v, sa_v, sb_v = cos_ref[...], sa_ref[...], sb_ref[...]
        kpe = _rope(lat_v[:, QR + KVR:LAT_W], cos_v, sa_v, sb_v).astype(BF16)
        for h in range(MLA_H):
            q_ref[h, :, 0:DN] = (_dot(ql, wq_ref[:, h * DN:(h + 1) * DN]) * Q_SCALE).astype(BF16)
            qpe = _dot(ql, wq_ref[:, D + h * 128:D + (h + 1) * 128])
            q_ref[h, :, DN:2 * DN] = (_rope(qpe, cos_v, sa_v, sb_v) * Q_SCALE).astype(BF16)
            k_ref[h, :, 0:DN] = _dot(kl, wkv_ref[:, h * DN:(h + 1) * DN]).astype(BF16)
            k_ref[h, :, DN:2 * DN] = kpe
            v_ref[h] = _dot(kl, wkv_ref[:, D + h * DV:D + (h + 1) * DV]).astype(BF16)

    hb = lambda w: pl.BlockSpec((MLA_H, tr, w), lambda i: (0, i, 0))
    return pl.pallas_call(
        body, name="mla_prep", grid=(lp // tr,),
        in_specs=[_rows(tr, LAT_W), _full((1, QR)), _full((1, KVR)), _full(wq.shape), _full(wkv.shape),
                  _rows(tr, 128), _rows(tr, 128), _rows(tr, 128)],
        out_specs=[hb(256), hb(256), hb(128), _rows(tr, QR), _rows(tr, KVR)],
        out_shape=[_sds((MLA_H, lp, 256), BF16), _sds((MLA_H, lp, 256), BF16), _sds((MLA_H, lp, 128), BF16),
                   _sds((lp, QR), BF16), _sds((lp, KVR), BF16)],
        compiler_params=_cp("parallel"))(lat, qg, kvg, wq, wkv, cos, sa, sb)


def _attn_mask(r0, c0, tq, tk, transposed=False):
    if transposed:
        kk = c0 + lax.broadcasted_iota(jnp.int32, (tk, tq), 0)
        qq = r0 + lax.broadcasted_iota(jnp.int32, (tk, tq), 1)
    else:
        qq = r0 + lax.broadcasted_iota(jnp.int32, (tq, tk), 0)
        kk = c0 + lax.broadcasted_iota(jnp.int32, (tq, tk), 1)
    return jnp.logical_and(kk <= qq, kk >= PAD_ROWS)


def _attn_fwd(q, k, v, ride=()):
    lp = q.shape[1]
    t = _rt(lp, (640, 128))
    nq = lp // t

    hp = 2

    nr = len(ride)
    steps = (MLA_H // hp) * nq

    def body(q_ref, k_ref, v_ref, *rest):
        o_ref, lse_ref = rest[nr:nr + 2]
        bufs, sems = rest[nr + 2:2 * nr + 2], rest[2 * nr + 2:]
        qi = pl.program_id(1)
        step = pl.program_id(0) * nq + qi
        if nr:
            pl.when(step == 0)(lambda: _ride_gather(bufs, *sems, 0))
            pl.when(step == steps // 2)(lambda: _ride_gather(bufs, *sems, 1))
        qv = [q_ref[a] for a in range(hp)]

        def tile(kj, carries, masked, live=None):
            kv_rows = pl.ds(pl.multiple_of(kj * t, t), t)
            out = []
            for a in range(hp):
                m, l, acc = carries[a]
                kk = k_ref[a, kv_rows, :]
                vv = v_ref[a, kv_rows, :]
                s = _dot_nt(qv[a], kk)
                if masked:
                    keep = _attn_mask(qi * t, kj * t, t, t)
                    if live is not None:
                        keep = jnp.logical_and(keep, live)
                    s = jnp.where(keep, s, NEG)
                m_new = jnp.maximum(m, jnp.max(s, axis=-1, keepdims=True))
                alpha = jnp.exp2(m - m_new)
                p = jnp.exp2(s - m_new)
                l = alpha * l + jnp.sum(p, axis=-1, keepdims=True)
                acc = alpha * acc + _dot(p.astype(BF16), vv)
                out.append((m_new, l, acc))
            return tuple(out)

        init = tuple((jnp.full((t, 1), NEG, F32), jnp.zeros((t, 1), F32), jnp.zeros((t, DV), F32)) for _ in range(hp))
        carries = tile(0, init, True)
        carries = lax.fori_loop(1, qi, lambda kj, c: tile(kj, c, False), carries)
        carries = tile(qi, carries, True, live=qi > 0)
        for a in range(hp):
            m, l, acc = carries[a]
            o_ref[:, a * DV:(a + 1) * DV] = acc / l
            lse_ref[a] = jnp.broadcast_to(m + jnp.log(l) * LOG2E, (t, 128)).T[:8]
        if nr:
            pl.when(step == steps - 1)(lambda: _ride_gather(bufs, *sems, 2))

    sems = [pltpu.SemaphoreType.DMA((6 * nr,)), pltpu.SemaphoreType.DMA((6 * nr,))] if nr else []
    outs = pl.pallas_call(
        body, name="attn_fwd", grid=(MLA_H // hp, nq),
        in_specs=[pl.BlockSpec((hp, t, 256), lambda h, i: (h, i, 0)), pl.BlockSpec((hp, lp, 256), lambda h, i: (h, 0, 0)),
                  pl.BlockSpec((hp, lp, 128), lambda h, i: (h, 0, 0))] + [ANY] * nr,
        out_specs=[pl.BlockSpec((t, hp * DV), lambda h, i: (i, h)), pl.BlockSpec((hp, 8, t), lambda h, i: (h, 0, i))] + [ANY] * nr,
        out_shape=[_sds((lp, MLA_H * DV), F32), _sds((MLA_H, 8, lp), F32)] + [_sds(b.shape, b.dtype) for b in ride],
        input_output_aliases={3 + w: 2 + w for w in range(nr)}, scratch_shapes=sems,
        compiler_params=_cp("arbitrary", "arbitrary"))(q, k, v, *ride)
    return outs[0], outs[1], list(outs[2:])


def _mixout_bwd(mix, g_post, dh1, o, g_ao, w_out):
    lp = o.shape[0]
    tr = _rt(lp, MM_ROWS)

    def body(mix_ref, gp_ref, dh_ref, o_ref, g_ref, w_ref, dmix_ref, dssm_ref, do_ref, dgp_ref, dg_ref, dl_ref):
        i = pl.program_id(0)
        grow = i * tr + lax.broadcasted_iota(jnp.int32, (tr, D), 0)
        dmix, dgp = _rms_bwd(mix_ref[...], gp_ref[...], jnp.where(grow >= PAD_ROWS, dh_ref[...], 0.0))
        dmix = dmix.astype(BF16)
        dmix_ref[...] = dmix
        _acc_rows(dgp_ref, dgp, i == 0)
        dssm_ref[...] = _dot_nt(dmix, w_ref[D:, :])
        ov = o_ref[...]
        do, dg = _rms_bwd(ov, g_ref[...], _dot_nt(dmix, w_ref[:D, :]))
        do_ref[...] = do
        _acc_rows(dg_ref, dg, i == 0)
        prod = do * ov
        lane = lax.broadcasted_iota(jnp.int32, (1, 128), 1)
        cols = jnp.zeros((tr, 128), F32)
        for h in range(MLA_H):
            cols = cols + jnp.sum(prod[:, h * DV:(h + 1) * DV], axis=-1, keepdims=True) * (lane == h).astype(F32)
        dl_ref[...] = cols.T[:MLA_H]

    return pl.pallas_call(
        body, name="mixout_bwd", grid=(lp // tr,),
        in_specs=[_rows(tr, D), _full((1, D)), _rows(tr, D), _rows(tr, D), _full((1, D)), _full(w_out.shape)],
        out_specs=[_rows(tr, D), _rows(tr, D), _rows(tr, D), _full((1, D)), _full((1, D)), pl.BlockSpec((MLA_H, tr), lambda i: (0, i))],
        out_shape=[_sds((lp, D), BF16), _sds((lp, D), F32), _sds((lp, D), F32), _sds((1, D), F32), _sds((1, D), F32),
                   _sds((MLA_H, lp), F32)],
        compiler_params=_cp("arbitrary"))(mix, g_post, dh1, o, g_ao, w_out)


def _attn_bwd(q, k, v, do, lse_row, delta_row, ride=()):
    lp = q.shape[1]
    t = _rt(lp, (640, 128))
    nq = lp // t

    nr = len(ride)

    def body(q_ref, k_ref, v_ref, do_ref, lse_ref, dl_ref, *rest):
        ps = rest[:nr]
        dq_ref, dk_ref, dv_ref = rest[nr:nr + 3]
        got, sems = rest[nr + 3:2 * nr + 3], rest[2 * nr + 3:]
        kj = pl.program_id(1)
        step = pl.program_id(0) * nq + kj
        if nr:
            pl.when(step == 0)(lambda: _ride_exchange(ps, got, *sems, 0))
        kk = k_ref[0]
        vv = v_ref[0]

        @pl.when(kj == 0)
        def _():
            dq_ref[...] = jnp.zeros_like(dq_ref)

        def tile(qi, carry, masked):
            dk, dv = carry
            q_rows = pl.ds(pl.multiple_of(qi * t, t), t)
            qv = q_ref[0, q_rows, :]
            dob = do_ref[q_rows, :].astype(BF16)
            st = _dot_nt(kk, qv)
            if masked:
                st = jnp.where(_attn_mask(qi * t, kj * t, t, t, transposed=True), st, NEG)
            pt = jnp.exp2(st - lse_ref[0, qi])
            dpt = _dot_nt(vv, dob)
            dst = (pt * (dpt - dl_ref[0, qi])).astype(BF16)
            dv = dv + _dot(pt.astype(BF16), dob)
            dk = dk + _dot(dst, qv)
            dq_ref[0, q_rows, :] += _dot_tn(dst, kk)
            return dk, dv

        carry = tile(kj, (jnp.zeros((t, 256), F32), jnp.zeros((t, DV), F32)), True)
        split = jnp.where(kj == 0, nq, kj + 1)
        carry = lax.fori_loop(kj + 1, split, lambda qi, c: tile(qi, c, True), carry)
        dk, dv = lax.fori_loop(split, nq, lambda qi, c: tile(qi, c, False), carry)
        dk_ref[0] = dk * LN2
        dv_ref[0] = dv
        if nr:
            pl.when(step == MLA_H * nq - 1)(lambda: _ride_exchange(ps, got, *sems, 1))

    stat = pl.BlockSpec((1, nq, 1, t), lambda h, j: (h, 0, 0, 0))
    sems = [pltpu.SemaphoreType.DMA((3 * nr,)), pltpu.SemaphoreType.DMA((3 * nr,))] if nr else []
    outs = pl.pallas_call(
        body, name="attn_bwd", grid=(MLA_H, nq),
        in_specs=[pl.BlockSpec((1, lp, 256), lambda h, j: (h, 0, 0)), pl.BlockSpec((1, t, 256), lambda h, j: (h, j, 0)),
                  pl.BlockSpec((1, t, 128), lambda h, j: (h, j, 0)), pl.BlockSpec((lp, DV), lambda h, j: (0, h)), stat, stat]
        + [ANY] * nr,
        out_specs=[pl.BlockSpec((1, lp, 256), lambda h, j: (h, 0, 0)), pl.BlockSpec((1, t, 256), lambda h, j: (h, j, 0)),
                   pl.BlockSpec((1, t, 128), lambda h, j: (h, j, 0))] + [ANY] * nr,
        out_shape=[_sds((MLA_H, lp, 256), F32), _sds((MLA_H, lp, 256), F32), _sds((MLA_H, lp, 128), F32)]
        + [_sds((3,) + p.shape[1:], p.dtype) for p in ride],
        scratch_shapes=sems, compiler_params=_cp("arbitrary", "arbitrary"))(q, k, v, do, lse_row, delta_row, *ride)
    return outs[0], outs[1], outs[2], list(outs[3:])


def _ssd_consts():
    ri = lax.broadcasted_iota(jnp.int32, (CHUNK, CHUNK), 0)
    ci = lax.broadcasted_iota(jnp.int32, (CHUNK, CHUNK), 1)
    expand = (lax.broadcasted_iota(jnp.int32, (128, D_SSM), 0)
              == lax.broadcasted_iota(jnp.int32, (128, D_SSM), 1) // SSM_P).astype(F32)
    return ri, ci, expand


def _ssd_chunk(c, x_ref, xh_ref, dt_ref, dtT_ref, cw_ref, cb_ref, dtb_ref, dtbT_ref, al_ref, alT_ref):
    ri, ci, expand = _ssd_consts()
    x = x_ref[...]
    halo = jnp.where(c > 0, xh_ref[...], 0.0)
    sh = [x] + [_shift_down(x, halo, j) for j in range(1, SSM_K)]
    cv = cb_ref[...]
    for kk in range(SSM_K):
        cv = cv + cw_ref[kk:kk + 1, :] * sh[SSM_K - 1 - kk]
    xa = _silu(cv)
    grow = c * CHUNK + ri
    gcol = c * CHUNK + lax.broadcasted_iota(jnp.int32, (SSM_H, CHUNK), 1)
    sp = dt_ref[...] + dtb_ref[...]
    spT = dtT_ref[...] + dtbT_ref[...]
    dtc = jnp.where(grow >= PAD_ROWS, _softplus(sp), 0.0)
    dtr = jnp.where(gcol >= PAD_ROWS, _softplus(spT), 0.0)
    arow = -jnp.exp(al_ref[...])
    acolT = -jnp.exp(alT_ref[...])
    ltri = (ci <= ri).astype(F32)
    acs = _dot_hi(ltri, dtc * arow, split="b")
    acsT = _dot_hi(dtr * acolT, (ri <= ci).astype(F32))
    return dict(x=x, sh=sh, cv=cv, xa=xa, sp=sp, dtc=dtc, arow=arow, acs=acs, acsT=acsT, ri=ri, ci=ci, expand=expand,
                grow=grow)


def _ssd_mats(k, s_prev):
    xa, acs, acsT, expand, ri, ci = k["xa"], k["acs"], k["acsT"], k["expand"], k["ri"], k["ci"]
    xs = xa[:, :D_SSM]
    dt_e = _dot_hi(k["dtc"], expand)
    acs_e = _dot_hi(acs, expand)
    last_e = acs_e[CHUNK - 1:CHUNK, :]
    ea = jnp.exp(acs_e)
    f = jnp.exp(last_e - acs_e)
    cd = jnp.exp(last_e)
    xdt = xs * dt_e
    bm = [xa[:, D_SSM + g * SSM_N:D_SSM + (g + 1) * SSM_N] for g in range(SSM_G)]
    cm = [xa[:, D_SSM + (SSM_G + g) * SSM_N:D_SSM + (SSM_G + g + 1) * SSM_N] for g in range(SSM_G)]
    bmb = [b.astype(BF16) for b in bm]
    cmb = [cc.astype(BF16) for cc in cm]
    cb = [_dot_nt(cmb[g], bmb[g]) for g in range(SSM_G)]
    lam, mm = [], []
    for h in range(SSM_H):
        diff = acs[:, h:h + 1] - acsT[h:h + 1, :]
        lam_h = jnp.exp(jnp.where(ci <= ri, diff, NEG))
        lam.append(lam_h)
        mm.append(cb[h // (SSM_H // SSM_G)] * lam_h)
    lo = lax.broadcasted_iota(jnp.int32, (CHUNK, 128), 1) < SSM_P
    xdt_h = []
    for h in range(SSM_H):
        pair = xdt[:, (h // 2) * 128:(h // 2 + 1) * 128]
        xdt_h.append(jnp.where(lo if h % 2 == 0 else jnp.logical_not(lo), pair, 0.0).astype(BF16))
    ydiag = jnp.concatenate(
        [_dot(mm[2 * j].astype(BF16), xdt_h[2 * j]) + _dot(mm[2 * j + 1].astype(BF16), xdt_h[2 * j + 1])
         for j in range(SSM_H // 2)], axis=1)
    t_off = [_dot(cmb[g], s_prev[g].astype(BF16)) for g in range(SSM_G)]
    yoff = jnp.concatenate(t_off, axis=1) * ea
    return dict(xs=xs, dt_e=dt_e, acs_e=acs_e, ea=ea, f=f, cd=cd, xdt=xdt, bm=bm, cm=cm, bmb=bmb, cmb=cmb, cb=cb, lam=lam,
                mm=mm, lo=lo, xdt_h=xdt_h, ydiag=ydiag, t_off=t_off, yoff=yoff)


def _ssd_specs(nc, rev):
    ix = (lambda i: nc - 1 - i) if rev else (lambda i: i)
    return [
        pl.BlockSpec((CHUNK, D_XBC), lambda i: (ix(i), 0)),
        pl.BlockSpec((8, D_XBC), lambda i: (jnp.maximum(ix(i) * (CHUNK // 8) - 1, 0), 0)),
        pl.BlockSpec((CHUNK, D_SSM), lambda i: (ix(i), 0)),
        pl.BlockSpec((CHUNK, 128), lambda i: (ix(i), 0)),
        pl.BlockSpec((SSM_H, CHUNK), lambda i: (0, ix(i))),
        _full((8, D_XBC)), _full((1, D_XBC)), _full((1, 128)), _full((SSM_H, 1)), _full((1, 128)), _full((SSM_H, 1)),
        _full((1, D_SSM)), _full((1, D_SSM)),
    ]


def _ssd_fwd(xbc, z, dtr, dtrT, cw, cb, dtb, dtbT, alog, alogT, d_e, ng):
    lp = xbc.shape[0]
    nc = lp // CHUNK

    def body(x_ref, xh_ref, z_ref, dt_ref, dtT_ref, cw_ref, cb_ref, dtb_ref, dtbT_ref, al_ref, alT_ref, de_ref, ng_ref,
             y_ref, st_ref, s_scr):
        c = pl.program_id(0)

        @pl.when(c == 0)
        def _():
            s_scr[...] = jnp.zeros_like(s_scr)

        k = _ssd_chunk(c, x_ref, xh_ref, dt_ref, dtT_ref, cw_ref, cb_ref, dtb_ref, dtbT_ref, al_ref, alT_ref)
        s_prev = [s_scr[g] for g in range(SSM_G)]
        st_ref[0] = s_scr[...]
        m = _ssd_mats(k, s_prev)
        xd = (m["xdt"] * m["f"]).astype(BF16)
        for g in range(SSM_G):
            sl = slice(g * GSZ, (g + 1) * GSZ)
            s_scr[g] = m["cd"][:, sl] * s_prev[g] + _dot(m["bm"][g].T.astype(BF16), xd[:, sl])
        y = m["ydiag"] + m["yoff"] + de_ref[...] * m["xs"]
        u = y * _silu(z_ref[...])
        outs = []
        for g in range(SSM_G):
            ug = u[:, g * GSZ:(g + 1) * GSZ]
            outs.append(ug * lax.rsqrt(jnp.mean(ug * ug, axis=-1, keepdims=True) + EPS))
        y_ref[...] = jnp.concatenate(outs, axis=1) * ng_ref[...]

    return pl.pallas_call(
        body, name="ssd_fwd", grid=(nc,), in_specs=_ssd_specs(nc, False),
        out_specs=[_rows(CHUNK, D_SSM), pl.BlockSpec((1, SSM_G, SSM_N, GSZ), lambda i: (i, 0, 0, 0))],
        out_shape=[_sds((lp, D_SSM), F32), _sds((nc, SSM_G, SSM_N, GSZ), F32)],
        scratch_shapes=[pltpu.VMEM((SSM_G, SSM_N, GSZ), F32)],
        compiler_params=_cp("arbitrary"))(xbc, xbc, z, dtr, dtrT, cw, cb, dtb, dtbT, alog, alogT, d_e, ng)


def _ssd_bwd(dssm, xbc, z, dtr, dtrT, st, cw, cb, dtb, dtbT, alog, alogT, d_e, ng):
    lp = xbc.shape[0]
    nc = lp // CHUNK
    hpg = SSM_H // SSM_G

    def body(dy_ref, x_ref, xh_ref, z_ref, dt_ref, dtT_ref, st_ref, cw_ref, cb_ref, dtb_ref, dtbT_ref, al_ref, alT_ref,
             de_ref, ng_ref, dz_ref, dx_ref, ddt_ref, dcw_ref, dcb_ref, ddtb_ref, dal_ref, dd_ref, dng_ref, ds_scr, nx_scr):
        i = pl.program_id(0)
        c = nc - 1 - i
        first = i == 0

        @pl.when(first)
        def _():
            ds_scr[...] = jnp.zeros_like(ds_scr)
            nx_scr[...] = jnp.zeros_like(nx_scr)

        k = _ssd_chunk(c, x_ref, xh_ref, dt_ref, dtT_ref, cw_ref, cb_ref, dtb_ref, dtbT_ref, al_ref, alT_ref)
        s_prev = [st_ref[0, g] for g in range(SSM_G)]
        m = _ssd_mats(k, s_prev)
        ri, ci, expand = k["ri"], k["ci"], k["expand"]
        xs, acs, acsT = m["xs"], k["acs"], k["acsT"]
        zv = z_ref[...]
        dout = dy_ref[...]
        ngv = ng_ref[...]
        y = m["ydiag"] + m["yoff"] + de_ref[...] * xs
        sz = _silu(zv)
        u = y * sz
        du_parts, dng_parts = [], []
        for g in range(SSM_G):
            sl = slice(g * GSZ, (g + 1) * GSZ)
            dug, dngg = _rms_bwd(u[:, sl], ngv[:, sl], dout[:, sl])
            du_parts.append(dug)
            dng_parts.append(dngg)
        du = jnp.concatenate(du_parts, axis=1)
        _acc_rows(dng_ref, jnp.concatenate(dng_parts, axis=1), first)
        dy = du * sz
        dz_ref[...] = du * y * _dsilu(zv)
        dd_e = jnp.sum(dy * xs, axis=0, keepdims=True)
        _acc_rows(dd_ref, _dot_nt_hi(dd_e, expand), first)
        dxs = de_ref[...] * dy
        dacs_e = dy * m["yoff"]
        dtg = (dy * m["ea"]).astype(BF16)
        dxdt = jnp.zeros_like(xs)
        dlast_e = []
        db, dc, ds_prev = [], [], []
        xd = m["xdt"] * m["f"]
        dxd_all = []
        for g in range(SSM_G):
            sl = slice(g * GSZ, (g + 1) * GSZ)
            dsg = ds_scr[g]
            spb = s_prev[g].astype(BF16)
            dc.append(_dot_nt(dtg[:, sl], spb))
            dsp = _dot(m["cm"][g].T.astype(BF16), dtg[:, sl]) + m["cd"][:, sl] * dsg
            ds_prev.append(dsp)
            dlast_e.append(jnp.sum(dsg * s_prev[g], axis=0, keepdims=True) * m["cd"][:, sl])
            dsb = dsg.astype(BF16)
            db.append(_dot_nt(xd[:, sl].astype(BF16), dsb))
            dxd_all.append(_dot(m["bmb"][g], dsb))
        dxd = jnp.concatenate(dxd_all, axis=1)
        dxdt = dxd * m["f"]
        dff = dxd * xd
        dacs_e = dacs_e - dff
        dlast_row = jnp.concatenate(dlast_e, axis=1) + jnp.sum(dff, axis=0, keepdims=True)
        dacs = jnp.zeros((CHUNK, 128), F32)
        lane = lax.broadcasted_iota(jnp.int32, (1, 128), 1)
        cbT = [_dot_nt(m["bmb"][g], m["cmb"][g]) for g in range(SSM_G)]
        dgs = [jnp.zeros((CHUNK, CHUNK), F32) for _ in range(SSM_G)]
        dgTs = [jnp.zeros((CHUNK, CHUNK), F32) for _ in range(SSM_G)]
        dxdt_pairs = []
        for h in range(SSM_H):
            g = h // hpg
            pr = slice((h // 2) * 128, (h // 2 + 1) * 128)
            lo_h = m["lo"] if h % 2 == 0 else jnp.logical_not(m["lo"])
            dyp = jnp.where(lo_h, dy[:, pr], 0.0).astype(BF16)
            xdp = m["xdt"][:, pr].astype(BF16)
            dm = _dot_nt(dyp, xdp)
            dmT = _dot_nt(xdp, dyp)
            lamT = jnp.exp(jnp.where(ri <= ci, acsT[h:h + 1, :] - acs[:, h:h + 1], NEG))
            mT = cbT[g] * lamT
            dgs[g] = dgs[g] + dm * m["lam"][h]
            dgTs[g] = dgTs[g] + dmT * lamT
            v1 = jnp.sum(dm * m["mm"][h], axis=1, keepdims=True)
            v2 = jnp.sum(dmT * mT, axis=1, keepdims=True)
            dacs = dacs + (v1 - v2) * (lane == h).astype(F32)
            part = _dot(mT.astype(BF16), dyp)
            if h % 2 == 0:
                dxdt_pairs.append(part)
            else:
                dxdt_pairs[-1] = dxdt_pairs[-1] + part
        dxdt = dxdt + jnp.concatenate(dxdt_pairs, axis=1)
        for g in range(SSM_G):
            dc[g] = dc[g] + _dot(dgs[g].astype(BF16), m["bmb"][g])
            db[g] = db[g] + _dot(dgTs[g].astype(BF16), m["cmb"][g])
        dacs = dacs + _dot_nt_hi(dacs_e, expand)
        dlast = _dot_nt_hi(dlast_row, expand)
        dacs = dacs + jnp.where(ri == CHUNK - 1, dlast, 0.0)
        dxs = dxs + dxdt * m["dt_e"]
        ddt = _dot_nt_hi(dxdt * xs, expand)
        da = _dot_hi((ri <= ci).astype(F32), dacs, split="b")
        ddt = ddt + da * k["arow"]
        dA = jnp.sum(da * k["dtc"], axis=0, keepdims=True)
        _acc_rows(dal_ref, dA * k["arow"], first)
        ddtr = jnp.where(k["grow"] >= PAD_ROWS, ddt * _sigmoid(k["sp"]), 0.0)
        ddt_ref[...] = ddtr
        _acc_rows(ddtb_ref, jnp.sum(ddtr, axis=0, keepdims=True), first)
        for g in range(SSM_G):
            ds_scr[g] = ds_prev[g]
        dxa = jnp.concatenate([dxs] + db + dc, axis=1)
        dcv = dxa * _dsilu(k["cv"])
        _acc_rows(dcb_ref, jnp.sum(dcv, axis=0, keepdims=True), first)
        dcw_rows = [jnp.sum(dcv * k["sh"][SSM_K - 1 - kk], axis=0, keepdims=True) for kk in range(SSM_K)]
        dcw_rows.append(jnp.zeros((8 - SSM_K, D_XBC), F32))
        _acc_rows(dcw_ref, jnp.concatenate(dcw_rows, axis=0), first)
        nxt = nx_scr[...]
        dx = cw_ref[SSM_K - 1:SSM_K, :] * dcv
        for j in range(1, SSM_K):
            dx = dx + cw_ref[SSM_K - 1 - j:SSM_K - j, :] * _shift_up(dcv, nxt, j)
        grow_x = c * CHUNK + lax.broadcasted_iota(jnp.int32, (CHUNK, D_XBC), 0)
        dx_ref[...] = jnp.where(grow_x >= PAD_ROWS, dx, 0.0)
        nx_scr[...] = dcv[:8]

    specs = _ssd_specs(nc, True)
    in_specs = [pl.BlockSpec((CHUNK, D_SSM), lambda i: (nc - 1 - i, 0))] + specs[:5] + [
        pl.BlockSpec((1, SSM_G, SSM_N, GSZ), lambda i: (nc - 1 - i, 0, 0, 0))] + specs[5:]
    rv = lambda w: pl.BlockSpec((CHUNK, w), lambda i: (nc - 1 - i, 0))
    return pl.pallas_call(
        body, name="ssd_bwd", grid=(nc,), in_specs=in_specs,
        out_specs=[rv(D_SSM), rv(D_XBC), rv(128), _full((8, D_XBC)), _full((1, D_XBC)), _full((1, 128)), _full((1, 128)),
                   _full((1, 128)), _full((1, D_SSM))],
        out_shape=[_sds((lp, D_SSM), F32), _sds((lp, D_XBC), F32), _sds((lp, 128), F32), _sds((8, D_XBC), F32),
                   _sds((1, D_XBC), F32), _sds((1, 128), F32), _sds((1, 128), F32), _sds((1, 128), F32), _sds((1, D_SSM), F32)],
        scratch_shapes=[pltpu.VMEM((SSM_G, SSM_N, GSZ), F32), pltpu.VMEM((8, D_XBC), F32)],
        compiler_params=_cp("arbitrary"))(dssm, xbc, xbc, z, dtr, dtrT, st, cw, cb, dtb, dtbT, alog, alogT, d_e, ng)


def _mixout_fwd(o, ssm, h0, g_ao, g_post, w):
    lp = o.shape[0]
    tr = _rt(lp, MM_ROWS)

    def body(o_ref, s_ref, h_ref, ga_ref, gp_ref, w_ref, mi_ref, mix_ref, h1_ref):
        mixin = jnp.concatenate([_rms(o_ref[...], ga_ref[...]), s_ref[...]], axis=1).astype(BF16)
        mi_ref[...] = mixin
        mix = _dot(mixin, w_ref[...])
        mix_ref[...] = mix
        grow = pl.program_id(0) * tr + lax.broadcasted_iota(jnp.int32, (tr, D), 0)
        h1_ref[...] = h_ref[...] + jnp.where(grow >= PAD_ROWS, _rms(mix, gp_ref[...]), 0.0)

    return pl.pallas_call(
        body, name="mixout_fwd", grid=(lp // tr,),
        in_specs=[_rows(tr, D), _rows(tr, D), _rows(tr, D), _full((1, D)), _full((1, D)), _full(w.shape)],
        out_specs=[_rows(tr, 2 * D), _rows(tr, D), _rows(tr, D)],
        out_shape=[_sds((lp, 2 * D), BF16), _sds((lp, D), F32), _sds((lp, D), F32)],
        compiler_params=_cp("parallel"))(o, ssm, h0, g_ao, g_post, w)


def _ffn_up(h1, g, w):
    lp = h1.shape[0]
    tr = _rt(lp, MM_ROWS)
    tn = D_FF // 2

    def body(h_ref, g_ref, w_ref, hn_ref, u_ref):
        hn = _rms(h_ref[...], g_ref[...]).astype(BF16)
        hn_ref[...] = hn
        u_ref[...] = _dot(hn, w_ref[...])

    return pl.pallas_call(
        body, name="ffn_up", grid=(lp // tr, 2 * D_FF // tn),
        in_specs=[pl.BlockSpec((tr, D), lambda i, j: (i, 0)), _full((1, D)), pl.BlockSpec((None, D, tn), lambda i, j: (j, 0, 0))],
        out_specs=[pl.BlockSpec((tr, D), lambda i, j: (i, 0)), pl.BlockSpec((tr, tn), lambda i, j: (i, j))],
        out_shape=[_sds((lp, D), BF16), _sds((lp, 2 * D_FF), F32)],
        compiler_params=_cp("parallel", "arbitrary"))(h1, g, w)


def _ffn_in_bwd(du, w4, h1, g, dh2):
    lp = du.shape[0]
    nch, _, tn = w4.shape
    tr = _rt(lp, (320, 128))

    def body(du_ref, w_ref, h_ref, g_ref, r_ref, o_ref, dg_ref):
        acc = _dot_nt(du_ref[:, 0:tn], w_ref[0])
        for j in range(1, nch):
            acc = acc + _dot_nt(du_ref[:, j * tn:(j + 1) * tn], w_ref[j])
        dx, dg = _rms_bwd(h_ref[...], g_ref[...], acc)
        o_ref[...] = dx + r_ref[...]
        _acc_rows(dg_ref, dg, pl.program_id(0) == 0)

    return pl.pallas_call(
        body, name="ffn_in_bwd", grid=(lp // tr,),
        in_specs=[_rows(tr, nch * tn), _full(w4.shape), _rows(tr, D), _full((1, D)), _rows(tr, D)],
        out_specs=[_rows(tr, D), _full((1, D))], out_shape=[_sds((lp, D), F32), _sds((1, D), F32)],
        compiler_params=_cp("arbitrary"))(du, w4, h1, g, dh2)


FFN_CB = 256


def _ffn_gate(u, cw, cb):
    lp = u.shape[0]
    tr = _rt(lp, (320, 128))

    def body(u_ref, uh_ref, cw_ref, cb_ref, uc_ref, a_ref):
        i = pl.program_id(0)
        for j in range(D_FF // FFN_CB):
            halves = []
            for off in (0, D_FF):
                sl = slice(off + j * FFN_CB, off + (j + 1) * FFN_CB)
                x = u_ref[:, sl]
                halo = jnp.where(i > 0, uh_ref[:, sl], 0.0)
                cv = cb_ref[:, sl] + cw_ref[FFN_K - 1:FFN_K, sl] * x
                for s in range(1, FFN_K):
                    cv = cv + cw_ref[FFN_K - 1 - s:FFN_K - s, sl] * _shift_down(x, halo, s)
                uc_ref[:, sl] = cv
                halves.append(cv)
            a_ref[:, j * FFN_CB:(j + 1) * FFN_CB] = (_silu(halves[0]) * halves[1]).astype(BF16)

    return pl.pallas_call(
        body, name="ffn_gate", grid=(lp // tr,),
        in_specs=[_rows(tr, 2 * D_FF), pl.BlockSpec((8, 2 * D_FF), lambda i: (jnp.maximum(i * (tr // 8) - 1, 0), 0)),
                  _full((8, 2 * D_FF)), _full((1, 2 * D_FF))],
        out_specs=[_rows(tr, 2 * D_FF), _rows(tr, D_FF)], out_shape=[_sds((lp, 2 * D_FF), F32), _sds((lp, D_FF), BF16)],
        compiler_params=_cp("parallel"))(u, u, cw, cb)


def _ffn_down(a, w, h1, tgt, g_post):
    lp = a.shape[0]
    tr = _rt(lp, MM_ROWS)
    nb = tr // FRONT

    def body(a_ref, w_ref, h_ref, *rest):
        t_refs, (g_ref, dh2_ref, dd_ref, dg_ref, loss_ref) = rest[:nb], rest[nb:]
        i = pl.program_id(0)
        d = _dot(a_ref[...], w_ref[...])
        gv = g_ref[...]
        h2 = h_ref[...] + _rms(d, gv)
        grow = i * tr + lax.broadcasted_iota(jnp.int32, (tr, D), 0)
        tgt_v = jnp.concatenate([r[...] for r in t_refs], axis=0)
        err = jnp.where(grow >= FRONT, h2 - tgt_v, 0.0)
        dh2 = err * (1.0 / D)
        dh2_ref[...] = dh2
        dd, dg = _rms_bwd(d, gv, dh2)
        dd_ref[...] = dd.astype(BF16)
        _acc_rows(dg_ref, dg, i == 0)
        part = 0.5 * jnp.sum(jnp.sum(err * err, axis=1, keepdims=True), axis=0, keepdims=True) * (1.0 / D)
        _acc_rows(loss_ref, jnp.broadcast_to(part, (8, 128)), i == 0)

    return pl.pallas_call(
        body, name="ffn_down", grid=(lp // tr,),
        in_specs=[_rows(tr, D_FF), _full(w.shape), _rows(tr, D)]
        + [pl.BlockSpec((FRONT, D), functools.partial(lambda i, b: (jnp.maximum(i * nb - 1 + b, 0), 0), b=b)) for b in range(nb)]
        + [_full((1, D))],
        out_specs=[_rows(tr, D), _rows(tr, D), _full((1, D)), _full((8, 128))],
        out_shape=[_sds((lp, D), F32), _sds((lp, D), BF16), _sds((1, D), F32), _sds((8, 128), F32)],
        compiler_params=_cp("arbitrary"))(a, w, h1, *([tgt] * nb), g_post)


def _ffn_gate_bwd(u, uc, dd, w_down, cw):
    lp = u.shape[0]
    tr = _rt(lp, (320, 128))
    n = lp // tr

    def body(u_ref, uc_ref, dd_ref, wd_ref, cw_ref, du_ref, dcw_ref, dcb_ref, nx_scr):
        i = pl.program_id(0)
        t = n - 1 - i
        first = i == 0

        @pl.when(first)
        def _():
            nx_scr[...] = jnp.zeros_like(nx_scr)

        grow = t * tr + lax.broadcasted_iota(jnp.int32, (tr, FFN_CB), 0)
        ddv = dd_ref[...]
        for j in range(D_FF // FFN_CB):
            sls = [slice(off + j * FFN_CB, off + (j + 1) * FFN_CB) for off in (0, D_FF)]
            cvg, cvv = uc_ref[:, sls[0]], uc_ref[:, sls[1]]
            dav = _dot_nt(ddv, wd_ref[j * FFN_CB:(j + 1) * FFN_CB, :])
            dcv = (dav * cvv * _dsilu(cvg), dav * _silu(cvg))
            for hf in range(2):
                sl = sls[hf]
                g = dcv[hf]
                nxt = nx_scr[:, sl]
                ahead = [g] + [_shift_up(g, nxt, s) for s in range(1, FFN_K)]
                x = u_ref[:, sl]
                rows = [jnp.sum(x * ahead[FFN_K - 1 - kk], axis=0, keepdims=True) for kk in range(FFN_K)]
                rows.append(jnp.zeros((8 - FFN_K, FFN_CB), F32))
                upd_w = jnp.concatenate(rows, axis=0)
                upd_b = jnp.sum(g, axis=0, keepdims=True)

                @pl.when(first)
                def _():
                    dcw_ref[:, sl] = upd_w
                    dcb_ref[:, sl] = upd_b

                @pl.when(jnp.logical_not(first))
                def _():
                    dcw_ref[:, sl] += upd_w
                    dcb_ref[:, sl] += upd_b

                du = cw_ref[FFN_K - 1:FFN_K, sl] * g
                for s in range(1, FFN_K):
                    du = du + cw_ref[FFN_K - 1 - s:FFN_K - s, sl] * ahead[s]
                du_ref[:, sl] = jnp.where(grow >= PAD_ROWS, du, 0.0).astype(BF16)
                nx_scr[:, sl] = g[:8]

    wide = pl.BlockSpec((tr, 2 * D_FF), lambda i: (n - 1 - i, 0))
    return pl.pallas_call(
        body, name="ffn_gate_bwd", grid=(n,),
        in_specs=[wide, wide, pl.BlockSpec((tr, D), lambda i: (n - 1 - i, 0)), _full(w_down.shape), _full((8, 2 * D_FF))],
        out_specs=[wide, _full((8, 2 * D_FF)), _full((1, 2 * D_FF))],
        out_shape=[_sds((lp, 2 * D_FF), BF16), _sds((8, 2 * D_FF), F32), _sds((1, 2 * D_FF), F32)],
        scratch_shapes=[pltpu.VMEM((8, 2 * D_FF), F32)],
        compiler_params=_cp("arbitrary"))(u, uc, dd, w_down, cw)


def _mla_bwd(dq, dk, dv, lat, qg, kvg, wq, wkv, cos, sa, sb):
    lp = lat.shape[0]
    tr = _rt(lp, (320, 128))

    def body(dq_ref, dk_ref, dv_ref, lat_ref, qg_ref, kvg_ref, wq_ref, wkv_ref, cos_ref, sa_ref, sb_ref,
             dqf_ref, dkvf_ref, dlat_ref, dqg_ref, dkvg_ref):
        i = pl.program_id(0)
        cos_v, sa_v, sb_v = cos_ref[...], sa_ref[...], sb_ref[...]
        dkpe = jnp.zeros((tr, 128), F32)
        for h in range(MLA_H):
            dqh = dq_ref[h] * SOFTMAX_SCALE
            dqf_ref[:, h * DN:(h + 1) * DN] = dqh[:, :DN].astype(BF16)
            dqf_ref[:, D + h * 128:D + (h + 1) * 128] = _rope_t(dqh[:, DN:], cos_v, sa_v, sb_v).astype(BF16)
            dkh = dk_ref[h]
            dkvf_ref[:, h * DN:(h + 1) * DN] = dkh[:, :DN].astype(BF16)
            dkpe = dkpe + dkh[:, DN:]
            dkvf_ref[:, D + h * DV:D + (h + 1) * DV] = dv_ref[h].astype(BF16)
        dql = _dot_nt(dqf_ref[...], wq_ref[...])
        dkl = _dot_nt(dkvf_ref[...], wkv_ref[...])
        lat_v = lat_ref[...]
        dqc, dqg = _rms_bwd(lat_v[:, :QR], qg_ref[...], dql)
        dkc, dkg = _rms_bwd(lat_v[:, QR:QR + KVR], kvg_ref[...], dkl)
        dlat_ref[:, :QR] = dqc
        dlat_ref[:, QR:QR + KVR] = dkc
        dlat_ref[:, QR + KVR:] = _rope_t(dkpe, cos_v, sa_v, sb_v)
        _acc_rows(dqg_ref, dqg, i == 0)
        _acc_rows(dkvg_ref, dkg, i == 0)

    hb = lambda w: pl.BlockSpec((MLA_H, tr, w), lambda i: (0, i, 0))
    return pl.pallas_call(
        body, name="mla_bwd", grid=(lp // tr,),
        in_specs=[hb(256), hb(256), hb(128), _rows(tr, LAT_W), _full((1, QR)), _full((1, KVR)), _full(wq.shape),
                  _full(wkv.shape), _rows(tr, 128), _rows(tr, 128), _rows(tr, 128)],
        out_specs=[_rows(tr, 2 * D), _rows(tr, 2 * D), _rows(tr, LAT_W), _full((1, QR)), _full((1, KVR))],
        out_shape=[_sds((lp, 2 * D), BF16), _sds((lp, 2 * D), BF16), _sds((lp, LAT_W), F32), _sds((1, QR), F32),
                   _sds((1, KVR), F32)],
        compiler_params=_cp("arbitrary"))(dq, dk, dv, lat, qg, kvg, wq, wkv, cos, sa, sb)


def _inproj_bwd(dlat, dz, dxbc, ddt, w, h0, g, dh1, ride=()):
    lp = h0.shape[0]
    tr = _rt(lp, (320, 128))
    segs = ((0, LAT_W), (LAT_W, LAT_W + D_SSM), (LAT_W + D_SSM, LAT_W + D_SSM + D_XBC), (IN_P - 128, IN_P))
    nr = len(ride)
    steps = lp // tr

    def body(dl_ref, dz_ref, dx_ref, dt_ref, w_ref, h_ref, g_ref, r_ref, *rest):
        ps = rest[:nr]
        o_ref, dg_ref = rest[nr:nr + 2]
        got, sems = rest[nr + 2:2 * nr + 2], rest[2 * nr + 2:]
        step = pl.program_id(0)
        if nr:
            pl.when(step == 0)(lambda: _ride_exchange(ps, got, *sems, 0))
        dhn = jnp.zeros((tr, D), F32)
        for ref, (a, b) in zip((dl_ref, dz_ref, dx_ref, dt_ref), segs):
            dhn = dhn + _dot_nt(ref[...].astype(BF16), w_ref[:, a:b])
        dx, dg = _rms_bwd(h_ref[...], g_ref[...], dhn)
        o_ref[...] = dx + r_ref[...]
        _acc_rows(dg_ref, dg, step == 0)
        if nr:
            pl.when(step == steps - 1)(lambda: _ride_exchange(ps, got, *sems, 1))

    sems = [pltpu.SemaphoreType.DMA((3 * nr,)), pltpu.SemaphoreType.DMA((3 * nr,))] if nr else []
    outs = pl.pallas_call(
        body, name="inproj_bwd", grid=(steps,),
        in_specs=[_rows(tr, LAT_W), _rows(tr, D_SSM), _rows(tr, D_XBC), _rows(tr, 128), _full(w.shape), _rows(tr, D),
                  _full((1, D)), _rows(tr, D)] + [ANY] * nr,
        out_specs=[_rows(tr, D), _full((1, D))] + [ANY] * nr,
        out_shape=[_sds((lp, D), F32), _sds((1, D), F32)] + [_sds((3,) + p.shape[1:], p.dtype) for p in ride],
        scratch_shapes=sems, compiler_params=_cp("arbitrary"))(dlat, dz, dxbc, ddt, w, h0, g, dh1, *ride)
    return outs[0], outs[1], list(outs[2:])


def _rope_tables(lp):
    pos = (jnp.arange(lp, dtype=jnp.int32) - PAD_ROWS).astype(F32)
    inv = ROPE_THETA ** (-jnp.arange(0, DR, 2, dtype=F32) / DR)
    ang = pos[:, None] * inv[None, :]
    cos, sin = jnp.cos(ang), jnp.sin(ang)
    zero = jnp.zeros_like(sin)
    cos128 = jnp.concatenate([cos, cos, cos, cos], axis=1)
    sa128 = jnp.concatenate([-sin, zero, -sin, zero], axis=1)
    sb128 = jnp.concatenate([zero, sin, zero, sin], axis=1)
    return cos128, sa128, sb128


def _pad_rows8(w):
    return jnp.concatenate([w, jnp.zeros((8 - w.shape[0], w.shape[1]), w.dtype)], axis=0)


def _lane_pad(v):
    return jnp.concatenate([v, jnp.zeros((v.shape[0], 128 - v.shape[1]), v.dtype)], axis=1)


def _late_weights(bufs):
    w_out, w_up, w_down = bufs
    return dict(w_out=w_out.reshape(2 * D, D), w_up=w_up.reshape(N_CHIPS, D, 2 * D_FF // N_CHIPS), w_down=w_down.reshape(D_FF, D))


def _device_step(x, tgt, meta, p, late_bufs=(), early_reduce=None, last_reduce=None):
    s = x.shape[0]
    lp = s + FRONT
    zpad = jnp.zeros((PAD_ROWS, D), F32)
    h0 = jnp.concatenate([zpad, meta, x], axis=0)
    cos, sa, sb = _rope_tables(lp)

    w_in = p["w_in"]
    w_in_p = jnp.concatenate([w_in[:, :QR + KVR + DR], jnp.zeros((D, 64), BF16), w_in[:, QR + KVR + DR:],
                              jnp.zeros((D, 128 - SSM_H), BF16)], axis=1)
    w_uq = p["w_uq"]
    wq_p = jnp.concatenate([w_uq[:, :, :DN].reshape(QR, MLA_H * DN),
                            jnp.concatenate([w_uq[:, :, DN:], jnp.zeros((QR, MLA_H, 128 - DR), BF16)], axis=2).reshape(QR, MLA_H * 128)],
                           axis=1)
    w_ukv = p["w_ukv"]
    wkv_p = jnp.concatenate([w_ukv[:, :, :DN].reshape(KVR, MLA_H * DN), w_ukv[:, :, DN:].reshape(KVR, MLA_H * DV)], axis=1)
    scw = _pad_rows8(p["ssm_conv_w"])
    fcw = _pad_rows8(p["ffn_conv_w"])
    dtb, alog = _lane_pad(p["ssm_dt_bias"]), _lane_pad(p["ssm_A_log"])
    dtbT, alogT = p["ssm_dt_bias"].reshape(SSM_H, 1), p["ssm_A_log"].reshape(SSM_H, 1)
    d_e = jnp.repeat(p["ssm_D"], SSM_P, axis=1)

    hn, lat, z, xbc, dtr = _inproj(h0, p["norm_mix_pre"], w_in_p)
    dtrT = dtr[:, :SSM_H].T
    q, k, v, qlat, kvlat = _mla_prep(lat, p["q_a_norm"], p["kv_a_norm"], wq_p, wkv_p, cos, sa, sb)
    o, lse, gathered = _attn_fwd(q, k, v, ride=late_bufs)
    if late_bufs:
        p = dict(p, **_late_weights(gathered))
    ssm, st = _ssd_fwd(xbc, z, dtr, dtrT, scw, p["ssm_conv_b"], dtb, dtbT, alog, alogT, d_e, p["ssm_norm"])
    mixin, mix, h1 = _mixout_fwd(o, ssm, h0, p["attn_out_norm"], p["norm_mix_post"], p["w_out"])
    hn2, u = _ffn_up(h1, p["norm_ffn_pre"], p["w_up"])
    uc, a = _ffn_gate(u, fcw, p["ffn_conv_b"])
    dh2, dd, g_ffn_post, loss = _ffn_down(a, p["w_down"], h1, tgt, p["norm_ffn_post"])

    g_w_down = _mm_tn(a, dd, "ffn_dw_down", tn=512)
    du, g_fcw, g_fcb = _ffn_gate_bwd(u, uc, dd, p["w_down"], fcw)
    dh1, g_ffn_pre = _ffn_in_bwd(du, p["w_up"], h1, p["norm_ffn_pre"], dh2)
    g_w_up = _mm_tn(hn2, du, "ffn_dw_up", tn=D_FF // 2, chunked=True)
    dmix, dssm, do, g_mix_post, g_ao, delta = _mixout_bwd(mix, p["norm_mix_post"], dh1, o, p["attn_out_norm"], p["w_out"])
    g_w_out = _mm_tn(mixin, dmix, "mix_dw_out", tn=512)
    t = _rt(lp, (640, 128))
    pairs = early_reduce(dict(w_out=g_w_out, w_up=g_w_up, w_down=g_w_down)) if early_reduce else ()
    dq, dk, dv, got = _attn_bwd(q, k, v, do, lse[:, 0, :].reshape(MLA_H, lp // t, 1, t), delta.reshape(MLA_H, lp // t, 1, t),
                                ride=pairs)
    dqf, dkvf, dlat, g_qa, g_kva = _mla_bwd(dq, dk, dv, lat, p["q_a_norm"], p["kv_a_norm"], wq_p, wkv_p, cos, sa, sb)
    g_wq_p = _mm_tn(qlat, dqf, "mla_dw_uq")
    g_wkv_p = _mm_tn(kvlat, dkvf, "mla_dw_ukv")
    dz, dxbc, ddtr, g_scw, g_scb, g_dtb, g_alog, g_dd, g_ssm_norm = _ssd_bwd(
        dssm, xbc, z, dtr, dtrT, st, scw, p["ssm_conv_b"], dtb, dtbT, alog, alogT, d_e, p["ssm_norm"])
    g_in_p = jnp.concatenate([_mm_tn(hn, dlat, "in_dw_lat"), _mm_tn(hn, dz, "in_dw_z"), _mm_tn(hn, dxbc, "in_dw_xbc"),
                              _mm_tn(hn, ddtr, "in_dw_dt")], axis=1)
    g_w_in = jnp.concatenate([g_in_p[:, :QR + KVR + DR], g_in_p[:, LAT_W:LAT_W + D_SSM + D_XBC + SSM_H]], axis=1)
    g_w_uq = jnp.concatenate([g_wq_p[:, :D].reshape(QR, MLA_H, DN), g_wq_p[:, D:].reshape(QR, MLA_H, 128)[:, :, :DR]], axis=2)
    g_w_ukv = jnp.concatenate([g_wkv_p[:, :D].reshape(KVR, MLA_H, DN), g_wkv_p[:, D:].reshape(KVR, MLA_H, DV)], axis=2)
    pairs2 = last_reduce(dict(w_in=g_w_in, w_uq=g_w_uq, w_ukv=g_w_ukv)) if last_reduce else ()
    dh0, g_mix_pre, got2 = _inproj_bwd(dlat, dz, dxbc, ddtr, w_in_p, h0, p["norm_mix_pre"], dh1, ride=pairs2)
    grads = dict(
        norm_mix_pre=g_mix_pre, norm_mix_post=g_mix_post, norm_ffn_pre=g_ffn_pre, norm_ffn_post=g_ffn_post, w_in=g_w_in,
        q_a_norm=g_qa, w_uq=g_w_uq, kv_a_norm=g_kva, w_ukv=g_w_ukv, attn_out_norm=g_ao, ssm_conv_w=g_scw[:SSM_K],
        ssm_conv_b=g_scb, ssm_dt_bias=g_dtb[:, :SSM_H], ssm_A_log=g_alog[:, :SSM_H], ssm_D=g_dd[:, :SSM_H],
        ssm_norm=g_ssm_norm, w_out=g_w_out, w_up=g_w_up, ffn_conv_w=g_fcw[:FFN_K], ffn_conv_b=g_fcb, w_down=g_w_down)
    return loss, dh0[FRONT:], dh0[PAD_ROWS:FRONT], grads, (list(pairs2) + list(pairs), list(got2) + list(got))


N_CHIPS = 4
BIG = (("w_in", (D, D_IN // N_CHIPS)), ("w_uq", (QR // N_CHIPS, MLA_H, DN + DR)), ("w_ukv", (KVR // N_CHIPS, MLA_H, DN + DV)),
       ("w_out", (2 * D // N_CHIPS, D)), ("w_up", (D, 2 * D_FF // N_CHIPS)), ("w_down", (D_FF // N_CHIPS, D)))
SMALL_SHARDED = (("meta_tokens", (N_META, D // N_CHIPS)), ("ssm_conv_w", (SSM_K, D_XBC // N_CHIPS)),
                 ("ffn_conv_w", (FFN_K, 2 * D_FF // N_CHIPS)))
SMALL_REPL = (("norm_mix_pre", D), ("norm_mix_post", D), ("norm_ffn_pre", D), ("norm_ffn_post", D), ("q_a_norm", QR),
              ("kv_a_norm", KVR), ("attn_out_norm", D), ("ssm_conv_b", D_XBC), ("ssm_dt_bias", SSM_H), ("ssm_A_log", SSM_H),
              ("ssm_D", SSM_H), ("ssm_norm", D_SSM), ("ffn_conv_b", 2 * D_FF))
ANY = pl.BlockSpec(memory_space=pl.ANY)


def _pad128(v):
    n = v.shape[0]
    return jnp.concatenate([v, jnp.zeros(((-n) % 128,), v.dtype)]) if n % 128 else v


def _pack_rows(vs, rows):
    flat = jnp.concatenate([_pad128(v.reshape(-1)) for v in vs])
    flat = jnp.concatenate([flat, jnp.zeros((rows * 128 - flat.shape[0],), flat.dtype)])
    return flat.reshape(rows, 128)


def _unpack_rows(pack, sizes):
    flat = pack.reshape(-1)
    out, off = [], 0
    for n in sizes:
        out.append(flat[off:off + n])
        off += n + (-n) % 128
    return out


def _my_place():
    return lax.axis_index("x"), lax.axis_index("y"), lax.axis_index("c")


def _other_chips(x, y):
    return [(1 - x, y), (x, 1 - y), (1 - x, 1 - y)]


def _remote(src, dst, send, recv, dev):
    return pltpu.make_async_remote_copy(src_ref=src, dst_ref=dst, send_sem=send, recv_sem=recv, device_id=dev,
                                        device_id_type=MESH)


SMALL_AG_ROWS = 80


def _gather_weights(shards, small, name):
    arrs = list(shards) + ([] if small is None else [small])
    n, nb = len(arrs), len(shards)

    def body(*refs):
        ins, outs = refs[:n], refs[n:2 * n]
        send, recv, lsem = refs[2 * n:]
        x, y, c = _my_place()
        me = 2 * x + y
        chips = _other_chips(x, y)
        slot = lambda w, chip, cc: outs[w].at[chip, cc] if w < nb else outs[w].at[chip]
        mine = lambda w: slot(w, me, c) if w < nb else ins[w]
        loc = [pltpu.make_async_copy(ins[w], outs[w].at[me], lsem.at[w - nb]) for w in range(nb, n)]
        for cp in loc:
            cp.start()
        sends = []
        for w in range(n):
            for kk, (cx, cy) in enumerate(chips):
                sends.append(_remote(mine(w), slot(w, me, c), send.at[3 * w + kk], recv.at[3 * w + kk], (cx, cy, c)))
        for cp in sends:
            cp.start()
        for w in range(nb):
            for kk, (cx, cy) in enumerate(chips):
                src = 2 * cx + cy
                _remote(mine(w), slot(w, src, c), send.at[3 * w + kk], recv.at[3 * w + kk], (cx, cy, c)).wait_recv()
                fwd = _remote(slot(w, src, c), slot(w, src, c), send.at[3 * (n + w) + kk], recv.at[3 * (n + w) + kk], (x, y, 1 - c))
                fwd.start()
                sends.append(fwd)
        for w in range(n):
            for kk, (cx, cy) in enumerate(chips):
                src = 2 * cx + cy
                if w < nb:
                    _remote(mine(w), slot(w, src, 1 - c), send.at[3 * (n + w) + kk], recv.at[3 * (n + w) + kk],
                            (x, y, 1 - c)).wait_recv()
                else:
                    _remote(ins[w], slot(w, src, c), send.at[3 * w + kk], recv.at[3 * w + kk], (cx, cy, c)).wait_recv()
        for cp in sends:
            cp.wait_send()
        for cp in loc:
            cp.wait()

    return pl.pallas_call(
        body, name=name, in_specs=[ANY] * n, out_specs=[ANY] * n,
        out_shape=[_sds(a.shape, a.dtype) for a in shards] + ([] if small is None else [_sds((N_CHIPS,) + small.shape, small.dtype)]),
        input_output_aliases={w: w for w in range(nb)},
        scratch_shapes=[pltpu.SemaphoreType.DMA((3 * (n + nb),)), pltpu.SemaphoreType.DMA((3 * (n + nb),)),
                        pltpu.SemaphoreType.DMA((max(n - nb, 1),))])(*arrs)


def _place_own(wt, chip, name):
    r, c = wt.shape
    tr = _row_tile(r, c)

    def body(c_ref, w_ref, o_ref):
        o_ref[...] = w_ref[...].astype(BF16)

    return pl.pallas_call(
        body, name=name, out_shape=_sds((N_CHIPS, r, c), BF16),
        grid_spec=pltpu.PrefetchScalarGridSpec(
            num_scalar_prefetch=1, grid=(r // tr,), in_specs=[pl.BlockSpec((tr, c), lambda i, cr: (i, 0))],
            out_specs=pl.BlockSpec((None, tr, c), lambda i, cr: (cr[0], i, 0))),
        compiler_params=_cp("parallel"))(chip, wt)


def _send_sibling_halves(gs, name):
    n = len(gs)

    def body(*refs):
        ins, outs, send, recv = refs[:n], refs[n:2 * n], refs[2 * n], refs[2 * n + 1]
        x, y, c = _my_place()
        cps = [_remote(ins[w].at[:, 1 - c], outs[w], send.at[w], recv.at[w], (x, y, 1 - c)) for w in range(n)]
        for cp in cps:
            cp.start()
        for cp in cps:
            cp.wait()

    return pl.pallas_call(
        body, name=name, in_specs=[ANY] * n, out_specs=[ANY] * n,
        out_shape=[_sds((g.shape[0],) + g.shape[2:], g.dtype) for g in gs],
        scratch_shapes=[pltpu.SemaphoreType.DMA((n,)), pltpu.SemaphoreType.DMA((n,))])(*gs)


def _ride_gather(bufs, send, recv, phase):
    n = len(bufs)
    x, y, c = _my_place()
    me = 2 * x + y
    for w in range(n):
        for kk, (cx, cy) in enumerate(_other_chips(x, y)):
            src = 2 * cx + cy
            out = lambda: _remote(bufs[w].at[me, c], bufs[w].at[me, c], send.at[3 * w + kk], recv.at[3 * w + kk], (cx, cy, c))
            fwd = lambda: _remote(bufs[w].at[src, c], bufs[w].at[src, c], send.at[3 * (n + w) + kk],
                                  recv.at[3 * (n + w) + kk], (x, y, 1 - c))
            if phase == 0:
                out().start()
            elif phase == 1:
                _remote(bufs[w].at[me, c], bufs[w].at[src, c], send.at[3 * w + kk], recv.at[3 * w + kk], (cx, cy, c)).wait_recv()
                fwd().start()
            else:
                _remote(bufs[w].at[me, c], bufs[w].at[src, 1 - c], send.at[3 * (n + w) + kk], recv.at[3 * (n + w) + kk],
                        (x, y, 1 - c)).wait_recv()
                out().wait_send()
                fwd().wait_send()


def _ride_exchange(ps, outs, send, recv, phase):
    x, y, c = _my_place()
    for w in range(len(ps)):
        for kk, (cx, cy) in enumerate(_other_chips(x, y)):
            cp = _remote(ps[w].at[2 * cx + cy], outs[w].at[kk], send.at[3 * w + kk], recv.at[3 * w + kk], (cx, cy, c))
            if phase == 0:
                cp.start()
            else:
                cp.wait()


def _share_sibling(halves):
    n = len(halves)

    def body(*refs):
        outs, send, recv = refs[n:2 * n], refs[2 * n], refs[2 * n + 1]
        x, y, c = _my_place()
        cps = [_remote(outs[w].at[c], outs[w].at[c], send.at[w], recv.at[w], (x, y, 1 - c)) for w in range(n)]
        for cp in cps:
            cp.start()
        for w in range(n):
            cps[w].wait_send()
            _remote(outs[w].at[c], outs[w].at[1 - c], send.at[w], recv.at[w], (x, y, 1 - c)).wait_recv()

    return pl.pallas_call(
        body, name="share_sibling", in_specs=[ANY] * n, out_specs=[ANY] * n,
        out_shape=[_sds(h.shape, h.dtype) for h in halves], input_output_aliases={w: w for w in range(n)},
        scratch_shapes=[pltpu.SemaphoreType.DMA((n,)), pltpu.SemaphoreType.DMA((n,))])(*halves)


def _row_tile(r, c, cap=1 << 20):
    return next(t for t in range(r, 0, -1) if r % t == 0 and (t % 8 == 0 or t == r) and t * c * 4 <= cap)


def _add_pair(g, t, core, name):
    _, _, r, c = g.shape
    tr = _row_tile(r, c)

    def body(c_ref, g_ref, t_ref, o_ref):
        o_ref[...] = (g_ref[...] + t_ref[...]).astype(BF16)

    return pl.pallas_call(
        body, name=name, out_shape=_sds(t.shape, BF16),
        grid_spec=pltpu.PrefetchScalarGridSpec(
            num_scalar_prefetch=1, grid=(N_CHIPS, r // tr),
            in_specs=[pl.BlockSpec((None, None, tr, c), lambda j, i, cr: (j, cr[0], i, 0)),
                      pl.BlockSpec((None, tr, c), lambda j, i, cr: (j, i, 0))],
            out_specs=pl.BlockSpec((None, tr, c), lambda j, i, cr: (j, i, 0))),
        compiler_params=_cp("parallel", "parallel"))(core, g, t)


def _add_chips(p, got, chip, name):
    _, r, c = p.shape
    tr = _row_tile(r, c)

    def body(c_ref, p_ref, g_ref, o_ref):
        o_ref[...] = ((p_ref[...].astype(F32) + g_ref[0].astype(F32)) + g_ref[1].astype(F32)) + g_ref[2].astype(F32)

    return pl.pallas_call(
        body, name=name, out_shape=_sds((2, r, c), F32),
        grid_spec=pltpu.PrefetchScalarGridSpec(
            num_scalar_prefetch=1, grid=(r // tr,),
            in_specs=[pl.BlockSpec((None, tr, c), lambda i, cr: (cr[0], i, 0)), pl.BlockSpec((3, tr, c), lambda i, cr: (0, i, 0))],
            out_specs=pl.BlockSpec((None, tr, c), lambda i, cr: (cr[1], i, 0))),
        compiler_params=_cp("parallel"))(chip, p, got)


SMALL_AR_ROWS = 424


def _allreduce_small(v):
    def body(v_ref, o_ref, gath, send, recv):
        x, y, c = _my_place()
        me = 4 * x + 2 * y + c
        gath[me] = v_ref[...]
        cps = []
        for dd in range(1, 8):
            dx, dy, dc = dd >> 2, (dd >> 1) & 1, dd & 1
            peer = (1 - x if dx else x, 1 - y if dy else y, 1 - c if dc else c)
            cps.append(_remote(v_ref, gath.at[me], send.at[dd - 1], recv.at[dd - 1], peer))
        for cp in cps:
            cp.start()
        for cp in cps:
            cp.wait()
        acc = gath[0]
        for dev in range(1, 8):
            acc = acc + gath[dev]
        o_ref[...] = acc

    vm = pl.BlockSpec(memory_space=pltpu.VMEM)
    return pl.pallas_call(
        body, name="allreduce_small", in_specs=[vm], out_specs=vm, out_shape=_sds(v.shape, F32),
        scratch_shapes=[pltpu.VMEM((8,) + v.shape, F32), pltpu.SemaphoreType.DMA((7,)), pltpu.SemaphoreType.DMA((7,))])(v)


def _adamw(w, g, m, v, name):
    r, c = w.shape
    tr = _row_tile(r, c)

    def body(w_ref, g_ref, m_ref, v_ref, d_ref, m2_ref, v2_ref):
        gv = g_ref[...]
        m2 = ADAM_B1 * m_ref[...] + (1.0 - ADAM_B1) * gv
        v2 = ADAM_B2 * v_ref[...] + (1.0 - ADAM_B2) * jnp.square(gv)
        m_hat = m2 / (1.0 - ADAM_B1 ** ADAM_STEP)
        v_hat = v2 / (1.0 - ADAM_B2 ** ADAM_STEP)
        d_ref[...] = -ADAM_LR * (m_hat / (jnp.sqrt(v_hat) + ADAM_EPS) + ADAM_WD * w_ref[...])
        m2_ref[...] = m2
        v2_ref[...] = v2

    return pl.pallas_call(
        body, name=name, grid=(r // tr,), in_specs=[_rows(tr, c)] * 4, out_specs=[_rows(tr, c)] * 3,
        out_shape=[_sds((r, c), F32)] * 3, compiler_params=_cp("parallel"))(w, g, m, v)


WEIGHT_NAMES = ("meta_tokens", "norm_mix_pre", "norm_mix_post", "norm_ffn_pre", "norm_ffn_post", "w_in", "q_a_norm", "w_uq",
                "kv_a_norm", "w_ukv", "attn_out_norm", "ssm_conv_w", "ssm_conv_b", "ssm_dt_bias", "ssm_A_log", "ssm_D",
                "ssm_norm", "w_out", "w_up", "ffn_conv_w", "ffn_conv_b", "w_down")
SMALL_ADAM_ROWS = 192


def kernel(x, meta_tokens, norm_mix_pre, norm_mix_post, norm_ffn_pre, norm_ffn_post, w_in, q_a_norm, w_uq, kv_a_norm, w_ukv, attn_out_norm, ssm_conv_w, ssm_conv_b, ssm_dt_bias, ssm_A_log, ssm_D, ssm_norm, w_out, w_up, ffn_conv_w, ffn_conv_b, w_down, loss_target, m_meta_tokens, m_norm_mix_pre, m_norm_mix_post, m_norm_ffn_pre, m_norm_ffn_post, m_w_in, m_q_a_norm, m_w_uq, m_kv_a_norm, m_w_ukv, m_attn_out_norm, m_ssm_conv_w, m_ssm_conv_b, m_ssm_dt_bias, m_ssm_A_log, m_ssm_D, m_ssm_norm, m_w_out, m_w_up, m_ffn_conv_w, m_ffn_conv_b, m_w_down, v_meta_tokens, v_norm_mix_pre, v_norm_mix_post, v_norm_ffn_pre, v_norm_ffn_post, v_w_in, v_q_a_norm, v_w_uq, v_kv_a_norm, v_w_ukv, v_attn_out_norm, v_ssm_conv_w, v_ssm_conv_b, v_ssm_dt_bias, v_ssm_A_log, v_ssm_D, v_ssm_norm, v_w_out, v_w_up, v_ffn_conv_w, v_ffn_conv_b, v_w_down):
    args = locals()
    w = {n: args[n] for n in WEIGHT_NAMES}
    mom = {n: args["m_" + n] for n in WEIGHT_NAMES}
    var = {n: args["v_" + n] for n in WEIGHT_NAMES}
    cx, cy, cc = _my_place()
    chip = 2 * cx + cy

    two_d = {n: (shp[0], functools.reduce(lambda a, b: a * b, shp[1:])) for n, shp in BIG}
    names = [n for n, _ in BIG]
    core_i = cc.astype(jnp.int32).reshape(1)
    chip_i = chip.astype(jnp.int32).reshape(1)
    early, late = names[:3], names[3:]
    halves = lambda n, a: a.reshape(N_CHIPS, 2, two_d[n][0] // 2, two_d[n][1])
    bufs = {n: halves(n, _place_own(w[n].reshape(two_d[n]), chip_i, "place_" + n)) for n in names}
    small = _pack_rows([w[n] for n, _ in SMALL_SHARDED], SMALL_AG_ROWS)
    *gathered, small_all = _gather_weights([bufs[n] for n in early], small, "allgather_weights")
    gath = {n: a.reshape((N_CHIPS,) + two_d[n]) for n, a in zip(early, gathered)}
    p = dict(w_in=gath["w_in"].transpose(1, 0, 2).reshape(D, D_IN), w_uq=gath["w_uq"].reshape(QR, MLA_H, DN + DR),
             w_ukv=gath["w_ukv"].reshape(KVR, MLA_H, DN + DV))
    sm_parts = [_unpack_rows(small_all[j], [a * b for _, (a, b) in SMALL_SHARDED]) for j in range(N_CHIPS)]
    for i, (n, shp) in enumerate(SMALL_SHARDED):
        p[n] = jnp.concatenate([sm_parts[j][i].reshape(shp) for j in range(N_CHIPS)], axis=1)
    for n, _ in SMALL_REPL:
        p[n] = w[n]
    meta_full = p.pop("meta_tokens")

    place_i = jnp.stack([chip, cc]).astype(jnp.int32)

    def pair_sums(gd, group):
        gd = dict(gd)
        if "w_in" in gd:
            gd["w_in"] = gd["w_in"].reshape(D, N_CHIPS, D_IN // N_CHIPS).transpose(1, 0, 2)
        gs = [halves(n, gd[n]) for n in group]
        from_sib = _send_sibling_halves(gs, "reduce_sibling_" + group[0])
        return [_add_pair(gg, tt, core_i, "reduce_pair_" + n) for n, gg, tt in zip(group, gs, from_sib)]

    loss_part, gx, gmeta, g, (pairs, got) = _device_step(
        x[0], loss_target[0], meta_full, p, late_bufs=[bufs[n] for n in late], early_reduce=lambda gd: pair_sums(gd, late),
        last_reduce=lambda gd: pair_sums(gd, early))

    small_names = [n for n, _ in SMALL_REPL] + ["ssm_conv_w", "ffn_conv_w"]
    small_sizes = [128] + [sz for _, sz in SMALL_REPL] + [N_META * D, SSM_K * D_XBC, FFN_K * 2 * D_FF]
    order = [n for n, _ in SMALL_REPL]
    sp = _pack_rows([loss_part[0]] + [g[n] for n in order] + [gmeta, g["ssm_conv_w"], g["ffn_conv_w"]], SMALL_AR_ROWS)
    red = _unpack_rows(_allreduce_small(sp), small_sizes)
    loss = red[0][0]
    gfull = {n: red[1 + i].reshape(1, -1) for i, n in enumerate(order)}
    n_r = len(order)
    gfull["meta_tokens"] = lax.dynamic_slice_in_dim(red[1 + n_r].reshape(N_META, D), chip * (D // N_CHIPS), D // N_CHIPS, axis=1)
    gfull["ssm_conv_w"] = lax.dynamic_slice_in_dim(red[2 + n_r].reshape(SSM_K, D_XBC), chip * (D_XBC // N_CHIPS),
                                                   D_XBC // N_CHIPS, axis=1)[None]
    gfull["ffn_conv_w"] = lax.dynamic_slice_in_dim(red[3 + n_r].reshape(FFN_K, 2 * D_FF), chip * (2 * D_FF // N_CHIPS),
                                                   2 * D_FF // N_CHIPS, axis=1)[None]

    mine = [_add_chips(pp, gg, place_i, "reduce_chips_" + n) for n, pp, gg in zip(names, pairs, got)]
    for n, both in zip(names, _share_sibling(mine)):
        gfull[n] = both.reshape(two_d[n])

    delta, new_m, new_v = {}, {}, {}
    for n, shp in BIG:
        outs = _adamw(w[n].reshape(two_d[n]), gfull[n], mom[n].reshape(two_d[n]), var[n].reshape(two_d[n]), "adamw_" + n)
        delta[n], new_m[n], new_v[n] = (o.reshape((1,) + shp) for o in outs)
    snames = order + ["meta_tokens", "ssm_conv_w", "ffn_conv_w"]
    ssizes = [functools.reduce(lambda a, b: a * b, w[n].shape) for n in snames]
    packs = [_pack_rows([d[n] for n in snames], SMALL_ADAM_ROWS) for d in (w, gfull, mom, var)]
    outs = _adamw(*packs, "adamw_small")
    for d, o in zip((delta, new_m, new_v), outs):
        for n, piece in zip(snames, _unpack_rows(o, ssizes)):
            d[n] = piece.reshape(w[n].shape)
    gout = {n: gfull[n].reshape(w[n].shape) for n in WEIGHT_NAMES}
    return (loss, gx[None], *[gout[n] for n in WEIGHT_NAMES], *[delta[n] for n in WEIGHT_NAMES],
            *[new_m[n] for n in WEIGHT_NAMES], *[new_v[n] for n in WEIGHT_NAMES])
```

```python
import functools

import jax
import jax.numpy as jnp
from jax import lax
from jax.experimental import pallas as pl
from jax.experimental.pallas import tpu as pltpu

F32 = jnp.float32
BF16 = jnp.bfloat16

D = 1024
N_META = 16
FRONT = 128
PAD_ROWS = FRONT - N_META
MLA_H = 8
DN, DR, DV = 128, 64, 128
QR, KVR = 384, 256
SOFTMAX_SCALE = (DN + DR) ** -0.5
ROPE_THETA = 10000.0
SSM_H, SSM_P, SSM_G, SSM_N, SSM_K = 16, 64, 2, 128, 4
CHUNK = 128
D_SSM = SSM_H * SSM_P
D_XBC = D_SSM + 2 * SSM_G * SSM_N
GSZ = D_SSM // SSM_G
D_FF = 2816
FFN_K = 3
EPS = 1e-6
IN_SPLITS = (QR, KVR, DR, D_SSM, D_XBC, SSM_H)
D_IN = sum(IN_SPLITS)
LAT_W = 768
IN_P = LAT_W + D_SSM + D_XBC + 128
NEG = -1e30
LOG2E = 1.4426950408889634
LN2 = 0.6931471805599453
Q_SCALE = SOFTMAX_SCALE * LOG2E

ADAM_LR, ADAM_B1, ADAM_B2, ADAM_EPS, ADAM_WD, ADAM_STEP = 0.001, 0.9, 0.999, 1e-08, 0.01, 10

VMEM_LIMIT = 56 * 1024 * 1024
MM_ROWS = (640, 320, 128)
MESH = pl.DeviceIdType.MESH


def _sds(shape, dtype):
    return jax.ShapeDtypeStruct(shape, dtype)


def _cp(*sem):
    return pltpu.CompilerParams(dimension_semantics=sem, vmem_limit_bytes=VMEM_LIMIT)


def _rt(n, cands):
    for c in cands:
        if n % c == 0:
            return c
    raise ValueError((n, cands))


def _full(shape):
    nd = len(shape)
    return pl.BlockSpec(shape, lambda *_: (0,) * nd)


def _rows(tr, c):
    return pl.BlockSpec((tr, c), lambda i: (i, 0))


def _sigmoid(x):
    return 1.0 / (1.0 + jnp.exp(-x))


def _silu(x):
    return x * _sigmoid(x)


def _dsilu(x):
    s = _sigmoid(x)
    return s * (1.0 + x * (1.0 - s))


def _softplus(x):
    return jnp.maximum(x, 0.0) + jnp.log(1.0 + jnp.exp(-jnp.abs(x)))


def _rms(x, g):
    r = lax.rsqrt(jnp.mean(x * x, axis=-1, keepdims=True) + EPS)
    return x * r * g


def _rms_bwd(x, g, dy):
    r = lax.rsqrt(jnp.mean(x * x, axis=-1, keepdims=True) + EPS)
    xh = x * r
    dxh = dy * g
    dx = r * (dxh - xh * jnp.mean(dxh * xh, axis=-1, keepdims=True))
    return dx, jnp.sum(dy * xh, axis=0, keepdims=True)


def _dot(a, b):
    return jnp.dot(a, b, preferred_element_type=F32)


def _dot_nt(a, b):
    return lax.dot_general(a, b, (((1,), (1,)), ((), ())), preferred_element_type=F32)


def _dot_tn(a, b):
    return lax.dot_general(a, b, (((0,), (0,)), ((), ())), preferred_element_type=F32)


def _split3(x):
    hi = x.astype(BF16)
    r = x - hi.astype(F32)
    mid = r.astype(BF16)
    return hi, mid, (r - mid.astype(F32)).astype(BF16)


def _dot_hi(a, b, split="a"):
    if split == "a":
        bb = b.astype(BF16)
        return sum(_dot(t, bb) for t in _split3(a))
    ab = a.astype(BF16)
    return sum(_dot(ab, t) for t in _split3(b))


def _dot_nt_hi(a, b):
    bb = b.astype(BF16)
    return sum(_dot_nt(t, bb) for t in _split3(a))


def _shift_down(x, halo, j):
    xr = pltpu.roll(x, j, axis=0)
    hr = pltpu.roll(halo, j, axis=0)
    row = lax.broadcasted_iota(jnp.int32, (8, x.shape[1]), 0)
    first = jnp.where(row < j, hr, xr[:8])
    return jnp.concatenate([first, xr[8:]], axis=0)


def _shift_up(x, nxt, j):
    t = x.shape[0]
    xr = pltpu.roll(x, t - j, axis=0)
    nr = pltpu.roll(nxt, 8 - j, axis=0)
    row = lax.broadcasted_iota(jnp.int32, (8, x.shape[1]), 0)
    last = jnp.where(row + j >= 8, nr, xr[t - 8:])
    return jnp.concatenate([xr[:t - 8], last], axis=0)


def _acc_rows(ref, val, first):
    @pl.when(first)
    def _():
        ref[...] = val

    @pl.when(jnp.logical_not(first))
    def _():
        ref[...] += val


def _mm_tn(a, b, name, tn=None, trs=(1664, 640, 128), chunked=False):
    r, m = a.shape
    n = b.shape[1]
    tn = n if tn is None else tn
    tr = _rt(r, trs)

    def body(a_ref, b_ref, o_ref):
        part = _dot_tn(a_ref[...].astype(BF16), b_ref[...].astype(BF16))
        _acc_rows(o_ref, part, pl.program_id(1) == 0)

    if chunked:
        out_specs, out_shape = pl.BlockSpec((None, m, tn), lambda j, i: (j, 0, 0)), _sds((n // tn, m, tn), F32)
    else:
        out_specs, out_shape = pl.BlockSpec((m, tn), lambda j, i: (0, j)), _sds((m, n), F32)
    return pl.pallas_call(
        body, name=name, grid=(n // tn, r // tr),
        in_specs=[pl.BlockSpec((tr, m), lambda j, i: (i, 0)), pl.BlockSpec((tr, tn), lambda j, i: (i, j))],
        out_specs=out_specs, out_shape=out_shape, compiler_params=_cp("parallel", "arbitrary"))(a, b)


def _inproj(h0, g, w):
    lp = h0.shape[0]
    tr = _rt(lp, MM_ROWS)
    segs = ((0, LAT_W), (LAT_W, LAT_W + D_SSM), (LAT_W + D_SSM, LAT_W + D_SSM + D_XBC), (IN_P - 128, IN_P))

    def body(h_ref, g_ref, w_ref, hn_ref, lat_ref, z_ref, xbc_ref, dt_ref):
        hn = _rms(h_ref[...], g_ref[...]).astype(BF16)
        hn_ref[...] = hn
        for ref, (a, b) in zip((lat_ref, z_ref, xbc_ref, dt_ref), segs):
            ref[...] = _dot(hn, w_ref[:, a:b])

    return pl.pallas_call(
        body, name="inproj", grid=(lp // tr,), in_specs=[_rows(tr, D), _full((1, D)), _full(w.shape)],
        out_specs=[_rows(tr, D), _rows(tr, LAT_W), _rows(tr, D_SSM), _rows(tr, D_XBC), _rows(tr, 128)],
        out_shape=[_sds((lp, D), BF16), _sds((lp, LAT_W), F32), _sds((lp, D_SSM), F32), _sds((lp, D_XBC), F32),
                   _sds((lp, 128), F32)],
        compiler_params=_cp("parallel"))(h0, g, w)


def _rope(x, cos, sa, sb):
    return x * cos + pltpu.roll(x, 96, axis=1) * sa + pltpu.roll(x, 32, axis=1) * sb


def _rope_t(g, cos, sa, sb):
    return g * cos + pltpu.roll(g * sa, 32, axis=1) + pltpu.roll(g * sb, 96, axis=1)


def _mla_prep(lat, qg, kvg, wq, wkv, cos, sa, sb):
    lp = lat.shape[0]
    tr = _rt(lp, MM_ROWS)

    def body(lat_ref, qg_ref, kvg_ref, wq_ref, wkv_ref, cos_ref, sa_ref, sb_ref, q_ref, k_ref, v_ref, ql_ref, kl_ref):
        lat_v = lat_ref[...]
        ql = _rms(lat_v[:, :QR], qg_ref[...]).astype(BF16)
        kl = _rms(lat_v[:, QR:QR + KVR], kvg_ref[...]).astype(BF16)
        ql_ref[...] = ql
        kl_ref[...] = kl
        cos_v, sa_v, sb_v = cos_ref[...], sa_ref[...], sb_ref[...]
        kpe = _rope(lat_v[:, QR + KVR:LAT_W], cos_v, sa_v, sb_v).astype(BF16)
        for h in range(MLA_H):
            q_ref[h, :, 0:DN] = (_dot(ql, wq_ref[:, h * DN:(h + 1) * DN]) * Q_SCALE).astype(BF16)
            qpe = _dot(ql, wq_ref[:, D + h * 128:D + (h + 1) * 128])
            q_ref[h, :, DN:2 * DN] = (_rope(qpe, cos_v, sa_v, sb_v) * Q_SCALE).astype(BF16)
            k_ref[h, :, 0:DN] = _dot(kl, wkv_ref[:, h * DN:(h + 1) * DN]).astype(BF16)
            k_ref[h, :, DN:2 * DN] = kpe
            v_ref[h] = _dot(kl, wkv_ref[:, D + h * DV:D + (h + 1) * DV]).astype(BF16)

    hb = lambda w: pl.BlockSpec((MLA_H, tr, w), lambda i: (0, i, 0))
    return pl.pallas_call(
        body, name="mla_prep", grid=(lp // tr,),
        in_specs=[_rows(tr, LAT_W), _full((1, QR)), _full((1, KVR)), _full(wq.shape), _full(wkv.shape),
                  _rows(tr, 128), _rows(tr, 128), _rows(tr, 128)],
        out_specs=[hb(256), hb(256), hb(128), _rows(tr, QR), _rows(tr, KVR)],
        out_shape=[_sds((MLA_H, lp, 256), BF16), _sds((MLA_H, lp, 256), BF16), _sds((MLA_H, lp, 128), BF16),
                   _sds((lp, QR), BF16), _sds((lp, KVR), BF16)],
        compiler_params=_cp("parallel"))(lat, qg, kvg, wq, wkv, cos, sa, sb)


def _attn_mask(r0, c0, tq, tk, transposed=False):
    if transposed:
        kk = c0 + lax.broadcasted_iota(jnp.int32, (tk, tq), 0)
        qq = r0 + lax.broadcasted_iota(jnp.int32, (tk, tq), 1)
    else:
        qq = r0 + lax.broadcasted_iota(jnp.int32, (tq, tk), 0)
        kk = c0 + lax.broadcasted_iota(jnp.int32, (tq, tk), 1)
    return jnp.logical_and(kk <= qq, kk >= PAD_ROWS)


def _attn_fwd(q, k, v, ride=()):
    lp = q.shape[1]
    t = _rt(lp, (640, 128))
    nq = lp // t

    hp = 2

    nr = len(ride)
    steps = (MLA_H // hp) * nq

    def body(q_ref, k_ref, v_ref, *rest):
        o_ref, lse_ref = rest[nr:nr + 2]
        bufs, sems = rest[nr + 2:2 * nr + 2], rest[2 * nr + 2:]
        qi = pl.program_id(1)
        step = pl.program_id(0) * nq + qi
        if nr:
            pl.when(step == 0)(lambda: _ride_gather(bufs, *sems, 0))
            pl.when(step == steps // 2)(lambda: _ride_gather(bufs, *sems, 1))
        qv = [q_ref[a] for a in range(hp)]

        def tile(kj, carries, masked, live=None):
            kv_rows = pl.ds(pl.multiple_of(kj * t, t), t)
            out = []
            for a in range(hp):
                m, l, acc = carries[a]
                kk = k_ref[a, kv_rows, :]
                vv = v_ref[a, kv_rows, :]
                s = _dot_nt(qv[a], kk)
                if masked:
                    keep = _attn_mask(qi * t, kj * t, t, t)
                    if live is not None:
                        keep = jnp.logical_and(keep, live)
                    s = jnp.where(keep, s, NEG)
                m_new = jnp.maximum(m, jnp.max(s, axis=-1, keepdims=True))
                alpha = jnp.exp2(m - m_new)
                p = jnp.exp2(s - m_new)
                l = alpha * l + jnp.sum(p, axis=-1, keepdims=True)
                acc = alpha * acc + _dot(p.astype(BF16), vv)
                out.append((m_new, l, acc))
            return tuple(out)

        init = tuple((jnp.full((t, 1), NEG, F32), jnp.zeros((t, 1), F32), jnp.zeros((t, DV), F32)) for _ in range(hp))
        carries = tile(0, init, True)
        carries = lax.fori_loop(1, qi, lambda kj, c: tile(kj, c, False), carries)
        carries = tile(qi, carries, True, live=qi > 0)
        for a in range(hp):
            m, l, acc = carries[a]
            o_ref[:, a * DV:(a + 1) * DV] = acc / l
            lse_ref[a] = jnp.broadcast_to(m + jnp.log(l) * LOG2E, (t, 128)).T[:8]
        if nr:
            pl.when(step == steps - 1)(lambda: _ride_gather(bufs, *sems, 2))

    sems = [pltpu.SemaphoreType.DMA((6 * nr,)), pltpu.SemaphoreType.DMA((6 * nr,))] if nr else []
    outs = pl.pallas_call(
        body, name="attn_fwd", grid=(MLA_H // hp, nq),
        in_specs=[pl.BlockSpec((hp, t, 256), lambda h, i: (h, i, 0)), pl.BlockSpec((hp, lp, 256), lambda h, i: (h, 0, 0)),
                  pl.BlockSpec((hp, lp, 128), lambda h, i: (h, 0, 0))] + [ANY] * nr,
        out_specs=[pl.BlockSpec((t, hp * DV), lambda h, i: (i, h)), pl.BlockSpec((hp, 8, t), lambda h, i: (h, 0, i))] + [ANY] * nr,
        out_shape=[_sds((lp, MLA_H * DV), F32), _sds((MLA_H, 8, lp), F32)] + [_sds(b.shape, b.dtype) for b in ride],
        input_output_aliases={3 + w: 2 + w for w in range(nr)}, scratch_shapes=sems,
        compiler_params=_cp("arbitrary", "arbitrary"))(q, k, v, *ride)
    return outs[0], outs[1], list(outs[2:])


def _mixout_bwd(mix, g_post, dh1, o, g_ao, w_out):
    lp = o.shape[0]
    tr = _rt(lp, MM_ROWS)

    def body(mix_ref, gp_ref, dh_ref, o_ref, g_ref, w_ref, dmix_ref, dssm_ref, do_ref, dgp_ref, dg_ref, dl_ref):
        i = pl.program_id(0)
        grow = i * tr + lax.broadcasted_iota(jnp.int32, (tr, D), 0)
        dmix, dgp = _rms_bwd(mix_ref[...], gp_ref[...], jnp.where(grow >= PAD_ROWS, dh_ref[...], 0.0))
        dmix = dmix.astype(BF16)
        dmix_ref[...] = dmix
        _acc_rows(dgp_ref, dgp, i == 0)
        dssm_ref[...] = _dot_nt(dmix, w_ref[D:, :])
        ov = o_ref[...]
        do, dg = _rms_bwd(ov, g_ref[...], _dot_nt(dmix, w_ref[:D, :]))
        do_ref[...] = do
        _acc_rows(dg_ref, dg, i == 0)
        prod = do * ov
        lane = lax.broadcasted_iota(jnp.int32, (1, 128), 1)
        cols = jnp.zeros((tr, 128), F32)
        for h in range(MLA_H):
            cols = cols + jnp.sum(prod[:, h * DV:(h + 1) * DV], axis=-1, keepdims=True) * (lane == h).astype(F32)
        dl_ref[...] = cols.T[:MLA_H]

    return pl.pallas_call(
        body, name="mixout_bwd", grid=(lp // tr,),
        in_specs=[_rows(tr, D), _full((1, D)), _rows(tr, D), _rows(tr, D), _full((1, D)), _full(w_out.shape)],
        out_specs=[_rows(tr, D), _rows(tr, D), _rows(tr, D), _full((1, D)), _full((1, D)), pl.BlockSpec((MLA_H, tr), lambda i: (0, i))],
        out_shape=[_sds((lp, D), BF16), _sds((lp, D), F32), _sds((lp, D), F32), _sds((1, D), F32), _sds((1, D), F32),
                   _sds((MLA_H, lp), F32)],
        compiler_params=_cp("arbitrary"))(mix, g_post, dh1, o, g_ao, w_out)


def _attn_bwd(q, k, v, do, lse_row, delta_row, ride=()):
    lp = q.shape[1]
    t = _rt(lp, (640, 128))
    nq = lp // t

    nr = len(ride)

    def body(q_ref, k_ref, v_ref, do_ref, lse_ref, dl_ref, *rest):
        ps = rest[:nr]
        dq_ref, dk_ref, dv_ref = rest[nr:nr + 3]
        got, sems = rest[nr + 3:2 * nr + 3], rest[2 * nr + 3:]
        kj = pl.program_id(1)
        step = pl.program_id(0) * nq + kj
        if nr:
            pl.when(step == 0)(lambda: _ride_exchange(ps, got, *sems, 0))
        kk = k_ref[0]
        vv = v_ref[0]

        @pl.when(kj == 0)
        def _():
            dq_ref[...] = jnp.zeros_like(dq_ref)

        def tile(qi, carry, masked):
            dk, dv = carry
            q_rows = pl.ds(pl.multiple_of(qi * t, t), t)
            qv = q_ref[0, q_rows, :]
            dob = do_ref[q_rows, :].astype(BF16)
            st = _dot_nt(kk, qv)
            if masked:
                st = jnp.where(_attn_mask(qi * t, kj * t, t, t, transposed=True), st, NEG)
            pt = jnp.exp2(st - lse_ref[0, qi])
            dpt = _dot_nt(vv, dob)
            dst = (pt * (dpt - dl_ref[0, qi])).astype(BF16)
            dv = dv + _dot(pt.astype(BF16), dob)
            dk = dk + _dot(dst, qv)
            dq_ref[0, q_rows, :] += _dot_tn(dst, kk)
            return dk, dv

        carry = tile(kj, (jnp.zeros((t, 256), F32), jnp.zeros((t, DV), F32)), True)
        split = jnp.where(kj == 0, nq, kj + 1)
        carry = lax.fori_loop(kj + 1, split, lambda qi, c: tile(qi, c, True), carry)
        dk, dv = lax.fori_loop(split, nq, lambda qi, c: tile(qi, c, False), carry)
        dk_ref[0] = dk * LN2
        dv_ref[0] = dv
        if nr:
            pl.when(step == MLA_H * nq - 1)(lambda: _ride_exchange(ps, got, *sems, 1))

    stat = pl.BlockSpec((1, nq, 1, t), lambda h, j: (h, 0, 0, 0))
    sems = [pltpu.SemaphoreType.DMA((3 * nr,)), pltpu.SemaphoreType.DMA((3 * nr,))] if nr else []
    outs = pl.pallas_call(
        body, name="attn_bwd", grid=(MLA_H, nq),
        in_specs=[pl.BlockSpec((1, lp, 256), lambda h, j: (h, 0, 0)), pl.BlockSpec((1, t, 256), lambda h, j: (h, j, 0)),
                  pl.BlockSpec((1, t, 128), lambda h, j: (h, j, 0)), pl.BlockSpec((lp, DV), lambda h, j: (0, h)), stat, stat]
        + [ANY] * nr,
        out_specs=[pl.BlockSpec((1, lp, 256), lambda h, j: (h, 0, 0)), pl.BlockSpec((1, t, 256), lambda h, j: (h, j, 0)),
                   pl.BlockSpec((1, t, 128), lambda h, j: (h, j, 0))] + [ANY] * nr,
        out_shape=[_sds((MLA_H, lp, 256), F32), _sds((MLA_H, lp, 256), F32), _sds((MLA_H, lp, 128), F32)]
        + [_sds((3,) + p.shape[1:], p.dtype) for p in ride],
        scratch_shapes=sems, compiler_params=_cp("arbitrary", "arbitrary"))(q, k, v, do, lse_row, delta_row, *ride)
    return outs[0], outs[1], outs[2], list(outs[3:])


def _ssd_consts():
    ri = lax.broadcasted_iota(jnp.int32, (CHUNK, CHUNK), 0)
    ci = lax.broadcasted_iota(jnp.int32, (CHUNK, CHUNK), 1)
    expand = (lax.broadcasted_iota(jnp.int32, (128, D_SSM), 0)
              == lax.broadcasted_iota(jnp.int32, (128, D_SSM), 1) // SSM_P).astype(F32)
    return ri, ci, expand


def _ssd_chunk(c, x_ref, xh_ref, dt_ref, dtT_ref, cw_ref, cb_ref, dtb_ref, dtbT_ref, al_ref, alT_ref):
    ri, ci, expand = _ssd_consts()
    x = x_ref[...]
    halo = jnp.where(c > 0, xh_ref[...], 0.0)
    sh = [x] + [_shift_down(x, halo, j) for j in range(1, SSM_K)]
    cv = cb_ref[...]
    for kk in range(SSM_K):
        cv = cv + cw_ref[kk:kk + 1, :] * sh[SSM_K - 1 - kk]
    xa = _silu(cv)
    grow = c * CHUNK + ri
    gcol = c * CHUNK + lax.broadcasted_iota(jnp.int32, (SSM_H, CHUNK), 1)
    sp = dt_ref[...] + dtb_ref[...]
    spT = dtT_ref[...] + dtbT_ref[...]
    dtc = jnp.where(grow >= PAD_ROWS, _softplus(sp), 0.0)
    dtr = jnp.where(gcol >= PAD_ROWS, _softplus(spT), 0.0)
    arow = -jnp.exp(al_ref[...])
    acolT = -jnp.exp(alT_ref[...])
    ltri = (ci <= ri).astype(F32)
    acs = _dot_hi(ltri, dtc * arow, split="b")
    acsT = _dot_hi(dtr * acolT, (ri <= ci).astype(F32))
    return dict(x=x, sh=sh, cv=cv, xa=xa, sp=sp, dtc=dtc, arow=arow, acs=acs, acsT=acsT, ri=ri, ci=ci, expand=expand,
                grow=grow)


def _ssd_mats(k, s_prev):
    xa, acs, acsT, expand, ri, ci = k["xa"], k["acs"], k["acsT"], k["expand"], k["ri"], k["ci"]
    xs = xa[:, :D_SSM]
    dt_e = _dot_hi(k["dtc"], expand)
    acs_e = _dot_hi(acs, expand)
    last_e = acs_e[CHUNK - 1:CHUNK, :]
    ea = jnp.exp(acs_e)
    f = jnp.exp(last_e - acs_e)
    cd = jnp.exp(last_e)
    xdt = xs * dt_e
    bm = [xa[:, D_SSM + g * SSM_N:D_SSM + (g + 1) * SSM_N] for g in range(SSM_G)]
    cm = [xa[:, D_SSM + (SSM_G + g) * SSM_N:D_SSM + (SSM_G + g + 1) * SSM_N] for g in range(SSM_G)]
    bmb = [b.astype(BF16) for b in bm]
    cmb = [cc.astype(BF16) for cc in cm]
    cb = [_dot_nt(cmb[g], bmb[g]) for g in range(SSM_G)]
    lam, mm = [], []
    for h in range(SSM_H):
        diff = acs[:, h:h + 1] - acsT[h:h + 1, :]
        lam_h = jnp.exp(jnp.where(ci <= ri, diff, NEG))
        lam.append(lam_h)
        mm.append(cb[h // (SSM_H // SSM_G)] * lam_h)
    lo = lax.broadcasted_iota(jnp.int32, (CHUNK, 128), 1) < SSM_P
    xdt_h = []
    for h in range(SSM_H):
        pair = xdt[:, (h // 2) * 128:(h // 2 + 1) * 128]
        xdt_h.append(jnp.where(lo if h % 2 == 0 else jnp.logical_not(lo), pair, 0.0).astype(BF16))
    ydiag = jnp.concatenate(
        [_dot(mm[2 * j].astype(BF16), xdt_h[2 * j]) + _dot(mm[2 * j + 1].astype(BF16), xdt_h[2 * j + 1])
         for j in range(SSM_H // 2)], axis=1)
    t_off = [_dot(cmb[g], s_prev[g].astype(BF16)) for g in range(SSM_G)]
    yoff = jnp.concatenate(t_off, axis=1) * ea
    return dict(xs=xs, dt_e=dt_e, acs_e=acs_e, ea=ea, f=f, cd=cd, xdt=xdt, bm=bm, cm=cm, bmb=bmb, cmb=cmb, cb=cb, lam=lam,
                mm=mm, lo=lo, xdt_h=xdt_h, ydiag=ydiag, t_off=t_off, yoff=yoff)


def _ssd_specs(nc, rev):
    ix = (lambda i: nc - 1 - i) if rev else (lambda i: i)
    return [
        pl.BlockSpec((CHUNK, D_XBC), lambda i: (ix(i), 0)),
        pl.BlockSpec((8, D_XBC), lambda i: (jnp.maximum(ix(i) * (CHUNK // 8) - 1, 0), 0)),
        pl.BlockSpec((CHUNK, D_SSM), lambda i: (ix(i), 0)),
        pl.BlockSpec((CHUNK, 128), lambda i: (ix(i), 0)),
        pl.BlockSpec((SSM_H, CHUNK), lambda i: (0, ix(i))),
        _full((8, D_XBC)), _full((1, D_XBC)), _full((1, 128)), _full((SSM_H, 1)), _full((1, 128)), _full((SSM_H, 1)),
        _full((1, D_SSM)), _full((1, D_SSM)),
    ]


def _ssd_fwd(xbc, z, dtr, dtrT, cw, cb, dtb, dtbT, alog, alogT, d_e, ng):
    lp = xbc.shape[0]
    nc = lp // CHUNK

    def body(x_ref, xh_ref, z_ref, dt_ref, dtT_ref, cw_ref, cb_ref, dtb_ref, dtbT_ref, al_ref, alT_ref, de_ref, ng_ref,
             y_ref, st_ref, s_scr):
        c = pl.program_id(0)

        @pl.when(c == 0)
        def _():
            s_scr[...] = jnp.zeros_like(s_scr)

        k = _ssd_chunk(c, x_ref, xh_ref, dt_ref, dtT_ref, cw_ref, cb_ref, dtb_ref, dtbT_ref, al_ref, alT_ref)
        s_prev = [s_scr[g] for g in range(SSM_G)]
        st_ref[0] = s_scr[...]
        m = _ssd_mats(k, s_prev)
        xd = (m["xdt"] * m["f"]).astype(BF16)
        for g in range(SSM_G):
            sl = slice(g * GSZ, (g + 1) * GSZ)
            s_scr[g] = m["cd"][:, sl] * s_prev[g] + _dot(m["bm"][g].T.astype(BF16), xd[:, sl])
        y = m["ydiag"] + m["yoff"] + de_ref[...] * m["xs"]
        u = y * _silu(z_ref[...])
        outs = []
        for g in range(SSM_G):
            ug = u[:, g * GSZ:(g + 1) * GSZ]
            outs.append(ug * lax.rsqrt(jnp.mean(ug * ug, axis=-1, keepdims=True) + EPS))
        y_ref[...] = jnp.concatenate(outs, axis=1) * ng_ref[...]

    return pl.pallas_call(
        body, name="ssd_fwd", grid=(nc,), in_specs=_ssd_specs(nc, False),
        out_specs=[_rows(CHUNK, D_SSM), pl.BlockSpec((1, SSM_G, SSM_N, GSZ), lambda i: (i, 0, 0, 0))],
        out_shape=[_sds((lp, D_SSM), F32), _sds((nc, SSM_G, SSM_N, GSZ), F32)],
        scratch_shapes=[pltpu.VMEM((SSM_G, SSM_N, GSZ), F32)],
        compiler_params=_cp("arbitrary"))(xbc, xbc, z, dtr, dtrT, cw, cb, dtb, dtbT, alog, alogT, d_e, ng)


def _ssd_bwd(dssm, xbc, z, dtr, dtrT, st, cw, cb, dtb, dtbT, alog, alogT, d_e, ng):
    lp = xbc.shape[0]
    nc = lp // CHUNK
    hpg = SSM_H // SSM_G

    def body(dy_ref, x_ref, xh_ref, z_ref, dt_ref, dtT_ref, st_ref, cw_ref, cb_ref, dtb_ref, dtbT_ref, al_ref, alT_ref,
             de_ref, ng_ref, dz_ref, dx_ref, ddt_ref, dcw_ref, dcb_ref, ddtb_ref, dal_ref, dd_ref, dng_ref, ds_scr, nx_scr):
        i = pl.program_id(0)
        c = nc - 1 - i
        first = i == 0

        @pl.when(first)
        def _():
            ds_scr[...] = jnp.zeros_like(ds_scr)
            nx_scr[...] = jnp.zeros_like(nx_scr)

        k = _ssd_chunk(c, x_ref, xh_ref, dt_ref, dtT_ref, cw_ref, cb_ref, dtb_ref, dtbT_ref, al_ref, alT_ref)
        s_prev = [st_ref[0, g] for g in range(SSM_G)]
        m = _ssd_mats(k, s_prev)
        ri, ci, expand = k["ri"], k["ci"], k["expand"]
        xs, acs, acsT = m["xs"], k["acs"], k["acsT"]
        zv = z_ref[...]
        dout = dy_ref[...]
        ngv = ng_ref[...]
        y = m["ydiag"] + m["yoff"] + de_ref[...] * xs
        sz = _silu(zv)
        u = y * sz
        du_parts, dng_parts = [], []
        for g in range(SSM_G):
            sl = slice(g * GSZ, (g + 1) * GSZ)
            dug, dngg = _rms_bwd(u[:, sl], ngv[:, sl], dout[:, sl])
            du_parts.append(dug)
            dng_parts.append(dngg)
        du = jnp.concatenate(du_parts, axis=1)
        _acc_rows(dng_ref, jnp.concatenate(dng_parts, axis=1), first)
        dy = du * sz
        dz_ref[...] = du * y * _dsilu(zv)
        dd_e = jnp.sum(dy * xs, axis=0, keepdims=True)
        _acc_rows(dd_ref, _dot_nt_hi(dd_e, expand), first)
        dxs = de_ref[...] * dy
        dacs_e = dy * m["yoff"]
        dtg = (dy * m["ea"]).astype(BF16)
        dxdt = jnp.zeros_like(xs)
        dlast_e = []
        db, dc, ds_prev = [], [], []
        xd = m["xdt"] * m["f"]
        dxd_all = []
        for g in range(SSM_G):
            sl = slice(g * GSZ, (g + 1) * GSZ)
            dsg = ds_scr[g]
            spb = s_prev[g].astype(BF16)
            dc.append(_dot_nt(dtg[:, sl], spb))
            dsp = _dot(m["cm"][g].T.astype(BF16), dtg[:, sl]) + m["cd"][:, sl] * dsg
            ds_prev.append(dsp)
            dlast_e.append(jnp.sum(dsg * s_prev[g], axis=0, keepdims=True) * m["cd"][:, sl])
            dsb = dsg.astype(BF16)
            db.append(_dot_nt(xd[:, sl].astype(BF16), dsb))
            dxd_all.append(_dot(m["bmb"][g], dsb))
        dxd = jnp.concatenate(dxd_all, axis=1)
        dxdt = dxd * m["f"]
        dff = dxd * xd
        dacs_e = dacs_e - dff
        dlast_row = jnp.concatenate(dlast_e, axis=1) + jnp.sum(dff, axis=0, keepdims=True)
        dacs = jnp.zeros((CHUNK, 128), F32)
        lane = lax.broadcasted_iota(jnp.int32, (1, 128), 1)
        cbT = [_dot_nt(m["bmb"][g], m["cmb"][g]) for g in range(SSM_G)]
        dgs = [jnp.zeros((CHUNK, CHUNK), F32) for _ in range(SSM_G)]
        dgTs = [jnp.zeros((CHUNK, CHUNK), F32) for _ in range(SSM_G)]
        dxdt_pairs = []
        for h in range(SSM_H):
            g = h // hpg
            pr = slice((h // 2) * 128, (h // 2 + 1) * 128)
            lo_h = m["lo"] if h % 2 == 0 else jnp.logical_not(m["lo"])
            dyp = jnp.where(lo_h, dy[:, pr], 0.0).astype(BF16)
            xdp = m["xdt"][:, pr].astype(BF16)
            dm = _dot_nt(dyp, xdp)
            dmT = _dot_nt(xdp, dyp)
            lamT = jnp.exp(jnp.where(ri <= ci, acsT[h:h + 1, :] - acs[:, h:h + 1], NEG))
            mT = cbT[g] * lamT
            dgs[g] = dgs[g] + dm * m["lam"][h]
            dgTs[g] = dgTs[g] + dmT * lamT
            v1 = jnp.sum(dm * m["mm"][h], axis=1, keepdims=True)
            v2 = jnp.sum(dmT * mT, axis=1, keepdims=True)
            dacs = dacs + (v1 - v2) * (lane == h).astype(F32)
            part = _dot(mT.astype(BF16), dyp)
            if h % 2 == 0:
                dxdt_pairs.append(part)
            else:
                dxdt_pairs[-1] = dxdt_pairs[-1] + part
        dxdt = dxdt + jnp.concatenate(dxdt_pairs, axis=1)
        for g in range(SSM_G):
            dc[g] = dc[g] + _dot(dgs[g].astype(BF16), m["bmb"][g])
            db[g] = db[g] + _dot(dgTs[g].astype(BF16), m["cmb"][g])
        dacs = dacs + _dot_nt_hi(dacs_e, expand)
        dlast = _dot_nt_hi(dlast_row, expand)
        dacs = dacs + jnp.where(ri == CHUNK - 1, dlast, 0.0)
        dxs = dxs + dxdt * m["dt_e"]
        ddt = _dot_nt_hi(dxdt * xs, expand)
        da = _dot_hi((ri <= ci).astype(F32), dacs, split="b")
        ddt = ddt + da * k["arow"]
        dA = jnp.sum(da * k["dtc"], axis=0, keepdims=True)
        _acc_rows(dal_ref, dA * k["arow"], first)
        ddtr = jnp.where(k["grow"] >= PAD_ROWS, ddt * _sigmoid(k["sp"]), 0.0)
        ddt_ref[...] = ddtr
        _acc_rows(ddtb_ref, jnp.sum(ddtr, axis=0, keepdims=True), first)
        for g in range(SSM_G):
            ds_scr[g] = ds_prev[g]
        dxa = jnp.concatenate([dxs] + db + dc, axis=1)
        dcv = dxa * _dsilu(k["cv"])
        _acc_rows(dcb_ref, jnp.sum(dcv, axis=0, keepdims=True), first)
        dcw_rows = [jnp.sum(dcv * k["sh"][SSM_K - 1 - kk], axis=0, keepdims=True) for kk in range(SSM_K)]
        dcw_rows.append(jnp.zeros((8 - SSM_K, D_XBC), F32))
        _acc_rows(dcw_ref, jnp.concatenate(dcw_rows, axis=0), first)
        nxt = nx_scr[...]
        dx = cw_ref[SSM_K - 1:SSM_K, :] * dcv
        for j in range(1, SSM_K):
            dx = dx + cw_ref[SSM_K - 1 - j:SSM_K - j, :] * _shift_up(dcv, nxt, j)
        grow_x = c * CHUNK + lax.broadcasted_iota(jnp.int32, (CHUNK, D_XBC), 0)
        dx_ref[...] = jnp.where(grow_x >= PAD_ROWS, dx, 0.0)
        nx_scr[...] = dcv[:8]

    specs = _ssd_specs(nc, True)
    in_specs = [pl.BlockSpec((CHUNK, D_SSM), lambda i: (nc - 1 - i, 0))] + specs[:5] + [
        pl.BlockSpec((1, SSM_G, SSM_N, GSZ), lambda i: (nc - 1 - i, 0, 0, 0))] + specs[5:]
    rv = lambda w: pl.BlockSpec((CHUNK, w), lambda i: (nc - 1 - i, 0))
    return pl.pallas_call(
        body, name="ssd_bwd", grid=(nc,), in_specs=in_specs,
        out_specs=[rv(D_SSM), rv(D_XBC), rv(128), _full((8, D_XBC)), _full((1, D_XBC)), _full((1, 128)), _full((1, 128)),
                   _full((1, 128)), _full((1, D_SSM))],
        out_shape=[_sds((lp, D_SSM), F32), _sds((lp, D_XBC), F32), _sds((lp, 128), F32), _sds((8, D_XBC), F32),
                   _sds((1, D_XBC), F32), _sds((1, 128), F32), _sds((1, 128), F32), _sds((1, 128), F32), _sds((1, D_SSM), F32)],
        scratch_shapes=[pltpu.VMEM((SSM_G, SSM_N, GSZ), F32), pltpu.VMEM((8, D_XBC), F32)],
        compiler_params=_cp("arbitrary"))(dssm, xbc, xbc, z, dtr, dtrT, st, cw, cb, dtb, dtbT, alog, alogT, d_e, ng)


def _mixout_fwd(o, ssm, h0, g_ao, g_post, w):
    lp = o.shape[0]
    tr = _rt(lp, MM_ROWS)

    def body(o_ref, s_ref, h_ref, ga_ref, gp_ref, w_ref, mi_ref, mix_ref, h1_ref):
        mixin = jnp.concatenate([_rms(o_ref[...], ga_ref[...]), s_ref[...]], axis=1).astype(BF16)
        mi_ref[...] = mixin
        mix = _dot(mixin, w_ref[...])
        mix_ref[...] = mix
        grow = pl.program_id(0) * tr + lax.broadcasted_iota(jnp.int32, (tr, D), 0)
        h1_ref[...] = h_ref[...] + jnp.where(grow >= PAD_ROWS, _rms(mix, gp_ref[...]), 0.0)

    return pl.pallas_call(
        body, name="mixout_fwd", grid=(lp // tr,),
        in_specs=[_rows(tr, D), _rows(tr, D), _rows(tr, D), _full((1, D)), _full((1, D)), _full(w.shape)],
        out_specs=[_rows(tr, 2 * D), _rows(tr, D), _rows(tr, D)],
        out_shape=[_sds((lp, 2 * D), BF16), _sds((lp, D), F32), _sds((lp, D), F32)],
        compiler_params=_cp("parallel"))(o, ssm, h0, g_ao, g_post, w)


def _ffn_up(h1, g, w):
    lp = h1.shape[0]
    tr = _rt(lp, MM_ROWS)
    tn = D_FF // 2

    def body(h_ref, g_ref, w_ref, hn_ref, u_ref):
        hn = _rms(h_ref[...], g_ref[...]).astype(BF16)
        hn_ref[...] = hn
        u_ref[...] = _dot(hn, w_ref[...]).astype(BF16)

    return pl.pallas_call(
        body, name="ffn_up", grid=(lp // tr, 2 * D_FF // tn),
        in_specs=[pl.BlockSpec((tr, D), lambda i, j: (i, 0)), _full((1, D)), pl.BlockSpec((None, D, tn), lambda i, j: (j, 0, 0))],
        out_specs=[pl.BlockSpec((tr, D), lambda i, j: (i, 0)), pl.BlockSpec((tr, tn), lambda i, j: (i, j))],
        out_shape=[_sds((lp, D), BF16), _sds((lp, 2 * D_FF), BF16)],
        compiler_params=_cp("parallel", "arbitrary"))(h1, g, w)


def _ffn_in_bwd(du, w4, h1, g, dh2):
    lp = du.shape[0]
    nch, _, tn = w4.shape
    tr = _rt(lp, (320, 128))

    def body(du_ref, w_ref, h_ref, g_ref, r_ref, o_ref, dg_ref):
        acc = _dot_nt(du_ref[:, 0:tn], w_ref[0])
        for j in range(1, nch):
            acc = acc + _dot_nt(du_ref[:, j * tn:(j + 1) * tn], w_ref[j])
        dx, dg = _rms_bwd(h_ref[...], g_ref[...], acc)
        o_ref[...] = dx + r_ref[...]
        _acc_rows(dg_ref, dg, pl.program_id(0) == 0)

    return pl.pallas_call(
        body, name="ffn_in_bwd", grid=(lp // tr,),
        in_specs=[_rows(tr, nch * tn), _full(w4.shape), _rows(tr, D), _full((1, D)), _rows(tr, D)],
        out_specs=[_rows(tr, D), _full((1, D))], out_shape=[_sds((lp, D), F32), _sds((1, D), F32)],
        compiler_params=_cp("arbitrary"))(du, w4, h1, g, dh2)


FFN_CB = 256


def _ffn_gate(u, cw, cb):
    lp = u.shape[0]
    tr = _rt(lp, (320, 128))

    def body(u_ref, uh_ref, cw_ref, cb_ref, uc_ref, a_ref):
        i = pl.program_id(0)
        for j in range(D_FF // FFN_CB):
            halves = []
            for off in (0, D_FF):
                sl = slice(off + j * FFN_CB, off + (j + 1) * FFN_CB)
                x = u_ref[:, sl].astype(F32)
                halo = jnp.where(i > 0, uh_ref[8:16, sl].astype(F32), 0.0)
                cv = cb_ref[:, sl] + cw_ref[FFN_K - 1:FFN_K, sl] * x
                for s in range(1, FFN_K):
                    cv = cv + cw_ref[FFN_K - 1 - s:FFN_K - s, sl] * _shift_down(x, halo, s)
                uc_ref[:, sl] = cv.astype(BF16)
                halves.append(cv)
            a_ref[:, j * FFN_CB:(j + 1) * FFN_CB] = (_silu(halves[0]) * halves[1]).astype(BF16)

    return pl.pallas_call(
        body, name="ffn_gate", grid=(lp // tr,),
        in_specs=[_rows(tr, 2 * D_FF), pl.BlockSpec((16, 2 * D_FF), lambda i: (jnp.maximum(i * (tr // 16) - 1, 0), 0)),
                  _full((8, 2 * D_FF)), _full((1, 2 * D_FF))],
        out_specs=[_rows(tr, 2 * D_FF), _rows(tr, D_FF)], out_shape=[_sds((lp, 2 * D_FF), BF16), _sds((lp, D_FF), BF16)],
        compiler_params=_cp("parallel"))(u, u, cw, cb)


def _ffn_down(a, w, h1, tgt, g_post):
    lp = a.shape[0]
    tr = _rt(lp, MM_ROWS)
    nb = tr // FRONT

    def body(a_ref, w_ref, h_ref, *rest):
        t_refs, (g_ref, dh2_ref, dd_ref, dg_ref, loss_ref) = rest[:nb], rest[nb:]
        i = pl.program_id(0)
        d = _dot(a_ref[...], w_ref[...])
        gv = g_ref[...]
        h2 = h_ref[...] + _rms(d, gv)
        grow = i * tr + lax.broadcasted_iota(jnp.int32, (tr, D), 0)
        tgt_v = jnp.concatenate([r[...] for r in t_refs], axis=0)
        err = jnp.where(grow >= FRONT, h2 - tgt_v, 0.0)
        dh2 = err * (1.0 / D)
        dh2_ref[...] = dh2
        dd, dg = _rms_bwd(d, gv, dh2)
        dd_ref[...] = dd.astype(BF16)
        _acc_rows(dg_ref, dg, i == 0)
        part = 0.5 * jnp.sum(jnp.sum(err * err, axis=1, keepdims=True), axis=0, keepdims=True) * (1.0 / D)
        _acc_rows(loss_ref, jnp.broadcast_to(part, (8, 128)), i == 0)

    return pl.pallas_call(
        body, name="ffn_down", grid=(lp // tr,),
        in_specs=[_rows(tr, D_FF), _full(w.shape), _rows(tr, D)]
        + [pl.BlockSpec((FRONT, D), functools.partial(lambda i, b: (jnp.maximum(i * nb - 1 + b, 0), 0), b=b)) for b in range(nb)]
        + [_full((1, D))],
        out_specs=[_rows(tr, D), _rows(tr, D), _full((1, D)), _full((8, 128))],
        out_shape=[_sds((lp, D), F32), _sds((lp, D), BF16), _sds((1, D), F32), _sds((8, 128), F32)],
        compiler_params=_cp("arbitrary"))(a, w, h1, *([tgt] * nb), g_post)


def _ffn_gate_bwd(u, uc, dd, w_down, cw):
    lp = u.shape[0]
    tr = _rt(lp, (320, 128))
    n = lp // tr

    def body(u_ref, uc_ref, dd_ref, wd_ref, cw_ref, du_ref, dcw_ref, dcb_ref, nx_scr):
        i = pl.program_id(0)
        t = n - 1 - i
        first = i == 0

        @pl.when(first)
        def _():
            nx_scr[...] = jnp.zeros_like(nx_scr)

        grow = t * tr + lax.broadcasted_iota(jnp.int32, (tr, FFN_CB), 0)
        ddv = dd_ref[...]
        for j in range(D_FF // FFN_CB):
            sls = [slice(off + j * FFN_CB, off + (j + 1) * FFN_CB) for off in (0, D_FF)]
            cvg, cvv = uc_ref[:, sls[0]].astype(F32), uc_ref[:, sls[1]].astype(F32)
            dav = _dot_nt(ddv, wd_ref[j * FFN_CB:(j + 1) * FFN_CB, :])
            dcv = (dav * cvv * _dsilu(cvg), dav * _silu(cvg))
            for hf in range(2):
                sl = sls[hf]
                g = dcv[hf]
                nxt = nx_scr[:, sl]
                ahead = [g] + [_shift_up(g, nxt, s) for s in range(1, FFN_K)]
                x = u_ref[:, sl].astype(F32)
                rows = [jnp.sum(x * ahead[FFN_K - 1 - kk], axis=0, keepdims=True) for kk in range(FFN_K)]
                rows.append(jnp.zeros((8 - FFN_K, FFN_CB), F32))
                upd_w = jnp.concatenate(rows, axis=0)
                upd_b = jnp.sum(g, axis=0, keepdims=True)

                @pl.when(first)
                def _():
                    dcw_ref[:, sl] = upd_w
                    dcb_ref[:, sl] = upd_b

                @pl.when(jnp.logical_not(first))
                def _():
                    dcw_ref[:, sl] += upd_w
                    dcb_ref[:, sl] += upd_b

                du = cw_ref[FFN_K - 1:FFN_K, sl] * g
                for s in range(1, FFN_K):
                    du = du + cw_ref[FFN_K - 1 - s:FFN_K - s, sl] * ahead[s]
                du_ref[:, sl] = jnp.where(grow >= PAD_ROWS, du, 0.0).astype(BF16)
                nx_scr[:, sl] = g[:8]

    wide = pl.BlockSpec((tr, 2 * D_FF), lambda i: (n - 1 - i, 0))
    return pl.pallas_call(
        body, name="ffn_gate_bwd", grid=(n,),
        in_specs=[wide, wide, pl.BlockSpec((tr, D), lambda i: (n - 1 - i, 0)), _full(w_down.shape), _full((8, 2 * D_FF))],
        out_specs=[wide, _full((8, 2 * D_FF)), _full((1, 2 * D_FF))],
        out_shape=[_sds((lp, 2 * D_FF), BF16), _sds((8, 2 * D_FF), F32), _sds((1, 2 * D_FF), F32)],
        scratch_shapes=[pltpu.VMEM((8, 2 * D_FF), F32)],
        compiler_params=_cp("arbitrary"))(u, uc, dd, w_down, cw)


def _mla_bwd(dq, dk, dv, lat, qg, kvg, wq, wkv, cos, sa, sb):
    lp = lat.shape[0]
    tr = _rt(lp, (320, 128))

    def body(dq_ref, dk_ref, dv_ref, lat_ref, qg_ref, kvg_ref, wq_ref, wkv_ref, cos_ref, sa_ref, sb_ref,
             dqf_ref, dkvf_ref, dlat_ref, dqg_ref, dkvg_ref):
        i = pl.program_id(0)
        cos_v, sa_v, sb_v = cos_ref[...], sa_ref[...], sb_ref[...]
        dkpe = jnp.zeros((tr, 128), F32)
        for h in range(MLA_H):
            dqh = dq_ref[h] * SOFTMAX_SCALE
            dqf_ref[:, h * DN:(h + 1) * DN] = dqh[:, :DN].astype(BF16)
            dqf_ref[:, D + h * 128:D + (h + 1) * 128] = _rope_t(dqh[:, DN:], cos_v, sa_v, sb_v).astype(BF16)
            dkh = dk_ref[h]
            dkvf_ref[:, h * DN:(h + 1) * DN] = dkh[:, :DN].astype(BF16)
            dkpe = dkpe + dkh[:, DN:]
            dkvf_ref[:, D + h * DV:D + (h + 1) * DV] = dv_ref[h].astype(BF16)
        dql = _dot_nt(dqf_ref[...], wq_ref[...])
        dkl = _dot_nt(dkvf_ref[...], wkv_ref[...])
        lat_v = lat_ref[...]
        dqc, dqg = _rms_bwd(lat_v[:, :QR], qg_ref[...], dql)
        dkc, dkg = _rms_bwd(lat_v[:, QR:QR + KVR], kvg_ref[...], dkl)
        dlat_ref[:, :QR] = dqc
        dlat_ref[:, QR:QR + KVR] = dkc
        dlat_ref[:, QR + KVR:] = _rope_t(dkpe, cos_v, sa_v, sb_v)
        _acc_rows(dqg_ref, dqg, i == 0)
        _acc_rows(dkvg_ref, dkg, i == 0)

    hb = lambda w: pl.BlockSpec((MLA_H, tr, w), lambda i: (0, i, 0))
    return pl.pallas_call(
        body, name="mla_bwd", grid=(lp // tr,),
        in_specs=[hb(256), hb(256), hb(128), _rows(tr, LAT_W), _full((1, QR)), _full((1, KVR)), _full(wq.shape),
                  _full(wkv.shape), _rows(tr, 128), _rows(tr, 128), _rows(tr, 128)],
        out_specs=[_rows(tr, 2 * D), _rows(tr, 2 * D), _rows(tr, LAT_W), _full((1, QR)), _full((1, KVR))],
        out_shape=[_sds((lp, 2 * D), BF16), _sds((lp, 2 * D), BF16), _sds((lp, LAT_W), F32), _sds((1, QR), F32),
                   _sds((1, KVR), F32)],
        compiler_params=_cp("arbitrary"))(dq, dk, dv, lat, qg, kvg, wq, wkv, cos, sa, sb)


def _inproj_bwd(dlat, dz, dxbc, ddt, w, h0, g, dh1, ride=()):
    lp = h0.shape[0]
    tr = _rt(lp, (320, 128))
    segs = ((0, LAT_W), (LAT_W, LAT_W + D_SSM), (LAT_W + D_SSM, LAT_W + D_SSM + D_XBC), (IN_P - 128, IN_P))
    nr = len(ride)
    steps = lp // tr

    def body(dl_ref, dz_ref, dx_ref, dt_ref, w_ref, h_ref, g_ref, r_ref, *rest):
        ps = rest[:nr]
        o_ref, dg_ref = rest[nr:nr + 2]
        got, sems = rest[nr + 2:2 * nr + 2], rest[2 * nr + 2:]
        step = pl.program_id(0)
        if nr:
            pl.when(step == 0)(lambda: _ride_exchange(ps, got, *sems, 0))
        dhn = jnp.zeros((tr, D), F32)
        for ref, (a, b) in zip((dl_ref, dz_ref, dx_ref, dt_ref), segs):
            dhn = dhn + _dot_nt(ref[...].astype(BF16), w_ref[:, a:b])
        dx, dg = _rms_bwd(h_ref[...], g_ref[...], dhn)
        o_ref[...] = dx + r_ref[...]
        _acc_rows(dg_ref, dg, step == 0)
        if nr:
            pl.when(step == steps - 1)(lambda: _ride_exchange(ps, got, *sems, 1))

    sems = [pltpu.SemaphoreType.DMA((3 * nr,)), pltpu.SemaphoreType.DMA((3 * nr,))] if nr else []
    outs = pl.pallas_call(
        body, name="inproj_bwd", grid=(steps,),
        in_specs=[_rows(tr, LAT_W), _rows(tr, D_SSM), _rows(tr, D_XBC), _rows(tr, 128), _full(w.shape), _rows(tr, D),
                  _full((1, D)), _rows(tr, D)] + [ANY] * nr,
        out_specs=[_rows(tr, D), _full((1, D))] + [ANY] * nr,
        out_shape=[_sds((lp, D), F32), _sds((1, D), F32)] + [_sds((3,) + p.shape[1:], p.dtype) for p in ride],
        scratch_shapes=sems, compiler_params=_cp("arbitrary"))(dlat, dz, dxbc, ddt, w, h0, g, dh1, *ride)
    return outs[0], outs[1], list(outs[2:])


def _rope_tables(lp):
    pos = (jnp.arange(lp, dtype=jnp.int32) - PAD_ROWS).astype(F32)
    inv = ROPE_THETA ** (-jnp.arange(0, DR, 2, dtype=F32) / DR)
    ang = pos[:, None] * inv[None, :]
    cos, sin = jnp.cos(ang), jnp.sin(ang)
    zero = jnp.zeros_like(sin)
    cos128 = jnp.concatenate([cos, cos, cos, cos], axis=1)
    sa128 = jnp.concatenate([-sin, zero, -sin, zero], axis=1)
    sb128 = jnp.concatenate([zero, sin, zero, sin], axis=1)
    return cos128, sa128, sb128


def _pad_rows8(w):
    return jnp.concatenate([w, jnp.zeros((8 - w.shape[0], w.shape[1]), w.dtype)], axis=0)


def _lane_pad(v):
    return jnp.concatenate([v, jnp.zeros((v.shape[0], 128 - v.shape[1]), v.dtype)], axis=1)


def _late_weights(bufs):
    w_out, w_up, w_down = bufs
    return dict(w_out=w_out.reshape(2 * D, D), w_up=w_up.reshape(N_CHIPS, D, 2 * D_FF // N_CHIPS), w_down=w_down.reshape(D_FF, D))


def _device_step(x, tgt, meta, p, late_bufs=(), early_reduce=None, last_reduce=None):
    s = x.shape[0]
    lp = s + FRONT
    zpad = jnp.zeros((PAD_ROWS, D), F32)
    h0 = jnp.concatenate([zpad, meta, x], axis=0)
    cos, sa, sb = _rope_tables(lp)

    w_in = p["w_in"]
    w_in_p = jnp.concatenate([w_in[:, :QR + KVR + DR], jnp.zeros((D, 64), BF16), w_in[:, QR + KVR + DR:],
                              jnp.zeros((D, 128 - SSM_H), BF16)], axis=1)
    w_uq = p["w_uq"]
    wq_p = jnp.concatenate([w_uq[:, :, :DN].reshape(QR, MLA_H * DN),
                            jnp.concatenate([w_uq[:, :, DN:], jnp.zeros((QR, MLA_H, 128 - DR), BF16)], axis=2).reshape(QR, MLA_H * 128)],
                           axis=1)
    w_ukv = p["w_ukv"]
    wkv_p = jnp.concatenate([w_ukv[:, :, :DN].reshape(KVR, MLA_H * DN), w_ukv[:, :, DN:].reshape(KVR, MLA_H * DV)], axis=1)
    scw = _pad_rows8(p["ssm_conv_w"])
    fcw = _pad_rows8(p["ffn_conv_w"])
    dtb, alog = _lane_pad(p["ssm_dt_bias"]), _lane_pad(p["ssm_A_log"])
    dtbT, alogT = p["ssm_dt_bias"].reshape(SSM_H, 1), p["ssm_A_log"].reshape(SSM_H, 1)
    d_e = jnp.repeat(p["ssm_D"], SSM_P, axis=1)

    hn, lat, z, xbc, dtr = _inproj(h0, p["norm_mix_pre"], w_in_p)
    dtrT = dtr[:, :SSM_H].T
    q, k, v, qlat, kvlat = _mla_prep(lat, p["q_a_norm"], p["kv_a_norm"], wq_p, wkv_p, cos, sa, sb)
    o, lse, gathered = _attn_fwd(q, k, v, ride=late_bufs)
    if late_bufs:
        p = dict(p, **_late_weights(gathered))
    ssm, st = _ssd_fwd(xbc, z, dtr, dtrT, scw, p["ssm_conv_b"], dtb, dtbT, alog, alogT, d_e, p["ssm_norm"])
    mixin, mix, h1 = _mixout_fwd(o, ssm, h0, p["attn_out_norm"], p["norm_mix_post"], p["w_out"])
    hn2, u = _ffn_up(h1, p["norm_ffn_pre"], p["w_up"])
    uc, a = _ffn_gate(u, fcw, p["ffn_conv_b"])
    dh2, dd, g_ffn_post, loss = _ffn_down(a, p["w_down"], h1, tgt, p["norm_ffn_post"])

    g_w_down = _mm_tn(a, dd, "ffn_dw_down", tn=512)
    du, g_fcw, g_fcb = _ffn_gate_bwd(u, uc, dd, p["w_down"], fcw)
    dh1, g_ffn_pre = _ffn_in_bwd(du, p["w_up"], h1, p["norm_ffn_pre"], dh2)
    g_w_up = _mm_tn(hn2, du, "ffn_dw_up", tn=D_FF // 2, chunked=True)
    dmix, dssm, do, g_mix_post, g_ao, delta = _mixout_bwd(mix, p["norm_mix_post"], dh1, o, p["attn_out_norm"], p["w_out"])
    g_w_out = _mm_tn(mixin, dmix, "mix_dw_out", tn=512)
    t = _rt(lp, (640, 128))
    pairs = early_reduce(dict(w_out=g_w_out, w_up=g_w_up, w_down=g_w_down)) if early_reduce else ()
    dq, dk, dv, got = _attn_bwd(q, k, v, do, lse[:, 0, :].reshape(MLA_H, lp // t, 1, t), delta.reshape(MLA_H, lp // t, 1, t),
                                ride=pairs)
    dqf, dkvf, dlat, g_qa, g_kva = _mla_bwd(dq, dk, dv, lat, p["q_a_norm"], p["kv_a_norm"], wq_p, wkv_p, cos, sa, sb)
    g_wq_p = _mm_tn(qlat, dqf, "mla_dw_uq")
    g_wkv_p = _mm_tn(kvlat, dkvf, "mla_dw_ukv")
    dz, dxbc, ddtr, g_scw, g_scb, g_dtb, g_alog, g_dd, g_ssm_norm = _ssd_bwd(
        dssm, xbc, z, dtr, dtrT, st, scw, p["ssm_conv_b"], dtb, dtbT, alog, alogT, d_e, p["ssm_norm"])
    g_in_p = jnp.concatenate([_mm_tn(hn, dlat, "in_dw_lat"), _mm_tn(hn, dz, "in_dw_z"), _mm_tn(hn, dxbc, "in_dw_xbc"),
                              _mm_tn(hn, ddtr, "in_dw_dt")], axis=1)
    g_w_in = jnp.concatenate([g_in_p[:, :QR + KVR + DR], g_in_p[:, LAT_W:LAT_W + D_SSM + D_XBC + SSM_H]], axis=1)
    g_w_uq = jnp.concatenate([g_wq_p[:, :D].reshape(QR, MLA_H, DN), g_wq_p[:, D:].reshape(QR, MLA_H, 128)[:, :, :DR]], axis=2)
    g_w_ukv = jnp.concatenate([g_wkv_p[:, :D].reshape(KVR, MLA_H, DN), g_wkv_p[:, D:].reshape(KVR, MLA_H, DV)], axis=2)
    pairs2 = last_reduce(dict(w_in=g_w_in, w_uq=g_w_uq, w_ukv=g_w_ukv)) if last_reduce else ()
    dh0, g_mix_pre, got2 = _inproj_bwd(dlat, dz, dxbc, ddtr, w_in_p, h0, p["norm_mix_pre"], dh1, ride=pairs2)
    grads = dict(
        norm_mix_pre=g_mix_pre, norm_mix_post=g_mix_post, norm_ffn_pre=g_ffn_pre, norm_ffn_post=g_ffn_post, w_in=g_w_in,
        q_a_norm=g_qa, w_uq=g_w_uq, kv_a_norm=g_kva, w_ukv=g_w_ukv, attn_out_norm=g_ao, ssm_conv_w=g_scw[:SSM_K],
        ssm_conv_b=g_scb, ssm_dt_bias=g_dtb[:, :SSM_H], ssm_A_log=g_alog[:, :SSM_H], ssm_D=g_dd[:, :SSM_H],
        ssm_norm=g_ssm_norm, w_out=g_w_out, w_up=g_w_up, ffn_conv_w=g_fcw[:FFN_K], ffn_conv_b=g_fcb, w_down=g_w_down)
    return loss, dh0[FRONT:], dh0[PAD_ROWS:FRONT], grads, (list(pairs2) + list(pairs), list(got2) + list(got))


N_CHIPS = 4
BIG = (("w_in", (D, D_IN // N_CHIPS)), ("w_uq", (QR // N_CHIPS, MLA_H, DN + DR)), ("w_ukv", (KVR // N_CHIPS, MLA_H, DN + DV)),
       ("w_out", (2 * D // N_CHIPS, D)), ("w_up", (D, 2 * D_FF // N_CHIPS)), ("w_down", (D_FF // N_CHIPS, D)))
SMALL_SHARDED = (("meta_tokens", (N_META, D // N_CHIPS)), ("ssm_conv_w", (SSM_K, D_XBC // N_CHIPS)),
                 ("ffn_conv_w", (FFN_K, 2 * D_FF // N_CHIPS)))
SMALL_REPL = (("norm_mix_pre", D), ("norm_mix_post", D), ("norm_ffn_pre", D), ("norm_ffn_post", D), ("q_a_norm", QR),
              ("kv_a_norm", KVR), ("attn_out_norm", D), ("ssm_conv_b", D_XBC), ("ssm_dt_bias", SSM_H), ("ssm_A_log", SSM_H),
              ("ssm_D", SSM_H), ("ssm_norm", D_SSM), ("ffn_conv_b", 2 * D_FF))
ANY = pl.BlockSpec(memory_space=pl.ANY)


def _pad128(v):
    n = v.shape[0]
    return jnp.concatenate([v, jnp.zeros(((-n) % 128,), v.dtype)]) if n % 128 else v


def _pack_rows(vs, rows):
    flat = jnp.concatenate([_pad128(v.reshape(-1)) for v in vs])
    flat = jnp.concatenate([flat, jnp.zeros((rows * 128 - flat.shape[0],), flat.dtype)])
    return flat.reshape(rows, 128)


def _unpack_rows(pack, sizes):
    flat = pack.reshape(-1)
    out, off = [], 0
    for n in sizes:
        out.append(flat[off:off + n])
        off += n + (-n) % 128
    return out


def _my_place():
    return lax.axis_index("x"), lax.axis_index("y"), lax.axis_index("c")


def _other_chips(x, y):
    return [(1 - x, y), (x, 1 - y), (1 - x, 1 - y)]


def _remote(src, dst, send, recv, dev):
    return pltpu.make_async_remote_copy(src_ref=src, dst_ref=dst, send_sem=send, recv_sem=recv, device_id=dev,
                                        device_id_type=MESH)


SMALL_AG_ROWS = 80


def _gather_weights(shards, small, name):
    arrs = list(shards) + ([] if small is None else [small])
    n, nb = len(arrs), len(shards)

    def body(*refs):
        ins, outs = refs[:n], refs[n:2 * n]
        send, recv, lsem = refs[2 * n:]
        x, y, c = _my_place()
        me = 2 * x + y
        chips = _other_chips(x, y)
        slot = lambda w, chip, cc: outs[w].at[chip, cc] if w < nb else outs[w].at[chip]
        mine = lambda w: slot(w, me, c) if w < nb else ins[w]
        loc = [pltpu.make_async_copy(ins[w], outs[w].at[me], lsem.at[w - nb]) for w in range(nb, n)]
        for cp in loc:
            cp.start()
        sends = []
        for w in range(n):
            for kk, (cx, cy) in enumerate(chips):
                sends.append(_remote(mine(w), slot(w, me, c), send.at[3 * w + kk], recv.at[3 * w + kk], (cx, cy, c)))
        for cp in sends:
            cp.start()
        for w in range(nb):
            for kk, (cx, cy) in enumerate(chips):
                src = 2 * cx + cy
                _remote(mine(w), slot(w, src, c), send.at[3 * w + kk], recv.at[3 * w + kk], (cx, cy, c)).wait_recv()
                fwd = _remote(slot(w, src, c), slot(w, src, c), send.at[3 * (n + w) + kk], recv.at[3 * (n + w) + kk], (x, y, 1 - c))
                fwd.start()
                sends.append(fwd)
        for w in range(n):
            for kk, (cx, cy) in enumerate(chips):
                src = 2 * cx + cy
                if w < nb:
                    _remote(mine(w), slot(w, src, 1 - c), send.at[3 * (n + w) + kk], recv.at[3 * (n + w) + kk],
                            (x, y, 1 - c)).wait_recv()
                else:
                    _remote(ins[w], slot(w, src, c), send.at[3 * w + kk], recv.at[3 * w + kk], (cx, cy, c)).wait_recv()
        for cp in sends:
            cp.wait_send()
        for cp in loc:
            cp.wait()

    return pl.pallas_call(
        body, name=name, in_specs=[ANY] * n, out_specs=[ANY] * n,
        out_shape=[_sds(a.shape, a.dtype) for a in shards] + ([] if small is None else [_sds((N_CHIPS,) + small.shape, small.dtype)]),
        input_output_aliases={w: w for w in range(nb)},
        scratch_shapes=[pltpu.SemaphoreType.DMA((3 * (n + nb),)), pltpu.SemaphoreType.DMA((3 * (n + nb),)),
                        pltpu.SemaphoreType.DMA((max(n - nb, 1),))])(*arrs)


def _place_own(wt, chip, name):
    r, c = wt.shape
    tr = _row_tile(r, c)

    def body(c_ref, w_ref, o_ref):
        o_ref[...] = w_ref[...].astype(BF16)

    return pl.pallas_call(
        body, name=name, out_shape=_sds((N_CHIPS, r, c), BF16),
        grid_spec=pltpu.PrefetchScalarGridSpec(
            num_scalar_prefetch=1, grid=(r // tr,), in_specs=[pl.BlockSpec((tr, c), lambda i, cr: (i, 0))],
            out_specs=pl.BlockSpec((None, tr, c), lambda i, cr: (cr[0], i, 0))),
        compiler_params=_cp("parallel"))(chip, wt)


def _send_sibling_halves(gs, name):
    n = len(gs)

    def body(*refs):
        ins, outs, send, recv = refs[:n], refs[n:2 * n], refs[2 * n], refs[2 * n + 1]
        x, y, c = _my_place()
        cps = [_remote(ins[w].at[:, 1 - c], outs[w], send.at[w], recv.at[w], (x, y, 1 - c)) for w in range(n)]
        for cp in cps:
            cp.start()
        for cp in cps:
            cp.wait()

    return pl.pallas_call(
        body, name=name, in_specs=[ANY] * n, out_specs=[ANY] * n,
        out_shape=[_sds((g.shape[0],) + g.shape[2:], g.dtype) for g in gs],
        scratch_shapes=[pltpu.SemaphoreType.DMA((n,)), pltpu.SemaphoreType.DMA((n,))])(*gs)


def _ride_gather(bufs, send, recv, phase):
    n = len(bufs)
    x, y, c = _my_place()
    me = 2 * x + y
    for w in range(n):
        for kk, (cx, cy) in enumerate(_other_chips(x, y)):
            src = 2 * cx + cy
            out = lambda: _remote(bufs[w].at[me, c], bufs[w].at[me, c], send.at[3 * w + kk], recv.at[3 * w + kk], (cx, cy, c))
            fwd = lambda: _remote(bufs[w].at[src, c], bufs[w].at[src, c], send.at[3 * (n + w) + kk],
                                  recv.at[3 * (n + w) + kk], (x, y, 1 - c))
            if phase == 0:
                out().start()
            elif phase == 1:
                _remote(bufs[w].at[me, c], bufs[w].at[src, c], send.at[3 * w + kk], recv.at[3 * w + kk], (cx, cy, c)).wait_recv()
                fwd().start()
            else:
                _remote(bufs[w].at[me, c], bufs[w].at[src, 1 - c], send.at[3 * (n + w) + kk], recv.at[3 * (n + w) + kk],
                        (x, y, 1 - c)).wait_recv()
                out().wait_send()
                fwd().wait_send()


def _ride_exchange(ps, outs, send, recv, phase):
    x, y, c = _my_place()
    for w in range(len(ps)):
        for kk, (cx, cy) in enumerate(_other_chips(x, y)):
            cp = _remote(ps[w].at[2 * cx + cy], outs[w].at[kk], send.at[3 * w + kk], recv.at[3 * w + kk], (cx, cy, c))
            if phase == 0:
                cp.start()
            else:
                cp.wait()


def _share_sibling(halves):
    n = len(halves)

    def body(*refs):
        outs, send, recv = refs[n:2 * n], refs[2 * n], refs[2 * n + 1]
        x, y, c = _my_place()
        cps = [_remote(outs[w].at[c], outs[w].at[c], send.at[w], recv.at[w], (x, y, 1 - c)) for w in range(n)]
        for cp in cps:
            cp.start()
        for w in range(n):
            cps[w].wait_send()
            _remote(outs[w].at[c], outs[w].at[1 - c], send.at[w], recv.at[w], (x, y, 1 - c)).wait_recv()

    return pl.pallas_call(
        body, name="share_sibling", in_specs=[ANY] * n, out_specs=[ANY] * n,
        out_shape=[_sds(h.shape, h.dtype) for h in halves], input_output_aliases={w: w for w in range(n)},
        scratch_shapes=[pltpu.SemaphoreType.DMA((n,)), pltpu.SemaphoreType.DMA((n,))])(*halves)


def _row_tile(r, c, cap=1 << 20):
    return next(t for t in range(r, 0, -1) if r % t == 0 and (t % 8 == 0 or t == r) and t * c * 4 <= cap)


def _add_pair(g, t, core, name):
    _, _, r, c = g.shape
    tr = _row_tile(r, c)

    def body(c_ref, g_ref, t_ref, o_ref):
        o_ref[...] = (g_ref[...] + t_ref[...]).astype(BF16)

    return pl.pallas_call(
        body, name=name, out_shape=_sds(t.shape, BF16),
        grid_spec=pltpu.PrefetchScalarGridSpec(
            num_scalar_prefetch=1, grid=(N_CHIPS, r // tr),
            in_specs=[pl.BlockSpec((None, None, tr, c), lambda j, i, cr: (j, cr[0], i, 0)),
                      pl.BlockSpec((None, tr, c), lambda j, i, cr: (j, i, 0))],
            out_specs=pl.BlockSpec((None, tr, c), lambda j, i, cr: (j, i, 0))),
        compiler_params=_cp("parallel", "parallel"))(core, g, t)


def _add_chips(p, got, chip, name):
    _, r, c = p.shape
    tr = _row_tile(r, c)

    def body(c_ref, p_ref, g_ref, o_ref):
        o_ref[...] = ((p_ref[...].astype(F32) + g_ref[0].astype(F32)) + g_ref[1].astype(F32)) + g_ref[2].astype(F32)

    return pl.pallas_call(
        body, name=name, out_shape=_sds((2, r, c), F32),
        grid_spec=pltpu.PrefetchScalarGridSpec(
            num_scalar_prefetch=1, grid=(r // tr,),
            in_specs=[pl.BlockSpec((None, tr, c), lambda i, cr: (cr[0], i, 0)), pl.BlockSpec((3, tr, c), lambda i, cr: (0, i, 0))],
            out_specs=pl.BlockSpec((None, tr, c), lambda i, cr: (cr[1], i, 0))),
        compiler_params=_cp("parallel"))(chip, p, got)


SMALL_AR_ROWS = 424


def _allreduce_small(v):
    def body(v_ref, o_ref, gath, send, recv):
        x, y, c = _my_place()
        me = 4 * x + 2 * y + c
        gath[me] = v_ref[...]
        cps = []
        for dd in range(1, 8):
            dx, dy, dc = dd >> 2, (dd >> 1) & 1, dd & 1
            peer = (1 - x if dx else x, 1 - y if dy else y, 1 - c if dc else c)
            cps.append(_remote(v_ref, gath.at[me], send.at[dd - 1], recv.at[dd - 1], peer))
        for cp in cps:
            cp.start()
        for cp in cps:
            cp.wait()
        acc = gath[0]
        for dev in range(1, 8):
            acc = acc + gath[dev]
        o_ref[...] = acc

    vm = pl.BlockSpec(memory_space=pltpu.VMEM)
    return pl.pallas_call(
        body, name="allreduce_small", in_specs=[vm], out_specs=vm, out_shape=_sds(v.shape, F32),
        scratch_shapes=[pltpu.VMEM((8,) + v.shape, F32), pltpu.SemaphoreType.DMA((7,)), pltpu.SemaphoreType.DMA((7,))])(v)


def _adamw(w, g, m, v, name):
    r, c = w.shape
    tr = _row_tile(r, c)

    def body(w_ref, g_ref, m_ref, v_ref, d_ref, m2_ref, v2_ref):
        gv = g_ref[...]
        m2 = ADAM_B1 * m_ref[...] + (1.0 - ADAM_B1) * gv
        v2 = ADAM_B2 * v_ref[...] + (1.0 - ADAM_B2) * jnp.square(gv)
        m_hat = m2 / (1.0 - ADAM_B1 ** ADAM_STEP)
        v_hat = v2 / (1.0 - ADAM_B2 ** ADAM_STEP)
        d_ref[...] = -ADAM_LR * (m_hat / (jnp.sqrt(v_hat) + ADAM_EPS) + ADAM_WD * w_ref[...])
        m2_ref[...] = m2
        v2_ref[...] = v2

    return pl.pallas_call(
        body, name=name, grid=(r // tr,), in_specs=[_rows(tr, c)] * 4, out_specs=[_rows(tr, c)] * 3,
        out_shape=[_sds((r, c), F32)] * 3, compiler_params=_cp("parallel"))(w, g, m, v)


WEIGHT_NAMES = ("meta_tokens", "norm_mix_pre", "norm_mix_post", "norm_ffn_pre", "norm_ffn_post", "w_in", "q_a_norm", "w_uq",
                "kv_a_norm", "w_ukv", "attn_out_norm", "ssm_conv_w", "ssm_conv_b", "ssm_dt_bias", "ssm_A_log", "ssm_D",
                "ssm_norm", "w_out", "w_up", "ffn_conv_w", "ffn_conv_b", "w_down")
SMALL_ADAM_ROWS = 192


def kernel(x, meta_tokens, norm_mix_pre, norm_mix_post, norm_ffn_pre, norm_ffn_post, w_in, q_a_norm, w_uq, kv_a_norm, w_ukv, attn_out_norm, ssm_conv_w, ssm_conv_b, ssm_dt_bias, ssm_A_log, ssm_D, ssm_norm, w_out, w_up, ffn_conv_w, ffn_conv_b, w_down, loss_target, m_meta_tokens, m_norm_mix_pre, m_norm_mix_post, m_norm_ffn_pre, m_norm_ffn_post, m_w_in, m_q_a_norm, m_w_uq, m_kv_a_norm, m_w_ukv, m_attn_out_norm, m_ssm_conv_w, m_ssm_conv_b, m_ssm_dt_bias, m_ssm_A_log, m_ssm_D, m_ssm_norm, m_w_out, m_w_up, m_ffn_conv_w, m_ffn_conv_b, m_w_down, v_meta_tokens, v_norm_mix_pre, v_norm_mix_post, v_norm_ffn_pre, v_norm_ffn_post, v_w_in, v_q_a_norm, v_w_uq, v_kv_a_norm, v_w_ukv, v_attn_out_norm, v_ssm_conv_w, v_ssm_conv_b, v_ssm_dt_bias, v_ssm_A_log, v_ssm_D, v_ssm_norm, v_w_out, v_w_up, v_ffn_conv_w, v_ffn_conv_b, v_w_down):
    args = locals()
    w = {n: args[n] for n in WEIGHT_NAMES}
    mom = {n: args["m_" + n] for n in WEIGHT_NAMES}
    var = {n: args["v_" + n] for n in WEIGHT_NAMES}
    cx, cy, cc = _my_place()
    chip = 2 * cx + cy

    two_d = {n: (shp[0], functools.reduce(lambda a, b: a * b, shp[1:])) for n, shp in BIG}
    names = [n for n, _ in BIG]
    core_i = cc.astype(jnp.int32).reshape(1)
    chip_i = chip.astype(jnp.int32).reshape(1)
    early, late = names[:3], names[3:]
    halves = lambda n, a: a.reshape(N_CHIPS, 2, two_d[n][0] // 2, two_d[n][1])
    bufs = {n: halves(n, _place_own(w[n].reshape(two_d[n]), chip_i, "place_" + n)) for n in names}
    small = _pack_rows([w[n] for n, _ in SMALL_SHARDED], SMALL_AG_ROWS)
    *gathered, small_all = _gather_weights([bufs[n] for n in early], small, "allgather_weights")
    gath = {n: a.reshape((N_CHIPS,) + two_d[n]) for n, a in zip(early, gathered)}
    p = dict(w_in=gath["w_in"].transpose(1, 0, 2).reshape(D, D_IN), w_uq=gath["w_uq"].reshape(QR, MLA_H, DN + DR),
             w_ukv=gath["w_ukv"].reshape(KVR, MLA_H, DN + DV))
    sm_parts = [_unpack_rows(small_all[j], [a * b for _, (a, b) in SMALL_SHARDED]) for j in range(N_CHIPS)]
    for i, (n, shp) in enumerate(SMALL_SHARDED):
        p[n] = jnp.concatenate([sm_parts[j][i].reshape(shp) for j in range(N_CHIPS)], axis=1)
    for n, _ in SMALL_REPL:
        p[n] = w[n]
    meta_full = p.pop("meta_tokens")

    place_i = jnp.stack([chip, cc]).astype(jnp.int32)

    def pair_sums(gd, group):
        gd = dict(gd)
        if "w_in" in gd:
            gd["w_in"] = gd["w_in"].reshape(D, N_CHIPS, D_IN // N_CHIPS).transpose(1, 0, 2)
        gs = [halves(n, gd[n]) for n in group]
        from_sib = _send_sibling_halves(gs, "reduce_sibling_" + group[0])
        return [_add_pair(gg, tt, core_i, "reduce_pair_" + n) for n, gg, tt in zip(group, gs, from_sib)]

    loss_part, gx, gmeta, g, (pairs, got) = _device_step(
        x[0], loss_target[0], meta_full, p, late_bufs=[bufs[n] for n in late], early_reduce=lambda gd: pair_sums(gd, late),
        last_reduce=lambda gd: pair_sums(gd, early))

    small_names = [n for n, _ in SMALL_REPL] + ["ssm_conv_w", "ffn_conv_w"]
    small_sizes = [128] + [sz for _, sz in SMALL_REPL] + [N_META * D, SSM_K * D_XBC, FFN_K * 2 * D_FF]
    order = [n for n, _ in SMALL_REPL]
    sp = _pack_rows([loss_part[0]] + [g[n] for n in order] + [gmeta, g["ssm_conv_w"], g["ffn_conv_w"]], SMALL_AR_ROWS)
    red = _unpack_rows(_allreduce_small(sp), small_sizes)
    loss = red[0][0]
    gfull = {n: red[1 + i].reshape(1, -1) for i, n in enumerate(order)}
    n_r = len(order)
    gfull["meta_tokens"] = lax.dynamic_slice_in_dim(red[1 + n_r].reshape(N_META, D), chip * (D // N_CHIPS), D // N_CHIPS, axis=1)
    gfull["ssm_conv_w"] = lax.dynamic_slice_in_dim(red[2 + n_r].reshape(SSM_K, D_XBC), chip * (D_XBC // N_CHIPS),
                                                   D_XBC // N_CHIPS, axis=1)[None]
    gfull["ffn_conv_w"] = lax.dynamic_slice_in_dim(red[3 + n_r].reshape(FFN_K, 2 * D_FF), chip * (2 * D_FF // N_CHIPS),
                                                   2 * D_FF // N_CHIPS, axis=1)[None]

    mine = [_add_chips(pp, gg, place_i, "reduce_chips_" + n) for n, pp, gg in zip(names, pairs, got)]
    for n, both in zip(names, _share_sibling(mine)):
        gfull[n] = both.reshape(two_d[n])

    delta, new_m, new_v = {}, {}, {}
    for n, shp in BIG:
        outs = _adamw(w[n].reshape(two_d[n]), gfull[n], mom[n].reshape(two_d[n]), var[n].reshape(two_d[n]), "adamw_" + n)
        delta[n], new_m[n], new_v[n] = (o.reshape((1,) + shp) for o in outs)
    snames = order + ["meta_tokens", "ssm_conv_w", "ffn_conv_w"]
    ssizes = [functools.reduce(lambda a, b: a * b, w[n].shape) for n in snames]
    packs = [_pack_rows([d[n] for n in snames], SMALL_ADAM_ROWS) for d in (w, gfull, mom, var)]
    outs = _adamw(*packs, "adamw_small")
    for d, o in zip((delta, new_m, new_v), outs):
        for n, piece in zip(snames, _unpack_rows(o, ssizes)):
            d[n] = piece.reshape(w[n].shape)
    gout = {n: gfull[n].reshape(w[n].shape) for n in WEIGHT_NAMES}
    return (loss, gx[None], *[gout[n] for n in WEIGHT_NAMES], *[delta[n] for n in WEIGHT_NAMES],
            *[new_m[n] for n in WEIGHT_NAMES], *[new_v[n] for n in WEIGHT_NAMES])
```

```python
import functools

import jax
import jax.numpy as jnp
from jax import lax
from jax.experimental import pallas as pl
from jax.experimental.pallas import tpu as pltpu

F32 = jnp.float32
BF16 = jnp.bfloat16

D = 1024
N_META = 16
FRONT = 128
PAD_ROWS = FRONT - N_META
MLA_H = 8
DN, DR, DV = 128, 64, 128
QR, KVR = 384, 256
SOFTMAX_SCALE = (DN + DR) ** -0.5
ROPE_THETA = 10000.0
SSM_H, SSM_P, SSM_G, SSM_N, SSM_K = 16, 64, 2, 128, 4
CHUNK = 128
D_SSM = SSM_H * SSM_P
D_XBC = D_SSM + 2 * SSM_G * SSM_N
GSZ = D_SSM // SSM_G
D_FF = 2816
FFN_K = 3
EPS = 1e-6
IN_SPLITS = (QR, KVR, DR, D_SSM, D_XBC, SSM_H)
D_IN = sum(IN_SPLITS)
LAT_W = 768
IN_P = LAT_W + D_SSM + D_XBC + 128
NEG = -1e30
LOG2E = 1.4426950408889634
LN2 = 0.6931471805599453
Q_SCALE = SOFTMAX_SCALE * LOG2E

ADAM_LR, ADAM_B1, ADAM_B2, ADAM_EPS, ADAM_WD, ADAM_STEP = 0.001, 0.9, 0.999, 1e-08, 0.01, 10

VMEM_LIMIT = 56 * 1024 * 1024
MM_ROWS = (640, 320, 128)
MESH = pl.DeviceIdType.MESH


def _sds(shape, dtype):
    return jax.ShapeDtypeStruct(shape, dtype)


def _cp(*sem):
    return pltpu.CompilerParams(dimension_semantics=sem, vmem_limit_bytes=VMEM_LIMIT)


def _rt(n, cands):
    for c in cands:
        if n % c == 0:
            return c
    raise ValueError((n, cands))


def _full(shape):
    nd = len(shape)
    return pl.BlockSpec(shape, lambda *_: (0,) * nd)


def _rows(tr, c):
    return pl.BlockSpec((tr, c), lambda i: (i, 0))


def _sigmoid(x):
    return 1.0 / (1.0 + jnp.exp(-x))


def _silu(x):
    return x * _sigmoid(x)


def _dsilu(x):
    s = _sigmoid(x)
    return s * (1.0 + x * (1.0 - s))


def _softplus(x):
    return jnp.maximum(x, 0.0) + jnp.log(1.0 + jnp.exp(-jnp.abs(x)))


def _rms(x, g):
    r = lax.rsqrt(jnp.mean(x * x, axis=-1, keepdims=True) + EPS)
    return x * r * g


def _rms_bwd(x, g, dy):
    r = lax.rsqrt(jnp.mean(x * x, axis=-1, keepdims=True) + EPS)
    xh = x * r
    dxh = dy * g
    dx = r * (dxh - xh * jnp.mean(dxh * xh, axis=-1, keepdims=True))
    return dx, jnp.sum(dy * xh, axis=0, keepdims=True)


def _dot(a, b):
    return jnp.dot(a, b, preferred_element_type=F32)


def _dot_nt(a, b):
    return lax.dot_general(a, b, (((1,), (1,)), ((), ())), preferred_element_type=F32)


def _dot_tn(a, b):
    return lax.dot_general(a, b, (((0,), (0,)), ((), ())), preferred_element_type=F32)


def _split3(x):
    hi = x.astype(BF16)
    r = x - hi.astype(F32)
    mid = r.astype(BF16)
    return hi, mid, (r - mid.astype(F32)).astype(BF16)


def _dot_hi(a, b, split="a"):
    if split == "a":
        bb = b.astype(BF16)
        return sum(_dot(t, bb) for t in _split3(a))
    ab = a.astype(BF16)
    return sum(_dot(ab, t) for t in _split3(b))


def _dot_nt_hi(a, b):
    bb = b.astype(BF16)
    return sum(_dot_nt(t, bb) for t in _split3(a))


def _shift_down(x, halo, j):
    xr = pltpu.roll(x, j, axis=0)
    hr = pltpu.roll(halo, j, axis=0)
    row = lax.broadcasted_iota(jnp.int32, (8, x.shape[1]), 0)
    first = jnp.where(row < j, hr, xr[:8])
    return jnp.concatenate([first, xr[8:]], axis=0)


def _shift_up(x, nxt, j):
    t = x.shape[0]
    xr = pltpu.roll(x, t - j, axis=0)
    nr = pltpu.roll(nxt, 8 - j, axis=0)
    row = lax.broadcasted_iota(jnp.int32, (8, x.shape[1]), 0)
    last = jnp.where(row + j >= 8, nr, xr[t - 8:])
    return jnp.concatenate([xr[:t - 8], last], axis=0)


def _acc_rows(ref, val, first):
    @pl.when(first)
    def _():
        ref[...] = val

    @pl.when(jnp.logical_not(first))
    def _():
        ref[...] += val


def _mm_tn(a, b, name, tn=None, trs=(1664, 640, 128), chunked=False):
    r, m = a.shape
    n = b.shape[1]
    tn = n if tn is None else tn
    tr = _rt(r, trs)

    def body(a_ref, b_ref, o_ref):
        part = _dot_tn(a_ref[...].astype(BF16), b_ref[...].astype(BF16))
        _acc_rows(o_ref, part, pl.program_id(1) == 0)

    if chunked:
        out_specs, out_shape = pl.BlockSpec((None, m, tn), lambda j, i: (j, 0, 0)), _sds((n // tn, m, tn), F32)
    else:
        out_specs, out_shape = pl.BlockSpec((m, tn), lambda j, i: (0, j)), _sds((m, n), F32)
    return pl.pallas_call(
        body, name=name, grid=(n // tn, r // tr),
        in_specs=[pl.BlockSpec((tr, m), lambda j, i: (i, 0)), pl.BlockSpec((tr, tn), lambda j, i: (i, j))],
        out_specs=out_specs, out_shape=out_shape, compiler_params=_cp("parallel", "arbitrary"))(a, b)


def _inproj(h0, g, w):
    lp = h0.shape[0]
    tr = _rt(lp, MM_ROWS)
    segs = ((0, LAT_W), (LAT_W, LAT_W + D_SSM), (LAT_W + D_SSM, LAT_W + D_SSM + D_XBC), (IN_P - 128, IN_P))

    def body(h_ref, g_ref, w_ref, hn_ref, lat_ref, z_ref, xbc_ref, dt_ref):
        hn = _rms(h_ref[...], g_ref[...]).astype(BF16)
        hn_ref[...] = hn
        for ref, (a, b) in zip((lat_ref, z_ref, xbc_ref, dt_ref), segs):
            ref[...] = _dot(hn, w_ref[:, a:b])

    return pl.pallas_call(
        body, name="inproj", grid=(lp // tr,), in_specs=[_rows(tr, D), _full((1, D)), _full(w.shape)],
        out_specs=[_rows(tr, D), _rows(tr, LAT_W), _rows(tr, D_SSM), _rows(tr, D_XBC), _rows(tr, 128)],
        out_shape=[_sds((lp, D), BF16), _sds((lp, LAT_W), F32), _sds((lp, D_SSM), F32), _sds((lp, D_XBC), F32),
                   _sds((lp, 128), F32)],
        compiler_params=_cp("parallel"))(h0, g, w)


def _rope(x, cos, sa, sb):
    return x * cos + pltpu.roll(x, 96, axis=1) * sa + pltpu.roll(x, 32, axis=1) * sb


def _rope_t(g, cos, sa, sb):
    return g * cos + pltpu.roll(g * sa, 32, axis=1) + pltpu.roll(g * sb, 96, axis=1)


def _mla_prep(lat, qg, kvg, wq, wkv, cos, sa, sb):
    lp = lat.shape[0]
    tr = _rt(lp, MM_ROWS)

    def body(lat_ref, qg_ref, kvg_ref, wq_ref, wkv_ref, cos_ref, sa_ref, sb_ref, q_ref, k_ref, v_ref, ql_ref, kl_ref):
        lat_v = lat_ref[...]
        ql = _rms(lat_v[:, :QR], qg_ref[...]).astype(BF16)
        kl = _rms(lat_v[:, QR:QR + KVR], kvg_ref[...]).astype(BF16)
        ql_ref[...] = ql
        kl_ref[...] = kl
        cos_v, sa_v, sb_v = cos_ref[...], sa_ref[...], sb_ref[...]
        kpe = _rope(lat_v[:, QR + KVR:LAT_W], cos_v, sa_v, sb_v).astype(BF16)
        for h in range(MLA_H):
            q_ref[h, :, 0:DN] = (_dot(ql, wq_ref[:, h * DN:(h + 1) * DN]) * Q_SCALE).astype(BF16)
            qpe = _dot(ql, wq_ref[:, D + h * 128:D + (h + 1) * 128])
            q_ref[h, :, DN:2 * DN] = (_rope(qpe, cos_v, sa_v, sb_v) * Q_SCALE).astype(BF16)
            k_ref[h, :, 0:DN] = _dot(kl, wkv_ref[:, h * DN:(h + 1) * DN]).astype(BF16)
            k_ref[h, :, DN:2 * DN] = kpe
            v_ref[h] = _dot(kl, wkv_ref[:, D + h * DV:D + (h + 1) * DV]).astype(BF16)

    hb = lambda w: pl.BlockSpec((MLA_H, tr, w), lambda i: (0, i, 0))
    return pl.pallas_call(
        body, name="mla_prep", grid=(lp // tr,),
        in_specs=[_rows(tr, LAT_W), _full((1, QR)), _full((1, KVR)), _full(wq.shape), _full(wkv.shape),
                  _rows(tr, 128), _rows(tr, 128), _rows(tr, 128)],
        out_specs=[hb(256), hb(256), hb(128), _rows(tr, QR), _rows(tr, KVR)],
        out_shape=[_sds((MLA_H, lp, 256), BF16), _sds((MLA_H, lp, 256), BF16), _sds((MLA_H, lp, 128), BF16),
                   _sds((lp, QR), BF16), _sds((lp, KVR), BF16)],
        compiler_params=_cp("parallel"))(lat, qg, kvg, wq, wkv, cos, sa, sb)


def _attn_mask(r0, c0, tq, tk, transposed=False):
    if transposed:
        kk = c0 + lax.broadcasted_iota(jnp.int32, (tk, tq), 0)
        qq = r0 + lax.broadcasted_iota(jnp.int32, (tk, tq), 1)
    else:
        qq = r0 + lax.broadcasted_iota(jnp.int32, (tq, tk), 0)
        kk = c0 + lax.broadcasted_iota(jnp.int32, (tq, tk), 1)
    return jnp.logical_and(kk <= qq, kk >= PAD_ROWS)


def _attn_fwd(q, k, v, ride=()):
    lp = q.shape[1]
    t = _rt(lp, (640, 128))
    nq = lp // t

    hp = 2

    nr = len(ride)
    steps = (MLA_H // hp) * nq

    def body(q_ref, k_ref, v_ref, *rest):
        o_ref, lse_ref = rest[nr:nr + 2]
        bufs, sems = rest[nr + 2:2 * nr + 2], rest[2 * nr + 2:]
        qi = pl.program_id(1)
        step = pl.program_id(0) * nq + qi
        if nr:
            pl.when(step == 0)(lambda: _ride_gather(bufs, *sems, 0))
            pl.when(step == steps // 2)(lambda: _ride_gather(bufs, *sems, 1))
        qv = [q_ref[a] for a in range(hp)]

        def tile(kj, carries, masked, live=None):
            kv_rows = pl.ds(pl.multiple_of(kj * t, t), t)
            out = []
            for a in range(hp):
                m, l, acc = carries[a]
                kk = k_ref[a, kv_rows, :]
                vv = v_ref[a, kv_rows, :]
                s = _dot_nt(qv[a], kk)
                if masked:
                    keep = _attn_mask(qi * t, kj * t, t, t)
                    if live is not None:
                        keep = jnp.logical_and(keep, live)
                    s = jnp.where(keep, s, NEG)
                m_new = jnp.maximum(m, jnp.max(s, axis=-1, keepdims=True))
                alpha = jnp.exp2(m - m_new)
                p = jnp.exp2(s - m_new)
                l = alpha * l + jnp.sum(p, axis=-1, keepdims=True)
                acc = alpha * acc + _dot(p.astype(BF16), vv)
                out.append((m_new, l, acc))
            return tuple(out)

        init = tuple((jnp.full((t, 1), NEG, F32), jnp.zeros((t, 1), F32), jnp.zeros((t, DV), F32)) for _ in range(hp))
        carries = tile(0, init, True)
        carries = lax.fori_loop(1, qi, lambda kj, c: tile(kj, c, False), carries)
        carries = tile(qi, carries, True, live=qi > 0)
        for a in range(hp):
            m, l, acc = carries[a]
            o_ref[:, a * DV:(a + 1) * DV] = acc / l
            lse_ref[a] = jnp.broadcast_to(m + jnp.log(l) * LOG2E, (t, 128)).T[:8]
        if nr:
            pl.when(step == steps - 1)(lambda: _ride_gather(bufs, *sems, 2))

    sems = [pltpu.SemaphoreType.DMA((6 * nr,)), pltpu.SemaphoreType.DMA((6 * nr,))] if nr else []
    outs = pl.pallas_call(
        body, name="attn_fwd", grid=(MLA_H // hp, nq),
        in_specs=[pl.BlockSpec((hp, t, 256), lambda h, i: (h, i, 0)), pl.BlockSpec((hp, lp, 256), lambda h, i: (h, 0, 0)),
                  pl.BlockSpec((hp, lp, 128), lambda h, i: (h, 0, 0))] + [ANY] * nr,
        out_specs=[pl.BlockSpec((t, hp * DV), lambda h, i: (i, h)), pl.BlockSpec((hp, 8, t), lambda h, i: (h, 0, i))] + [ANY] * nr,
        out_shape=[_sds((lp, MLA_H * DV), F32), _sds((MLA_H, 8, lp), F32)] + [_sds(b.shape, b.dtype) for b in ride],
        input_output_aliases={3 + w: 2 + w for w in range(nr)}, scratch_shapes=sems,
        compiler_params=_cp("arbitrary", "arbitrary"))(q, k, v, *ride)
    return outs[0], outs[1], list(outs[2:])


def _mixout_bwd(mix, g_post, dh1, o, g_ao, w_out):
    lp = o.shape[0]
    tr = _rt(lp, MM_ROWS)

    def body(mix_ref, gp_ref, dh_ref, o_ref, g_ref, w_ref, dmix_ref, dssm_ref, do_ref, dgp_ref, dg_ref, dl_ref):
        i = pl.program_id(0)
        grow = i * tr + lax.broadcasted_iota(jnp.int32, (tr, D), 0)
        dmix, dgp = _rms_bwd(mix_ref[...], gp_ref[...], jnp.where(grow >= PAD_ROWS, dh_ref[...], 0.0))
        dmix = dmix.astype(BF16)
        dmix_ref[...] = dmix
        _acc_rows(dgp_ref, dgp, i == 0)
        dssm_ref[...] = _dot_nt(dmix, w_ref[D:, :])
        ov = o_ref[...]
        do, dg = _rms_bwd(ov, g_ref[...], _dot_nt(dmix, w_ref[:D, :]))
        do_ref[...] = do
        _acc_rows(dg_ref, dg, i == 0)
        prod = do * ov
        lane = lax.broadcasted_iota(jnp.int32, (1, 128), 1)
        cols = jnp.zeros((tr, 128), F32)
        for h in range(MLA_H):
            cols = cols + jnp.sum(prod[:, h * DV:(h + 1) * DV], axis=-1, keepdims=True) * (lane == h).astype(F32)
        dl_ref[...] = cols.T[:MLA_H]

    return pl.pallas_call(
        body, name="mixout_bwd", grid=(lp // tr,),
        in_specs=[_rows(tr, D), _full((1, D)), _rows(tr, D), _rows(tr, D), _full((1, D)), _full(w_out.shape)],
        out_specs=[_rows(tr, D), _rows(tr, D), _rows(tr, D), _full((1, D)), _full((1, D)), pl.BlockSpec((MLA_H, tr), lambda i: (0, i))],
        out_shape=[_sds((lp, D), BF16), _sds((lp, D), F32), _sds((lp, D), F32), _sds((1, D), F32), _sds((1, D), F32),
                   _sds((MLA_H, lp), F32)],
        compiler_params=_cp("arbitrary"))(mix, g_post, dh1, o, g_ao, w_out)


def _attn_bwd(q, k, v, do, lse_row, delta_row, ride=()):
    lp = q.shape[1]
    t = _rt(lp, (640, 128))
    nq = lp // t

    nr = len(ride)

    def body(q_ref, k_ref, v_ref, do_ref, lse_ref, dl_ref, *rest):
        ps = rest[:nr]
        dq_ref, dk_ref, dv_ref = rest[nr:nr + 3]
        got, sems = rest[nr + 3:2 * nr + 3], rest[2 * nr + 3:]
        kj = pl.program_id(1)
        step = pl.program_id(0) * nq + kj
        if nr:
            pl.when(step == 0)(lambda: _ride_exchange(ps, got, *sems, 0))
        kk = k_ref[0]
        vv = v_ref[0]

        @pl.when(kj == 0)
        def _():
            dq_ref[...] = jnp.zeros_like(dq_ref)

        def tile(qi, carry, masked):
            dk, dv = carry
            q_rows = pl.ds(pl.multiple_of(qi * t, t), t)
            qv = q_ref[0, q_rows, :]
            dob = do_ref[q_rows, :].astype(BF16)
            st = _dot_nt(kk, qv)
            if masked:
                st = jnp.where(_attn_mask(qi * t, kj * t, t, t, transposed=True), st, NEG)
            pt = jnp.exp2(st - lse_ref[0, qi])
            dpt = _dot_nt(vv, dob)
            dst = (pt * (dpt - dl_ref[0, qi])).astype(BF16)
            dv = dv + _dot(pt.astype(BF16), dob)
            dk = dk + _dot(dst, qv)
            dq_ref[0, q_rows, :] += _dot_tn(dst, kk)
            return dk, dv

        carry = tile(kj, (jnp.zeros((t, 256), F32), jnp.zeros((t, DV), F32)), True)
        split = jnp.where(kj == 0, nq, kj + 1)
        carry = lax.fori_loop(kj + 1, split, lambda qi, c: tile(qi, c, True), carry)
        dk, dv = lax.fori_loop(split, nq, lambda qi, c: tile(qi, c, False), carry)
        dk_ref[0] = dk * LN2
        dv_ref[0] = dv
        if nr:
            pl.when(step == MLA_H * nq - 1)(lambda: _ride_exchange(ps, got, *sems, 1))

    stat = pl.BlockSpec((1, nq, 1, t), lambda h, j: (h, 0, 0, 0))
    sems = [pltpu.SemaphoreType.DMA((3 * nr,)), pltpu.SemaphoreType.DMA((3 * nr,))] if nr else []
    outs = pl.pallas_call(
        body, name="attn_bwd", grid=(MLA_H, nq),
        in_specs=[pl.BlockSpec((1, lp, 256), lambda h, j: (h, 0, 0)), pl.BlockSpec((1, t, 256), lambda h, j: (h, j, 0)),
                  pl.BlockSpec((1, t, 128), lambda h, j: (h, j, 0)), pl.BlockSpec((lp, DV), lambda h, j: (0, h)), stat, stat]
        + [ANY] * nr,
        out_specs=[pl.BlockSpec((1, lp, 256), lambda h, j: (h, 0, 0)), pl.BlockSpec((1, t, 256), lambda h, j: (h, j, 0)),
                   pl.BlockSpec((1, t, 128), lambda h, j: (h, j, 0))] + [ANY] * nr,
        out_shape=[_sds((MLA_H, lp, 256), F32), _sds((MLA_H, lp, 256), F32), _sds((MLA_H, lp, 128), F32)]
        + [_sds((3,) + p.shape[1:], p.dtype) for p in ride],
        scratch_shapes=sems, compiler_params=_cp("arbitrary", "arbitrary"))(q, k, v, do, lse_row, delta_row, *ride)
    return outs[0], outs[1], outs[2], list(outs[3:])


def _ssd_consts():
    ri = lax.broadcasted_iota(jnp.int32, (CHUNK, CHUNK), 0)
    ci = lax.broadcasted_iota(jnp.int32, (CHUNK, CHUNK), 1)
    expand = (lax.broadcasted_iota(jnp.int32, (128, D_SSM), 0)
              == lax.broadcasted_iota(jnp.int32, (128, D_SSM), 1) // SSM_P).astype(F32)
    return ri, ci, expand


def _ssd_chunk(c, x_ref, xh_ref, dt_ref, dtT_ref, cw_ref, cb_ref, dtb_ref, dtbT_ref, al_ref, alT_ref):
    ri, ci, expand = _ssd_consts()
    x = x_ref[...]
    halo = jnp.where(c > 0, xh_ref[...], 0.0)
    sh = [x] + [_shift_down(x, halo, j) for j in range(1, SSM_K)]
    cv = cb_ref[...]
    for kk in range(SSM_K):
        cv = cv + cw_ref[kk:kk + 1, :] * sh[SSM_K - 1 - kk]
    xa = _silu(cv)
    grow = c * CHUNK + ri
    gcol = c * CHUNK + lax.broadcasted_iota(jnp.int32, (SSM_H, CHUNK), 1)
    sp = dt_ref[...] + dtb_ref[...]
    spT = dtT_ref[...] + dtbT_ref[...]
    dtc = jnp.where(grow >= PAD_ROWS, _softplus(sp), 0.0)
    dtr = jnp.where(gcol >= PAD_ROWS, _softplus(spT), 0.0)
    arow = -jnp.exp(al_ref[...])
    acolT = -jnp.exp(alT_ref[...])
    ltri = (ci <= ri).astype(F32)
    acs = _dot_hi(ltri, dtc * arow, split="b")
    acsT = _dot_hi(dtr * acolT, (ri <= ci).astype(F32))
    return dict(x=x, sh=sh, cv=cv, xa=xa, sp=sp, dtc=dtc, arow=arow, acs=acs, acsT=acsT, ri=ri, ci=ci, expand=expand,
                grow=grow)


def _ssd_mats(k, s_prev):
    xa, acs, acsT, expand, ri, ci = k["xa"], k["acs"], k["acsT"], k["expand"], k["ri"], k["ci"]
    xs = xa[:, :D_SSM]
    dt_e = _dot_hi(k["dtc"], expand)
    acs_e = _dot_hi(acs, expand)
    last_e = acs_e[CHUNK - 1:CHUNK, :]
    ea = jnp.exp(acs_e)
    f = jnp.exp(last_e - acs_e)
    cd = jnp.exp(last_e)
    xdt = xs * dt_e
    bm = [xa[:, D_SSM + g * SSM_N:D_SSM + (g + 1) * SSM_N] for g in range(SSM_G)]
    cm = [xa[:, D_SSM + (SSM_G + g) * SSM_N:D_SSM + (SSM_G + g + 1) * SSM_N] for g in range(SSM_G)]
    bmb = [b.astype(BF16) for b in bm]
    cmb = [cc.astype(BF16) for cc in cm]
    cb = [_dot_nt(cmb[g], bmb[g]) for g in range(SSM_G)]
    lam, mm = [], []
    causal = jnp.where(ci <= ri, 0.0, NEG)
    for h in range(SSM_H):
        lam_h = jnp.exp((acs[:, h:h + 1] - acsT[h:h + 1, :]) + causal)
        lam.append(lam_h)
        mm.append(cb[h // (SSM_H // SSM_G)] * lam_h)
    lo = lax.broadcasted_iota(jnp.int32, (CHUNK, 128), 1) < SSM_P
    xdt_h = []
    for h in range(SSM_H):
        pair = xdt[:, (h // 2) * 128:(h // 2 + 1) * 128]
        xdt_h.append(jnp.where(lo if h % 2 == 0 else jnp.logical_not(lo), pair, 0.0).astype(BF16))
    ydiag = jnp.concatenate(
        [_dot(mm[2 * j].astype(BF16), xdt_h[2 * j]) + _dot(mm[2 * j + 1].astype(BF16), xdt_h[2 * j + 1])
         for j in range(SSM_H // 2)], axis=1)
    t_off = [_dot(cmb[g], s_prev[g].astype(BF16)) for g in range(SSM_G)]
    yoff = jnp.concatenate(t_off, axis=1) * ea
    return dict(xs=xs, dt_e=dt_e, acs_e=acs_e, ea=ea, f=f, cd=cd, xdt=xdt, bm=bm, cm=cm, bmb=bmb, cmb=cmb, cb=cb, lam=lam,
                mm=mm, lo=lo, xdt_h=xdt_h, ydiag=ydiag, t_off=t_off, yoff=yoff)


def _ssd_specs(nc, rev):
    ix = (lambda i: nc - 1 - i) if rev else (lambda i: i)
    return [
        pl.BlockSpec((CHUNK, D_XBC), lambda i: (ix(i), 0)),
        pl.BlockSpec((8, D_XBC), lambda i: (jnp.maximum(ix(i) * (CHUNK // 8) - 1, 0), 0)),
        pl.BlockSpec((CHUNK, D_SSM), lambda i: (ix(i), 0)),
        pl.BlockSpec((CHUNK, 128), lambda i: (ix(i), 0)),
        pl.BlockSpec((SSM_H, CHUNK), lambda i: (0, ix(i))),
        _full((8, D_XBC)), _full((1, D_XBC)), _full((1, 128)), _full((SSM_H, 1)), _full((1, 128)), _full((SSM_H, 1)),
        _full((1, D_SSM)), _full((1, D_SSM)),
    ]


def _ssd_fwd(xbc, z, dtr, dtrT, cw, cb, dtb, dtbT, alog, alogT, d_e, ng):
    lp = xbc.shape[0]
    nc = lp // CHUNK

    def body(x_ref, xh_ref, z_ref, dt_ref, dtT_ref, cw_ref, cb_ref, dtb_ref, dtbT_ref, al_ref, alT_ref, de_ref, ng_ref,
             y_ref, st_ref, s_scr):
        c = pl.program_id(0)

        @pl.when(c == 0)
        def _():
            s_scr[...] = jnp.zeros_like(s_scr)

        k = _ssd_chunk(c, x_ref, xh_ref, dt_ref, dtT_ref, cw_ref, cb_ref, dtb_ref, dtbT_ref, al_ref, alT_ref)
        s_prev = [s_scr[g] for g in range(SSM_G)]
        st_ref[0] = s_scr[...]
        m = _ssd_mats(k, s_prev)
        xd = (m["xdt"] * m["f"]).astype(BF16)
        for g in range(SSM_G):
            sl = slice(g * GSZ, (g + 1) * GSZ)
            s_scr[g] = m["cd"][:, sl] * s_prev[g] + _dot(m["bm"][g].T.astype(BF16), xd[:, sl])
        y = m["ydiag"] + m["yoff"] + de_ref[...] * m["xs"]
        u = y * _silu(z_ref[...])
        outs = []
        for g in range(SSM_G):
            ug = u[:, g * GSZ:(g + 1) * GSZ]
            outs.append(ug * lax.rsqrt(jnp.mean(ug * ug, axis=-1, keepdims=True) + EPS))
        y_ref[...] = jnp.concatenate(outs, axis=1) * ng_ref[...]

    return pl.pallas_call(
        body, name="ssd_fwd", grid=(nc,), in_specs=_ssd_specs(nc, False),
        out_specs=[_rows(CHUNK, D_SSM), pl.BlockSpec((1, SSM_G, SSM_N, GSZ), lambda i: (i, 0, 0, 0))],
        out_shape=[_sds((lp, D_SSM), F32), _sds((nc, SSM_G, SSM_N, GSZ), F32)],
        scratch_shapes=[pltpu.VMEM((SSM_G, SSM_N, GSZ), F32)],
        compiler_params=_cp("arbitrary"))(xbc, xbc, z, dtr, dtrT, cw, cb, dtb, dtbT, alog, alogT, d_e, ng)


def _ssd_bwd(dssm, xbc, z, dtr, dtrT, st, cw, cb, dtb, dtbT, alog, alogT, d_e, ng):
    lp = xbc.shape[0]
    nc = lp // CHUNK
    hpg = SSM_H // SSM_G

    def body(dy_ref, x_ref, xh_ref, z_ref, dt_ref, dtT_ref, st_ref, cw_ref, cb_ref, dtb_ref, dtbT_ref, al_ref, alT_ref,
             de_ref, ng_ref, dz_ref, dx_ref, ddt_ref, dcw_ref, dcb_ref, ddtb_ref, dal_ref, dd_ref, dng_ref, ds_scr, nx_scr):
        i = pl.program_id(0)
        c = nc - 1 - i
        first = i == 0

        @pl.when(first)
        def _():
            ds_scr[...] = jnp.zeros_like(ds_scr)
            nx_scr[...] = jnp.zeros_like(nx_scr)

        k = _ssd_chunk(c, x_ref, xh_ref, dt_ref, dtT_ref, cw_ref, cb_ref, dtb_ref, dtbT_ref, al_ref, alT_ref)
        s_prev = [st_ref[0, g] for g in range(SSM_G)]
        m = _ssd_mats(k, s_prev)
        ri, ci, expand = k["ri"], k["ci"], k["expand"]
        xs, acs, acsT = m["xs"], k["acs"], k["acsT"]
        zv = z_ref[...]
        dout = dy_ref[...]
        ngv = ng_ref[...]
        y = m["ydiag"] + m["yoff"] + de_ref[...] * xs
        sz = _silu(zv)
        u = y * sz
        du_parts, dng_parts = [], []
        for g in range(SSM_G):
            sl = slice(g * GSZ, (g + 1) * GSZ)
            dug, dngg = _rms_bwd(u[:, sl], ngv[:, sl], dout[:, sl])
            du_parts.append(dug)
            dng_parts.append(dngg)
        du = jnp.concatenate(du_parts, axis=1)
        _acc_rows(dng_ref, jnp.concatenate(dng_parts, axis=1), first)
        dy = du * sz
        dz_ref[...] = du * y * _dsilu(zv)
        dd_e = jnp.sum(dy * xs, axis=0, keepdims=True)
        _acc_rows(dd_ref, _dot_nt_hi(dd_e, expand), first)
        dxs = de_ref[...] * dy
        dacs_e = dy * m["yoff"]
        dtg = (dy * m["ea"]).astype(BF16)
        dxdt = jnp.zeros_like(xs)
        dlast_e = []
        db, dc, ds_prev = [], [], []
        xd = m["xdt"] * m["f"]
        dxd_all = []
        for g in range(SSM_G):
            sl = slice(g * GSZ, (g + 1) * GSZ)
            dsg = ds_scr[g]
            spb = s_prev[g].astype(BF16)
            dc.append(_dot_nt(dtg[:, sl], spb))
            dsp = _dot(m["cm"][g].T.astype(BF16), dtg[:, sl]) + m["cd"][:, sl] * dsg
            ds_prev.append(dsp)
            dlast_e.append(jnp.sum(dsg * s_prev[g], axis=0, keepdims=True) * m["cd"][:, sl])
            dsb = dsg.astype(BF16)
            db.append(_dot_nt(xd[:, sl].astype(BF16), dsb))
            dxd_all.append(_dot(m["bmb"][g], dsb))
        dxd = jnp.concatenate(dxd_all, axis=1)
        dxdt = dxd * m["f"]
        dff = dxd * xd
        dacs_e = dacs_e - dff
        dlast_row = jnp.concatenate(dlast_e, axis=1) + jnp.sum(dff, axis=0, keepdims=True)
        dacs = jnp.zeros((CHUNK, 128), F32)
        dacs_t = jnp.zeros((CHUNK, CHUNK), F32)
        lane = lax.broadcasted_iota(jnp.int32, (1, 128), 1)
        dgs = [jnp.zeros((CHUNK, CHUNK), F32) for _ in range(SSM_G)]
        dxdt_pairs = []
        for h in range(SSM_H):
            g = h // hpg
            pr = slice((h // 2) * 128, (h // 2 + 1) * 128)
            lo_h = m["lo"] if h % 2 == 0 else jnp.logical_not(m["lo"])
            dyp = jnp.where(lo_h, dy[:, pr], 0.0).astype(BF16)
            dm = _dot_nt(dyp, m["xdt"][:, pr].astype(BF16))
            dgs[g] = dgs[g] + dm * m["lam"][h]
            w_h = dm * m["mm"][h]
            dacs = dacs + jnp.sum(w_h, axis=1, keepdims=True) * (lane == h).astype(F32)
            dacs_t = dacs_t + jnp.where(ri == h, jnp.sum(w_h, axis=0, keepdims=True), 0.0)
            part = _dot_tn(m["mm"][h].astype(BF16), dyp)
            if h % 2 == 0:
                dxdt_pairs.append(part)
            else:
                dxdt_pairs[-1] = dxdt_pairs[-1] + part
        dxdt = dxdt + jnp.concatenate(dxdt_pairs, axis=1)
        for g in range(SSM_G):
            dgb = dgs[g].astype(BF16)
            dc[g] = dc[g] + _dot(dgb, m["bmb"][g])
            db[g] = db[g] + _dot_tn(dgb, m["cmb"][g])
        expand_b = expand.astype(BF16)
        dacs = dacs - dacs_t.T + _dot_nt(dacs_e.astype(BF16), expand_b)
        dlast = _dot_nt_hi(dlast_row, expand)
        dacs = dacs + jnp.where(ri == CHUNK - 1, dlast, 0.0)
        dxs = dxs + dxdt * m["dt_e"]
        ddt = _dot_nt((dxdt * xs).astype(BF16), expand_b)
        da = _dot_hi((ri <= ci).astype(F32), dacs, split="b")
        ddt = ddt + da * k["arow"]
        dA = jnp.sum(da * k["dtc"], axis=0, keepdims=True)
        _acc_rows(dal_ref, dA * k["arow"], first)
        ddtr = jnp.where(k["grow"] >= PAD_ROWS, ddt * _sigmoid(k["sp"]), 0.0)
        ddt_ref[...] = ddtr
        _acc_rows(ddtb_ref, jnp.sum(ddtr, axis=0, keepdims=True), first)
        for g in range(SSM_G):
            ds_scr[g] = ds_prev[g]
        dxa = jnp.concatenate([dxs] + db + dc, axis=1)
        dcv = dxa * _dsilu(k["cv"])
        _acc_rows(dcb_ref, jnp.sum(dcv, axis=0, keepdims=True), first)
        dcw_rows = [jnp.sum(dcv * k["sh"][SSM_K - 1 - kk], axis=0, keepdims=True) for kk in range(SSM_K)]
        dcw_rows.append(jnp.zeros((8 - SSM_K, D_XBC), F32))
        _acc_rows(dcw_ref, jnp.concatenate(dcw_rows, axis=0), first)
        nxt = nx_scr[...]
        dx = cw_ref[SSM_K - 1:SSM_K, :] * dcv
        for j in range(1, SSM_K):
            dx = dx + cw_ref[SSM_K - 1 - j:SSM_K - j, :] * _shift_up(dcv, nxt, j)
        grow_x = c * CHUNK + lax.broadcasted_iota(jnp.int32, (CHUNK, D_XBC), 0)
        dx_ref[...] = jnp.where(grow_x >= PAD_ROWS, dx, 0.0)
        nx_scr[...] = dcv[:8]

    specs = _ssd_specs(nc, True)
    in_specs = [pl.BlockSpec((CHUNK, D_SSM), lambda i: (nc - 1 - i, 0))] + specs[:5] + [
        pl.BlockSpec((1, SSM_G, SSM_N, GSZ), lambda i: (nc - 1 - i, 0, 0, 0))] + specs[5:]
    rv = lambda w: pl.BlockSpec((CHUNK, w), lambda i: (nc - 1 - i, 0))
    return pl.pallas_call(
        body, name="ssd_bwd", grid=(nc,), in_specs=in_specs,
        out_specs=[rv(D_SSM), rv(D_XBC), rv(128), _full((8, D_XBC)), _full((1, D_XBC)), _full((1, 128)), _full((1, 128)),
                   _full((1, 128)), _full((1, D_SSM))],
        out_shape=[_sds((lp, D_SSM), F32), _sds((lp, D_XBC), F32), _sds((lp, 128), F32), _sds((8, D_XBC), F32),
                   _sds((1, D_XBC), F32), _sds((1, 128), F32), _sds((1, 128), F32), _sds((1, 128), F32), _sds((1, D_SSM), F32)],
        scratch_shapes=[pltpu.VMEM((SSM_G, SSM_N, GSZ), F32), pltpu.VMEM((8, D_XBC), F32)],
        compiler_params=_cp("arbitrary"))(dssm, xbc, xbc, z, dtr, dtrT, st, cw, cb, dtb, dtbT, alog, alogT, d_e, ng)


def _mixout_fwd(o, ssm, h0, g_ao, g_post, w):
    lp = o.shape[0]
    tr = _rt(lp, MM_ROWS)

    def body(o_ref, s_ref, h_ref, ga_ref, gp_ref, w_ref, mi_ref, mix_ref, h1_ref):
        mixin = jnp.concatenate([_rms(o_ref[...], ga_ref[...]), s_ref[...]], axis=1).astype(BF16)
        mi_ref[...] = mixin
        mix = _dot(mixin, w_ref[...])
        mix_ref[...] = mix
        grow = pl.program_id(0) * tr + lax.broadcasted_iota(jnp.int32, (tr, D), 0)
        h1_ref[...] = h_ref[...] + jnp.where(grow >= PAD_ROWS, _rms(mix, gp_ref[...]), 0.0)

    return pl.pallas_call(
        body, name="mixout_fwd", grid=(lp // tr,),
        in_specs=[_rows(tr, D), _rows(tr, D), _rows(tr, D), _full((1, D)), _full((1, D)), _full(w.shape)],
        out_specs=[_rows(tr, 2 * D), _rows(tr, D), _rows(tr, D)],
        out_shape=[_sds((lp, 2 * D), BF16), _sds((lp, D), F32), _sds((lp, D), F32)],
        compiler_params=_cp("parallel"))(o, ssm, h0, g_ao, g_post, w)


def _ffn_up(h1, g, w):
    lp = h1.shape[0]
    tr = _rt(lp, MM_ROWS)
    tn = D_FF // 2

    def body(h_ref, g_ref, w_ref, hn_ref, u_ref):
        hn = _rms(h_ref[...], g_ref[...]).astype(BF16)
        hn_ref[...] = hn
        u_ref[...] = _dot(hn, w_ref[...]).astype(BF16)

    return pl.pallas_call(
        body, name="ffn_up", grid=(lp // tr, 2 * D_FF // tn),
        in_specs=[pl.BlockSpec((tr, D), lambda i, j: (i, 0)), _full((1, D)), pl.BlockSpec((None, D, tn), lambda i, j: (j, 0, 0))],
        out_specs=[pl.BlockSpec((tr, D), lambda i, j: (i, 0)), pl.BlockSpec((tr, tn), lambda i, j: (i, j))],
        out_shape=[_sds((lp, D), BF16), _sds((lp, 2 * D_FF), BF16)],
        compiler_params=_cp("parallel", "arbitrary"))(h1, g, w)


def _ffn_in_bwd(du, w4, h1, g, dh2):
    lp = du.shape[0]
    nch, _, tn = w4.shape
    tr = _rt(lp, (320, 128))

    def body(du_ref, w_ref, h_ref, g_ref, r_ref, o_ref, dg_ref):
        acc = _dot_nt(du_ref[:, 0:tn], w_ref[0])
        for j in range(1, nch):
            acc = acc + _dot_nt(du_ref[:, j * tn:(j + 1) * tn], w_ref[j])
        dx, dg = _rms_bwd(h_ref[...], g_ref[...], acc)
        o_ref[...] = dx + r_ref[...]
        _acc_rows(dg_ref, dg, pl.program_id(0) == 0)

    return pl.pallas_call(
        body, name="ffn_in_bwd", grid=(lp // tr,),
        in_specs=[_rows(tr, nch * tn), _full(w4.shape), _rows(tr, D), _full((1, D)), _rows(tr, D)],
        out_specs=[_rows(tr, D), _full((1, D))], out_shape=[_sds((lp, D), F32), _sds((1, D), F32)],
        compiler_params=_cp("arbitrary"))(du, w4, h1, g, dh2)


FFN_CB = 256


def _ffn_gate(u, cw, cb):
    lp = u.shape[0]
    tr = _rt(lp, (320, 128))

    def body(u_ref, uh_ref, cw_ref, cb_ref, uc_ref, a_ref):
        i = pl.program_id(0)
        for j in range(D_FF // FFN_CB):
            halves = []
            for off in (0, D_FF):
                sl = slice(off + j * FFN_CB, off + (j + 1) * FFN_CB)
                x = u_ref[:, sl].astype(F32)
                halo = jnp.where(i > 0, uh_ref[8:16, sl].astype(F32), 0.0)
                cv = cb_ref[:, sl] + cw_ref[FFN_K - 1:FFN_K, sl] * x
                for s in range(1, FFN_K):
                    cv = cv + cw_ref[FFN_K - 1 - s:FFN_K - s, sl] * _shift_down(x, halo, s)
                uc_ref[:, sl] = cv.astype(BF16)
                halves.append(cv)
            a_ref[:, j * FFN_CB:(j + 1) * FFN_CB] = (_silu(halves[0]) * halves[1]).astype(BF16)

    return pl.pallas_call(
        body, name="ffn_gate", grid=(lp // tr,),
        in_specs=[_rows(tr, 2 * D_FF), pl.BlockSpec((16, 2 * D_FF), lambda i: (jnp.maximum(i * (tr // 16) - 1, 0), 0)),
                  _full((8, 2 * D_FF)), _full((1, 2 * D_FF))],
        out_specs=[_rows(tr, 2 * D_FF), _rows(tr, D_FF)], out_shape=[_sds((lp, 2 * D_FF), BF16), _sds((lp, D_FF), BF16)],
        compiler_params=_cp("parallel"))(u, u, cw, cb)


def _ffn_down(a, w, h1, tgt, g_post):
    lp = a.shape[0]
    tr = _rt(lp, MM_ROWS)
    nb = tr // FRONT

    def body(a_ref, w_ref, h_ref, *rest):
        t_refs, (g_ref, dh2_ref, dd_ref, dg_ref, loss_ref) = rest[:nb], rest[nb:]
        i = pl.program_id(0)
        d = _dot(a_ref[...], w_ref[...])
        gv = g_ref[...]
        h2 = h_ref[...] + _rms(d, gv)
        grow = i * tr + lax.broadcasted_iota(jnp.int32, (tr, D), 0)
        tgt_v = jnp.concatenate([r[...] for r in t_refs], axis=0)
        err = jnp.where(grow >= FRONT, h2 - tgt_v, 0.0)
        dh2 = err * (1.0 / D)
        dh2_ref[...] = dh2
        dd, dg = _rms_bwd(d, gv, dh2)
        dd_ref[...] = dd.astype(BF16)
        _acc_rows(dg_ref, dg, i == 0)
        part = 0.5 * jnp.sum(jnp.sum(err * err, axis=1, keepdims=True), axis=0, keepdims=True) * (1.0 / D)
        _acc_rows(loss_ref, jnp.broadcast_to(part, (8, 128)), i == 0)

    return pl.pallas_call(
        body, name="ffn_down", grid=(lp // tr,),
        in_specs=[_rows(tr, D_FF), _full(w.shape), _rows(tr, D)]
        + [pl.BlockSpec((FRONT, D), functools.partial(lambda i, b: (jnp.maximum(i * nb - 1 + b, 0), 0), b=b)) for b in range(nb)]
        + [_full((1, D))],
        out_specs=[_rows(tr, D), _rows(tr, D), _full((1, D)), _full((8, 128))],
        out_shape=[_sds((lp, D), F32), _sds((lp, D), BF16), _sds((1, D), F32), _sds((8, 128), F32)],
        compiler_params=_cp("arbitrary"))(a, w, h1, *([tgt] * nb), g_post)


def _ffn_gate_bwd(u, uc, dd, w_down, cw):
    lp = u.shape[0]
    tr = _rt(lp, (320, 128))
    n = lp // tr

    def body(u_ref, uc_ref, dd_ref, wd_ref, cw_ref, du_ref, dcw_ref, dcb_ref, nx_scr):
        i = pl.program_id(0)
        t = n - 1 - i
        first = i == 0

        @pl.when(first)
        def _():
            nx_scr[...] = jnp.zeros_like(nx_scr)

        grow = t * tr + lax.broadcasted_iota(jnp.int32, (tr, FFN_CB), 0)
        ddv = dd_ref[...]
        for j in range(D_FF // FFN_CB):
            sls = [slice(off + j * FFN_CB, off + (j + 1) * FFN_CB) for off in (0, D_FF)]
            cvg, cvv = uc_ref[:, sls[0]].astype(F32), uc_ref[:, sls[1]].astype(F32)
            dav = _dot_nt(ddv, wd_ref[j * FFN_CB:(j + 1) * FFN_CB, :])
            dcv = (dav * cvv * _dsilu(cvg), dav * _silu(cvg))
            for hf in range(2):
                sl = sls[hf]
                g = dcv[hf]
                nxt = nx_scr[:, sl]
                ahead = [g] + [_shift_up(g, nxt, s) for s in range(1, FFN_K)]
                x = u_ref[:, sl].astype(F32)
                rows = [jnp.sum(x * ahead[FFN_K - 1 - kk], axis=0, keepdims=True) for kk in range(FFN_K)]
                rows.append(jnp.zeros((8 - FFN_K, FFN_CB), F32))
                upd_w = jnp.concatenate(rows, axis=0)
                upd_b = jnp.sum(g, axis=0, keepdims=True)

                @pl.when(first)
                def _():
                    dcw_ref[:, sl] = upd_w
                    dcb_ref[:, sl] = upd_b

                @pl.when(jnp.logical_not(first))
                def _():
                    dcw_ref[:, sl] += upd_w
                    dcb_ref[:, sl] += upd_b

                du = cw_ref[FFN_K - 1:FFN_K, sl] * g
                for s in range(1, FFN_K):
                    du = du + cw_ref[FFN_K - 1 - s:FFN_K - s, sl] * ahead[s]
                du_ref[:, sl] = jnp.where(grow >= PAD_ROWS, du, 0.0).astype(BF16)
                nx_scr[:, sl] = g[:8]

    wide = pl.BlockSpec((tr, 2 * D_FF), lambda i: (n - 1 - i, 0))
    return pl.pallas_call(
        body, name="ffn_gate_bwd", grid=(n,),
        in_specs=[wide, wide, pl.BlockSpec((tr, D), lambda i: (n - 1 - i, 0)), _full(w_down.shape), _full((8, 2 * D_FF))],
        out_specs=[wide, _full((8, 2 * D_FF)), _full((1, 2 * D_FF))],
        out_shape=[_sds((lp, 2 * D_FF), BF16), _sds((8, 2 * D_FF), F32), _sds((1, 2 * D_FF), F32)],
        scratch_shapes=[pltpu.VMEM((8, 2 * D_FF), F32)],
        compiler_params=_cp("arbitrary"))(u, uc, dd, w_down, cw)


def _mla_bwd(dq, dk, dv, lat, qg, kvg, wq, wkv, cos, sa, sb):
    lp = lat.shape[0]
    tr = _rt(lp, (320, 128))

    def body(dq_ref, dk_ref, dv_ref, lat_ref, qg_ref, kvg_ref, wq_ref, wkv_ref, cos_ref, sa_ref, sb_ref,
             dqf_ref, dkvf_ref, dlat_ref, dqg_ref, dkvg_ref):
        i = pl.program_id(0)
        cos_v, sa_v, sb_v = cos_ref[...], sa_ref[...], sb_ref[...]
        dkpe = jnp.zeros((tr, 128), F32)
        for h in range(MLA_H):
            dqh = dq_ref[h] * SOFTMAX_SCALE
            dqf_ref[:, h * DN:(h + 1) * DN] = dqh[:, :DN].astype(BF16)
            dqf_ref[:, D + h * 128:D + (h + 1) * 128] = _rope_t(dqh[:, DN:], cos_v, sa_v, sb_v).astype(BF16)
            dkh = dk_ref[h]
            dkvf_ref[:, h * DN:(h + 1) * DN] = dkh[:, :DN].astype(BF16)
            dkpe = dkpe + dkh[:, DN:]
            dkvf_ref[:, D + h * DV:D + (h + 1) * DV] = dv_ref[h].astype(BF16)
        dql = _dot_nt(dqf_ref[...], wq_ref[...])
        dkl = _dot_nt(dkvf_ref[...], wkv_ref[...])
        lat_v = lat_ref[...]
        dqc, dqg = _rms_bwd(lat_v[:, :QR], qg_ref[...], dql)
        dkc, dkg = _rms_bwd(lat_v[:, QR:QR + KVR], kvg_ref[...], dkl)
        dlat_ref[:, :QR] = dqc
        dlat_ref[:, QR:QR + KVR] = dkc
        dlat_ref[:, QR + KVR:] = _rope_t(dkpe, cos_v, sa_v, sb_v)
        _acc_rows(dqg_ref, dqg, i == 0)
        _acc_rows(dkvg_ref, dkg, i == 0)

    hb = lambda w: pl.BlockSpec((MLA_H, tr, w), lambda i: (0, i, 0))
    return pl.pallas_call(
        body, name="mla_bwd", grid=(lp // tr,),
        in_specs=[hb(256), hb(256), hb(128), _rows(tr, LAT_W), _full((1, QR)), _full((1, KVR)), _full(wq.shape),
                  _full(wkv.shape), _rows(tr, 128), _rows(tr, 128), _rows(tr, 128)],
        out_specs=[_rows(tr, 2 * D), _rows(tr, 2 * D), _rows(tr, LAT_W), _full((1, QR)), _full((1, KVR))],
        out_shape=[_sds((lp, 2 * D), BF16), _sds((lp, 2 * D), BF16), _sds((lp, LAT_W), F32), _sds((1, QR), F32),
                   _sds((1, KVR), F32)],
        compiler_params=_cp("arbitrary"))(dq, dk, dv, lat, qg, kvg, wq, wkv, cos, sa, sb)


def _inproj_bwd(dlat, dz, dxbc, ddt, w, h0, g, dh1, ride=()):
    lp = h0.shape[0]
    tr = _rt(lp, (320, 128))
    segs = ((0, LAT_W), (LAT_W, LAT_W + D_SSM), (LAT_W + D_SSM, LAT_W + D_SSM + D_XBC), (IN_P - 128, IN_P))
    nr = len(ride)
    steps = lp // tr

    def body(dl_ref, dz_ref, dx_ref, dt_ref, w_ref, h_ref, g_ref, r_ref, *rest):
        ps = rest[:nr]
        o_ref, dg_ref = rest[nr:nr + 2]
        got, sems = rest[nr + 2:2 * nr + 2], rest[2 * nr + 2:]
        step = pl.program_id(0)
        if nr:
            pl.when(step == 0)(lambda: _ride_exchange(ps, got, *sems, 0))
        dhn = jnp.zeros((tr, D), F32)
        for ref, (a, b) in zip((dl_ref, dz_ref, dx_ref, dt_ref), segs):
            dhn = dhn + _dot_nt(ref[...].astype(BF16), w_ref[:, a:b])
        dx, dg = _rms_bwd(h_ref[...], g_ref[...], dhn)
        o_ref[...] = dx + r_ref[...]
        _acc_rows(dg_ref, dg, step == 0)
        if nr:
            pl.when(step == steps - 1)(lambda: _ride_exchange(ps, got, *sems, 1))

    sems = [pltpu.SemaphoreType.DMA((3 * nr,)), pltpu.SemaphoreType.DMA((3 * nr,))] if nr else []
    outs = pl.pallas_call(
        body, name="inproj_bwd", grid=(steps,),
        in_specs=[_rows(tr, LAT_W), _rows(tr, D_SSM), _rows(tr, D_XBC), _rows(tr, 128), _full(w.shape), _rows(tr, D),
                  _full((1, D)), _rows(tr, D)] + [ANY] * nr,
        out_specs=[_rows(tr, D), _full((1, D))] + [ANY] * nr,
        out_shape=[_sds((lp, D), F32), _sds((1, D), F32)] + [_sds((3,) + p.shape[1:], p.dtype) for p in ride],
        scratch_shapes=sems, compiler_params=_cp("arbitrary"))(dlat, dz, dxbc, ddt, w, h0, g, dh1, *ride)
    return outs[0], outs[1], list(outs[2:])


def _rope_tables(lp):
    pos = (jnp.arange(lp, dtype=jnp.int32) - PAD_ROWS).astype(F32)
    inv = ROPE_THETA ** (-jnp.arange(0, DR, 2, dtype=F32) / DR)
    ang = pos[:, None] * inv[None, :]
    cos, sin = jnp.cos(ang), jnp.sin(ang)
    zero = jnp.zeros_like(sin)
    cos128 = jnp.concatenate([cos, cos, cos, cos], axis=1)
    sa128 = jnp.concatenate([-sin, zero, -sin, zero], axis=1)
    sb128 = jnp.concatenate([zero, sin, zero, sin], axis=1)
    return cos128, sa128, sb128


def _pad_rows8(w):
    return jnp.concatenate([w, jnp.zeros((8 - w.shape[0], w.shape[1]), w.dtype)], axis=0)


def _lane_pad(v):
    return jnp.concatenate([v, jnp.zeros((v.shape[0], 128 - v.shape[1]), v.dtype)], axis=1)


def _late_weights(bufs):
    w_out, w_up, w_down = bufs
    return dict(w_out=w_out.reshape(2 * D, D), w_up=w_up.reshape(N_CHIPS, D, 2 * D_FF // N_CHIPS), w_down=w_down.reshape(D_FF, D))


def _device_step(x, tgt, meta, p, late_bufs=(), early_reduce=None, last_reduce=None):
    s = x.shape[0]
    lp = s + FRONT
    zpad = jnp.zeros((PAD_ROWS, D), F32)
    h0 = jnp.concatenate([zpad, meta, x], axis=0)
    cos, sa, sb = _rope_tables(lp)

    w_in = p["w_in"]
    w_in_p = jnp.concatenate([w_in[:, :QR + KVR + DR], jnp.zeros((D, 64), BF16), w_in[:, QR + KVR + DR:],
                              jnp.zeros((D, 128 - SSM_H), BF16)], axis=1)
    w_uq = p["w_uq"]
    wq_p = jnp.concatenate([w_uq[:, :, :DN].reshape(QR, MLA_H * DN),
                            jnp.concatenate([w_uq[:, :, DN:], jnp.zeros((QR, MLA_H, 128 - DR), BF16)], axis=2).reshape(QR, MLA_H * 128)],
                           axis=1)
    w_ukv = p["w_ukv"]
    wkv_p = jnp.concatenate([w_ukv[:, :, :DN].reshape(KVR, MLA_H * DN), w_ukv[:, :, DN:].reshape(KVR, MLA_H * DV)], axis=1)
    scw = _pad_rows8(p["ssm_conv_w"])
    fcw = _pad_rows8(p["ffn_conv_w"])
    dtb, alog = _lane_pad(p["ssm_dt_bias"]), _lane_pad(p["ssm_A_log"])
    dtbT, alogT = p["ssm_dt_bias"].reshape(SSM_H, 1), p["ssm_A_log"].reshape(SSM_H, 1)
    d_e = jnp.repeat(p["ssm_D"], SSM_P, axis=1)

    hn, lat, z, xbc, dtr = _inproj(h0, p["norm_mix_pre"], w_in_p)
    dtrT = dtr[:, :SSM_H].T
    q, k, v, qlat, kvlat = _mla_prep(lat, p["q_a_norm"], p["kv_a_norm"], wq_p, wkv_p, cos, sa, sb)
    o, lse, gathered = _attn_fwd(q, k, v, ride=late_bufs)
    if late_bufs:
        p = dict(p, **_late_weights(gathered))
    ssm, st = _ssd_fwd(xbc, z, dtr, dtrT, scw, p["ssm_conv_b"], dtb, dtbT, alog, alogT, d_e, p["ssm_norm"])
    mixin, mix, h1 = _mixout_fwd(o, ssm, h0, p["attn_out_norm"], p["norm_mix_post"], p["w_out"])
    hn2, u = _ffn_up(h1, p["norm_ffn_pre"], p["w_up"])
    uc, a = _ffn_gate(u, fcw, p["ffn_conv_b"])
    dh2, dd, g_ffn_post, loss = _ffn_down(a, p["w_down"], h1, tgt, p["norm_ffn_post"])

    g_w_down = _mm_tn(a, dd, "ffn_dw_down", tn=512)
    du, g_fcw, g_fcb = _ffn_gate_bwd(u, uc, dd, p["w_down"], fcw)
    dh1, g_ffn_pre = _ffn_in_bwd(du, p["w_up"], h1, p["norm_ffn_pre"], dh2)
    g_w_up = _mm_tn(hn2, du, "ffn_dw_up", tn=D_FF // 2, chunked=True)
    dmix, dssm, do, g_mix_post, g_ao, delta = _mixout_bwd(mix, p["norm_mix_post"], dh1, o, p["attn_out_norm"], p["w_out"])
    g_w_out = _mm_tn(mixin, dmix, "mix_dw_out", tn=512)
    t = _rt(lp, (640, 128))
    pairs = early_reduce(dict(w_out=g_w_out, w_up=g_w_up, w_down=g_w_down)) if early_reduce else ()
    dq, dk, dv, got = _attn_bwd(q, k, v, do, lse[:, 0, :].reshape(MLA_H, lp // t, 1, t), delta.reshape(MLA_H, lp // t, 1, t),
                                ride=pairs)
    dqf, dkvf, dlat, g_qa, g_kva = _mla_bwd(dq, dk, dv, lat, p["q_a_norm"], p["kv_a_norm"], wq_p, wkv_p, cos, sa, sb)
    g_wq_p = _mm_tn(qlat, dqf, "mla_dw_uq")
    g_wkv_p = _mm_tn(kvlat, dkvf, "mla_dw_ukv")
    dz, dxbc, ddtr, g_scw, g_scb, g_dtb, g_alog, g_dd, g_ssm_norm = _ssd_bwd(
        dssm, xbc, z, dtr, dtrT, st, scw, p["ssm_conv_b"], dtb, dtbT, alog, alogT, d_e, p["ssm_norm"])
    g_in_p = jnp.concatenate([_mm_tn(hn, dlat, "in_dw_lat"), _mm_tn(hn, dz, "in_dw_z"), _mm_tn(hn, dxbc, "in_dw_xbc"),
                              _mm_tn(hn, ddtr, "in_dw_dt")], axis=1)
    g_w_in = jnp.concatenate([g_in_p[:, :QR + KVR + DR], g_in_p[:, LAT_W:LAT_W + D_SSM + D_XBC + SSM_H]], axis=1)
    g_w_uq = jnp.concatenate([g_wq_p[:, :D].reshape(QR, MLA_H, DN), g_wq_p[:, D:].reshape(QR, MLA_H, 128)[:, :, :DR]], axis=2)
    g_w_ukv = jnp.concatenate([g_wkv_p[:, :D].reshape(KVR, MLA_H, DN), g_wkv_p[:, D:].reshape(KVR, MLA_H, DV)], axis=2)
    pairs2 = last_reduce(dict(w_in=g_w_in, w_uq=g_w_uq, w_ukv=g_w_ukv)) if last_reduce else ()
    dh0, g_mix_pre, got2 = _inproj_bwd(dlat, dz, dxbc, ddtr, w_in_p, h0, p["norm_mix_pre"], dh1, ride=pairs2)
    grads = dict(
        norm_mix_pre=g_mix_pre, norm_mix_post=g_mix_post, norm_ffn_pre=g_ffn_pre, norm_ffn_post=g_ffn_post, w_in=g_w_in,
        q_a_norm=g_qa, w_uq=g_w_uq, kv_a_norm=g_kva, w_ukv=g_w_ukv, attn_out_norm=g_ao, ssm_conv_w=g_scw[:SSM_K],
        ssm_conv_b=g_scb, ssm_dt_bias=g_dtb[:, :SSM_H], ssm_A_log=g_alog[:, :SSM_H], ssm_D=g_dd[:, :SSM_H],
        ssm_norm=g_ssm_norm, w_out=g_w_out, w_up=g_w_up, ffn_conv_w=g_fcw[:FFN_K], ffn_conv_b=g_fcb, w_down=g_w_down)
    return loss, dh0[FRONT:], dh0[PAD_ROWS:FRONT], grads, (list(pairs2) + list(pairs), list(got2) + list(got))


N_CHIPS = 4
BIG = (("w_in", (D, D_IN // N_CHIPS)), ("w_uq", (QR // N_CHIPS, MLA_H, DN + DR)), ("w_ukv", (KVR // N_CHIPS, MLA_H, DN + DV)),
       ("w_out", (2 * D // N_CHIPS, D)), ("w_up", (D, 2 * D_FF // N_CHIPS)), ("w_down", (D_FF // N_CHIPS, D)))
SMALL_SHARDED = (("meta_tokens", (N_META, D // N_CHIPS)), ("ssm_conv_w", (SSM_K, D_XBC // N_CHIPS)),
                 ("ffn_conv_w", (FFN_K, 2 * D_FF // N_CHIPS)))
SMALL_REPL = (("norm_mix_pre", D), ("norm_mix_post", D), ("norm_ffn_pre", D), ("norm_ffn_post", D), ("q_a_norm", QR),
              ("kv_a_norm", KVR), ("attn_out_norm", D), ("ssm_conv_b", D_XBC), ("ssm_dt_bias", SSM_H), ("ssm_A_log", SSM_H),
              ("ssm_D", SSM_H), ("ssm_norm", D_SSM), ("ffn_conv_b", 2 * D_FF))
ANY = pl.BlockSpec(memory_space=pl.ANY)


def _pad128(v):
    n = v.shape[0]
    return jnp.concatenate([v, jnp.zeros(((-n) % 128,), v.dtype)]) if n % 128 else v


def _pack_rows(vs, rows):
    flat = jnp.concatenate([_pad128(v.reshape(-1)) for v in vs])
    flat = jnp.concatenate([flat, jnp.zeros((rows * 128 - flat.shape[0],), flat.dtype)])
    return flat.reshape(rows, 128)


def _unpack_rows(pack, sizes):
    flat = pack.reshape(-1)
    out, off = [], 0
    for n in sizes:
        out.append(flat[off:off + n])
        off += n + (-n) % 128
    return out


def _my_place():
    return lax.axis_index("x"), lax.axis_index("y"), lax.axis_index("c")


def _other_chips(x, y):
    return [(1 - x, y), (x, 1 - y), (1 - x, 1 - y)]


def _remote(src, dst, send, recv, dev):
    return pltpu.make_async_remote_copy(src_ref=src, dst_ref=dst, send_sem=send, recv_sem=recv, device_id=dev,
                                        device_id_type=MESH)


SMALL_AG_ROWS = 80


def _gather_weights(shards, small, name):
    arrs = list(shards) + ([] if small is None else [small])
    n, nb = len(arrs), len(shards)

    def body(*refs):
        ins, outs = refs[:n], refs[n:2 * n]
        send, recv, lsem = refs[2 * n:]
        x, y, c = _my_place()
        me = 2 * x + y
        chips = _other_chips(x, y)
        slot = lambda w, chip, cc: outs[w].at[chip, cc] if w < nb else outs[w].at[chip]
        mine = lambda w: slot(w, me, c) if w < nb else ins[w]
        loc = [pltpu.make_async_copy(ins[w], outs[w].at[me], lsem.at[w - nb]) for w in range(nb, n)]
        for cp in loc:
            cp.start()
        sends = []
        for w in range(n):
            for kk, (cx, cy) in enumerate(chips):
                sends.append(_remote(mine(w), slot(w, me, c), send.at[3 * w + kk], recv.at[3 * w + kk], (cx, cy, c)))
        for cp in sends:
            cp.start()
        for w in range(nb):
            for kk, (cx, cy) in enumerate(chips):
                src = 2 * cx + cy
                _remote(mine(w), slot(w, src, c), send.at[3 * w + kk], recv.at[3 * w + kk], (cx, cy, c)).wait_recv()
                fwd = _remote(slot(w, src, c), slot(w, src, c), send.at[3 * (n + w) + kk], recv.at[3 * (n + w) + kk], (x, y, 1 - c))
                fwd.start()
                sends.append(fwd)
        for w in range(n):
            for kk, (cx, cy) in enumerate(chips):
                src = 2 * cx + cy
                if w < nb:
                    _remote(mine(w), slot(w, src, 1 - c), send.at[3 * (n + w) + kk], recv.at[3 * (n + w) + kk],
                            (x, y, 1 - c)).wait_recv()
                else:
                    _remote(ins[w], slot(w, src, c), send.at[3 * w + kk], recv.at[3 * w + kk], (cx, cy, c)).wait_recv()
        for cp in sends:
            cp.wait_send()
        for cp in loc:
            cp.wait()

    return pl.pallas_call(
        body, name=name, in_specs=[ANY] * n, out_specs=[ANY] * n,
        out_shape=[_sds(a.shape, a.dtype) for a in shards] + ([] if small is None else [_sds((N_CHIPS,) + small.shape, small.dtype)]),
        input_output_aliases={w: w for w in range(nb)},
        scratch_shapes=[pltpu.SemaphoreType.DMA((3 * (n + nb),)), pltpu.SemaphoreType.DMA((3 * (n + nb),)),
                        pltpu.SemaphoreType.DMA((max(n - nb, 1),))])(*arrs)


def _place_own(wt, chip, name):
    r, c = wt.shape
    tr = _row_tile(r, c)

    def body(c_ref, w_ref, o_ref):
        o_ref[...] = w_ref[...].astype(BF16)

    return pl.pallas_call(
        body, name=name, out_shape=_sds((N_CHIPS, r, c), BF16),
        grid_spec=pltpu.PrefetchScalarGridSpec(
            num_scalar_prefetch=1, grid=(r // tr,), in_specs=[pl.BlockSpec((tr, c), lambda i, cr: (i, 0))],
            out_specs=pl.BlockSpec((None, tr, c), lambda i, cr: (cr[0], i, 0))),
        compiler_params=_cp("parallel"))(chip, wt)


def _send_sibling_halves(gs, name):
    n = len(gs)

    def body(*refs):
        ins, outs, send, recv = refs[:n], refs[n:2 * n], refs[2 * n], refs[2 * n + 1]
        x, y, c = _my_place()
        cps = [_remote(ins[w].at[:, 1 - c], outs[w], send.at[w], recv.at[w], (x, y, 1 - c)) for w in range(n)]
        for cp in cps:
            cp.start()
        for cp in cps:
            cp.wait()

    return pl.pallas_call(
        body, name=name, in_specs=[ANY] * n, out_specs=[ANY] * n,
        out_shape=[_sds((g.shape[0],) + g.shape[2:], g.dtype) for g in gs],
        scratch_shapes=[pltpu.SemaphoreType.DMA((n,)), pltpu.SemaphoreType.DMA((n,))])(*gs)


def _ride_gather(bufs, send, recv, phase):
    n = len(bufs)
    x, y, c = _my_place()
    me = 2 * x + y
    for w in range(n):
        for kk, (cx, cy) in enumerate(_other_chips(x, y)):
            src = 2 * cx + cy
            out = lambda: _remote(bufs[w].at[me, c], bufs[w].at[me, c], send.at[3 * w + kk], recv.at[3 * w + kk], (cx, cy, c))
            fwd = lambda: _remote(bufs[w].at[src, c], bufs[w].at[src, c], send.at[3 * (n + w) + kk],
                                  recv.at[3 * (n + w) + kk], (x, y, 1 - c))
            if phase == 0:
                out().start()
            elif phase == 1:
                _remote(bufs[w].at[me, c], bufs[w].at[src, c], send.at[3 * w + kk], recv.at[3 * w + kk], (cx, cy, c)).wait_recv()
                fwd().start()
            else:
                _remote(bufs[w].at[me, c], bufs[w].at[src, 1 - c], send.at[3 * (n + w) + kk], recv.at[3 * (n + w) + kk],
                        (x, y, 1 - c)).wait_recv()
                out().wait_send()
                fwd().wait_send()


def _ride_exchange(ps, outs, send, recv, phase):
    x, y, c = _my_place()
    for w in range(len(ps)):
        for kk, (cx, cy) in enumerate(_other_chips(x, y)):
            cp = _remote(ps[w].at[2 * cx + cy], outs[w].at[kk], send.at[3 * w + kk], recv.at[3 * w + kk], (cx, cy, c))
            if phase == 0:
                cp.start()
            else:
                cp.wait()


def _share_sibling(halves):
    n = len(halves)

    def body(*refs):
        outs, send, recv = refs[n:2 * n], refs[2 * n], refs[2 * n + 1]
        x, y, c = _my_place()
        cps = [_remote(outs[w].at[c], outs[w].at[c], send.at[w], recv.at[w], (x, y, 1 - c)) for w in range(n)]
        for cp in cps:
            cp.start()
        for w in range(n):
            cps[w].wait_send()
            _remote(outs[w].at[c], outs[w].at[1 - c], send.at[w], recv.at[w], (x, y, 1 - c)).wait_recv()

    return pl.pallas_call(
        body, name="share_sibling", in_specs=[ANY] * n, out_specs=[ANY] * n,
        out_shape=[_sds(h.shape, h.dtype) for h in halves], input_output_aliases={w: w for w in range(n)},
        scratch_shapes=[pltpu.SemaphoreType.DMA((n,)), pltpu.SemaphoreType.DMA((n,))])(*halves)


def _row_tile(r, c, cap=1 << 20):
    return next(t for t in range(r, 0, -1) if r % t == 0 and (t % 8 == 0 or t == r) and t * c * 4 <= cap)


def _add_pair(g, t, core, name):
    _, _, r, c = g.shape
    tr = _row_tile(r, c)

    def body(c_ref, g_ref, t_ref, o_ref):
        o_ref[...] = (g_ref[...] + t_ref[...]).astype(BF16)

    return pl.pallas_call(
        body, name=name, out_shape=_sds(t.shape, BF16),
        grid_spec=pltpu.PrefetchScalarGridSpec(
            num_scalar_prefetch=1, grid=(N_CHIPS, r // tr),
            in_specs=[pl.BlockSpec((None, None, tr, c), lambda j, i, cr: (j, cr[0], i, 0)),
                      pl.BlockSpec((None, tr, c), lambda j, i, cr: (j, i, 0))],
            out_specs=pl.BlockSpec((None, tr, c), lambda j, i, cr: (j, i, 0))),
        compiler_params=_cp("parallel", "parallel"))(core, g, t)


def _add_chips(p, got, chip, name):
    _, r, c = p.shape
    tr = _row_tile(r, c)

    def body(c_ref, p_ref, g_ref, o_ref):
        o_ref[...] = ((p_ref[...].astype(F32) + g_ref[0].astype(F32)) + g_ref[1].astype(F32)) + g_ref[2].astype(F32)

    return pl.pallas_call(
        body, name=name, out_shape=_sds((2, r, c), F32),
        grid_spec=pltpu.PrefetchScalarGridSpec(
            num_scalar_prefetch=1, grid=(r // tr,),
            in_specs=[pl.BlockSpec((None, tr, c), lambda i, cr: (cr[0], i, 0)), pl.BlockSpec((3, tr, c), lambda i, cr: (0, i, 0))],
            out_specs=pl.BlockSpec((None, tr, c), lambda i, cr: (cr[1], i, 0))),
        compiler_params=_cp("parallel"))(chip, p, got)


SMALL_AR_ROWS = 424


def _allreduce_small(v):
    def body(v_ref, o_ref, gath, send, recv):
        x, y, c = _my_place()
        me = 4 * x + 2 * y + c
        gath[me] = v_ref[...]
        cps = []
        for dd in range(1, 8):
            dx, dy, dc = dd >> 2, (dd >> 1) & 1, dd & 1
            peer = (1 - x if dx else x, 1 - y if dy else y, 1 - c if dc else c)
            cps.append(_remote(v_ref, gath.at[me], send.at[dd - 1], recv.at[dd - 1], peer))
        for cp in cps:
            cp.start()
        for cp in cps:
            cp.wait()
        acc = gath[0]
        for dev in range(1, 8):
            acc = acc + gath[dev]
        o_ref[...] = acc

    vm = pl.BlockSpec(memory_space=pltpu.VMEM)
    return pl.pallas_call(
        body, name="allreduce_small", in_specs=[vm], out_specs=vm, out_shape=_sds(v.shape, F32),
        scratch_shapes=[pltpu.VMEM((8,) + v.shape, F32), pltpu.SemaphoreType.DMA((7,)), pltpu.SemaphoreType.DMA((7,))])(v)


def _adamw(w, g, m, v, name):
    r, c = w.shape
    tr = _row_tile(r, c)

    def body(w_ref, g_ref, m_ref, v_ref, d_ref, m2_ref, v2_ref):
        gv = g_ref[...]
        m2 = ADAM_B1 * m_ref[...] + (1.0 - ADAM_B1) * gv
        v2 = ADAM_B2 * v_ref[...] + (1.0 - ADAM_B2) * jnp.square(gv)
        m_hat = m2 / (1.0 - ADAM_B1 ** ADAM_STEP)
        v_hat = v2 / (1.0 - ADAM_B2 ** ADAM_STEP)
        d_ref[...] = -ADAM_LR * (m_hat / (jnp.sqrt(v_hat) + ADAM_EPS) + ADAM_WD * w_ref[...])
        m2_ref[...] = m2
        v2_ref[...] = v2

    return pl.pallas_call(
        body, name=name, grid=(r // tr,), in_specs=[_rows(tr, c)] * 4, out_specs=[_rows(tr, c)] * 3,
        out_shape=[_sds((r, c), F32)] * 3, compiler_params=_cp("parallel"))(w, g, m, v)


WEIGHT_NAMES = ("meta_tokens", "norm_mix_pre", "norm_mix_post", "norm_ffn_pre", "norm_ffn_post", "w_in", "q_a_norm", "w_uq",
                "kv_a_norm", "w_ukv", "attn_out_norm", "ssm_conv_w", "ssm_conv_b", "ssm_dt_bias", "ssm_A_log", "ssm_D",
                "ssm_norm", "w_out", "w_up", "ffn_conv_w", "ffn_conv_b", "w_down")
SMALL_ADAM_ROWS = 192


def kernel(x, meta_tokens, norm_mix_pre, norm_mix_post, norm_ffn_pre, norm_ffn_post, w_in, q_a_norm, w_uq, kv_a_norm, w_ukv, attn_out_norm, ssm_conv_w, ssm_conv_b, ssm_dt_bias, ssm_A_log, ssm_D, ssm_norm, w_out, w_up, ffn_conv_w, ffn_conv_b, w_down, loss_target, m_meta_tokens, m_norm_mix_pre, m_norm_mix_post, m_norm_ffn_pre, m_norm_ffn_post, m_w_in, m_q_a_norm, m_w_uq, m_kv_a_norm, m_w_ukv, m_attn_out_norm, m_ssm_conv_w, m_ssm_conv_b, m_ssm_dt_bias, m_ssm_A_log, m_ssm_D, m_ssm_norm, m_w_out, m_w_up, m_ffn_conv_w, m_ffn_conv_b, m_w_down, v_meta_tokens, v_norm_mix_pre, v_norm_mix_post, v_norm_ffn_pre, v_norm_ffn_post, v_w_in, v_q_a_norm, v_w_uq, v_kv_a_norm, v_w_ukv, v_attn_out_norm, v_ssm_conv_w, v_ssm_conv_b, v_ssm_dt_bias, v_ssm_A_log, v_ssm_D, v_ssm_norm, v_w_out, v_w_up, v_ffn_conv_w, v_ffn_conv_b, v_w_down):
    args = locals()
    w = {n: args[n] for n in WEIGHT_NAMES}
    mom = {n: args["m_" + n] for n in WEIGHT_NAMES}
    var = {n: args["v_" + n] for n in WEIGHT_NAMES}
    cx, cy, cc = _my_place()
    chip = 2 * cx + cy

    two_d = {n: (shp[0], functools.reduce(lambda a, b: a * b, shp[1:])) for n, shp in BIG}
    names = [n for n, _ in BIG]
    core_i = cc.astype(jnp.int32).reshape(1)
    chip_i = chip.astype(jnp.int32).reshape(1)
    early, late = names[:3], names[3:]
    halves = lambda n, a: a.reshape(N_CHIPS, 2, two_d[n][0] // 2, two_d[n][1])
    bufs = {n: halves(n, _place_own(w[n].reshape(two_d[n]), chip_i, "place_" + n)) for n in names}
    small = _pack_rows([w[n] for n, _ in SMALL_SHARDED], SMALL_AG_ROWS)
    *gathered, small_all = _gather_weights([bufs[n] for n in early], small, "allgather_weights")
    gath = {n: a.reshape((N_CHIPS,) + two_d[n]) for n, a in zip(early, gathered)}
    p = dict(w_in=gath["w_in"].transpose(1, 0, 2).reshape(D, D_IN), w_uq=gath["w_uq"].reshape(QR, MLA_H, DN + DR),
             w_ukv=gath["w_ukv"].reshape(KVR, MLA_H, DN + DV))
    sm_parts = [_unpack_rows(small_all[j], [a * b for _, (a, b) in SMALL_SHARDED]) for j in range(N_CHIPS)]
    for i, (n, shp) in enumerate(SMALL_SHARDED):
        p[n] = jnp.concatenate([sm_parts[j][i].reshape(shp) for j in range(N_CHIPS)], axis=1)
    for n, _ in SMALL_REPL:
        p[n] = w[n]
    meta_full = p.pop("meta_tokens")

    place_i = jnp.stack([chip, cc]).astype(jnp.int32)

    def pair_sums(gd, group):
        gd = dict(gd)
        if "w_in" in gd:
            gd["w_in"] = gd["w_in"].reshape(D, N_CHIPS, D_IN // N_CHIPS).transpose(1, 0, 2)
        gs = [halves(n, gd[n]) for n in group]
        from_sib = _send_sibling_halves(gs, "reduce_sibling_" + group[0])
        return [_add_pair(gg, tt, core_i, "reduce_pair_" + n) for n, gg, tt in zip(group, gs, from_sib)]

    loss_part, gx, gmeta, g, (pairs, got) = _device_step(
        x[0], loss_target[0], meta_full, p, late_bufs=[bufs[n] for n in late], early_reduce=lambda gd: pair_sums(gd, late),
        last_reduce=lambda gd: pair_sums(gd, early))

    small_names = [n for n, _ in SMALL_REPL] + ["ssm_conv_w", "ffn_conv_w"]
    small_sizes = [128] + [sz for _, sz in SMALL_REPL] + [N_META * D, SSM_K * D_XBC, FFN_K * 2 * D_FF]
    order = [n for n, _ in SMALL_REPL]
    sp = _pack_rows([loss_part[0]] + [g[n] for n in order] + [gmeta, g["ssm_conv_w"], g["ffn_conv_w"]], SMALL_AR_ROWS)
    red = _unpack_rows(_allreduce_small(sp), small_sizes)
    loss = red[0][0]
    gfull = {n: red[1 + i].reshape(1, -1) for i, n in enumerate(order)}
    n_r = len(order)
    gfull["meta_tokens"] = lax.dynamic_slice_in_dim(red[1 + n_r].reshape(N_META, D), chip * (D // N_CHIPS), D // N_CHIPS, axis=1)
    gfull["ssm_conv_w"] = lax.dynamic_slice_in_dim(red[2 + n_r].reshape(SSM_K, D_XBC), chip * (D_XBC // N_CHIPS),
                                                   D_XBC // N_CHIPS, axis=1)[None]
    gfull["ffn_conv_w"] = lax.dynamic_slice_in_dim(red[3 + n_r].reshape(FFN_K, 2 * D_FF), chip * (2 * D_FF // N_CHIPS),
                                                   2 * D_FF // N_CHIPS, axis=1)[None]

    mine = [_add_chips(pp, gg, place_i, "reduce_chips_" + n) for n, pp, gg in zip(names, pairs, got)]
    for n, both in zip(names, _share_sibling(mine)):
        gfull[n] = both.reshape(two_d[n])

    delta, new_m, new_v = {}, {}, {}
    for n, shp in BIG:
        outs = _adamw(w[n].reshape(two_d[n]), gfull[n], mom[n].reshape(two_d[n]), var[n].reshape(two_d[n]), "adamw_" + n)
        delta[n], new_m[n], new_v[n] = (o.reshape((1,) + shp) for o in outs)
    snames = order + ["meta_tokens", "ssm_conv_w", "ffn_conv_w"]
    ssizes = [functools.reduce(lambda a, b: a * b, w[n].shape) for n in snames]
    packs = [_pack_rows([d[n] for n in snames], SMALL_ADAM_ROWS) for d in (w, gfull, mom, var)]
    outs = _adamw(*packs, "adamw_small")
    for d, o in zip((delta, new_m, new_v), outs):
        for n, piece in zip(snames, _unpack_rows(o, ssizes)):
            d[n] = piece.reshape(w[n].shape)
    gout = {n: gfull[n].reshape(w[n].shape) for n in WEIGHT_NAMES}
    return (loss, gx[None], *[gout[n] for n in WEIGHT_NAMES], *[delta[n] for n in WEIGHT_NAMES],
            *[new_m[n] for n in WEIGHT_NAMES], *[new_v[n] for n in WEIGHT_NAMES])
```

```python
import functools

import jax
import jax.numpy as jnp
from jax import lax
from jax.experimental import pallas as pl
from jax.experimental.pallas import tpu as pltpu

F32 = jnp.float32
BF16 = jnp.bfloat16

D = 1024
N_META = 16
FRONT = 128
PAD_ROWS = FRONT - N_META
MLA_H = 8
DN, DR, DV = 128, 64, 128
QR, KVR = 384, 256
SOFTMAX_SCALE = (DN + DR) ** -0.5
ROPE_THETA = 10000.0
SSM_H, SSM_P, SSM_G, SSM_N, SSM_K = 16, 64, 2, 128, 4
CHUNK = 128
D_SSM = SSM_H * SSM_P
D_XBC = D_SSM + 2 * SSM_G * SSM_N
GSZ = D_SSM // SSM_G
D_FF = 2816
FFN_K = 3
EPS = 1e-6
IN_SPLITS = (QR, KVR, DR, D_SSM, D_XBC, SSM_H)
D_IN = sum(IN_SPLITS)
LAT_W = 768
IN_P = LAT_W + D_SSM + D_XBC + 128
NEG = -1e30
LOG2E = 1.4426950408889634
LN2 = 0.6931471805599453
Q_SCALE = SOFTMAX_SCALE * LOG2E

ADAM_LR, ADAM_B1, ADAM_B2, ADAM_EPS, ADAM_WD, ADAM_STEP = 0.001, 0.9, 0.999, 1e-08, 0.01, 10

VMEM_LIMIT = 56 * 1024 * 1024
MM_ROWS = (640, 320, 128)
MESH = pl.DeviceIdType.MESH


def _sds(shape, dtype):
    return jax.ShapeDtypeStruct(shape, dtype)


def _cp(*sem):
    return pltpu.CompilerParams(dimension_semantics=sem, vmem_limit_bytes=VMEM_LIMIT)


def _rt(n, cands):
    for c in cands:
        if n % c == 0:
            return c
    raise ValueError((n, cands))


def _full(shape):
    nd = len(shape)
    return pl.BlockSpec(shape, lambda *_: (0,) * nd)


def _rows(tr, c):
    return pl.BlockSpec((tr, c), lambda i: (i, 0))


def _sigmoid(x):
    return 1.0 / (1.0 + jnp.exp(-x))


def _silu(x):
    return x * _sigmoid(x)


def _dsilu(x):
    s = _sigmoid(x)
    return s * (1.0 + x * (1.0 - s))


def _softplus(x):
    return jnp.maximum(x, 0.0) + jnp.log(1.0 + jnp.exp(-jnp.abs(x)))


def _rms(x, g):
    r = lax.rsqrt(jnp.mean(x * x, axis=-1, keepdims=True) + EPS)
    return x * r * g


def _rms_bwd(x, g, dy):
    r = lax.rsqrt(jnp.mean(x * x, axis=-1, keepdims=True) + EPS)
    xh = x * r
    dxh = dy * g
    dx = r * (dxh - xh * jnp.mean(dxh * xh, axis=-1, keepdims=True))
    return dx, jnp.sum(dy * xh, axis=0, keepdims=True)


def _dot(a, b):
    return jnp.dot(a, b, preferred_element_type=F32)


def _dot_nt(a, b):
    return lax.dot_general(a, b, (((1,), (1,)), ((), ())), preferred_element_type=F32)


def _dot_tn(a, b):
    return lax.dot_general(a, b, (((0,), (0,)), ((), ())), preferred_element_type=F32)


def _split3(x):
    hi = x.astype(BF16)
    r = x - hi.astype(F32)
    mid = r.astype(BF16)
    return hi, mid, (r - mid.astype(F32)).astype(BF16)


def _dot_hi(a, b, split="a"):
    if split == "a":
        bb = b.astype(BF16)
        return sum(_dot(t, bb) for t in _split3(a))
    ab = a.astype(BF16)
    return sum(_dot(ab, t) for t in _split3(b))


def _dot_nt_hi(a, b):
    bb = b.astype(BF16)
    return sum(_dot_nt(t, bb) for t in _split3(a))


def _shift_down(x, halo, j):
    xr = pltpu.roll(x, j, axis=0)
    hr = pltpu.roll(halo, j, axis=0)
    row = lax.broadcasted_iota(jnp.int32, (8, x.shape[1]), 0)
    first = jnp.where(row < j, hr, xr[:8])
    return jnp.concatenate([first, xr[8:]], axis=0)


def _shift_up(x, nxt, j):
    t = x.shape[0]
    xr = pltpu.roll(x, t - j, axis=0)
    nr = pltpu.roll(nxt, 8 - j, axis=0)
    row = lax.broadcasted_iota(jnp.int32, (8, x.shape[1]), 0)
    last = jnp.where(row + j >= 8, nr, xr[t - 8:])
    return jnp.concatenate([xr[:t - 8], last], axis=0)


def _acc_rows(ref, val, first):
    @pl.when(first)
    def _():
        ref[...] = val

    @pl.when(jnp.logical_not(first))
    def _():
        ref[...] += val


def _mm_tn(a, b, name, tn=None, trs=(1664, 640, 128), chunked=False):
    r, m = a.shape
    n = b.shape[1]
    tn = n if tn is None else tn
    tr = _rt(r, trs)

    def body(a_ref, b_ref, o_ref):
        part = _dot_tn(a_ref[...].astype(BF16), b_ref[...].astype(BF16))
        _acc_rows(o_ref, part, pl.program_id(1) == 0)

    if chunked:
        out_specs, out_shape = pl.BlockSpec((None, m, tn), lambda j, i: (j, 0, 0)), _sds((n // tn, m, tn), F32)
    else:
        out_specs, out_shape = pl.BlockSpec((m, tn), lambda j, i: (0, j)), _sds((m, n), F32)
    return pl.pallas_call(
        body, name=name, grid=(n // tn, r // tr),
        in_specs=[pl.BlockSpec((tr, m), lambda j, i: (i, 0)), pl.BlockSpec((tr, tn), lambda j, i: (i, j))],
        out_specs=out_specs, out_shape=out_shape, compiler_params=_cp("parallel", "arbitrary"))(a, b)


def _inproj(h0, g, w):
    lp = h0.shape[0]
    tr = _rt(lp, MM_ROWS)
    segs = ((0, LAT_W), (LAT_W, LAT_W + D_SSM), (LAT_W + D_SSM, LAT_W + D_SSM + D_XBC), (IN_P - 128, IN_P))

    def body(h_ref, g_ref, w_ref, hn_ref, lat_ref, z_ref, xbc_ref, dt_ref):
        hn = _rms(h_ref[...], g_ref[...]).astype(BF16)
        hn_ref[...] = hn
        for ref, (a, b) in zip((lat_ref, z_ref, xbc_ref, dt_ref), segs):
            ref[...] = _dot(hn, w_ref[:, a:b])

    return pl.pallas_call(
        body, name="inproj", grid=(lp // tr,), in_specs=[_rows(tr, D), _full((1, D)), _full(w.shape)],
        out_specs=[_rows(tr, D), _rows(tr, LAT_W), _rows(tr, D_SSM), _rows(tr, D_XBC), _rows(tr, 128)],
        out_shape=[_sds((lp, D), BF16), _sds((lp, LAT_W), F32), _sds((lp, D_SSM), F32), _sds((lp, D_XBC), F32),
                   _sds((lp, 128), F32)],
        compiler_params=_cp("parallel"))(h0, g, w)


def _rope(x, cos, sa, sb):
    return x * cos + pltpu.roll(x, 96, axis=1) * sa + pltpu.roll(x, 32, axis=1) * sb


def _rope_t(g, cos, sa, sb):
    return g * cos + pltpu.roll(g * sa, 32, axis=1) + pltpu.roll(g * sb, 96, axis=1)


def _mla_prep(lat, qg, kvg, wq, wkv, cos, sa, sb):
    lp = lat.shape[0]
    tr = _rt(lp, MM_ROWS)

    def body(lat_ref, qg_ref, kvg_ref, wq_ref, wkv_ref, cos_ref, sa_ref, sb_ref, q_ref, k_ref, v_ref, ql_ref, kl_ref):
        lat_v = lat_ref[...]
        ql = _rms(lat_v[:, :QR], qg_ref[...]).astype(BF16)
        kl = _rms(lat_v[:, QR:QR + KVR], kvg_ref[...]).astype(BF16)
        ql_ref[...] = ql
        kl_ref[...] = kl
        cos_v, sa_v, sb_v = cos_ref[...], sa_ref[...], sb_ref[...]
        kpe = _rope(lat_v[:, QR + KVR:LAT_W], cos_v, sa_v, sb_v).astype(BF16)
        for h in range(MLA_H):
            q_ref[h, :, 0:DN] = (_dot(ql, wq_ref[:, h * DN:(h + 1) * DN]) * Q_SCALE).astype(BF16)
            qpe = _dot(ql, wq_ref[:, D + h * 128:D + (h + 1) * 128])
            q_ref[h, :, DN:2 * DN] = (_rope(qpe, cos_v, sa_v, sb_v) * Q_SCALE).astype(BF16)
            k_ref[h, :, 0:DN] = _dot(kl, wkv_ref[:, h * DN:(h + 1) * DN]).astype(BF16)
            k_ref[h, :, DN:2 * DN] = kpe
            v_ref[h] = _dot(kl, wkv_ref[:, D + h * DV:D + (h + 1) * DV]).astype(BF16)

    hb = lambda w: pl.BlockSpec((MLA_H, tr, w), lambda i: (0, i, 0))
    return pl.pallas_call(
        body, name="mla_prep", grid=(lp // tr,),
        in_specs=[_rows(tr, LAT_W), _full((1, QR)), _full((1, KVR)), _full(wq.shape), _full(wkv.shape),
                  _rows(tr, 128), _rows(tr, 128), _rows(tr, 128)],
        out_specs=[hb(256), hb(256), hb(128), _rows(tr, QR), _rows(tr, KVR)],
        out_shape=[_sds((MLA_H, lp, 256), BF16), _sds((MLA_H, lp, 256), BF16), _sds((MLA_H, lp, 128), BF16),
                   _sds((lp, QR), BF16), _sds((lp, KVR), BF16)],
        compiler_params=_cp("parallel"))(lat, qg, kvg, wq, wkv, cos, sa, sb)


def _attn_mask(r0, c0, tq, tk, transposed=False):
    if transposed:
        kk = c0 + lax.broadcasted_iota(jnp.int32, (tk, tq), 0)
        qq = r0 + lax.broadcasted_iota(jnp.int32, (tk, tq), 1)
    else:
        qq = r0 + lax.broadcasted_iota(jnp.int32, (tq, tk), 0)
        kk = c0 + lax.broadcasted_iota(jnp.int32, (tq, tk), 1)
    return jnp.logical_and(kk <= qq, kk >= PAD_ROWS)


def _attn_fwd(q, k, v, ride=()):
    lp = q.shape[1]
    t = _rt(lp, (640, 128))
    nq = lp // t

    hp = 2

    nr = len(ride)
    steps = (MLA_H // hp) * nq

    def body(q_ref, k_ref, v_ref, *rest):
        o_ref, lse_ref = rest[nr:nr + 2]
        bufs, sems = rest[nr + 2:2 * nr + 2], rest[2 * nr + 2:]
        qi = pl.program_id(1)
        step = pl.program_id(0) * nq + qi
        if nr:
            pl.when(step == 0)(lambda: _ride_gather(bufs, *sems, 0))
            pl.when(step == steps // 2)(lambda: _ride_gather(bufs, *sems, 1))
        qv = [q_ref[a] for a in range(hp)]

        def tile(kj, carries, bias=None):
            kv_rows = pl.ds(pl.multiple_of(kj * t, t), t)
            out = []
            for a in range(hp):
                m, l, acc = carries[a]
                kk = k_ref[a, kv_rows, :]
                vv = v_ref[a, kv_rows, :]
                s = _dot_nt(qv[a], kk)
                if bias is not None:
                    s = s + bias
                m_new = jnp.maximum(m, jnp.max(s, axis=-1, keepdims=True))
                alpha = jnp.exp2(m - m_new)
                p = jnp.exp2(s - m_new)
                l = alpha * l + jnp.sum(p, axis=-1, keepdims=True)
                acc = alpha * acc + _dot(p.astype(BF16), vv)
                out.append((m_new, l, acc))
            return tuple(out)

        key = lax.broadcasted_iota(jnp.int32, (1, t), 1)
        pad_bias = jnp.where(jnp.logical_and(key >= PAD_ROWS, qi > 0), 0.0, NEG)
        init = tuple((jnp.full((t, 1), NEG, F32), jnp.zeros((t, 1), F32), jnp.zeros((t, DV), F32)) for _ in range(hp))
        carries = tile(0, init, pad_bias)
        carries = lax.fori_loop(1, qi, lambda kj, c: tile(kj, c), carries)
        carries = tile(qi, carries, jnp.where(_attn_mask(qi * t, qi * t, t, t), 0.0, NEG))
        for a in range(hp):
            m, l, acc = carries[a]
            o_ref[:, a * DV:(a + 1) * DV] = acc / l
            lse_ref[a] = jnp.broadcast_to(m + jnp.log(l) * LOG2E, (t, 128)).T[:8]
        if nr:
            pl.when(step == steps - 1)(lambda: _ride_gather(bufs, *sems, 2))

    sems = [pltpu.SemaphoreType.DMA((6 * nr,)), pltpu.SemaphoreType.DMA((6 * nr,))] if nr else []
    outs = pl.pallas_call(
        body, name="attn_fwd", grid=(MLA_H // hp, nq),
        in_specs=[pl.BlockSpec((hp, t, 256), lambda h, i: (h, i, 0)), pl.BlockSpec((hp, lp, 256), lambda h, i: (h, 0, 0)),
                  pl.BlockSpec((hp, lp, 128), lambda h, i: (h, 0, 0))] + [ANY] * nr,
        out_specs=[pl.BlockSpec((t, hp * DV), lambda h, i: (i, h)), pl.BlockSpec((hp, 8, t), lambda h, i: (h, 0, i))] + [ANY] * nr,
        out_shape=[_sds((lp, MLA_H * DV), F32), _sds((MLA_H, 8, lp), F32)] + [_sds(b.shape, b.dtype) for b in ride],
        input_output_aliases={3 + w: 2 + w for w in range(nr)}, scratch_shapes=sems,
        compiler_params=_cp("arbitrary", "arbitrary"))(q, k, v, *ride)
    return outs[0], outs[1], list(outs[2:])


def _mixout_bwd(mix, g_post, dh1, o, g_ao, w_out, ride=()):
    lp = o.shape[0]
    tr = _rt(lp, MM_ROWS)
    nr = len(ride)
    steps = lp // tr

    def body(mix_ref, gp_ref, dh_ref, o_ref, g_ref, w_ref, *rest):
        gs = rest[:nr]
        dmix_ref, dssm_ref, do_ref, dgp_ref, dg_ref, dl_ref = rest[nr:nr + 6]
        from_sib, sems = rest[nr + 6:2 * nr + 6], rest[2 * nr + 6:]
        i = pl.program_id(0)
        if nr:
            pl.when(i == 0)(lambda: _ride_sibling(gs, from_sib, *sems, 0))
        grow = i * tr + lax.broadcasted_iota(jnp.int32, (tr, D), 0)
        dmix, dgp = _rms_bwd(mix_ref[...], gp_ref[...], jnp.where(grow >= PAD_ROWS, dh_ref[...], 0.0))
        dmix = dmix.astype(BF16)
        dmix_ref[...] = dmix
        _acc_rows(dgp_ref, dgp, i == 0)
        dssm_ref[...] = _dot_nt(dmix, w_ref[D:, :])
        ov = o_ref[...]
        do, dg = _rms_bwd(ov, g_ref[...], _dot_nt(dmix, w_ref[:D, :]))
        do_ref[...] = do
        _acc_rows(dg_ref, dg, i == 0)
        prod = do * ov
        lane = lax.broadcasted_iota(jnp.int32, (1, 128), 1)
        cols = jnp.zeros((tr, 128), F32)
        for h in range(MLA_H):
            cols = cols + jnp.sum(prod[:, h * DV:(h + 1) * DV], axis=-1, keepdims=True) * (lane == h).astype(F32)
        dl_ref[...] = cols.T[:MLA_H]
        if nr:
            pl.when(i == steps - 1)(lambda: _ride_sibling(gs, from_sib, *sems, 1))

    sems = [pltpu.SemaphoreType.DMA((nr,)), pltpu.SemaphoreType.DMA((nr,))] if nr else []
    outs = pl.pallas_call(
        body, name="mixout_bwd", grid=(steps,),
        in_specs=[_rows(tr, D), _full((1, D)), _rows(tr, D), _rows(tr, D), _full((1, D)), _full(w_out.shape)] + [ANY] * nr,
        out_specs=[_rows(tr, D), _rows(tr, D), _rows(tr, D), _full((1, D)), _full((1, D)),
                   pl.BlockSpec((MLA_H, tr), lambda i: (0, i))] + [ANY] * nr,
        out_shape=[_sds((lp, D), BF16), _sds((lp, D), F32), _sds((lp, D), F32), _sds((1, D), F32), _sds((1, D), F32),
                   _sds((MLA_H, lp), F32)] + [_sds((g.shape[0],) + g.shape[2:], g.dtype) for g in ride],
        scratch_shapes=sems, compiler_params=_cp("arbitrary"))(mix, g_post, dh1, o, g_ao, w_out, *ride)
    return tuple(outs[:6]) + (list(outs[6:]),)


def _attn_bwd(q, k, v, do, lse_row, delta_row, ride=()):
    lp = q.shape[1]
    t = _rt(lp, (640, 128))
    nq = lp // t

    nr = len(ride)

    def body(q_ref, k_ref, v_ref, do_ref, lse_ref, dl_ref, *rest):
        ps = rest[:nr]
        dq_ref, dk_ref, dv_ref = rest[nr:nr + 3]
        got, sems = rest[nr + 3:2 * nr + 3], rest[2 * nr + 3:]
        kj = pl.program_id(1)
        step = pl.program_id(0) * nq + kj
        if nr:
            pl.when(step == 0)(lambda: _ride_exchange(ps, got, *sems, 0))
        kk = k_ref[0]
        vv = v_ref[0]

        @pl.when(kj == 0)
        def _():
            dq_ref[...] = jnp.zeros_like(dq_ref)

        def tile(qi, carry, masked):
            dk, dv = carry
            q_rows = pl.ds(pl.multiple_of(qi * t, t), t)
            qv = q_ref[0, q_rows, :]
            dob = do_ref[q_rows, :].astype(BF16)
            st = _dot_nt(kk, qv)
            if masked:
                st = jnp.where(_attn_mask(qi * t, kj * t, t, t, transposed=True), st, NEG)
            pt = jnp.exp2(st - lse_ref[0, qi])
            dpt = _dot_nt(vv, dob)
            dst = (pt * (dpt - dl_ref[0, qi])).astype(BF16)
            dv = dv + _dot(pt.astype(BF16), dob)
            dk = dk + _dot(dst, qv)
            dq_ref[0, q_rows, :] += _dot_tn(dst, kk)
            return dk, dv

        carry = tile(kj, (jnp.zeros((t, 256), F32), jnp.zeros((t, DV), F32)), True)
        split = jnp.where(kj == 0, nq, kj + 1)
        carry = lax.fori_loop(kj + 1, split, lambda qi, c: tile(qi, c, True), carry)
        dk, dv = lax.fori_loop(split, nq, lambda qi, c: tile(qi, c, False), carry)
        dk_ref[0] = dk * LN2
        dv_ref[0] = dv
        if nr:
            pl.when(step == MLA_H * nq - 1)(lambda: _ride_exchange(ps, got, *sems, 1))

    stat = pl.BlockSpec((1, nq, 1, t), lambda h, j: (h, 0, 0, 0))
    sems = [pltpu.SemaphoreType.DMA((3 * nr,)), pltpu.SemaphoreType.DMA((3 * nr,))] if nr else []
    outs = pl.pallas_call(
        body, name="attn_bwd", grid=(MLA_H, nq),
        in_specs=[pl.BlockSpec((1, lp, 256), lambda h, j: (h, 0, 0)), pl.BlockSpec((1, t, 256), lambda h, j: (h, j, 0)),
                  pl.BlockSpec((1, t, 128), lambda h, j: (h, j, 0)), pl.BlockSpec((lp, DV), lambda h, j: (0, h)), stat, stat]
        + [ANY] * nr,
        out_specs=[pl.BlockSpec((1, lp, 256), lambda h, j: (h, 0, 0)), pl.BlockSpec((1, t, 256), lambda h, j: (h, j, 0)),
                   pl.BlockSpec((1, t, 128), lambda h, j: (h, j, 0))] + [ANY] * nr,
        out_shape=[_sds((MLA_H, lp, 256), F32), _sds((MLA_H, lp, 256), F32), _sds((MLA_H, lp, 128), F32)]
        + [_sds((3,) + p.shape[1:], p.dtype) for p in ride],
        scratch_shapes=sems, compiler_params=_cp("arbitrary", "arbitrary"))(q, k, v, do, lse_row, delta_row, *ride)
    return outs[0], outs[1], outs[2], list(outs[3:])


def _ssd_consts():
    ri = lax.broadcasted_iota(jnp.int32, (CHUNK, CHUNK), 0)
    ci = lax.broadcasted_iota(jnp.int32, (CHUNK, CHUNK), 1)
    expand = (lax.broadcasted_iota(jnp.int32, (128, D_SSM), 0)
              == lax.broadcasted_iota(jnp.int32, (128, D_SSM), 1) // SSM_P).astype(F32)
    return ri, ci, expand


def _ssd_chunk(c, x_ref, xh_ref, dt_ref, dtT_ref, cw_ref, cb_ref, dtb_ref, dtbT_ref, al_ref, alT_ref):
    ri, ci, expand = _ssd_consts()
    x = x_ref[...]
    halo = jnp.where(c > 0, xh_ref[...], 0.0)
    sh = [x] + [_shift_down(x, halo, j) for j in range(1, SSM_K)]
    cv = cb_ref[...]
    for kk in range(SSM_K):
        cv = cv + cw_ref[kk:kk + 1, :] * sh[SSM_K - 1 - kk]
    xa = _silu(cv)
    grow = c * CHUNK + ri
    gcol = c * CHUNK + lax.broadcasted_iota(jnp.int32, (SSM_H, CHUNK), 1)
    sp = dt_ref[...] + dtb_ref[...]
    spT = dtT_ref[...] + dtbT_ref[...]
    dtc = jnp.where(grow >= PAD_ROWS, _softplus(sp), 0.0)
    dtr = jnp.where(gcol >= PAD_ROWS, _softplus(spT), 0.0)
    arow = -jnp.exp(al_ref[...])
    acolT = -jnp.exp(alT_ref[...])
    ltri = (ci <= ri).astype(F32)
    acs = _dot_hi(ltri, dtc * arow, split="b")
    acsT = _dot_hi(dtr * acolT, (ri <= ci).astype(F32))
    return dict(x=x, sh=sh, cv=cv, xa=xa, sp=sp, dtc=dtc, arow=arow, acs=acs, acsT=acsT, ri=ri, ci=ci, expand=expand,
                grow=grow)


def _ssd_mats(k, s_prev):
    xa, acs, acsT, expand, ri, ci = k["xa"], k["acs"], k["acsT"], k["expand"], k["ri"], k["ci"]
    xs = xa[:, :D_SSM]
    dt_e = _dot_hi(k["dtc"], expand)
    acs_e = _dot_hi(acs, expand)
    last_e = acs_e[CHUNK - 1:CHUNK, :]
    ea = jnp.exp(acs_e)
    f = jnp.exp(last_e - acs_e)
    cd = jnp.exp(last_e)
    xdt = xs * dt_e
    bm = [xa[:, D_SSM + g * SSM_N:D_SSM + (g + 1) * SSM_N] for g in range(SSM_G)]
    cm = [xa[:, D_SSM + (SSM_G + g) * SSM_N:D_SSM + (SSM_G + g + 1) * SSM_N] for g in range(SSM_G)]
    bmb = [b.astype(BF16) for b in bm]
    cmb = [cc.astype(BF16) for cc in cm]
    cb = [_dot_nt(cmb[g], bmb[g]) for g in range(SSM_G)]
    lam, mm = [], []
    causal = jnp.where(ci <= ri, 0.0, NEG)
    for h in range(SSM_H):
        lam_h = jnp.exp((acs[:, h:h + 1] - acsT[h:h + 1, :]) + causal)
        lam.append(lam_h)
        mm.append(cb[h // (SSM_H // SSM_G)] * lam_h)
    lo = lax.broadcasted_iota(jnp.int32, (CHUNK, 128), 1) < SSM_P
    xdt_h = []
    for h in range(SSM_H):
        pair = xdt[:, (h // 2) * 128:(h // 2 + 1) * 128]
        xdt_h.append(jnp.where(lo if h % 2 == 0 else jnp.logical_not(lo), pair, 0.0).astype(BF16))
    ydiag = jnp.concatenate(
        [_dot(mm[2 * j].astype(BF16), xdt_h[2 * j]) + _dot(mm[2 * j + 1].astype(BF16), xdt_h[2 * j + 1])
         for j in range(SSM_H // 2)], axis=1)
    t_off = [_dot(cmb[g], s_prev[g].astype(BF16)) for g in range(SSM_G)]
    yoff = jnp.concatenate(t_off, axis=1) * ea
    return dict(xs=xs, dt_e=dt_e, acs_e=acs_e, ea=ea, f=f, cd=cd, xdt=xdt, bm=bm, cm=cm, bmb=bmb, cmb=cmb, cb=cb, lam=lam,
                mm=mm, lo=lo, xdt_h=xdt_h, ydiag=ydiag, t_off=t_off, yoff=yoff)


def _ssd_specs(nc, rev):
    ix = (lambda i: nc - 1 - i) if rev else (lambda i: i)
    return [
        pl.BlockSpec((CHUNK, D_XBC), lambda i: (ix(i), 0)),
        pl.BlockSpec((8, D_XBC), lambda i: (jnp.maximum(ix(i) * (CHUNK // 8) - 1, 0), 0)),
        pl.BlockSpec((CHUNK, D_SSM), lambda i: (ix(i), 0)),
        pl.BlockSpec((CHUNK, 128), lambda i: (ix(i), 0)),
        pl.BlockSpec((SSM_H, CHUNK), lambda i: (0, ix(i))),
        _full((8, D_XBC)), _full((1, D_XBC)), _full((1, 128)), _full((SSM_H, 1)), _full((1, 128)), _full((SSM_H, 1)),
        _full((1, D_SSM)), _full((1, D_SSM)),
    ]


def _ssd_fwd(xbc, z, dtr, dtrT, cw, cb, dtb, dtbT, alog, alogT, d_e, ng):
    lp = xbc.shape[0]
    nc = lp // CHUNK

    def body(x_ref, xh_ref, z_ref, dt_ref, dtT_ref, cw_ref, cb_ref, dtb_ref, dtbT_ref, al_ref, alT_ref, de_ref, ng_ref,
             y_ref, st_ref, s_scr):
        c = pl.program_id(0)

        @pl.when(c == 0)
        def _():
            s_scr[...] = jnp.zeros_like(s_scr)

        k = _ssd_chunk(c, x_ref, xh_ref, dt_ref, dtT_ref, cw_ref, cb_ref, dtb_ref, dtbT_ref, al_ref, alT_ref)
        s_prev = [s_scr[g] for g in range(SSM_G)]
        st_ref[0] = s_scr[...]
        m = _ssd_mats(k, s_prev)
        xd = (m["xdt"] * m["f"]).astype(BF16)
        for g in range(SSM_G):
            sl = slice(g * GSZ, (g + 1) * GSZ)
            s_scr[g] = m["cd"][:, sl] * s_prev[g] + _dot(m["bm"][g].T.astype(BF16), xd[:, sl])
        y = m["ydiag"] + m["yoff"] + de_ref[...] * m["xs"]
        u = y * _silu(z_ref[...])
        outs = []
        for g in range(SSM_G):
            ug = u[:, g * GSZ:(g + 1) * GSZ]
            outs.append(ug * lax.rsqrt(jnp.mean(ug * ug, axis=-1, keepdims=True) + EPS))
        y_ref[...] = jnp.concatenate(outs, axis=1) * ng_ref[...]

    return pl.pallas_call(
        body, name="ssd_fwd", grid=(nc,), in_specs=_ssd_specs(nc, False),
        out_specs=[_rows(CHUNK, D_SSM), pl.BlockSpec((1, SSM_G, SSM_N, GSZ), lambda i: (i, 0, 0, 0))],
        out_shape=[_sds((lp, D_SSM), F32), _sds((nc, SSM_G, SSM_N, GSZ), F32)],
        scratch_shapes=[pltpu.VMEM((SSM_G, SSM_N, GSZ), F32)],
        compiler_params=_cp("arbitrary"))(xbc, xbc, z, dtr, dtrT, cw, cb, dtb, dtbT, alog, alogT, d_e, ng)


def _ssd_bwd(dssm, xbc, z, dtr, dtrT, st, cw, cb, dtb, dtbT, alog, alogT, d_e, ng):
    lp = xbc.shape[0]
    nc = lp // CHUNK
    hpg = SSM_H // SSM_G

    def body(dy_ref, x_ref, xh_ref, z_ref, dt_ref, dtT_ref, st_ref, cw_ref, cb_ref, dtb_ref, dtbT_ref, al_ref, alT_ref,
             de_ref, ng_ref, dz_ref, dx_ref, ddt_ref, dcw_ref, dcb_ref, ddtb_ref, dal_ref, dd_ref, dng_ref, ds_scr, nx_scr):
        i = pl.program_id(0)
        c = nc - 1 - i
        first = i == 0

        @pl.when(first)
        def _():
            ds_scr[...] = jnp.zeros_like(ds_scr)
            nx_scr[...] = jnp.zeros_like(nx_scr)

        k = _ssd_chunk(c, x_ref, xh_ref, dt_ref, dtT_ref, cw_ref, cb_ref, dtb_ref, dtbT_ref, al_ref, alT_ref)
        s_prev = [st_ref[0, g] for g in range(SSM_G)]
        m = _ssd_mats(k, s_prev)
        ri, ci, expand = k["ri"], k["ci"], k["expand"]
        xs, acs, acsT = m["xs"], k["acs"], k["acsT"]
        zv = z_ref[...]
        dout = dy_ref[...]
        ngv = ng_ref[...]
        y = m["ydiag"] + m["yoff"] + de_ref[...] * xs
        sz = _silu(zv)
        u = y * sz
        du_parts, dng_parts = [], []
        for g in range(SSM_G):
            sl = slice(g * GSZ, (g + 1) * GSZ)
            dug, dngg = _rms_bwd(u[:, sl], ngv[:, sl], dout[:, sl])
            du_parts.append(dug)
            dng_parts.append(dngg)
        du = jnp.concatenate(du_parts, axis=1)
        _acc_rows(dng_ref, jnp.concatenate(dng_parts, axis=1), first)
        dy = du * sz
        dz_ref[...] = du * y * _dsilu(zv)
        dd_e = jnp.sum(dy * xs, axis=0, keepdims=True)
        _acc_rows(dd_ref, _dot_nt_hi(dd_e, expand), first)
        dxs = de_ref[...] * dy
        dacs_e = dy * m["yoff"]
        dtg = (dy * m["ea"]).astype(BF16)
        dxdt = jnp.zeros_like(xs)
        dlast_e = []
        db, dc, ds_prev = [], [], []
        xd = m["xdt"] * m["f"]
        dxd_all = []
        for g in range(SSM_G):
            sl = slice(g * GSZ, (g + 1) * GSZ)
            dsg = ds_scr[g]
            spb = s_prev[g].astype(BF16)
            dc.append(_dot_nt(dtg[:, sl], spb))
            dsp = _dot(m["cm"][g].T.astype(BF16), dtg[:, sl]) + m["cd"][:, sl] * dsg
            ds_prev.append(dsp)
            dlast_e.append(jnp.sum(dsg * s_prev[g], axis=0, keepdims=True) * m["cd"][:, sl])
            dsb = dsg.astype(BF16)
            db.append(_dot_nt(xd[:, sl].astype(BF16), dsb))
            dxd_all.append(_dot(m["bmb"][g], dsb))
        dxd = jnp.concatenate(dxd_all, axis=1)
        dxdt = dxd * m["f"]
        dff = dxd * xd
        dacs_e = dacs_e - dff
        dlast_row = jnp.concatenate(dlast_e, axis=1) + jnp.sum(dff, axis=0, keepdims=True)
        dacs = jnp.zeros((CHUNK, 128), F32)
        dacs_t = jnp.zeros((CHUNK, CHUNK), F32)
        lane = lax.broadcasted_iota(jnp.int32, (1, 128), 1)
        dgs = [jnp.zeros((CHUNK, CHUNK), F32) for _ in range(SSM_G)]
        dxdt_pairs = []
        for h in range(SSM_H):
            g = h // hpg
            pr = slice((h // 2) * 128, (h // 2 + 1) * 128)
            lo_h = m["lo"] if h % 2 == 0 else jnp.logical_not(m["lo"])
            dyp = jnp.where(lo_h, dy[:, pr], 0.0).astype(BF16)
            dm = _dot_nt(dyp, m["xdt"][:, pr].astype(BF16))
            dgs[g] = dgs[g] + dm * m["lam"][h]
            w_h = dm * m["mm"][h]
            dacs = dacs + jnp.sum(w_h, axis=1, keepdims=True) * (lane == h).astype(F32)
            dacs_t = dacs_t + jnp.where(ri == h, jnp.sum(w_h, axis=0, keepdims=True), 0.0)
            part = _dot_tn(m["mm"][h].astype(BF16), dyp)
            if h % 2 == 0:
                dxdt_pairs.append(part)
            else:
                dxdt_pairs[-1] = dxdt_pairs[-1] + part
        dxdt = dxdt + jnp.concatenate(dxdt_pairs, axis=1)
        for g in range(SSM_G):
            dgb = dgs[g].astype(BF16)
            dc[g] = dc[g] + _dot(dgb, m["bmb"][g])
            db[g] = db[g] + _dot_tn(dgb, m["cmb"][g])
        dacs = dacs - dacs_t.T + _dot_nt_hi(dacs_e, expand)
        dlast = _dot_nt_hi(dlast_row, expand)
        dacs = dacs + jnp.where(ri == CHUNK - 1, dlast, 0.0)
        dxs = dxs + dxdt * m["dt_e"]
        ddt = _dot_nt_hi(dxdt * xs, expand)
        da = _dot_hi((ri <= ci).astype(F32), dacs, split="b")
        ddt = ddt + da * k["arow"]
        dA = jnp.sum(da * k["dtc"], axis=0, keepdims=True)
        _acc_rows(dal_ref, dA * k["arow"], first)
        ddtr = jnp.where(k["grow"] >= PAD_ROWS, ddt * _sigmoid(k["sp"]), 0.0)
        ddt_ref[...] = ddtr
        _acc_rows(ddtb_ref, jnp.sum(ddtr, axis=0, keepdims=True), first)
        for g in range(SSM_G):
            ds_scr[g] = ds_prev[g]
        dxa = jnp.concatenate([dxs] + db + dc, axis=1)
        dcv = dxa * _dsilu(k["cv"])
        _acc_rows(dcb_ref, jnp.sum(dcv, axis=0, keepdims=True), first)
        dcw_rows = [jnp.sum(dcv * k["sh"][SSM_K - 1 - kk], axis=0, keepdims=True) for kk in range(SSM_K)]
        dcw_rows.append(jnp.zeros((8 - SSM_K, D_XBC), F32))
        _acc_rows(dcw_ref, jnp.concatenate(dcw_rows, axis=0), first)
        nxt = nx_scr[...]
        dx = cw_ref[SSM_K - 1:SSM_K, :] * dcv
        for j in range(1, SSM_K):
            dx = dx + cw_ref[SSM_K - 1 - j:SSM_K - j, :] * _shift_up(dcv, nxt, j)
        grow_x = c * CHUNK + lax.broadcasted_iota(jnp.int32, (CHUNK, D_XBC), 0)
        dx_ref[...] = jnp.where(grow_x >= PAD_ROWS, dx, 0.0)
        nx_scr[...] = dcv[:8]

    specs = _ssd_specs(nc, True)
    in_specs = [pl.BlockSpec((CHUNK, D_SSM), lambda i: (nc - 1 - i, 0))] + specs[:5] + [
        pl.BlockSpec((1, SSM_G, SSM_N, GSZ), lambda i: (nc - 1 - i, 0, 0, 0))] + specs[5:]
    rv = lambda w: pl.BlockSpec((CHUNK, w), lambda i: (nc - 1 - i, 0))
    return pl.pallas_call(
        body, name="ssd_bwd", grid=(nc,), in_specs=in_specs,
        out_specs=[rv(D_SSM), rv(D_XBC), rv(128), _full((8, D_XBC)), _full((1, D_XBC)), _full((1, 128)), _full((1, 128)),
                   _full((1, 128)), _full((1, D_SSM))],
        out_shape=[_sds((lp, D_SSM), F32), _sds((lp, D_XBC), F32), _sds((lp, 128), F32), _sds((8, D_XBC), F32),
                   _sds((1, D_XBC), F32), _sds((1, 128), F32), _sds((1, 128), F32), _sds((1, 128), F32), _sds((1, D_SSM), F32)],
        scratch_shapes=[pltpu.VMEM((SSM_G, SSM_N, GSZ), F32), pltpu.VMEM((8, D_XBC), F32)],
        compiler_params=_cp("arbitrary"))(dssm, xbc, xbc, z, dtr, dtrT, st, cw, cb, dtb, dtbT, alog, alogT, d_e, ng)


def _mixout_fwd(o, ssm, h0, g_ao, g_post, w):
    lp = o.shape[0]
    tr = _rt(lp, MM_ROWS)

    def body(o_ref, s_ref, h_ref, ga_ref, gp_ref, w_ref, mi_ref, mix_ref, h1_ref):
        mixin = jnp.concatenate([_rms(o_ref[...], ga_ref[...]), s_ref[...]], axis=1).astype(BF16)
        mi_ref[...] = mixin
        mix = _dot(mixin, w_ref[...])
        mix_ref[...] = mix
        grow = pl.program_id(0) * tr + lax.broadcasted_iota(jnp.int32, (tr, D), 0)
        h1_ref[...] = h_ref[...] + jnp.where(grow >= PAD_ROWS, _rms(mix, gp_ref[...]), 0.0)

    return pl.pallas_call(
        body, name="mixout_fwd", grid=(lp // tr,),
        in_specs=[_rows(tr, D), _rows(tr, D), _rows(tr, D), _full((1, D)), _full((1, D)), _full(w.shape)],
        out_specs=[_rows(tr, 2 * D), _rows(tr, D), _rows(tr, D)],
        out_shape=[_sds((lp, 2 * D), BF16), _sds((lp, D), F32), _sds((lp, D), F32)],
        compiler_params=_cp("parallel"))(o, ssm, h0, g_ao, g_post, w)


def _ffn_up(h1, g, w):
    lp = h1.shape[0]
    tr = _rt(lp, MM_ROWS)
    tn = D_FF // 2

    def body(h_ref, g_ref, w_ref, hn_ref, u_ref):
        hn = _rms(h_ref[...], g_ref[...]).astype(BF16)
        hn_ref[...] = hn
        u_ref[...] = _dot(hn, w_ref[...]).astype(BF16)

    return pl.pallas_call(
        body, name="ffn_up", grid=(lp // tr, 2 * D_FF // tn),
        in_specs=[pl.BlockSpec((tr, D), lambda i, j: (i, 0)), _full((1, D)), pl.BlockSpec((None, D, tn), lambda i, j: (j, 0, 0))],
        out_specs=[pl.BlockSpec((tr, D), lambda i, j: (i, 0)), pl.BlockSpec((tr, tn), lambda i, j: (i, j))],
        out_shape=[_sds((lp, D), BF16), _sds((lp, 2 * D_FF), BF16)],
        compiler_params=_cp("parallel", "arbitrary"))(h1, g, w)


def _ffn_in_bwd(du, w4, h1, g, dh2):
    lp = du.shape[0]
    nch, _, tn = w4.shape
    tr = _rt(lp, (320, 128))

    def body(du_ref, w_ref, h_ref, g_ref, r_ref, o_ref, dg_ref):
        acc = _dot_nt(du_ref[:, 0:tn], w_ref[0])
        for j in range(1, nch):
            acc = acc + _dot_nt(du_ref[:, j * tn:(j + 1) * tn], w_ref[j])
        dx, dg = _rms_bwd(h_ref[...], g_ref[...], acc)
        o_ref[...] = dx + r_ref[...]
        _acc_rows(dg_ref, dg, pl.program_id(0) == 0)

    return pl.pallas_call(
        body, name="ffn_in_bwd", grid=(lp // tr,),
        in_specs=[_rows(tr, nch * tn), _full(w4.shape), _rows(tr, D), _full((1, D)), _rows(tr, D)],
        out_specs=[_rows(tr, D), _full((1, D))], out_shape=[_sds((lp, D), F32), _sds((1, D), F32)],
        compiler_params=_cp("arbitrary"))(du, w4, h1, g, dh2)


FFN_CB = 256


def _ffn_gate(u, cw, cb):
    lp = u.shape[0]
    tr = _rt(lp, (320, 128))

    def body(u_ref, uh_ref, cw_ref, cb_ref, uc_ref, a_ref):
        i = pl.program_id(0)
        for j in range(D_FF // FFN_CB):
            halves = []
            for off in (0, D_FF):
                sl = slice(off + j * FFN_CB, off + (j + 1) * FFN_CB)
                x = u_ref[:, sl].astype(F32)
                halo = jnp.where(i > 0, uh_ref[8:16, sl].astype(F32), 0.0)
                cv = cb_ref[:, sl] + cw_ref[FFN_K - 1:FFN_K, sl] * x
                for s in range(1, FFN_K):
                    cv = cv + cw_ref[FFN_K - 1 - s:FFN_K - s, sl] * _shift_down(x, halo, s)
                uc_ref[:, sl] = cv.astype(BF16)
                halves.append(cv)
            a_ref[:, j * FFN_CB:(j + 1) * FFN_CB] = (_silu(halves[0]) * halves[1]).astype(BF16)

    return pl.pallas_call(
        body, name="ffn_gate", grid=(lp // tr,),
        in_specs=[_rows(tr, 2 * D_FF), pl.BlockSpec((16, 2 * D_FF), lambda i: (jnp.maximum(i * (tr // 16) - 1, 0), 0)),
                  _full((8, 2 * D_FF)), _full((1, 2 * D_FF))],
        out_specs=[_rows(tr, 2 * D_FF), _rows(tr, D_FF)], out_shape=[_sds((lp, 2 * D_FF), BF16), _sds((lp, D_FF), BF16)],
        compiler_params=_cp("parallel"))(u, u, cw, cb)


def _ffn_down(a, w, h1, tgt, g_post):
    lp = a.shape[0]
    tr = _rt(lp, MM_ROWS)
    nb = tr // FRONT

    def body(a_ref, w_ref, h_ref, *rest):
        t_refs, (g_ref, dh2_ref, dd_ref, dg_ref, loss_ref) = rest[:nb], rest[nb:]
        i = pl.program_id(0)
        d = _dot(a_ref[...], w_ref[...])
        gv = g_ref[...]
        h2 = h_ref[...] + _rms(d, gv)
        grow = i * tr + lax.broadcasted_iota(jnp.int32, (tr, D), 0)
        tgt_v = jnp.concatenate([r[...] for r in t_refs], axis=0)
        err = jnp.where(grow >= FRONT, h2 - tgt_v, 0.0)
        dh2 = err * (1.0 / D)
        dh2_ref[...] = dh2
        dd, dg = _rms_bwd(d, gv, dh2)
        dd_ref[...] = dd.astype(BF16)
        _acc_rows(dg_ref, dg, i == 0)
        part = 0.5 * jnp.sum(jnp.sum(err * err, axis=1, keepdims=True), axis=0, keepdims=True) * (1.0 / D)
        _acc_rows(loss_ref, jnp.broadcast_to(part, (8, 128)), i == 0)

    return pl.pallas_call(
        body, name="ffn_down", grid=(lp // tr,),
        in_specs=[_rows(tr, D_FF), _full(w.shape), _rows(tr, D)]
        + [pl.BlockSpec((FRONT, D), functools.partial(lambda i, b: (jnp.maximum(i * nb - 1 + b, 0), 0), b=b)) for b in range(nb)]
        + [_full((1, D))],
        out_specs=[_rows(tr, D), _rows(tr, D), _full((1, D)), _full((8, 128))],
        out_shape=[_sds((lp, D), F32), _sds((lp, D), BF16), _sds((1, D), F32), _sds((8, 128), F32)],
        compiler_params=_cp("arbitrary"))(a, w, h1, *([tgt] * nb), g_post)


def _ffn_gate_bwd(u, uc, dd, w_down, cw):
    lp = u.shape[0]
    tr = _rt(lp, (320, 128))
    n = lp // tr

    def body(u_ref, uc_ref, dd_ref, wd_ref, cw_ref, du_ref, dcw_ref, dcb_ref, nx_scr):
        i = pl.program_id(0)
        t = n - 1 - i
        first = i == 0

        @pl.when(first)
        def _():
            nx_scr[...] = jnp.zeros_like(nx_scr)

        grow = t * tr + lax.broadcasted_iota(jnp.int32, (tr, FFN_CB), 0)
        ddv = dd_ref[...]
        for j in range(D_FF // FFN_CB):
            sls = [slice(off + j * FFN_CB, off + (j + 1) * FFN_CB) for off in (0, D_FF)]
            cvg, cvv = uc_ref[:, sls[0]].astype(F32), uc_ref[:, sls[1]].astype(F32)
            dav = _dot_nt(ddv, wd_ref[j * FFN_CB:(j + 1) * FFN_CB, :])
            dcv = (dav * cvv * _dsilu(cvg), dav * _silu(cvg))
            for hf in range(2):
                sl = sls[hf]
                g = dcv[hf]
                nxt = nx_scr[:, sl]
                ahead = [g] + [_shift_up(g, nxt, s) for s in range(1, FFN_K)]
                x = u_ref[:, sl].astype(F32)
                rows = [jnp.sum(x * ahead[FFN_K - 1 - kk], axis=0, keepdims=True) for kk in range(FFN_K)]
                rows.append(jnp.zeros((8 - FFN_K, FFN_CB), F32))
                upd_w = jnp.concatenate(rows, axis=0)
                upd_b = jnp.sum(g, axis=0, keepdims=True)

                @pl.when(first)
                def _():
                    dcw_ref[:, sl] = upd_w
                    dcb_ref[:, sl] = upd_b

                @pl.when(jnp.logical_not(first))
                def _():
                    dcw_ref[:, sl] += upd_w
                    dcb_ref[:, sl] += upd_b

                du = cw_ref[FFN_K - 1:FFN_K, sl] * g
                for s in range(1, FFN_K):
                    du = du + cw_ref[FFN_K - 1 - s:FFN_K - s, sl] * ahead[s]
                du_ref[:, sl] = jnp.where(grow >= PAD_ROWS, du, 0.0).astype(BF16)
                nx_scr[:, sl] = g[:8]

    wide = pl.BlockSpec((tr, 2 * D_FF), lambda i: (n - 1 - i, 0))
    return pl.pallas_call(
        body, name="ffn_gate_bwd", grid=(n,),
        in_specs=[wide, wide, pl.BlockSpec((tr, D), lambda i: (n - 1 - i, 0)), _full(w_down.shape), _full((8, 2 * D_FF))],
        out_specs=[wide, _full((8, 2 * D_FF)), _full((1, 2 * D_FF))],
        out_shape=[_sds((lp, 2 * D_FF), BF16), _sds((8, 2 * D_FF), F32), _sds((1, 2 * D_FF), F32)],
        scratch_shapes=[pltpu.VMEM((8, 2 * D_FF), F32)],
        compiler_params=_cp("arbitrary"))(u, uc, dd, w_down, cw)


def _mla_bwd(dq, dk, dv, lat, qg, kvg, wq, wkv, cos, sa, sb):
    lp = lat.shape[0]
    tr = _rt(lp, (320, 128))

    def body(dq_ref, dk_ref, dv_ref, lat_ref, qg_ref, kvg_ref, wq_ref, wkv_ref, cos_ref, sa_ref, sb_ref,
             dqf_ref, dkvf_ref, dlat_ref, dqg_ref, dkvg_ref):
        i = pl.program_id(0)
        cos_v, sa_v, sb_v = cos_ref[...], sa_ref[...], sb_ref[...]
        dkpe = jnp.zeros((tr, 128), F32)
        for h in range(MLA_H):
            dqh = dq_ref[h] * SOFTMAX_SCALE
            dqf_ref[:, h * DN:(h + 1) * DN] = dqh[:, :DN].astype(BF16)
            dqf_ref[:, D + h * 128:D + (h + 1) * 128] = _rope_t(dqh[:, DN:], cos_v, sa_v, sb_v).astype(BF16)
            dkh = dk_ref[h]
            dkvf_ref[:, h * DN:(h + 1) * DN] = dkh[:, :DN].astype(BF16)
            dkpe = dkpe + dkh[:, DN:]
            dkvf_ref[:, D + h * DV:D + (h + 1) * DV] = dv_ref[h].astype(BF16)
        dql = _dot_nt(dqf_ref[...], wq_ref[...])
        dkl = _dot_nt(dkvf_ref[...], wkv_ref[...])
        lat_v = lat_ref[...]
        dqc, dqg = _rms_bwd(lat_v[:, :QR], qg_ref[...], dql)
        dkc, dkg = _rms_bwd(lat_v[:, QR:QR + KVR], kvg_ref[...], dkl)
        dlat_ref[:, :QR] = dqc
        dlat_ref[:, QR:QR + KVR] = dkc
        dlat_ref[:, QR + KVR:] = _rope_t(dkpe, cos_v, sa_v, sb_v)
        _acc_rows(dqg_ref, dqg, i == 0)
        _acc_rows(dkvg_ref, dkg, i == 0)

    hb = lambda w: pl.BlockSpec((MLA_H, tr, w), lambda i: (0, i, 0))
    return pl.pallas_call(
        body, name="mla_bwd", grid=(lp // tr,),
        in_specs=[hb(256), hb(256), hb(128), _rows(tr, LAT_W), _full((1, QR)), _full((1, KVR)), _full(wq.shape),
                  _full(wkv.shape), _rows(tr, 128), _rows(tr, 128), _rows(tr, 128)],
        out_specs=[_rows(tr, 2 * D), _rows(tr, 2 * D), _rows(tr, LAT_W), _full((1, QR)), _full((1, KVR))],
        out_shape=[_sds((lp, 2 * D), BF16), _sds((lp, 2 * D), BF16), _sds((lp, LAT_W), F32), _sds((1, QR), F32),
                   _sds((1, KVR), F32)],
        compiler_params=_cp("arbitrary"))(dq, dk, dv, lat, qg, kvg, wq, wkv, cos, sa, sb)


def _inproj_bwd(dlat, dz, dxbc, ddt, w, h0, g, dh1, ride=()):
    lp = h0.shape[0]
    tr = _rt(lp, (320, 128))
    segs = ((0, LAT_W), (LAT_W, LAT_W + D_SSM), (LAT_W + D_SSM, LAT_W + D_SSM + D_XBC), (IN_P - 128, IN_P))
    nr = len(ride)
    steps = lp // tr

    def body(dl_ref, dz_ref, dx_ref, dt_ref, w_ref, h_ref, g_ref, r_ref, *rest):
        ps = rest[:nr]
        o_ref, dg_ref = rest[nr:nr + 2]
        got, sems = rest[nr + 2:2 * nr + 2], rest[2 * nr + 2:]
        step = pl.program_id(0)
        if nr:
            pl.when(step == 0)(lambda: _ride_exchange(ps, got, *sems, 0))
        dhn = jnp.zeros((tr, D), F32)
        for ref, (a, b) in zip((dl_ref, dz_ref, dx_ref, dt_ref), segs):
            dhn = dhn + _dot_nt(ref[...].astype(BF16), w_ref[:, a:b])
        dx, dg = _rms_bwd(h_ref[...], g_ref[...], dhn)
        o_ref[...] = dx + r_ref[...]
        _acc_rows(dg_ref, dg, step == 0)
        if nr:
            pl.when(step == steps - 1)(lambda: _ride_exchange(ps, got, *sems, 1))

    sems = [pltpu.SemaphoreType.DMA((3 * nr,)), pltpu.SemaphoreType.DMA((3 * nr,))] if nr else []
    outs = pl.pallas_call(
        body, name="inproj_bwd", grid=(steps,),
        in_specs=[_rows(tr, LAT_W), _rows(tr, D_SSM), _rows(tr, D_XBC), _rows(tr, 128), _full(w.shape), _rows(tr, D),
                  _full((1, D)), _rows(tr, D)] + [ANY] * nr,
        out_specs=[_rows(tr, D), _full((1, D))] + [ANY] * nr,
        out_shape=[_sds((lp, D), F32), _sds((1, D), F32)] + [_sds((3,) + p.shape[1:], p.dtype) for p in ride],
        scratch_shapes=sems, compiler_params=_cp("arbitrary"))(dlat, dz, dxbc, ddt, w, h0, g, dh1, *ride)
    return outs[0], outs[1], list(outs[2:])


def _rope_tables(lp):
    pos = (jnp.arange(lp, dtype=jnp.int32) - PAD_ROWS).astype(F32)
    inv = ROPE_THETA ** (-jnp.arange(0, DR, 2, dtype=F32) / DR)
    ang = pos[:, None] * inv[None, :]
    cos, sin = jnp.cos(ang), jnp.sin(ang)
    zero = jnp.zeros_like(sin)
    cos128 = jnp.concatenate([cos, cos, cos, cos], axis=1)
    sa128 = jnp.concatenate([-sin, zero, -sin, zero], axis=1)
    sb128 = jnp.concatenate([zero, sin, zero, sin], axis=1)
    return cos128, sa128, sb128


def _pad_rows8(w):
    return jnp.concatenate([w, jnp.zeros((8 - w.shape[0], w.shape[1]), w.dtype)], axis=0)


def _lane_pad(v):
    return jnp.concatenate([v, jnp.zeros((v.shape[0], 128 - v.shape[1]), v.dtype)], axis=1)


def _late_weights(bufs):
    w_out, w_up, w_down = bufs
    return dict(w_out=w_out.reshape(2 * D, D), w_up=w_up.reshape(N_CHIPS, D, 2 * D_FF // N_CHIPS), w_down=w_down.reshape(D_FF, D))


def _device_step(x, tgt, meta, p, late_bufs=(), early_reduce=None, last_reduce=None):
    s = x.shape[0]
    lp = s + FRONT
    zpad = jnp.zeros((PAD_ROWS, D), F32)
    h0 = jnp.concatenate([zpad, meta, x], axis=0)
    cos, sa, sb = _rope_tables(lp)

    w_in = p["w_in"]
    w_in_p = jnp.concatenate([w_in[:, :QR + KVR + DR], jnp.zeros((D, 64), BF16), w_in[:, QR + KVR + DR:],
                              jnp.zeros((D, 128 - SSM_H), BF16)], axis=1)
    w_uq = p["w_uq"]
    wq_p = jnp.concatenate([w_uq[:, :, :DN].reshape(QR, MLA_H * DN),
                            jnp.concatenate([w_uq[:, :, DN:], jnp.zeros((QR, MLA_H, 128 - DR), BF16)], axis=2).reshape(QR, MLA_H * 128)],
                           axis=1)
    w_ukv = p["w_ukv"]
    wkv_p = jnp.concatenate([w_ukv[:, :, :DN].reshape(KVR, MLA_H * DN), w_ukv[:, :, DN:].reshape(KVR, MLA_H * DV)], axis=1)
    scw = _pad_rows8(p["ssm_conv_w"])
    fcw = _pad_rows8(p["ffn_conv_w"])
    dtb, alog = _lane_pad(p["ssm_dt_bias"]), _lane_pad(p["ssm_A_log"])
    dtbT, alogT = p["ssm_dt_bias"].reshape(SSM_H, 1), p["ssm_A_log"].reshape(SSM_H, 1)
    d_e = jnp.repeat(p["ssm_D"], SSM_P, axis=1)

    hn, lat, z, xbc, dtr = _inproj(h0, p["norm_mix_pre"], w_in_p)
    dtrT = dtr[:, :SSM_H].T
    q, k, v, qlat, kvlat = _mla_prep(lat, p["q_a_norm"], p["kv_a_norm"], wq_p, wkv_p, cos, sa, sb)
    o, lse, gathered = _attn_fwd(q, k, v, ride=late_bufs)
    if late_bufs:
        p = dict(p, **_late_weights(gathered))
    ssm, st = _ssd_fwd(xbc, z, dtr, dtrT, scw, p["ssm_conv_b"], dtb, dtbT, alog, alogT, d_e, p["ssm_norm"])
    mixin, mix, h1 = _mixout_fwd(o, ssm, h0, p["attn_out_norm"], p["norm_mix_post"], p["w_out"])
    hn2, u = _ffn_up(h1, p["norm_ffn_pre"], p["w_up"])
    uc, a = _ffn_gate(u, fcw, p["ffn_conv_b"])
    dh2, dd, g_ffn_post, loss = _ffn_down(a, p["w_down"], h1, tgt, p["norm_ffn_post"])

    g_w_down = _mm_tn(a, dd, "ffn_dw_down", tn=512)
    du, g_fcw, g_fcb = _ffn_gate_bwd(u, uc, dd, p["w_down"], fcw)
    dh1, g_ffn_pre = _ffn_in_bwd(du, p["w_up"], h1, p["norm_ffn_pre"], dh2)
    g_w_up = _mm_tn(hn2, du, "ffn_dw_up", tn=D_FF // 2, chunked=True)
    ffn_gs = early_reduce.halves(dict(w_up=g_w_up, w_down=g_w_down)) if early_reduce else ()
    dmix, dssm, do, g_mix_post, g_ao, delta, ffn_sib = _mixout_bwd(mix, p["norm_mix_post"], dh1, o, p["attn_out_norm"],
                                                                  p["w_out"], ride=ffn_gs)
    g_w_out = _mm_tn(mixin, dmix, "mix_dw_out", tn=512)
    t = _rt(lp, (640, 128))
    pairs = early_reduce.pairs(dict(w_out=g_w_out), ffn_gs, ffn_sib) if early_reduce else ()
    dq, dk, dv, got = _attn_bwd(q, k, v, do, lse[:, 0, :].reshape(MLA_H, lp // t, 1, t), delta.reshape(MLA_H, lp // t, 1, t),
                                ride=pairs)
    dqf, dkvf, dlat, g_qa, g_kva = _mla_bwd(dq, dk, dv, lat, p["q_a_norm"], p["kv_a_norm"], wq_p, wkv_p, cos, sa, sb)
    g_wq_p = _mm_tn(qlat, dqf, "mla_dw_uq")
    g_wkv_p = _mm_tn(kvlat, dkvf, "mla_dw_ukv")
    dz, dxbc, ddtr, g_scw, g_scb, g_dtb, g_alog, g_dd, g_ssm_norm = _ssd_bwd(
        dssm, xbc, z, dtr, dtrT, st, scw, p["ssm_conv_b"], dtb, dtbT, alog, alogT, d_e, p["ssm_norm"])
    g_in_p = jnp.concatenate([_mm_tn(hn, dlat, "in_dw_lat"), _mm_tn(hn, dz, "in_dw_z"), _mm_tn(hn, dxbc, "in_dw_xbc"),
                              _mm_tn(hn, ddtr, "in_dw_dt")], axis=1)
    g_w_in = jnp.concatenate([g_in_p[:, :QR + KVR + DR], g_in_p[:, LAT_W:LAT_W + D_SSM + D_XBC + SSM_H]], axis=1)
    g_w_uq = jnp.concatenate([g_wq_p[:, :D].reshape(QR, MLA_H, DN), g_wq_p[:, D:].reshape(QR, MLA_H, 128)[:, :, :DR]], axis=2)
    g_w_ukv = jnp.concatenate([g_wkv_p[:, :D].reshape(KVR, MLA_H, DN), g_wkv_p[:, D:].reshape(KVR, MLA_H, DV)], axis=2)
    pairs2 = last_reduce(dict(w_in=g_w_in, w_uq=g_w_uq, w_ukv=g_w_ukv)) if last_reduce else ()
    dh0, g_mix_pre, got2 = _inproj_bwd(dlat, dz, dxbc, ddtr, w_in_p, h0, p["norm_mix_pre"], dh1, ride=pairs2)
    grads = dict(
        norm_mix_pre=g_mix_pre, norm_mix_post=g_mix_post, norm_ffn_pre=g_ffn_pre, norm_ffn_post=g_ffn_post, w_in=g_w_in,
        q_a_norm=g_qa, w_uq=g_w_uq, kv_a_norm=g_kva, w_ukv=g_w_ukv, attn_out_norm=g_ao, ssm_conv_w=g_scw[:SSM_K],
        ssm_conv_b=g_scb, ssm_dt_bias=g_dtb[:, :SSM_H], ssm_A_log=g_alog[:, :SSM_H], ssm_D=g_dd[:, :SSM_H],
        ssm_norm=g_ssm_norm, w_out=g_w_out, w_up=g_w_up, ffn_conv_w=g_fcw[:FFN_K], ffn_conv_b=g_fcb, w_down=g_w_down)
    return loss, dh0[FRONT:], dh0[PAD_ROWS:FRONT], grads, (list(pairs2) + list(pairs), list(got2) + list(got))


N_CHIPS = 4
BIG = (("w_in", (D, D_IN // N_CHIPS)), ("w_uq", (QR // N_CHIPS, MLA_H, DN + DR)), ("w_ukv", (KVR // N_CHIPS, MLA_H, DN + DV)),
       ("w_out", (2 * D // N_CHIPS, D)), ("w_up", (D, 2 * D_FF // N_CHIPS)), ("w_down", (D_FF // N_CHIPS, D)))
SMALL_SHARDED = (("meta_tokens", (N_META, D // N_CHIPS)), ("ssm_conv_w", (SSM_K, D_XBC // N_CHIPS)),
                 ("ffn_conv_w", (FFN_K, 2 * D_FF // N_CHIPS)))
SMALL_REPL = (("norm_mix_pre", D), ("norm_mix_post", D), ("norm_ffn_pre", D), ("norm_ffn_post", D), ("q_a_norm", QR),
              ("kv_a_norm", KVR), ("attn_out_norm", D), ("ssm_conv_b", D_XBC), ("ssm_dt_bias", SSM_H), ("ssm_A_log", SSM_H),
              ("ssm_D", SSM_H), ("ssm_norm", D_SSM), ("ffn_conv_b", 2 * D_FF))
ANY = pl.BlockSpec(memory_space=pl.ANY)


def _pad128(v):
    n = v.shape[0]
    return jnp.concatenate([v, jnp.zeros(((-n) % 128,), v.dtype)]) if n % 128 else v


def _pack_rows(vs, rows):
    flat = jnp.concatenate([_pad128(v.reshape(-1)) for v in vs])
    flat = jnp.concatenate([flat, jnp.zeros((rows * 128 - flat.shape[0],), flat.dtype)])
    return flat.reshape(rows, 128)


def _unpack_rows(pack, sizes):
    flat = pack.reshape(-1)
    out, off = [], 0
    for n in sizes:
        out.append(flat[off:off + n])
        off += n + (-n) % 128
    return out


def _my_place():
    return lax.axis_index("x"), lax.axis_index("y"), lax.axis_index("c")


def _other_chips(x, y):
    return [(1 - x, y), (x, 1 - y), (1 - x, 1 - y)]


def _remote(src, dst, send, recv, dev):
    return pltpu.make_async_remote_copy(src_ref=src, dst_ref=dst, send_sem=send, recv_sem=recv, device_id=dev,
                                        device_id_type=MESH)


SMALL_AG_ROWS = 80


def _gather_weights(shards, small, name):
    arrs = list(shards) + ([] if small is None else [small])
    n, nb = len(arrs), len(shards)

    def body(*refs):
        ins, outs = refs[:n], refs[n:2 * n]
        send, recv, lsem = refs[2 * n:]
        x, y, c = _my_place()
        me = 2 * x + y
        chips = _other_chips(x, y)
        slot = lambda w, chip, cc: outs[w].at[chip, cc] if w < nb else outs[w].at[chip]
        mine = lambda w: slot(w, me, c) if w < nb else ins[w]
        loc = [pltpu.make_async_copy(ins[w], outs[w].at[me], lsem.at[w - nb]) for w in range(nb, n)]
        for cp in loc:
            cp.start()
        sends = []
        for w in range(n):
            for kk, (cx, cy) in enumerate(chips):
                sends.append(_remote(mine(w), slot(w, me, c), send.at[3 * w + kk], recv.at[3 * w + kk], (cx, cy, c)))
        for cp in sends:
            cp.start()
        for w in range(nb):
            for kk, (cx, cy) in enumerate(chips):
                src = 2 * cx + cy
                _remote(mine(w), slot(w, src, c), send.at[3 * w + kk], recv.at[3 * w + kk], (cx, cy, c)).wait_recv()
                fwd = _remote(slot(w, src, c), slot(w, src, c), send.at[3 * (n + w) + kk], recv.at[3 * (n + w) + kk], (x, y, 1 - c))
                fwd.start()
                sends.append(fwd)
        for w in range(n):
            for kk, (cx, cy) in enumerate(chips):
                src = 2 * cx + cy
                if w < nb:
                    _remote(mine(w), slot(w, src, 1 - c), send.at[3 * (n + w) + kk], recv.at[3 * (n + w) + kk],
                            (x, y, 1 - c)).wait_recv()
                else:
                    _remote(ins[w], slot(w, src, c), send.at[3 * w + kk], recv.at[3 * w + kk], (cx, cy, c)).wait_recv()
        for cp in sends:
            cp.wait_send()
        for cp in loc:
            cp.wait()

    return pl.pallas_call(
        body, name=name, in_specs=[ANY] * n, out_specs=[ANY] * n,
        out_shape=[_sds(a.shape, a.dtype) for a in shards] + ([] if small is None else [_sds((N_CHIPS,) + small.shape, small.dtype)]),
        input_output_aliases={w: w for w in range(nb)},
        scratch_shapes=[pltpu.SemaphoreType.DMA((3 * (n + nb),)), pltpu.SemaphoreType.DMA((3 * (n + nb),)),
                        pltpu.SemaphoreType.DMA((max(n - nb, 1),))])(*arrs)


def _place_own(wt, chip, name):
    r, c = wt.shape
    tr = _row_tile(r, c)

    def body(c_ref, w_ref, o_ref):
        o_ref[...] = w_ref[...].astype(BF16)

    return pl.pallas_call(
        body, name=name, out_shape=_sds((N_CHIPS, r, c), BF16),
        grid_spec=pltpu.PrefetchScalarGridSpec(
            num_scalar_prefetch=1, grid=(r // tr,), in_specs=[pl.BlockSpec((tr, c), lambda i, cr: (i, 0))],
            out_specs=pl.BlockSpec((None, tr, c), lambda i, cr: (cr[0], i, 0))),
        compiler_params=_cp("parallel"))(chip, wt)


def _send_sibling_halves(gs, name):
    n = len(gs)

    def body(*refs):
        ins, outs, send, recv = refs[:n], refs[n:2 * n], refs[2 * n], refs[2 * n + 1]
        x, y, c = _my_place()
        cps = [_remote(ins[w].at[:, 1 - c], outs[w], send.at[w], recv.at[w], (x, y, 1 - c)) for w in range(n)]
        for cp in cps:
            cp.start()
        for cp in cps:
            cp.wait()

    return pl.pallas_call(
        body, name=name, in_specs=[ANY] * n, out_specs=[ANY] * n,
        out_shape=[_sds((g.shape[0],) + g.shape[2:], g.dtype) for g in gs],
        scratch_shapes=[pltpu.SemaphoreType.DMA((n,)), pltpu.SemaphoreType.DMA((n,))])(*gs)


def _ride_gather(bufs, send, recv, phase):
    n = len(bufs)
    x, y, c = _my_place()
    me = 2 * x + y
    for w in range(n):
        for kk, (cx, cy) in enumerate(_other_chips(x, y)):
            src = 2 * cx + cy
            out = lambda: _remote(bufs[w].at[me, c], bufs[w].at[me, c], send.at[3 * w + kk], recv.at[3 * w + kk], (cx, cy, c))
            fwd = lambda: _remote(bufs[w].at[src, c], bufs[w].at[src, c], send.at[3 * (n + w) + kk],
                                  recv.at[3 * (n + w) + kk], (x, y, 1 - c))
            if phase == 0:
                out().start()
            elif phase == 1:
                _remote(bufs[w].at[me, c], bufs[w].at[src, c], send.at[3 * w + kk], recv.at[3 * w + kk], (cx, cy, c)).wait_recv()
                fwd().start()
            else:
                _remote(bufs[w].at[me, c], bufs[w].at[src, 1 - c], send.at[3 * (n + w) + kk], recv.at[3 * (n + w) + kk],
                        (x, y, 1 - c)).wait_recv()
                out().wait_send()
                fwd().wait_send()


def _ride_sibling(gs, outs, send, recv, phase):
    x, y, c = _my_place()
    for w in range(len(gs)):
        cp = _remote(gs[w].at[:, 1 - c], outs[w], send.at[w], recv.at[w], (x, y, 1 - c))
        if phase == 0:
            cp.start()
        else:
            cp.wait()


def _ride_exchange(ps, outs, send, recv, phase):
    x, y, c = _my_place()
    for w in range(len(ps)):
        for kk, (cx, cy) in enumerate(_other_chips(x, y)):
            cp = _remote(ps[w].at[2 * cx + cy], outs[w].at[kk], send.at[3 * w + kk], recv.at[3 * w + kk], (cx, cy, c))
            if phase == 0:
                cp.start()
            else:
                cp.wait()


def _share_sibling(halves):
    n = len(halves)

    def body(*refs):
        outs, send, recv = refs[n:2 * n], refs[2 * n], refs[2 * n + 1]
        x, y, c = _my_place()
        cps = [_remote(outs[w].at[c], outs[w].at[c], send.at[w], recv.at[w], (x, y, 1 - c)) for w in range(n)]
        for cp in cps:
            cp.start()
        for w in range(n):
            cps[w].wait_send()
            _remote(outs[w].at[c], outs[w].at[1 - c], send.at[w], recv.at[w], (x, y, 1 - c)).wait_recv()

    return pl.pallas_call(
        body, name="share_sibling", in_specs=[ANY] * n, out_specs=[ANY] * n,
        out_shape=[_sds(h.shape, h.dtype) for h in halves], input_output_aliases={w: w for w in range(n)},
        scratch_shapes=[pltpu.SemaphoreType.DMA((n,)), pltpu.SemaphoreType.DMA((n,))])(*halves)


def _row_tile(r, c, cap=1 << 20):
    return next(t for t in range(r, 0, -1) if r % t == 0 and (t % 8 == 0 or t == r) and t * c * 4 <= cap)


def _add_pair(g, t, core, name):
    _, _, r, c = g.shape
    tr = _row_tile(r, c)

    def body(c_ref, g_ref, t_ref, o_ref):
        o_ref[...] = (g_ref[...] + t_ref[...]).astype(BF16)

    return pl.pallas_call(
        body, name=name, out_shape=_sds(t.shape, BF16),
        grid_spec=pltpu.PrefetchScalarGridSpec(
            num_scalar_prefetch=1, grid=(N_CHIPS, r // tr),
            in_specs=[pl.BlockSpec((None, None, tr, c), lambda j, i, cr: (j, cr[0], i, 0)),
                      pl.BlockSpec((None, tr, c), lambda j, i, cr: (j, i, 0))],
            out_specs=pl.BlockSpec((None, tr, c), lambda j, i, cr: (j, i, 0))),
        compiler_params=_cp("parallel", "parallel"))(core, g, t)


def _add_chips(p, got, chip, name):
    _, r, c = p.shape
    tr = _row_tile(r, c)

    def body(c_ref, p_ref, g_ref, o_ref):
        o_ref[...] = ((p_ref[...].astype(F32) + g_ref[0].astype(F32)) + g_ref[1].astype(F32)) + g_ref[2].astype(F32)

    return pl.pallas_call(
        body, name=name, out_shape=_sds((2, r, c), F32),
        grid_spec=pltpu.PrefetchScalarGridSpec(
            num_scalar_prefetch=1, grid=(r // tr,),
            in_specs=[pl.BlockSpec((None, tr, c), lambda i, cr: (cr[0], i, 0)), pl.BlockSpec((3, tr, c), lambda i, cr: (0, i, 0))],
            out_specs=pl.BlockSpec((None, tr, c), lambda i, cr: (cr[1], i, 0))),
        compiler_params=_cp("parallel"))(chip, p, got)


SMALL_AR_ROWS = 424


def _allreduce_small(v):
    def body(v_ref, o_ref, gath, send, recv):
        x, y, c = _my_place()
        me = 4 * x + 2 * y + c
        gath[me] = v_ref[...]
        cps = []
        for dd in range(1, 8):
            dx, dy, dc = dd >> 2, (dd >> 1) & 1, dd & 1
            peer = (1 - x if dx else x, 1 - y if dy else y, 1 - c if dc else c)
            cps.append(_remote(v_ref, gath.at[me], send.at[dd - 1], recv.at[dd - 1], peer))
        for cp in cps:
            cp.start()
        for cp in cps:
            cp.wait()
        acc = gath[0]
        for dev in range(1, 8):
            acc = acc + gath[dev]
        o_ref[...] = acc

    vm = pl.BlockSpec(memory_space=pltpu.VMEM)
    return pl.pallas_call(
        body, name="allreduce_small", in_specs=[vm], out_specs=vm, out_shape=_sds(v.shape, F32),
        scratch_shapes=[pltpu.VMEM((8,) + v.shape, F32), pltpu.SemaphoreType.DMA((7,)), pltpu.SemaphoreType.DMA((7,))])(v)


def _adamw(w, g, m, v, name):
    r, c = w.shape
    tr = _row_tile(r, c)

    def body(w_ref, g_ref, m_ref, v_ref, d_ref, m2_ref, v2_ref):
        gv = g_ref[...]
        m2 = ADAM_B1 * m_ref[...] + (1.0 - ADAM_B1) * gv
        v2 = ADAM_B2 * v_ref[...] + (1.0 - ADAM_B2) * jnp.square(gv)
        m_hat = m2 / (1.0 - ADAM_B1 ** ADAM_STEP)
        v_hat = v2 / (1.0 - ADAM_B2 ** ADAM_STEP)
        d_ref[...] = -ADAM_LR * (m_hat / (jnp.sqrt(v_hat) + ADAM_EPS) + ADAM_WD * w_ref[...])
        m2_ref[...] = m2
        v2_ref[...] = v2

    return pl.pallas_call(
        body, name=name, grid=(r // tr,), in_specs=[_rows(tr, c)] * 4, out_specs=[_rows(tr, c)] * 3,
        out_shape=[_sds((r, c), F32)] * 3, compiler_params=_cp("parallel"))(w, g, m, v)


WEIGHT_NAMES = ("meta_tokens", "norm_mix_pre", "norm_mix_post", "norm_ffn_pre", "norm_ffn_post", "w_in", "q_a_norm", "w_uq",
                "kv_a_norm", "w_ukv", "attn_out_norm", "ssm_conv_w", "ssm_conv_b", "ssm_dt_bias", "ssm_A_log", "ssm_D",
                "ssm_norm", "w_out", "w_up", "ffn_conv_w", "ffn_conv_b", "w_down")
SMALL_ADAM_ROWS = 192


def kernel(x, meta_tokens, norm_mix_pre, norm_mix_post, norm_ffn_pre, norm_ffn_post, w_in, q_a_norm, w_uq, kv_a_norm, w_ukv, attn_out_norm, ssm_conv_w, ssm_conv_b, ssm_dt_bias, ssm_A_log, ssm_D, ssm_norm, w_out, w_up, ffn_conv_w, ffn_conv_b, w_down, loss_target, m_meta_tokens, m_norm_mix_pre, m_norm_mix_post, m_norm_ffn_pre, m_norm_ffn_post, m_w_in, m_q_a_norm, m_w_uq, m_kv_a_norm, m_w_ukv, m_attn_out_norm, m_ssm_conv_w, m_ssm_conv_b, m_ssm_dt_bias, m_ssm_A_log, m_ssm_D, m_ssm_norm, m_w_out, m_w_up, m_ffn_conv_w, m_ffn_conv_b, m_w_down, v_meta_tokens, v_norm_mix_pre, v_norm_mix_post, v_norm_ffn_pre, v_norm_ffn_post, v_w_in, v_q_a_norm, v_w_uq, v_kv_a_norm, v_w_ukv, v_attn_out_norm, v_ssm_conv_w, v_ssm_conv_b, v_ssm_dt_bias, v_ssm_A_log, v_ssm_D, v_ssm_norm, v_w_out, v_w_up, v_ffn_conv_w, v_ffn_conv_b, v_w_down):
    args = locals()
    w = {n: args[n] for n in WEIGHT_NAMES}
    mom = {n: args["m_" + n] for n in WEIGHT_NAMES}
    var = {n: args["v_" + n] for n in WEIGHT_NAMES}
    cx, cy, cc = _my_place()
    chip = 2 * cx + cy

    two_d = {n: (shp[0], functools.reduce(lambda a, b: a * b, shp[1:])) for n, shp in BIG}
    names = [n for n, _ in BIG]
    core_i = cc.astype(jnp.int32).reshape(1)
    chip_i = chip.astype(jnp.int32).reshape(1)
    early, late = names[:3], names[3:]
    halves = lambda n, a: a.reshape(N_CHIPS, 2, two_d[n][0] // 2, two_d[n][1])
    bufs = {n: halves(n, _place_own(w[n].reshape(two_d[n]), chip_i, "place_" + n)) for n in names}
    small = _pack_rows([w[n] for n, _ in SMALL_SHARDED], SMALL_AG_ROWS)
    *gathered, small_all = _gather_weights([bufs[n] for n in early], small, "allgather_weights")
    gath = {n: a.reshape((N_CHIPS,) + two_d[n]) for n, a in zip(early, gathered)}
    p = dict(w_in=gath["w_in"].transpose(1, 0, 2).reshape(D, D_IN), w_uq=gath["w_uq"].reshape(QR, MLA_H, DN + DR),
             w_ukv=gath["w_ukv"].reshape(KVR, MLA_H, DN + DV))
    sm_parts = [_unpack_rows(small_all[j], [a * b for _, (a, b) in SMALL_SHARDED]) for j in range(N_CHIPS)]
    for i, (n, shp) in enumerate(SMALL_SHARDED):
        p[n] = jnp.concatenate([sm_parts[j][i].reshape(shp) for j in range(N_CHIPS)], axis=1)
    for n, _ in SMALL_REPL:
        p[n] = w[n]
    meta_full = p.pop("meta_tokens")

    place_i = jnp.stack([chip, cc]).astype(jnp.int32)

    def pair_sums(gd, group):
        gd = dict(gd)
        if "w_in" in gd:
            gd["w_in"] = gd["w_in"].reshape(D, N_CHIPS, D_IN // N_CHIPS).transpose(1, 0, 2)
        gs = [halves(n, gd[n]) for n in group]
        from_sib = _send_sibling_halves(gs, "reduce_sibling_" + group[0])
        return [_add_pair(gg, tt, core_i, "reduce_pair_" + n) for n, gg, tt in zip(group, gs, from_sib)]

    class LateReduce:
        @staticmethod
        def halves(gd):
            return [halves(n, gd[n]) for n in late[1:]]

        @staticmethod
        def pairs(gd, ffn_gs, ffn_sib):
            gs = [halves(late[0], gd[late[0]])]
            sib = _send_sibling_halves(gs, "reduce_sibling_" + late[0])
            return [_add_pair(gg, tt, core_i, "reduce_pair_" + n)
                    for n, gg, tt in zip(late, gs + list(ffn_gs), list(sib) + list(ffn_sib))]

    loss_part, gx, gmeta, g, (pairs, got) = _device_step(
        x[0], loss_target[0], meta_full, p, late_bufs=[bufs[n] for n in late], early_reduce=LateReduce,
        last_reduce=lambda gd: pair_sums(gd, early))

    small_names = [n for n, _ in SMALL_REPL] + ["ssm_conv_w", "ffn_conv_w"]
    small_sizes = [128] + [sz for _, sz in SMALL_REPL] + [N_META * D, SSM_K * D_XBC, FFN_K * 2 * D_FF]
    order = [n for n, _ in SMALL_REPL]
    sp = _pack_rows([loss_part[0]] + [g[n] for n in order] + [gmeta, g["ssm_conv_w"], g["ffn_conv_w"]], SMALL_AR_ROWS)
    red = _unpack_rows(_allreduce_small(sp), small_sizes)
    loss = red[0][0]
    gfull = {n: red[1 + i].reshape(1, -1) for i, n in enumerate(order)}
    n_r = len(order)
    gfull["meta_tokens"] = lax.dynamic_slice_in_dim(red[1 + n_r].reshape(N_META, D), chip * (D // N_CHIPS), D // N_CHIPS, axis=1)
    gfull["ssm_conv_w"] = lax.dynamic_slice_in_dim(red[2 + n_r].reshape(SSM_K, D_XBC), chip * (D_XBC // N_CHIPS),
                                                   D_XBC // N_CHIPS, axis=1)[None]
    gfull["ffn_conv_w"] = lax.dynamic_slice_in_dim(red[3 + n_r].reshape(FFN_K, 2 * D_FF), chip * (2 * D_FF // N_CHIPS),
                                                   2 * D_FF // N_CHIPS, axis=1)[None]

    mine = [_add_chips(pp, gg, place_i, "reduce_chips_" + n) for n, pp, gg in zip(names, pairs, got)]
    for n, both in zip(names, _share_sibling(mine)):
        gfull[n] = both.reshape(two_d[n])

    delta, new_m, new_v = {}, {}, {}
    for n, shp in BIG:
        outs = _adamw(w[n].reshape(two_d[n]), gfull[n], mom[n].reshape(two_d[n]), var[n].reshape(two_d[n]), "adamw_" + n)
        delta[n], new_m[n], new_v[n] = (o.reshape((1,) + shp) for o in outs)
    snames = order + ["meta_tokens", "ssm_conv_w", "ffn_conv_w"]
    ssizes = [functools.reduce(lambda a, b: a * b, w[n].shape) for n in snames]
    packs = [_pack_rows([d[n] for n in snames], SMALL_ADAM_ROWS) for d in (w, gfull, mom, var)]
    outs = _adamw(*packs, "adamw_small")
    for d, o in zip((delta, new_m, new_v), outs):
        for n, piece in zip(snames, _unpack_rows(o, ssizes)):
            d[n] = piece.reshape(w[n].shape)
    gout = {n: gfull[n].reshape(w[n].shape) for n in WEIGHT_NAMES}
    return (loss, gx[None], *[gout[n] for n in WEIGHT_NAMES], *[delta[n] for n in WEIGHT_NAMES],
            *[new_m[n] for n in WEIGHT_NAMES], *[new_v[n] for n in WEIGHT_NAMES])
```

```python
import functools

import jax
import jax.numpy as jnp
from jax import lax
from jax.experimental import pallas as pl
from jax.experimental.pallas import tpu as pltpu

F32 = jnp.float32
BF16 = jnp.bfloat16

D = 1024
N_META = 16
FRONT = 128
PAD_ROWS = FRONT - N_META
MLA_H = 8
DN, DR, DV = 128, 64, 128
QR, KVR = 384, 256
SOFTMAX_SCALE = (DN + DR) ** -0.5
ROPE_THETA = 10000.0
SSM_H, SSM_P, SSM_G, SSM_N, SSM_K = 16, 64, 2, 128, 4
CHUNK = 128
D_SSM = SSM_H * SSM_P
D_XBC = D_SSM + 2 * SSM_G * SSM_N
GSZ = D_SSM // SSM_G
D_FF = 2816
FFN_K = 3
EPS = 1e-6
IN_SPLITS = (QR, KVR, DR, D_SSM, D_XBC, SSM_H)
D_IN = sum(IN_SPLITS)
LAT_W = 768
IN_P = LAT_W + D_SSM + D_XBC + 128
NEG = -1e30
LOG2E = 1.4426950408889634
LN2 = 0.6931471805599453
Q_SCALE = SOFTMAX_SCALE * LOG2E

ADAM_LR, ADAM_B1, ADAM_B2, ADAM_EPS, ADAM_WD, ADAM_STEP = 0.001, 0.9, 0.999, 1e-08, 0.01, 10

VMEM_LIMIT = 56 * 1024 * 1024
MM_ROWS = (640, 320, 128)
MESH = pl.DeviceIdType.MESH


def _sds(shape, dtype):
    return jax.ShapeDtypeStruct(shape, dtype)


def _cp(*sem):
    return pltpu.CompilerParams(dimension_semantics=sem, vmem_limit_bytes=VMEM_LIMIT)


def _rt(n, cands):
    for c in cands:
        if n % c == 0:
            return c
    raise ValueError((n, cands))


def _full(shape):
    nd = len(shape)
    return pl.BlockSpec(shape, lambda *_: (0,) * nd)


def _rows(tr, c):
    return pl.BlockSpec((tr, c), lambda i: (i, 0))


def _sigmoid(x):
    return 1.0 / (1.0 + jnp.exp(-x))


def _silu(x):
    return x * _sigmoid(x)


def _dsilu(x):
    s = _sigmoid(x)
    return s * (1.0 + x * (1.0 - s))


def _softplus(x):
    return jnp.maximum(x, 0.0) + jnp.log(1.0 + jnp.exp(-jnp.abs(x)))


def _rms(x, g):
    r = lax.rsqrt(jnp.mean(x * x, axis=-1, keepdims=True) + EPS)
    return x * r * g


def _rms_bwd(x, g, dy):
    r = lax.rsqrt(jnp.mean(x * x, axis=-1, keepdims=True) + EPS)
    xh = x * r
    dxh = dy * g
    dx = r * (dxh - xh * jnp.mean(dxh * xh, axis=-1, keepdims=True))
    return dx, jnp.sum(dy * xh, axis=0, keepdims=True)


def _dot(a, b):
    return jnp.dot(a, b, preferred_element_type=F32)


def _dot_nt(a, b):
    return lax.dot_general(a, b, (((1,), (1,)), ((), ())), preferred_element_type=F32)


def _dot_tn(a, b):
    return lax.dot_general(a, b, (((0,), (0,)), ((), ())), preferred_element_type=F32)


def _split3(x):
    hi = x.astype(BF16)
    r = x - hi.astype(F32)
    mid = r.astype(BF16)
    return hi, mid, (r - mid.astype(F32)).astype(BF16)


def _dot_hi(a, b, split="a"):
    if split == "a":
        bb = b.astype(BF16)
        return sum(_dot(t, bb) for t in _split3(a))
    ab = a.astype(BF16)
    return sum(_dot(ab, t) for t in _split3(b))


def _dot_nt_hi(a, b):
    bb = b.astype(BF16)
    return sum(_dot_nt(t, bb) for t in _split3(a))


def _shift_down(x, halo, j):
    xr = pltpu.roll(x, j, axis=0)
    hr = pltpu.roll(halo, j, axis=0)
    row = lax.broadcasted_iota(jnp.int32, (8, x.shape[1]), 0)
    first = jnp.where(row < j, hr, xr[:8])
    return jnp.concatenate([first, xr[8:]], axis=0)


def _shift_up(x, nxt, j):
    t = x.shape[0]
    xr = pltpu.roll(x, t - j, axis=0)
    nr = pltpu.roll(nxt, 8 - j, axis=0)
    row = lax.broadcasted_iota(jnp.int32, (8, x.shape[1]), 0)
    last = jnp.where(row + j >= 8, nr, xr[t - 8:])
    return jnp.concatenate([xr[:t - 8], last], axis=0)


def _acc_rows(ref, val, first):
    @pl.when(first)
    def _():
        ref[...] = val

    @pl.when(jnp.logical_not(first))
    def _():
        ref[...] += val


def _mm_tn(a, b, name, tn=None, trs=(1664, 640, 128), chunked=False):
    r, m = a.shape
    n = b.shape[1]
    tn = n if tn is None else tn
    tr = _rt(r, trs)

    def body(a_ref, b_ref, o_ref):
        part = _dot_tn(a_ref[...].astype(BF16), b_ref[...].astype(BF16))
        _acc_rows(o_ref, part, pl.program_id(1) == 0)

    if chunked:
        out_specs, out_shape = pl.BlockSpec((None, m, tn), lambda j, i: (j, 0, 0)), _sds((n // tn, m, tn), F32)
    else:
        out_specs, out_shape = pl.BlockSpec((m, tn), lambda j, i: (0, j)), _sds((m, n), F32)
    return pl.pallas_call(
        body, name=name, grid=(n // tn, r // tr),
        in_specs=[pl.BlockSpec((tr, m), lambda j, i: (i, 0)), pl.BlockSpec((tr, tn), lambda j, i: (i, j))],
        out_specs=out_specs, out_shape=out_shape, compiler_params=_cp("parallel", "arbitrary"))(a, b)


def _inproj(h0, g, w):
    lp = h0.shape[0]
    tr = _rt(lp, MM_ROWS)
    segs = ((0, LAT_W), (LAT_W, LAT_W + D_SSM), (LAT_W + D_SSM, LAT_W + D_SSM + D_XBC), (IN_P - 128, IN_P))

    def body(h_ref, g_ref, w_ref, hn_ref, lat_ref, z_ref, xbc_ref, dt_ref):
        hn = _rms(h_ref[...], g_ref[...]).astype(BF16)
        hn_ref[...] = hn
        for ref, (a, b) in zip((lat_ref, z_ref, xbc_ref, dt_ref), segs):
            ref[...] = _dot(hn, w_ref[:, a:b])

    return pl.pallas_call(
        body, name="inproj", grid=(lp // tr,), in_specs=[_rows(tr, D), _full((1, D)), _full(w.shape)],
        out_specs=[_rows(tr, D), _rows(tr, LAT_W), _rows(tr, D_SSM), _rows(tr, D_XBC), _rows(tr, 128)],
        out_shape=[_sds((lp, D), BF16), _sds((lp, LAT_W), F32), _sds((lp, D_SSM), F32), _sds((lp, D_XBC), F32),
                   _sds((lp, 128), F32)],
        compiler_params=_cp("parallel"))(h0, g, w)


def _rope(x, cos, sa, sb):
    return x * cos + pltpu.roll(x, 96, axis=1) * sa + pltpu.roll(x, 32, axis=1) * sb


def _rope_t(g, cos, sa, sb):
    return g * cos + pltpu.roll(g * sa, 32, axis=1) + pltpu.roll(g * sb, 96, axis=1)


def _mla_prep(lat, qg, kvg, wq, wkv, cos, sa, sb):
    lp = lat.shape[0]
    tr = _rt(lp, MM_ROWS)

    def body(lat_ref, qg_ref, kvg_ref, wq_ref, wkv_ref, cos_ref, sa_ref, sb_ref, q_ref, k_ref, v_ref, ql_ref, kl_ref):
        lat_v = lat_ref[...]
        ql = _rms(lat_v[:, :QR], qg_ref[...]).astype(BF16)
        kl = _rms(lat_v[:, QR:QR + KVR], kvg_ref[...]).astype(BF16)
        ql_ref[...] = ql
        kl_ref[...] = kl
        cos_v, sa_v, sb_v = cos_ref[...], sa_ref[...], sb_ref[...]
        kpe = _rope(lat_v[:, QR + KVR:LAT_W], cos_v, sa_v, sb_v).astype(BF16)
        for h in range(MLA_H):
            q_ref[h, :, 0:DN] = (_dot(ql, wq_ref[:, h * DN:(h + 1) * DN]) * Q_SCALE).astype(BF16)
            qpe = _dot(ql, wq_ref[:, D + h * 128:D + (h + 1) * 128])
            q_ref[h, :, DN:2 * DN] = (_rope(qpe, cos_v, sa_v, sb_v) * Q_SCALE).astype(BF16)
            k_ref[h, :, 0:DN] = _dot(kl, wkv_ref[:, h * DN:(h + 1) * DN]).astype(BF16)
            k_ref[h, :, DN:2 * DN] = kpe
            v_ref[h] = _dot(kl, wkv_ref[:, D + h * DV:D + (h + 1) * DV]).astype(BF16)

    hb = lambda w: pl.BlockSpec((MLA_H, tr, w), lambda i: (0, i, 0))
    return pl.pallas_call(
        body, name="mla_prep", grid=(lp // tr,),
        in_specs=[_rows(tr, LAT_W), _full((1, QR)), _full((1, KVR)), _full(wq.shape), _full(wkv.shape),
                  _rows(tr, 128), _rows(tr, 128), _rows(tr, 128)],
        out_specs=[hb(256), hb(256), hb(128), _rows(tr, QR), _rows(tr, KVR)],
        out_shape=[_sds((MLA_H, lp, 256), BF16), _sds((MLA_H, lp, 256), BF16), _sds((MLA_H, lp, 128), BF16),
                   _sds((lp, QR), BF16), _sds((lp, KVR), BF16)],
        compiler_params=_cp("parallel"))(lat, qg, kvg, wq, wkv, cos, sa, sb)


def _attn_mask(r0, c0, tq, tk, transposed=False):
    if transposed:
        kk = c0 + lax.broadcasted_iota(jnp.int32, (tk, tq), 0)
        qq = r0 + lax.broadcasted_iota(jnp.int32, (tk, tq), 1)
    else:
        qq = r0 + lax.broadcasted_iota(jnp.int32, (tq, tk), 0)
        kk = c0 + lax.broadcasted_iota(jnp.int32, (tq, tk), 1)
    return jnp.logical_and(kk <= qq, kk >= PAD_ROWS)


def _attn_fwd(q, k, v, ride=()):
    lp = q.shape[1]
    t = _rt(lp, (640, 128))
    nq = lp // t

    hp = 2

    nr = len(ride)
    steps = (MLA_H // hp) * nq

    def body(q_ref, k_ref, v_ref, *rest):
        o_ref, lse_ref = rest[nr:nr + 2]
        bufs, sems = rest[nr + 2:2 * nr + 2], rest[2 * nr + 2:]
        qi = pl.program_id(1)
        step = pl.program_id(0) * nq + qi
        if nr:
            pl.when(step == 0)(lambda: _ride_gather(bufs, *sems, 0))
            pl.when(step == steps // 2)(lambda: _ride_gather(bufs, *sems, 1))
        qv = [q_ref[a] for a in range(hp)]

        def tile(kj, carries, bias=None):
            kv_rows = pl.ds(pl.multiple_of(kj * t, t), t)
            out = []
            for a in range(hp):
                m, l, acc = carries[a]
                kk = k_ref[a, kv_rows, :]
                vv = v_ref[a, kv_rows, :]
                s = _dot_nt(qv[a], kk)
                if bias is not None:
                    s = s + bias
                m_new = jnp.maximum(m, jnp.max(s, axis=-1, keepdims=True))
                alpha = jnp.exp2(m - m_new)
                p = jnp.exp2(s - m_new)
                l = alpha * l + jnp.sum(p, axis=-1, keepdims=True)
                acc = alpha * acc + _dot(p.astype(BF16), vv)
                out.append((m_new, l, acc))
            return tuple(out)

        key = lax.broadcasted_iota(jnp.int32, (1, t), 1)
        pad_bias = jnp.where(jnp.logical_and(key >= PAD_ROWS, qi > 0), 0.0, NEG)
        init = tuple((jnp.full((t, 1), NEG, F32), jnp.zeros((t, 1), F32), jnp.zeros((t, DV), F32)) for _ in range(hp))
        carries = tile(0, init, pad_bias)
        carries = lax.fori_loop(1, qi, lambda kj, c: tile(kj, c), carries)
        carries = tile(qi, carries, jnp.where(_attn_mask(qi * t, qi * t, t, t), 0.0, NEG))
        for a in range(hp):
            m, l, acc = carries[a]
            o_ref[:, a * DV:(a + 1) * DV] = acc / l
            lse_ref[a] = jnp.broadcast_to(m + jnp.log(l) * LOG2E, (t, 128)).T[:8]
        if nr:
            pl.when(step == steps - 1)(lambda: _ride_gather(bufs, *sems, 2))

    sems = [pltpu.SemaphoreType.DMA((6 * nr,)), pltpu.SemaphoreType.DMA((6 * nr,))] if nr else []
    outs = pl.pallas_call(
        body, name="attn_fwd", grid=(MLA_H // hp, nq),
        in_specs=[pl.BlockSpec((hp, t, 256), lambda h, i: (h, i, 0)), pl.BlockSpec((hp, lp, 256), lambda h, i: (h, 0, 0)),
                  pl.BlockSpec((hp, lp, 128), lambda h, i: (h, 0, 0))] + [ANY] * nr,
        out_specs=[pl.BlockSpec((t, hp * DV), lambda h, i: (i, h)), pl.BlockSpec((hp, 8, t), lambda h, i: (h, 0, i))] + [ANY] * nr,
        out_shape=[_sds((lp, MLA_H * DV), F32), _sds((MLA_H, 8, lp), F32)] + [_sds(b.shape, b.dtype) for b in ride],
        input_output_aliases={3 + w: 2 + w for w in range(nr)}, scratch_shapes=sems,
        compiler_params=_cp("arbitrary", "arbitrary"))(q, k, v, *ride)
    return outs[0], outs[1], list(outs[2:])


def _mixout_bwd(mix, g_post, dh1, o, g_ao, w_out, ride=()):
    lp = o.shape[0]
    tr = _rt(lp, MM_ROWS)
    nr = len(ride)
    steps = lp // tr

    def body(mix_ref, gp_ref, dh_ref, o_ref, g_ref, w_ref, *rest):
        gs = rest[:nr]
        dmix_ref, dssm_ref, do_ref, dgp_ref, dg_ref, dl_ref = rest[nr:nr + 6]
        from_sib, sems = rest[nr + 6:2 * nr + 6], rest[2 * nr + 6:]
        i = pl.program_id(0)
        if nr:
            pl.when(i == 0)(lambda: _ride_sibling(gs, from_sib, *sems, 0))
        grow = i * tr + lax.broadcasted_iota(jnp.int32, (tr, D), 0)
        dmix, dgp = _rms_bwd(mix_ref[...], gp_ref[...], jnp.where(grow >= PAD_ROWS, dh_ref[...], 0.0))
        dmix = dmix.astype(BF16)
        dmix_ref[...] = dmix
        _acc_rows(dgp_ref, dgp, i == 0)
        dssm_ref[...] = _dot_nt(dmix, w_ref[D:, :])
        ov = o_ref[...]
        do, dg = _rms_bwd(ov, g_ref[...], _dot_nt(dmix, w_ref[:D, :]))
        do_ref[...] = do
        _acc_rows(dg_ref, dg, i == 0)
        prod = do * ov
        lane = lax.broadcasted_iota(jnp.int32, (1, 128), 1)
        cols = jnp.zeros((tr, 128), F32)
        for h in range(MLA_H):
            cols = cols + jnp.sum(prod[:, h * DV:(h + 1) * DV], axis=-1, keepdims=True) * (lane == h).astype(F32)
        dl_ref[...] = cols.T[:MLA_H]
        if nr:
            pl.when(i == steps - 1)(lambda: _ride_sibling(gs, from_sib, *sems, 1))

    sems = [pltpu.SemaphoreType.DMA((nr,)), pltpu.SemaphoreType.DMA((nr,))] if nr else []
    outs = pl.pallas_call(
        body, name="mixout_bwd", grid=(steps,),
        in_specs=[_rows(tr, D), _full((1, D)), _rows(tr, D), _rows(tr, D), _full((1, D)), _full(w_out.shape)] + [ANY] * nr,
        out_specs=[_rows(tr, D), _rows(tr, D), _rows(tr, D), _full((1, D)), _full((1, D)),
                   pl.BlockSpec((MLA_H, tr), lambda i: (0, i))] + [ANY] * nr,
        out_shape=[_sds((lp, D), BF16), _sds((lp, D), F32), _sds((lp, D), F32), _sds((1, D), F32), _sds((1, D), F32),
                   _sds((MLA_H, lp), F32)] + [_sds((g.shape[0],) + g.shape[2:], g.dtype) for g in ride],
        scratch_shapes=sems, compiler_params=_cp("arbitrary"))(mix, g_post, dh1, o, g_ao, w_out, *ride)
    return tuple(outs[:6]) + (list(outs[6:]),)


def _attn_bwd(q, k, v, do, lse_row, delta_row, ride=()):
    lp = q.shape[1]
    t = _rt(lp, (640, 128))
    nq = lp // t

    nr = len(ride)

    def body(q_ref, k_ref, v_ref, do_ref, lse_ref, dl_ref, *rest):
        ps = rest[:nr]
        dq_ref, dk_ref, dv_ref = rest[nr:nr + 3]
        got, sems = rest[nr + 3:2 * nr + 3], rest[2 * nr + 3:]
        kj = pl.program_id(1)
        step = pl.program_id(0) * nq + kj
        if nr:
            pl.when(step == 0)(lambda: _ride_exchange(ps, got, *sems, 0))
        kk = k_ref[0]
        vv = v_ref[0]

        @pl.when(kj == 0)
        def _():
            dq_ref[...] = jnp.zeros_like(dq_ref)

        def tile(qi, carry, masked):
            dk, dv = carry
            q_rows = pl.ds(pl.multiple_of(qi * t, t), t)
            qv = q_ref[0, q_rows, :]
            dob = do_ref[q_rows, :].astype(BF16)
            st = _dot_nt(kk, qv)
            if masked:
                st = jnp.where(_attn_mask(qi * t, kj * t, t, t, transposed=True), st, NEG)
            pt = jnp.exp2(st - lse_ref[0, qi])
            dpt = _dot_nt(vv, dob)
            dst = (pt * (dpt - dl_ref[0, qi])).astype(BF16)
            dv = dv + _dot(pt.astype(BF16), dob)
            dk = dk + _dot(dst, qv)
            dq_ref[0, q_rows, :] += _dot_tn(dst, kk)
            return dk, dv

        carry = tile(kj, (jnp.zeros((t, 256), F32), jnp.zeros((t, DV), F32)), True)
        split = jnp.where(kj == 0, nq, kj + 1)
        carry = lax.fori_loop(kj + 1, split, lambda qi, c: tile(qi, c, True), carry)
        dk, dv = lax.fori_loop(split, nq, lambda qi, c: tile(qi, c, False), carry)
        dk_ref[0] = dk * LN2
        dv_ref[0] = dv
        if nr:
            pl.when(step == MLA_H * nq - 1)(lambda: _ride_exchange(ps, got, *sems, 1))

    stat = pl.BlockSpec((1, nq, 1, t), lambda h, j: (h, 0, 0, 0))
    sems = [pltpu.SemaphoreType.DMA((3 * nr,)), pltpu.SemaphoreType.DMA((3 * nr,))] if nr else []
    outs = pl.pallas_call(
        body, name="attn_bwd", grid=(MLA_H, nq),
        in_specs=[pl.BlockSpec((1, lp, 256), lambda h, j: (h, 0, 0)), pl.BlockSpec((1, t, 256), lambda h, j: (h, j, 0)),
                  pl.BlockSpec((1, t, 128), lambda h, j: (h, j, 0)), pl.BlockSpec((lp, DV), lambda h, j: (0, h)), stat, stat]
        + [ANY] * nr,
        out_specs=[pl.BlockSpec((1, lp, 256), lambda h, j: (h, 0, 0)), pl.BlockSpec((1, t, 256), lambda h, j: (h, j, 0)),
                   pl.BlockSpec((1, t, 128), lambda h, j: (h, j, 0))] + [ANY] * nr,
        out_shape=[_sds((MLA_H, lp, 256), F32), _sds((MLA_H, lp, 256), F32), _sds((MLA_H, lp, 128), F32)]
        + [_sds((3,) + p.shape[1:], p.dtype) for p in ride],
        scratch_shapes=sems, compiler_params=_cp("arbitrary", "arbitrary"))(q, k, v, do, lse_row, delta_row, *ride)
    return outs[0], outs[1], outs[2], list(outs[3:])


def _ssd_consts():
    ri = lax.broadcasted_iota(jnp.int32, (CHUNK, CHUNK), 0)
    ci = lax.broadcasted_iota(jnp.int32, (CHUNK, CHUNK), 1)
    expand = (lax.broadcasted_iota(jnp.int32, (128, D_SSM), 0)
              == lax.broadcasted_iota(jnp.int32, (128, D_SSM), 1) // SSM_P).astype(F32)
    return ri, ci, expand


def _ssd_chunk(c, x_ref, xh_ref, dt_ref, dtT_ref, cw_ref, cb_ref, dtb_ref, dtbT_ref, al_ref, alT_ref):
    ri, ci, expand = _ssd_consts()
    x = x_ref[...]
    halo = jnp.where(c > 0, xh_ref[...], 0.0)
    sh = [x] + [_shift_down(x, halo, j) for j in range(1, SSM_K)]
    cv = cb_ref[...]
    for kk in range(SSM_K):
        cv = cv + cw_ref[kk:kk + 1, :] * sh[SSM_K - 1 - kk]
    xa = _silu(cv)
    grow = c * CHUNK + ri
    gcol = c * CHUNK + lax.broadcasted_iota(jnp.int32, (SSM_H, CHUNK), 1)
    sp = dt_ref[...] + dtb_ref[...]
    spT = dtT_ref[...] + dtbT_ref[...]
    dtc = jnp.where(grow >= PAD_ROWS, _softplus(sp), 0.0)
    dtr = jnp.where(gcol >= PAD_ROWS, _softplus(spT), 0.0)
    arow = -jnp.exp(al_ref[...])
    acolT = -jnp.exp(alT_ref[...])
    ltri = (ci <= ri).astype(F32)
    acs = _dot_hi(ltri, dtc * arow, split="b")
    acsT = _dot_hi(dtr * acolT, (ri <= ci).astype(F32))
    return dict(x=x, sh=sh, cv=cv, xa=xa, sp=sp, dtc=dtc, arow=arow, acs=acs, acsT=acsT, ri=ri, ci=ci, expand=expand,
                grow=grow)


def _ssd_mats(k, s_prev):
    xa, acs, acsT, expand, ri, ci = k["xa"], k["acs"], k["acsT"], k["expand"], k["ri"], k["ci"]
    xs = xa[:, :D_SSM]
    dt_e = _dot_hi(k["dtc"], expand)
    acs_e = _dot_hi(acs, expand)
    last_e = acs_e[CHUNK - 1:CHUNK, :]
    ea = jnp.exp(acs_e)
    f = jnp.exp(last_e - acs_e)
    cd = jnp.exp(last_e)
    xdt = xs * dt_e
    bm = [xa[:, D_SSM + g * SSM_N:D_SSM + (g + 1) * SSM_N] for g in range(SSM_G)]
    cm = [xa[:, D_SSM + (SSM_G + g) * SSM_N:D_SSM + (SSM_G + g + 1) * SSM_N] for g in range(SSM_G)]
    bmb = [b.astype(BF16) for b in bm]
    cmb = [cc.astype(BF16) for cc in cm]
    cb = [_dot_nt(cmb[g], bmb[g]) for g in range(SSM_G)]
    lam, mm = [], []
    causal = jnp.where(ci <= ri, 0.0, NEG)
    for h in range(SSM_H):
        lam_h = jnp.exp((acs[:, h:h + 1] - acsT[h:h + 1, :]) + causal)
        lam.append(lam_h)
        mm.append(cb[h // (SSM_H // SSM_G)] * lam_h)
    lo = lax.broadcasted_iota(jnp.int32, (CHUNK, 128), 1) < SSM_P
    xdt_h = []
    for h in range(SSM_H):
        pair = xdt[:, (h // 2) * 128:(h // 2 + 1) * 128]
        xdt_h.append(jnp.where(lo if h % 2 == 0 else jnp.logical_not(lo), pair, 0.0).astype(BF16))
    ydiag = jnp.concatenate(
        [_dot(mm[2 * j].astype(BF16), xdt_h[2 * j]) + _dot(mm[2 * j + 1].astype(BF16), xdt_h[2 * j + 1])
         for j in range(SSM_H // 2)], axis=1)
    t_off = [_dot(cmb[g], s_prev[g].astype(BF16)) for g in range(SSM_G)]
    yoff = jnp.concatenate(t_off, axis=1) * ea
    return dict(xs=xs, dt_e=dt_e, acs_e=acs_e, ea=ea, f=f, cd=cd, xdt=xdt, bm=bm, cm=cm, bmb=bmb, cmb=cmb, cb=cb, lam=lam,
                mm=mm, lo=lo, xdt_h=xdt_h, ydiag=ydiag, t_off=t_off, yoff=yoff)


def _ssd_specs(nc, rev):
    ix = (lambda i: nc - 1 - i) if rev else (lambda i: i)
    return [
        pl.BlockSpec((CHUNK, D_XBC), lambda i: (ix(i), 0)),
        pl.BlockSpec((8, D_XBC), lambda i: (jnp.maximum(ix(i) * (CHUNK // 8) - 1, 0), 0)),
        pl.BlockSpec((CHUNK, D_SSM), lambda i: (ix(i), 0)),
        pl.BlockSpec((CHUNK, 128), lambda i: (ix(i), 0)),
        pl.BlockSpec((SSM_H, CHUNK), lambda i: (0, ix(i))),
        _full((8, D_XBC)), _full((1, D_XBC)), _full((1, 128)), _full((SSM_H, 1)), _full((1, 128)), _full((SSM_H, 1)),
        _full((1, D_SSM)), _full((1, D_SSM)),
    ]


def _ssd_fwd(xbc, z, dtr, dtrT, cw, cb, dtb, dtbT, alog, alogT, d_e, ng):
    lp = xbc.shape[0]
    nc = lp // CHUNK

    def body(x_ref, xh_ref, z_ref, dt_ref, dtT_ref, cw_ref, cb_ref, dtb_ref, dtbT_ref, al_ref, alT_ref, de_ref, ng_ref,
             y_ref, st_ref, s_scr):
        c = pl.program_id(0)

        @pl.when(c == 0)
        def _():
            s_scr[...] = jnp.zeros_like(s_scr)

        k = _ssd_chunk(c, x_ref, xh_ref, dt_ref, dtT_ref, cw_ref, cb_ref, dtb_ref, dtbT_ref, al_ref, alT_ref)
        s_prev = [s_scr[g] for g in range(SSM_G)]
        st_ref[0] = s_scr[...]
        m = _ssd_mats(k, s_prev)
        xd = (m["xdt"] * m["f"]).astype(BF16)
        for g in range(SSM_G):
            sl = slice(g * GSZ, (g + 1) * GSZ)
            s_scr[g] = m["cd"][:, sl] * s_prev[g] + _dot(m["bm"][g].T.astype(BF16), xd[:, sl])
        y = m["ydiag"] + m["yoff"] + de_ref[...] * m["xs"]
        u = y * _silu(z_ref[...])
        outs = []
        for g in range(SSM_G):
            ug = u[:, g * GSZ:(g + 1) * GSZ]
            outs.append(ug * lax.rsqrt(jnp.mean(ug * ug, axis=-1, keepdims=True) + EPS))
        y_ref[...] = jnp.concatenate(outs, axis=1) * ng_ref[...]

    return pl.pallas_call(
        body, name="ssd_fwd", grid=(nc,), in_specs=_ssd_specs(nc, False),
        out_specs=[_rows(CHUNK, D_SSM), pl.BlockSpec((1, SSM_G, SSM_N, GSZ), lambda i: (i, 0, 0, 0))],
        out_shape=[_sds((lp, D_SSM), F32), _sds((nc, SSM_G, SSM_N, GSZ), F32)],
        scratch_shapes=[pltpu.VMEM((SSM_G, SSM_N, GSZ), F32)],
        compiler_params=_cp("arbitrary"))(xbc, xbc, z, dtr, dtrT, cw, cb, dtb, dtbT, alog, alogT, d_e, ng)


def _ssd_bwd(dssm, xbc, z, dtr, dtrT, st, cw, cb, dtb, dtbT, alog, alogT, d_e, ng):
    lp = xbc.shape[0]
    nc = lp // CHUNK
    hpg = SSM_H // SSM_G

    def body(dy_ref, x_ref, xh_ref, z_ref, dt_ref, dtT_ref, st_ref, cw_ref, cb_ref, dtb_ref, dtbT_ref, al_ref, alT_ref,
             de_ref, ng_ref, dz_ref, dx_ref, ddt_ref, dcw_ref, dcb_ref, ddtb_ref, dal_ref, dd_ref, dng_ref, ds_scr, nx_scr):
        i = pl.program_id(0)
        c = nc - 1 - i
        first = i == 0

        @pl.when(first)
        def _():
            ds_scr[...] = jnp.zeros_like(ds_scr)
            nx_scr[...] = jnp.zeros_like(nx_scr)

        k = _ssd_chunk(c, x_ref, xh_ref, dt_ref, dtT_ref, cw_ref, cb_ref, dtb_ref, dtbT_ref, al_ref, alT_ref)
        s_prev = [st_ref[0, g] for g in range(SSM_G)]
        m = _ssd_mats(k, s_prev)
        ri, ci, expand = k["ri"], k["ci"], k["expand"]
        xs, acs, acsT = m["xs"], k["acs"], k["acsT"]
        zv = z_ref[...]
        dout = dy_ref[...]
        ngv = ng_ref[...]
        y = m["ydiag"] + m["yoff"] + de_ref[...] * xs
        sz = _silu(zv)
        u = y * sz
        du_parts, dng_parts = [], []
        for g in range(SSM_G):
            sl = slice(g * GSZ, (g + 1) * GSZ)
            dug, dngg = _rms_bwd(u[:, sl], ngv[:, sl], dout[:, sl])
            du_parts.append(dug)
            dng_parts.append(dngg)
        du = jnp.concatenate(du_parts, axis=1)
        _acc_rows(dng_ref, jnp.concatenate(dng_parts, axis=1), first)
        dy = du * sz
        dz_ref[...] = du * y * _dsilu(zv)
        dd_e = jnp.sum(dy * xs, axis=0, keepdims=True)
        _acc_rows(dd_ref, _dot_nt_hi(dd_e, expand), first)
        dxs = de_ref[...] * dy
        dacs_e = dy * m["yoff"]
        dtg = (dy * m["ea"]).astype(BF16)
        dxdt = jnp.zeros_like(xs)
        dlast_e = []
        db, dc, ds_prev = [], [], []
        xd = m["xdt"] * m["f"]
        dxd_all = []
        for g in range(SSM_G):
            sl = slice(g * GSZ, (g + 1) * GSZ)
            dsg = ds_scr[g]
            spb = s_prev[g].astype(BF16)
            dc.append(_dot_nt(dtg[:, sl], spb))
            dsp = _dot(m["cm"][g].T.astype(BF16), dtg[:, sl]) + m["cd"][:, sl] * dsg
            ds_prev.append(dsp)
            dlast_e.append(jnp.sum(dsg * s_prev[g], axis=0, keepdims=True) * m["cd"][:, sl])
            dsb = dsg.astype(BF16)
            db.append(_dot_nt(xd[:, sl].astype(BF16), dsb))
            dxd_all.append(_dot(m["bmb"][g], dsb))
        dxd = jnp.concatenate(dxd_all, axis=1)
        dxdt = dxd * m["f"]
        dff = dxd * xd
        dacs_e = dacs_e - dff
        dlast_row = jnp.concatenate(dlast_e, axis=1) + jnp.sum(dff, axis=0, keepdims=True)
        dacs = jnp.zeros((CHUNK, 128), F32)
        dacs_t = jnp.zeros((CHUNK, CHUNK), F32)
        lane = lax.broadcasted_iota(jnp.int32, (1, 128), 1)
        dgs = [jnp.zeros((CHUNK, CHUNK), F32) for _ in range(SSM_G)]
        dxdt_pairs = []
        for h in range(SSM_H):
            g = h // hpg
            pr = slice((h // 2) * 128, (h // 2 + 1) * 128)
            lo_h = m["lo"] if h % 2 == 0 else jnp.logical_not(m["lo"])
            dyp = jnp.where(lo_h, dy[:, pr], 0.0).astype(BF16)
            dm = _dot_nt(dyp, m["xdt"][:, pr].astype(BF16))
            dgs[g] = dgs[g] + dm * m["lam"][h]
            w_h = dm * m["mm"][h]
            dacs = dacs + jnp.sum(w_h, axis=1, keepdims=True) * (lane == h).astype(F32)
            dacs_t = dacs_t + jnp.where(ri == h, jnp.sum(w_h, axis=0, keepdims=True), 0.0)
            part = _dot_tn(m["mm"][h].astype(BF16), dyp)
            if h % 2 == 0:
                dxdt_pairs.append(part)
            else:
                dxdt_pairs[-1] = dxdt_pairs[-1] + part
        dxdt = dxdt + jnp.concatenate(dxdt_pairs, axis=1)
        for g in range(SSM_G):
            dgb = dgs[g].astype(BF16)
            dc[g] = dc[g] + _dot(dgb, m["bmb"][g])
            db[g] = db[g] + _dot_tn(dgb, m["cmb"][g])
        dacs = dacs - dacs_t.T + _dot_nt_hi(dacs_e, expand)
        dlast = _dot_nt_hi(dlast_row, expand)
        dacs = dacs + jnp.where(ri == CHUNK - 1, dlast, 0.0)
        dxs = dxs + dxdt * m["dt_e"]
        ddt = _dot_nt_hi(dxdt * xs, expand)
        da = _dot_hi((ri <= ci).astype(F32), dacs, split="b")
        ddt = ddt + da * k["arow"]
        dA = jnp.sum(da * k["dtc"], axis=0, keepdims=True)
        _acc_rows(dal_ref, dA * k["arow"], first)
        ddtr = jnp.where(k["grow"] >= PAD_ROWS, ddt * _sigmoid(k["sp"]), 0.0)
        ddt_ref[...] = ddtr
        _acc_rows(ddtb_ref, jnp.sum(ddtr, axis=0, keepdims=True), first)
        for g in range(SSM_G):
            ds_scr[g] = ds_prev[g]
        dxa = jnp.concatenate([dxs] + db + dc, axis=1)
        dcv = dxa * _dsilu(k["cv"])
        _acc_rows(dcb_ref, jnp.sum(dcv, axis=0, keepdims=True), first)
        dcw_rows = [jnp.sum(dcv * k["sh"][SSM_K - 1 - kk], axis=0, keepdims=True) for kk in range(SSM_K)]
        dcw_rows.append(jnp.zeros((8 - SSM_K, D_XBC), F32))
        _acc_rows(dcw_ref, jnp.concatenate(dcw_rows, axis=0), first)
        nxt = nx_scr[...]
        dx = cw_ref[SSM_K - 1:SSM_K, :] * dcv
        for j in range(1, SSM_K):
            dx = dx + cw_ref[SSM_K - 1 - j:SSM_K - j, :] * _shift_up(dcv, nxt, j)
        grow_x = c * CHUNK + lax.broadcasted_iota(jnp.int32, (CHUNK, D_XBC), 0)
        dx_ref[...] = jnp.where(grow_x >= PAD_ROWS, dx, 0.0)
        nx_scr[...] = dcv[:8]

    specs = _ssd_specs(nc, True)
    in_specs = [pl.BlockSpec((CHUNK, D_SSM), lambda i: (nc - 1 - i, 0))] + specs[:5] + [
        pl.BlockSpec((1, SSM_G, SSM_N, GSZ), lambda i: (nc - 1 - i, 0, 0, 0))] + specs[5:]
    rv = lambda w: pl.BlockSpec((CHUNK, w), lambda i: (nc - 1 - i, 0))
    return pl.pallas_call(
        body, name="ssd_bwd", grid=(nc,), in_specs=in_specs,
        out_specs=[rv(D_SSM), rv(D_XBC), rv(128), _full((8, D_XBC)), _full((1, D_XBC)), _full((1, 128)), _full((1, 128)),
                   _full((1, 128)), _full((1, D_SSM))],
        out_shape=[_sds((lp, D_SSM), F32), _sds((lp, D_XBC), F32), _sds((lp, 128), F32), _sds((8, D_XBC), F32),
                   _sds((1, D_XBC), F32), _sds((1, 128), F32), _sds((1, 128), F32), _sds((1, 128), F32), _sds((1, D_SSM), F32)],
        scratch_shapes=[pltpu.VMEM((SSM_G, SSM_N, GSZ), F32), pltpu.VMEM((8, D_XBC), F32)],
        compiler_params=_cp("arbitrary"))(dssm, xbc, xbc, z, dtr, dtrT, st, cw, cb, dtb, dtbT, alog, alogT, d_e, ng)


def _mixout_fwd(o, ssm, h0, g_ao, g_post, w):
    lp = o.shape[0]
    tr = _rt(lp, MM_ROWS)

    def body(o_ref, s_ref, h_ref, ga_ref, gp_ref, w_ref, mi_ref, mix_ref, h1_ref):
        mixin = jnp.concatenate([_rms(o_ref[...], ga_ref[...]), s_ref[...]], axis=1).astype(BF16)
        mi_ref[...] = mixin
        mix = _dot(mixin, w_ref[...])
        mix_ref[...] = mix
        grow = pl.program_id(0) * tr + lax.broadcasted_iota(jnp.int32, (tr, D), 0)
        h1_ref[...] = h_ref[...] + jnp.where(grow >= PAD_ROWS, _rms(mix, gp_ref[...]), 0.0)

    return pl.pallas_call(
        body, name="mixout_fwd", grid=(lp // tr,),
        in_specs=[_rows(tr, D), _rows(tr, D), _rows(tr, D), _full((1, D)), _full((1, D)), _full(w.shape)],
        out_specs=[_rows(tr, 2 * D), _rows(tr, D), _rows(tr, D)],
        out_shape=[_sds((lp, 2 * D), BF16), _sds((lp, D), F32), _sds((lp, D), F32)],
        compiler_params=_cp("parallel"))(o, ssm, h0, g_ao, g_post, w)


def _ffn_up(h1, g, w):
    lp = h1.shape[0]
    tr = _rt(lp, MM_ROWS)
    tn = D_FF // 2

    def body(h_ref, g_ref, w_ref, hn_ref, u_ref):
        @pl.when(pl.program_id(1) == 0)
        def _():
            hn_ref[...] = _rms(h_ref[...], g_ref[...]).astype(BF16)

        u_ref[...] = _dot(hn_ref[...], w_ref[...]).astype(BF16)

    return pl.pallas_call(
        body, name="ffn_up", grid=(lp // tr, 2 * D_FF // tn),
        in_specs=[pl.BlockSpec((tr, D), lambda i, j: (i, 0)), _full((1, D)), pl.BlockSpec((None, D, tn), lambda i, j: (j, 0, 0))],
        out_specs=[pl.BlockSpec((tr, D), lambda i, j: (i, 0)), pl.BlockSpec((tr, tn), lambda i, j: (i, j))],
        out_shape=[_sds((lp, D), BF16), _sds((lp, 2 * D_FF), BF16)],
        compiler_params=_cp("parallel", "arbitrary"))(h1, g, w)


def _ffn_in_bwd(du, w4, h1, g, dh2):
    lp = du.shape[0]
    nch, _, tn = w4.shape
    tr = _rt(lp, (320, 128))

    def body(du_ref, w_ref, h_ref, g_ref, r_ref, o_ref, dg_ref):
        acc = _dot_nt(du_ref[:, 0:tn], w_ref[0])
        for j in range(1, nch):
            acc = acc + _dot_nt(du_ref[:, j * tn:(j + 1) * tn], w_ref[j])
        dx, dg = _rms_bwd(h_ref[...], g_ref[...], acc)
        o_ref[...] = dx + r_ref[...]
        _acc_rows(dg_ref, dg, pl.program_id(0) == 0)

    return pl.pallas_call(
        body, name="ffn_in_bwd", grid=(lp // tr,),
        in_specs=[_rows(tr, nch * tn), _full(w4.shape), _rows(tr, D), _full((1, D)), _rows(tr, D)],
        out_specs=[_rows(tr, D), _full((1, D))], out_shape=[_sds((lp, D), F32), _sds((1, D), F32)],
        compiler_params=_cp("arbitrary"))(du, w4, h1, g, dh2)


FFN_CB = 256


def _ffn_gate(u, cw, cb):
    lp = u.shape[0]
    tr = _rt(lp, (320, 128))

    def body(u_ref, uh_ref, cw_ref, cb_ref, uc_ref, a_ref):
        i = pl.program_id(0)
        for j in range(D_FF // FFN_CB):
            halves = []
            for off in (0, D_FF):
                sl = slice(off + j * FFN_CB, off + (j + 1) * FFN_CB)
                x = u_ref[:, sl].astype(F32)
                halo = jnp.where(i > 0, uh_ref[8:16, sl].astype(F32), 0.0)
                cv = cb_ref[:, sl] + cw_ref[FFN_K - 1:FFN_K, sl] * x
                for s in range(1, FFN_K):
                    cv = cv + cw_ref[FFN_K - 1 - s:FFN_K - s, sl] * _shift_down(x, halo, s)
                uc_ref[:, sl] = cv.astype(BF16)
                halves.append(cv)
            a_ref[:, j * FFN_CB:(j + 1) * FFN_CB] = (_silu(halves[0]) * halves[1]).astype(BF16)

    return pl.pallas_call(
        body, name="ffn_gate", grid=(lp // tr,),
        in_specs=[_rows(tr, 2 * D_FF), pl.BlockSpec((16, 2 * D_FF), lambda i: (jnp.maximum(i * (tr // 16) - 1, 0), 0)),
                  _full((8, 2 * D_FF)), _full((1, 2 * D_FF))],
        out_specs=[_rows(tr, 2 * D_FF), _rows(tr, D_FF)], out_shape=[_sds((lp, 2 * D_FF), BF16), _sds((lp, D_FF), BF16)],
        compiler_params=_cp("parallel"))(u, u, cw, cb)


def _ffn_down(a, w, h1, tgt, g_post):
    lp = a.shape[0]
    tr = _rt(lp, MM_ROWS)
    nb = tr // FRONT

    def body(a_ref, w_ref, h_ref, *rest):
        t_refs, (g_ref, dh2_ref, dd_ref, dg_ref, loss_ref) = rest[:nb], rest[nb:]
        i = pl.program_id(0)
        d = _dot(a_ref[...], w_ref[...])
        gv = g_ref[...]
        h2 = h_ref[...] + _rms(d, gv)
        grow = i * tr + lax.broadcasted_iota(jnp.int32, (tr, D), 0)
        tgt_v = jnp.concatenate([r[...] for r in t_refs], axis=0)
        err = jnp.where(grow >= FRONT, h2 - tgt_v, 0.0)
        dh2 = err * (1.0 / D)
        dh2_ref[...] = dh2
        dd, dg = _rms_bwd(d, gv, dh2)
        dd_ref[...] = dd.astype(BF16)
        _acc_rows(dg_ref, dg, i == 0)
        part = 0.5 * jnp.sum(jnp.sum(err * err, axis=1, keepdims=True), axis=0, keepdims=True) * (1.0 / D)
        _acc_rows(loss_ref, jnp.broadcast_to(part, (8, 128)), i == 0)

    return pl.pallas_call(
        body, name="ffn_down", grid=(lp // tr,),
        in_specs=[_rows(tr, D_FF), _full(w.shape), _rows(tr, D)]
        + [pl.BlockSpec((FRONT, D), functools.partial(lambda i, b: (jnp.maximum(i * nb - 1 + b, 0), 0), b=b)) for b in range(nb)]
        + [_full((1, D))],
        out_specs=[_rows(tr, D), _rows(tr, D), _full((1, D)), _full((8, 128))],
        out_shape=[_sds((lp, D), F32), _sds((lp, D), BF16), _sds((1, D), F32), _sds((8, 128), F32)],
        compiler_params=_cp("arbitrary"))(a, w, h1, *([tgt] * nb), g_post)


def _ffn_gate_bwd(u, uc, dd, w_down, cw):
    lp = u.shape[0]
    tr = _rt(lp, (320, 128))
    n = lp // tr

    def body(u_ref, uc_ref, dd_ref, wd_ref, cw_ref, du_ref, dcw_ref, dcb_ref, nx_scr):
        i = pl.program_id(0)
        t = n - 1 - i
        first = i == 0

        @pl.when(first)
        def _():
            nx_scr[...] = jnp.zeros_like(nx_scr)

        grow = t * tr + lax.broadcasted_iota(jnp.int32, (tr, FFN_CB), 0)
        ddv = dd_ref[...]
        for j in range(D_FF // FFN_CB):
            sls = [slice(off + j * FFN_CB, off + (j + 1) * FFN_CB) for off in (0, D_FF)]
            cvg, cvv = uc_ref[:, sls[0]].astype(F32), uc_ref[:, sls[1]].astype(F32)
            dav = _dot_nt(ddv, wd_ref[j * FFN_CB:(j + 1) * FFN_CB, :])
            dcv = (dav * cvv * _dsilu(cvg), dav * _silu(cvg))
            for hf in range(2):
                sl = sls[hf]
                g = dcv[hf]
                nxt = nx_scr[:, sl]
                ahead = [g] + [_shift_up(g, nxt, s) for s in range(1, FFN_K)]
                x = u_ref[:, sl].astype(F32)
                rows = [jnp.sum(x * ahead[FFN_K - 1 - kk], axis=0, keepdims=True) for kk in range(FFN_K)]
                rows.append(jnp.zeros((8 - FFN_K, FFN_CB), F32))
                upd_w = jnp.concatenate(rows, axis=0)
                upd_b = jnp.sum(g, axis=0, keepdims=True)

                @pl.when(first)
                def _():
                    dcw_ref[:, sl] = upd_w
                    dcb_ref[:, sl] = upd_b

                @pl.when(jnp.logical_not(first))
                def _():
                    dcw_ref[:, sl] += upd_w
                    dcb_ref[:, sl] += upd_b

                du = cw_ref[FFN_K - 1:FFN_K, sl] * g
                for s in range(1, FFN_K):
                    du = du + cw_ref[FFN_K - 1 - s:FFN_K - s, sl] * ahead[s]
                du_ref[:, sl] = jnp.where(grow >= PAD_ROWS, du, 0.0).astype(BF16)
                nx_scr[:, sl] = g[:8]

    wide = pl.BlockSpec((tr, 2 * D_FF), lambda i: (n - 1 - i, 0))
    return pl.pallas_call(
        body, name="ffn_gate_bwd", grid=(n,),
        in_specs=[wide, wide, pl.BlockSpec((tr, D), lambda i: (n - 1 - i, 0)), _full(w_down.shape), _full((8, 2 * D_FF))],
        out_specs=[wide, _full((8, 2 * D_FF)), _full((1, 2 * D_FF))],
        out_shape=[_sds((lp, 2 * D_FF), BF16), _sds((8, 2 * D_FF), F32), _sds((1, 2 * D_FF), F32)],
        scratch_shapes=[pltpu.VMEM((8, 2 * D_FF), F32)],
        compiler_params=_cp("arbitrary"))(u, uc, dd, w_down, cw)


def _mla_bwd(dq, dk, dv, lat, qg, kvg, wq, wkv, cos, sa, sb):
    lp = lat.shape[0]
    tr = _rt(lp, (320, 128))

    def body(dq_ref, dk_ref, dv_ref, lat_ref, qg_ref, kvg_ref, wq_ref, wkv_ref, cos_ref, sa_ref, sb_ref,
             dqf_ref, dkvf_ref, dlat_ref, dqg_ref, dkvg_ref):
        i = pl.program_id(0)
        cos_v, sa_v, sb_v = cos_ref[...], sa_ref[...], sb_ref[...]
        dkpe = jnp.zeros((tr, 128), F32)
        for h in range(MLA_H):
            dqh = dq_ref[h] * SOFTMAX_SCALE
            dqf_ref[:, h * DN:(h + 1) * DN] = dqh[:, :DN].astype(BF16)
            dqf_ref[:, D + h * 128:D + (h + 1) * 128] = _rope_t(dqh[:, DN:], cos_v, sa_v, sb_v).astype(BF16)
            dkh = dk_ref[h]
            dkvf_ref[:, h * DN:(h + 1) * DN] = dkh[:, :DN].astype(BF16)
            dkpe = dkpe + dkh[:, DN:]
            dkvf_ref[:, D + h * DV:D + (h + 1) * DV] = dv_ref[h].astype(BF16)
        dql = _dot_nt(dqf_ref[...], wq_ref[...])
        dkl = _dot_nt(dkvf_ref[...], wkv_ref[...])
        lat_v = lat_ref[...]
        dqc, dqg = _rms_bwd(lat_v[:, :QR], qg_ref[...], dql)
        dkc, dkg = _rms_bwd(lat_v[:, QR:QR + KVR], kvg_ref[...], dkl)
        dlat_ref[:, :QR] = dqc
        dlat_ref[:, QR:QR + KVR] = dkc
        dlat_ref[:, QR + KVR:] = _rope_t(dkpe, cos_v, sa_v, sb_v)
        _acc_rows(dqg_ref, dqg, i == 0)
        _acc_rows(dkvg_ref, dkg, i == 0)

    hb = lambda w: pl.BlockSpec((MLA_H, tr, w), lambda i: (0, i, 0))
    return pl.pallas_call(
        body, name="mla_bwd", grid=(lp // tr,),
        in_specs=[hb(256), hb(256), hb(128), _rows(tr, LAT_W), _full((1, QR)), _full((1, KVR)), _full(wq.shape),
                  _full(wkv.shape), _rows(tr, 128), _rows(tr, 128), _rows(tr, 128)],
        out_specs=[_rows(tr, 2 * D), _rows(tr, 2 * D), _rows(tr, LAT_W), _full((1, QR)), _full((1, KVR))],
        out_shape=[_sds((lp, 2 * D), BF16), _sds((lp, 2 * D), BF16), _sds((lp, LAT_W), F32), _sds((1, QR), F32),
                   _sds((1, KVR), F32)],
        compiler_params=_cp("arbitrary"))(dq, dk, dv, lat, qg, kvg, wq, wkv, cos, sa, sb)


def _inproj_bwd(dlat, dz, dxbc, ddt, w, h0, g, dh1, ride=()):
    lp = h0.shape[0]
    tr = _rt(lp, (320, 128))
    segs = ((0, LAT_W), (LAT_W, LAT_W + D_SSM), (LAT_W + D_SSM, LAT_W + D_SSM + D_XBC), (IN_P - 128, IN_P))
    nr = len(ride)
    steps = lp // tr

    def body(dl_ref, dz_ref, dx_ref, dt_ref, w_ref, h_ref, g_ref, r_ref, *rest):
        ps = rest[:nr]
        o_ref, dg_ref = rest[nr:nr + 2]
        got, sems = rest[nr + 2:2 * nr + 2], rest[2 * nr + 2:]
        step = pl.program_id(0)
        if nr:
            pl.when(step == 0)(lambda: _ride_exchange(ps, got, *sems, 0))
        dhn = jnp.zeros((tr, D), F32)
        for ref, (a, b) in zip((dl_ref, dz_ref, dx_ref, dt_ref), segs):
            dhn = dhn + _dot_nt(ref[...].astype(BF16), w_ref[:, a:b])
        dx, dg = _rms_bwd(h_ref[...], g_ref[...], dhn)
        o_ref[...] = dx + r_ref[...]
        _acc_rows(dg_ref, dg, step == 0)
        if nr:
            pl.when(step == steps - 1)(lambda: _ride_exchange(ps, got, *sems, 1))

    sems = [pltpu.SemaphoreType.DMA((3 * nr,)), pltpu.SemaphoreType.DMA((3 * nr,))] if nr else []
    outs = pl.pallas_call(
        body, name="inproj_bwd", grid=(steps,),
        in_specs=[_rows(tr, LAT_W), _rows(tr, D_SSM), _rows(tr, D_XBC), _rows(tr, 128), _full(w.shape), _rows(tr, D),
                  _full((1, D)), _rows(tr, D)] + [ANY] * nr,
        out_specs=[_rows(tr, D), _full((1, D))] + [ANY] * nr,
        out_shape=[_sds((lp, D), F32), _sds((1, D), F32)] + [_sds((3,) + p.shape[1:], p.dtype) for p in ride],
        scratch_shapes=sems, compiler_params=_cp("arbitrary"))(dlat, dz, dxbc, ddt, w, h0, g, dh1, *ride)
    return outs[0], outs[1], list(outs[2:])


def _rope_tables(lp):
    pos = (jnp.arange(lp, dtype=jnp.int32) - PAD_ROWS).astype(F32)
    inv = ROPE_THETA ** (-jnp.arange(0, DR, 2, dtype=F32) / DR)
    ang = pos[:, None] * inv[None, :]
    cos, sin = jnp.cos(ang), jnp.sin(ang)
    zero = jnp.zeros_like(sin)
    cos128 = jnp.concatenate([cos, cos, cos, cos], axis=1)
    sa128 = jnp.concatenate([-sin, zero, -sin, zero], axis=1)
    sb128 = jnp.concatenate([zero, sin, zero, sin], axis=1)
    return cos128, sa128, sb128


def _pad_rows8(w):
    return jnp.concatenate([w, jnp.zeros((8 - w.shape[0], w.shape[1]), w.dtype)], axis=0)


def _lane_pad(v):
    return jnp.concatenate([v, jnp.zeros((v.shape[0], 128 - v.shape[1]), v.dtype)], axis=1)


def _late_weights(bufs):
    w_out, w_up, w_down = bufs
    return dict(w_out=w_out.reshape(2 * D, D), w_up=w_up.reshape(N_CHIPS, D, 2 * D_FF // N_CHIPS), w_down=w_down.reshape(D_FF, D))


def _device_step(x, tgt, meta, p, late_bufs=(), early_reduce=None, last_reduce=None):
    s = x.shape[0]
    lp = s + FRONT
    zpad = jnp.zeros((PAD_ROWS, D), F32)
    h0 = jnp.concatenate([zpad, meta, x], axis=0)
    cos, sa, sb = _rope_tables(lp)

    w_in = p["w_in"]
    w_in_p = jnp.concatenate([w_in[:, :QR + KVR + DR], jnp.zeros((D, 64), BF16), w_in[:, QR + KVR + DR:],
                              jnp.zeros((D, 128 - SSM_H), BF16)], axis=1)
    w_uq = p["w_uq"]
    wq_p = jnp.concatenate([w_uq[:, :, :DN].reshape(QR, MLA_H * DN),
                            jnp.concatenate([w_uq[:, :, DN:], jnp.zeros((QR, MLA_H, 128 - DR), BF16)], axis=2).reshape(QR, MLA_H * 128)],
                           axis=1)
    w_ukv = p["w_ukv"]
    wkv_p = jnp.concatenate([w_ukv[:, :, :DN].reshape(KVR, MLA_H * DN), w_ukv[:, :, DN:].reshape(KVR, MLA_H * DV)], axis=1)
    scw = _pad_rows8(p["ssm_conv_w"])
    fcw = _pad_rows8(p["ffn_conv_w"])
    dtb, alog = _lane_pad(p["ssm_dt_bias"]), _lane_pad(p["ssm_A_log"])
    dtbT, alogT = p["ssm_dt_bias"].reshape(SSM_H, 1), p["ssm_A_log"].reshape(SSM_H, 1)
    d_e = jnp.repeat(p["ssm_D"], SSM_P, axis=1)

    hn, lat, z, xbc, dtr = _inproj(h0, p["norm_mix_pre"], w_in_p)
    dtrT = dtr[:, :SSM_H].T
    q, k, v, qlat, kvlat = _mla_prep(lat, p["q_a_norm"], p["kv_a_norm"], wq_p, wkv_p, cos, sa, sb)
    o, lse, gathered = _attn_fwd(q, k, v, ride=late_bufs)
    if late_bufs:
        p = dict(p, **_late_weights(gathered))
    ssm, st = _ssd_fwd(xbc, z, dtr, dtrT, scw, p["ssm_conv_b"], dtb, dtbT, alog, alogT, d_e, p["ssm_norm"])
    mixin, mix, h1 = _mixout_fwd(o, ssm, h0, p["attn_out_norm"], p["norm_mix_post"], p["w_out"])
    hn2, u = _ffn_up(h1, p["norm_ffn_pre"], p["w_up"])
    uc, a = _ffn_gate(u, fcw, p["ffn_conv_b"])
    dh2, dd, g_ffn_post, loss = _ffn_down(a, p["w_down"], h1, tgt, p["norm_ffn_post"])

    g_w_down = _mm_tn(a, dd, "ffn_dw_down", tn=512)
    du, g_fcw, g_fcb = _ffn_gate_bwd(u, uc, dd, p["w_down"], fcw)
    dh1, g_ffn_pre = _ffn_in_bwd(du, p["w_up"], h1, p["norm_ffn_pre"], dh2)
    g_w_up = _mm_tn(hn2, du, "ffn_dw_up", tn=D_FF // 2, chunked=True)
    ffn_gs = early_reduce.halves(dict(w_up=g_w_up, w_down=g_w_down)) if early_reduce else ()
    dmix, dssm, do, g_mix_post, g_ao, delta, ffn_sib = _mixout_bwd(mix, p["norm_mix_post"], dh1, o, p["attn_out_norm"],
                                                                  p["w_out"], ride=ffn_gs)
    g_w_out = _mm_tn(mixin, dmix, "mix_dw_out", tn=512)
    t = _rt(lp, (640, 128))
    pairs = early_reduce.pairs(dict(w_out=g_w_out), ffn_gs, ffn_sib) if early_reduce else ()
    dq, dk, dv, got = _attn_bwd(q, k, v, do, lse[:, 0, :].reshape(MLA_H, lp // t, 1, t), delta.reshape(MLA_H, lp // t, 1, t),
                                ride=pairs)
    dqf, dkvf, dlat, g_qa, g_kva = _mla_bwd(dq, dk, dv, lat, p["q_a_norm"], p["kv_a_norm"], wq_p, wkv_p, cos, sa, sb)
    g_wq_p = _mm_tn(qlat, dqf, "mla_dw_uq")
    g_wkv_p = _mm_tn(kvlat, dkvf, "mla_dw_ukv")
    dz, dxbc, ddtr, g_scw, g_scb, g_dtb, g_alog, g_dd, g_ssm_norm = _ssd_bwd(
        dssm, xbc, z, dtr, dtrT, st, scw, p["ssm_conv_b"], dtb, dtbT, alog, alogT, d_e, p["ssm_norm"])
    g_in_p = jnp.concatenate([_mm_tn(hn, dlat, "in_dw_lat"), _mm_tn(hn, dz, "in_dw_z"), _mm_tn(hn, dxbc, "in_dw_xbc"),
                              _mm_tn(hn, ddtr, "in_dw_dt")], axis=1)
    g_w_in = jnp.concatenate([g_in_p[:, :QR + KVR + DR], g_in_p[:, LAT_W:LAT_W + D_SSM + D_XBC + SSM_H]], axis=1)
    g_w_uq = jnp.concatenate([g_wq_p[:, :D].reshape(QR, MLA_H, DN), g_wq_p[:, D:].reshape(QR, MLA_H, 128)[:, :, :DR]], axis=2)
    g_w_ukv = jnp.concatenate([g_wkv_p[:, :D].reshape(KVR, MLA_H, DN), g_wkv_p[:, D:].reshape(KVR, MLA_H, DV)], axis=2)
    pairs2 = last_reduce(dict(w_in=g_w_in, w_uq=g_w_uq, w_ukv=g_w_ukv)) if last_reduce else ()
    dh0, g_mix_pre, got2 = _inproj_bwd(dlat, dz, dxbc, ddtr, w_in_p, h0, p["norm_mix_pre"], dh1, ride=pairs2)
    grads = dict(
        norm_mix_pre=g_mix_pre, norm_mix_post=g_mix_post, norm_ffn_pre=g_ffn_pre, norm_ffn_post=g_ffn_post, w_in=g_w_in,
        q_a_norm=g_qa, w_uq=g_w_uq, kv_a_norm=g_kva, w_ukv=g_w_ukv, attn_out_norm=g_ao, ssm_conv_w=g_scw[:SSM_K],
        ssm_conv_b=g_scb, ssm_dt_bias=g_dtb[:, :SSM_H], ssm_A_log=g_alog[:, :SSM_H], ssm_D=g_dd[:, :SSM_H],
        ssm_norm=g_ssm_norm, w_out=g_w_out, w_up=g_w_up, ffn_conv_w=g_fcw[:FFN_K], ffn_conv_b=g_fcb, w_down=g_w_down)
    return loss, dh0[FRONT:], dh0[PAD_ROWS:FRONT], grads, (list(pairs2) + list(pairs), list(got2) + list(got))


N_CHIPS = 4
BIG = (("w_in", (D, D_IN // N_CHIPS)), ("w_uq", (QR // N_CHIPS, MLA_H, DN + DR)), ("w_ukv", (KVR // N_CHIPS, MLA_H, DN + DV)),
       ("w_out", (2 * D // N_CHIPS, D)), ("w_up", (D, 2 * D_FF // N_CHIPS)), ("w_down", (D_FF // N_CHIPS, D)))
SMALL_SHARDED = (("meta_tokens", (N_META, D // N_CHIPS)), ("ssm_conv_w", (SSM_K, D_XBC // N_CHIPS)),
                 ("ffn_conv_w", (FFN_K, 2 * D_FF // N_CHIPS)))
SMALL_REPL = (("norm_mix_pre", D), ("norm_mix_post", D), ("norm_ffn_pre", D), ("norm_ffn_post", D), ("q_a_norm", QR),
              ("kv_a_norm", KVR), ("attn_out_norm", D), ("ssm_conv_b", D_XBC), ("ssm_dt_bias", SSM_H), ("ssm_A_log", SSM_H),
              ("ssm_D", SSM_H), ("ssm_norm", D_SSM), ("ffn_conv_b", 2 * D_FF))
ANY = pl.BlockSpec(memory_space=pl.ANY)


def _pad128(v):
    n = v.shape[0]
    return jnp.concatenate([v, jnp.zeros(((-n) % 128,), v.dtype)]) if n % 128 else v


def _pack_rows(vs, rows):
    flat = jnp.concatenate([_pad128(v.reshape(-1)) for v in vs])
    flat = jnp.concatenate([flat, jnp.zeros((rows * 128 - flat.shape[0],), flat.dtype)])
    return flat.reshape(rows, 128)


def _unpack_rows(pack, sizes):
    flat = pack.reshape(-1)
    out, off = [], 0
    for n in sizes:
        out.append(flat[off:off + n])
        off += n + (-n) % 128
    return out


def _my_place():
    return lax.axis_index("x"), lax.axis_index("y"), lax.axis_index("c")


def _other_chips(x, y):
    return [(1 - x, y), (x, 1 - y), (1 - x, 1 - y)]


def _remote(src, dst, send, recv, dev):
    return pltpu.make_async_remote_copy(src_ref=src, dst_ref=dst, send_sem=send, recv_sem=recv, device_id=dev,
                                        device_id_type=MESH)


SMALL_AG_ROWS = 80


def _gather_weights(shards, small, name):
    arrs = list(shards) + ([] if small is None else [small])
    n, nb = len(arrs), len(shards)

    def body(*refs):
        ins, outs = refs[:n], refs[n:2 * n]
        send, recv, lsem = refs[2 * n:]
        x, y, c = _my_place()
        me = 2 * x + y
        chips = _other_chips(x, y)
        slot = lambda w, chip, cc: outs[w].at[chip, cc] if w < nb else outs[w].at[chip]
        mine = lambda w: slot(w, me, c) if w < nb else ins[w]
        loc = [pltpu.make_async_copy(ins[w], outs[w].at[me], lsem.at[w - nb]) for w in range(nb, n)]
        for cp in loc:
            cp.start()
        sends = []
        for w in range(n):
            for kk, (cx, cy) in enumerate(chips):
                sends.append(_remote(mine(w), slot(w, me, c), send.at[3 * w + kk], recv.at[3 * w + kk], (cx, cy, c)))
        for cp in sends:
            cp.start()
        for w in range(nb):
            for kk, (cx, cy) in enumerate(chips):
                src = 2 * cx + cy
                _remote(mine(w), slot(w, src, c), send.at[3 * w + kk], recv.at[3 * w + kk], (cx, cy, c)).wait_recv()
                fwd = _remote(slot(w, src, c), slot(w, src, c), send.at[3 * (n + w) + kk], recv.at[3 * (n + w) + kk], (x, y, 1 - c))
                fwd.start()
                sends.append(fwd)
        for w in range(n):
            for kk, (cx, cy) in enumerate(chips):
                src = 2 * cx + cy
                if w < nb:
                    _remote(mine(w), slot(w, src, 1 - c), send.at[3 * (n + w) + kk], recv.at[3 * (n + w) + kk],
                            (x, y, 1 - c)).wait_recv()
                else:
                    _remote(ins[w], slot(w, src, c), send.at[3 * w + kk], recv.at[3 * w + kk], (cx, cy, c)).wait_recv()
        for cp in sends:
            cp.wait_send()
        for cp in loc:
            cp.wait()

    return pl.pallas_call(
        body, name=name, in_specs=[ANY] * n, out_specs=[ANY] * n,
        out_shape=[_sds(a.shape, a.dtype) for a in shards] + ([] if small is None else [_sds((N_CHIPS,) + small.shape, small.dtype)]),
        input_output_aliases={w: w for w in range(nb)},
        scratch_shapes=[pltpu.SemaphoreType.DMA((3 * (n + nb),)), pltpu.SemaphoreType.DMA((3 * (n + nb),)),
                        pltpu.SemaphoreType.DMA((max(n - nb, 1),))])(*arrs)


def _place_own(wt, chip, name):
    r, c = wt.shape
    tr = _row_tile(r, c)

    def body(c_ref, w_ref, o_ref):
        o_ref[...] = w_ref[...].astype(BF16)

    return pl.pallas_call(
        body, name=name, out_shape=_sds((N_CHIPS, r, c), BF16),
        grid_spec=pltpu.PrefetchScalarGridSpec(
            num_scalar_prefetch=1, grid=(r // tr,), in_specs=[pl.BlockSpec((tr, c), lambda i, cr: (i, 0))],
            out_specs=pl.BlockSpec((None, tr, c), lambda i, cr: (cr[0], i, 0))),
        compiler_params=_cp("parallel"))(chip, wt)


def _send_sibling_halves(gs, name):
    n = len(gs)

    def body(*refs):
        ins, outs, send, recv = refs[:n], refs[n:2 * n], refs[2 * n], refs[2 * n + 1]
        x, y, c = _my_place()
        cps = [_remote(ins[w].at[:, 1 - c], outs[w], send.at[w], recv.at[w], (x, y, 1 - c)) for w in range(n)]
        for cp in cps:
            cp.start()
        for cp in cps:
            cp.wait()

    return pl.pallas_call(
        body, name=name, in_specs=[ANY] * n, out_specs=[ANY] * n,
        out_shape=[_sds((g.shape[0],) + g.shape[2:], g.dtype) for g in gs],
        scratch_shapes=[pltpu.SemaphoreType.DMA((n,)), pltpu.SemaphoreType.DMA((n,))])(*gs)


def _ride_gather(bufs, send, recv, phase):
    n = len(bufs)
    x, y, c = _my_place()
    me = 2 * x + y
    for w in range(n):
        for kk, (cx, cy) in enumerate(_other_chips(x, y)):
            src = 2 * cx + cy
            out = lambda: _remote(bufs[w].at[me, c], bufs[w].at[me, c], send.at[3 * w + kk], recv.at[3 * w + kk], (cx, cy, c))
            fwd = lambda: _remote(bufs[w].at[src, c], bufs[w].at[src, c], send.at[3 * (n + w) + kk],
                                  recv.at[3 * (n + w) + kk], (x, y, 1 - c))
            if phase == 0:
                out().start()
            elif phase == 1:
                _remote(bufs[w].at[me, c], bufs[w].at[src, c], send.at[3 * w + kk], recv.at[3 * w + kk], (cx, cy, c)).wait_recv()
                fwd().start()
            else:
                _remote(bufs[w].at[me, c], bufs[w].at[src, 1 - c], send.at[3 * (n + w) + kk], recv.at[3 * (n + w) + kk],
                        (x, y, 1 - c)).wait_recv()
                out().wait_send()
                fwd().wait_send()


def _ride_sibling(gs, outs, send, recv, phase):
    x, y, c = _my_place()
    for w in range(len(gs)):
        cp = _remote(gs[w].at[:, 1 - c], outs[w], send.at[w], recv.at[w], (x, y, 1 - c))
        if phase == 0:
            cp.start()
        else:
            cp.wait()


def _ride_exchange(ps, outs, send, recv, phase):
    x, y, c = _my_place()
    for w in range(len(ps)):
        for kk, (cx, cy) in enumerate(_other_chips(x, y)):
            cp = _remote(ps[w].at[2 * cx + cy], outs[w].at[kk], send.at[3 * w + kk], recv.at[3 * w + kk], (cx, cy, c))
            if phase == 0:
                cp.start()
            else:
                cp.wait()


def _share_sibling(halves):
    n = len(halves)

    def body(*refs):
        outs, send, recv = refs[n:2 * n], refs[2 * n], refs[2 * n + 1]
        x, y, c = _my_place()
        cps = [_remote(outs[w].at[c], outs[w].at[c], send.at[w], recv.at[w], (x, y, 1 - c)) for w in range(n)]
        for cp in cps:
            cp.start()
        for w in range(n):
            cps[w].wait_send()
            _remote(outs[w].at[c], outs[w].at[1 - c], send.at[w], recv.at[w], (x, y, 1 - c)).wait_recv()

    return pl.pallas_call(
        body, name="share_sibling", in_specs=[ANY] * n, out_specs=[ANY] * n,
        out_shape=[_sds(h.shape, h.dtype) for h in halves], input_output_aliases={w: w for w in range(n)},
        scratch_shapes=[pltpu.SemaphoreType.DMA((n,)), pltpu.SemaphoreType.DMA((n,))])(*halves)


def _row_tile(r, c, cap=1 << 20):
    return next(t for t in range(r, 0, -1) if r % t == 0 and (t % 8 == 0 or t == r) and t * c * 4 <= cap)


def _add_pair(g, t, core, name):
    _, _, r, c = g.shape
    tr = _row_tile(r, c)

    def body(c_ref, g_ref, t_ref, o_ref):
        o_ref[...] = (g_ref[...] + t_ref[...]).astype(BF16)

    return pl.pallas_call(
        body, name=name, out_shape=_sds(t.shape, BF16),
        grid_spec=pltpu.PrefetchScalarGridSpec(
            num_scalar_prefetch=1, grid=(N_CHIPS, r // tr),
            in_specs=[pl.BlockSpec((None, None, tr, c), lambda j, i, cr: (j, cr[0], i, 0)),
                      pl.BlockSpec((None, tr, c), lambda j, i, cr: (j, i, 0))],
            out_specs=pl.BlockSpec((None, tr, c), lambda j, i, cr: (j, i, 0))),
        compiler_params=_cp("parallel", "parallel"))(core, g, t)


def _add_chips(p, got, chip, name):
    _, r, c = p.shape
    tr = _row_tile(r, c)

    def body(c_ref, p_ref, g_ref, o_ref):
        o_ref[...] = ((p_ref[...].astype(F32) + g_ref[0].astype(F32)) + g_ref[1].astype(F32)) + g_ref[2].astype(F32)

    return pl.pallas_call(
        body, name=name, out_shape=_sds((2, r, c), F32),
        grid_spec=pltpu.PrefetchScalarGridSpec(
            num_scalar_prefetch=1, grid=(r // tr,),
            in_specs=[pl.BlockSpec((None, tr, c), lambda i, cr: (cr[0], i, 0)), pl.BlockSpec((3, tr, c), lambda i, cr: (0, i, 0))],
            out_specs=pl.BlockSpec((None, tr, c), lambda i, cr: (cr[1], i, 0))),
        compiler_params=_cp("parallel"))(chip, p, got)


SMALL_AR_ROWS = 424


def _allreduce_small(v):
    def body(v_ref, o_ref, gath, send, recv):
        x, y, c = _my_place()
        me = 4 * x + 2 * y + c
        gath[me] = v_ref[...]
        cps = []
        for dd in range(1, 8):
            dx, dy, dc = dd >> 2, (dd >> 1) & 1, dd & 1
            peer = (1 - x if dx else x, 1 - y if dy else y, 1 - c if dc else c)
            cps.append(_remote(v_ref, gath.at[me], send.at[dd - 1], recv.at[dd - 1], peer))
        for cp in cps:
            cp.start()
        for cp in cps:
            cp.wait()
        acc = gath[0]
        for dev in range(1, 8):
            acc = acc + gath[dev]
        o_ref[...] = acc

    vm = pl.BlockSpec(memory_space=pltpu.VMEM)
    return pl.pallas_call(
        body, name="allreduce_small", in_specs=[vm], out_specs=vm, out_shape=_sds(v.shape, F32),
        scratch_shapes=[pltpu.VMEM((8,) + v.shape, F32), pltpu.SemaphoreType.DMA((7,)), pltpu.SemaphoreType.DMA((7,))])(v)


def _adamw(w, g, m, v, name):
    r, c = w.shape
    tr = _row_tile(r, c)

    def body(w_ref, g_ref, m_ref, v_ref, d_ref, m2_ref, v2_ref):
        gv = g_ref[...]
        m2 = ADAM_B1 * m_ref[...] + (1.0 - ADAM_B1) * gv
        v2 = ADAM_B2 * v_ref[...] + (1.0 - ADAM_B2) * jnp.square(gv)
        m_hat = m2 / (1.0 - ADAM_B1 ** ADAM_STEP)
        v_hat = v2 / (1.0 - ADAM_B2 ** ADAM_STEP)
        d_ref[...] = -ADAM_LR * (m_hat / (jnp.sqrt(v_hat) + ADAM_EPS) + ADAM_WD * w_ref[...])
        m2_ref[...] = m2
        v2_ref[...] = v2

    return pl.pallas_call(
        body, name=name, grid=(r // tr,), in_specs=[_rows(tr, c)] * 4, out_specs=[_rows(tr, c)] * 3,
        out_shape=[_sds((r, c), F32)] * 3, compiler_params=_cp("parallel"))(w, g, m, v)


WEIGHT_NAMES = ("meta_tokens", "norm_mix_pre", "norm_mix_post", "norm_ffn_pre", "norm_ffn_post", "w_in", "q_a_norm", "w_uq",
                "kv_a_norm", "w_ukv", "attn_out_norm", "ssm_conv_w", "ssm_conv_b", "ssm_dt_bias", "ssm_A_log", "ssm_D",
                "ssm_norm", "w_out", "w_up", "ffn_conv_w", "ffn_conv_b", "w_down")
SMALL_ADAM_ROWS = 192


def kernel(x, meta_tokens, norm_mix_pre, norm_mix_post, norm_ffn_pre, norm_ffn_post, w_in, q_a_norm, w_uq, kv_a_norm, w_ukv, attn_out_norm, ssm_conv_w, ssm_conv_b, ssm_dt_bias, ssm_A_log, ssm_D, ssm_norm, w_out, w_up, ffn_conv_w, ffn_conv_b, w_down, loss_target, m_meta_tokens, m_norm_mix_pre, m_norm_mix_post, m_norm_ffn_pre, m_norm_ffn_post, m_w_in, m_q_a_norm, m_w_uq, m_kv_a_norm, m_w_ukv, m_attn_out_norm, m_ssm_conv_w, m_ssm_conv_b, m_ssm_dt_bias, m_ssm_A_log, m_ssm_D, m_ssm_norm, m_w_out, m_w_up, m_ffn_conv_w, m_ffn_conv_b, m_w_down, v_meta_tokens, v_norm_mix_pre, v_norm_mix_post, v_norm_ffn_pre, v_norm_ffn_post, v_w_in, v_q_a_norm, v_w_uq, v_kv_a_norm, v_w_ukv, v_attn_out_norm, v_ssm_conv_w, v_ssm_conv_b, v_ssm_dt_bias, v_ssm_A_log, v_ssm_D, v_ssm_norm, v_w_out, v_w_up, v_ffn_conv_w, v_ffn_conv_b, v_w_down):
    args = locals()
    w = {n: args[n] for n in WEIGHT_NAMES}
    mom = {n: args["m_" + n] for n in WEIGHT_NAMES}
    var = {n: args["v_" + n] for n in WEIGHT_NAMES}
    cx, cy, cc = _my_place()
    chip = 2 * cx + cy

    two_d = {n: (shp[0], functools.reduce(lambda a, b: a * b, shp[1:])) for n, shp in BIG}
    names = [n for n, _ in BIG]
    core_i = cc.astype(jnp.int32).reshape(1)
    chip_i = chip.astype(jnp.int32).reshape(1)
    early, late = names[:3], names[3:]
    halves = lambda n, a: a.reshape(N_CHIPS, 2, two_d[n][0] // 2, two_d[n][1])
    bufs = {n: halves(n, _place_own(w[n].reshape(two_d[n]), chip_i, "place_" + n)) for n in names}
    small = _pack_rows([w[n] for n, _ in SMALL_SHARDED], SMALL_AG_ROWS)
    *gathered, small_all = _gather_weights([bufs[n] for n in early], small, "allgather_weights")
    gath = {n: a.reshape((N_CHIPS,) + two_d[n]) for n, a in zip(early, gathered)}
    p = dict(w_in=gath["w_in"].transpose(1, 0, 2).reshape(D, D_IN), w_uq=gath["w_uq"].reshape(QR, MLA_H, DN + DR),
             w_ukv=gath["w_ukv"].reshape(KVR, MLA_H, DN + DV))
    sm_parts = [_unpack_rows(small_all[j], [a * b for _, (a, b) in SMALL_SHARDED]) for j in range(N_CHIPS)]
    for i, (n, shp) in enumerate(SMALL_SHARDED):
        p[n] = jnp.concatenate([sm_parts[j][i].reshape(shp) for j in range(N_CHIPS)], axis=1)
    for n, _ in SMALL_REPL:
        p[n] = w[n]
    meta_full = p.pop("meta_tokens")

    place_i = jnp.stack([chip, cc]).astype(jnp.int32)

    def pair_sums(gd, group):
        gd = dict(gd)
        if "w_in" in gd:
            gd["w_in"] = gd["w_in"].reshape(D, N_CHIPS, D_IN // N_CHIPS).transpose(1, 0, 2)
        gs = [halves(n, gd[n]) for n in group]
        from_sib = _send_sibling_halves(gs, "reduce_sibling_" + group[0])
        return [_add_pair(gg, tt, core_i, "reduce_pair_" + n) for n, gg, tt in zip(group, gs, from_sib)]

    class LateReduce:
        @staticmethod
        def halves(gd):
            return [halves(n, gd[n]) for n in late[1:]]

        @staticmethod
        def pairs(gd, ffn_gs, ffn_sib):
            gs = [halves(late[0], gd[late[0]])]
            sib = _send_sibling_halves(gs, "reduce_sibling_" + late[0])
            return [_add_pair(gg, tt, core_i, "reduce_pair_" + n)
                    for n, gg, tt in zip(late, gs + list(ffn_gs), list(sib) + list(ffn_sib))]

    loss_part, gx, gmeta, g, (pairs, got) = _device_step(
        x[0], loss_target[0], meta_full, p, late_bufs=[bufs[n] for n in late], early_reduce=LateReduce,
        last_reduce=lambda gd: pair_sums(gd, early))

    small_names = [n for n, _ in SMALL_REPL] + ["ssm_conv_w", "ffn_conv_w"]
    small_sizes = [128] + [sz for _, sz in SMALL_REPL] + [N_META * D, SSM_K * D_XBC, FFN_K * 2 * D_FF]
    order = [n for n, _ in SMALL_REPL]
    sp = _pack_rows([loss_part[0]] + [g[n] for n in order] + [gmeta, g["ssm_conv_w"], g["ffn_conv_w"]], SMALL_AR_ROWS)
    red = _unpack_rows(_allreduce_small(sp), small_sizes)
    loss = red[0][0]
    gfull = {n: red[1 + i].reshape(1, -1) for i, n in enumerate(order)}
    n_r = len(order)
    gfull["meta_tokens"] = lax.dynamic_slice_in_dim(red[1 + n_r].reshape(N_META, D), chip * (D // N_CHIPS), D // N_CHIPS, axis=1)
    gfull["ssm_conv_w"] = lax.dynamic_slice_in_dim(red[2 + n_r].reshape(SSM_K, D_XBC), chip * (D_XBC // N_CHIPS),
                                                   D_XBC // N_CHIPS, axis=1)[None]
    gfull["ffn_conv_w"] = lax.dynamic_slice_in_dim(red[3 + n_r].reshape(FFN_K, 2 * D_FF), chip * (2 * D_FF // N_CHIPS),
                                                   2 * D_FF // N_CHIPS, axis=1)[None]

    mine = [_add_chips(pp, gg, place_i, "reduce_chips_" + n) for n, pp, gg in zip(names, pairs, got)]
    for n, both in zip(names, _share_sibling(mine)):
        gfull[n] = both.reshape(two_d[n])

    delta, new_m, new_v = {}, {}, {}
    for n, shp in BIG:
        outs = _adamw(w[n].reshape(two_d[n]), gfull[n], mom[n].reshape(two_d[n]), var[n].reshape(two_d[n]), "adamw_" + n)
        delta[n], new_m[n], new_v[n] = (o.reshape((1,) + shp) for o in outs)
    snames = order + ["meta_tokens", "ssm_conv_w", "ffn_conv_w"]
    ssizes = [functools.reduce(lambda a, b: a * b, w[n].shape) for n in snames]
    packs = [_pack_rows([d[n] for n in snames], SMALL_ADAM_ROWS) for d in (w, gfull, mom, var)]
    outs = _adamw(*packs, "adamw_small")
    for d, o in zip((delta, new_m, new_v), outs):
        for n, piece in zip(snames, _unpack_rows(o, ssizes)):
            d[n] = piece.reshape(w[n].shape)
    gout = {n: gfull[n].reshape(w[n].shape) for n in WEIGHT_NAMES}
    return (loss, gx[None], *[gout[n] for n in WEIGHT_NAMES], *[delta[n] for n in WEIGHT_NAMES],
            *[new_m[n] for n in WEIGHT_NAMES], *[new_v[n] for n in WEIGHT_NAMES])
```

```python
import functools

import jax
import jax.numpy as jnp
from jax import lax
from jax.experimental import pallas as pl
from jax.experimental.pallas import tpu as pltpu

F32 = jnp.float32
BF16 = jnp.bfloat16

D = 1024
N_META = 16
FRONT = 128
PAD_ROWS = FRONT - N_META
MLA_H = 8
DN, DR, DV = 128, 64, 128
QR, KVR = 384, 256
SOFTMAX_SCALE = (DN + DR) ** -0.5
ROPE_THETA = 10000.0
SSM_H, SSM_P, SSM_G, SSM_N, SSM_K = 16, 64, 2, 128, 4
CHUNK = 128
D_SSM = SSM_H * SSM_P
D_XBC = D_SSM + 2 * SSM_G * SSM_N
GSZ = D_SSM // SSM_G
D_FF = 2816
FFN_K = 3
EPS = 1e-6
IN_SPLITS = (QR, KVR, DR, D_SSM, D_XBC, SSM_H)
D_IN = sum(IN_SPLITS)
LAT_W = 768
IN_P = LAT_W + D_SSM + D_XBC + 128
NEG = -1e30
LOG2E = 1.4426950408889634
LN2 = 0.6931471805599453
Q_SCALE = SOFTMAX_SCALE * LOG2E

ADAM_LR, ADAM_B1, ADAM_B2, ADAM_EPS, ADAM_WD, ADAM_STEP = 0.001, 0.9, 0.999, 1e-08, 0.01, 10

VMEM_LIMIT = 56 * 1024 * 1024
MM_ROWS = (640, 320, 128)
MESH = pl.DeviceIdType.MESH


def _sds(shape, dtype):
    return jax.ShapeDtypeStruct(shape, dtype)


def _cp(*sem):
    return pltpu.CompilerParams(dimension_semantics=sem, vmem_limit_bytes=VMEM_LIMIT)


def _rt(n, cands):
    for c in cands:
        if n % c == 0:
            return c
    raise ValueError((n, cands))


def _full(shape):
    nd = len(shape)
    return pl.BlockSpec(shape, lambda *_: (0,) * nd)


def _rows(tr, c):
    return pl.BlockSpec((tr, c), lambda i: (i, 0))


def _sigmoid(x):
    return 1.0 / (1.0 + jnp.exp(-x))


def _silu(x):
    return x * _sigmoid(x)


def _dsilu(x):
    s = _sigmoid(x)
    return s * (1.0 + x * (1.0 - s))


def _softplus(x):
    return jnp.maximum(x, 0.0) + jnp.log(1.0 + jnp.exp(-jnp.abs(x)))


def _rms(x, g):
    r = lax.rsqrt(jnp.mean(x * x, axis=-1, keepdims=True) + EPS)
    return x * r * g


def _rms_bwd(x, g, dy):
    r = lax.rsqrt(jnp.mean(x * x, axis=-1, keepdims=True) + EPS)
    xh = x * r
    dxh = dy * g
    dx = r * (dxh - xh * jnp.mean(dxh * xh, axis=-1, keepdims=True))
    return dx, jnp.sum(dy * xh, axis=0, keepdims=True)


def _dot(a, b):
    return jnp.dot(a, b, preferred_element_type=F32)


def _dot_nt(a, b):
    return lax.dot_general(a, b, (((1,), (1,)), ((), ())), preferred_element_type=F32)


def _dot_tn(a, b):
    return lax.dot_general(a, b, (((0,), (0,)), ((), ())), preferred_element_type=F32)


def _split3(x):
    hi = x.astype(BF16)
    r = x - hi.astype(F32)
    mid = r.astype(BF16)
    return hi, mid, (r - mid.astype(F32)).astype(BF16)


def _dot_hi(a, b, split="a"):
    if split == "a":
        bb = b.astype(BF16)
        return sum(_dot(t, bb) for t in _split3(a))
    ab = a.astype(BF16)
    return sum(_dot(ab, t) for t in _split3(b))


def _dot_nt_hi(a, b):
    bb = b.astype(BF16)
    return sum(_dot_nt(t, bb) for t in _split3(a))


def _shift_down(x, halo, j):
    xr = pltpu.roll(x, j, axis=0)
    hr = pltpu.roll(halo, j, axis=0)
    row = lax.broadcasted_iota(jnp.int32, (8, x.shape[1]), 0)
    first = jnp.where(row < j, hr, xr[:8])
    return jnp.concatenate([first, xr[8:]], axis=0)


def _shift_up(x, nxt, j):
    t = x.shape[0]
    xr = pltpu.roll(x, t - j, axis=0)
    nr = pltpu.roll(nxt, 8 - j, axis=0)
    row = lax.broadcasted_iota(jnp.int32, (8, x.shape[1]), 0)
    last = jnp.where(row + j >= 8, nr, xr[t - 8:])
    return jnp.concatenate([xr[:t - 8], last], axis=0)


def _acc_rows(ref, val, first):
    @pl.when(first)
    def _():
        ref[...] = val

    @pl.when(jnp.logical_not(first))
    def _():
        ref[...] += val


def _mm_tn(a, b, name, tn=None, trs=(1664, 640, 128), chunked=False):
    r, m = a.shape
    n = b.shape[1]
    tn = n if tn is None else tn
    tr = _rt(r, trs)

    def body(a_ref, b_ref, o_ref):
        part = _dot_tn(a_ref[...].astype(BF16), b_ref[...].astype(BF16))
        _acc_rows(o_ref, part, pl.program_id(1) == 0)

    if chunked:
        out_specs, out_shape = pl.BlockSpec((None, m, tn), lambda j, i: (j, 0, 0)), _sds((n // tn, m, tn), F32)
    else:
        out_specs, out_shape = pl.BlockSpec((m, tn), lambda j, i: (0, j)), _sds((m, n), F32)
    return pl.pallas_call(
        body, name=name, grid=(n // tn, r // tr),
        in_specs=[pl.BlockSpec((tr, m), lambda j, i: (i, 0)), pl.BlockSpec((tr, tn), lambda j, i: (i, j))],
        out_specs=out_specs, out_shape=out_shape, compiler_params=_cp("parallel", "arbitrary"))(a, b)


def _inproj(h0, g, w):
    lp = h0.shape[0]
    tr = _rt(lp, MM_ROWS)
    segs = ((0, LAT_W), (LAT_W, LAT_W + D_SSM), (LAT_W + D_SSM, LAT_W + D_SSM + D_XBC), (IN_P - 128, IN_P))

    def body(h_ref, g_ref, w_ref, hn_ref, lat_ref, z_ref, xbc_ref, dt_ref):
        hn = _rms(h_ref[...], g_ref[...]).astype(BF16)
        hn_ref[...] = hn
        for ref, (a, b) in zip((lat_ref, z_ref, xbc_ref, dt_ref), segs):
            ref[...] = _dot(hn, w_ref[:, a:b])

    return pl.pallas_call(
        body, name="inproj", grid=(lp // tr,), in_specs=[_rows(tr, D), _full((1, D)), _full(w.shape)],
        out_specs=[_rows(tr, D), _rows(tr, LAT_W), _rows(tr, D_SSM), _rows(tr, D_XBC), _rows(tr, 128)],
        out_shape=[_sds((lp, D), BF16), _sds((lp, LAT_W), F32), _sds((lp, D_SSM), F32), _sds((lp, D_XBC), F32),
                   _sds((lp, 128), F32)],
        compiler_params=_cp("parallel"))(h0, g, w)


def _rope(x, cos, sa, sb):
    return x * cos + pltpu.roll(x, 96, axis=1) * sa + pltpu.roll(x, 32, axis=1) * sb


def _rope_t(g, cos, sa, sb):
    return g * cos + pltpu.roll(g * sa, 32, axis=1) + pltpu.roll(g * sb, 96, axis=1)


def _mla_prep(lat, qg, kvg, wq, wkv, cos, sa, sb):
    lp = lat.shape[0]
    tr = _rt(lp, MM_ROWS)

    def body(lat_ref, qg_ref, kvg_ref, wq_ref, wkv_ref, cos_ref, sa_ref, sb_ref, q_ref, k_ref, v_ref, ql_ref, kl_ref):
        lat_v = lat_ref[...]
        ql = _rms(lat_v[:, :QR], qg_ref[...]).astype(BF16)
        kl = _rms(lat_v[:, QR:QR + KVR], kvg_ref[...]).astype(BF16)
        ql_ref[...] = ql
        kl_ref[...] = kl
        cos_v, sa_v, sb_v = cos_ref[...], sa_ref[...], sb_ref[...]
        kpe = _rope(lat_v[:, QR + KVR:LAT_W], cos_v, sa_v, sb_v).astype(BF16)
        for h in range(MLA_H):
            q_ref[h, :, 0:DN] = (_dot(ql, wq_ref[:, h * DN:(h + 1) * DN]) * Q_SCALE).astype(BF16)
            qpe = _dot(ql, wq_ref[:, D + h * 128:D + (h + 1) * 128])
            q_ref[h, :, DN:2 * DN] = (_rope(qpe, cos_v, sa_v, sb_v) * Q_SCALE).astype(BF16)
            k_ref[h, :, 0:DN] = _dot(kl, wkv_ref[:, h * DN:(h + 1) * DN]).astype(BF16)
            k_ref[h, :, DN:2 * DN] = kpe
            v_ref[h] = _dot(kl, wkv_ref[:, D + h * DV:D + (h + 1) * DV]).astype(BF16)

    hb = lambda w: pl.BlockSpec((MLA_H, tr, w), lambda i: (0, i, 0))
    return pl.pallas_call(
        body, name="mla_prep", grid=(lp // tr,),
        in_specs=[_rows(tr, LAT_W), _full((1, QR)), _full((1, KVR)), _full(wq.shape), _full(wkv.shape),
                  _rows(tr, 128), _rows(tr, 128), _rows(tr, 128)],
        out_specs=[hb(256), hb(256), hb(128), _rows(tr, QR), _rows(tr, KVR)],
        out_shape=[_sds((MLA_H, lp, 256), BF16), _sds((MLA_H, lp, 256), BF16), _sds((MLA_H, lp, 128), BF16),
                   _sds((lp, QR), BF16), _sds((lp, KVR), BF16)],
        compiler_params=_cp("parallel"))(lat, qg, kvg, wq, wkv, cos, sa, sb)


def _attn_mask(r0, c0, tq, tk, transposed=False):
    if transposed:
        kk = c0 + lax.broadcasted_iota(jnp.int32, (tk, tq), 0)
        qq = r0 + lax.broadcasted_iota(jnp.int32, (tk, tq), 1)
    else:
        qq = r0 + lax.broadcasted_iota(jnp.int32, (tq, tk), 0)
        kk = c0 + lax.broadcasted_iota(jnp.int32, (tq, tk), 1)
    return jnp.logical_and(kk <= qq, kk >= PAD_ROWS)


def _attn_fwd(q, k, v, ride=()):
    lp = q.shape[1]
    t = _rt(lp, (640, 128))
    nq = lp // t

    hp = 2
    KW = (4, 2, 1)

    nr = len(ride)
    steps = (MLA_H // hp) * nq

    def body(q_ref, k_ref, v_ref, *rest):
        o_ref, lse_ref = rest[nr:nr + 2]
        bufs, sems = rest[nr + 2:2 * nr + 2], rest[2 * nr + 2:]
        qi = pl.program_id(1)
        step = pl.program_id(0) * nq + qi
        if nr:
            pl.when(step == 0)(lambda: _ride_gather(bufs, *sems, 0))
            pl.when(step == steps // 2)(lambda: _ride_gather(bufs, *sems, 1))
        qv = [q_ref[a] for a in range(hp)]

        def tile(kj, carries, bias=None, width=1):
            kv_rows = pl.ds(pl.multiple_of(kj * t, t), width * t)
            out = []
            for a in range(hp):
                m, l, acc = carries[a]
                kk = k_ref[a, kv_rows, :]
                vv = v_ref[a, kv_rows, :]
                s = _dot_nt(qv[a], kk)
                if bias is not None:
                    s = s + bias
                m_new = jnp.maximum(m, jnp.max(s, axis=-1, keepdims=True))
                alpha = jnp.exp2(m - m_new)
                p = jnp.exp2(s - m_new)
                l = alpha * l + jnp.sum(p, axis=-1, keepdims=True)
                acc = alpha * acc + _dot(p.astype(BF16), vv)
                out.append((m_new, l, acc))
            return tuple(out)

        key = lax.broadcasted_iota(jnp.int32, (1, t), 1)
        pad_bias = jnp.where(jnp.logical_and(key >= PAD_ROWS, qi > 0), 0.0, NEG)
        init = tuple((jnp.full((t, 1), NEG, F32), jnp.zeros((t, 1), F32), jnp.zeros((t, DV), F32)) for _ in range(hp))
        carries = tile(0, init, pad_bias)
        nxt = 1
        for width in KW:
            reps = jnp.maximum(qi - nxt, 0) // width
            carries = lax.fori_loop(0, reps, functools.partial(lambda j, c, nxt, width: tile(nxt + width * j, c, width=width),
                                                               nxt=nxt, width=width), carries)
            nxt = nxt + width * reps
        carries = tile(qi, carries, jnp.where(_attn_mask(qi * t, qi * t, t, t), 0.0, NEG))
        for a in range(hp):
            m, l, acc = carries[a]
            o_ref[:, a * DV:(a + 1) * DV] = acc / l
            lse_ref[a] = jnp.broadcast_to(m + jnp.log(l) * LOG2E, (t, 128)).T[:8]
        if nr:
            pl.when(step == steps - 1)(lambda: _ride_gather(bufs, *sems, 2))

    sems = [pltpu.SemaphoreType.DMA((6 * nr,)), pltpu.SemaphoreType.DMA((6 * nr,))] if nr else []
    outs = pl.pallas_call(
        body, name="attn_fwd", grid=(MLA_H // hp, nq),
        in_specs=[pl.BlockSpec((hp, t, 256), lambda h, i: (h, i, 0)), pl.BlockSpec((hp, lp, 256), lambda h, i: (h, 0, 0)),
                  pl.BlockSpec((hp, lp, 128), lambda h, i: (h, 0, 0))] + [ANY] * nr,
        out_specs=[pl.BlockSpec((t, hp * DV), lambda h, i: (i, h)), pl.BlockSpec((hp, 8, t), lambda h, i: (h, 0, i))] + [ANY] * nr,
        out_shape=[_sds((lp, MLA_H * DV), F32), _sds((MLA_H, 8, lp), F32)] + [_sds(b.shape, b.dtype) for b in ride],
        input_output_aliases={3 + w: 2 + w for w in range(nr)}, scratch_shapes=sems,
        compiler_params=_cp("arbitrary", "arbitrary"))(q, k, v, *ride)
    return outs[0], outs[1], list(outs[2:])


def _mixout_bwd(mix, g_post, dh1, o, g_ao, w_out, ride=()):
    lp = o.shape[0]
    tr = _rt(lp, MM_ROWS)
    nr = len(ride)
    steps = lp // tr

    def body(mix_ref, gp_ref, dh_ref, o_ref, g_ref, w_ref, *rest):
        gs = rest[:nr]
        dmix_ref, dssm_ref, do_ref, dgp_ref, dg_ref, dl_ref = rest[nr:nr + 6]
        from_sib, sems = rest[nr + 6:2 * nr + 6], rest[2 * nr + 6:]
        i = pl.program_id(0)
        if nr:
            pl.when(i == 0)(lambda: _ride_sibling(gs, from_sib, *sems, 0))
        grow = i * tr + lax.broadcasted_iota(jnp.int32, (tr, D), 0)
        dmix, dgp = _rms_bwd(mix_ref[...], gp_ref[...], jnp.where(grow >= PAD_ROWS, dh_ref[...], 0.0))
        dmix = dmix.astype(BF16)
        dmix_ref[...] = dmix
        _acc_rows(dgp_ref, dgp, i == 0)
        dssm_ref[...] = _dot_nt(dmix, w_ref[D:, :])
        ov = o_ref[...]
        do, dg = _rms_bwd(ov, g_ref[...], _dot_nt(dmix, w_ref[:D, :]))
        do_ref[...] = do
        _acc_rows(dg_ref, dg, i == 0)
        prod = do * ov
        lane = lax.broadcasted_iota(jnp.int32, (1, 128), 1)
        cols = jnp.zeros((tr, 128), F32)
        for h in range(MLA_H):
            cols = cols + jnp.sum(prod[:, h * DV:(h + 1) * DV], axis=-1, keepdims=True) * (lane == h).astype(F32)
        dl_ref[...] = cols.T[:MLA_H]
        if nr:
            pl.when(i == steps - 1)(lambda: _ride_sibling(gs, from_sib, *sems, 1))

    sems = [pltpu.SemaphoreType.DMA((nr,)), pltpu.SemaphoreType.DMA((nr,))] if nr else []
    outs = pl.pallas_call(
        body, name="mixout_bwd", grid=(steps,),
        in_specs=[_rows(tr, D), _full((1, D)), _rows(tr, D), _rows(tr, D), _full((1, D)), _full(w_out.shape)] + [ANY] * nr,
        out_specs=[_rows(tr, D), _rows(tr, D), _rows(tr, D), _full((1, D)), _full((1, D)),
                   pl.BlockSpec((MLA_H, tr), lambda i: (0, i))] + [ANY] * nr,
        out_shape=[_sds((lp, D), BF16), _sds((lp, D), F32), _sds((lp, D), F32), _sds((1, D), F32), _sds((1, D), F32),
                   _sds((MLA_H, lp), F32)] + [_sds((g.shape[0],) + g.shape[2:], g.dtype) for g in ride],
        scratch_shapes=sems, compiler_params=_cp("arbitrary"))(mix, g_post, dh1, o, g_ao, w_out, *ride)
    return tuple(outs[:6]) + (list(outs[6:]),)


def _attn_bwd(q, k, v, do, lse_row, delta_row, ride=()):
    lp = q.shape[1]
    t = _rt(lp, (640, 128))
    nq = lp // t

    nr = len(ride)

    def body(q_ref, k_ref, v_ref, do_ref, lse_ref, dl_ref, *rest):
        ps = rest[:nr]
        dq_ref, dk_ref, dv_ref = rest[nr:nr + 3]
        got, sems = rest[nr + 3:2 * nr + 3], rest[2 * nr + 3:]
        kj = pl.program_id(1)
        step = pl.program_id(0) * nq + kj
        if nr:
            pl.when(step == 0)(lambda: _ride_exchange(ps, got, *sems, 0))
        kk = k_ref[0]
        vv = v_ref[0]

        @pl.when(kj == 0)
        def _():
            dq_ref[...] = jnp.zeros_like(dq_ref)

        def tile(qi, carry, masked, width=1):
            dk, dv = carry
            q_rows = pl.ds(pl.multiple_of(qi * t, t), width * t)
            qv = q_ref[0, q_rows, :]
            dob = do_ref[q_rows, :].astype(BF16)
            lse_v = jnp.concatenate([lse_ref[0, qi + b] for b in range(width)], axis=1)
            dl_v = jnp.concatenate([dl_ref[0, qi + b] for b in range(width)], axis=1)
            st = _dot_nt(kk, qv)
            if masked:
                st = jnp.where(_attn_mask(qi * t, kj * t, width * t, t, transposed=True), st, NEG)
            pt = jnp.exp2(st - lse_v)
            dpt = _dot_nt(vv, dob)
            dst = (pt * (dpt - dl_v)).astype(BF16)
            dv = dv + _dot(pt.astype(BF16), dob)
            dk = dk + _dot(dst, qv)
            dq_ref[0, q_rows, :] += _dot_tn(dst, kk)
            return dk, dv

        carry = tile(kj, (jnp.zeros((t, 256), F32), jnp.zeros((t, DV), F32)), True)
        split = jnp.where(kj == 0, nq, kj + 1)

        def span(lo, hi, masked, carry):
            pairs = (hi - lo) // 2
            carry = lax.fori_loop(0, pairs, lambda j, c: tile(lo + 2 * j, c, masked, width=2), carry)
            return lax.fori_loop(lo + 2 * pairs, hi, lambda qi, c: tile(qi, c, masked), carry)

        dk, dv = span(split, nq, False, span(kj + 1, split, True, carry))
        dk_ref[0] = dk * LN2
        dv_ref[0] = dv
        if nr:
            pl.when(step == MLA_H * nq - 1)(lambda: _ride_exchange(ps, got, *sems, 1))

    stat = pl.BlockSpec((1, nq, 1, t), lambda h, j: (h, 0, 0, 0))
    sems = [pltpu.SemaphoreType.DMA((3 * nr,)), pltpu.SemaphoreType.DMA((3 * nr,))] if nr else []
    outs = pl.pallas_call(
        body, name="attn_bwd", grid=(MLA_H, nq),
        in_specs=[pl.BlockSpec((1, lp, 256), lambda h, j: (h, 0, 0)), pl.BlockSpec((1, t, 256), lambda h, j: (h, j, 0)),
                  pl.BlockSpec((1, t, 128), lambda h, j: (h, j, 0)), pl.BlockSpec((lp, DV), lambda h, j: (0, h)), stat, stat]
        + [ANY] * nr,
        out_specs=[pl.BlockSpec((1, lp, 256), lambda h, j: (h, 0, 0)), pl.BlockSpec((1, t, 256), lambda h, j: (h, j, 0)),
                   pl.BlockSpec((1, t, 128), lambda h, j: (h, j, 0))] + [ANY] * nr,
        out_shape=[_sds((MLA_H, lp, 256), F32), _sds((MLA_H, lp, 256), F32), _sds((MLA_H, lp, 128), F32)]
        + [_sds((3,) + p.shape[1:], p.dtype) for p in ride],
        scratch_shapes=sems, compiler_params=_cp("arbitrary", "arbitrary"))(q, k, v, do, lse_row, delta_row, *ride)
    return outs[0], outs[1], outs[2], list(outs[3:])


def _ssd_consts():
    ri = lax.broadcasted_iota(jnp.int32, (CHUNK, CHUNK), 0)
    ci = lax.broadcasted_iota(jnp.int32, (CHUNK, CHUNK), 1)
    expand = (lax.broadcasted_iota(jnp.int32, (128, D_SSM), 0)
              == lax.broadcasted_iota(jnp.int32, (128, D_SSM), 1) // SSM_P).astype(F32)
    return ri, ci, expand


def _ssd_chunk(c, x_ref, xh_ref, dt_ref, dtT_ref, cw_ref, cb_ref, dtb_ref, dtbT_ref, al_ref, alT_ref):
    ri, ci, expand = _ssd_consts()
    x = x_ref[...]
    halo = jnp.where(c > 0, xh_ref[...], 0.0)
    sh = [x] + [_shift_down(x, halo, j) for j in range(1, SSM_K)]
    cv = cb_ref[...]
    for kk in range(SSM_K):
        cv = cv + cw_ref[kk:kk + 1, :] * sh[SSM_K - 1 - kk]
    xa = _silu(cv)
    grow = c * CHUNK + ri
    gcol = c * CHUNK + lax.broadcasted_iota(jnp.int32, (SSM_H, CHUNK), 1)
    sp = dt_ref[...] + dtb_ref[...]
    spT = dtT_ref[...] + dtbT_ref[...]
    dtc = jnp.where(grow >= PAD_ROWS, _softplus(sp), 0.0)
    dtr = jnp.where(gcol >= PAD_ROWS, _softplus(spT), 0.0)
    arow = -jnp.exp(al_ref[...])
    acolT = -jnp.exp(alT_ref[...])
    ltri = (ci <= ri).astype(F32)
    acs = _dot_hi(ltri, dtc * arow, split="b")
    acsT = _dot_hi(dtr * acolT, (ri <= ci).astype(F32))
    return dict(x=x, sh=sh, cv=cv, xa=xa, sp=sp, dtc=dtc, arow=arow, acs=acs, acsT=acsT, ri=ri, ci=ci, expand=expand,
                grow=grow)


def _ssd_mats(k, s_prev):
    xa, acs, acsT, expand, ri, ci = k["xa"], k["acs"], k["acsT"], k["expand"], k["ri"], k["ci"]
    xs = xa[:, :D_SSM]
    dt_e = _dot_hi(k["dtc"], expand)
    acs_e = _dot_hi(acs, expand)
    last_e = acs_e[CHUNK - 1:CHUNK, :]
    ea = jnp.exp(acs_e)
    f = jnp.exp(last_e - acs_e)
    cd = jnp.exp(last_e)
    xdt = xs * dt_e
    bm = [xa[:, D_SSM + g * SSM_N:D_SSM + (g + 1) * SSM_N] for g in range(SSM_G)]
    cm = [xa[:, D_SSM + (SSM_G + g) * SSM_N:D_SSM + (SSM_G + g + 1) * SSM_N] for g in range(SSM_G)]
    bmb = [b.astype(BF16) for b in bm]
    cmb = [cc.astype(BF16) for cc in cm]
    cb = [_dot_nt(cmb[g], bmb[g]) for g in range(SSM_G)]
    lam, mm = [], []
    causal = jnp.where(ci <= ri, 0.0, NEG)
    for h in range(SSM_H):
        lam_h = jnp.exp((acs[:, h:h + 1] - acsT[h:h + 1, :]) + causal)
        lam.append(lam_h)
        mm.append(cb[h // (SSM_H // SSM_G)] * lam_h)
    lo = lax.broadcasted_iota(jnp.int32, (CHUNK, 128), 1) < SSM_P
    xdt_h = []
    for h in range(SSM_H):
        pair = xdt[:, (h // 2) * 128:(h // 2 + 1) * 128]
        xdt_h.append(jnp.where(lo if h % 2 == 0 else jnp.logical_not(lo), pair, 0.0).astype(BF16))
    ydiag = jnp.concatenate(
        [_dot(mm[2 * j].astype(BF16), xdt_h[2 * j]) + _dot(mm[2 * j + 1].astype(BF16), xdt_h[2 * j + 1])
         for j in range(SSM_H // 2)], axis=1)
    t_off = [_dot(cmb[g], s_prev[g].astype(BF16)) for g in range(SSM_G)]
    yoff = jnp.concatenate(t_off, axis=1) * ea
    return dict(xs=xs, dt_e=dt_e, acs_e=acs_e, ea=ea, f=f, cd=cd, xdt=xdt, bm=bm, cm=cm, bmb=bmb, cmb=cmb, cb=cb, lam=lam,
                mm=mm, lo=lo, xdt_h=xdt_h, ydiag=ydiag, t_off=t_off, yoff=yoff)


def _ssd_specs(nc, rev):
    ix = (lambda i: nc - 1 - i) if rev else (lambda i: i)
    return [
        pl.BlockSpec((CHUNK, D_XBC), lambda i: (ix(i), 0)),
        pl.BlockSpec((8, D_XBC), lambda i: (jnp.maximum(ix(i) * (CHUNK // 8) - 1, 0), 0)),
        pl.BlockSpec((CHUNK, D_SSM), lambda i: (ix(i), 0)),
        pl.BlockSpec((CHUNK, 128), lambda i: (ix(i), 0)),
        pl.BlockSpec((SSM_H, CHUNK), lambda i: (0, ix(i))),
        _full((8, D_XBC)), _full((1, D_XBC)), _full((1, 128)), _full((SSM_H, 1)), _full((1, 128)), _full((SSM_H, 1)),
        _full((1, D_SSM)), _full((1, D_SSM)),
    ]


def _ssd_fwd(xbc, z, dtr, dtrT, cw, cb, dtb, dtbT, alog, alogT, d_e, ng):
    lp = xbc.shape[0]
    nc = lp // CHUNK

    def body(x_ref, xh_ref, z_ref, dt_ref, dtT_ref, cw_ref, cb_ref, dtb_ref, dtbT_ref, al_ref, alT_ref, de_ref, ng_ref,
             y_ref, st_ref, s_scr):
        c = pl.program_id(0)

        @pl.when(c == 0)
        def _():
            s_scr[...] = jnp.zeros_like(s_scr)

        k = _ssd_chunk(c, x_ref, xh_ref, dt_ref, dtT_ref, cw_ref, cb_ref, dtb_ref, dtbT_ref, al_ref, alT_ref)
        s_prev = [s_scr[g] for g in range(SSM_G)]
        st_ref[0] = s_scr[...]
        m = _ssd_mats(k, s_prev)
        xd = (m["xdt"] * m["f"]).astype(BF16)
        for g in range(SSM_G):
            sl = slice(g * GSZ, (g + 1) * GSZ)
            s_scr[g] = m["cd"][:, sl] * s_prev[g] + _dot(m["bm"][g].T.astype(BF16), xd[:, sl])
        y = m["ydiag"] + m["yoff"] + de_ref[...] * m["xs"]
        u = y * _silu(z_ref[...])
        outs = []
        for g in range(SSM_G):
            ug = u[:, g * GSZ:(g + 1) * GSZ]
            outs.append(ug * lax.rsqrt(jnp.mean(ug * ug, axis=-1, keepdims=True) + EPS))
        y_ref[...] = jnp.concatenate(outs, axis=1) * ng_ref[...]

    return pl.pallas_call(
        body, name="ssd_fwd", grid=(nc,), in_specs=_ssd_specs(nc, False),
        out_specs=[_rows(CHUNK, D_SSM), pl.BlockSpec((1, SSM_G, SSM_N, GSZ), lambda i: (i, 0, 0, 0))],
        out_shape=[_sds((lp, D_SSM), F32), _sds((nc, SSM_G, SSM_N, GSZ), F32)],
        scratch_shapes=[pltpu.VMEM((SSM_G, SSM_N, GSZ), F32)],
        compiler_params=_cp("arbitrary"))(xbc, xbc, z, dtr, dtrT, cw, cb, dtb, dtbT, alog, alogT, d_e, ng)


def _ssd_bwd(dssm, xbc, z, dtr, dtrT, st, cw, cb, dtb, dtbT, alog, alogT, d_e, ng):
    lp = xbc.shape[0]
    nc = lp // CHUNK
    hpg = SSM_H // SSM_G

    def body(dy_ref, x_ref, xh_ref, z_ref, dt_ref, dtT_ref, st_ref, cw_ref, cb_ref, dtb_ref, dtbT_ref, al_ref, alT_ref,
             de_ref, ng_ref, dz_ref, dx_ref, ddt_ref, dcw_ref, dcb_ref, ddtb_ref, dal_ref, dd_ref, dng_ref, ds_scr, nx_scr):
        i = pl.program_id(0)
        c = nc - 1 - i
        first = i == 0

        @pl.when(first)
        def _():
            ds_scr[...] = jnp.zeros_like(ds_scr)
            nx_scr[...] = jnp.zeros_like(nx_scr)

        k = _ssd_chunk(c, x_ref, xh_ref, dt_ref, dtT_ref, cw_ref, cb_ref, dtb_ref, dtbT_ref, al_ref, alT_ref)
        s_prev = [st_ref[0, g] for g in range(SSM_G)]
        m = _ssd_mats(k, s_prev)
        ri, ci, expand = k["ri"], k["ci"], k["expand"]
        xs, acs, acsT = m["xs"], k["acs"], k["acsT"]
        zv = z_ref[...]
        dout = dy_ref[...]
        ngv = ng_ref[...]
        y = m["ydiag"] + m["yoff"] + de_ref[...] * xs
        sz = _silu(zv)
        u = y * sz
        du_parts, dng_parts = [], []
        for g in range(SSM_G):
            sl = slice(g * GSZ, (g + 1) * GSZ)
            dug, dngg = _rms_bwd(u[:, sl], ngv[:, sl], dout[:, sl])
            du_parts.append(dug)
            dng_parts.append(dngg)
        du = jnp.concatenate(du_parts, axis=1)
        _acc_rows(dng_ref, jnp.concatenate(dng_parts, axis=1), first)
        dy = du * sz
        dz_ref[...] = du * y * _dsilu(zv)
        dd_e = jnp.sum(dy * xs, axis=0, keepdims=True)
        _acc_rows(dd_ref, _dot_nt_hi(dd_e, expand), first)
        dxs = de_ref[...] * dy
        dacs_e = dy * m["yoff"]
        dtg = (dy * m["ea"]).astype(BF16)
        dxdt = jnp.zeros_like(xs)
        dlast_e = []
        db, dc, ds_prev = [], [], []
        xd = m["xdt"] * m["f"]
        dxd_all = []
        for g in range(SSM_G):
            sl = slice(g * GSZ, (g + 1) * GSZ)
            dsg = ds_scr[g]
            spb = s_prev[g].astype(BF16)
            dc.append(_dot_nt(dtg[:, sl], spb))
            dsp = _dot(m["cm"][g].T.astype(BF16), dtg[:, sl]) + m["cd"][:, sl] * dsg
            ds_prev.append(dsp)
            dlast_e.append(jnp.sum(dsg * s_prev[g], axis=0, keepdims=True) * m["cd"][:, sl])
            dsb = dsg.astype(BF16)
            db.append(_dot_nt(xd[:, sl].astype(BF16), dsb))
            dxd_all.append(_dot(m["bmb"][g], dsb))
        dxd = jnp.concatenate(dxd_all, axis=1)
        dxdt = dxd * m["f"]
        dff = dxd * xd
        dacs_e = dacs_e - dff
        dlast_row = jnp.concatenate(dlast_e, axis=1) + jnp.sum(dff, axis=0, keepdims=True)
        dacs = jnp.zeros((CHUNK, 128), F32)
        dacs_t = jnp.zeros((CHUNK, CHUNK), F32)
        lane = lax.broadcasted_iota(jnp.int32, (1, 128), 1)
        dgs = [jnp.zeros((CHUNK, CHUNK), F32) for _ in range(SSM_G)]
        dxdt_pairs = []
        for h in range(SSM_H):
            g = h // hpg
            pr = slice((h // 2) * 128, (h // 2 + 1) * 128)
            lo_h = m["lo"] if h % 2 == 0 else jnp.logical_not(m["lo"])
            dyp = jnp.where(lo_h, dy[:, pr], 0.0).astype(BF16)
            dm = _dot_nt(dyp, m["xdt"][:, pr].astype(BF16))
            dgs[g] = dgs[g] + dm * m["lam"][h]
            w_h = dm * m["mm"][h]
            dacs = dacs + jnp.sum(w_h, axis=1, keepdims=True) * (lane == h).astype(F32)
            dacs_t = dacs_t + jnp.where(ri == h, jnp.sum(w_h, axis=0, keepdims=True), 0.0)
            part = _dot_tn(m["mm"][h].astype(BF16), dyp)
            if h % 2 == 0:
                dxdt_pairs.append(part)
            else:
                dxdt_pairs[-1] = dxdt_pairs[-1] + part
        dxdt = dxdt + jnp.concatenate(dxdt_pairs, axis=1)
        for g in range(SSM_G):
            dgb = dgs[g].astype(BF16)
            dc[g] = dc[g] + _dot(dgb, m["bmb"][g])
            db[g] = db[g] + _dot_tn(dgb, m["cmb"][g])
        dacs = dacs - dacs_t.T + _dot_nt_hi(dacs_e, expand)
        dlast = _dot_nt_hi(dlast_row, expand)
        dacs = dacs + jnp.where(ri == CHUNK - 1, dlast, 0.0)
        dxs = dxs + dxdt * m["dt_e"]
        ddt = _dot_nt_hi(dxdt * xs, expand)
        da = _dot_hi((ri <= ci).astype(F32), dacs, split="b")
        ddt = ddt + da * k["arow"]
        dA = jnp.sum(da * k["dtc"], axis=0, keepdims=True)
        _acc_rows(dal_ref, dA * k["arow"], first)
        ddtr = jnp.where(k["grow"] >= PAD_ROWS, ddt * _sigmoid(k["sp"]), 0.0)
        ddt_ref[...] = ddtr
        _acc_rows(ddtb_ref, jnp.sum(ddtr, axis=0, keepdims=True), first)
        for g in range(SSM_G):
            ds_scr[g] = ds_prev[g]
        dxa = jnp.concatenate([dxs] + db + dc, axis=1)
        dcv = dxa * _dsilu(k["cv"])
        _acc_rows(dcb_ref, jnp.sum(dcv, axis=0, keepdims=True), first)
        dcw_rows = [jnp.sum(dcv * k["sh"][SSM_K - 1 - kk], axis=0, keepdims=True) for kk in range(SSM_K)]
        dcw_rows.append(jnp.zeros((8 - SSM_K, D_XBC), F32))
        _acc_rows(dcw_ref, jnp.concatenate(dcw_rows, axis=0), first)
        nxt = nx_scr[...]
        dx = cw_ref[SSM_K - 1:SSM_K, :] * dcv
        for j in range(1, SSM_K):
            dx = dx + cw_ref[SSM_K - 1 - j:SSM_K - j, :] * _shift_up(dcv, nxt, j)
        grow_x = c * CHUNK + lax.broadcasted_iota(jnp.int32, (CHUNK, D_XBC), 0)
        dx_ref[...] = jnp.where(grow_x >= PAD_ROWS, dx, 0.0)
        nx_scr[...] = dcv[:8]

    specs = _ssd_specs(nc, True)
    in_specs = [pl.BlockSpec((CHUNK, D_SSM), lambda i: (nc - 1 - i, 0))] + specs[:5] + [
        pl.BlockSpec((1, SSM_G, SSM_N, GSZ), lambda i: (nc - 1 - i, 0, 0, 0))] + specs[5:]
    rv = lambda w: pl.BlockSpec((CHUNK, w), lambda i: (nc - 1 - i, 0))
    return pl.pallas_call(
        body, name="ssd_bwd", grid=(nc,), in_specs=in_specs,
        out_specs=[rv(D_SSM), rv(D_XBC), rv(128), _full((8, D_XBC)), _full((1, D_XBC)), _full((1, 128)), _full((1, 128)),
                   _full((1, 128)), _full((1, D_SSM))],
        out_shape=[_sds((lp, D_SSM), F32), _sds((lp, D_XBC), F32), _sds((lp, 128), F32), _sds((8, D_XBC), F32),
                   _sds((1, D_XBC), F32), _sds((1, 128), F32), _sds((1, 128), F32), _sds((1, 128), F32), _sds((1, D_SSM), F32)],
        scratch_shapes=[pltpu.VMEM((SSM_G, SSM_N, GSZ), F32), pltpu.VMEM((8, D_XBC), F32)],
        compiler_params=_cp("arbitrary"))(dssm, xbc, xbc, z, dtr, dtrT, st, cw, cb, dtb, dtbT, alog, alogT, d_e, ng)


def _mixout_fwd(o, ssm, h0, g_ao, g_post, w):
    lp = o.shape[0]
    tr = _rt(lp, MM_ROWS)

    def body(o_ref, s_ref, h_ref, ga_ref, gp_ref, w_ref, mi_ref, mix_ref, h1_ref):
        mixin = jnp.concatenate([_rms(o_ref[...], ga_ref[...]), s_ref[...]], axis=1).astype(BF16)
        mi_ref[...] = mixin
        mix = _dot(mixin, w_ref[...])
        mix_ref[...] = mix
        grow = pl.program_id(0) * tr + lax.broadcasted_iota(jnp.int32, (tr, D), 0)
        h1_ref[...] = h_ref[...] + jnp.where(grow >= PAD_ROWS, _rms(mix, gp_ref[...]), 0.0)

    return pl.pallas_call(
        body, name="mixout_fwd", grid=(lp // tr,),
        in_specs=[_rows(tr, D), _rows(tr, D), _rows(tr, D), _full((1, D)), _full((1, D)), _full(w.shape)],
        out_specs=[_rows(tr, 2 * D), _rows(tr, D), _rows(tr, D)],
        out_shape=[_sds((lp, 2 * D), BF16), _sds((lp, D), F32), _sds((lp, D), F32)],
        compiler_params=_cp("parallel"))(o, ssm, h0, g_ao, g_post, w)


def _ffn_up(h1, g, w):
    lp = h1.shape[0]
    tr = _rt(lp, MM_ROWS)
    tn = D_FF // 2

    def body(h_ref, g_ref, w_ref, hn_ref, u_ref):
        @pl.when(pl.program_id(1) == 0)
        def _():
            hn_ref[...] = _rms(h_ref[...], g_ref[...]).astype(BF16)

        u_ref[...] = _dot(hn_ref[...], w_ref[...]).astype(BF16)

    return pl.pallas_call(
        body, name="ffn_up", grid=(lp // tr, 2 * D_FF // tn),
        in_specs=[pl.BlockSpec((tr, D), lambda i, j: (i, 0)), _full((1, D)), pl.BlockSpec((None, D, tn), lambda i, j: (j, 0, 0))],
        out_specs=[pl.BlockSpec((tr, D), lambda i, j: (i, 0)), pl.BlockSpec((tr, tn), lambda i, j: (i, j))],
        out_shape=[_sds((lp, D), BF16), _sds((lp, 2 * D_FF), BF16)],
        compiler_params=_cp("parallel", "arbitrary"))(h1, g, w)


def _ffn_in_bwd(du, w4, h1, g, dh2):
    lp = du.shape[0]
    nch, _, tn = w4.shape
    tr = _rt(lp, (320, 128))

    def body(du_ref, w_ref, h_ref, g_ref, r_ref, o_ref, dg_ref):
        acc = _dot_nt(du_ref[:, 0:tn], w_ref[0])
        for j in range(1, nch):
            acc = acc + _dot_nt(du_ref[:, j * tn:(j + 1) * tn], w_ref[j])
        dx, dg = _rms_bwd(h_ref[...], g_ref[...], acc)
        o_ref[...] = dx + r_ref[...]
        _acc_rows(dg_ref, dg, pl.program_id(0) == 0)

    return pl.pallas_call(
        body, name="ffn_in_bwd", grid=(lp // tr,),
        in_specs=[_rows(tr, nch * tn), _full(w4.shape), _rows(tr, D), _full((1, D)), _rows(tr, D)],
        out_specs=[_rows(tr, D), _full((1, D))], out_shape=[_sds((lp, D), F32), _sds((1, D), F32)],
        compiler_params=_cp("arbitrary"))(du, w4, h1, g, dh2)


FFN_CB = 256


def _ffn_gate(u, cw, cb):
    lp = u.shape[0]
    tr = _rt(lp, (320, 128))

    def body(u_ref, uh_ref, cw_ref, cb_ref, uc_ref, a_ref):
        i = pl.program_id(0)
        for j in range(D_FF // FFN_CB):
            halves = []
            for off in (0, D_FF):
                sl = slice(off + j * FFN_CB, off + (j + 1) * FFN_CB)
                x = u_ref[:, sl].astype(F32)
                halo = jnp.where(i > 0, uh_ref[8:16, sl].astype(F32), 0.0)
                cv = cb_ref[:, sl] + cw_ref[FFN_K - 1:FFN_K, sl] * x
                for s in range(1, FFN_K):
                    cv = cv + cw_ref[FFN_K - 1 - s:FFN_K - s, sl] * _shift_down(x, halo, s)
                uc_ref[:, sl] = cv.astype(BF16)
                halves.append(cv)
            a_ref[:, j * FFN_CB:(j + 1) * FFN_CB] = (_silu(halves[0]) * halves[1]).astype(BF16)

    return pl.pallas_call(
        body, name="ffn_gate", grid=(lp // tr,),
        in_specs=[_rows(tr, 2 * D_FF), pl.BlockSpec((16, 2 * D_FF), lambda i: (jnp.maximum(i * (tr // 16) - 1, 0), 0)),
                  _full((8, 2 * D_FF)), _full((1, 2 * D_FF))],
        out_specs=[_rows(tr, 2 * D_FF), _rows(tr, D_FF)], out_shape=[_sds((lp, 2 * D_FF), BF16), _sds((lp, D_FF), BF16)],
        compiler_params=_cp("parallel"))(u, u, cw, cb)


def _ffn_down(a, w, h1, tgt, g_post):
    lp = a.shape[0]
    tr = _rt(lp, MM_ROWS)
    nb = tr // FRONT

    def body(a_ref, w_ref, h_ref, *rest):
        t_refs, (g_ref, dh2_ref, dd_ref, dg_ref, loss_ref) = rest[:nb], rest[nb:]
        i = pl.program_id(0)
        d = _dot(a_ref[...], w_ref[...])
        gv = g_ref[...]
        h2 = h_ref[...] + _rms(d, gv)
        grow = i * tr + lax.broadcasted_iota(jnp.int32, (tr, D), 0)
        tgt_v = jnp.concatenate([r[...] for r in t_refs], axis=0)
        err = jnp.where(grow >= FRONT, h2 - tgt_v, 0.0)
        dh2 = err * (1.0 / D)
        dh2_ref[...] = dh2
        dd, dg = _rms_bwd(d, gv, dh2)
        dd_ref[...] = dd.astype(BF16)
        _acc_rows(dg_ref, dg, i == 0)
        part = 0.5 * jnp.sum(jnp.sum(err * err, axis=1, keepdims=True), axis=0, keepdims=True) * (1.0 / D)
        _acc_rows(loss_ref, jnp.broadcast_to(part, (8, 128)), i == 0)

    return pl.pallas_call(
        body, name="ffn_down", grid=(lp // tr,),
        in_specs=[_rows(tr, D_FF), _full(w.shape), _rows(tr, D)]
        + [pl.BlockSpec((FRONT, D), functools.partial(lambda i, b: (jnp.maximum(i * nb - 1 + b, 0), 0), b=b)) for b in range(nb)]
        + [_full((1, D))],
        out_specs=[_rows(tr, D), _rows(tr, D), _full((1, D)), _full((8, 128))],
        out_shape=[_sds((lp, D), F32), _sds((lp, D), BF16), _sds((1, D), F32), _sds((8, 128), F32)],
        compiler_params=_cp("arbitrary"))(a, w, h1, *([tgt] * nb), g_post)


def _ffn_gate_bwd(u, uc, dd, w_down, cw):
    lp = u.shape[0]
    tr = _rt(lp, (320, 128))
    n = lp // tr

    def body(u_ref, uc_ref, dd_ref, wd_ref, cw_ref, du_ref, dcw_ref, dcb_ref, nx_scr):
        i = pl.program_id(0)
        t = n - 1 - i
        first = i == 0

        @pl.when(first)
        def _():
            nx_scr[...] = jnp.zeros_like(nx_scr)

        grow = t * tr + lax.broadcasted_iota(jnp.int32, (tr, FFN_CB), 0)
        ddv = dd_ref[...]
        for j in range(D_FF // FFN_CB):
            sls = [slice(off + j * FFN_CB, off + (j + 1) * FFN_CB) for off in (0, D_FF)]
            cvg, cvv = uc_ref[:, sls[0]].astype(F32), uc_ref[:, sls[1]].astype(F32)
            dav = _dot_nt(ddv, wd_ref[j * FFN_CB:(j + 1) * FFN_CB, :])
            dcv = (dav * cvv * _dsilu(cvg), dav * _silu(cvg))
            for hf in range(2):
                sl = sls[hf]
                g = dcv[hf]
                nxt = nx_scr[:, sl]
                ahead = [g] + [_shift_up(g, nxt, s) for s in range(1, FFN_K)]
                x = u_ref[:, sl].astype(F32)
                rows = [jnp.sum(x * ahead[FFN_K - 1 - kk], axis=0, keepdims=True) for kk in range(FFN_K)]
                rows.append(jnp.zeros((8 - FFN_K, FFN_CB), F32))
                upd_w = jnp.concatenate(rows, axis=0)
                upd_b = jnp.sum(g, axis=0, keepdims=True)

                @pl.when(first)
                def _():
                    dcw_ref[:, sl] = upd_w
                    dcb_ref[:, sl] = upd_b

                @pl.when(jnp.logical_not(first))
                def _():
                    dcw_ref[:, sl] += upd_w
                    dcb_ref[:, sl] += upd_b

                du = cw_ref[FFN_K - 1:FFN_K, sl] * g
                for s in range(1, FFN_K):
                    du = du + cw_ref[FFN_K - 1 - s:FFN_K - s, sl] * ahead[s]
                du_ref[:, sl] = jnp.where(grow >= PAD_ROWS, du, 0.0).astype(BF16)
                nx_scr[:, sl] = g[:8]

    wide = pl.BlockSpec((tr, 2 * D_FF), lambda i: (n - 1 - i, 0))
    return pl.pallas_call(
        body, name="ffn_gate_bwd", grid=(n,),
        in_specs=[wide, wide, pl.BlockSpec((tr, D), lambda i: (n - 1 - i, 0)), _full(w_down.shape), _full((8, 2 * D_FF))],
        out_specs=[wide, _full((8, 2 * D_FF)), _full((1, 2 * D_FF))],
        out_shape=[_sds((lp, 2 * D_FF), BF16), _sds((8, 2 * D_FF), F32), _sds((1, 2 * D_FF), F32)],
        scratch_shapes=[pltpu.VMEM((8, 2 * D_FF), F32)],
        compiler_params=_cp("arbitrary"))(u, uc, dd, w_down, cw)


def _mla_bwd(dq, dk, dv, lat, qg, kvg, wq, wkv, cos, sa, sb):
    lp = lat.shape[0]
    tr = _rt(lp, (320, 128))

    def body(dq_ref, dk_ref, dv_ref, lat_ref, qg_ref, kvg_ref, wq_ref, wkv_ref, cos_ref, sa_ref, sb_ref,
             dqf_ref, dkvf_ref, dlat_ref, dqg_ref, dkvg_ref):
        i = pl.program_id(0)
        cos_v, sa_v, sb_v = cos_ref[...], sa_ref[...], sb_ref[...]
        dkpe = jnp.zeros((tr, 128), F32)
        for h in range(MLA_H):
            dqh = dq_ref[h] * SOFTMAX_SCALE
            dqf_ref[:, h * DN:(h + 1) * DN] = dqh[:, :DN].astype(BF16)
            dqf_ref[:, D + h * 128:D + (h + 1) * 128] = _rope_t(dqh[:, DN:], cos_v, sa_v, sb_v).astype(BF16)
            dkh = dk_ref[h]
            dkvf_ref[:, h * DN:(h + 1) * DN] = dkh[:, :DN].astype(BF16)
            dkpe = dkpe + dkh[:, DN:]
            dkvf_ref[:, D + h * DV:D + (h + 1) * DV] = dv_ref[h].astype(BF16)
        dql = _dot_nt(dqf_ref[...], wq_ref[...])
        dkl = _dot_nt(dkvf_ref[...], wkv_ref[...])
        lat_v = lat_ref[...]
        dqc, dqg = _rms_bwd(lat_v[:, :QR], qg_ref[...], dql)
        dkc, dkg = _rms_bwd(lat_v[:, QR:QR + KVR], kvg_ref[...], dkl)
        dlat_ref[:, :QR] = dqc
        dlat_ref[:, QR:QR + KVR] = dkc
        dlat_ref[:, QR + KVR:] = _rope_t(dkpe, cos_v, sa_v, sb_v)
        _acc_rows(dqg_ref, dqg, i == 0)
        _acc_rows(dkvg_ref, dkg, i == 0)

    hb = lambda w: pl.BlockSpec((MLA_H, tr, w), lambda i: (0, i, 0))
    return pl.pallas_call(
        body, name="mla_bwd", grid=(lp // tr,),
        in_specs=[hb(256), hb(256), hb(128), _rows(tr, LAT_W), _full((1, QR)), _full((1, KVR)), _full(wq.shape),
                  _full(wkv.shape), _rows(tr, 128), _rows(tr, 128), _rows(tr, 128)],
        out_specs=[_rows(tr, 2 * D), _rows(tr, 2 * D), _rows(tr, LAT_W), _full((1, QR)), _full((1, KVR))],
        out_shape=[_sds((lp, 2 * D), BF16), _sds((lp, 2 * D), BF16), _sds((lp, LAT_W), F32), _sds((1, QR), F32),
                   _sds((1, KVR), F32)],
        compiler_params=_cp("arbitrary"))(dq, dk, dv, lat, qg, kvg, wq, wkv, cos, sa, sb)


def _inproj_bwd(dlat, dz, dxbc, ddt, w, h0, g, dh1, ride=()):
    lp = h0.shape[0]
    tr = _rt(lp, (320, 128))
    segs = ((0, LAT_W), (LAT_W, LAT_W + D_SSM), (LAT_W + D_SSM, LAT_W + D_SSM + D_XBC), (IN_P - 128, IN_P))
    nr = len(ride)
    steps = lp // tr

    def body(dl_ref, dz_ref, dx_ref, dt_ref, w_ref, h_ref, g_ref, r_ref, *rest):
        ps = rest[:nr]
        o_ref, dg_ref = rest[nr:nr + 2]
        got, sems = rest[nr + 2:2 * nr + 2], rest[2 * nr + 2:]
        step = pl.program_id(0)
        if nr:
            pl.when(step == 0)(lambda: _ride_exchange(ps, got, *sems, 0))
        dhn = jnp.zeros((tr, D), F32)
        for ref, (a, b) in zip((dl_ref, dz_ref, dx_ref, dt_ref), segs):
            dhn = dhn + _dot_nt(ref[...].astype(BF16), w_ref[:, a:b])
        dx, dg = _rms_bwd(h_ref[...], g_ref[...], dhn)
        o_ref[...] = dx + r_ref[...]
        _acc_rows(dg_ref, dg, step == 0)
        if nr:
            pl.when(step == steps - 1)(lambda: _ride_exchange(ps, got, *sems, 1))

    sems = [pltpu.SemaphoreType.DMA((3 * nr,)), pltpu.SemaphoreType.DMA((3 * nr,))] if nr else []
    outs = pl.pallas_call(
        body, name="inproj_bwd", grid=(steps,),
        in_specs=[_rows(tr, LAT_W), _rows(tr, D_SSM), _rows(tr, D_XBC), _rows(tr, 128), _full(w.shape), _rows(tr, D),
                  _full((1, D)), _rows(tr, D)] + [ANY] * nr,
        out_specs=[_rows(tr, D), _full((1, D))] + [ANY] * nr,
        out_shape=[_sds((lp, D), F32), _sds((1, D), F32)] + [_sds((3,) + p.shape[1:], p.dtype) for p in ride],
        scratch_shapes=sems, compiler_params=_cp("arbitrary"))(dlat, dz, dxbc, ddt, w, h0, g, dh1, *ride)
    return outs[0], outs[1], list(outs[2:])


def _rope_tables(lp):
    pos = (jnp.arange(lp, dtype=jnp.int32) - PAD_ROWS).astype(F32)
    inv = ROPE_THETA ** (-jnp.arange(0, DR, 2, dtype=F32) / DR)
    ang = pos[:, None] * inv[None, :]
    cos, sin = jnp.cos(ang), jnp.sin(ang)
    zero = jnp.zeros_like(sin)
    cos128 = jnp.concatenate([cos, cos, cos, cos], axis=1)
    sa128 = jnp.concatenate([-sin, zero, -sin, zero], axis=1)
    sb128 = jnp.concatenate([zero, sin, zero, sin], axis=1)
    return cos128, sa128, sb128


def _pad_rows8(w):
    return jnp.concatenate([w, jnp.zeros((8 - w.shape[0], w.shape[1]), w.dtype)], axis=0)


def _lane_pad(v):
    return jnp.concatenate([v, jnp.zeros((v.shape[0], 128 - v.shape[1]), v.dtype)], axis=1)


def _late_weights(bufs):
    w_out, w_up, w_down = bufs
    return dict(w_out=w_out.reshape(2 * D, D), w_up=w_up.reshape(N_CHIPS, D, 2 * D_FF // N_CHIPS), w_down=w_down.reshape(D_FF, D))


def _device_step(x, tgt, meta, p, late_bufs=(), early_reduce=None, last_reduce=None):
    s = x.shape[0]
    lp = s + FRONT
    zpad = jnp.zeros((PAD_ROWS, D), F32)
    h0 = jnp.concatenate([zpad, meta, x], axis=0)
    cos, sa, sb = _rope_tables(lp)

    w_in = p["w_in"]
    w_in_p = jnp.concatenate([w_in[:, :QR + KVR + DR], jnp.zeros((D, 64), BF16), w_in[:, QR + KVR + DR:],
                              jnp.zeros((D, 128 - SSM_H), BF16)], axis=1)
    w_uq = p["w_uq"]
    wq_p = jnp.concatenate([w_uq[:, :, :DN].reshape(QR, MLA_H * DN),
                            jnp.concatenate([w_uq[:, :, DN:], jnp.zeros((QR, MLA_H, 128 - DR), BF16)], axis=2).reshape(QR, MLA_H * 128)],
                           axis=1)
    w_ukv = p["w_ukv"]
    wkv_p = jnp.concatenate([w_ukv[:, :, :DN].reshape(KVR, MLA_H * DN), w_ukv[:, :, DN:].reshape(KVR, MLA_H * DV)], axis=1)
    scw = _pad_rows8(p["ssm_conv_w"])
    fcw = _pad_rows8(p["ffn_conv_w"])
    dtb, alog = _lane_pad(p["ssm_dt_bias"]), _lane_pad(p["ssm_A_log"])
    dtbT, alogT = p["ssm_dt_bias"].reshape(SSM_H, 1), p["ssm_A_log"].reshape(SSM_H, 1)
    d_e = jnp.repeat(p["ssm_D"], SSM_P, axis=1)

    hn, lat, z, xbc, dtr = _inproj(h0, p["norm_mix_pre"], w_in_p)
    dtrT = dtr[:, :SSM_H].T
    q, k, v, qlat, kvlat = _mla_prep(lat, p["q_a_norm"], p["kv_a_norm"], wq_p, wkv_p, cos, sa, sb)
    o, lse, gathered = _attn_fwd(q, k, v, ride=late_bufs)
    if late_bufs:
        p = dict(p, **_late_weights(gathered))
    ssm, st = _ssd_fwd(xbc, z, dtr, dtrT, scw, p["ssm_conv_b"], dtb, dtbT, alog, alogT, d_e, p["ssm_norm"])
    mixin, mix, h1 = _mixout_fwd(o, ssm, h0, p["attn_out_norm"], p["norm_mix_post"], p["w_out"])
    hn2, u = _ffn_up(h1, p["norm_ffn_pre"], p["w_up"])
    uc, a = _ffn_gate(u, fcw, p["ffn_conv_b"])
    dh2, dd, g_ffn_post, loss = _ffn_down(a, p["w_down"], h1, tgt, p["norm_ffn_post"])

    g_w_down = _mm_tn(a, dd, "ffn_dw_down", tn=512)
    du, g_fcw, g_fcb = _ffn_gate_bwd(u, uc, dd, p["w_down"], fcw)
    dh1, g_ffn_pre = _ffn_in_bwd(du, p["w_up"], h1, p["norm_ffn_pre"], dh2)
    g_w_up = _mm_tn(hn2, du, "ffn_dw_up", tn=D_FF // 2, chunked=True)
    ffn_gs = early_reduce.halves(dict(w_up=g_w_up, w_down=g_w_down)) if early_reduce else ()
    dmix, dssm, do, g_mix_post, g_ao, delta, ffn_sib = _mixout_bwd(mix, p["norm_mix_post"], dh1, o, p["attn_out_norm"],
                                                                  p["w_out"], ride=ffn_gs)
    g_w_out = _mm_tn(mixin, dmix, "mix_dw_out", tn=512)
    t = _rt(lp, (640, 128))
    pairs = early_reduce.pairs(dict(w_out=g_w_out), ffn_gs, ffn_sib) if early_reduce else ()
    dq, dk, dv, got = _attn_bwd(q, k, v, do, lse[:, 0, :].reshape(MLA_H, lp // t, 1, t), delta.reshape(MLA_H, lp // t, 1, t),
                                ride=pairs)
    dqf, dkvf, dlat, g_qa, g_kva = _mla_bwd(dq, dk, dv, lat, p["q_a_norm"], p["kv_a_norm"], wq_p, wkv_p, cos, sa, sb)
    g_wq_p = _mm_tn(qlat, dqf, "mla_dw_uq")
    g_wkv_p = _mm_tn(kvlat, dkvf, "mla_dw_ukv")
    dz, dxbc, ddtr, g_scw, g_scb, g_dtb, g_alog, g_dd, g_ssm_norm = _ssd_bwd(
        dssm, xbc, z, dtr, dtrT, st, scw, p["ssm_conv_b"], dtb, dtbT, alog, alogT, d_e, p["ssm_norm"])
    g_in_p = jnp.concatenate([_mm_tn(hn, dlat, "in_dw_lat"), _mm_tn(hn, dz, "in_dw_z"), _mm_tn(hn, dxbc, "in_dw_xbc"),
                              _mm_tn(hn, ddtr, "in_dw_dt")], axis=1)
    g_w_in = jnp.concatenate([g_in_p[:, :QR + KVR + DR], g_in_p[:, LAT_W:LAT_W + D_SSM + D_XBC + SSM_H]], axis=1)
    g_w_uq = jnp.concatenate([g_wq_p[:, :D].reshape(QR, MLA_H, DN), g_wq_p[:, D:].reshape(QR, MLA_H, 128)[:, :, :DR]], axis=2)
    g_w_ukv = jnp.concatenate([g_wkv_p[:, :D].reshape(KVR, MLA_H, DN), g_wkv_p[:, D:].reshape(KVR, MLA_H, DV)], axis=2)
    pairs2 = last_reduce(dict(w_in=g_w_in, w_uq=g_w_uq, w_ukv=g_w_ukv)) if last_reduce else ()
    dh0, g_mix_pre, got2 = _inproj_bwd(dlat, dz, dxbc, ddtr, w_in_p, h0, p["norm_mix_pre"], dh1, ride=pairs2)
    grads = dict(
        norm_mix_pre=g_mix_pre, norm_mix_post=g_mix_post, norm_ffn_pre=g_ffn_pre, norm_ffn_post=g_ffn_post, w_in=g_w_in,
        q_a_norm=g_qa, w_uq=g_w_uq, kv_a_norm=g_kva, w_ukv=g_w_ukv, attn_out_norm=g_ao, ssm_conv_w=g_scw[:SSM_K],
        ssm_conv_b=g_scb, ssm_dt_bias=g_dtb[:, :SSM_H], ssm_A_log=g_alog[:, :SSM_H], ssm_D=g_dd[:, :SSM_H],
        ssm_norm=g_ssm_norm, w_out=g_w_out, w_up=g_w_up, ffn_conv_w=g_fcw[:FFN_K], ffn_conv_b=g_fcb, w_down=g_w_down)
    return loss, dh0[FRONT:], dh0[PAD_ROWS:FRONT], grads, (list(pairs2) + list(pairs), list(got2) + list(got))


N_CHIPS = 4
BIG = (("w_in", (D, D_IN // N_CHIPS)), ("w_uq", (QR // N_CHIPS, MLA_H, DN + DR)), ("w_ukv", (KVR // N_CHIPS, MLA_H, DN + DV)),
       ("w_out", (2 * D // N_CHIPS, D)), ("w_up", (D, 2 * D_FF // N_CHIPS)), ("w_down", (D_FF // N_CHIPS, D)))
SMALL_SHARDED = (("meta_tokens", (N_META, D // N_CHIPS)), ("ssm_conv_w", (SSM_K, D_XBC // N_CHIPS)),
                 ("ffn_conv_w", (FFN_K, 2 * D_FF // N_CHIPS)))
SMALL_REPL = (("norm_mix_pre", D), ("norm_mix_post", D), ("norm_ffn_pre", D), ("norm_ffn_post", D), ("q_a_norm", QR),
              ("kv_a_norm", KVR), ("attn_out_norm", D), ("ssm_conv_b", D_XBC), ("ssm_dt_bias", SSM_H), ("ssm_A_log", SSM_H),
              ("ssm_D", SSM_H), ("ssm_norm", D_SSM), ("ffn_conv_b", 2 * D_FF))
ANY = pl.BlockSpec(memory_space=pl.ANY)


def _pad128(v):
    n = v.shape[0]
    return jnp.concatenate([v, jnp.zeros(((-n) % 128,), v.dtype)]) if n % 128 else v


def _pack_rows(vs, rows):
    flat = jnp.concatenate([_pad128(v.reshape(-1)) for v in vs])
    flat = jnp.concatenate([flat, jnp.zeros((rows * 128 - flat.shape[0],), flat.dtype)])
    return flat.reshape(rows, 128)


def _unpack_rows(pack, sizes):
    flat = pack.reshape(-1)
    out, off = [], 0
    for n in sizes:
        out.append(flat[off:off + n])
        off += n + (-n) % 128
    return out


def _my_place():
    return lax.axis_index("x"), lax.axis_index("y"), lax.axis_index("c")


def _other_chips(x, y):
    return [(1 - x, y), (x, 1 - y), (1 - x, 1 - y)]


def _remote(src, dst, send, recv, dev):
    return pltpu.make_async_remote_copy(src_ref=src, dst_ref=dst, send_sem=send, recv_sem=recv, device_id=dev,
                                        device_id_type=MESH)


SMALL_AG_ROWS = 80


def _gather_weights(shards, small, name):
    arrs = list(shards) + ([] if small is None else [small])
    n, nb = len(arrs), len(shards)

    def body(*refs):
        ins, outs = refs[:n], refs[n:2 * n]
        send, recv, lsem = refs[2 * n:]
        x, y, c = _my_place()
        me = 2 * x + y
        chips = _other_chips(x, y)
        slot = lambda w, chip, cc: outs[w].at[chip, cc] if w < nb else outs[w].at[chip]
        mine = lambda w: slot(w, me, c) if w < nb else ins[w]
        loc = [pltpu.make_async_copy(ins[w], outs[w].at[me], lsem.at[w - nb]) for w in range(nb, n)]
        for cp in loc:
            cp.start()
        sends = []
        for w in range(n):
            for kk, (cx, cy) in enumerate(chips):
                sends.append(_remote(mine(w), slot(w, me, c), send.at[3 * w + kk], recv.at[3 * w + kk], (cx, cy, c)))
        for cp in sends:
            cp.start()
        for w in range(nb):
            for kk, (cx, cy) in enumerate(chips):
                src = 2 * cx + cy
                _remote(mine(w), slot(w, src, c), send.at[3 * w + kk], recv.at[3 * w + kk], (cx, cy, c)).wait_recv()
                fwd = _remote(slot(w, src, c), slot(w, src, c), send.at[3 * (n + w) + kk], recv.at[3 * (n + w) + kk], (x, y, 1 - c))
                fwd.start()
                sends.append(fwd)
        for w in range(n):
            for kk, (cx, cy) in enumerate(chips):
                src = 2 * cx + cy
                if w < nb:
                    _remote(mine(w), slot(w, src, 1 - c), send.at[3 * (n + w) + kk], recv.at[3 * (n + w) + kk],
                            (x, y, 1 - c)).wait_recv()
                else:
                    _remote(ins[w], slot(w, src, c), send.at[3 * w + kk], recv.at[3 * w + kk], (cx, cy, c)).wait_recv()
        for cp in sends:
            cp.wait_send()
        for cp in loc:
            cp.wait()

    return pl.pallas_call(
        body, name=name, in_specs=[ANY] * n, out_specs=[ANY] * n,
        out_shape=[_sds(a.shape, a.dtype) for a in shards] + ([] if small is None else [_sds((N_CHIPS,) + small.shape, small.dtype)]),
        input_output_aliases={w: w for w in range(nb)},
        scratch_shapes=[pltpu.SemaphoreType.DMA((3 * (n + nb),)), pltpu.SemaphoreType.DMA((3 * (n + nb),)),
                        pltpu.SemaphoreType.DMA((max(n - nb, 1),))])(*arrs)


def _place_own(wt, chip, name):
    r, c = wt.shape
    tr = _row_tile(r, c)

    def body(c_ref, w_ref, o_ref):
        o_ref[...] = w_ref[...].astype(BF16)

    return pl.pallas_call(
        body, name=name, out_shape=_sds((N_CHIPS, r, c), BF16),
        grid_spec=pltpu.PrefetchScalarGridSpec(
            num_scalar_prefetch=1, grid=(r // tr,), in_specs=[pl.BlockSpec((tr, c), lambda i, cr: (i, 0))],
            out_specs=pl.BlockSpec((None, tr, c), lambda i, cr: (cr[0], i, 0))),
        compiler_params=_cp("parallel"))(chip, wt)


def _send_sibling_halves(gs, name):
    n = len(gs)

    def body(*refs):
        ins, outs, send, recv = refs[:n], refs[n:2 * n], refs[2 * n], refs[2 * n + 1]
        x, y, c = _my_place()
        cps = [_remote(ins[w].at[:, 1 - c], outs[w], send.at[w], recv.at[w], (x, y, 1 - c)) for w in range(n)]
        for cp in cps:
            cp.start()
        for cp in cps:
            cp.wait()

    return pl.pallas_call(
        body, name=name, in_specs=[ANY] * n, out_specs=[ANY] * n,
        out_shape=[_sds((g.shape[0],) + g.shape[2:], g.dtype) for g in gs],
        scratch_shapes=[pltpu.SemaphoreType.DMA((n,)), pltpu.SemaphoreType.DMA((n,))])(*gs)


def _ride_gather(bufs, send, recv, phase):
    n = len(bufs)
    x, y, c = _my_place()
    me = 2 * x + y
    for w in range(n):
        for kk, (cx, cy) in enumerate(_other_chips(x, y)):
            src = 2 * cx + cy
            out = lambda: _remote(bufs[w].at[me, c], bufs[w].at[me, c], send.at[3 * w + kk], recv.at[3 * w + kk], (cx, cy, c))
            fwd = lambda: _remote(bufs[w].at[src, c], bufs[w].at[src, c], send.at[3 * (n + w) + kk],
                                  recv.at[3 * (n + w) + kk], (x, y, 1 - c))
            if phase == 0:
                out().start()
            elif phase == 1:
                _remote(bufs[w].at[me, c], bufs[w].at[src, c], send.at[3 * w + kk], recv.at[3 * w + kk], (cx, cy, c)).wait_recv()
                fwd().start()
            else:
                _remote(bufs[w].at[me, c], bufs[w].at[src, 1 - c], send.at[3 * (n + w) + kk], recv.at[3 * (n + w) + kk],
                        (x, y, 1 - c)).wait_recv()
                out().wait_send()
                fwd().wait_send()


def _ride_sibling(gs, outs, send, recv, phase):
    x, y, c = _my_place()
    for w in range(len(gs)):
        cp = _remote(gs[w].at[:, 1 - c], outs[w], send.at[w], recv.at[w], (x, y, 1 - c))
        if phase == 0:
            cp.start()
        else:
            cp.wait()


def _ride_exchange(ps, outs, send, recv, phase):
    x, y, c = _my_place()
    for w in range(len(ps)):
        for kk, (cx, cy) in enumerate(_other_chips(x, y)):
            cp = _remote(ps[w].at[2 * cx + cy], outs[w].at[kk], send.at[3 * w + kk], recv.at[3 * w + kk], (cx, cy, c))
            if phase == 0:
                cp.start()
            else:
                cp.wait()


def _share_sibling(halves):
    n = len(halves)

    def body(*refs):
        outs, send, recv = refs[n:2 * n], refs[2 * n], refs[2 * n + 1]
        x, y, c = _my_place()
        cps = [_remote(outs[w].at[c], outs[w].at[c], send.at[w], recv.at[w], (x, y, 1 - c)) for w in range(n)]
        for cp in cps:
            cp.start()
        for w in range(n):
            cps[w].wait_send()
            _remote(outs[w].at[c], outs[w].at[1 - c], send.at[w], recv.at[w], (x, y, 1 - c)).wait_recv()

    return pl.pallas_call(
        body, name="share_sibling", in_specs=[ANY] * n, out_specs=[ANY] * n,
        out_shape=[_sds(h.shape, h.dtype) for h in halves], input_output_aliases={w: w for w in range(n)},
        scratch_shapes=[pltpu.SemaphoreType.DMA((n,)), pltpu.SemaphoreType.DMA((n,))])(*halves)


def _row_tile(r, c, cap=1 << 20):
    return next(t for t in range(r, 0, -1) if r % t == 0 and (t % 8 == 0 or t == r) and t * c * 4 <= cap)


def _add_pair(g, t, core, name):
    _, _, r, c = g.shape
    tr = _row_tile(r, c)

    def body(c_ref, g_ref, t_ref, o_ref):
        o_ref[...] = (g_ref[...] + t_ref[...]).astype(BF16)

    return pl.pallas_call(
        body, name=name, out_shape=_sds(t.shape, BF16),
        grid_spec=pltpu.PrefetchScalarGridSpec(
            num_scalar_prefetch=1, grid=(N_CHIPS, r // tr),
            in_specs=[pl.BlockSpec((None, None, tr, c), lambda j, i, cr: (j, cr[0], i, 0)),
                      pl.BlockSpec((None, tr, c), lambda j, i, cr: (j, i, 0))],
            out_specs=pl.BlockSpec((None, tr, c), lambda j, i, cr: (j, i, 0))),
        compiler_params=_cp("parallel", "parallel"))(core, g, t)


def _add_chips(p, got, chip, name):
    _, r, c = p.shape
    tr = _row_tile(r, c)

    def body(c_ref, p_ref, g_ref, o_ref):
        o_ref[...] = ((p_ref[...].astype(F32) + g_ref[0].astype(F32)) + g_ref[1].astype(F32)) + g_ref[2].astype(F32)

    return pl.pallas_call(
        body, name=name, out_shape=_sds((2, r, c), F32),
        grid_spec=pltpu.PrefetchScalarGridSpec(
            num_scalar_prefetch=1, grid=(r // tr,),
            in_specs=[pl.BlockSpec((None, tr, c), lambda i, cr: (cr[0], i, 0)), pl.BlockSpec((3, tr, c), lambda i, cr: (0, i, 0))],
            out_specs=pl.BlockSpec((None, tr, c), lambda i, cr: (cr[1], i, 0))),
        compiler_params=_cp("parallel"))(chip, p, got)


SMALL_AR_ROWS = 424


def _allreduce_small(v):
    def body(v_ref, o_ref, gath, send, recv):
        x, y, c = _my_place()
        me = 4 * x + 2 * y + c
        gath[me] = v_ref[...]
        cps = []
        for dd in range(1, 8):
            dx, dy, dc = dd >> 2, (dd >> 1) & 1, dd & 1
            peer = (1 - x if dx else x, 1 - y if dy else y, 1 - c if dc else c)
            cps.append(_remote(v_ref, gath.at[me], send.at[dd - 1], recv.at[dd - 1], peer))
        for cp in cps:
            cp.start()
        for cp in cps:
            cp.wait()
        acc = gath[0]
        for dev in range(1, 8):
            acc = acc + gath[dev]
        o_ref[...] = acc

    vm = pl.BlockSpec(memory_space=pltpu.VMEM)
    return pl.pallas_call(
        body, name="allreduce_small", in_specs=[vm], out_specs=vm, out_shape=_sds(v.shape, F32),
        scratch_shapes=[pltpu.VMEM((8,) + v.shape, F32), pltpu.SemaphoreType.DMA((7,)), pltpu.SemaphoreType.DMA((7,))])(v)


def _adamw(w, g, m, v, name):
    r, c = w.shape
    tr = _row_tile(r, c)

    def body(w_ref, g_ref, m_ref, v_ref, d_ref, m2_ref, v2_ref):
        gv = g_ref[...]
        m2 = ADAM_B1 * m_ref[...] + (1.0 - ADAM_B1) * gv
        v2 = ADAM_B2 * v_ref[...] + (1.0 - ADAM_B2) * jnp.square(gv)
        m_hat = m2 / (1.0 - ADAM_B1 ** ADAM_STEP)
        v_hat = v2 / (1.0 - ADAM_B2 ** ADAM_STEP)
        d_ref[...] = -ADAM_LR * (m_hat / (jnp.sqrt(v_hat) + ADAM_EPS) + ADAM_WD * w_ref[...])
        m2_ref[...] = m2
        v2_ref[...] = v2

    return pl.pallas_call(
        body, name=name, grid=(r // tr,), in_specs=[_rows(tr, c)] * 4, out_specs=[_rows(tr, c)] * 3,
        out_shape=[_sds((r, c), F32)] * 3, compiler_params=_cp("parallel"))(w, g, m, v)


WEIGHT_NAMES = ("meta_tokens", "norm_mix_pre", "norm_mix_post", "norm_ffn_pre", "norm_ffn_post", "w_in", "q_a_norm", "w_uq",
                "kv_a_norm", "w_ukv", "attn_out_norm", "ssm_conv_w", "ssm_conv_b", "ssm_dt_bias", "ssm_A_log", "ssm_D",
                "ssm_norm", "w_out", "w_up", "ffn_conv_w", "ffn_conv_b", "w_down")
SMALL_ADAM_ROWS = 192


def kernel(x, meta_tokens, norm_mix_pre, norm_mix_post, norm_ffn_pre, norm_ffn_post, w_in, q_a_norm, w_uq, kv_a_norm, w_ukv, attn_out_norm, ssm_conv_w, ssm_conv_b, ssm_dt_bias, ssm_A_log, ssm_D, ssm_norm, w_out, w_up, ffn_conv_w, ffn_conv_b, w_down, loss_target, m_meta_tokens, m_norm_mix_pre, m_norm_mix_post, m_norm_ffn_pre, m_norm_ffn_post, m_w_in, m_q_a_norm, m_w_uq, m_kv_a_norm, m_w_ukv, m_attn_out_norm, m_ssm_conv_w, m_ssm_conv_b, m_ssm_dt_bias, m_ssm_A_log, m_ssm_D, m_ssm_norm, m_w_out, m_w_up, m_ffn_conv_w, m_ffn_conv_b, m_w_down, v_meta_tokens, v_norm_mix_pre, v_norm_mix_post, v_norm_ffn_pre, v_norm_ffn_post, v_w_in, v_q_a_norm, v_w_uq, v_kv_a_norm, v_w_ukv, v_attn_out_norm, v_ssm_conv_w, v_ssm_conv_b, v_ssm_dt_bias, v_ssm_A_log, v_ssm_D, v_ssm_norm, v_w_out, v_w_up, v_ffn_conv_w, v_ffn_conv_b, v_w_down):
    args = locals()
    w = {n: args[n] for n in WEIGHT_NAMES}
    mom = {n: args["m_" + n] for n in WEIGHT_NAMES}
    var = {n: args["v_" + n] for n in WEIGHT_NAMES}
    cx, cy, cc = _my_place()
    chip = 2 * cx + cy

    two_d = {n: (shp[0], functools.reduce(lambda a, b: a * b, shp[1:])) for n, shp in BIG}
    names = [n for n, _ in BIG]
    core_i = cc.astype(jnp.int32).reshape(1)
    chip_i = chip.astype(jnp.int32).reshape(1)
    early, late = names[:3], names[3:]
    halves = lambda n, a: a.reshape(N_CHIPS, 2, two_d[n][0] // 2, two_d[n][1])
    bufs = {n: halves(n, _place_own(w[n].reshape(two_d[n]), chip_i, "place_" + n)) for n in names}
    small = _pack_rows([w[n] for n, _ in SMALL_SHARDED], SMALL_AG_ROWS)
    *gathered, small_all = _gather_weights([bufs[n] for n in early], small, "allgather_weights")
    gath = {n: a.reshape((N_CHIPS,) + two_d[n]) for n, a in zip(early, gathered)}
    p = dict(w_in=gath["w_in"].transpose(1, 0, 2).reshape(D, D_IN), w_uq=gath["w_uq"].reshape(QR, MLA_H, DN + DR),
             w_ukv=gath["w_ukv"].reshape(KVR, MLA_H, DN + DV))
    sm_parts = [_unpack_rows(small_all[j], [a * b for _, (a, b) in SMALL_SHARDED]) for j in range(N_CHIPS)]
    for i, (n, shp) in enumerate(SMALL_SHARDED):
        p[n] = jnp.concatenate([sm_parts[j][i].reshape(shp) for j in range(N_CHIPS)], axis=1)
    for n, _ in SMALL_REPL:
        p[n] = w[n]
    meta_full = p.pop("meta_tokens")

    place_i = jnp.stack([chip, cc]).astype(jnp.int32)

    def pair_sums(gd, group):
        gd = dict(gd)
        if "w_in" in gd:
            gd["w_in"] = gd["w_in"].reshape(D, N_CHIPS, D_IN // N_CHIPS).transpose(1, 0, 2)
        gs = [halves(n, gd[n]) for n in group]
        from_sib = _send_sibling_halves(gs, "reduce_sibling_" + group[0])
        return [_add_pair(gg, tt, core_i, "reduce_pair_" + n) for n, gg, tt in zip(group, gs, from_sib)]

    class LateReduce:
        @staticmethod
        def halves(gd):
            return [halves(n, gd[n]) for n in late[1:]]

        @staticmethod
        def pairs(gd, ffn_gs, ffn_sib):
            gs = [halves(late[0], gd[late[0]])]
            sib = _send_sibling_halves(gs, "reduce_sibling_" + late[0])
            return [_add_pair(gg, tt, core_i, "reduce_pair_" + n)
                    for n, gg, tt in zip(late, gs + list(ffn_gs), list(sib) + list(ffn_sib))]

    loss_part, gx, gmeta, g, (pairs, got) = _device_step(
        x[0], loss_target[0], meta_full, p, late_bufs=[bufs[n] for n in late], early_reduce=LateReduce,
        last_reduce=lambda gd: pair_sums(gd, early))

    small_names = [n for n, _ in SMALL_REPL] + ["ssm_conv_w", "ffn_conv_w"]
    small_sizes = [128] + [sz for _, sz in SMALL_REPL] + [N_META * D, SSM_K * D_XBC, FFN_K * 2 * D_FF]
    order = [n for n, _ in SMALL_REPL]
    sp = _pack_rows([loss_part[0]] + [g[n] for n in order] + [gmeta, g["ssm_conv_w"], g["ffn_conv_w"]], SMALL_AR_ROWS)
    red = _unpack_rows(_allreduce_small(sp), small_sizes)
    loss = red[0][0]
    gfull = {n: red[1 + i].reshape(1, -1) for i, n in enumerate(order)}
    n_r = len(order)
    gfull["meta_tokens"] = lax.dynamic_slice_in_dim(red[1 + n_r].reshape(N_META, D), chip * (D // N_CHIPS), D // N_CHIPS, axis=1)
    gfull["ssm_conv_w"] = lax.dynamic_slice_in_dim(red[2 + n_r].reshape(SSM_K, D_XBC), chip * (D_XBC // N_CHIPS),
                                                   D_XBC // N_CHIPS, axis=1)[None]
    gfull["ffn_conv_w"] = lax.dynamic_slice_in_dim(red[3 + n_r].reshape(FFN_K, 2 * D_FF), chip * (2 * D_FF // N_CHIPS),
                                                   2 * D_FF // N_CHIPS, axis=1)[None]

    mine = [_add_chips(pp, gg, place_i, "reduce_chips_" + n) for n, pp, gg in zip(names, pairs, got)]
    for n, both in zip(names, _share_sibling(mine)):
        gfull[n] = both.reshape(two_d[n])

    delta, new_m, new_v = {}, {}, {}
    for n, shp in BIG:
        outs = _adamw(w[n].reshape(two_d[n]), gfull[n], mom[n].reshape(two_d[n]), var[n].reshape(two_d[n]), "adamw_" + n)
        delta[n], new_m[n], new_v[n] = (o.reshape((1,) + shp) for o in outs)
    snames = order + ["meta_tokens", "ssm_conv_w", "ffn_conv_w"]
    ssizes = [functools.reduce(lambda a, b: a * b, w[n].shape) for n in snames]
    packs = [_pack_rows([d[n] for n in snames], SMALL_ADAM_ROWS) for d in (w, gfull, mom, var)]
    outs = _adamw(*packs, "adamw_small")
    for d, o in zip((delta, new_m, new_v), outs):
        for n, piece in zip(snames, _unpack_rows(o, ssizes)):
            d[n] = piece.reshape(w[n].shape)
    gout = {n: gfull[n].reshape(w[n].shape) for n in WEIGHT_NAMES}
    return (loss, gx[None], *[gout[n] for n in WEIGHT_NAMES], *[delta[n] for n in WEIGHT_NAMES],
            *[new_m[n] for n in WEIGHT_NAMES], *[new_v[n] for n in WEIGHT_NAMES])
```

```python
import functools

import jax
import jax.numpy as jnp
from jax import lax
from jax.experimental import pallas as pl
from jax.experimental.pallas import tpu as pltpu

F32 = jnp.float32
BF16 = jnp.bfloat16

D = 1024
N_META = 16
FRONT = 128
PAD_ROWS = FRONT - N_META
MLA_H = 8
DN, DR, DV = 128, 64, 128
QR, KVR = 384, 256
SOFTMAX_SCALE = (DN + DR) ** -0.5
ROPE_THETA = 10000.0
SSM_H, SSM_P, SSM_G, SSM_N, SSM_K = 16, 64, 2, 128, 4
CHUNK = 128
D_SSM = SSM_H * SSM_P
D_XBC = D_SSM + 2 * SSM_G * SSM_N
GSZ = D_SSM // SSM_G
D_FF = 2816
FFN_K = 3
EPS = 1e-6
IN_SPLITS = (QR, KVR, DR, D_SSM, D_XBC, SSM_H)
D_IN = sum(IN_SPLITS)
LAT_W = 768
IN_P = LAT_W + D_SSM + D_XBC + 128
NEG = -1e30
LOG2E = 1.4426950408889634
LN2 = 0.6931471805599453
Q_SCALE = SOFTMAX_SCALE * LOG2E

ADAM_LR, ADAM_B1, ADAM_B2, ADAM_EPS, ADAM_WD, ADAM_STEP = 0.001, 0.9, 0.999, 1e-08, 0.01, 10

VMEM_LIMIT = 56 * 1024 * 1024
MM_ROWS = (640, 320, 128)
MESH = pl.DeviceIdType.MESH


def _sds(shape, dtype):
    return jax.ShapeDtypeStruct(shape, dtype)


def _cp(*sem):
    return pltpu.CompilerParams(dimension_semantics=sem, vmem_limit_bytes=VMEM_LIMIT)


def _rt(n, cands):
    for c in cands:
        if n % c == 0:
            return c
    raise ValueError((n, cands))


def _full(shape):
    nd = len(shape)
    return pl.BlockSpec(shape, lambda *_: (0,) * nd)


def _rows(tr, c):
    return pl.BlockSpec((tr, c), lambda i: (i, 0))


def _sigmoid(x):
    return 1.0 / (1.0 + jnp.exp(-x))


def _silu(x):
    return x * _sigmoid(x)


def _dsilu(x):
    s = _sigmoid(x)
    return s * (1.0 + x * (1.0 - s))


def _softplus(x):
    return jnp.maximum(x, 0.0) + jnp.log(1.0 + jnp.exp(-jnp.abs(x)))


def _rms(x, g):
    r = lax.rsqrt(jnp.mean(x * x, axis=-1, keepdims=True) + EPS)
    return x * r * g


def _rms_bwd(x, g, dy):
    r = lax.rsqrt(jnp.mean(x * x, axis=-1, keepdims=True) + EPS)
    xh = x * r
    dxh = dy * g
    dx = r * (dxh - xh * jnp.mean(dxh * xh, axis=-1, keepdims=True))
    return dx, jnp.sum(dy * xh, axis=0, keepdims=True)


def _dot(a, b):
    return jnp.dot(a, b, preferred_element_type=F32)


def _dot_nt(a, b):
    return lax.dot_general(a, b, (((1,), (1,)), ((), ())), preferred_element_type=F32)


def _dot_tn(a, b):
    return lax.dot_general(a, b, (((0,), (0,)), ((), ())), preferred_element_type=F32)


def _split3(x):
    hi = x.astype(BF16)
    r = x - hi.astype(F32)
    mid = r.astype(BF16)
    return hi, mid, (r - mid.astype(F32)).astype(BF16)


def _dot_hi(a, b, split="a"):
    if split == "a":
        bb = b.astype(BF16)
        return sum(_dot(t, bb) for t in _split3(a))
    ab = a.astype(BF16)
    return sum(_dot(ab, t) for t in _split3(b))


def _dot_nt_hi(a, b):
    bb = b.astype(BF16)
    return sum(_dot_nt(t, bb) for t in _split3(a))


def _shift_down(x, halo, j):
    xr = pltpu.roll(x, j, axis=0)
    hr = pltpu.roll(halo, j, axis=0)
    row = lax.broadcasted_iota(jnp.int32, (8, x.shape[1]), 0)
    first = jnp.where(row < j, hr, xr[:8])
    return jnp.concatenate([first, xr[8:]], axis=0)


def _shift_up(x, nxt, j):
    t = x.shape[0]
    xr = pltpu.roll(x, t - j, axis=0)
    nr = pltpu.roll(nxt, 8 - j, axis=0)
    row = lax.broadcasted_iota(jnp.int32, (8, x.shape[1]), 0)
    last = jnp.where(row + j >= 8, nr, xr[t - 8:])
    return jnp.concatenate([xr[:t - 8], last], axis=0)


def _acc_rows(ref, val, first):
    @pl.when(first)
    def _():
        ref[...] = val

    @pl.when(jnp.logical_not(first))
    def _():
        ref[...] += val


def _mm_tn(a, b, name, tn=None, trs=(1664, 640, 128), chunked=False):
    r, m = a.shape
    n = b.shape[1]
    tn = n if tn is None else tn
    tr = _rt(r, trs)

    def body(a_ref, b_ref, o_ref):
        part = _dot_tn(a_ref[...].astype(BF16), b_ref[...].astype(BF16))
        _acc_rows(o_ref, part, pl.program_id(1) == 0)

    if chunked:
        out_specs, out_shape = pl.BlockSpec((None, m, tn), lambda j, i: (j, 0, 0)), _sds((n // tn, m, tn), F32)
    else:
        out_specs, out_shape = pl.BlockSpec((m, tn), lambda j, i: (0, j)), _sds((m, n), F32)
    return pl.pallas_call(
        body, name=name, grid=(n // tn, r // tr),
        in_specs=[pl.BlockSpec((tr, m), lambda j, i: (i, 0)), pl.BlockSpec((tr, tn), lambda j, i: (i, j))],
        out_specs=out_specs, out_shape=out_shape, compiler_params=_cp("parallel", "arbitrary"))(a, b)


def _inproj(h0, g, w):
    lp = h0.shape[0]
    tr = _rt(lp, MM_ROWS)
    segs = ((0, LAT_W), (LAT_W, LAT_W + D_SSM), (LAT_W + D_SSM, LAT_W + D_SSM + D_XBC), (IN_P - 128, IN_P))

    def body(h_ref, g_ref, w_ref, hn_ref, lat_ref, z_ref, xbc_ref, dt_ref):
        hn = _rms(h_ref[...], g_ref[...]).astype(BF16)
        hn_ref[...] = hn
        for ref, (a, b) in zip((lat_ref, z_ref, xbc_ref, dt_ref), segs):
            ref[...] = _dot(hn, w_ref[:, a:b])

    return pl.pallas_call(
        body, name="inproj", grid=(lp // tr,), in_specs=[_rows(tr, D), _full((1, D)), _full(w.shape)],
        out_specs=[_rows(tr, D), _rows(tr, LAT_W), _rows(tr, D_SSM), _rows(tr, D_XBC), _rows(tr, 128)],
        out_shape=[_sds((lp, D), BF16), _sds((lp, LAT_W), F32), _sds((lp, D_SSM), F32), _sds((lp, D_XBC), F32),
                   _sds((lp, 128), F32)],
        compiler_params=_cp("parallel"))(h0, g, w)


def _rope(x, cos, sa, sb):
    return x * cos + pltpu.roll(x, 96, axis=1) * sa + pltpu.roll(x, 32, axis=1) * sb


def _rope_t(g, cos, sa, sb):
    return g * cos + pltpu.roll(g * sa, 32, axis=1) + pltpu.roll(g * sb, 96, axis=1)


def _mla_prep(lat, qg, kvg, wq, wkv, cos, sa, sb):
    lp = lat.shape[0]
    tr = _rt(lp, MM_ROWS)

    def body(lat_ref, qg_ref, kvg_ref, wq_ref, wkv_ref, cos_ref, sa_ref, sb_ref, q_ref, k_ref, v_ref, ql_ref, kl_ref):
        lat_v = lat_ref[...]
        ql = _rms(lat_v[:, :QR], qg_ref[...]).astype(BF16)
        kl = _rms(lat_v[:, QR:QR + KVR], kvg_ref[...]).astype(BF16)
        ql_ref[...] = ql
        kl_ref[...] = kl
        cos_v, sa_v, sb_v = cos_ref[...], sa_ref[...], sb_ref[...]
        kpe = _rope(lat_v[:, QR + KVR:LAT_W], cos_v, sa_v, sb_v).astype(BF16)
        for h in range(MLA_H):
            q_ref[h, :, 0:DN] = (_dot(ql, wq_ref[:, h * DN:(h + 1) * DN]) * Q_SCALE).astype(BF16)
            qpe = _dot(ql, wq_ref[:, D + h * 128:D + (h + 1) * 128])
            q_ref[h, :, DN:2 * DN] = (_rope(qpe, cos_v, sa_v, sb_v) * Q_SCALE).astype(BF16)
            k_ref[h, :, 0:DN] = _dot(kl, wkv_ref[:, h * DN:(h + 1) * DN]).astype(BF16)
            k_ref[h, :, DN:2 * DN] = kpe
            v_ref[h] = _dot(kl, wkv_ref[:, D + h * DV:D + (h + 1) * DV]).astype(BF16)

    hb = lambda w: pl.BlockSpec((MLA_H, tr, w), lambda i: (0, i, 0))
    return pl.pallas_call(
        body, name="mla_prep", grid=(lp // tr,),
        in_specs=[_rows(tr, LAT_W), _full((1, QR)), _full((1, KVR)), _full(wq.shape), _full(wkv.shape),
                  _rows(tr, 128), _rows(tr, 128), _rows(tr, 128)],
        out_specs=[hb(256), hb(256), hb(128), _rows(tr, QR), _rows(tr, KVR)],
        out_shape=[_sds((MLA_H, lp, 256), BF16), _sds((MLA_H, lp, 256), BF16), _sds((MLA_H, lp, 128), BF16),
                   _sds((lp, QR), BF16), _sds((lp, KVR), BF16)],
        compiler_params=_cp("parallel"))(lat, qg, kvg, wq, wkv, cos, sa, sb)


def _attn_mask(r0, c0, tq, tk, transposed=False):
    if transposed:
        kk = c0 + lax.broadcasted_iota(jnp.int32, (tk, tq), 0)
        qq = r0 + lax.broadcasted_iota(jnp.int32, (tk, tq), 1)
    else:
        qq = r0 + lax.broadcasted_iota(jnp.int32, (tq, tk), 0)
        kk = c0 + lax.broadcasted_iota(jnp.int32, (tq, tk), 1)
    return jnp.logical_and(kk <= qq, kk >= PAD_ROWS)


def _attn_fwd(q, k, v, ride=()):
    lp = q.shape[1]
    t = _rt(lp, (640, 128))
    nq = lp // t

    hp = 2
    KW = (4, 2, 1)

    nr = len(ride)
    steps = (MLA_H // hp) * nq

    def body(q_ref, k_ref, v_ref, *rest):
        o_ref, lse_ref = rest[nr:nr + 2]
        bufs, sems = rest[nr + 2:2 * nr + 2], rest[2 * nr + 2:]
        qi = pl.program_id(1)
        step = pl.program_id(0) * nq + qi
        if nr:
            pl.when(step == 0)(lambda: _ride_gather(bufs, *sems, 0))
            pl.when(step == steps // 2)(lambda: _ride_gather(bufs, *sems, 1))
        qv = [q_ref[a] for a in range(hp)]

        def tile(kj, carries, bias=None, width=1):
            starts = kj if isinstance(kj, tuple) else (kj,)
            kv_rows = [pl.ds(pl.multiple_of(k0 * t, t), width * t) for k0 in starts]
            out = []
            for a in range(hp):
                m, l, acc = carries[a]
                kk = jnp.concatenate([k_ref[a, r, :] for r in kv_rows], axis=0)
                vv = jnp.concatenate([v_ref[a, r, :] for r in kv_rows], axis=0)
                s = _dot_nt(qv[a], kk)
                if bias is not None:
                    s = s + bias
                m_new = jnp.maximum(m, jnp.max(s, axis=-1, keepdims=True))
                alpha = jnp.exp2(m - m_new)
                p = jnp.exp2(s - m_new)
                l = alpha * l + jnp.sum(p, axis=-1, keepdims=True)
                acc = alpha * acc + _dot(p.astype(BF16), vv)
                out.append((m_new, l, acc))
            return tuple(out)

        key = lax.broadcasted_iota(jnp.int32, (t, t), 1)
        pad_bias = jnp.where(jnp.logical_and(key >= PAD_ROWS, qi > 0), 0.0, NEG)
        diag_bias = jnp.where(_attn_mask(qi * t, qi * t, t, t), 0.0, NEG)
        init = tuple((jnp.full((t, 1), NEG, F32), jnp.zeros((t, 1), F32), jnp.zeros((t, DV), F32)) for _ in range(hp))
        carries = tile((0, qi), init, jnp.concatenate([pad_bias, diag_bias], axis=1))
        nxt = 1
        for width in KW:
            reps = jnp.maximum(qi - nxt, 0) // width
            carries = lax.fori_loop(0, reps, functools.partial(lambda j, c, nxt, width: tile(nxt + width * j, c, width=width),
                                                               nxt=nxt, width=width), carries)
            nxt = nxt + width * reps
        for a in range(hp):
            m, l, acc = carries[a]
            o_ref[:, a * DV:(a + 1) * DV] = acc / l
            lse_ref[a] = jnp.broadcast_to(m + jnp.log(l) * LOG2E, (t, 128)).T[:8]
        if nr:
            pl.when(step == steps - 1)(lambda: _ride_gather(bufs, *sems, 2))

    sems = [pltpu.SemaphoreType.DMA((6 * nr,)), pltpu.SemaphoreType.DMA((6 * nr,))] if nr else []
    outs = pl.pallas_call(
        body, name="attn_fwd", grid=(MLA_H // hp, nq),
        in_specs=[pl.BlockSpec((hp, t, 256), lambda h, i: (h, i, 0)), pl.BlockSpec((hp, lp, 256), lambda h, i: (h, 0, 0)),
                  pl.BlockSpec((hp, lp, 128), lambda h, i: (h, 0, 0))] + [ANY] * nr,
        out_specs=[pl.BlockSpec((t, hp * DV), lambda h, i: (i, h)), pl.BlockSpec((hp, 8, t), lambda h, i: (h, 0, i))] + [ANY] * nr,
        out_shape=[_sds((lp, MLA_H * DV), F32), _sds((MLA_H, 8, lp), F32)] + [_sds(b.shape, b.dtype) for b in ride],
        input_output_aliases={3 + w: 2 + w for w in range(nr)}, scratch_shapes=sems,
        compiler_params=_cp("arbitrary", "arbitrary"))(q, k, v, *ride)
    return outs[0], outs[1], list(outs[2:])


def _mixout_bwd(mix, g_post, dh1, o, g_ao, w_out, ride=()):
    lp = o.shape[0]
    tr = _rt(lp, MM_ROWS)
    nr = len(ride)
    steps = lp // tr

    def body(mix_ref, gp_ref, dh_ref, o_ref, g_ref, w_ref, *rest):
        gs = rest[:nr]
        dmix_ref, dssm_ref, do_ref, dgp_ref, dg_ref, dl_ref = rest[nr:nr + 6]
        from_sib, sems = rest[nr + 6:2 * nr + 6], rest[2 * nr + 6:]
        i = pl.program_id(0)
        if nr:
            pl.when(i == 0)(lambda: _ride_sibling(gs, from_sib, *sems, 0))
        grow = i * tr + lax.broadcasted_iota(jnp.int32, (tr, D), 0)
        dmix, dgp = _rms_bwd(mix_ref[...], gp_ref[...], jnp.where(grow >= PAD_ROWS, dh_ref[...], 0.0))
        dmix = dmix.astype(BF16)
        dmix_ref[...] = dmix
        _acc_rows(dgp_ref, dgp, i == 0)
        dssm_ref[...] = _dot_nt(dmix, w_ref[D:, :])
        ov = o_ref[...]
        do, dg = _rms_bwd(ov, g_ref[...], _dot_nt(dmix, w_ref[:D, :]))
        do_ref[...] = do
        _acc_rows(dg_ref, dg, i == 0)
        prod = do * ov
        lane = lax.broadcasted_iota(jnp.int32, (1, 128), 1)
        cols = jnp.zeros((tr, 128), F32)
        for h in range(MLA_H):
            cols = cols + jnp.sum(prod[:, h * DV:(h + 1) * DV], axis=-1, keepdims=True) * (lane == h).astype(F32)
        dl_ref[...] = cols.T[:MLA_H]
        if nr:
            pl.when(i == steps - 1)(lambda: _ride_sibling(gs, from_sib, *sems, 1))

    sems = [pltpu.SemaphoreType.DMA((nr,)), pltpu.SemaphoreType.DMA((nr,))] if nr else []
    outs = pl.pallas_call(
        body, name="mixout_bwd", grid=(steps,),
        in_specs=[_rows(tr, D), _full((1, D)), _rows(tr, D), _rows(tr, D), _full((1, D)), _full(w_out.shape)] + [ANY] * nr,
        out_specs=[_rows(tr, D), _rows(tr, D), _rows(tr, D), _full((1, D)), _full((1, D)),
                   pl.BlockSpec((MLA_H, tr), lambda i: (0, i))] + [ANY] * nr,
        out_shape=[_sds((lp, D), BF16), _sds((lp, D), F32), _sds((lp, D), F32), _sds((1, D), F32), _sds((1, D), F32),
                   _sds((MLA_H, lp), F32)] + [_sds((g.shape[0],) + g.shape[2:], g.dtype) for g in ride],
        scratch_shapes=sems, compiler_params=_cp("arbitrary"))(mix, g_post, dh1, o, g_ao, w_out, *ride)
    return tuple(outs[:6]) + (list(outs[6:]),)


def _attn_bwd(q, k, v, do, lse_row, delta_row, ride=()):
    lp = q.shape[1]
    t = _rt(lp, (640, 128))
    nq = lp // t

    nr = len(ride)
    QW = (3, 2, 1)

    def body(q_ref, k_ref, v_ref, do_ref, lse_ref, dl_ref, *rest):
        ps = rest[:nr]
        dq_ref, dk_ref, dv_ref = rest[nr:nr + 3]
        got, sems = rest[nr + 3:2 * nr + 3], rest[2 * nr + 3:]
        kj = pl.program_id(1)
        step = pl.program_id(0) * nq + kj
        if nr:
            pl.when(step == 0)(lambda: _ride_exchange(ps, got, *sems, 0))
        kk = k_ref[0]
        vv = v_ref[0]

        @pl.when(kj == 0)
        def _():
            dq_ref[...] = jnp.zeros_like(dq_ref)

        def tile(qi, carry, masked, width=1):
            dk, dv = carry
            q_rows = pl.ds(pl.multiple_of(qi * t, t), width * t)
            qv = q_ref[0, q_rows, :]
            dob = do_ref[q_rows, :].astype(BF16)
            lse_v = jnp.concatenate([lse_ref[0, qi + b] for b in range(width)], axis=1)
            dl_v = jnp.concatenate([dl_ref[0, qi + b] for b in range(width)], axis=1)
            st = _dot_nt(kk, qv)
            if masked:
                st = jnp.where(_attn_mask(qi * t, kj * t, width * t, t, transposed=True), st, NEG)
            pt = jnp.exp2(st - lse_v)
            dpt = _dot_nt(vv, dob)
            dst = (pt * (dpt - dl_v)).astype(BF16)
            dv = dv + _dot(pt.astype(BF16), dob)
            dk = dk + _dot(dst, qv)
            dq_ref[0, q_rows, :] += _dot_tn(dst, kk)
            return dk, dv

        carry = tile(kj, (jnp.zeros((t, 256), F32), jnp.zeros((t, DV), F32)), True)
        split = jnp.where(kj == 0, nq, kj + 1)

        def span(lo, hi, masked, carry):
            for width in QW:
                reps = (hi - lo) // width
                carry = lax.fori_loop(0, reps, functools.partial(
                    lambda j, c, lo, width: tile(lo + width * j, c, masked, width=width), lo=lo, width=width), carry)
                lo = lo + width * reps
            return carry

        dk, dv = span(split, nq, False, span(kj + 1, split, True, carry))
        dk_ref[0] = dk * LN2
        dv_ref[0] = dv
        if nr:
            pl.when(step == MLA_H * nq - 1)(lambda: _ride_exchange(ps, got, *sems, 1))

    stat = pl.BlockSpec((1, nq, 1, t), lambda h, j: (h, 0, 0, 0))
    sems = [pltpu.SemaphoreType.DMA((3 * nr,)), pltpu.SemaphoreType.DMA((3 * nr,))] if nr else []
    outs = pl.pallas_call(
        body, name="attn_bwd", grid=(MLA_H, nq),
        in_specs=[pl.BlockSpec((1, lp, 256), lambda h, j: (h, 0, 0)), pl.BlockSpec((1, t, 256), lambda h, j: (h, j, 0)),
                  pl.BlockSpec((1, t, 128), lambda h, j: (h, j, 0)), pl.BlockSpec((lp, DV), lambda h, j: (0, h)), stat, stat]
        + [ANY] * nr,
        out_specs=[pl.BlockSpec((1, lp, 256), lambda h, j: (h, 0, 0)), pl.BlockSpec((1, t, 256), lambda h, j: (h, j, 0)),
                   pl.BlockSpec((1, t, 128), lambda h, j: (h, j, 0))] + [ANY] * nr,
        out_shape=[_sds((MLA_H, lp, 256), F32), _sds((MLA_H, lp, 256), F32), _sds((MLA_H, lp, 128), F32)]
        + [_sds((3,) + p.shape[1:], p.dtype) for p in ride],
        scratch_shapes=sems, compiler_params=_cp("arbitrary", "arbitrary"))(q, k, v, do, lse_row, delta_row, *ride)
    return outs[0], outs[1], outs[2], list(outs[3:])


def _ssd_consts():
    ri = lax.broadcasted_iota(jnp.int32, (CHUNK, CHUNK), 0)
    ci = lax.broadcasted_iota(jnp.int32, (CHUNK, CHUNK), 1)
    expand = (lax.broadcasted_iota(jnp.int32, (128, D_SSM), 0)
              == lax.broadcasted_iota(jnp.int32, (128, D_SSM), 1) // SSM_P).astype(F32)
    return ri, ci, expand


def _ssd_chunk(c, x_ref, xh_ref, dt_ref, dtT_ref, cw_ref, cb_ref, dtb_ref, dtbT_ref, al_ref, alT_ref):
    ri, ci, expand = _ssd_consts()
    x = x_ref[...]
    halo = jnp.where(c > 0, xh_ref[...], 0.0)
    sh = [x] + [_shift_down(x, halo, j) for j in range(1, SSM_K)]
    cv = cb_ref[...]
    for kk in range(SSM_K):
        cv = cv + cw_ref[kk:kk + 1, :] * sh[SSM_K - 1 - kk]
    xa = _silu(cv)
    grow = c * CHUNK + ri
    gcol = c * CHUNK + lax.broadcasted_iota(jnp.int32, (SSM_H, CHUNK), 1)
    sp = dt_ref[...] + dtb_ref[...]
    spT = dtT_ref[...] + dtbT_ref[...]
    dtc = jnp.where(grow >= PAD_ROWS, _softplus(sp), 0.0)
    dtr = jnp.where(gcol >= PAD_ROWS, _softplus(spT), 0.0)
    arow = -jnp.exp(al_ref[...])
    acolT = -jnp.exp(alT_ref[...])
    ltri = (ci <= ri).astype(F32)
    acs = _dot_hi(ltri, dtc * arow, split="b")
    acsT = _dot_hi(dtr * acolT, (ri <= ci).astype(F32))
    return dict(x=x, sh=sh, cv=cv, xa=xa, sp=sp, dtc=dtc, arow=arow, acs=acs, acsT=acsT, ri=ri, ci=ci, expand=expand,
                grow=grow)


def _ssd_mats(k, s_prev):
    xa, acs, acsT, expand, ri, ci = k["xa"], k["acs"], k["acsT"], k["expand"], k["ri"], k["ci"]
    xs = xa[:, :D_SSM]
    dt_e = _dot_hi(k["dtc"], expand)
    acs_e = _dot_hi(acs, expand)
    last_e = acs_e[CHUNK - 1:CHUNK, :]
    ea = jnp.exp(acs_e)
    f = jnp.exp(last_e - acs_e)
    cd = jnp.exp(last_e)
    xdt = xs * dt_e
    bm = [xa[:, D_SSM + g * SSM_N:D_SSM + (g + 1) * SSM_N] for g in range(SSM_G)]
    cm = [xa[:, D_SSM + (SSM_G + g) * SSM_N:D_SSM + (SSM_G + g + 1) * SSM_N] for g in range(SSM_G)]
    bmb = [b.astype(BF16) for b in bm]
    cmb = [cc.astype(BF16) for cc in cm]
    cb = [_dot_nt(cmb[g], bmb[g]) for g in range(SSM_G)]
    lam, mm = [], []
    causal = jnp.where(ci <= ri, 0.0, NEG)
    for h in range(SSM_H):
        lam_h = jnp.exp((acs[:, h:h + 1] - acsT[h:h + 1, :]) + causal)
        lam.append(lam_h)
        mm.append(cb[h // (SSM_H // SSM_G)] * lam_h)
    lo = lax.broadcasted_iota(jnp.int32, (CHUNK, 128), 1) < SSM_P
    xdt_h = []
    for h in range(SSM_H):
        pair = xdt[:, (h // 2) * 128:(h // 2 + 1) * 128]
        xdt_h.append(jnp.where(lo if h % 2 == 0 else jnp.logical_not(lo), pair, 0.0).astype(BF16))
    ydiag = jnp.concatenate(
        [_dot(mm[2 * j].astype(BF16), xdt_h[2 * j]) + _dot(mm[2 * j + 1].astype(BF16), xdt_h[2 * j + 1])
         for j in range(SSM_H // 2)], axis=1)
    t_off = [_dot(cmb[g], s_prev[g].astype(BF16)) for g in range(SSM_G)]
    yoff = jnp.concatenate(t_off, axis=1) * ea
    return dict(xs=xs, dt_e=dt_e, acs_e=acs_e, ea=ea, f=f, cd=cd, xdt=xdt, bm=bm, cm=cm, bmb=bmb, cmb=cmb, cb=cb, lam=lam,
                mm=mm, lo=lo, xdt_h=xdt_h, ydiag=ydiag, t_off=t_off, yoff=yoff)


def _ssd_specs(nc, rev):
    ix = (lambda i: nc - 1 - i) if rev else (lambda i: i)
    return [
        pl.BlockSpec((CHUNK, D_XBC), lambda i: (ix(i), 0)),
        pl.BlockSpec((8, D_XBC), lambda i: (jnp.maximum(ix(i) * (CHUNK // 8) - 1, 0), 0)),
        pl.BlockSpec((CHUNK, D_SSM), lambda i: (ix(i), 0)),
        pl.BlockSpec((CHUNK, 128), lambda i: (ix(i), 0)),
        pl.BlockSpec((SSM_H, CHUNK), lambda i: (0, ix(i))),
        _full((8, D_XBC)), _full((1, D_XBC)), _full((1, 128)), _full((SSM_H, 1)), _full((1, 128)), _full((SSM_H, 1)),
        _full((1, D_SSM)), _full((1, D_SSM)),
    ]


def _ssd_fwd(xbc, z, dtr, dtrT, cw, cb, dtb, dtbT, alog, alogT, d_e, ng):
    lp = xbc.shape[0]
    nc = lp // CHUNK

    def body(x_ref, xh_ref, z_ref, dt_ref, dtT_ref, cw_ref, cb_ref, dtb_ref, dtbT_ref, al_ref, alT_ref, de_ref, ng_ref,
             y_ref, st_ref, s_scr):
        c = pl.program_id(0)

        @pl.when(c == 0)
        def _():
            s_scr[...] = jnp.zeros_like(s_scr)

        k = _ssd_chunk(c, x_ref, xh_ref, dt_ref, dtT_ref, cw_ref, cb_ref, dtb_ref, dtbT_ref, al_ref, alT_ref)
        s_prev = [s_scr[g] for g in range(SSM_G)]
        st_ref[0] = s_scr[...]
        m = _ssd_mats(k, s_prev)
        xd = (m["xdt"] * m["f"]).astype(BF16)
        for g in range(SSM_G):
            sl = slice(g * GSZ, (g + 1) * GSZ)
            s_scr[g] = m["cd"][:, sl] * s_prev[g] + _dot(m["bm"][g].T.astype(BF16), xd[:, sl])
        y = m["ydiag"] + m["yoff"] + de_ref[...] * m["xs"]
        u = y * _silu(z_ref[...])
        outs = []
        for g in range(SSM_G):
            ug = u[:, g * GSZ:(g + 1) * GSZ]
            outs.append(ug * lax.rsqrt(jnp.mean(ug * ug, axis=-1, keepdims=True) + EPS))
        y_ref[...] = jnp.concatenate(outs, axis=1) * ng_ref[...]

    return pl.pallas_call(
        body, name="ssd_fwd", grid=(nc,), in_specs=_ssd_specs(nc, False),
        out_specs=[_rows(CHUNK, D_SSM), pl.BlockSpec((1, SSM_G, SSM_N, GSZ), lambda i: (i, 0, 0, 0))],
        out_shape=[_sds((lp, D_SSM), F32), _sds((nc, SSM_G, SSM_N, GSZ), F32)],
        scratch_shapes=[pltpu.VMEM((SSM_G, SSM_N, GSZ), F32)],
        compiler_params=_cp("arbitrary"))(xbc, xbc, z, dtr, dtrT, cw, cb, dtb, dtbT, alog, alogT, d_e, ng)


def _ssd_bwd(dssm, xbc, z, dtr, dtrT, st, cw, cb, dtb, dtbT, alog, alogT, d_e, ng):
    lp = xbc.shape[0]
    nc = lp // CHUNK
    hpg = SSM_H // SSM_G

    def body(dy_ref, x_ref, xh_ref, z_ref, dt_ref, dtT_ref, st_ref, cw_ref, cb_ref, dtb_ref, dtbT_ref, al_ref, alT_ref,
             de_ref, ng_ref, dz_ref, dx_ref, ddt_ref, dcw_ref, dcb_ref, ddtb_ref, dal_ref, dd_ref, dng_ref, ds_scr, nx_scr):
        i = pl.program_id(0)
        c = nc - 1 - i
        first = i == 0

        @pl.when(first)
        def _():
            ds_scr[...] = jnp.zeros_like(ds_scr)
            nx_scr[...] = jnp.zeros_like(nx_scr)

        k = _ssd_chunk(c, x_ref, xh_ref, dt_ref, dtT_ref, cw_ref, cb_ref, dtb_ref, dtbT_ref, al_ref, alT_ref)
        s_prev = [st_ref[0, g] for g in range(SSM_G)]
        m = _ssd_mats(k, s_prev)
        ri, ci, expand = k["ri"], k["ci"], k["expand"]
        xs, acs, acsT = m["xs"], k["acs"], k["acsT"]
        zv = z_ref[...]
        dout = dy_ref[...]
        ngv = ng_ref[...]
        y = m["ydiag"] + m["yoff"] + de_ref[...] * xs
        sz = _silu(zv)
        u = y * sz
        du_parts, dng_parts = [], []
        for g in range(SSM_G):
            sl = slice(g * GSZ, (g + 1) * GSZ)
            dug, dngg = _rms_bwd(u[:, sl], ngv[:, sl], dout[:, sl])
            du_parts.append(dug)
            dng_parts.append(dngg)
        du = jnp.concatenate(du_parts, axis=1)
        _acc_rows(dng_ref, jnp.concatenate(dng_parts, axis=1), first)
        dy = du * sz
        dz_ref[...] = du * y * _dsilu(zv)
        dd_e = jnp.sum(dy * xs, axis=0, keepdims=True)
        _acc_rows(dd_ref, _dot_nt_hi(dd_e, expand), first)
        dxs = de_ref[...] * dy
        dacs_e = dy * m["yoff"]
        dtg = (dy * m["ea"]).astype(BF16)
        dxdt = jnp.zeros_like(xs)
        dlast_e = []
        db, dc, ds_prev = [], [], []
        xd = m["xdt"] * m["f"]
        dxd_all = []
        for g in range(SSM_G):
            sl = slice(g * GSZ, (g + 1) * GSZ)
            dsg = ds_scr[g]
            spb = s_prev[g].astype(BF16)
            dc.append(_dot_nt(dtg[:, sl], spb))
            dsp = _dot(m["cm"][g].T.astype(BF16), dtg[:, sl]) + m["cd"][:, sl] * dsg
            ds_prev.append(dsp)
            dlast_e.append(jnp.sum(dsg * s_prev[g], axis=0, keepdims=True) * m["cd"][:, sl])
            dsb = dsg.astype(BF16)
            db.append(_dot_nt(xd[:, sl].astype(BF16), dsb))
            dxd_all.append(_dot(m["bmb"][g], dsb))
        dxd = jnp.concatenate(dxd_all, axis=1)
        dxdt = dxd * m["f"]
        dff = dxd * xd
        dacs_e = dacs_e - dff
        dlast_row = jnp.concatenate(dlast_e, axis=1) + jnp.sum(dff, axis=0, keepdims=True)
        dacs = jnp.zeros((CHUNK, 128), F32)
        dacs_t = jnp.zeros((CHUNK, CHUNK), F32)
        lane = lax.broadcasted_iota(jnp.int32, (1, 128), 1)
        dgs = [jnp.zeros((CHUNK, CHUNK), F32) for _ in range(SSM_G)]
        dxdt_pairs = []
        for h in range(SSM_H):
            g = h // hpg
            pr = slice((h // 2) * 128, (h // 2 + 1) * 128)
            lo_h = m["lo"] if h % 2 == 0 else jnp.logical_not(m["lo"])
            dyp = jnp.where(lo_h, dy[:, pr], 0.0).astype(BF16)
            dm = _dot_nt(dyp, m["xdt"][:, pr].astype(BF16))
            dgs[g] = dgs[g] + dm * m["lam"][h]
            w_h = dm * m["mm"][h]
            dacs = dacs + jnp.sum(w_h, axis=1, keepdims=True) * (lane == h).astype(F32)
            dacs_t = dacs_t + jnp.where(ri == h, jnp.sum(w_h, axis=0, keepdims=True), 0.0)
            part = _dot_tn(m["mm"][h].astype(BF16), dyp)
            if h % 2 == 0:
                dxdt_pairs.append(part)
            else:
                dxdt_pairs[-1] = dxdt_pairs[-1] + part
        dxdt = dxdt + jnp.concatenate(dxdt_pairs, axis=1)
        for g in range(SSM_G):
            dgb = dgs[g].astype(BF16)
            dc[g] = dc[g] + _dot(dgb, m["bmb"][g])
            db[g] = db[g] + _dot_tn(dgb, m["cmb"][g])
        dacs = dacs - dacs_t.T + _dot_nt_hi(dacs_e, expand)
        dlast = _dot_nt_hi(dlast_row, expand)
        dacs = dacs + jnp.where(ri == CHUNK - 1, dlast, 0.0)
        dxs = dxs + dxdt * m["dt_e"]
        ddt = _dot_nt_hi(dxdt * xs, expand)
        da = _dot_hi((ri <= ci).astype(F32), dacs, split="b")
        ddt = ddt + da * k["arow"]
        dA = jnp.sum(da * k["dtc"], axis=0, keepdims=True)
        _acc_rows(dal_ref, dA * k["arow"], first)
        ddtr = jnp.where(k["grow"] >= PAD_ROWS, ddt * _sigmoid(k["sp"]), 0.0)
        ddt_ref[...] = ddtr
        _acc_rows(ddtb_ref, jnp.sum(ddtr, axis=0, keepdims=True), first)
        for g in range(SSM_G):
            ds_scr[g] = ds_prev[g]
        dxa = jnp.concatenate([dxs] + db + dc, axis=1)
        dcv = dxa * _dsilu(k["cv"])
        _acc_rows(dcb_ref, jnp.sum(dcv, axis=0, keepdims=True), first)
        dcw_rows = [jnp.sum(dcv * k["sh"][SSM_K - 1 - kk], axis=0, keepdims=True) for kk in range(SSM_K)]
        dcw_rows.append(jnp.zeros((8 - SSM_K, D_XBC), F32))
        _acc_rows(dcw_ref, jnp.concatenate(dcw_rows, axis=0), first)
        nxt = nx_scr[...]
        dx = cw_ref[SSM_K - 1:SSM_K, :] * dcv
        for j in range(1, SSM_K):
            dx = dx + cw_ref[SSM_K - 1 - j:SSM_K - j, :] * _shift_up(dcv, nxt, j)
        grow_x = c * CHUNK + lax.broadcasted_iota(jnp.int32, (CHUNK, D_XBC), 0)
        dx_ref[...] = jnp.where(grow_x >= PAD_ROWS, dx, 0.0)
        nx_scr[...] = dcv[:8]

    specs = _ssd_specs(nc, True)
    in_specs = [pl.BlockSpec((CHUNK, D_SSM), lambda i: (nc - 1 - i, 0))] + specs[:5] + [
        pl.BlockSpec((1, SSM_G, SSM_N, GSZ), lambda i: (nc - 1 - i, 0, 0, 0))] + specs[5:]
    rv = lambda w: pl.BlockSpec((CHUNK, w), lambda i: (nc - 1 - i, 0))
    return pl.pallas_call(
        body, name="ssd_bwd", grid=(nc,), in_specs=in_specs,
        out_specs=[rv(D_SSM), rv(D_XBC), rv(128), _full((8, D_XBC)), _full((1, D_XBC)), _full((1, 128)), _full((1, 128)),
                   _full((1, 128)), _full((1, D_SSM))],
        out_shape=[_sds((lp, D_SSM), F32), _sds((lp, D_XBC), F32), _sds((lp, 128), F32), _sds((8, D_XBC), F32),
                   _sds((1, D_XBC), F32), _sds((1, 128), F32), _sds((1, 128), F32), _sds((1, 128), F32), _sds((1, D_SSM), F32)],
        scratch_shapes=[pltpu.VMEM((SSM_G, SSM_N, GSZ), F32), pltpu.VMEM((8, D_XBC), F32)],
        compiler_params=_cp("arbitrary"))(dssm, xbc, xbc, z, dtr, dtrT, st, cw, cb, dtb, dtbT, alog, alogT, d_e, ng)


def _mixout_fwd(o, ssm, h0, g_ao, g_post, w):
    lp = o.shape[0]
    tr = _rt(lp, MM_ROWS)

    def body(o_ref, s_ref, h_ref, ga_ref, gp_ref, w_ref, mi_ref, mix_ref, h1_ref):
        mixin = jnp.concatenate([_rms(o_ref[...], ga_ref[...]), s_ref[...]], axis=1).astype(BF16)
        mi_ref[...] = mixin
        mix = _dot(mixin, w_ref[...])
        mix_ref[...] = mix
        grow = pl.program_id(0) * tr + lax.broadcasted_iota(jnp.int32, (tr, D), 0)
        h1_ref[...] = h_ref[...] + jnp.where(grow >= PAD_ROWS, _rms(mix, gp_ref[...]), 0.0)

    return pl.pallas_call(
        body, name="mixout_fwd", grid=(lp // tr,),
        in_specs=[_rows(tr, D), _rows(tr, D), _rows(tr, D), _full((1, D)), _full((1, D)), _full(w.shape)],
        out_specs=[_rows(tr, 2 * D), _rows(tr, D), _rows(tr, D)],
        out_shape=[_sds((lp, 2 * D), BF16), _sds((lp, D), F32), _sds((lp, D), F32)],
        compiler_params=_cp("parallel"))(o, ssm, h0, g_ao, g_post, w)


def _ffn_up(h1, g, w):
    lp = h1.shape[0]
    tr = _rt(lp, MM_ROWS)
    tn = D_FF // 2

    def body(h_ref, g_ref, w_ref, hn_ref, u_ref):
        @pl.when(pl.program_id(1) == 0)
        def _():
            hn_ref[...] = _rms(h_ref[...], g_ref[...]).astype(BF16)

        u_ref[...] = _dot(hn_ref[...], w_ref[...]).astype(BF16)

    return pl.pallas_call(
        body, name="ffn_up", grid=(lp // tr, 2 * D_FF // tn),
        in_specs=[pl.BlockSpec((tr, D), lambda i, j: (i, 0)), _full((1, D)), pl.BlockSpec((None, D, tn), lambda i, j: (j, 0, 0))],
        out_specs=[pl.BlockSpec((tr, D), lambda i, j: (i, 0)), pl.BlockSpec((tr, tn), lambda i, j: (i, j))],
        out_shape=[_sds((lp, D), BF16), _sds((lp, 2 * D_FF), BF16)],
        compiler_params=_cp("parallel", "arbitrary"))(h1, g, w)


def _ffn_in_bwd(du, w4, h1, g, dh2):
    lp = du.shape[0]
    nch, _, tn = w4.shape
    tr = _rt(lp, (320, 128))

    def body(du_ref, w_ref, h_ref, g_ref, r_ref, o_ref, dg_ref):
        acc = _dot_nt(du_ref[:, 0:tn], w_ref[0])
        for j in range(1, nch):
            acc = acc + _dot_nt(du_ref[:, j * tn:(j + 1) * tn], w_ref[j])
        dx, dg = _rms_bwd(h_ref[...], g_ref[...], acc)
        o_ref[...] = dx + r_ref[...]
        _acc_rows(dg_ref, dg, pl.program_id(0) == 0)

    return pl.pallas_call(
        body, name="ffn_in_bwd", grid=(lp // tr,),
        in_specs=[_rows(tr, nch * tn), _full(w4.shape), _rows(tr, D), _full((1, D)), _rows(tr, D)],
        out_specs=[_rows(tr, D), _full((1, D))], out_shape=[_sds((lp, D), F32), _sds((1, D), F32)],
        compiler_params=_cp("arbitrary"))(du, w4, h1, g, dh2)


FFN_CB = 256


def _ffn_gate(u, cw, cb):
    lp = u.shape[0]
    tr = _rt(lp, (320, 128))

    def body(u_ref, uh_ref, cw_ref, cb_ref, uc_ref, a_ref):
        i = pl.program_id(0)
        for j in range(D_FF // FFN_CB):
            halves = []
            for off in (0, D_FF):
                sl = slice(off + j * FFN_CB, off + (j + 1) * FFN_CB)
                x = u_ref[:, sl].astype(F32)
                halo = jnp.where(i > 0, uh_ref[8:16, sl].astype(F32), 0.0)
                cv = cb_ref[:, sl] + cw_ref[FFN_K - 1:FFN_K, sl] * x
                for s in range(1, FFN_K):
                    cv = cv + cw_ref[FFN_K - 1 - s:FFN_K - s, sl] * _shift_down(x, halo, s)
                uc_ref[:, sl] = cv.astype(BF16)
                halves.append(cv)
            a_ref[:, j * FFN_CB:(j + 1) * FFN_CB] = (_silu(halves[0]) * halves[1]).astype(BF16)

    return pl.pallas_call(
        body, name="ffn_gate", grid=(lp // tr,),
        in_specs=[_rows(tr, 2 * D_FF), pl.BlockSpec((16, 2 * D_FF), lambda i: (jnp.maximum(i * (tr // 16) - 1, 0), 0)),
                  _full((8, 2 * D_FF)), _full((1, 2 * D_FF))],
        out_specs=[_rows(tr, 2 * D_FF), _rows(tr, D_FF)], out_shape=[_sds((lp, 2 * D_FF), BF16), _sds((lp, D_FF), BF16)],
        compiler_params=_cp("parallel"))(u, u, cw, cb)


def _ffn_down(a, w, h1, tgt, g_post):
    lp = a.shape[0]
    tr = _rt(lp, MM_ROWS)
    nb = tr // FRONT

    def body(a_ref, w_ref, h_ref, *rest):
        t_refs, (g_ref, dh2_ref, dd_ref, dg_ref, loss_ref) = rest[:nb], rest[nb:]
        i = pl.program_id(0)
        d = _dot(a_ref[...], w_ref[...])
        gv = g_ref[...]
        h2 = h_ref[...] + _rms(d, gv)
        grow = i * tr + lax.broadcasted_iota(jnp.int32, (tr, D), 0)
        tgt_v = jnp.concatenate([r[...] for r in t_refs], axis=0)
        err = jnp.where(grow >= FRONT, h2 - tgt_v, 0.0)
        dh2 = err * (1.0 / D)
        dh2_ref[...] = dh2
        dd, dg = _rms_bwd(d, gv, dh2)
        dd_ref[...] = dd.astype(BF16)
        _acc_rows(dg_ref, dg, i == 0)
        part = 0.5 * jnp.sum(jnp.sum(err * err, axis=1, keepdims=True), axis=0, keepdims=True) * (1.0 / D)
        _acc_rows(loss_ref, jnp.broadcast_to(part, (8, 128)), i == 0)

    return pl.pallas_call(
        body, name="ffn_down", grid=(lp // tr,),
        in_specs=[_rows(tr, D_FF), _full(w.shape), _rows(tr, D)]
        + [pl.BlockSpec((FRONT, D), functools.partial(lambda i, b: (jnp.maximum(i * nb - 1 + b, 0), 0), b=b)) for b in range(nb)]
        + [_full((1, D))],
        out_specs=[_rows(tr, D), _rows(tr, D), _full((1, D)), _full((8, 128))],
        out_shape=[_sds((lp, D), F32), _sds((lp, D), BF16), _sds((1, D), F32), _sds((8, 128), F32)],
        compiler_params=_cp("arbitrary"))(a, w, h1, *([tgt] * nb), g_post)


def _ffn_gate_bwd(u, uc, dd, w_down, cw):
    lp = u.shape[0]
    tr = _rt(lp, (320, 128))
    n = lp // tr

    def body(u_ref, uc_ref, dd_ref, wd_ref, cw_ref, du_ref, dcw_ref, dcb_ref, nx_scr):
        i = pl.program_id(0)
        t = n - 1 - i
        first = i == 0

        @pl.when(first)
        def _():
            nx_scr[...] = jnp.zeros_like(nx_scr)

        grow = t * tr + lax.broadcasted_iota(jnp.int32, (tr, FFN_CB), 0)
        ddv = dd_ref[...]
        for j in range(D_FF // FFN_CB):
            sls = [slice(off + j * FFN_CB, off + (j + 1) * FFN_CB) for off in (0, D_FF)]
            cvg, cvv = uc_ref[:, sls[0]].astype(F32), uc_ref[:, sls[1]].astype(F32)
            dav = _dot_nt(ddv, wd_ref[j * FFN_CB:(j + 1) * FFN_CB, :])
            dcv = (dav * cvv * _dsilu(cvg), dav * _silu(cvg))
            for hf in range(2):
                sl = sls[hf]
                g = dcv[hf]
                nxt = nx_scr[:, sl]
                ahead = [g] + [_shift_up(g, nxt, s) for s in range(1, FFN_K)]
                x = u_ref[:, sl].astype(F32)
                rows = [jnp.sum(x * ahead[FFN_K - 1 - kk], axis=0, keepdims=True) for kk in range(FFN_K)]
                rows.append(jnp.zeros((8 - FFN_K, FFN_CB), F32))
                upd_w = jnp.concatenate(rows, axis=0)
                upd_b = jnp.sum(g, axis=0, keepdims=True)

                @pl.when(first)
                def _():
                    dcw_ref[:, sl] = upd_w
                    dcb_ref[:, sl] = upd_b

                @pl.when(jnp.logical_not(first))
                def _():
                    dcw_ref[:, sl] += upd_w
                    dcb_ref[:, sl] += upd_b

                du = cw_ref[FFN_K - 1:FFN_K, sl] * g
                for s in range(1, FFN_K):
                    du = du + cw_ref[FFN_K - 1 - s:FFN_K - s, sl] * ahead[s]
                du_ref[:, sl] = jnp.where(grow >= PAD_ROWS, du, 0.0).astype(BF16)
                nx_scr[:, sl] = g[:8]

    wide = pl.BlockSpec((tr, 2 * D_FF), lambda i: (n - 1 - i, 0))
    return pl.pallas_call(
        body, name="ffn_gate_bwd", grid=(n,),
        in_specs=[wide, wide, pl.BlockSpec((tr, D), lambda i: (n - 1 - i, 0)), _full(w_down.shape), _full((8, 2 * D_FF))],
        out_specs=[wide, _full((8, 2 * D_FF)), _full((1, 2 * D_FF))],
        out_shape=[_sds((lp, 2 * D_FF), BF16), _sds((8, 2 * D_FF), F32), _sds((1, 2 * D_FF), F32)],
        scratch_shapes=[pltpu.VMEM((8, 2 * D_FF), F32)],
        compiler_params=_cp("arbitrary"))(u, uc, dd, w_down, cw)


def _mla_bwd(dq, dk, dv, lat, qg, kvg, wq, wkv, cos, sa, sb):
    lp = lat.shape[0]
    tr = _rt(lp, (320, 128))

    def body(dq_ref, dk_ref, dv_ref, lat_ref, qg_ref, kvg_ref, wq_ref, wkv_ref, cos_ref, sa_ref, sb_ref,
             dqf_ref, dkvf_ref, dlat_ref, dqg_ref, dkvg_ref):
        i = pl.program_id(0)
        cos_v, sa_v, sb_v = cos_ref[...], sa_ref[...], sb_ref[...]
        dkpe = jnp.zeros((tr, 128), F32)
        for h in range(MLA_H):
            dqh = dq_ref[h] * SOFTMAX_SCALE
            dqf_ref[:, h * DN:(h + 1) * DN] = dqh[:, :DN].astype(BF16)
            dqf_ref[:, D + h * 128:D + (h + 1) * 128] = _rope_t(dqh[:, DN:], cos_v, sa_v, sb_v).astype(BF16)
            dkh = dk_ref[h]
            dkvf_ref[:, h * DN:(h + 1) * DN] = dkh[:, :DN].astype(BF16)
            dkpe = dkpe + dkh[:, DN:]
            dkvf_ref[:, D + h * DV:D + (h + 1) * DV] = dv_ref[h].astype(BF16)
        dql = _dot_nt(dqf_ref[...], wq_ref[...])
        dkl = _dot_nt(dkvf_ref[...], wkv_ref[...])
        lat_v = lat_ref[...]
        dqc, dqg = _rms_bwd(lat_v[:, :QR], qg_ref[...], dql)
        dkc, dkg = _rms_bwd(lat_v[:, QR:QR + KVR], kvg_ref[...], dkl)
        dlat_ref[:, :QR] = dqc
        dlat_ref[:, QR:QR + KVR] = dkc
        dlat_ref[:, QR + KVR:] = _rope_t(dkpe, cos_v, sa_v, sb_v)
        _acc_rows(dqg_ref, dqg, i == 0)
        _acc_rows(dkvg_ref, dkg, i == 0)

    hb = lambda w: pl.BlockSpec((MLA_H, tr, w), lambda i: (0, i, 0))
    return pl.pallas_call(
        body, name="mla_bwd", grid=(lp // tr,),
        in_specs=[hb(256), hb(256), hb(128), _rows(tr, LAT_W), _full((1, QR)), _full((1, KVR)), _full(wq.shape),
                  _full(wkv.shape), _rows(tr, 128), _rows(tr, 128), _rows(tr, 128)],
        out_specs=[_rows(tr, 2 * D), _rows(tr, 2 * D), _rows(tr, LAT_W), _full((1, QR)), _full((1, KVR))],
        out_shape=[_sds((lp, 2 * D), BF16), _sds((lp, 2 * D), BF16), _sds((lp, LAT_W), F32), _sds((1, QR), F32),
                   _sds((1, KVR), F32)],
        compiler_params=_cp("arbitrary"))(dq, dk, dv, lat, qg, kvg, wq, wkv, cos, sa, sb)


def _inproj_bwd(dlat, dz, dxbc, ddt, w, h0, g, dh1, ride=()):
    lp = h0.shape[0]
    tr = _rt(lp, (320, 128))
    segs = ((0, LAT_W), (LAT_W, LAT_W + D_SSM), (LAT_W + D_SSM, LAT_W + D_SSM + D_XBC), (IN_P - 128, IN_P))
    nr = len(ride)
    steps = lp // tr

    def body(dl_ref, dz_ref, dx_ref, dt_ref, w_ref, h_ref, g_ref, r_ref, *rest):
        ps = rest[:nr]
        o_ref, dg_ref = rest[nr:nr + 2]
        got, sems = rest[nr + 2:2 * nr + 2], rest[2 * nr + 2:]
        step = pl.program_id(0)
        if nr:
            pl.when(step == 0)(lambda: _ride_exchange(ps, got, *sems, 0))
        dhn = jnp.zeros((tr, D), F32)
        for ref, (a, b) in zip((dl_ref, dz_ref, dx_ref, dt_ref), segs):
            dhn = dhn + _dot_nt(ref[...].astype(BF16), w_ref[:, a:b])
        dx, dg = _rms_bwd(h_ref[...], g_ref[...], dhn)
        o_ref[...] = dx + r_ref[...]
        _acc_rows(dg_ref, dg, step == 0)
        if nr:
            pl.when(step == steps - 1)(lambda: _ride_exchange(ps, got, *sems, 1))

    sems = [pltpu.SemaphoreType.DMA((3 * nr,)), pltpu.SemaphoreType.DMA((3 * nr,))] if nr else []
    outs = pl.pallas_call(
        body, name="inproj_bwd", grid=(steps,),
        in_specs=[_rows(tr, LAT_W), _rows(tr, D_SSM), _rows(tr, D_XBC), _rows(tr, 128), _full(w.shape), _rows(tr, D),
                  _full((1, D)), _rows(tr, D)] + [ANY] * nr,
        out_specs=[_rows(tr, D), _full((1, D))] + [ANY] * nr,
        out_shape=[_sds((lp, D), F32), _sds((1, D), F32)] + [_sds((3,) + p.shape[1:], p.dtype) for p in ride],
        scratch_shapes=sems, compiler_params=_cp("arbitrary"))(dlat, dz, dxbc, ddt, w, h0, g, dh1, *ride)
    return outs[0], outs[1], list(outs[2:])


def _rope_tables(lp):
    pos = (jnp.arange(lp, dtype=jnp.int32) - PAD_ROWS).astype(F32)
    inv = ROPE_THETA ** (-jnp.arange(0, DR, 2, dtype=F32) / DR)
    ang = pos[:, None] * inv[None, :]
    cos, sin = jnp.cos(ang), jnp.sin(ang)
    zero = jnp.zeros_like(sin)
    cos128 = jnp.concatenate([cos, cos, cos, cos], axis=1)
    sa128 = jnp.concatenate([-sin, zero, -sin, zero], axis=1)
    sb128 = jnp.concatenate([zero, sin, zero, sin], axis=1)
    return cos128, sa128, sb128


def _pad_rows8(w):
    return jnp.concatenate([w, jnp.zeros((8 - w.shape[0], w.shape[1]), w.dtype)], axis=0)


def _lane_pad(v):
    return jnp.concatenate([v, jnp.zeros((v.shape[0], 128 - v.shape[1]), v.dtype)], axis=1)


def _late_weights(bufs):
    w_out, w_up, w_down = bufs
    return dict(w_out=w_out.reshape(2 * D, D), w_up=w_up.reshape(N_CHIPS, D, 2 * D_FF // N_CHIPS), w_down=w_down.reshape(D_FF, D))


def _device_step(x, tgt, meta, p, late_bufs=(), early_reduce=None, last_reduce=None):
    s = x.shape[0]
    lp = s + FRONT
    zpad = jnp.zeros((PAD_ROWS, D), F32)
    h0 = jnp.concatenate([zpad, meta, x], axis=0)
    cos, sa, sb = _rope_tables(lp)

    w_in = p["w_in"]
    w_in_p = jnp.concatenate([w_in[:, :QR + KVR + DR], jnp.zeros((D, 64), BF16), w_in[:, QR + KVR + DR:],
                              jnp.zeros((D, 128 - SSM_H), BF16)], axis=1)
    w_uq = p["w_uq"]
    wq_p = jnp.concatenate([w_uq[:, :, :DN].reshape(QR, MLA_H * DN),
                            jnp.concatenate([w_uq[:, :, DN:], jnp.zeros((QR, MLA_H, 128 - DR), BF16)], axis=2).reshape(QR, MLA_H * 128)],
                           axis=1)
    w_ukv = p["w_ukv"]
    wkv_p = jnp.concatenate([w_ukv[:, :, :DN].reshape(KVR, MLA_H * DN), w_ukv[:, :, DN:].reshape(KVR, MLA_H * DV)], axis=1)
    scw = _pad_rows8(p["ssm_conv_w"])
    fcw = _pad_rows8(p["ffn_conv_w"])
    dtb, alog = _lane_pad(p["ssm_dt_bias"]), _lane_pad(p["ssm_A_log"])
    dtbT, alogT = p["ssm_dt_bias"].reshape(SSM_H, 1), p["ssm_A_log"].reshape(SSM_H, 1)
    d_e = jnp.repeat(p["ssm_D"], SSM_P, axis=1)

    hn, lat, z, xbc, dtr = _inproj(h0, p["norm_mix_pre"], w_in_p)
    dtrT = dtr[:, :SSM_H].T
    q, k, v, qlat, kvlat = _mla_prep(lat, p["q_a_norm"], p["kv_a_norm"], wq_p, wkv_p, cos, sa, sb)
    o, lse, gathered = _attn_fwd(q, k, v, ride=late_bufs)
    if late_bufs:
        p = dict(p, **_late_weights(gathered))
    ssm, st = _ssd_fwd(xbc, z, dtr, dtrT, scw, p["ssm_conv_b"], dtb, dtbT, alog, alogT, d_e, p["ssm_norm"])
    mixin, mix, h1 = _mixout_fwd(o, ssm, h0, p["attn_out_norm"], p["norm_mix_post"], p["w_out"])
    hn2, u = _ffn_up(h1, p["norm_ffn_pre"], p["w_up"])
    uc, a = _ffn_gate(u, fcw, p["ffn_conv_b"])
    dh2, dd, g_ffn_post, loss = _ffn_down(a, p["w_down"], h1, tgt, p["norm_ffn_post"])

    g_w_down = _mm_tn(a, dd, "ffn_dw_down", tn=512)
    du, g_fcw, g_fcb = _ffn_gate_bwd(u, uc, dd, p["w_down"], fcw)
    dh1, g_ffn_pre = _ffn_in_bwd(du, p["w_up"], h1, p["norm_ffn_pre"], dh2)
    g_w_up = _mm_tn(hn2, du, "ffn_dw_up", tn=D_FF // 2, chunked=True)
    ffn_gs = early_reduce.halves(dict(w_up=g_w_up, w_down=g_w_down)) if early_reduce else ()
    dmix, dssm, do, g_mix_post, g_ao, delta, ffn_sib = _mixout_bwd(mix, p["norm_mix_post"], dh1, o, p["attn_out_norm"],
                                                                  p["w_out"], ride=ffn_gs)
    g_w_out = _mm_tn(mixin, dmix, "mix_dw_out", tn=512)
    t = _rt(lp, (640, 128))
    pairs = early_reduce.pairs(dict(w_out=g_w_out), ffn_gs, ffn_sib) if early_reduce else ()
    dq, dk, dv, got = _attn_bwd(q, k, v, do, lse[:, 0, :].reshape(MLA_H, lp // t, 1, t), delta.reshape(MLA_H, lp // t, 1, t),
                                ride=pairs)
    dqf, dkvf, dlat, g_qa, g_kva = _mla_bwd(dq, dk, dv, lat, p["q_a_norm"], p["kv_a_norm"], wq_p, wkv_p, cos, sa, sb)
    g_wq_p = _mm_tn(qlat, dqf, "mla_dw_uq")
    g_wkv_p = _mm_tn(kvlat, dkvf, "mla_dw_ukv")
    dz, dxbc, ddtr, g_scw, g_scb, g_dtb, g_alog, g_dd, g_ssm_norm = _ssd_bwd(
        dssm, xbc, z, dtr, dtrT, st, scw, p["ssm_conv_b"], dtb, dtbT, alog, alogT, d_e, p["ssm_norm"])
    g_in_p = jnp.concatenate([_mm_tn(hn, dlat, "in_dw_lat"), _mm_tn(hn, dz, "in_dw_z"), _mm_tn(hn, dxbc, "in_dw_xbc"),
                              _mm_tn(hn, ddtr, "in_dw_dt")], axis=1)
    g_w_in = jnp.concatenate([g_in_p[:, :QR + KVR + DR], g_in_p[:, LAT_W:LAT_W + D_SSM + D_XBC + SSM_H]], axis=1)
    g_w_uq = jnp.concatenate([g_wq_p[:, :D].reshape(QR, MLA_H, DN), g_wq_p[:, D:].reshape(QR, MLA_H, 128)[:, :, :DR]], axis=2)
    g_w_ukv = jnp.concatenate([g_wkv_p[:, :D].reshape(KVR, MLA_H, DN), g_wkv_p[:, D:].reshape(KVR, MLA_H, DV)], axis=2)
    pairs2 = last_reduce(dict(w_in=g_w_in, w_uq=g_w_uq, w_ukv=g_w_ukv)) if last_reduce else ()
    dh0, g_mix_pre, got2 = _inproj_bwd(dlat, dz, dxbc, ddtr, w_in_p, h0, p["norm_mix_pre"], dh1, ride=pairs2)
    grads = dict(
        norm_mix_pre=g_mix_pre, norm_mix_post=g_mix_post, norm_ffn_pre=g_ffn_pre, norm_ffn_post=g_ffn_post, w_in=g_w_in,
        q_a_norm=g_qa, w_uq=g_w_uq, kv_a_norm=g_kva, w_ukv=g_w_ukv, attn_out_norm=g_ao, ssm_conv_w=g_scw[:SSM_K],
        ssm_conv_b=g_scb, ssm_dt_bias=g_dtb[:, :SSM_H], ssm_A_log=g_alog[:, :SSM_H], ssm_D=g_dd[:, :SSM_H],
        ssm_norm=g_ssm_norm, w_out=g_w_out, w_up=g_w_up, ffn_conv_w=g_fcw[:FFN_K], ffn_conv_b=g_fcb, w_down=g_w_down)
    return loss, dh0[FRONT:], dh0[PAD_ROWS:FRONT], grads, (list(pairs2) + list(pairs), list(got2) + list(got))


N_CHIPS = 4
BIG = (("w_in", (D, D_IN // N_CHIPS)), ("w_uq", (QR // N_CHIPS, MLA_H, DN + DR)), ("w_ukv", (KVR // N_CHIPS, MLA_H, DN + DV)),
       ("w_out", (2 * D // N_CHIPS, D)), ("w_up", (D, 2 * D_FF // N_CHIPS)), ("w_down", (D_FF // N_CHIPS, D)))
SMALL_SHARDED = (("meta_tokens", (N_META, D // N_CHIPS)), ("ssm_conv_w", (SSM_K, D_XBC // N_CHIPS)),
                 ("ffn_conv_w", (FFN_K, 2 * D_FF // N_CHIPS)))
SMALL_REPL = (("norm_mix_pre", D), ("norm_mix_post", D), ("norm_ffn_pre", D), ("norm_ffn_post", D), ("q_a_norm", QR),
              ("kv_a_norm", KVR), ("attn_out_norm", D), ("ssm_conv_b", D_XBC), ("ssm_dt_bias", SSM_H), ("ssm_A_log", SSM_H),
              ("ssm_D", SSM_H), ("ssm_norm", D_SSM), ("ffn_conv_b", 2 * D_FF))
ANY = pl.BlockSpec(memory_space=pl.ANY)


def _pad128(v):
    n = v.shape[0]
    return jnp.concatenate([v, jnp.zeros(((-n) % 128,), v.dtype)]) if n % 128 else v


def _pack_rows(vs, rows):
    flat = jnp.concatenate([_pad128(v.reshape(-1)) for v in vs])
    flat = jnp.concatenate([flat, jnp.zeros((rows * 128 - flat.shape[0],), flat.dtype)])
    return flat.reshape(rows, 128)


def _unpack_rows(pack, sizes):
    flat = pack.reshape(-1)
    out, off = [], 0
    for n in sizes:
        out.append(flat[off:off + n])
        off += n + (-n) % 128
    return out


def _my_place():
    return lax.axis_index("x"), lax.axis_index("y"), lax.axis_index("c")


def _other_chips(x, y):
    return [(1 - x, y), (x, 1 - y), (1 - x, 1 - y)]


def _remote(src, dst, send, recv, dev):
    return pltpu.make_async_remote_copy(src_ref=src, dst_ref=dst, send_sem=send, recv_sem=recv, device_id=dev,
                                        device_id_type=MESH)


SMALL_AG_ROWS = 80


def _gather_weights(shards, small, name):
    arrs = list(shards) + ([] if small is None else [small])
    n, nb = len(arrs), len(shards)

    def body(*refs):
        ins, outs = refs[:n], refs[n:2 * n]
        send, recv, lsem = refs[2 * n:]
        x, y, c = _my_place()
        me = 2 * x + y
        chips = _other_chips(x, y)
        slot = lambda w, chip, cc: outs[w].at[chip, cc] if w < nb else outs[w].at[chip]
        mine = lambda w: slot(w, me, c) if w < nb else ins[w]
        loc = [pltpu.make_async_copy(ins[w], outs[w].at[me], lsem.at[w - nb]) for w in range(nb, n)]
        for cp in loc:
            cp.start()
        sends = []
        for w in range(n):
            for kk, (cx, cy) in enumerate(chips):
                sends.append(_remote(mine(w), slot(w, me, c), send.at[3 * w + kk], recv.at[3 * w + kk], (cx, cy, c)))
        for cp in sends:
            cp.start()
        for w in range(nb):
            for kk, (cx, cy) in enumerate(chips):
                src = 2 * cx + cy
                _remote(mine(w), slot(w, src, c), send.at[3 * w + kk], recv.at[3 * w + kk], (cx, cy, c)).wait_recv()
                fwd = _remote(slot(w, src, c), slot(w, src, c), send.at[3 * (n + w) + kk], recv.at[3 * (n + w) + kk], (x, y, 1 - c))
                fwd.start()
                sends.append(fwd)
        for w in range(n):
            for kk, (cx, cy) in enumerate(chips):
                src = 2 * cx + cy
                if w < nb:
                    _remote(mine(w), slot(w, src, 1 - c), send.at[3 * (n + w) + kk], recv.at[3 * (n + w) + kk],
                            (x, y, 1 - c)).wait_recv()
                else:
                    _remote(ins[w], slot(w, src, c), send.at[3 * w + kk], recv.at[3 * w + kk], (cx, cy, c)).wait_recv()
        for cp in sends:
            cp.wait_send()
        for cp in loc:
            cp.wait()

    return pl.pallas_call(
        body, name=name, in_specs=[ANY] * n, out_specs=[ANY] * n,
        out_shape=[_sds(a.shape, a.dtype) for a in shards] + ([] if small is None else [_sds((N_CHIPS,) + small.shape, small.dtype)]),
        input_output_aliases={w: w for w in range(nb)},
        scratch_shapes=[pltpu.SemaphoreType.DMA((3 * (n + nb),)), pltpu.SemaphoreType.DMA((3 * (n + nb),)),
                        pltpu.SemaphoreType.DMA((max(n - nb, 1),))])(*arrs)


def _place_own(wt, chip, name):
    r, c = wt.shape
    tr = _row_tile(r, c)

    def body(c_ref, w_ref, o_ref):
        o_ref[...] = w_ref[...].astype(BF16)

    return pl.pallas_call(
        body, name=name, out_shape=_sds((N_CHIPS, r, c), BF16),
        grid_spec=pltpu.PrefetchScalarGridSpec(
            num_scalar_prefetch=1, grid=(r // tr,), in_specs=[pl.BlockSpec((tr, c), lambda i, cr: (i, 0))],
            out_specs=pl.BlockSpec((None, tr, c), lambda i, cr: (cr[0], i, 0))),
        compiler_params=_cp("parallel"))(chip, wt)


def _send_sibling_halves(gs, name):
    n = len(gs)

    def body(*refs):
        ins, outs, send, recv = refs[:n], refs[n:2 * n], refs[2 * n], refs[2 * n + 1]
        x, y, c = _my_place()
        cps = [_remote(ins[w].at[:, 1 - c], outs[w], send.at[w], recv.at[w], (x, y, 1 - c)) for w in range(n)]
        for cp in cps:
            cp.start()
        for cp in cps:
            cp.wait()

    return pl.pallas_call(
        body, name=name, in_specs=[ANY] * n, out_specs=[ANY] * n,
        out_shape=[_sds((g.shape[0],) + g.shape[2:], g.dtype) for g in gs],
        scratch_shapes=[pltpu.SemaphoreType.DMA((n,)), pltpu.SemaphoreType.DMA((n,))])(*gs)


def _ride_gather(bufs, send, recv, phase):
    n = len(bufs)
    x, y, c = _my_place()
    me = 2 * x + y
    for w in range(n):
        for kk, (cx, cy) in enumerate(_other_chips(x, y)):
            src = 2 * cx + cy
            out = lambda: _remote(bufs[w].at[me, c], bufs[w].at[me, c], send.at[3 * w + kk], recv.at[3 * w + kk], (cx, cy, c))
            fwd = lambda: _remote(bufs[w].at[src, c], bufs[w].at[src, c], send.at[3 * (n + w) + kk],
                                  recv.at[3 * (n + w) + kk], (x, y, 1 - c))
            if phase == 0:
                out().start()
            elif phase == 1:
                _remote(bufs[w].at[me, c], bufs[w].at[src, c], send.at[3 * w + kk], recv.at[3 * w + kk], (cx, cy, c)).wait_recv()
                fwd().start()
            else:
                _remote(bufs[w].at[me, c], bufs[w].at[src, 1 - c], send.at[3 * (n + w) + kk], recv.at[3 * (n + w) + kk],
                        (x, y, 1 - c)).wait_recv()
                out().wait_send()
                fwd().wait_send()


def _ride_sibling(gs, outs, send, recv, phase):
    x, y, c = _my_place()
    for w in range(len(gs)):
        cp = _remote(gs[w].at[:, 1 - c], outs[w], send.at[w], recv.at[w], (x, y, 1 - c))
        if phase == 0:
            cp.start()
        else:
            cp.wait()


def _ride_exchange(ps, outs, send, recv, phase):
    x, y, c = _my_place()
    for w in range(len(ps)):
        for kk, (cx, cy) in enumerate(_other_chips(x, y)):
            cp = _remote(ps[w].at[2 * cx + cy], outs[w].at[kk], send.at[3 * w + kk], recv.at[3 * w + kk], (cx, cy, c))
            if phase == 0:
                cp.start()
            else:
                cp.wait()


def _share_sibling(halves):
    n = len(halves)

    def body(*refs):
        outs, send, recv = refs[n:2 * n], refs[2 * n], refs[2 * n + 1]
        x, y, c = _my_place()
        cps = [_remote(outs[w].at[c], outs[w].at[c], send.at[w], recv.at[w], (x, y, 1 - c)) for w in range(n)]
        for cp in cps:
            cp.start()
        for w in range(n):
            cps[w].wait_send()
            _remote(outs[w].at[c], outs[w].at[1 - c], send.at[w], recv.at[w], (x, y, 1 - c)).wait_recv()

    return pl.pallas_call(
        body, name="share_sibling", in_specs=[ANY] * n, out_specs=[ANY] * n,
        out_shape=[_sds(h.shape, h.dtype) for h in halves], input_output_aliases={w: w for w in range(n)},
        scratch_shapes=[pltpu.SemaphoreType.DMA((n,)), pltpu.SemaphoreType.DMA((n,))])(*halves)


def _row_tile(r, c, cap=1 << 20):
    return next(t for t in range(r, 0, -1) if r % t == 0 and (t % 8 == 0 or t == r) and t * c * 4 <= cap)


def _add_pair(g, t, core, name):
    _, _, r, c = g.shape
    tr = _row_tile(r, c)

    def body(c_ref, g_ref, t_ref, o_ref):
        o_ref[...] = (g_ref[...] + t_ref[...]).astype(BF16)

    return pl.pallas_call(
        body, name=name, out_shape=_sds(t.shape, BF16),
        grid_spec=pltpu.PrefetchScalarGridSpec(
            num_scalar_prefetch=1, grid=(N_CHIPS, r // tr),
            in_specs=[pl.BlockSpec((None, None, tr, c), lambda j, i, cr: (j, cr[0], i, 0)),
                      pl.BlockSpec((None, tr, c), lambda j, i, cr: (j, i, 0))],
            out_specs=pl.BlockSpec((None, tr, c), lambda j, i, cr: (j, i, 0))),
        compiler_params=_cp("parallel", "parallel"))(core, g, t)


def _add_chips(p, got, chip, name):
    _, r, c = p.shape
    tr = _row_tile(r, c)

    def body(c_ref, p_ref, g_ref, o_ref):
        o_ref[...] = ((p_ref[...].astype(F32) + g_ref[0].astype(F32)) + g_ref[1].astype(F32)) + g_ref[2].astype(F32)

    return pl.pallas_call(
        body, name=name, out_shape=_sds((2, r, c), F32),
        grid_spec=pltpu.PrefetchScalarGridSpec(
            num_scalar_prefetch=1, grid=(r // tr,),
            in_specs=[pl.BlockSpec((None, tr, c), lambda i, cr: (cr[0], i, 0)), pl.BlockSpec((3, tr, c), lambda i, cr: (0, i, 0))],
            out_specs=pl.BlockSpec((None, tr, c), lambda i, cr: (cr[1], i, 0))),
        compiler_params=_cp("parallel"))(chip, p, got)


SMALL_AR_ROWS = 424


def _allreduce_small(v):
    def body(v_ref, o_ref, gath, send, recv):
        x, y, c = _my_place()
        me = 4 * x + 2 * y + c
        gath[me] = v_ref[...]
        cps = []
        for dd in range(1, 8):
            dx, dy, dc = dd >> 2, (dd >> 1) & 1, dd & 1
            peer = (1 - x if dx else x, 1 - y if dy else y, 1 - c if dc else c)
            cps.append(_remote(v_ref, gath.at[me], send.at[dd - 1], recv.at[dd - 1], peer))
        for cp in cps:
            cp.start()
        for cp in cps:
            cp.wait()
        acc = gath[0]
        for dev in range(1, 8):
            acc = acc + gath[dev]
        o_ref[...] = acc

    vm = pl.BlockSpec(memory_space=pltpu.VMEM)
    return pl.pallas_call(
        body, name="allreduce_small", in_specs=[vm], out_specs=vm, out_shape=_sds(v.shape, F32),
        scratch_shapes=[pltpu.VMEM((8,) + v.shape, F32), pltpu.SemaphoreType.DMA((7,)), pltpu.SemaphoreType.DMA((7,))])(v)


def _adamw(w, g, m, v, name):
    r, c = w.shape
    tr = _row_tile(r, c)

    def body(w_ref, g_ref, m_ref, v_ref, d_ref, m2_ref, v2_ref):
        gv = g_ref[...]
        m2 = ADAM_B1 * m_ref[...] + (1.0 - ADAM_B1) * gv
        v2 = ADAM_B2 * v_ref[...] + (1.0 - ADAM_B2) * jnp.square(gv)
        m_hat = m2 / (1.0 - ADAM_B1 ** ADAM_STEP)
        v_hat = v2 / (1.0 - ADAM_B2 ** ADAM_STEP)
        d_ref[...] = -ADAM_LR * (m_hat / (jnp.sqrt(v_hat) + ADAM_EPS) + ADAM_WD * w_ref[...])
        m2_ref[...] = m2
        v2_ref[...] = v2

    return pl.pallas_call(
        body, name=name, grid=(r // tr,), in_specs=[_rows(tr, c)] * 4, out_specs=[_rows(tr, c)] * 3,
        out_shape=[_sds((r, c), F32)] * 3, compiler_params=_cp("parallel"))(w, g, m, v)


WEIGHT_NAMES = ("meta_tokens", "norm_mix_pre", "norm_mix_post", "norm_ffn_pre", "norm_ffn_post", "w_in", "q_a_norm", "w_uq",
                "kv_a_norm", "w_ukv", "attn_out_norm", "ssm_conv_w", "ssm_conv_b", "ssm_dt_bias", "ssm_A_log", "ssm_D",
                "ssm_norm", "w_out", "w_up", "ffn_conv_w", "ffn_conv_b", "w_down")
SMALL_ADAM_ROWS = 192


def kernel(x, meta_tokens, norm_mix_pre, norm_mix_post, norm_ffn_pre, norm_ffn_post, w_in, q_a_norm, w_uq, kv_a_norm, w_ukv, attn_out_norm, ssm_conv_w, ssm_conv_b, ssm_dt_bias, ssm_A_log, ssm_D, ssm_norm, w_out, w_up, ffn_conv_w, ffn_conv_b, w_down, loss_target, m_meta_tokens, m_norm_mix_pre, m_norm_mix_post, m_norm_ffn_pre, m_norm_ffn_post, m_w_in, m_q_a_norm, m_w_uq, m_kv_a_norm, m_w_ukv, m_attn_out_norm, m_ssm_conv_w, m_ssm_conv_b, m_ssm_dt_bias, m_ssm_A_log, m_ssm_D, m_ssm_norm, m_w_out, m_w_up, m_ffn_conv_w, m_ffn_conv_b, m_w_down, v_meta_tokens, v_norm_mix_pre, v_norm_mix_post, v_norm_ffn_pre, v_norm_ffn_post, v_w_in, v_q_a_norm, v_w_uq, v_kv_a_norm, v_w_ukv, v_attn_out_norm, v_ssm_conv_w, v_ssm_conv_b, v_ssm_dt_bias, v_ssm_A_log, v_ssm_D, v_ssm_norm, v_w_out, v_w_up, v_ffn_conv_w, v_ffn_conv_b, v_w_down):
    args = locals()
    w = {n: args[n] for n in WEIGHT_NAMES}
    mom = {n: args["m_" + n] for n in WEIGHT_NAMES}
    var = {n: args["v_" + n] for n in WEIGHT_NAMES}
    cx, cy, cc = _my_place()
    chip = 2 * cx + cy

    two_d = {n: (shp[0], functools.reduce(lambda a, b: a * b, shp[1:])) for n, shp in BIG}
    names = [n for n, _ in BIG]
    core_i = cc.astype(jnp.int32).reshape(1)
    chip_i = chip.astype(jnp.int32).reshape(1)
    early, late = names[:3], names[3:]
    halves = lambda n, a: a.reshape(N_CHIPS, 2, two_d[n][0] // 2, two_d[n][1])
    bufs = {n: halves(n, _place_own(w[n].reshape(two_d[n]), chip_i, "place_" + n)) for n in names}
    small = _pack_rows([w[n] for n, _ in SMALL_SHARDED], SMALL_AG_ROWS)
    *gathered, small_all = _gather_weights([bufs[n] for n in early], small, "allgather_weights")
    gath = {n: a.reshape((N_CHIPS,) + two_d[n]) for n, a in zip(early, gathered)}
    p = dict(w_in=gath["w_in"].transpose(1, 0, 2).reshape(D, D_IN), w_uq=gath["w_uq"].reshape(QR, MLA_H, DN + DR),
             w_ukv=gath["w_ukv"].reshape(KVR, MLA_H, DN + DV))
    sm_parts = [_unpack_rows(small_all[j], [a * b for _, (a, b) in SMALL_SHARDED]) for j in range(N_CHIPS)]
    for i, (n, shp) in enumerate(SMALL_SHARDED):
        p[n] = jnp.concatenate([sm_parts[j][i].reshape(shp) for j in range(N_CHIPS)], axis=1)
    for n, _ in SMALL_REPL:
        p[n] = w[n]
    meta_full = p.pop("meta_tokens")

    place_i = jnp.stack([chip, cc]).astype(jnp.int32)

    def pair_sums(gd, group):
        gd = dict(gd)
        if "w_in" in gd:
            gd["w_in"] = gd["w_in"].reshape(D, N_CHIPS, D_IN // N_CHIPS).transpose(1, 0, 2)
        gs = [halves(n, gd[n]) for n in group]
        from_sib = _send_sibling_halves(gs, "reduce_sibling_" + group[0])
        return [_add_pair(gg, tt, core_i, "reduce_pair_" + n) for n, gg, tt in zip(group, gs, from_sib)]

    class LateReduce:
        @staticmethod
        def halves(gd):
            return [halves(n, gd[n]) for n in late[1:]]

        @staticmethod
        def pairs(gd, ffn_gs, ffn_sib):
            gs = [halves(late[0], gd[late[0]])]
            sib = _send_sibling_halves(gs, "reduce_sibling_" + late[0])
            return [_add_pair(gg, tt, core_i, "reduce_pair_" + n)
                    for n, gg, tt in zip(late, gs + list(ffn_gs), list(sib) + list(ffn_sib))]

    loss_part, gx, gmeta, g, (pairs, got) = _device_step(
        x[0], loss_target[0], meta_full, p, late_bufs=[bufs[n] for n in late], early_reduce=LateReduce,
        last_reduce=lambda gd: pair_sums(gd, early))

    small_names = [n for n, _ in SMALL_REPL] + ["ssm_conv_w", "ffn_conv_w"]
    small_sizes = [128] + [sz for _, sz in SMALL_REPL] + [N_META * D, SSM_K * D_XBC, FFN_K * 2 * D_FF]
    order = [n for n, _ in SMALL_REPL]
    sp = _pack_rows([loss_part[0]] + [g[n] for n in order] + [gmeta, g["ssm_conv_w"], g["ffn_conv_w"]], SMALL_AR_ROWS)
    red = _unpack_rows(_allreduce_small(sp), small_sizes)
    loss = red[0][0]
    gfull = {n: red[1 + i].reshape(1, -1) for i, n in enumerate(order)}
    n_r = len(order)
    gfull["meta_tokens"] = lax.dynamic_slice_in_dim(red[1 + n_r].reshape(N_META, D), chip * (D // N_CHIPS), D // N_CHIPS, axis=1)
    gfull["ssm_conv_w"] = lax.dynamic_slice_in_dim(red[2 + n_r].reshape(SSM_K, D_XBC), chip * (D_XBC // N_CHIPS),
                                                   D_XBC // N_CHIPS, axis=1)[None]
    gfull["ffn_conv_w"] = lax.dynamic_slice_in_dim(red[3 + n_r].reshape(FFN_K, 2 * D_FF), chip * (2 * D_FF // N_CHIPS),
                                                   2 * D_FF // N_CHIPS, axis=1)[None]

    mine = [_add_chips(pp, gg, place_i, "reduce_chips_" + n) for n, pp, gg in zip(names, pairs, got)]
    for n, both in zip(names, _share_sibling(mine)):
        gfull[n] = both.reshape(two_d[n])

    delta, new_m, new_v = {}, {}, {}
    for n, shp in BIG:
        outs = _adamw(w[n].reshape(two_d[n]), gfull[n], mom[n].reshape(two_d[n]), var[n].reshape(two_d[n]), "adamw_" + n)
        delta[n], new_m[n], new_v[n] = (o.reshape((1,) + shp) for o in outs)
    snames = order + ["meta_tokens", "ssm_conv_w", "ffn_conv_w"]
    ssizes = [functools.reduce(lambda a, b: a * b, w[n].shape) for n in snames]
    packs = [_pack_rows([d[n] for n in snames], SMALL_ADAM_ROWS) for d in (w, gfull, mom, var)]
    outs = _adamw(*packs, "adamw_small")
    for d, o in zip((delta, new_m, new_v), outs):
        for n, piece in zip(snames, _unpack_rows(o, ssizes)):
            d[n] = piece.reshape(w[n].shape)
    gout = {n: gfull[n].reshape(w[n].shape) for n in WEIGHT_NAMES}
    return (loss, gx[None], *[gout[n] for n in WEIGHT_NAMES], *[delta[n] for n in WEIGHT_NAMES],
            *[new_m[n] for n in WEIGHT_NAMES], *[new_v[n] for n in WEIGHT_NAMES])
```

```python
import functools

import jax
import jax.numpy as jnp
from jax import lax
from jax.experimental import pallas as pl
from jax.experimental.pallas import tpu as pltpu

F32 = jnp.float32
BF16 = jnp.bfloat16

D = 1024
N_META = 16
FRONT = 128
PAD_ROWS = FRONT - N_META
MLA_H = 8
DN, DR, DV = 128, 64, 128
QR, KVR = 384, 256
SOFTMAX_SCALE = (DN + DR) ** -0.5
ROPE_THETA = 10000.0
SSM_H, SSM_P, SSM_G, SSM_N, SSM_K = 16, 64, 2, 128, 4
CHUNK = 128
D_SSM = SSM_H * SSM_P
D_XBC = D_SSM + 2 * SSM_G * SSM_N
GSZ = D_SSM // SSM_G
D_FF = 2816
FFN_K = 3
EPS = 1e-6
IN_SPLITS = (QR, KVR, DR, D_SSM, D_XBC, SSM_H)
D_IN = sum(IN_SPLITS)
LAT_W = 768
IN_P = LAT_W + D_SSM + D_XBC + 128
NEG = -1e30
LOG2E = 1.4426950408889634
LN2 = 0.6931471805599453
Q_SCALE = SOFTMAX_SCALE * LOG2E

ADAM_LR, ADAM_B1, ADAM_B2, ADAM_EPS, ADAM_WD, ADAM_STEP = 0.001, 0.9, 0.999, 1e-08, 0.01, 10

VMEM_LIMIT = 56 * 1024 * 1024
MM_ROWS = (640, 320, 128)
MESH = pl.DeviceIdType.MESH


def _sds(shape, dtype):
    return jax.ShapeDtypeStruct(shape, dtype)


def _cp(*sem):
    return pltpu.CompilerParams(dimension_semantics=sem, vmem_limit_bytes=VMEM_LIMIT)


def _rt(n, cands):
    for c in cands:
        if n % c == 0:
            return c
    raise ValueError((n, cands))


def _full(shape):
    nd = len(shape)
    return pl.BlockSpec(shape, lambda *_: (0,) * nd)


def _rows(tr, c):
    return pl.BlockSpec((tr, c), lambda i: (i, 0))


def _sigmoid(x):
    return 1.0 / (1.0 + jnp.exp(-x))


def _silu(x):
    return x * _sigmoid(x)


def _dsilu(x):
    s = _sigmoid(x)
    return s * (1.0 + x * (1.0 - s))


def _softplus(x):
    return jnp.maximum(x, 0.0) + jnp.log(1.0 + jnp.exp(-jnp.abs(x)))


def _rms(x, g):
    r = lax.rsqrt(jnp.mean(x * x, axis=-1, keepdims=True) + EPS)
    return x * r * g


def _rms_bwd(x, g, dy):
    r = lax.rsqrt(jnp.mean(x * x, axis=-1, keepdims=True) + EPS)
    xh = x * r
    dxh = dy * g
    dx = r * (dxh - xh * jnp.mean(dxh * xh, axis=-1, keepdims=True))
    return dx, jnp.sum(dy * xh, axis=0, keepdims=True)


def _dot(a, b):
    return jnp.dot(a, b, preferred_element_type=F32)


def _dot_nt(a, b):
    return lax.dot_general(a, b, (((1,), (1,)), ((), ())), preferred_element_type=F32)


def _dot_tn(a, b):
    return lax.dot_general(a, b, (((0,), (0,)), ((), ())), preferred_element_type=F32)


def _split3(x):
    hi = x.astype(BF16)
    r = x - hi.astype(F32)
    mid = r.astype(BF16)
    return hi, mid, (r - mid.astype(F32)).astype(BF16)


def _dot_hi(a, b, split="a"):
    if split == "a":
        bb = b.astype(BF16)
        return sum(_dot(t, bb) for t in _split3(a))
    ab = a.astype(BF16)
    return sum(_dot(ab, t) for t in _split3(b))


def _dot_nt_hi(a, b):
    bb = b.astype(BF16)
    return sum(_dot_nt(t, bb) for t in _split3(a))


def _shift_down(x, halo, j):
    xr = pltpu.roll(x, j, axis=0)
    hr = pltpu.roll(halo, j, axis=0)
    row = lax.broadcasted_iota(jnp.int32, (8, x.shape[1]), 0)
    first = jnp.where(row < j, hr, xr[:8])
    return jnp.concatenate([first, xr[8:]], axis=0)


def _shift_up(x, nxt, j):
    t = x.shape[0]
    xr = pltpu.roll(x, t - j, axis=0)
    nr = pltpu.roll(nxt, 8 - j, axis=0)
    row = lax.broadcasted_iota(jnp.int32, (8, x.shape[1]), 0)
    last = jnp.where(row + j >= 8, nr, xr[t - 8:])
    return jnp.concatenate([xr[:t - 8], last], axis=0)


def _acc_rows(ref, val, first):
    @pl.when(first)
    def _():
        ref[...] = val

    @pl.when(jnp.logical_not(first))
    def _():
        ref[...] += val


def _mm_tn(a, b, name, tn=None, trs=(1664, 640, 128), chunked=False):
    r, m = a.shape
    n = b.shape[1]
    tn = n if tn is None else tn
    tr = _rt(r, trs)

    def body(a_ref, b_ref, o_ref):
        part = _dot_tn(a_ref[...].astype(BF16), b_ref[...].astype(BF16))
        _acc_rows(o_ref, part, pl.program_id(1) == 0)

    if chunked:
        out_specs, out_shape = pl.BlockSpec((None, m, tn), lambda j, i: (j, 0, 0)), _sds((n // tn, m, tn), F32)
    else:
        out_specs, out_shape = pl.BlockSpec((m, tn), lambda j, i: (0, j)), _sds((m, n), F32)
    return pl.pallas_call(
        body, name=name, grid=(n // tn, r // tr),
        in_specs=[pl.BlockSpec((tr, m), lambda j, i: (i, 0)), pl.BlockSpec((tr, tn), lambda j, i: (i, j))],
        out_specs=out_specs, out_shape=out_shape, compiler_params=_cp("parallel", "arbitrary"))(a, b)


def _inproj(h0, g, w):
    lp = h0.shape[0]
    tr = _rt(lp, MM_ROWS)
    segs = ((0, LAT_W), (LAT_W, LAT_W + D_SSM), (LAT_W + D_SSM, LAT_W + D_SSM + D_XBC), (IN_P - 128, IN_P))

    def body(h_ref, g_ref, w_ref, hn_ref, lat_ref, z_ref, xbc_ref, dt_ref):
        hn = _rms(h_ref[...], g_ref[...]).astype(BF16)
        hn_ref[...] = hn
        for ref, (a, b) in zip((lat_ref, z_ref, xbc_ref, dt_ref), segs):
            ref[...] = _dot(hn, w_ref[:, a:b])

    return pl.pallas_call(
        body, name="inproj", grid=(lp // tr,), in_specs=[_rows(tr, D), _full((1, D)), _full(w.shape)],
        out_specs=[_rows(tr, D), _rows(tr, LAT_W), _rows(tr, D_SSM), _rows(tr, D_XBC), _rows(tr, 128)],
        out_shape=[_sds((lp, D), BF16), _sds((lp, LAT_W), F32), _sds((lp, D_SSM), F32), _sds((lp, D_XBC), F32),
                   _sds((lp, 128), F32)],
        compiler_params=_cp("parallel"))(h0, g, w)


def _rope(x, cos, sa, sb):
    return x * cos + pltpu.roll(x, 96, axis=1) * sa + pltpu.roll(x, 32, axis=1) * sb


def _rope_t(g, cos, sa, sb):
    return g * cos + pltpu.roll(g * sa, 32, axis=1) + pltpu.roll(g * sb, 96, axis=1)


def _mla_prep(lat, qg, kvg, wq, wkv, cos, sa, sb):
    lp = lat.shape[0]
    tr = _rt(lp, MM_ROWS)

    def body(lat_ref, qg_ref, kvg_ref, wq_ref, wkv_ref, cos_ref, sa_ref, sb_ref, q_ref, k_ref, v_ref, ql_ref, kl_ref):
        lat_v = lat_ref[...]
        ql = _rms(lat_v[:, :QR], qg_ref[...]).astype(BF16)
        kl = _rms(lat_v[:, QR:QR + KVR], kvg_ref[...]).astype(BF16)
        ql_ref[...] = ql
        kl_ref[...] = kl
        cos_v, sa_v, sb_v = cos_ref[...], sa_ref[...], sb_ref[...]
        kpe = _rope(lat_v[:, QR + KVR:LAT_W], cos_v, sa_v, sb_v).astype(BF16)
        for h in range(MLA_H):
            q_ref[h, :, 0:DN] = (_dot(ql, wq_ref[:, h * DN:(h + 1) * DN]) * Q_SCALE).astype(BF16)
            qpe = _dot(ql, wq_ref[:, D + h * 128:D + (h + 1) * 128])
            q_ref[h, :, DN:2 * DN] = (_rope(qpe, cos_v, sa_v, sb_v) * Q_SCALE).astype(BF16)
            k_ref[h, :, 0:DN] = _dot(kl, wkv_ref[:, h * DN:(h + 1) * DN]).astype(BF16)
            k_ref[h, :, DN:2 * DN] = kpe
            v_ref[h] = _dot(kl, wkv_ref[:, D + h * DV:D + (h + 1) * DV]).astype(BF16)

    hb = lambda w: pl.BlockSpec((MLA_H, tr, w), lambda i: (0, i, 0))
    return pl.pallas_call(
        body, name="mla_prep", grid=(lp // tr,),
        in_specs=[_rows(tr, LAT_W), _full((1, QR)), _full((1, KVR)), _full(wq.shape), _full(wkv.shape),
                  _rows(tr, 128), _rows(tr, 128), _rows(tr, 128)],
        out_specs=[hb(256), hb(256), hb(128), _rows(tr, QR), _rows(tr, KVR)],
        out_shape=[_sds((MLA_H, lp, 256), BF16), _sds((MLA_H, lp, 256), BF16), _sds((MLA_H, lp, 128), BF16),
                   _sds((lp, QR), BF16), _sds((lp, KVR), BF16)],
        compiler_params=_cp("parallel"))(lat, qg, kvg, wq, wkv, cos, sa, sb)


def _attn_mask(r0, c0, tq, tk, transposed=False):
    if transposed:
        kk = c0 + lax.broadcasted_iota(jnp.int32, (tk, tq), 0)
        qq = r0 + lax.broadcasted_iota(jnp.int32, (tk, tq), 1)
    else:
        qq = r0 + lax.broadcasted_iota(jnp.int32, (tq, tk), 0)
        kk = c0 + lax.broadcasted_iota(jnp.int32, (tq, tk), 1)
    return jnp.logical_and(kk <= qq, kk >= PAD_ROWS)


def _attn_fwd(q, k, v, ride=()):
    lp = q.shape[1]
    t = _rt(lp, (640, 128))
    nq = lp // t

    hp = 2
    KW = (4, 3, 2, 1)

    nr = len(ride)
    steps = (MLA_H // hp) * nq

    def body(q_ref, k_ref, v_ref, *rest):
        o_ref, lse_ref = rest[nr:nr + 2]
        bufs, sems = rest[nr + 2:2 * nr + 2], rest[2 * nr + 2:]
        qi = pl.program_id(1)
        step = pl.program_id(0) * nq + qi
        if nr:
            pl.when(step == 0)(lambda: _ride_gather(bufs, *sems, 0))
            pl.when(step == steps // 2)(lambda: _ride_gather(bufs, *sems, 1))
        qv = [q_ref[a] for a in range(hp)]

        def tile(kj, carries, bias=None, width=1):
            starts = kj if isinstance(kj, tuple) else (kj,)
            kv_rows = [pl.ds(pl.multiple_of(k0 * t, t), width * t) for k0 in starts]
            out = []
            for a in range(hp):
                m, l, acc = carries[a]
                kk = jnp.concatenate([k_ref[a, r, :] for r in kv_rows], axis=0)
                vv = jnp.concatenate([v_ref[a, r, :] for r in kv_rows], axis=0)
                s = _dot_nt(qv[a], kk)
                if bias is not None:
                    s = s + bias
                m_new = jnp.maximum(m, jnp.max(s, axis=-1, keepdims=True))
                alpha = jnp.exp2(m - m_new)
                p = jnp.exp2(s - m_new)
                l = alpha * l + jnp.sum(p, axis=-1, keepdims=True)
                acc = alpha * acc + _dot(p.astype(BF16), vv)
                out.append((m_new, l, acc))
            return tuple(out)

        key = lax.broadcasted_iota(jnp.int32, (t, t), 1)
        pad_bias = jnp.where(jnp.logical_and(key >= PAD_ROWS, qi > 0), 0.0, NEG)
        diag_bias = jnp.where(_attn_mask(qi * t, qi * t, t, t), 0.0, NEG)
        init = tuple((jnp.full((t, 1), NEG, F32), jnp.zeros((t, 1), F32), jnp.zeros((t, DV), F32)) for _ in range(hp))
        carries = tile((0, qi), init, jnp.concatenate([pad_bias, diag_bias], axis=1))
        nxt = 1
        for width in KW:
            reps = jnp.maximum(qi - nxt, 0) // width
            carries = lax.fori_loop(0, reps, functools.partial(lambda j, c, nxt, width: tile(nxt + width * j, c, width=width),
                                                               nxt=nxt, width=width), carries)
            nxt = nxt + width * reps
        for a in range(hp):
            m, l, acc = carries[a]
            o_ref[:, a * DV:(a + 1) * DV] = acc / l
            lse_ref[a] = jnp.broadcast_to(m + jnp.log(l) * LOG2E, (t, 128)).T[:8]
        if nr:
            pl.when(step == steps - 1)(lambda: _ride_gather(bufs, *sems, 2))

    sems = [pltpu.SemaphoreType.DMA((6 * nr,)), pltpu.SemaphoreType.DMA((6 * nr,))] if nr else []
    outs = pl.pallas_call(
        body, name="attn_fwd", grid=(MLA_H // hp, nq),
        in_specs=[pl.BlockSpec((hp, t, 256), lambda h, i: (h, i, 0)), pl.BlockSpec((hp, lp, 256), lambda h, i: (h, 0, 0)),
                  pl.BlockSpec((hp, lp, 128), lambda h, i: (h, 0, 0))] + [ANY] * nr,
        out_specs=[pl.BlockSpec((t, hp * DV), lambda h, i: (i, h)), pl.BlockSpec((hp, 8, t), lambda h, i: (h, 0, i))] + [ANY] * nr,
        out_shape=[_sds((lp, MLA_H * DV), F32), _sds((MLA_H, 8, lp), F32)] + [_sds(b.shape, b.dtype) for b in ride],
        input_output_aliases={3 + w: 2 + w for w in range(nr)}, scratch_shapes=sems,
        compiler_params=_cp("arbitrary", "arbitrary"))(q, k, v, *ride)
    return outs[0], outs[1], list(outs[2:])


def _mixout_bwd(mix, g_post, dh1, o, g_ao, w_out, ride=()):
    lp = o.shape[0]
    tr = _rt(lp, MM_ROWS)
    nr = len(ride)
    steps = lp // tr

    def body(mix_ref, gp_ref, dh_ref, o_ref, g_ref, w_ref, *rest):
        gs = rest[:nr]
        dmix_ref, dssm_ref, do_ref, dgp_ref, dg_ref, dl_ref = rest[nr:nr + 6]
        from_sib, sems = rest[nr + 6:2 * nr + 6], rest[2 * nr + 6:]
        i = pl.program_id(0)
        if nr:
            pl.when(i == 0)(lambda: _ride_sibling(gs, from_sib, *sems, 0))
        grow = i * tr + lax.broadcasted_iota(jnp.int32, (tr, D), 0)
        dmix, dgp = _rms_bwd(mix_ref[...], gp_ref[...], jnp.where(grow >= PAD_ROWS, dh_ref[...], 0.0))
        dmix = dmix.astype(BF16)
        dmix_ref[...] = dmix
        _acc_rows(dgp_ref, dgp, i == 0)
        dssm_ref[...] = _dot_nt(dmix, w_ref[D:, :])
        ov = o_ref[...]
        do, dg = _rms_bwd(ov, g_ref[...], _dot_nt(dmix, w_ref[:D, :]))
        do_ref[...] = do
        _acc_rows(dg_ref, dg, i == 0)
        prod = do * ov
        lane = lax.broadcasted_iota(jnp.int32, (1, 128), 1)
        cols = jnp.zeros((tr, 128), F32)
        for h in range(MLA_H):
            cols = cols + jnp.sum(prod[:, h * DV:(h + 1) * DV], axis=-1, keepdims=True) * (lane == h).astype(F32)
        dl_ref[...] = cols.T[:MLA_H]
        if nr:
            pl.when(i == steps - 1)(lambda: _ride_sibling(gs, from_sib, *sems, 1))

    sems = [pltpu.SemaphoreType.DMA((nr,)), pltpu.SemaphoreType.DMA((nr,))] if nr else []
    outs = pl.pallas_call(
        body, name="mixout_bwd", grid=(steps,),
        in_specs=[_rows(tr, D), _full((1, D)), _rows(tr, D), _rows(tr, D), _full((1, D)), _full(w_out.shape)] + [ANY] * nr,
        out_specs=[_rows(tr, D), _rows(tr, D), _rows(tr, D), _full((1, D)), _full((1, D)),
                   pl.BlockSpec((MLA_H, tr), lambda i: (0, i))] + [ANY] * nr,
        out_shape=[_sds((lp, D), BF16), _sds((lp, D), F32), _sds((lp, D), F32), _sds((1, D), F32), _sds((1, D), F32),
                   _sds((MLA_H, lp), F32)] + [_sds((g.shape[0],) + g.shape[2:], g.dtype) for g in ride],
        scratch_shapes=sems, compiler_params=_cp("arbitrary"))(mix, g_post, dh1, o, g_ao, w_out, *ride)
    return tuple(outs[:6]) + (list(outs[6:]),)


def _attn_bwd(q, k, v, do, lse_row, delta_row, ride=()):
    lp = q.shape[1]
    t = _rt(lp, (640, 128))
    nq = lp // t

    nr = len(ride)
    QW = (3, 2, 1)

    def body(q_ref, k_ref, v_ref, do_ref, lse_ref, dl_ref, *rest):
        ps = rest[:nr]
        dq_ref, dk_ref, dv_ref = rest[nr:nr + 3]
        got, sems = rest[nr + 3:2 * nr + 3], rest[2 * nr + 3:]
        kj = pl.program_id(1)
        step = pl.program_id(0) * nq + kj
        if nr:
            pl.when(step == 0)(lambda: _ride_exchange(ps, got, *sems, 0))
        kk = k_ref[0]
        vv = v_ref[0]

        @pl.when(kj == 0)
        def _():
            dq_ref[...] = jnp.zeros_like(dq_ref)

        def tile(qi, carry, masked, width=1):
            dk, dv = carry
            q_rows = pl.ds(pl.multiple_of(qi * t, t), width * t)
            qv = q_ref[0, q_rows, :]
            dob = do_ref[q_rows, :].astype(BF16)
            lse_v = jnp.concatenate([lse_ref[0, qi + b] for b in range(width)], axis=1)
            dl_v = jnp.concatenate([dl_ref[0, qi + b] for b in range(width)], axis=1)
            st = _dot_nt(kk, qv)
            if masked:
                st = jnp.where(_attn_mask(qi * t, kj * t, width * t, t, transposed=True), st, NEG)
            pt = jnp.exp2(st - lse_v)
            dpt = _dot_nt(vv, dob)
            dst = (pt * (dpt - dl_v)).astype(BF16)
            dv = dv + _dot(pt.astype(BF16), dob)
            dk = dk + _dot(dst, qv)
            dq_ref[0, q_rows, :] += _dot_tn(dst, kk)
            return dk, dv

        carry = tile(kj, (jnp.zeros((t, 256), F32), jnp.zeros((t, DV), F32)), True)
        split = jnp.where(kj == 0, nq, kj + 1)

        def span(lo, hi, masked, carry):
            for width in QW:
                reps = (hi - lo) // width
                carry = lax.fori_loop(0, reps, functools.partial(
                    lambda j, c, lo, width: tile(lo + width * j, c, masked, width=width), lo=lo, width=width), carry)
                lo = lo + width * reps
            return carry

        dk, dv = span(split, nq, False, span(kj + 1, split, True, carry))
        dk_ref[0] = dk * LN2
        dv_ref[0] = dv
        if nr:
            pl.when(step == MLA_H * nq - 1)(lambda: _ride_exchange(ps, got, *sems, 1))

    stat = pl.BlockSpec((1, nq, 1, t), lambda h, j: (h, 0, 0, 0))
    sems = [pltpu.SemaphoreType.DMA((3 * nr,)), pltpu.SemaphoreType.DMA((3 * nr,))] if nr else []
    outs = pl.pallas_call(
        body, name="attn_bwd", grid=(MLA_H, nq),
        in_specs=[pl.BlockSpec((1, lp, 256), lambda h, j: (h, 0, 0)), pl.BlockSpec((1, t, 256), lambda h, j: (h, j, 0)),
                  pl.BlockSpec((1, t, 128), lambda h, j: (h, j, 0)), pl.BlockSpec((lp, DV), lambda h, j: (0, h)), stat, stat]
        + [ANY] * nr,
        out_specs=[pl.BlockSpec((1, lp, 256), lambda h, j: (h, 0, 0)), pl.BlockSpec((1, t, 256), lambda h, j: (h, j, 0)),
                   pl.BlockSpec((1, t, 128), lambda h, j: (h, j, 0))] + [ANY] * nr,
        out_shape=[_sds((MLA_H, lp, 256), F32), _sds((MLA_H, lp, 256), F32), _sds((MLA_H, lp, 128), F32)]
        + [_sds((3,) + p.shape[1:], p.dtype) for p in ride],
        scratch_shapes=sems, compiler_params=_cp("arbitrary", "arbitrary"))(q, k, v, do, lse_row, delta_row, *ride)
    return outs[0], outs[1], outs[2], list(outs[3:])


def _ssd_consts():
    ri = lax.broadcasted_iota(jnp.int32, (CHUNK, CHUNK), 0)
    ci = lax.broadcasted_iota(jnp.int32, (CHUNK, CHUNK), 1)
    expand = (lax.broadcasted_iota(jnp.int32, (128, D_SSM), 0)
              == lax.broadcasted_iota(jnp.int32, (128, D_SSM), 1) // SSM_P).astype(F32)
    return ri, ci, expand


def _ssd_chunk(c, x_ref, xh_ref, dt_ref, dtT_ref, cw_ref, cb_ref, dtb_ref, dtbT_ref, al_ref, alT_ref):
    ri, ci, expand = _ssd_consts()
    x = x_ref[...]
    halo = jnp.where(c > 0, xh_ref[...], 0.0)
    sh = [x] + [_shift_down(x, halo, j) for j in range(1, SSM_K)]
    cv = cb_ref[...]
    for kk in range(SSM_K):
        cv = cv + cw_ref[kk:kk + 1, :] * sh[SSM_K - 1 - kk]
    xa = _silu(cv)
    grow = c * CHUNK + ri
    gcol = c * CHUNK + lax.broadcasted_iota(jnp.int32, (SSM_H, CHUNK), 1)
    sp = dt_ref[...] + dtb_ref[...]
    spT = dtT_ref[...] + dtbT_ref[...]
    dtc = jnp.where(grow >= PAD_ROWS, _softplus(sp), 0.0)
    dtr = jnp.where(gcol >= PAD_ROWS, _softplus(spT), 0.0)
    arow = -jnp.exp(al_ref[...])
    acolT = -jnp.exp(alT_ref[...])
    ltri = (ci <= ri).astype(F32)
    acs = _dot_hi(ltri, dtc * arow, split="b")
    acsT = _dot_hi(dtr * acolT, (ri <= ci).astype(F32))
    return dict(x=x, sh=sh, cv=cv, xa=xa, sp=sp, dtc=dtc, arow=arow, acs=acs, acsT=acsT, ri=ri, ci=ci, expand=expand,
                grow=grow)


def _ssd_mats(k, s_prev):
    xa, acs, acsT, expand, ri, ci = k["xa"], k["acs"], k["acsT"], k["expand"], k["ri"], k["ci"]
    xs = xa[:, :D_SSM]
    dt_e = _dot_hi(k["dtc"], expand)
    acs_e = _dot_hi(acs, expand)
    last_e = acs_e[CHUNK - 1:CHUNK, :]
    ea = jnp.exp(acs_e)
    f = jnp.exp(last_e - acs_e)
    cd = jnp.exp(last_e)
    xdt = xs * dt_e
    bm = [xa[:, D_SSM + g * SSM_N:D_SSM + (g + 1) * SSM_N] for g in range(SSM_G)]
    cm = [xa[:, D_SSM + (SSM_G + g) * SSM_N:D_SSM + (SSM_G + g + 1) * SSM_N] for g in range(SSM_G)]
    bmb = [b.astype(BF16) for b in bm]
    cmb = [cc.astype(BF16) for cc in cm]
    cb = [_dot_nt(cmb[g], bmb[g]) for g in range(SSM_G)]
    lam, mm = [], []
    causal = jnp.where(ci <= ri, 0.0, NEG)
    for h in range(SSM_H):
        lam_h = jnp.exp((acs[:, h:h + 1] - acsT[h:h + 1, :]) + causal)
        lam.append(lam_h)
        mm.append(cb[h // (SSM_H // SSM_G)] * lam_h)
    lo = lax.broadcasted_iota(jnp.int32, (CHUNK, 128), 1) < SSM_P
    xdt_h = []
    for h in range(SSM_H):
        pair = xdt[:, (h // 2) * 128:(h // 2 + 1) * 128]
        xdt_h.append(jnp.where(lo if h % 2 == 0 else jnp.logical_not(lo), pair, 0.0).astype(BF16))
    ydiag = jnp.concatenate(
        [_dot(mm[2 * j].astype(BF16), xdt_h[2 * j]) + _dot(mm[2 * j + 1].astype(BF16), xdt_h[2 * j + 1])
         for j in range(SSM_H // 2)], axis=1)
    t_off = [_dot(cmb[g], s_prev[g].astype(BF16)) for g in range(SSM_G)]
    yoff = jnp.concatenate(t_off, axis=1) * ea
    return dict(xs=xs, dt_e=dt_e, acs_e=acs_e, ea=ea, f=f, cd=cd, xdt=xdt, bm=bm, cm=cm, bmb=bmb, cmb=cmb, cb=cb, lam=lam,
                mm=mm, lo=lo, xdt_h=xdt_h, ydiag=ydiag, t_off=t_off, yoff=yoff)


def _ssd_specs(nc, rev):
    ix = (lambda i: nc - 1 - i) if rev else (lambda i: i)
    return [
        pl.BlockSpec((CHUNK, D_XBC), lambda i: (ix(i), 0)),
        pl.BlockSpec((8, D_XBC), lambda i: (jnp.maximum(ix(i) * (CHUNK // 8) - 1, 0), 0)),
        pl.BlockSpec((CHUNK, D_SSM), lambda i: (ix(i), 0)),
        pl.BlockSpec((CHUNK, 128), lambda i: (ix(i), 0)),
        pl.BlockSpec((SSM_H, CHUNK), lambda i: (0, ix(i))),
        _full((8, D_XBC)), _full((1, D_XBC)), _full((1, 128)), _full((SSM_H, 1)), _full((1, 128)), _full((SSM_H, 1)),
        _full((1, D_SSM)), _full((1, D_SSM)),
    ]


def _ssd_fwd(xbc, z, dtr, dtrT, cw, cb, dtb, dtbT, alog, alogT, d_e, ng):
    lp = xbc.shape[0]
    nc = lp // CHUNK

    def body(x_ref, xh_ref, z_ref, dt_ref, dtT_ref, cw_ref, cb_ref, dtb_ref, dtbT_ref, al_ref, alT_ref, de_ref, ng_ref,
             y_ref, st_ref, s_scr):
        c = pl.program_id(0)

        @pl.when(c == 0)
        def _():
            s_scr[...] = jnp.zeros_like(s_scr)

        k = _ssd_chunk(c, x_ref, xh_ref, dt_ref, dtT_ref, cw_ref, cb_ref, dtb_ref, dtbT_ref, al_ref, alT_ref)
        s_prev = [s_scr[g] for g in range(SSM_G)]
        st_ref[0] = s_scr[...]
        m = _ssd_mats(k, s_prev)
        xd = (m["xdt"] * m["f"]).astype(BF16)
        for g in range(SSM_G):
            sl = slice(g * GSZ, (g + 1) * GSZ)
            s_scr[g] = m["cd"][:, sl] * s_prev[g] + _dot(m["bm"][g].T.astype(BF16), xd[:, sl])
        y = m["ydiag"] + m["yoff"] + de_ref[...] * m["xs"]
        u = y * _silu(z_ref[...])
        outs = []
        for g in range(SSM_G):
            ug = u[:, g * GSZ:(g + 1) * GSZ]
            outs.append(ug * lax.rsqrt(jnp.mean(ug * ug, axis=-1, keepdims=True) + EPS))
        y_ref[...] = jnp.concatenate(outs, axis=1) * ng_ref[...]

    return pl.pallas_call(
        body, name="ssd_fwd", grid=(nc,), in_specs=_ssd_specs(nc, False),
        out_specs=[_rows(CHUNK, D_SSM), pl.BlockSpec((1, SSM_G, SSM_N, GSZ), lambda i: (i, 0, 0, 0))],
        out_shape=[_sds((lp, D_SSM), F32), _sds((nc, SSM_G, SSM_N, GSZ), F32)],
        scratch_shapes=[pltpu.VMEM((SSM_G, SSM_N, GSZ), F32)],
        compiler_params=_cp("arbitrary"))(xbc, xbc, z, dtr, dtrT, cw, cb, dtb, dtbT, alog, alogT, d_e, ng)


def _ssd_bwd(dssm, xbc, z, dtr, dtrT, st, cw, cb, dtb, dtbT, alog, alogT, d_e, ng):
    lp = xbc.shape[0]
    nc = lp // CHUNK
    hpg = SSM_H // SSM_G

    def body(dy_ref, x_ref, xh_ref, z_ref, dt_ref, dtT_ref, st_ref, cw_ref, cb_ref, dtb_ref, dtbT_ref, al_ref, alT_ref,
             de_ref, ng_ref, dz_ref, dx_ref, ddt_ref, dcw_ref, dcb_ref, ddtb_ref, dal_ref, dd_ref, dng_ref, ds_scr, nx_scr):
        i = pl.program_id(0)
        c = nc - 1 - i
        first = i == 0

        @pl.when(first)
        def _():
            ds_scr[...] = jnp.zeros_like(ds_scr)
            nx_scr[...] = jnp.zeros_like(nx_scr)

        k = _ssd_chunk(c, x_ref, xh_ref, dt_ref, dtT_ref, cw_ref, cb_ref, dtb_ref, dtbT_ref, al_ref, alT_ref)
        s_prev = [st_ref[0, g] for g in range(SSM_G)]
        m = _ssd_mats(k, s_prev)
        ri, ci, expand = k["ri"], k["ci"], k["expand"]
        xs, acs, acsT = m["xs"], k["acs"], k["acsT"]
        zv = z_ref[...]
        dout = dy_ref[...]
        ngv = ng_ref[...]
        y = m["ydiag"] + m["yoff"] + de_ref[...] * xs
        sz = _silu(zv)
        u = y * sz
        du_parts, dng_parts = [], []
        for g in range(SSM_G):
            sl = slice(g * GSZ, (g + 1) * GSZ)
            dug, dngg = _rms_bwd(u[:, sl], ngv[:, sl], dout[:, sl])
            du_parts.append(dug)
            dng_parts.append(dngg)
        du = jnp.concatenate(du_parts, axis=1)
        _acc_rows(dng_ref, jnp.concatenate(dng_parts, axis=1), first)
        dy = du * sz
        dz_ref[...] = du * y * _dsilu(zv)
        dd_e = jnp.sum(dy * xs, axis=0, keepdims=True)
        _acc_rows(dd_ref, _dot_nt_hi(dd_e, expand), first)
        dxs = de_ref[...] * dy
        dacs_e = dy * m["yoff"]
        dtg = (dy * m["ea"]).astype(BF16)
        dxdt = jnp.zeros_like(xs)
        dlast_e = []
        db, dc, ds_prev = [], [], []
        xd = m["xdt"] * m["f"]
        dxd_all = []
        for g in range(SSM_G):
            sl = slice(g * GSZ, (g + 1) * GSZ)
            dsg = ds_scr[g]
            spb = s_prev[g].astype(BF16)
            dc.append(_dot_nt(dtg[:, sl], spb))
            dsp = _dot(m["cm"][g].T.astype(BF16), dtg[:, sl]) + m["cd"][:, sl] * dsg
            ds_prev.append(dsp)
            dlast_e.append(jnp.sum(dsg * s_prev[g], axis=0, keepdims=True) * m["cd"][:, sl])
            dsb = dsg.astype(BF16)
            db.append(_dot_nt(xd[:, sl].astype(BF16), dsb))
            dxd_all.append(_dot(m["bmb"][g], dsb))
        dxd = jnp.concatenate(dxd_all, axis=1)
        dxdt = dxd * m["f"]
        dff = dxd * xd
        dacs_e = dacs_e - dff
        dlast_row = jnp.concatenate(dlast_e, axis=1) + jnp.sum(dff, axis=0, keepdims=True)
        dacs = jnp.zeros((CHUNK, 128), F32)
        dacs_t = jnp.zeros((CHUNK, CHUNK), F32)
        lane = lax.broadcasted_iota(jnp.int32, (1, 128), 1)
        dgs = [jnp.zeros((CHUNK, CHUNK), F32) for _ in range(SSM_G)]
        dxdt_pairs = []
        for h in range(SSM_H):
            g = h // hpg
            pr = slice((h // 2) * 128, (h // 2 + 1) * 128)
            lo_h = m["lo"] if h % 2 == 0 else jnp.logical_not(m["lo"])
            dyp = jnp.where(lo_h, dy[:, pr], 0.0).astype(BF16)
            dm = _dot_nt(dyp, m["xdt"][:, pr].astype(BF16))
            dgs[g] = dgs[g] + dm * m["lam"][h]
            w_h = dm * m["mm"][h]
            dacs = dacs + jnp.sum(w_h, axis=1, keepdims=True) * (lane == h).astype(F32)
            dacs_t = dacs_t + jnp.where(ri == h, jnp.sum(w_h, axis=0, keepdims=True), 0.0)
            part = _dot_tn(m["mm"][h].astype(BF16), dyp)
            if h % 2 == 0:
                dxdt_pairs.append(part)
            else:
                dxdt_pairs[-1] = dxdt_pairs[-1] + part
        dxdt = dxdt + jnp.concatenate(dxdt_pairs, axis=1)
        for g in range(SSM_G):
            dgb = dgs[g].astype(BF16)
            dc[g] = dc[g] + _dot(dgb, m["bmb"][g])
            db[g] = db[g] + _dot_tn(dgb, m["cmb"][g])
        dacs = dacs - dacs_t.T + _dot_nt_hi(dacs_e, expand)
        dlast = _dot_nt_hi(dlast_row, expand)
        dacs = dacs + jnp.where(ri == CHUNK - 1, dlast, 0.0)
        dxs = dxs + dxdt * m["dt_e"]
        ddt = _dot_nt_hi(dxdt * xs, expand)
        da = _dot_hi((ri <= ci).astype(F32), dacs, split="b")
        ddt = ddt + da * k["arow"]
        dA = jnp.sum(da * k["dtc"], axis=0, keepdims=True)
        _acc_rows(dal_ref, dA * k["arow"], first)
        ddtr = jnp.where(k["grow"] >= PAD_ROWS, ddt * _sigmoid(k["sp"]), 0.0)
        ddt_ref[...] = ddtr
        _acc_rows(ddtb_ref, jnp.sum(ddtr, axis=0, keepdims=True), first)
        for g in range(SSM_G):
            ds_scr[g] = ds_prev[g]
        dxa = jnp.concatenate([dxs] + db + dc, axis=1)
        dcv = dxa * _dsilu(k["cv"])
        _acc_rows(dcb_ref, jnp.sum(dcv, axis=0, keepdims=True), first)
        dcw_rows = [jnp.sum(dcv * k["sh"][SSM_K - 1 - kk], axis=0, keepdims=True) for kk in range(SSM_K)]
        dcw_rows.append(jnp.zeros((8 - SSM_K, D_XBC), F32))
        _acc_rows(dcw_ref, jnp.concatenate(dcw_rows, axis=0), first)
        nxt = nx_scr[...]
        dx = cw_ref[SSM_K - 1:SSM_K, :] * dcv
        for j in range(1, SSM_K):
            dx = dx + cw_ref[SSM_K - 1 - j:SSM_K - j, :] * _shift_up(dcv, nxt, j)
        grow_x = c * CHUNK + lax.broadcasted_iota(jnp.int32, (CHUNK, D_XBC), 0)
        dx_ref[...] = jnp.where(grow_x >= PAD_ROWS, dx, 0.0)
        nx_scr[...] = dcv[:8]

    specs = _ssd_specs(nc, True)
    in_specs = [pl.BlockSpec((CHUNK, D_SSM), lambda i: (nc - 1 - i, 0))] + specs[:5] + [
        pl.BlockSpec((1, SSM_G, SSM_N, GSZ), lambda i: (nc - 1 - i, 0, 0, 0))] + specs[5:]
    rv = lambda w: pl.BlockSpec((CHUNK, w), lambda i: (nc - 1 - i, 0))
    return pl.pallas_call(
        body, name="ssd_bwd", grid=(nc,), in_specs=in_specs,
        out_specs=[rv(D_SSM), rv(D_XBC), rv(128), _full((8, D_XBC)), _full((1, D_XBC)), _full((1, 128)), _full((1, 128)),
                   _full((1, 128)), _full((1, D_SSM))],
        out_shape=[_sds((lp, D_SSM), F32), _sds((lp, D_XBC), F32), _sds((lp, 128), F32), _sds((8, D_XBC), F32),
                   _sds((1, D_XBC), F32), _sds((1, 128), F32), _sds((1, 128), F32), _sds((1, 128), F32), _sds((1, D_SSM), F32)],
        scratch_shapes=[pltpu.VMEM((SSM_G, SSM_N, GSZ), F32), pltpu.VMEM((8, D_XBC), F32)],
        compiler_params=_cp("arbitrary"))(dssm, xbc, xbc, z, dtr, dtrT, st, cw, cb, dtb, dtbT, alog, alogT, d_e, ng)


def _mixout_fwd(o, ssm, h0, g_ao, g_post, w):
    lp = o.shape[0]
    tr = _rt(lp, MM_ROWS)

    def body(o_ref, s_ref, h_ref, ga_ref, gp_ref, w_ref, mi_ref, mix_ref, h1_ref):
        mixin = jnp.concatenate([_rms(o_ref[...], ga_ref[...]), s_ref[...]], axis=1).astype(BF16)
        mi_ref[...] = mixin
        mix = _dot(mixin, w_ref[...])
        mix_ref[...] = mix
        grow = pl.program_id(0) * tr + lax.broadcasted_iota(jnp.int32, (tr, D), 0)
        h1_ref[...] = h_ref[...] + jnp.where(grow >= PAD_ROWS, _rms(mix, gp_ref[...]), 0.0)

    return pl.pallas_call(
        body, name="mixout_fwd", grid=(lp // tr,),
        in_specs=[_rows(tr, D), _rows(tr, D), _rows(tr, D), _full((1, D)), _full((1, D)), _full(w.shape)],
        out_specs=[_rows(tr, 2 * D), _rows(tr, D), _rows(tr, D)],
        out_shape=[_sds((lp, 2 * D), BF16), _sds((lp, D), F32), _sds((lp, D), F32)],
        compiler_params=_cp("parallel"))(o, ssm, h0, g_ao, g_post, w)


def _ffn_up(h1, g, w):
    lp = h1.shape[0]
    tr = _rt(lp, (1664,) + MM_ROWS)
    tn = D_FF // 2

    def body(h_ref, g_ref, w_ref, hn_ref, u_ref):
        @pl.when(pl.program_id(1) == 0)
        def _():
            hn_ref[...] = _rms(h_ref[...], g_ref[...]).astype(BF16)

        u_ref[...] = _dot(hn_ref[...], w_ref[...]).astype(BF16)

    return pl.pallas_call(
        body, name="ffn_up", grid=(lp // tr, 2 * D_FF // tn),
        in_specs=[pl.BlockSpec((tr, D), lambda i, j: (i, 0)), _full((1, D)), pl.BlockSpec((None, D, tn), lambda i, j: (j, 0, 0))],
        out_specs=[pl.BlockSpec((tr, D), lambda i, j: (i, 0)), pl.BlockSpec((tr, tn), lambda i, j: (i, j))],
        out_shape=[_sds((lp, D), BF16), _sds((lp, 2 * D_FF), BF16)],
        compiler_params=_cp("parallel", "arbitrary"))(h1, g, w)


def _ffn_in_bwd(du, w4, h1, g, dh2):
    lp = du.shape[0]
    nch, _, tn = w4.shape
    tr = _rt(lp, (320, 128))

    def body(du_ref, w_ref, h_ref, g_ref, r_ref, o_ref, dg_ref):
        acc = _dot_nt(du_ref[:, 0:tn], w_ref[0])
        for j in range(1, nch):
            acc = acc + _dot_nt(du_ref[:, j * tn:(j + 1) * tn], w_ref[j])
        dx, dg = _rms_bwd(h_ref[...], g_ref[...], acc)
        o_ref[...] = dx + r_ref[...]
        _acc_rows(dg_ref, dg, pl.program_id(0) == 0)

    return pl.pallas_call(
        body, name="ffn_in_bwd", grid=(lp // tr,),
        in_specs=[_rows(tr, nch * tn), _full(w4.shape), _rows(tr, D), _full((1, D)), _rows(tr, D)],
        out_specs=[_rows(tr, D), _full((1, D))], out_shape=[_sds((lp, D), F32), _sds((1, D), F32)],
        compiler_params=_cp("arbitrary"))(du, w4, h1, g, dh2)


FFN_CB = 256


def _ffn_gate(u, cw, cb):
    lp = u.shape[0]
    tr = _rt(lp, (320, 128))

    def body(u_ref, uh_ref, cw_ref, cb_ref, uc_ref, a_ref):
        i = pl.program_id(0)
        for j in range(D_FF // FFN_CB):
            halves = []
            for off in (0, D_FF):
                sl = slice(off + j * FFN_CB, off + (j + 1) * FFN_CB)
                x = u_ref[:, sl].astype(F32)
                halo = jnp.where(i > 0, uh_ref[8:16, sl].astype(F32), 0.0)
                cv = cb_ref[:, sl] + cw_ref[FFN_K - 1:FFN_K, sl] * x
                for s in range(1, FFN_K):
                    cv = cv + cw_ref[FFN_K - 1 - s:FFN_K - s, sl] * _shift_down(x, halo, s)
                uc_ref[:, sl] = cv.astype(BF16)
                halves.append(cv)
            a_ref[:, j * FFN_CB:(j + 1) * FFN_CB] = (_silu(halves[0]) * halves[1]).astype(BF16)

    return pl.pallas_call(
        body, name="ffn_gate", grid=(lp // tr,),
        in_specs=[_rows(tr, 2 * D_FF), pl.BlockSpec((16, 2 * D_FF), lambda i: (jnp.maximum(i * (tr // 16) - 1, 0), 0)),
                  _full((8, 2 * D_FF)), _full((1, 2 * D_FF))],
        out_specs=[_rows(tr, 2 * D_FF), _rows(tr, D_FF)], out_shape=[_sds((lp, 2 * D_FF), BF16), _sds((lp, D_FF), BF16)],
        compiler_params=_cp("parallel"))(u, u, cw, cb)


def _ffn_down(a, w, h1, tgt, g_post):
    lp = a.shape[0]
    tr = _rt(lp, MM_ROWS)
    nb = tr // FRONT

    def body(a_ref, w_ref, h_ref, *rest):
        t_refs, (g_ref, dh2_ref, dd_ref, dg_ref, loss_ref) = rest[:nb], rest[nb:]
        i = pl.program_id(0)
        d = _dot(a_ref[...], w_ref[...])
        gv = g_ref[...]
        h2 = h_ref[...] + _rms(d, gv)
        grow = i * tr + lax.broadcasted_iota(jnp.int32, (tr, D), 0)
        tgt_v = jnp.concatenate([r[...] for r in t_refs], axis=0)
        err = jnp.where(grow >= FRONT, h2 - tgt_v, 0.0)
        dh2 = err * (1.0 / D)
        dh2_ref[...] = dh2
        dd, dg = _rms_bwd(d, gv, dh2)
        dd_ref[...] = dd.astype(BF16)
        _acc_rows(dg_ref, dg, i == 0)
        part = 0.5 * jnp.sum(jnp.sum(err * err, axis=1, keepdims=True), axis=0, keepdims=True) * (1.0 / D)
        _acc_rows(loss_ref, jnp.broadcast_to(part, (8, 128)), i == 0)

    return pl.pallas_call(
        body, name="ffn_down", grid=(lp // tr,),
        in_specs=[_rows(tr, D_FF), _full(w.shape), _rows(tr, D)]
        + [pl.BlockSpec((FRONT, D), functools.partial(lambda i, b: (jnp.maximum(i * nb - 1 + b, 0), 0), b=b)) for b in range(nb)]
        + [_full((1, D))],
        out_specs=[_rows(tr, D), _rows(tr, D), _full((1, D)), _full((8, 128))],
        out_shape=[_sds((lp, D), F32), _sds((lp, D), BF16), _sds((1, D), F32), _sds((8, 128), F32)],
        compiler_params=_cp("arbitrary"))(a, w, h1, *([tgt] * nb), g_post)


def _ffn_gate_bwd(u, uc, dd, w_down, cw):
    lp = u.shape[0]
    tr = _rt(lp, (320, 128))
    n = lp // tr

    def body(u_ref, uc_ref, dd_ref, wd_ref, cw_ref, du_ref, dcw_ref, dcb_ref, nx_scr):
        i = pl.program_id(0)
        t = n - 1 - i
        first = i == 0

        @pl.when(first)
        def _():
            nx_scr[...] = jnp.zeros_like(nx_scr)

        grow = t * tr + lax.broadcasted_iota(jnp.int32, (tr, FFN_CB), 0)
        ddv = dd_ref[...]
        for j in range(D_FF // FFN_CB):
            sls = [slice(off + j * FFN_CB, off + (j + 1) * FFN_CB) for off in (0, D_FF)]
            cvg, cvv = uc_ref[:, sls[0]].astype(F32), uc_ref[:, sls[1]].astype(F32)
            dav = _dot_nt(ddv, wd_ref[j * FFN_CB:(j + 1) * FFN_CB, :])
            dcv = (dav * cvv * _dsilu(cvg), dav * _silu(cvg))
            for hf in range(2):
                sl = sls[hf]
                g = dcv[hf]
                nxt = nx_scr[:, sl]
                ahead = [g] + [_shift_up(g, nxt, s) for s in range(1, FFN_K)]
                x = u_ref[:, sl].astype(F32)
                rows = [jnp.sum(x * ahead[FFN_K - 1 - kk], axis=0, keepdims=True) for kk in range(FFN_K)]
                rows.append(jnp.zeros((8 - FFN_K, FFN_CB), F32))
                upd_w = jnp.concatenate(rows, axis=0)
                upd_b = jnp.sum(g, axis=0, keepdims=True)

                @pl.when(first)
                def _():
                    dcw_ref[:, sl] = upd_w
                    dcb_ref[:, sl] = upd_b

                @pl.when(jnp.logical_not(first))
                def _():
                    dcw_ref[:, sl] += upd_w
                    dcb_ref[:, sl] += upd_b

                du = cw_ref[FFN_K - 1:FFN_K, sl] * g
                for s in range(1, FFN_K):
                    du = du + cw_ref[FFN_K - 1 - s:FFN_K - s, sl] * ahead[s]
                du_ref[:, sl] = jnp.where(grow >= PAD_ROWS, du, 0.0).astype(BF16)
                nx_scr[:, sl] = g[:8]

    wide = pl.BlockSpec((tr, 2 * D_FF), lambda i: (n - 1 - i, 0))
    return pl.pallas_call(
        body, name="ffn_gate_bwd", grid=(n,),
        in_specs=[wide, wide, pl.BlockSpec((tr, D), lambda i: (n - 1 - i, 0)), _full(w_down.shape), _full((8, 2 * D_FF))],
        out_specs=[wide, _full((8, 2 * D_FF)), _full((1, 2 * D_FF))],
        out_shape=[_sds((lp, 2 * D_FF), BF16), _sds((8, 2 * D_FF), F32), _sds((1, 2 * D_FF), F32)],
        scratch_shapes=[pltpu.VMEM((8, 2 * D_FF), F32)],
        compiler_params=_cp("arbitrary"))(u, uc, dd, w_down, cw)


def _mla_bwd(dq, dk, dv, lat, qg, kvg, wq, wkv, cos, sa, sb):
    lp = lat.shape[0]
    tr = _rt(lp, (320, 128))

    def body(dq_ref, dk_ref, dv_ref, lat_ref, qg_ref, kvg_ref, wq_ref, wkv_ref, cos_ref, sa_ref, sb_ref,
             dqf_ref, dkvf_ref, dlat_ref, dqg_ref, dkvg_ref):
        i = pl.program_id(0)
        cos_v, sa_v, sb_v = cos_ref[...], sa_ref[...], sb_ref[...]
        dkpe = jnp.zeros((tr, 128), F32)
        for h in range(MLA_H):
            dqh = dq_ref[h] * SOFTMAX_SCALE
            dqf_ref[:, h * DN:(h + 1) * DN] = dqh[:, :DN].astype(BF16)
            dqf_ref[:, D + h * 128:D + (h + 1) * 128] = _rope_t(dqh[:, DN:], cos_v, sa_v, sb_v).astype(BF16)
            dkh = dk_ref[h]
            dkvf_ref[:, h * DN:(h + 1) * DN] = dkh[:, :DN].astype(BF16)
            dkpe = dkpe + dkh[:, DN:]
            dkvf_ref[:, D + h * DV:D + (h + 1) * DV] = dv_ref[h].astype(BF16)
        dql = _dot_nt(dqf_ref[...], wq_ref[...])
        dkl = _dot_nt(dkvf_ref[...], wkv_ref[...])
        lat_v = lat_ref[...]
        dqc, dqg = _rms_bwd(lat_v[:, :QR], qg_ref[...], dql)
        dkc, dkg = _rms_bwd(lat_v[:, QR:QR + KVR], kvg_ref[...], dkl)
        dlat_ref[:, :QR] = dqc
        dlat_ref[:, QR:QR + KVR] = dkc
        dlat_ref[:, QR + KVR:] = _rope_t(dkpe, cos_v, sa_v, sb_v)
        _acc_rows(dqg_ref, dqg, i == 0)
        _acc_rows(dkvg_ref, dkg, i == 0)

    hb = lambda w: pl.BlockSpec((MLA_H, tr, w), lambda i: (0, i, 0))
    return pl.pallas_call(
        body, name="mla_bwd", grid=(lp // tr,),
        in_specs=[hb(256), hb(256), hb(128), _rows(tr, LAT_W), _full((1, QR)), _full((1, KVR)), _full(wq.shape),
                  _full(wkv.shape), _rows(tr, 128), _rows(tr, 128), _rows(tr, 128)],
        out_specs=[_rows(tr, 2 * D), _rows(tr, 2 * D), _rows(tr, LAT_W), _full((1, QR)), _full((1, KVR))],
        out_shape=[_sds((lp, 2 * D), BF16), _sds((lp, 2 * D), BF16), _sds((lp, LAT_W), F32), _sds((1, QR), F32),
                   _sds((1, KVR), F32)],
        compiler_params=_cp("arbitrary"))(dq, dk, dv, lat, qg, kvg, wq, wkv, cos, sa, sb)


def _inproj_bwd(dlat, dz, dxbc, ddt, w, h0, g, dh1, ride=()):
    lp = h0.shape[0]
    tr = _rt(lp, (320, 128))
    segs = ((0, LAT_W), (LAT_W, LAT_W + D_SSM), (LAT_W + D_SSM, LAT_W + D_SSM + D_XBC), (IN_P - 128, IN_P))
    nr = len(ride)
    steps = lp // tr

    def body(dl_ref, dz_ref, dx_ref, dt_ref, w_ref, h_ref, g_ref, r_ref, *rest):
        ps = rest[:nr]
        o_ref, dg_ref = rest[nr:nr + 2]
        got, sems = rest[nr + 2:2 * nr + 2], rest[2 * nr + 2:]
        step = pl.program_id(0)
        if nr:
            pl.when(step == 0)(lambda: _ride_exchange(ps, got, *sems, 0))
        dhn = jnp.zeros((tr, D), F32)
        for ref, (a, b) in zip((dl_ref, dz_ref, dx_ref, dt_ref), segs):
            dhn = dhn + _dot_nt(ref[...].astype(BF16), w_ref[:, a:b])
        dx, dg = _rms_bwd(h_ref[...], g_ref[...], dhn)
        o_ref[...] = dx + r_ref[...]
        _acc_rows(dg_ref, dg, step == 0)
        if nr:
            pl.when(step == steps - 1)(lambda: _ride_exchange(ps, got, *sems, 1))

    sems = [pltpu.SemaphoreType.DMA((3 * nr,)), pltpu.SemaphoreType.DMA((3 * nr,))] if nr else []
    outs = pl.pallas_call(
        body, name="inproj_bwd", grid=(steps,),
        in_specs=[_rows(tr, LAT_W), _rows(tr, D_SSM), _rows(tr, D_XBC), _rows(tr, 128), _full(w.shape), _rows(tr, D),
                  _full((1, D)), _rows(tr, D)] + [ANY] * nr,
        out_specs=[_rows(tr, D), _full((1, D))] + [ANY] * nr,
        out_shape=[_sds((lp, D), F32), _sds((1, D), F32)] + [_sds((3,) + p.shape[1:], p.dtype) for p in ride],
        scratch_shapes=sems, compiler_params=_cp("arbitrary"))(dlat, dz, dxbc, ddt, w, h0, g, dh1, *ride)
    return outs[0], outs[1], list(outs[2:])


def _rope_tables(lp):
    pos = (jnp.arange(lp, dtype=jnp.int32) - PAD_ROWS).astype(F32)
    inv = ROPE_THETA ** (-jnp.arange(0, DR, 2, dtype=F32) / DR)
    ang = pos[:, None] * inv[None, :]
    cos, sin = jnp.cos(ang), jnp.sin(ang)
    zero = jnp.zeros_like(sin)
    cos128 = jnp.concatenate([cos, cos, cos, cos], axis=1)
    sa128 = jnp.concatenate([-sin, zero, -sin, zero], axis=1)
    sb128 = jnp.concatenate([zero, sin, zero, sin], axis=1)
    return cos128, sa128, sb128


def _pad_rows8(w):
    return jnp.concatenate([w, jnp.zeros((8 - w.shape[0], w.shape[1]), w.dtype)], axis=0)


def _lane_pad(v):
    return jnp.concatenate([v, jnp.zeros((v.shape[0], 128 - v.shape[1]), v.dtype)], axis=1)


def _late_weights(bufs):
    w_out, w_up, w_down = bufs
    return dict(w_out=w_out.reshape(2 * D, D), w_up=w_up.reshape(N_CHIPS, D, 2 * D_FF // N_CHIPS), w_down=w_down.reshape(D_FF, D))


def _device_step(x, tgt, meta, p, late_bufs=(), early_reduce=None, last_reduce=None):
    s = x.shape[0]
    lp = s + FRONT
    zpad = jnp.zeros((PAD_ROWS, D), F32)
    h0 = jnp.concatenate([zpad, meta, x], axis=0)
    cos, sa, sb = _rope_tables(lp)

    w_in = p["w_in"]
    w_in_p = jnp.concatenate([w_in[:, :QR + KVR + DR], jnp.zeros((D, 64), BF16), w_in[:, QR + KVR + DR:],
                              jnp.zeros((D, 128 - SSM_H), BF16)], axis=1)
    w_uq = p["w_uq"]
    wq_p = jnp.concatenate([w_uq[:, :, :DN].reshape(QR, MLA_H * DN),
                            jnp.concatenate([w_uq[:, :, DN:], jnp.zeros((QR, MLA_H, 128 - DR), BF16)], axis=2).reshape(QR, MLA_H * 128)],
                           axis=1)
    w_ukv = p["w_ukv"]
    wkv_p = jnp.concatenate([w_ukv[:, :, :DN].reshape(KVR, MLA_H * DN), w_ukv[:, :, DN:].reshape(KVR, MLA_H * DV)], axis=1)
    scw = _pad_rows8(p["ssm_conv_w"])
    fcw = _pad_rows8(p["ffn_conv_w"])
    dtb, alog = _lane_pad(p["ssm_dt_bias"]), _lane_pad(p["ssm_A_log"])
    dtbT, alogT = p["ssm_dt_bias"].reshape(SSM_H, 1), p["ssm_A_log"].reshape(SSM_H, 1)
    d_e = jnp.repeat(p["ssm_D"], SSM_P, axis=1)

    hn, lat, z, xbc, dtr = _inproj(h0, p["norm_mix_pre"], w_in_p)
    dtrT = dtr[:, :SSM_H].T
    q, k, v, qlat, kvlat = _mla_prep(lat, p["q_a_norm"], p["kv_a_norm"], wq_p, wkv_p, cos, sa, sb)
    o, lse, gathered = _attn_fwd(q, k, v, ride=late_bufs)
    if late_bufs:
        p = dict(p, **_late_weights(gathered))
    ssm, st = _ssd_fwd(xbc, z, dtr, dtrT, scw, p["ssm_conv_b"], dtb, dtbT, alog, alogT, d_e, p["ssm_norm"])
    mixin, mix, h1 = _mixout_fwd(o, ssm, h0, p["attn_out_norm"], p["norm_mix_post"], p["w_out"])
    hn2, u = _ffn_up(h1, p["norm_ffn_pre"], p["w_up"])
    uc, a = _ffn_gate(u, fcw, p["ffn_conv_b"])
    dh2, dd, g_ffn_post, loss = _ffn_down(a, p["w_down"], h1, tgt, p["norm_ffn_post"])

    g_w_down = _mm_tn(a, dd, "ffn_dw_down", tn=512)
    du, g_fcw, g_fcb = _ffn_gate_bwd(u, uc, dd, p["w_down"], fcw)
    dh1, g_ffn_pre = _ffn_in_bwd(du, p["w_up"], h1, p["norm_ffn_pre"], dh2)
    g_w_up = _mm_tn(hn2, du, "ffn_dw_up", tn=D_FF // 2, chunked=True)
    ffn_gs = early_reduce.halves(dict(w_up=g_w_up, w_down=g_w_down)) if early_reduce else ()
    dmix, dssm, do, g_mix_post, g_ao, delta, ffn_sib = _mixout_bwd(mix, p["norm_mix_post"], dh1, o, p["attn_out_norm"],
                                                                  p["w_out"], ride=ffn_gs)
    g_w_out = _mm_tn(mixin, dmix, "mix_dw_out", tn=512)
    t = _rt(lp, (640, 128))
    pairs = early_reduce.pairs(dict(w_out=g_w_out), ffn_gs, ffn_sib) if early_reduce else ()
    dq, dk, dv, got = _attn_bwd(q, k, v, do, lse[:, 0, :].reshape(MLA_H, lp // t, 1, t), delta.reshape(MLA_H, lp // t, 1, t),
                                ride=pairs)
    dqf, dkvf, dlat, g_qa, g_kva = _mla_bwd(dq, dk, dv, lat, p["q_a_norm"], p["kv_a_norm"], wq_p, wkv_p, cos, sa, sb)
    g_wq_p = _mm_tn(qlat, dqf, "mla_dw_uq")
    g_wkv_p = _mm_tn(kvlat, dkvf, "mla_dw_ukv")
    dz, dxbc, ddtr, g_scw, g_scb, g_dtb, g_alog, g_dd, g_ssm_norm = _ssd_bwd(
        dssm, xbc, z, dtr, dtrT, st, scw, p["ssm_conv_b"], dtb, dtbT, alog, alogT, d_e, p["ssm_norm"])
    g_in_p = jnp.concatenate([_mm_tn(hn, dlat, "in_dw_lat"), _mm_tn(hn, dz, "in_dw_z"), _mm_tn(hn, dxbc, "in_dw_xbc"),
                              _mm_tn(hn, ddtr, "in_dw_dt")], axis=1)
    g_w_in = jnp.concatenate([g_in_p[:, :QR + KVR + DR], g_in_p[:, LAT_W:LAT_W + D_SSM + D_XBC + SSM_H]], axis=1)
    g_w_uq = jnp.concatenate([g_wq_p[:, :D].reshape(QR, MLA_H, DN), g_wq_p[:, D:].reshape(QR, MLA_H, 128)[:, :, :DR]], axis=2)
    g_w_ukv = jnp.concatenate([g_wkv_p[:, :D].reshape(KVR, MLA_H, DN), g_wkv_p[:, D:].reshape(KVR, MLA_H, DV)], axis=2)
    pairs2 = last_reduce(dict(w_in=g_w_in, w_uq=g_w_uq, w_ukv=g_w_ukv)) if last_reduce else ()
    dh0, g_mix_pre, got2 = _inproj_bwd(dlat, dz, dxbc, ddtr, w_in_p, h0, p["norm_mix_pre"], dh1, ride=pairs2)
    grads = dict(
        norm_mix_pre=g_mix_pre, norm_mix_post=g_mix_post, norm_ffn_pre=g_ffn_pre, norm_ffn_post=g_ffn_post, w_in=g_w_in,
        q_a_norm=g_qa, w_uq=g_w_uq, kv_a_norm=g_kva, w_ukv=g_w_ukv, attn_out_norm=g_ao, ssm_conv_w=g_scw[:SSM_K],
        ssm_conv_b=g_scb, ssm_dt_bias=g_dtb[:, :SSM_H], ssm_A_log=g_alog[:, :SSM_H], ssm_D=g_dd[:, :SSM_H],
        ssm_norm=g_ssm_norm, w_out=g_w_out, w_up=g_w_up, ffn_conv_w=g_fcw[:FFN_K], ffn_conv_b=g_fcb, w_down=g_w_down)
    return loss, dh0[FRONT:], dh0[PAD_ROWS:FRONT], grads, (list(pairs2) + list(pairs), list(got2) + list(got))


N_CHIPS = 4
BIG = (("w_in", (D, D_IN // N_CHIPS)), ("w_uq", (QR // N_CHIPS, MLA_H, DN + DR)), ("w_ukv", (KVR // N_CHIPS, MLA_H, DN + DV)),
       ("w_out", (2 * D // N_CHIPS, D)), ("w_up", (D, 2 * D_FF // N_CHIPS)), ("w_down", (D_FF // N_CHIPS, D)))
SMALL_SHARDED = (("meta_tokens", (N_META, D // N_CHIPS)), ("ssm_conv_w", (SSM_K, D_XBC // N_CHIPS)),
                 ("ffn_conv_w", (FFN_K, 2 * D_FF // N_CHIPS)))
SMALL_REPL = (("norm_mix_pre", D), ("norm_mix_post", D), ("norm_ffn_pre", D), ("norm_ffn_post", D), ("q_a_norm", QR),
              ("kv_a_norm", KVR), ("attn_out_norm", D), ("ssm_conv_b", D_XBC), ("ssm_dt_bias", SSM_H), ("ssm_A_log", SSM_H),
              ("ssm_D", SSM_H), ("ssm_norm", D_SSM), ("ffn_conv_b", 2 * D_FF))
ANY = pl.BlockSpec(memory_space=pl.ANY)


def _pad128(v):
    n = v.shape[0]
    return jnp.concatenate([v, jnp.zeros(((-n) % 128,), v.dtype)]) if n % 128 else v


def _pack_rows(vs, rows):
    flat = jnp.concatenate([_pad128(v.reshape(-1)) for v in vs])
    flat = jnp.concatenate([flat, jnp.zeros((rows * 128 - flat.shape[0],), flat.dtype)])
    return flat.reshape(rows, 128)


def _unpack_rows(pack, sizes):
    flat = pack.reshape(-1)
    out, off = [], 0
    for n in sizes:
        out.append(flat[off:off + n])
        off += n + (-n) % 128
    return out


def _my_place():
    return lax.axis_index("x"), lax.axis_index("y"), lax.axis_index("c")


def _other_chips(x, y):
    return [(1 - x, y), (x, 1 - y), (1 - x, 1 - y)]


def _remote(src, dst, send, recv, dev):
    return pltpu.make_async_remote_copy(src_ref=src, dst_ref=dst, send_sem=send, recv_sem=recv, device_id=dev,
                                        device_id_type=MESH)


SMALL_AG_ROWS = 80


def _gather_weights(shards, small, name):
    arrs = list(shards) + ([] if small is None else [small])
    n, nb = len(arrs), len(shards)

    def body(*refs):
        ins, outs = refs[:n], refs[n:2 * n]
        send, recv, lsem = refs[2 * n:]
        x, y, c = _my_place()
        me = 2 * x + y
        chips = _other_chips(x, y)
        slot = lambda w, chip, cc: outs[w].at[chip, cc] if w < nb else outs[w].at[chip]
        mine = lambda w: slot(w, me, c) if w < nb else ins[w]
        loc = [pltpu.make_async_copy(ins[w], outs[w].at[me], lsem.at[w - nb]) for w in range(nb, n)]
        for cp in loc:
            cp.start()
        sends = []
        for w in range(n):
            for kk, (cx, cy) in enumerate(chips):
                sends.append(_remote(mine(w), slot(w, me, c), send.at[3 * w + kk], recv.at[3 * w + kk], (cx, cy, c)))
        for cp in sends:
            cp.start()
        for w in range(nb):
            for kk, (cx, cy) in enumerate(chips):
                src = 2 * cx + cy
                _remote(mine(w), slot(w, src, c), send.at[3 * w + kk], recv.at[3 * w + kk], (cx, cy, c)).wait_recv()
                fwd = _remote(slot(w, src, c), slot(w, src, c), send.at[3 * (n + w) + kk], recv.at[3 * (n + w) + kk], (x, y, 1 - c))
                fwd.start()
                sends.append(fwd)
        for w in range(n):
            for kk, (cx, cy) in enumerate(chips):
                src = 2 * cx + cy
                if w < nb:
                    _remote(mine(w), slot(w, src, 1 - c), send.at[3 * (n + w) + kk], recv.at[3 * (n + w) + kk],
                            (x, y, 1 - c)).wait_recv()
                else:
                    _remote(ins[w], slot(w, src, c), send.at[3 * w + kk], recv.at[3 * w + kk], (cx, cy, c)).wait_recv()
        for cp in sends:
            cp.wait_send()
        for cp in loc:
            cp.wait()

    return pl.pallas_call(
        body, name=name, in_specs=[ANY] * n, out_specs=[ANY] * n,
        out_shape=[_sds(a.shape, a.dtype) for a in shards] + ([] if small is None else [_sds((N_CHIPS,) + small.shape, small.dtype)]),
        input_output_aliases={w: w for w in range(nb)},
        scratch_shapes=[pltpu.SemaphoreType.DMA((3 * (n + nb),)), pltpu.SemaphoreType.DMA((3 * (n + nb),)),
                        pltpu.SemaphoreType.DMA((max(n - nb, 1),))])(*arrs)


def _place_own(wt, chip, name):
    r, c = wt.shape
    tr = _row_tile(r, c)

    def body(c_ref, w_ref, o_ref):
        o_ref[...] = w_ref[...].astype(BF16)

    return pl.pallas_call(
        body, name=name, out_shape=_sds((N_CHIPS, r, c), BF16),
        grid_spec=pltpu.PrefetchScalarGridSpec(
            num_scalar_prefetch=1, grid=(r // tr,), in_specs=[pl.BlockSpec((tr, c), lambda i, cr: (i, 0))],
            out_specs=pl.BlockSpec((None, tr, c), lambda i, cr: (cr[0], i, 0))),
        compiler_params=_cp("parallel"))(chip, wt)


def _send_sibling_halves(gs, name):
    n = len(gs)

    def body(*refs):
        ins, outs, send, recv = refs[:n], refs[n:2 * n], refs[2 * n], refs[2 * n + 1]
        x, y, c = _my_place()
        cps = [_remote(ins[w].at[:, 1 - c], outs[w], send.at[w], recv.at[w], (x, y, 1 - c)) for w in range(n)]
        for cp in cps:
            cp.start()
        for cp in cps:
            cp.wait()

    return pl.pallas_call(
        body, name=name, in_specs=[ANY] * n, out_specs=[ANY] * n,
        out_shape=[_sds((g.shape[0],) + g.shape[2:], g.dtype) for g in gs],
        scratch_shapes=[pltpu.SemaphoreType.DMA((n,)), pltpu.SemaphoreType.DMA((n,))])(*gs)


def _ride_gather(bufs, send, recv, phase):
    n = len(bufs)
    x, y, c = _my_place()
    me = 2 * x + y
    for w in range(n):
        for kk, (cx, cy) in enumerate(_other_chips(x, y)):
            src = 2 * cx + cy
            out = lambda: _remote(bufs[w].at[me, c], bufs[w].at[me, c], send.at[3 * w + kk], recv.at[3 * w + kk], (cx, cy, c))
            fwd = lambda: _remote(bufs[w].at[src, c], bufs[w].at[src, c], send.at[3 * (n + w) + kk],
                                  recv.at[3 * (n + w) + kk], (x, y, 1 - c))
            if phase == 0:
                out().start()
            elif phase == 1:
                _remote(bufs[w].at[me, c], bufs[w].at[src, c], send.at[3 * w + kk], recv.at[3 * w + kk], (cx, cy, c)).wait_recv()
                fwd().start()
            else:
                _remote(bufs[w].at[me, c], bufs[w].at[src, 1 - c], send.at[3 * (n + w) + kk], recv.at[3 * (n + w) + kk],
                        (x, y, 1 - c)).wait_recv()
                out().wait_send()
                fwd().wait_send()


def _ride_sibling(gs, outs, send, recv, phase):
    x, y, c = _my_place()
    for w in range(len(gs)):
        cp = _remote(gs[w].at[:, 1 - c], outs[w], send.at[w], recv.at[w], (x, y, 1 - c))
        if phase == 0:
            cp.start()
        else:
            cp.wait()


def _ride_exchange(ps, outs, send, recv, phase):
    x, y, c = _my_place()
    for w in range(len(ps)):
        for kk, (cx, cy) in enumerate(_other_chips(x, y)):
            cp = _remote(ps[w].at[2 * cx + cy], outs[w].at[kk], send.at[3 * w + kk], recv.at[3 * w + kk], (cx, cy, c))
            if phase == 0:
                cp.start()
            else:
                cp.wait()


def _share_sibling(halves):
    n = len(halves)

    def body(*refs):
        outs, send, recv = refs[n:2 * n], refs[2 * n], refs[2 * n + 1]
        x, y, c = _my_place()
        cps = [_remote(outs[w].at[c], outs[w].at[c], send.at[w], recv.at[w], (x, y, 1 - c)) for w in range(n)]
        for cp in cps:
            cp.start()
        for w in range(n):
            cps[w].wait_send()
            _remote(outs[w].at[c], outs[w].at[1 - c], send.at[w], recv.at[w], (x, y, 1 - c)).wait_recv()

    return pl.pallas_call(
        body, name="share_sibling", in_specs=[ANY] * n, out_specs=[ANY] * n,
        out_shape=[_sds(h.shape, h.dtype) for h in halves], input_output_aliases={w: w for w in range(n)},
        scratch_shapes=[pltpu.SemaphoreType.DMA((n,)), pltpu.SemaphoreType.DMA((n,))])(*halves)


def _row_tile(r, c, cap=1 << 20):
    return next(t for t in range(r, 0, -1) if r % t == 0 and (t % 8 == 0 or t == r) and t * c * 4 <= cap)


def _add_pair(g, t, core, name):
    _, _, r, c = g.shape
    tr = _row_tile(r, c)

    def body(c_ref, g_ref, t_ref, o_ref):
        o_ref[...] = (g_ref[...] + t_ref[...]).astype(BF16)

    return pl.pallas_call(
        body, name=name, out_shape=_sds(t.shape, BF16),
        grid_spec=pltpu.PrefetchScalarGridSpec(
            num_scalar_prefetch=1, grid=(N_CHIPS, r // tr),
            in_specs=[pl.BlockSpec((None, None, tr, c), lambda j, i, cr: (j, cr[0], i, 0)),
                      pl.BlockSpec((None, tr, c), lambda j, i, cr: (j, i, 0))],
            out_specs=pl.BlockSpec((None, tr, c), lambda j, i, cr: (j, i, 0))),
        compiler_params=_cp("parallel", "parallel"))(core, g, t)


def _add_chips(p, got, chip, name):
    _, r, c = p.shape
    tr = _row_tile(r, c)

    def body(c_ref, p_ref, g_ref, o_ref):
        o_ref[...] = ((p_ref[...].astype(F32) + g_ref[0].astype(F32)) + g_ref[1].astype(F32)) + g_ref[2].astype(F32)

    return pl.pallas_call(
        body, name=name, out_shape=_sds((2, r, c), F32),
        grid_spec=pltpu.PrefetchScalarGridSpec(
            num_scalar_prefetch=1, grid=(r // tr,),
            in_specs=[pl.BlockSpec((None, tr, c), lambda i, cr: (cr[0], i, 0)), pl.BlockSpec((3, tr, c), lambda i, cr: (0, i, 0))],
            out_specs=pl.BlockSpec((None, tr, c), lambda i, cr: (cr[1], i, 0))),
        compiler_params=_cp("parallel"))(chip, p, got)


SMALL_AR_ROWS = 424


def _allreduce_small(v):
    def body(v_ref, o_ref, gath, send, recv):
        x, y, c = _my_place()
        me = 4 * x + 2 * y + c
        gath[me] = v_ref[...]
        cps = []
        for dd in range(1, 8):
            dx, dy, dc = dd >> 2, (dd >> 1) & 1, dd & 1
            peer = (1 - x if dx else x, 1 - y if dy else y, 1 - c if dc else c)
            cps.append(_remote(v_ref, gath.at[me], send.at[dd - 1], recv.at[dd - 1], peer))
        for cp in cps:
            cp.start()
        for cp in cps:
            cp.wait()
        acc = gath[0]
        for dev in range(1, 8):
            acc = acc + gath[dev]
        o_ref[...] = acc

    vm = pl.BlockSpec(memory_space=pltpu.VMEM)
    return pl.pallas_call(
        body, name="allreduce_small", in_specs=[vm], out_specs=vm, out_shape=_sds(v.shape, F32),
        scratch_shapes=[pltpu.VMEM((8,) + v.shape, F32), pltpu.SemaphoreType.DMA((7,)), pltpu.SemaphoreType.DMA((7,))])(v)


def _adamw(w, g, m, v, name):
    r, c = w.shape
    tr = _row_tile(r, c)

    def body(w_ref, g_ref, m_ref, v_ref, d_ref, m2_ref, v2_ref):
        gv = g_ref[...]
        m2 = ADAM_B1 * m_ref[...] + (1.0 - ADAM_B1) * gv
        v2 = ADAM_B2 * v_ref[...] + (1.0 - ADAM_B2) * jnp.square(gv)
        m_hat = m2 / (1.0 - ADAM_B1 ** ADAM_STEP)
        v_hat = v2 / (1.0 - ADAM_B2 ** ADAM_STEP)
        d_ref[...] = -ADAM_LR * (m_hat / (jnp.sqrt(v_hat) + ADAM_EPS) + ADAM_WD * w_ref[...])
        m2_ref[...] = m2
        v2_ref[...] = v2

    return pl.pallas_call(
        body, name=name, grid=(r // tr,), in_specs=[_rows(tr, c)] * 4, out_specs=[_rows(tr, c)] * 3,
        out_shape=[_sds((r, c), F32)] * 3, compiler_params=_cp("parallel"))(w, g, m, v)


WEIGHT_NAMES = ("meta_tokens", "norm_mix_pre", "norm_mix_post", "norm_ffn_pre", "norm_ffn_post", "w_in", "q_a_norm", "w_uq",
                "kv_a_norm", "w_ukv", "attn_out_norm", "ssm_conv_w", "ssm_conv_b", "ssm_dt_bias", "ssm_A_log", "ssm_D",
                "ssm_norm", "w_out", "w_up", "ffn_conv_w", "ffn_conv_b", "w_down")
SMALL_ADAM_ROWS = 192


def kernel(x, meta_tokens, norm_mix_pre, norm_mix_post, norm_ffn_pre, norm_ffn_post, w_in, q_a_norm, w_uq, kv_a_norm, w_ukv, attn_out_norm, ssm_conv_w, ssm_conv_b, ssm_dt_bias, ssm_A_log, ssm_D, ssm_norm, w_out, w_up, ffn_conv_w, ffn_conv_b, w_down, loss_target, m_meta_tokens, m_norm_mix_pre, m_norm_mix_post, m_norm_ffn_pre, m_norm_ffn_post, m_w_in, m_q_a_norm, m_w_uq, m_kv_a_norm, m_w_ukv, m_attn_out_norm, m_ssm_conv_w, m_ssm_conv_b, m_ssm_dt_bias, m_ssm_A_log, m_ssm_D, m_ssm_norm, m_w_out, m_w_up, m_ffn_conv_w, m_ffn_conv_b, m_w_down, v_meta_tokens, v_norm_mix_pre, v_norm_mix_post, v_norm_ffn_pre, v_norm_ffn_post, v_w_in, v_q_a_norm, v_w_uq, v_kv_a_norm, v_w_ukv, v_attn_out_norm, v_ssm_conv_w, v_ssm_conv_b, v_ssm_dt_bias, v_ssm_A_log, v_ssm_D, v_ssm_norm, v_w_out, v_w_up, v_ffn_conv_w, v_ffn_conv_b, v_w_down):
    args = locals()
    w = {n: args[n] for n in WEIGHT_NAMES}
    mom = {n: args["m_" + n] for n in WEIGHT_NAMES}
    var = {n: args["v_" + n] for n in WEIGHT_NAMES}
    cx, cy, cc = _my_place()
    chip = 2 * cx + cy

    two_d = {n: (shp[0], functools.reduce(lambda a, b: a * b, shp[1:])) for n, shp in BIG}
    names = [n for n, _ in BIG]
    core_i = cc.astype(jnp.int32).reshape(1)
    chip_i = chip.astype(jnp.int32).reshape(1)
    early, late = names[:3], names[3:]
    halves = lambda n, a: a.reshape(N_CHIPS, 2, two_d[n][0] // 2, two_d[n][1])
    bufs = {n: halves(n, _place_own(w[n].reshape(two_d[n]), chip_i, "place_" + n)) for n in names}
    small = _pack_rows([w[n] for n, _ in SMALL_SHARDED], SMALL_AG_ROWS)
    *gathered, small_all = _gather_weights([bufs[n] for n in early], small, "allgather_weights")
    gath = {n: a.reshape((N_CHIPS,) + two_d[n]) for n, a in zip(early, gathered)}
    p = dict(w_in=gath["w_in"].transpose(1, 0, 2).reshape(D, D_IN), w_uq=gath["w_uq"].reshape(QR, MLA_H, DN + DR),
             w_ukv=gath["w_ukv"].reshape(KVR, MLA_H, DN + DV))
    sm_parts = [_unpack_rows(small_all[j], [a * b for _, (a, b) in SMALL_SHARDED]) for j in range(N_CHIPS)]
    for i, (n, shp) in enumerate(SMALL_SHARDED):
        p[n] = jnp.concatenate([sm_parts[j][i].reshape(shp) for j in range(N_CHIPS)], axis=1)
    for n, _ in SMALL_REPL:
        p[n] = w[n]
    meta_full = p.pop("meta_tokens")

    place_i = jnp.stack([chip, cc]).astype(jnp.int32)

    def pair_sums(gd, group):
        gd = dict(gd)
        if "w_in" in gd:
            gd["w_in"] = gd["w_in"].reshape(D, N_CHIPS, D_IN // N_CHIPS).transpose(1, 0, 2)
        gs = [halves(n, gd[n]) for n in group]
        from_sib = _send_sibling_halves(gs, "reduce_sibling_" + group[0])
        return [_add_pair(gg, tt, core_i, "reduce_pair_" + n) for n, gg, tt in zip(group, gs, from_sib)]

    class LateReduce:
        @staticmethod
        def halves(gd):
            return [halves(n, gd[n]) for n in late[1:]]

        @staticmethod
        def pairs(gd, ffn_gs, ffn_sib):
            gs = [halves(late[0], gd[late[0]])]
            sib = _send_sibling_halves(gs, "reduce_sibling_" + late[0])
            return [_add_pair(gg, tt, core_i, "reduce_pair_" + n)
                    for n, gg, tt in zip(late, gs + list(ffn_gs), list(sib) + list(ffn_sib))]

    loss_part, gx, gmeta, g, (pairs, got) = _device_step(
        x[0], loss_target[0], meta_full, p, late_bufs=[bufs[n] for n in late], early_reduce=LateReduce,
        last_reduce=lambda gd: pair_sums(gd, early))

    small_names = [n for n, _ in SMALL_REPL] + ["ssm_conv_w", "ffn_conv_w"]
    small_sizes = [128] + [sz for _, sz in SMALL_REPL] + [N_META * D, SSM_K * D_XBC, FFN_K * 2 * D_FF]
    order = [n for n, _ in SMALL_REPL]
    sp = _pack_rows([loss_part[0]] + [g[n] for n in order] + [gmeta, g["ssm_conv_w"], g["ffn_conv_w"]], SMALL_AR_ROWS)
    red = _unpack_rows(_allreduce_small(sp), small_sizes)
    loss = red[0][0]
    gfull = {n: red[1 + i].reshape(1, -1) for i, n in enumerate(order)}
    n_r = len(order)
    gfull["meta_tokens"] = lax.dynamic_slice_in_dim(red[1 + n_r].reshape(N_META, D), chip * (D // N_CHIPS), D // N_CHIPS, axis=1)
    gfull["ssm_conv_w"] = lax.dynamic_slice_in_dim(red[2 + n_r].reshape(SSM_K, D_XBC), chip * (D_XBC // N_CHIPS),
                                                   D_XBC // N_CHIPS, axis=1)[None]
    gfull["ffn_conv_w"] = lax.dynamic_slice_in_dim(red[3 + n_r].reshape(FFN_K, 2 * D_FF), chip * (2 * D_FF // N_CHIPS),
                                                   2 * D_FF // N_CHIPS, axis=1)[None]

    mine = [_add_chips(pp, gg, place_i, "reduce_chips_" + n) for n, pp, gg in zip(names, pairs, got)]
    for n, both in zip(names, _share_sibling(mine)):
        gfull[n] = both.reshape(two_d[n])

    delta, new_m, new_v = {}, {}, {}
    for n, shp in BIG:
        outs = _adamw(w[n].reshape(two_d[n]), gfull[n], mom[n].reshape(two_d[n]), var[n].reshape(two_d[n]), "adamw_" + n)
        delta[n], new_m[n], new_v[n] = (o.reshape((1,) + shp) for o in outs)
    snames = order + ["meta_tokens", "ssm_conv_w", "ffn_conv_w"]
    ssizes = [functools.reduce(lambda a, b: a * b, w[n].shape) for n in snames]
    packs = [_pack_rows([d[n] for n in snames], SMALL_ADAM_ROWS) for d in (w, gfull, mom, var)]
    outs = _adamw(*packs, "adamw_small")
    for d, o in zip((delta, new_m, new_v), outs):
        for n, piece in zip(snames, _unpack_rows(o, ssizes)):
            d[n] = piece.reshape(w[n].shape)
    gout = {n: gfull[n].reshape(w[n].shape) for n in WEIGHT_NAMES}
    return (loss, gx[None], *[gout[n] for n in WEIGHT_NAMES], *[delta[n] for n in WEIGHT_NAMES],
            *[new_m[n] for n in WEIGHT_NAMES], *[new_v[n] for n in WEIGHT_NAMES])
```

```python
import functools

import jax
import jax.numpy as jnp
from jax import lax
from jax.experimental import pallas as pl
from jax.experimental.pallas import tpu as pltpu

F32 = jnp.float32
BF16 = jnp.bfloat16

D = 1024
N_META = 16
FRONT = 128
PAD_ROWS = FRONT - N_META
MLA_H = 8
DN, DR, DV = 128, 64, 128
QR, KVR = 384, 256
SOFTMAX_SCALE = (DN + DR) ** -0.5
ROPE_THETA = 10000.0
SSM_H, SSM_P, SSM_G, SSM_N, SSM_K = 16, 64, 2, 128, 4
CHUNK = 128
D_SSM = SSM_H * SSM_P
D_XBC = D_SSM + 2 * SSM_G * SSM_N
GSZ = D_SSM // SSM_G
D_FF = 2816
FFN_K = 3
EPS = 1e-6
IN_SPLITS = (QR, KVR, DR, D_SSM, D_XBC, SSM_H)
D_IN = sum(IN_SPLITS)
LAT_W = 768
IN_P = LAT_W + D_SSM + D_XBC + 128
NEG = -1e30
LOG2E = 1.4426950408889634
LN2 = 0.6931471805599453
Q_SCALE = SOFTMAX_SCALE * LOG2E

ADAM_LR, ADAM_B1, ADAM_B2, ADAM_EPS, ADAM_WD, ADAM_STEP = 0.001, 0.9, 0.999, 1e-08, 0.01, 10

VMEM_LIMIT = 56 * 1024 * 1024
MM_ROWS = (640, 320, 128)
MESH = pl.DeviceIdType.MESH


def _sds(shape, dtype):
    return jax.ShapeDtypeStruct(shape, dtype)


def _cp(*sem):
    return pltpu.CompilerParams(dimension_semantics=sem, vmem_limit_bytes=VMEM_LIMIT)


def _rt(n, cands):
    for c in cands:
        if n % c == 0:
            return c
    raise ValueError((n, cands))


def _full(shape):
    nd = len(shape)
    return pl.BlockSpec(shape, lambda *_: (0,) * nd)


def _rows(tr, c):
    return pl.BlockSpec((tr, c), lambda i: (i, 0))


def _sigmoid(x):
    return 1.0 / (1.0 + jnp.exp(-x))


def _silu(x):
    return x * _sigmoid(x)


def _dsilu(x):
    s = _sigmoid(x)
    return s * (1.0 + x * (1.0 - s))


def _softplus(x):
    return jnp.maximum(x, 0.0) + jnp.log(1.0 + jnp.exp(-jnp.abs(x)))


def _rms(x, g):
    r = lax.rsqrt(jnp.mean(x * x, axis=-1, keepdims=True) + EPS)
    return x * r * g


def _rms_bwd(x, g, dy):
    r = lax.rsqrt(jnp.mean(x * x, axis=-1, keepdims=True) + EPS)
    xh = x * r
    dxh = dy * g
    dx = r * (dxh - xh * jnp.mean(dxh * xh, axis=-1, keepdims=True))
    return dx, jnp.sum(dy * xh, axis=0, keepdims=True)


def _dot(a, b):
    return jnp.dot(a, b, preferred_element_type=F32)


def _dot_nt(a, b):
    return lax.dot_general(a, b, (((1,), (1,)), ((), ())), preferred_element_type=F32)


def _dot_tn(a, b):
    return lax.dot_general(a, b, (((0,), (0,)), ((), ())), preferred_element_type=F32)


def _split3(x):
    hi = x.astype(BF16)
    r = x - hi.astype(F32)
    mid = r.astype(BF16)
    return hi, mid, (r - mid.astype(F32)).astype(BF16)


def _dot_hi(a, b, split="a"):
    if split == "a":
        bb = b.astype(BF16)
        return sum(_dot(t, bb) for t in _split3(a))
    ab = a.astype(BF16)
    return sum(_dot(ab, t) for t in _split3(b))


def _dot_nt_hi(a, b):
    bb = b.astype(BF16)
    return sum(_dot_nt(t, bb) for t in _split3(a))


def _shift_down(x, halo, j):
    xr = pltpu.roll(x, j, axis=0)
    hr = pltpu.roll(halo, j, axis=0)
    row = lax.broadcasted_iota(jnp.int32, (8, x.shape[1]), 0)
    first = jnp.where(row < j, hr, xr[:8])
    return jnp.concatenate([first, xr[8:]], axis=0)


def _shift_up(x, nxt, j):
    t = x.shape[0]
    xr = pltpu.roll(x, t - j, axis=0)
    nr = pltpu.roll(nxt, 8 - j, axis=0)
    row = lax.broadcasted_iota(jnp.int32, (8, x.shape[1]), 0)
    last = jnp.where(row + j >= 8, nr, xr[t - 8:])
    return jnp.concatenate([xr[:t - 8], last], axis=0)


def _acc_rows(ref, val, first):
    @pl.when(first)
    def _():
        ref[...] = val

    @pl.when(jnp.logical_not(first))
    def _():
        ref[...] += val


def _mm_tn(a, b, name, tn=None, trs=(1664, 640, 128), chunked=False):
    r, m = a.shape
    n = b.shape[1]
    tn = n if tn is None else tn
    tr = _rt(r, trs)

    def body(a_ref, b_ref, o_ref):
        part = _dot_tn(a_ref[...].astype(BF16), b_ref[...].astype(BF16))
        _acc_rows(o_ref, part, pl.program_id(1) == 0)

    if chunked:
        out_specs, out_shape = pl.BlockSpec((None, m, tn), lambda j, i: (j, 0, 0)), _sds((n // tn, m, tn), F32)
    else:
        out_specs, out_shape = pl.BlockSpec((m, tn), lambda j, i: (0, j)), _sds((m, n), F32)
    return pl.pallas_call(
        body, name=name, grid=(n // tn, r // tr),
        in_specs=[pl.BlockSpec((tr, m), lambda j, i: (i, 0)), pl.BlockSpec((tr, tn), lambda j, i: (i, j))],
        out_specs=out_specs, out_shape=out_shape, compiler_params=_cp("parallel", "arbitrary"))(a, b)


def _inproj(h0, g, w, ride=()):
    lp = h0.shape[0]
    tr = _rt(lp, MM_ROWS)
    segs = ((0, LAT_W), (LAT_W, LAT_W + D_SSM), (LAT_W + D_SSM, LAT_W + D_SSM + D_XBC), (IN_P - 128, IN_P))
    nr = len(ride)
    steps = lp // tr

    def body(h_ref, g_ref, w_ref, *rest):
        hn_ref, lat_ref, z_ref, xbc_ref, dt_ref = rest[nr:nr + 5]
        bufs, sems = rest[nr + 5:2 * nr + 5], rest[2 * nr + 5:]
        step = pl.program_id(0)
        if nr:
            pl.when(step == 0)(lambda: _ride_gather(bufs, *sems, 0))
            pl.when(step == steps // 2)(lambda: _ride_gather(bufs, *sems, 1))
        hn = _rms(h_ref[...], g_ref[...]).astype(BF16)
        hn_ref[...] = hn
        for ref, (a, b) in zip((lat_ref, z_ref, xbc_ref, dt_ref), segs):
            ref[...] = _dot(hn, w_ref[:, a:b])
        if nr:
            pl.when(step == steps - 1)(lambda: _ride_gather(bufs, *sems, 2))

    sems = [pltpu.SemaphoreType.DMA((6 * nr,)), pltpu.SemaphoreType.DMA((6 * nr,))] if nr else []
    outs = pl.pallas_call(
        body, name="inproj", grid=(steps,), in_specs=[_rows(tr, D), _full((1, D)), _full(w.shape)] + [ANY] * nr,
        out_specs=[_rows(tr, D), _rows(tr, LAT_W), _rows(tr, D_SSM), _rows(tr, D_XBC), _rows(tr, 128)] + [ANY] * nr,
        out_shape=[_sds((lp, D), BF16), _sds((lp, LAT_W), F32), _sds((lp, D_SSM), F32), _sds((lp, D_XBC), F32),
                   _sds((lp, 128), F32)] + [_sds(b.shape, b.dtype) for b in ride],
        input_output_aliases={3 + w_: 5 + w_ for w_ in range(nr)}, scratch_shapes=sems,
        compiler_params=_cp("arbitrary"))(h0, g, w, *ride)
    return tuple(outs[:5]) + (list(outs[5:]),)


def _rope(x, cos, sa, sb):
    return x * cos + pltpu.roll(x, 96, axis=1) * sa + pltpu.roll(x, 32, axis=1) * sb


def _rope_t(g, cos, sa, sb):
    return g * cos + pltpu.roll(g * sa, 32, axis=1) + pltpu.roll(g * sb, 96, axis=1)


def _mla_prep(lat, qg, kvg, wq, wkv, cos, sa, sb):
    lp = lat.shape[0]
    tr = _rt(lp, MM_ROWS)

    def body(lat_ref, qg_ref, kvg_ref, wq_ref, wkv_ref, cos_ref, sa_ref, sb_ref, q_ref, k_ref, v_ref, ql_ref, kl_ref):
        lat_v = lat_ref[...]
        ql = _rms(lat_v[:, :QR], qg_ref[...]).astype(BF16)
        kl = _rms(lat_v[:, QR:QR + KVR], kvg_ref[...]).astype(BF16)
        ql_ref[...] = ql
        kl_ref[...] = kl
        cos_v, sa_v, sb_v = cos_ref[...], sa_ref[...], sb_ref[...]
        kpe = _rope(lat_v[:, QR + KVR:LAT_W], cos_v, sa_v, sb_v).astype(BF16)
        for h in range(MLA_H):
            q_ref[h, :, 0:DN] = (_dot(ql, wq_ref[:, h * DN:(h + 1) * DN]) * Q_SCALE).astype(BF16)
            qpe = _dot(ql, wq_ref[:, D + h * 128:D + (h + 1) * 128])
            q_ref[h, :, DN:2 * DN] = (_rope(qpe, cos_v, sa_v, sb_v) * Q_SCALE).astype(BF16)
            k_ref[h, :, 0:DN] = _dot(kl, wkv_ref[:, h * DN:(h + 1) * DN]).astype(BF16)
            k_ref[h, :, DN:2 * DN] = kpe
            v_ref[h] = _dot(kl, wkv_ref[:, D + h * DV:D + (h + 1) * DV]).astype(BF16)

    hb = lambda w: pl.BlockSpec((MLA_H, tr, w), lambda i: (0, i, 0))
    return pl.pallas_call(
        body, name="mla_prep", grid=(lp // tr,),
        in_specs=[_rows(tr, LAT_W), _full((1, QR)), _full((1, KVR)), _full(wq.shape), _full(wkv.shape),
                  _rows(tr, 128), _rows(tr, 128), _rows(tr, 128)],
        out_specs=[hb(256), hb(256), hb(128), _rows(tr, QR), _rows(tr, KVR)],
        out_shape=[_sds((MLA_H, lp, 256), BF16), _sds((MLA_H, lp, 256), BF16), _sds((MLA_H, lp, 128), BF16),
                   _sds((lp, QR), BF16), _sds((lp, KVR), BF16)],
        compiler_params=_cp("parallel"))(lat, qg, kvg, wq, wkv, cos, sa, sb)


def _attn_mask(r0, c0, tq, tk, transposed=False):
    if transposed:
        kk = c0 + lax.broadcasted_iota(jnp.int32, (tk, tq), 0)
        qq = r0 + lax.broadcasted_iota(jnp.int32, (tk, tq), 1)
    else:
        qq = r0 + lax.broadcasted_iota(jnp.int32, (tq, tk), 0)
        kk = c0 + lax.broadcasted_iota(jnp.int32, (tq, tk), 1)
    return jnp.logical_and(kk <= qq, kk >= PAD_ROWS)


def _attn_fwd(q, k, v, ride=()):
    lp = q.shape[1]
    t = _rt(lp, (640, 128))
    nq = lp // t

    hp = 2
    KW = (4, 3, 2, 1)

    nr = len(ride)
    steps = (MLA_H // hp) * nq

    def body(q_ref, k_ref, v_ref, *rest):
        o_ref, lse_ref = rest[nr:nr + 2]
        bufs, sems = rest[nr + 2:2 * nr + 2], rest[2 * nr + 2:]
        qi = pl.program_id(1)
        step = pl.program_id(0) * nq + qi
        if nr:
            pl.when(step == 0)(lambda: _ride_gather(bufs, *sems, 0))
            pl.when(step == steps // 2)(lambda: _ride_gather(bufs, *sems, 1))
        qv = [q_ref[a] for a in range(hp)]

        def tile(kj, carries, bias=None, width=1):
            starts = kj if isinstance(kj, tuple) else (kj,)
            kv_rows = [pl.ds(pl.multiple_of(k0 * t, t), width * t) for k0 in starts]
            out = []
            for a in range(hp):
                m, l, acc = carries[a]
                kk = jnp.concatenate([k_ref[a, r, :] for r in kv_rows], axis=0)
                vv = jnp.concatenate([v_ref[a, r, :] for r in kv_rows], axis=0)
                s = _dot_nt(qv[a], kk)
                if bias is not None:
                    s = s + bias
                m_new = jnp.maximum(m, jnp.max(s, axis=-1, keepdims=True))
                alpha = jnp.exp2(m - m_new)
                p = jnp.exp2(s - m_new)
                l = alpha * l + jnp.sum(p, axis=-1, keepdims=True)
                acc = alpha * acc + _dot(p.astype(BF16), vv)
                out.append((m_new, l, acc))
            return tuple(out)

        key = lax.broadcasted_iota(jnp.int32, (t, t), 1)
        pad_bias = jnp.where(jnp.logical_and(key >= PAD_ROWS, qi > 0), 0.0, NEG)
        diag_bias = jnp.where(_attn_mask(qi * t, qi * t, t, t), 0.0, NEG)
        init = tuple((jnp.full((t, 1), NEG, F32), jnp.zeros((t, 1), F32), jnp.zeros((t, DV), F32)) for _ in range(hp))
        carries = tile((0, qi), init, jnp.concatenate([pad_bias, diag_bias], axis=1))
        nxt = 1
        for width in KW:
            reps = jnp.maximum(qi - nxt, 0) // width
            carries = lax.fori_loop(0, reps, functools.partial(lambda j, c, nxt, width: tile(nxt + width * j, c, width=width),
                                                               nxt=nxt, width=width), carries)
            nxt = nxt + width * reps
        for a in range(hp):
            m, l, acc = carries[a]
            o_ref[:, a * DV:(a + 1) * DV] = acc / l
            lse_ref[a] = jnp.broadcast_to(m + jnp.log(l) * LOG2E, (t, 128)).T[:8]
        if nr:
            pl.when(step == steps - 1)(lambda: _ride_gather(bufs, *sems, 2))

    sems = [pltpu.SemaphoreType.DMA((6 * nr,)), pltpu.SemaphoreType.DMA((6 * nr,))] if nr else []
    outs = pl.pallas_call(
        body, name="attn_fwd", grid=(MLA_H // hp, nq),
        in_specs=[pl.BlockSpec((hp, t, 256), lambda h, i: (h, i, 0)), pl.BlockSpec((hp, lp, 256), lambda h, i: (h, 0, 0)),
                  pl.BlockSpec((hp, lp, 128), lambda h, i: (h, 0, 0))] + [ANY] * nr,
        out_specs=[pl.BlockSpec((t, hp * DV), lambda h, i: (i, h)), pl.BlockSpec((hp, 8, t), lambda h, i: (h, 0, i))] + [ANY] * nr,
        out_shape=[_sds((lp, MLA_H * DV), F32), _sds((MLA_H, 8, lp), F32)] + [_sds(b.shape, b.dtype) for b in ride],
        input_output_aliases={3 + w: 2 + w for w in range(nr)}, scratch_shapes=sems,
        compiler_params=_cp("arbitrary", "arbitrary"))(q, k, v, *ride)
    return outs[0], outs[1], list(outs[2:])


def _mixout_bwd(mix, g_post, dh1, o, g_ao, w_out, ride=()):
    lp = o.shape[0]
    tr = _rt(lp, MM_ROWS)
    nr = len(ride)
    steps = lp // tr

    def body(mix_ref, gp_ref, dh_ref, o_ref, g_ref, w_ref, *rest):
        gs = rest[:nr]
        dmix_ref, dssm_ref, do_ref, dgp_ref, dg_ref, dl_ref = rest[nr:nr + 6]
        from_sib, sems = rest[nr + 6:2 * nr + 6], rest[2 * nr + 6:]
        i = pl.program_id(0)
        if nr:
            pl.when(i == 0)(lambda: _ride_sibling(gs, from_sib, *sems, 0))
        grow = i * tr + lax.broadcasted_iota(jnp.int32, (tr, D), 0)
        dmix, dgp = _rms_bwd(mix_ref[...], gp_ref[...], jnp.where(grow >= PAD_ROWS, dh_ref[...], 0.0))
        dmix = dmix.astype(BF16)
        dmix_ref[...] = dmix
        _acc_rows(dgp_ref, dgp, i == 0)
        dssm_ref[...] = _dot_nt(dmix, w_ref[D:, :])
        ov = o_ref[...]
        do, dg = _rms_bwd(ov, g_ref[...], _dot_nt(dmix, w_ref[:D, :]))
        do_ref[...] = do
        _acc_rows(dg_ref, dg, i == 0)
        prod = do * ov
        lane = lax.broadcasted_iota(jnp.int32, (1, 128), 1)
        cols = jnp.zeros((tr, 128), F32)
        for h in range(MLA_H):
            cols = cols + jnp.sum(prod[:, h * DV:(h + 1) * DV], axis=-1, keepdims=True) * (lane == h).astype(F32)
        dl_ref[...] = cols.T[:MLA_H]
        if nr:
            pl.when(i == steps - 1)(lambda: _ride_sibling(gs, from_sib, *sems, 1))

    sems = [pltpu.SemaphoreType.DMA((nr,)), pltpu.SemaphoreType.DMA((nr,))] if nr else []
    outs = pl.pallas_call(
        body, name="mixout_bwd", grid=(steps,),
        in_specs=[_rows(tr, D), _full((1, D)), _rows(tr, D), _rows(tr, D), _full((1, D)), _full(w_out.shape)] + [ANY] * nr,
        out_specs=[_rows(tr, D), _rows(tr, D), _rows(tr, D), _full((1, D)), _full((1, D)),
                   pl.BlockSpec((MLA_H, tr), lambda i: (0, i))] + [ANY] * nr,
        out_shape=[_sds((lp, D), BF16), _sds((lp, D), F32), _sds((lp, D), F32), _sds((1, D), F32), _sds((1, D), F32),
                   _sds((MLA_H, lp), F32)] + [_sds((g.shape[0],) + g.shape[2:], g.dtype) for g in ride],
        scratch_shapes=sems, compiler_params=_cp("arbitrary"))(mix, g_post, dh1, o, g_ao, w_out, *ride)
    return tuple(outs[:6]) + (list(outs[6:]),)


def _attn_bwd(q, k, v, do, lse_row, delta_row, ride=()):
    lp = q.shape[1]
    t = _rt(lp, (640, 128))
    nq = lp // t

    nr = len(ride)
    QW = (3, 2, 1)

    def body(q_ref, k_ref, v_ref, do_ref, lse_ref, dl_ref, *rest):
        ps = rest[:nr]
        dq_ref, dk_ref, dv_ref = rest[nr:nr + 3]
        got, sems = rest[nr + 3:2 * nr + 3], rest[2 * nr + 3:]
        kj = pl.program_id(1)
        step = pl.program_id(0) * nq + kj
        if nr:
            pl.when(step == 0)(lambda: _ride_exchange(ps, got, *sems, 0))
        kk = k_ref[0]
        vv = v_ref[0]

        @pl.when(kj == 0)
        def _():
            dq_ref[...] = jnp.zeros_like(dq_ref)

        def tile(qi, carry, masked, width=1):
            dk, dv = carry
            q_rows = pl.ds(pl.multiple_of(qi * t, t), width * t)
            qv = q_ref[0, q_rows, :]
            dob = do_ref[q_rows, :].astype(BF16)
            lse_v = jnp.concatenate([lse_ref[0, qi + b] for b in range(width)], axis=1)
            dl_v = jnp.concatenate([dl_ref[0, qi + b] for b in range(width)], axis=1)
            st = _dot_nt(kk, qv)
            if masked:
                st = jnp.where(_attn_mask(qi * t, kj * t, width * t, t, transposed=True), st, NEG)
            pt = jnp.exp2(st - lse_v)
            dpt = _dot_nt(vv, dob)
            dst = (pt * (dpt - dl_v)).astype(BF16)
            dv = dv + _dot(pt.astype(BF16), dob)
            dk = dk + _dot(dst, qv)
            dq_ref[0, q_rows, :] += _dot_tn(dst, kk)
            return dk, dv

        carry = tile(kj, (jnp.zeros((t, 256), F32), jnp.zeros((t, DV), F32)), True)
        split = jnp.where(kj == 0, nq, kj + 1)

        def span(lo, hi, masked, carry):
            for width in QW:
                reps = (hi - lo) // width
                carry = lax.fori_loop(0, reps, functools.partial(
                    lambda j, c, lo, width: tile(lo + width * j, c, masked, width=width), lo=lo, width=width), carry)
                lo = lo + width * reps
            return carry

        dk, dv = span(split, nq, False, span(kj + 1, split, True, carry))
        dk_ref[0] = dk * LN2
        dv_ref[0] = dv
        if nr:
            pl.when(step == MLA_H * nq - 1)(lambda: _ride_exchange(ps, got, *sems, 1))

    stat = pl.BlockSpec((1, nq, 1, t), lambda h, j: (h, 0, 0, 0))
    sems = [pltpu.SemaphoreType.DMA((3 * nr,)), pltpu.SemaphoreType.DMA((3 * nr,))] if nr else []
    outs = pl.pallas_call(
        body, name="attn_bwd", grid=(MLA_H, nq),
        in_specs=[pl.BlockSpec((1, lp, 256), lambda h, j: (h, 0, 0)), pl.BlockSpec((1, t, 256), lambda h, j: (h, j, 0)),
                  pl.BlockSpec((1, t, 128), lambda h, j: (h, j, 0)), pl.BlockSpec((lp, DV), lambda h, j: (0, h)), stat, stat]
        + [ANY] * nr,
        out_specs=[pl.BlockSpec((1, lp, 256), lambda h, j: (h, 0, 0)), pl.BlockSpec((1, t, 256), lambda h, j: (h, j, 0)),
                   pl.BlockSpec((1, t, 128), lambda h, j: (h, j, 0))] + [ANY] * nr,
        out_shape=[_sds((MLA_H, lp, 256), F32), _sds((MLA_H, lp, 256), F32), _sds((MLA_H, lp, 128), F32)]
        + [_sds((3,) + p.shape[1:], p.dtype) for p in ride],
        scratch_shapes=sems, compiler_params=_cp("arbitrary", "arbitrary"))(q, k, v, do, lse_row, delta_row, *ride)
    return outs[0], outs[1], outs[2], list(outs[3:])


def _ssd_consts():
    ri = lax.broadcasted_iota(jnp.int32, (CHUNK, CHUNK), 0)
    ci = lax.broadcasted_iota(jnp.int32, (CHUNK, CHUNK), 1)
    expand = (lax.broadcasted_iota(jnp.int32, (128, D_SSM), 0)
              == lax.broadcasted_iota(jnp.int32, (128, D_SSM), 1) // SSM_P).astype(F32)
    return ri, ci, expand


def _ssd_chunk(c, x_ref, xh_ref, dt_ref, dtT_ref, cw_ref, cb_ref, dtb_ref, dtbT_ref, al_ref, alT_ref):
    ri, ci, expand = _ssd_consts()
    x = x_ref[...]
    halo = jnp.where(c > 0, xh_ref[...], 0.0)
    sh = [x] + [_shift_down(x, halo, j) for j in range(1, SSM_K)]
    cv = cb_ref[...]
    for kk in range(SSM_K):
        cv = cv + cw_ref[kk:kk + 1, :] * sh[SSM_K - 1 - kk]
    xa = _silu(cv)
    grow = c * CHUNK + ri
    gcol = c * CHUNK + lax.broadcasted_iota(jnp.int32, (SSM_H, CHUNK), 1)
    sp = dt_ref[...] + dtb_ref[...]
    spT = dtT_ref[...] + dtbT_ref[...]
    dtc = jnp.where(grow >= PAD_ROWS, _softplus(sp), 0.0)
    dtr = jnp.where(gcol >= PAD_ROWS, _softplus(spT), 0.0)
    arow = -jnp.exp(al_ref[...])
    acolT = -jnp.exp(alT_ref[...])
    ltri = (ci <= ri).astype(F32)
    acs = _dot_hi(ltri, dtc * arow, split="b")
    acsT = _dot_hi(dtr * acolT, (ri <= ci).astype(F32))
    return dict(x=x, sh=sh, cv=cv, xa=xa, sp=sp, dtc=dtc, arow=arow, acs=acs, acsT=acsT, ri=ri, ci=ci, expand=expand,
                grow=grow)


def _ssd_mats(k, s_prev):
    xa, acs, acsT, expand, ri, ci = k["xa"], k["acs"], k["acsT"], k["expand"], k["ri"], k["ci"]
    xs = xa[:, :D_SSM]
    dt_e = _dot_hi(k["dtc"], expand)
    acs_e = _dot_hi(acs, expand)
    last_e = acs_e[CHUNK - 1:CHUNK, :]
    ea = jnp.exp(acs_e)
    f = jnp.exp(last_e - acs_e)
    cd = jnp.exp(last_e)
    xdt = xs * dt_e
    bm = [xa[:, D_SSM + g * SSM_N:D_SSM + (g + 1) * SSM_N] for g in range(SSM_G)]
    cm = [xa[:, D_SSM + (SSM_G + g) * SSM_N:D_SSM + (SSM_G + g + 1) * SSM_N] for g in range(SSM_G)]
    bmb = [b.astype(BF16) for b in bm]
    cmb = [cc.astype(BF16) for cc in cm]
    cb = [_dot_nt(cmb[g], bmb[g]) for g in range(SSM_G)]
    lam, mm = [], []
    causal = jnp.where(ci <= ri, 0.0, NEG)
    for h in range(SSM_H):
        lam_h = jnp.exp((acs[:, h:h + 1] - acsT[h:h + 1, :]) + causal)
        lam.append(lam_h)
        mm.append(cb[h // (SSM_H // SSM_G)] * lam_h)
    lo = lax.broadcasted_iota(jnp.int32, (CHUNK, 128), 1) < SSM_P
    xdt_h = []
    for h in range(SSM_H):
        pair = xdt[:, (h // 2) * 128:(h // 2 + 1) * 128]
        xdt_h.append(jnp.where(lo if h % 2 == 0 else jnp.logical_not(lo), pair, 0.0).astype(BF16))
    ydiag = jnp.concatenate(
        [_dot(mm[2 * j].astype(BF16), xdt_h[2 * j]) + _dot(mm[2 * j + 1].astype(BF16), xdt_h[2 * j + 1])
         for j in range(SSM_H // 2)], axis=1)
    t_off = [_dot(cmb[g], s_prev[g].astype(BF16)) for g in range(SSM_G)]
    yoff = jnp.concatenate(t_off, axis=1) * ea
    return dict(xs=xs, dt_e=dt_e, acs_e=acs_e, ea=ea, f=f, cd=cd, xdt=xdt, bm=bm, cm=cm, bmb=bmb, cmb=cmb, cb=cb, lam=lam,
                mm=mm, lo=lo, xdt_h=xdt_h, ydiag=ydiag, t_off=t_off, yoff=yoff)


def _ssd_specs(nc, rev):
    ix = (lambda i: nc - 1 - i) if rev else (lambda i: i)
    return [
        pl.BlockSpec((CHUNK, D_XBC), lambda i: (ix(i), 0)),
        pl.BlockSpec((8, D_XBC), lambda i: (jnp.maximum(ix(i) * (CHUNK // 8) - 1, 0), 0)),
        pl.BlockSpec((CHUNK, D_SSM), lambda i: (ix(i), 0)),
        pl.BlockSpec((CHUNK, 128), lambda i: (ix(i), 0)),
        pl.BlockSpec((SSM_H, CHUNK), lambda i: (0, ix(i))),
        _full((8, D_XBC)), _full((1, D_XBC)), _full((1, 128)), _full((SSM_H, 1)), _full((1, 128)), _full((SSM_H, 1)),
        _full((1, D_SSM)), _full((1, D_SSM)),
    ]


def _ssd_fwd(xbc, z, dtr, dtrT, cw, cb, dtb, dtbT, alog, alogT, d_e, ng):
    lp = xbc.shape[0]
    nc = lp // CHUNK

    def body(x_ref, xh_ref, z_ref, dt_ref, dtT_ref, cw_ref, cb_ref, dtb_ref, dtbT_ref, al_ref, alT_ref, de_ref, ng_ref,
             y_ref, st_ref, s_scr):
        c = pl.program_id(0)

        @pl.when(c == 0)
        def _():
            s_scr[...] = jnp.zeros_like(s_scr)

        k = _ssd_chunk(c, x_ref, xh_ref, dt_ref, dtT_ref, cw_ref, cb_ref, dtb_ref, dtbT_ref, al_ref, alT_ref)
        s_prev = [s_scr[g] for g in range(SSM_G)]
        st_ref[0] = s_scr[...]
        m = _ssd_mats(k, s_prev)
        xd = (m["xdt"] * m["f"]).astype(BF16)
        for g in range(SSM_G):
            sl = slice(g * GSZ, (g + 1) * GSZ)
            s_scr[g] = m["cd"][:, sl] * s_prev[g] + _dot(m["bm"][g].T.astype(BF16), xd[:, sl])
        y = m["ydiag"] + m["yoff"] + de_ref[...] * m["xs"]
        u = y * _silu(z_ref[...])
        outs = []
        for g in range(SSM_G):
            ug = u[:, g * GSZ:(g + 1) * GSZ]
            outs.append(ug * lax.rsqrt(jnp.mean(ug * ug, axis=-1, keepdims=True) + EPS))
        y_ref[...] = jnp.concatenate(outs, axis=1) * ng_ref[...]

    return pl.pallas_call(
        body, name="ssd_fwd", grid=(nc,), in_specs=_ssd_specs(nc, False),
        out_specs=[_rows(CHUNK, D_SSM), pl.BlockSpec((1, SSM_G, SSM_N, GSZ), lambda i: (i, 0, 0, 0))],
        out_shape=[_sds((lp, D_SSM), F32), _sds((nc, SSM_G, SSM_N, GSZ), F32)],
        scratch_shapes=[pltpu.VMEM((SSM_G, SSM_N, GSZ), F32)],
        compiler_params=_cp("arbitrary"))(xbc, xbc, z, dtr, dtrT, cw, cb, dtb, dtbT, alog, alogT, d_e, ng)


def _ssd_bwd(dssm, xbc, z, dtr, dtrT, st, cw, cb, dtb, dtbT, alog, alogT, d_e, ng):
    lp = xbc.shape[0]
    nc = lp // CHUNK
    hpg = SSM_H // SSM_G

    def body(dy_ref, x_ref, xh_ref, z_ref, dt_ref, dtT_ref, st_ref, cw_ref, cb_ref, dtb_ref, dtbT_ref, al_ref, alT_ref,
             de_ref, ng_ref, dz_ref, dx_ref, ddt_ref, dcw_ref, dcb_ref, ddtb_ref, dal_ref, dd_ref, dng_ref, ds_scr, nx_scr):
        i = pl.program_id(0)
        c = nc - 1 - i
        first = i == 0

        @pl.when(first)
        def _():
            ds_scr[...] = jnp.zeros_like(ds_scr)
            nx_scr[...] = jnp.zeros_like(nx_scr)

        k = _ssd_chunk(c, x_ref, xh_ref, dt_ref, dtT_ref, cw_ref, cb_ref, dtb_ref, dtbT_ref, al_ref, alT_ref)
        s_prev = [st_ref[0, g] for g in range(SSM_G)]
        m = _ssd_mats(k, s_prev)
        ri, ci, expand = k["ri"], k["ci"], k["expand"]
        xs, acs, acsT = m["xs"], k["acs"], k["acsT"]
        zv = z_ref[...]
        dout = dy_ref[...]
        ngv = ng_ref[...]
        y = m["ydiag"] + m["yoff"] + de_ref[...] * xs
        sz = _silu(zv)
        u = y * sz
        du_parts, dng_parts = [], []
        for g in range(SSM_G):
            sl = slice(g * GSZ, (g + 1) * GSZ)
            dug, dngg = _rms_bwd(u[:, sl], ngv[:, sl], dout[:, sl])
            du_parts.append(dug)
            dng_parts.append(dngg)
        du = jnp.concatenate(du_parts, axis=1)
        _acc_rows(dng_ref, jnp.concatenate(dng_parts, axis=1), first)
        dy = du * sz
        dz_ref[...] = du * y * _dsilu(zv)
        dd_e = jnp.sum(dy * xs, axis=0, keepdims=True)
        _acc_rows(dd_ref, _dot_nt_hi(dd_e, expand), first)
        dxs = de_ref[...] * dy
        dacs_e = dy * m["yoff"]
        dtg = (dy * m["ea"]).astype(BF16)
        dxdt = jnp.zeros_like(xs)
        dlast_e = []
        db, dc, ds_prev = [], [], []
        xd = m["xdt"] * m["f"]
        dxd_all = []
        for g in range(SSM_G):
            sl = slice(g * GSZ, (g + 1) * GSZ)
            dsg = ds_scr[g]
            spb = s_prev[g].astype(BF16)
            dc.append(_dot_nt(dtg[:, sl], spb))
            dsp = _dot(m["cm"][g].T.astype(BF16), dtg[:, sl]) + m["cd"][:, sl] * dsg
            ds_prev.append(dsp)
            dlast_e.append(jnp.sum(dsg * s_prev[g], axis=0, keepdims=True) * m["cd"][:, sl])
            dsb = dsg.astype(BF16)
            db.append(_dot_nt(xd[:, sl].astype(BF16), dsb))
            dxd_all.append(_dot(m["bmb"][g], dsb))
        dxd = jnp.concatenate(dxd_all, axis=1)
        dxdt = dxd * m["f"]
        dff = dxd * xd
        dacs_e = dacs_e - dff
        dlast_row = jnp.concatenate(dlast_e, axis=1) + jnp.sum(dff, axis=0, keepdims=True)
        dacs = jnp.zeros((CHUNK, 128), F32)
        dacs_t = jnp.zeros((CHUNK, CHUNK), F32)
        lane = lax.broadcasted_iota(jnp.int32, (1, 128), 1)
        dgs = [jnp.zeros((CHUNK, CHUNK), F32) for _ in range(SSM_G)]
        dxdt_pairs = []
        for h in range(SSM_H):
            g = h // hpg
            pr = slice((h // 2) * 128, (h // 2 + 1) * 128)
            lo_h = m["lo"] if h % 2 == 0 else jnp.logical_not(m["lo"])
            dyp = jnp.where(lo_h, dy[:, pr], 0.0).astype(BF16)
            dm = _dot_nt(dyp, m["xdt"][:, pr].astype(BF16))
            dgs[g] = dgs[g] + dm * m["lam"][h]
            w_h = dm * m["mm"][h]
            dacs = dacs + jnp.sum(w_h, axis=1, keepdims=True) * (lane == h).astype(F32)
            dacs_t = dacs_t + jnp.where(ri == h, jnp.sum(w_h, axis=0, keepdims=True), 0.0)
            part = _dot_tn(m["mm"][h].astype(BF16), dyp)
            if h % 2 == 0:
                dxdt_pairs.append(part)
            else:
                dxdt_pairs[-1] = dxdt_pairs[-1] + part
        dxdt = dxdt + jnp.concatenate(dxdt_pairs, axis=1)
        for g in range(SSM_G):
            dgb = dgs[g].astype(BF16)
            dc[g] = dc[g] + _dot(dgb, m["bmb"][g])
            db[g] = db[g] + _dot_tn(dgb, m["cmb"][g])
        dacs = dacs - dacs_t.T + _dot_nt_hi(dacs_e, expand)
        dlast = _dot_nt_hi(dlast_row, expand)
        dacs = dacs + jnp.where(ri == CHUNK - 1, dlast, 0.0)
        dxs = dxs + dxdt * m["dt_e"]
        ddt = _dot_nt_hi(dxdt * xs, expand)
        da = _dot_hi((ri <= ci).astype(F32), dacs, split="b")
        ddt = ddt + da * k["arow"]
        dA = jnp.sum(da * k["dtc"], axis=0, keepdims=True)
        _acc_rows(dal_ref, dA * k["arow"], first)
        ddtr = jnp.where(k["grow"] >= PAD_ROWS, ddt * _sigmoid(k["sp"]), 0.0)
        ddt_ref[...] = ddtr
        _acc_rows(ddtb_ref, jnp.sum(ddtr, axis=0, keepdims=True), first)
        for g in range(SSM_G):
            ds_scr[g] = ds_prev[g]
        dxa = jnp.concatenate([dxs] + db + dc, axis=1)
        dcv = dxa * _dsilu(k["cv"])
        _acc_rows(dcb_ref, jnp.sum(dcv, axis=0, keepdims=True), first)
        dcw_rows = [jnp.sum(dcv * k["sh"][SSM_K - 1 - kk], axis=0, keepdims=True) for kk in range(SSM_K)]
        dcw_rows.append(jnp.zeros((8 - SSM_K, D_XBC), F32))
        _acc_rows(dcw_ref, jnp.concatenate(dcw_rows, axis=0), first)
        nxt = nx_scr[...]
        dx = cw_ref[SSM_K - 1:SSM_K, :] * dcv
        for j in range(1, SSM_K):
            dx = dx + cw_ref[SSM_K - 1 - j:SSM_K - j, :] * _shift_up(dcv, nxt, j)
        grow_x = c * CHUNK + lax.broadcasted_iota(jnp.int32, (CHUNK, D_XBC), 0)
        dx_ref[...] = jnp.where(grow_x >= PAD_ROWS, dx, 0.0)
        nx_scr[...] = dcv[:8]

    specs = _ssd_specs(nc, True)
    in_specs = [pl.BlockSpec((CHUNK, D_SSM), lambda i: (nc - 1 - i, 0))] + specs[:5] + [
        pl.BlockSpec((1, SSM_G, SSM_N, GSZ), lambda i: (nc - 1 - i, 0, 0, 0))] + specs[5:]
    rv = lambda w: pl.BlockSpec((CHUNK, w), lambda i: (nc - 1 - i, 0))
    return pl.pallas_call(
        body, name="ssd_bwd", grid=(nc,), in_specs=in_specs,
        out_specs=[rv(D_SSM), rv(D_XBC), rv(128), _full((8, D_XBC)), _full((1, D_XBC)), _full((1, 128)), _full((1, 128)),
                   _full((1, 128)), _full((1, D_SSM))],
        out_shape=[_sds((lp, D_SSM), F32), _sds((lp, D_XBC), F32), _sds((lp, 128), F32), _sds((8, D_XBC), F32),
                   _sds((1, D_XBC), F32), _sds((1, 128), F32), _sds((1, 128), F32), _sds((1, 128), F32), _sds((1, D_SSM), F32)],
        scratch_shapes=[pltpu.VMEM((SSM_G, SSM_N, GSZ), F32), pltpu.VMEM((8, D_XBC), F32)],
        compiler_params=_cp("arbitrary"))(dssm, xbc, xbc, z, dtr, dtrT, st, cw, cb, dtb, dtbT, alog, alogT, d_e, ng)


def _mixout_fwd(o, ssm, h0, g_ao, g_post, w):
    lp = o.shape[0]
    tr = _rt(lp, MM_ROWS)

    def body(o_ref, s_ref, h_ref, ga_ref, gp_ref, w_ref, mi_ref, mix_ref, h1_ref):
        mixin = jnp.concatenate([_rms(o_ref[...], ga_ref[...]), s_ref[...]], axis=1).astype(BF16)
        mi_ref[...] = mixin
        mix = _dot(mixin, w_ref[...])
        mix_ref[...] = mix
        grow = pl.program_id(0) * tr + lax.broadcasted_iota(jnp.int32, (tr, D), 0)
        h1_ref[...] = h_ref[...] + jnp.where(grow >= PAD_ROWS, _rms(mix, gp_ref[...]), 0.0)

    return pl.pallas_call(
        body, name="mixout_fwd", grid=(lp // tr,),
        in_specs=[_rows(tr, D), _rows(tr, D), _rows(tr, D), _full((1, D)), _full((1, D)), _full(w.shape)],
        out_specs=[_rows(tr, 2 * D), _rows(tr, D), _rows(tr, D)],
        out_shape=[_sds((lp, 2 * D), BF16), _sds((lp, D), F32), _sds((lp, D), F32)],
        compiler_params=_cp("parallel"))(o, ssm, h0, g_ao, g_post, w)


def _ffn_up(h1, g, w):
    lp = h1.shape[0]
    tr = _rt(lp, (1664,) + MM_ROWS)
    tn = D_FF // 2

    def body(h_ref, g_ref, w_ref, hn_ref, u_ref):
        @pl.when(pl.program_id(1) == 0)
        def _():
            hn_ref[...] = _rms(h_ref[...], g_ref[...]).astype(BF16)

        u_ref[...] = _dot(hn_ref[...], w_ref[...]).astype(BF16)

    return pl.pallas_call(
        body, name="ffn_up", grid=(lp // tr, 2 * D_FF // tn),
        in_specs=[pl.BlockSpec((tr, D), lambda i, j: (i, 0)), _full((1, D)), pl.BlockSpec((None, D, tn), lambda i, j: (j, 0, 0))],
        out_specs=[pl.BlockSpec((tr, D), lambda i, j: (i, 0)), pl.BlockSpec((tr, tn), lambda i, j: (i, j))],
        out_shape=[_sds((lp, D), BF16), _sds((lp, 2 * D_FF), BF16)],
        compiler_params=_cp("parallel", "arbitrary"))(h1, g, w)


def _ffn_in_bwd(du, w4, h1, g, dh2):
    lp = du.shape[0]
    nch, _, tn = w4.shape
    tr = _rt(lp, (320, 128))

    def body(du_ref, w_ref, h_ref, g_ref, r_ref, o_ref, dg_ref):
        acc = _dot_nt(du_ref[:, 0:tn], w_ref[0])
        for j in range(1, nch):
            acc = acc + _dot_nt(du_ref[:, j * tn:(j + 1) * tn], w_ref[j])
        dx, dg = _rms_bwd(h_ref[...], g_ref[...], acc)
        o_ref[...] = dx + r_ref[...]
        _acc_rows(dg_ref, dg, pl.program_id(0) == 0)

    return pl.pallas_call(
        body, name="ffn_in_bwd", grid=(lp // tr,),
        in_specs=[_rows(tr, nch * tn), _full(w4.shape), _rows(tr, D), _full((1, D)), _rows(tr, D)],
        out_specs=[_rows(tr, D), _full((1, D))], out_shape=[_sds((lp, D), F32), _sds((1, D), F32)],
        compiler_params=_cp("arbitrary"))(du, w4, h1, g, dh2)


FFN_CB = 256


def _ffn_gate(u, cw, cb):
    lp = u.shape[0]
    tr = _rt(lp, (320, 128))

    def body(u_ref, uh_ref, cw_ref, cb_ref, uc_ref, a_ref):
        i = pl.program_id(0)
        for j in range(D_FF // FFN_CB):
            halves = []
            for off in (0, D_FF):
                sl = slice(off + j * FFN_CB, off + (j + 1) * FFN_CB)
                x = u_ref[:, sl].astype(F32)
                halo = jnp.where(i > 0, uh_ref[8:16, sl].astype(F32), 0.0)
                cv = cb_ref[:, sl] + cw_ref[FFN_K - 1:FFN_K, sl] * x
                for s in range(1, FFN_K):
                    cv = cv + cw_ref[FFN_K - 1 - s:FFN_K - s, sl] * _shift_down(x, halo, s)
                uc_ref[:, sl] = cv.astype(BF16)
                halves.append(cv)
            a_ref[:, j * FFN_CB:(j + 1) * FFN_CB] = (_silu(halves[0]) * halves[1]).astype(BF16)

    return pl.pallas_call(
        body, name="ffn_gate", grid=(lp // tr,),
        in_specs=[_rows(tr, 2 * D_FF), pl.BlockSpec((16, 2 * D_FF), lambda i: (jnp.maximum(i * (tr // 16) - 1, 0), 0)),
                  _full((8, 2 * D_FF)), _full((1, 2 * D_FF))],
        out_specs=[_rows(tr, 2 * D_FF), _rows(tr, D_FF)], out_shape=[_sds((lp, 2 * D_FF), BF16), _sds((lp, D_FF), BF16)],
        compiler_params=_cp("parallel"))(u, u, cw, cb)


def _ffn_down(a, w, h1, tgt, g_post):
    lp = a.shape[0]
    tr = _rt(lp, MM_ROWS)
    nb = tr // FRONT

    def body(a_ref, w_ref, h_ref, *rest):
        t_refs, (g_ref, dh2_ref, dd_ref, dg_ref, loss_ref) = rest[:nb], rest[nb:]
        i = pl.program_id(0)
        d = _dot(a_ref[...], w_ref[...])
        gv = g_ref[...]
        h2 = h_ref[...] + _rms(d, gv)
        grow = i * tr + lax.broadcasted_iota(jnp.int32, (tr, D), 0)
        tgt_v = jnp.concatenate([r[...] for r in t_refs], axis=0)
        err = jnp.where(grow >= FRONT, h2 - tgt_v, 0.0)
        dh2 = err * (1.0 / D)
        dh2_ref[...] = dh2
        dd, dg = _rms_bwd(d, gv, dh2)
        dd_ref[...] = dd.astype(BF16)
        _acc_rows(dg_ref, dg, i == 0)
        part = 0.5 * jnp.sum(jnp.sum(err * err, axis=1, keepdims=True), axis=0, keepdims=True) * (1.0 / D)
        _acc_rows(loss_ref, jnp.broadcast_to(part, (8, 128)), i == 0)

    return pl.pallas_call(
        body, name="ffn_down", grid=(lp // tr,),
        in_specs=[_rows(tr, D_FF), _full(w.shape), _rows(tr, D)]
        + [pl.BlockSpec((FRONT, D), functools.partial(lambda i, b: (jnp.maximum(i * nb - 1 + b, 0), 0), b=b)) for b in range(nb)]
        + [_full((1, D))],
        out_specs=[_rows(tr, D), _rows(tr, D), _full((1, D)), _full((8, 128))],
        out_shape=[_sds((lp, D), F32), _sds((lp, D), BF16), _sds((1, D), F32), _sds((8, 128), F32)],
        compiler_params=_cp("arbitrary"))(a, w, h1, *([tgt] * nb), g_post)


def _ffn_gate_bwd(u, uc, dd, w_down, cw):
    lp = u.shape[0]
    tr = _rt(lp, (320, 128))
    n = lp // tr

    def body(u_ref, uc_ref, dd_ref, wd_ref, cw_ref, du_ref, dcw_ref, dcb_ref, nx_scr):
        i = pl.program_id(0)
        t = n - 1 - i
        first = i == 0

        @pl.when(first)
        def _():
            nx_scr[...] = jnp.zeros_like(nx_scr)

        grow = t * tr + lax.broadcasted_iota(jnp.int32, (tr, FFN_CB), 0)
        ddv = dd_ref[...]
        for j in range(D_FF // FFN_CB):
            sls = [slice(off + j * FFN_CB, off + (j + 1) * FFN_CB) for off in (0, D_FF)]
            cvg, cvv = uc_ref[:, sls[0]].astype(F32), uc_ref[:, sls[1]].astype(F32)
            dav = _dot_nt(ddv, wd_ref[j * FFN_CB:(j + 1) * FFN_CB, :])
            dcv = (dav * cvv * _dsilu(cvg), dav * _silu(cvg))
            for hf in range(2):
                sl = sls[hf]
                g = dcv[hf]
                nxt = nx_scr[:, sl]
                ahead = [g] + [_shift_up(g, nxt, s) for s in range(1, FFN_K)]
                x = u_ref[:, sl].astype(F32)
                rows = [jnp.sum(x * ahead[FFN_K - 1 - kk], axis=0, keepdims=True) for kk in range(FFN_K)]
                rows.append(jnp.zeros((8 - FFN_K, FFN_CB), F32))
                upd_w = jnp.concatenate(rows, axis=0)
                upd_b = jnp.sum(g, axis=0, keepdims=True)

                @pl.when(first)
                def _():
                    dcw_ref[:, sl] = upd_w
                    dcb_ref[:, sl] = upd_b

                @pl.when(jnp.logical_not(first))
                def _():
                    dcw_ref[:, sl] += upd_w
                    dcb_ref[:, sl] += upd_b

                du = cw_ref[FFN_K - 1:FFN_K, sl] * g
                for s in range(1, FFN_K):
                    du = du + cw_ref[FFN_K - 1 - s:FFN_K - s, sl] * ahead[s]
                du_ref[:, sl] = jnp.where(grow >= PAD_ROWS, du, 0.0).astype(BF16)
                nx_scr[:, sl] = g[:8]

    wide = pl.BlockSpec((tr, 2 * D_FF), lambda i: (n - 1 - i, 0))
    return pl.pallas_call(
        body, name="ffn_gate_bwd", grid=(n,),
        in_specs=[wide, wide, pl.BlockSpec((tr, D), lambda i: (n - 1 - i, 0)), _full(w_down.shape), _full((8, 2 * D_FF))],
        out_specs=[wide, _full((8, 2 * D_FF)), _full((1, 2 * D_FF))],
        out_shape=[_sds((lp, 2 * D_FF), BF16), _sds((8, 2 * D_FF), F32), _sds((1, 2 * D_FF), F32)],
        scratch_shapes=[pltpu.VMEM((8, 2 * D_FF), F32)],
        compiler_params=_cp("arbitrary"))(u, uc, dd, w_down, cw)


def _mla_bwd(dq, dk, dv, lat, qg, kvg, wq, wkv, cos, sa, sb):
    lp = lat.shape[0]
    tr = _rt(lp, (320, 128))

    def body(dq_ref, dk_ref, dv_ref, lat_ref, qg_ref, kvg_ref, wq_ref, wkv_ref, cos_ref, sa_ref, sb_ref,
             dqf_ref, dkvf_ref, dlat_ref, dqg_ref, dkvg_ref):
        i = pl.program_id(0)
        cos_v, sa_v, sb_v = cos_ref[...], sa_ref[...], sb_ref[...]
        dkpe = jnp.zeros((tr, 128), F32)
        for h in range(MLA_H):
            dqh = dq_ref[h] * SOFTMAX_SCALE
            dqf_ref[:, h * DN:(h + 1) * DN] = dqh[:, :DN].astype(BF16)
            dqf_ref[:, D + h * 128:D + (h + 1) * 128] = _rope_t(dqh[:, DN:], cos_v, sa_v, sb_v).astype(BF16)
            dkh = dk_ref[h]
            dkvf_ref[:, h * DN:(h + 1) * DN] = dkh[:, :DN].astype(BF16)
            dkpe = dkpe + dkh[:, DN:]
            dkvf_ref[:, D + h * DV:D + (h + 1) * DV] = dv_ref[h].astype(BF16)
        dql = _dot_nt(dqf_ref[...], wq_ref[...])
        dkl = _dot_nt(dkvf_ref[...], wkv_ref[...])
        lat_v = lat_ref[...]
        dqc, dqg = _rms_bwd(lat_v[:, :QR], qg_ref[...], dql)
        dkc, dkg = _rms_bwd(lat_v[:, QR:QR + KVR], kvg_ref[...], dkl)
        dlat_ref[:, :QR] = dqc
        dlat_ref[:, QR:QR + KVR] = dkc
        dlat_ref[:, QR + KVR:] = _rope_t(dkpe, cos_v, sa_v, sb_v)
        _acc_rows(dqg_ref, dqg, i == 0)
        _acc_rows(dkvg_ref, dkg, i == 0)

    hb = lambda w: pl.BlockSpec((MLA_H, tr, w), lambda i: (0, i, 0))
    return pl.pallas_call(
        body, name="mla_bwd", grid=(lp // tr,),
        in_specs=[hb(256), hb(256), hb(128), _rows(tr, LAT_W), _full((1, QR)), _full((1, KVR)), _full(wq.shape),
                  _full(wkv.shape), _rows(tr, 128), _rows(tr, 128), _rows(tr, 128)],
        out_specs=[_rows(tr, 2 * D), _rows(tr, 2 * D), _rows(tr, LAT_W), _full((1, QR)), _full((1, KVR))],
        out_shape=[_sds((lp, 2 * D), BF16), _sds((lp, 2 * D), BF16), _sds((lp, LAT_W), F32), _sds((1, QR), F32),
                   _sds((1, KVR), F32)],
        compiler_params=_cp("arbitrary"))(dq, dk, dv, lat, qg, kvg, wq, wkv, cos, sa, sb)


def _inproj_bwd(dlat, dz, dxbc, ddt, w, h0, g, dh1, ride=()):
    lp = h0.shape[0]
    tr = _rt(lp, (320, 128))
    segs = ((0, LAT_W), (LAT_W, LAT_W + D_SSM), (LAT_W + D_SSM, LAT_W + D_SSM + D_XBC), (IN_P - 128, IN_P))
    nr = len(ride)
    steps = lp // tr

    def body(dl_ref, dz_ref, dx_ref, dt_ref, w_ref, h_ref, g_ref, r_ref, *rest):
        ps = rest[:nr]
        o_ref, dg_ref = rest[nr:nr + 2]
        got, sems = rest[nr + 2:2 * nr + 2], rest[2 * nr + 2:]
        step = pl.program_id(0)
        if nr:
            pl.when(step == 0)(lambda: _ride_exchange(ps, got, *sems, 0))
        dhn = jnp.zeros((tr, D), F32)
        for ref, (a, b) in zip((dl_ref, dz_ref, dx_ref, dt_ref), segs):
            dhn = dhn + _dot_nt(ref[...].astype(BF16), w_ref[:, a:b])
        dx, dg = _rms_bwd(h_ref[...], g_ref[...], dhn)
        o_ref[...] = dx + r_ref[...]
        _acc_rows(dg_ref, dg, step == 0)
        if nr:
            pl.when(step == steps - 1)(lambda: _ride_exchange(ps, got, *sems, 1))

    sems = [pltpu.SemaphoreType.DMA((3 * nr,)), pltpu.SemaphoreType.DMA((3 * nr,))] if nr else []
    outs = pl.pallas_call(
        body, name="inproj_bwd", grid=(steps,),
        in_specs=[_rows(tr, LAT_W), _rows(tr, D_SSM), _rows(tr, D_XBC), _rows(tr, 128), _full(w.shape), _rows(tr, D),
                  _full((1, D)), _rows(tr, D)] + [ANY] * nr,
        out_specs=[_rows(tr, D), _full((1, D))] + [ANY] * nr,
        out_shape=[_sds((lp, D), F32), _sds((1, D), F32)] + [_sds((3,) + p.shape[1:], p.dtype) for p in ride],
        scratch_shapes=sems, compiler_params=_cp("arbitrary"))(dlat, dz, dxbc, ddt, w, h0, g, dh1, *ride)
    return outs[0], outs[1], list(outs[2:])


def _rope_tables(lp):
    pos = (jnp.arange(lp, dtype=jnp.int32) - PAD_ROWS).astype(F32)
    inv = ROPE_THETA ** (-jnp.arange(0, DR, 2, dtype=F32) / DR)
    ang = pos[:, None] * inv[None, :]
    cos, sin = jnp.cos(ang), jnp.sin(ang)
    zero = jnp.zeros_like(sin)
    cos128 = jnp.concatenate([cos, cos, cos, cos], axis=1)
    sa128 = jnp.concatenate([-sin, zero, -sin, zero], axis=1)
    sb128 = jnp.concatenate([zero, sin, zero, sin], axis=1)
    return cos128, sa128, sb128


def _pad_rows8(w):
    return jnp.concatenate([w, jnp.zeros((8 - w.shape[0], w.shape[1]), w.dtype)], axis=0)


def _lane_pad(v):
    return jnp.concatenate([v, jnp.zeros((v.shape[0], 128 - v.shape[1]), v.dtype)], axis=1)


def _late_weights(bufs):
    w_out, w_up, w_down = bufs
    return dict(w_out=w_out.reshape(2 * D, D), w_up=w_up.reshape(N_CHIPS, D, 2 * D_FF // N_CHIPS), w_down=w_down.reshape(D_FF, D))


def _device_step(x, tgt, meta, p, late_bufs=(), early_reduce=None, last_reduce=None, mla_bufs=()):
    s = x.shape[0]
    lp = s + FRONT
    zpad = jnp.zeros((PAD_ROWS, D), F32)
    h0 = jnp.concatenate([zpad, meta, x], axis=0)
    cos, sa, sb = _rope_tables(lp)

    w_in = p["w_in"]
    w_in_p = jnp.concatenate([w_in[:, :QR + KVR + DR], jnp.zeros((D, 64), BF16), w_in[:, QR + KVR + DR:],
                              jnp.zeros((D, 128 - SSM_H), BF16)], axis=1)
    scw = _pad_rows8(p["ssm_conv_w"])
    fcw = _pad_rows8(p["ffn_conv_w"])
    dtb, alog = _lane_pad(p["ssm_dt_bias"]), _lane_pad(p["ssm_A_log"])
    dtbT, alogT = p["ssm_dt_bias"].reshape(SSM_H, 1), p["ssm_A_log"].reshape(SSM_H, 1)
    d_e = jnp.repeat(p["ssm_D"], SSM_P, axis=1)

    hn, lat, z, xbc, dtr, mla_gathered = _inproj(h0, p["norm_mix_pre"], w_in_p, ride=mla_bufs)
    if mla_bufs:
        p = dict(p, w_uq=mla_gathered[0].reshape(QR, MLA_H, DN + DR), w_ukv=mla_gathered[1].reshape(KVR, MLA_H, DN + DV))
    w_uq = p["w_uq"]
    wq_p = jnp.concatenate([w_uq[:, :, :DN].reshape(QR, MLA_H * DN),
                            jnp.concatenate([w_uq[:, :, DN:], jnp.zeros((QR, MLA_H, 128 - DR), BF16)], axis=2).reshape(QR, MLA_H * 128)],
                           axis=1)
    w_ukv = p["w_ukv"]
    wkv_p = jnp.concatenate([w_ukv[:, :, :DN].reshape(KVR, MLA_H * DN), w_ukv[:, :, DN:].reshape(KVR, MLA_H * DV)], axis=1)
    dtrT = dtr[:, :SSM_H].T
    q, k, v, qlat, kvlat = _mla_prep(lat, p["q_a_norm"], p["kv_a_norm"], wq_p, wkv_p, cos, sa, sb)
    o, lse, gathered = _attn_fwd(q, k, v, ride=late_bufs)
    if late_bufs:
        p = dict(p, **_late_weights(gathered))
    ssm, st = _ssd_fwd(xbc, z, dtr, dtrT, scw, p["ssm_conv_b"], dtb, dtbT, alog, alogT, d_e, p["ssm_norm"])
    mixin, mix, h1 = _mixout_fwd(o, ssm, h0, p["attn_out_norm"], p["norm_mix_post"], p["w_out"])
    hn2, u = _ffn_up(h1, p["norm_ffn_pre"], p["w_up"])
    uc, a = _ffn_gate(u, fcw, p["ffn_conv_b"])
    dh2, dd, g_ffn_post, loss = _ffn_down(a, p["w_down"], h1, tgt, p["norm_ffn_post"])

    g_w_down = _mm_tn(a, dd, "ffn_dw_down", tn=512)
    du, g_fcw, g_fcb = _ffn_gate_bwd(u, uc, dd, p["w_down"], fcw)
    dh1, g_ffn_pre = _ffn_in_bwd(du, p["w_up"], h1, p["norm_ffn_pre"], dh2)
    g_w_up = _mm_tn(hn2, du, "ffn_dw_up", tn=D_FF // 2, chunked=True)
    ffn_gs = early_reduce.halves(dict(w_up=g_w_up, w_down=g_w_down)) if early_reduce else ()
    dmix, dssm, do, g_mix_post, g_ao, delta, ffn_sib = _mixout_bwd(mix, p["norm_mix_post"], dh1, o, p["attn_out_norm"],
                                                                  p["w_out"], ride=ffn_gs)
    g_w_out = _mm_tn(mixin, dmix, "mix_dw_out", tn=512)
    t = _rt(lp, (640, 128))
    pairs = early_reduce.pairs(dict(w_out=g_w_out), ffn_gs, ffn_sib) if early_reduce else ()
    dq, dk, dv, got = _attn_bwd(q, k, v, do, lse[:, 0, :].reshape(MLA_H, lp // t, 1, t), delta.reshape(MLA_H, lp // t, 1, t),
                                ride=pairs)
    dqf, dkvf, dlat, g_qa, g_kva = _mla_bwd(dq, dk, dv, lat, p["q_a_norm"], p["kv_a_norm"], wq_p, wkv_p, cos, sa, sb)
    g_wq_p = _mm_tn(qlat, dqf, "mla_dw_uq")
    g_wkv_p = _mm_tn(kvlat, dkvf, "mla_dw_ukv")
    dz, dxbc, ddtr, g_scw, g_scb, g_dtb, g_alog, g_dd, g_ssm_norm = _ssd_bwd(
        dssm, xbc, z, dtr, dtrT, st, scw, p["ssm_conv_b"], dtb, dtbT, alog, alogT, d_e, p["ssm_norm"])
    g_in_p = jnp.concatenate([_mm_tn(hn, dlat, "in_dw_lat"), _mm_tn(hn, dz, "in_dw_z"), _mm_tn(hn, dxbc, "in_dw_xbc"),
                              _mm_tn(hn, ddtr, "in_dw_dt")], axis=1)
    g_w_in = jnp.concatenate([g_in_p[:, :QR + KVR + DR], g_in_p[:, LAT_W:LAT_W + D_SSM + D_XBC + SSM_H]], axis=1)
    g_w_uq = jnp.concatenate([g_wq_p[:, :D].reshape(QR, MLA_H, DN), g_wq_p[:, D:].reshape(QR, MLA_H, 128)[:, :, :DR]], axis=2)
    g_w_ukv = jnp.concatenate([g_wkv_p[:, :D].reshape(KVR, MLA_H, DN), g_wkv_p[:, D:].reshape(KVR, MLA_H, DV)], axis=2)
    pairs2 = last_reduce(dict(w_in=g_w_in, w_uq=g_w_uq, w_ukv=g_w_ukv)) if last_reduce else ()
    dh0, g_mix_pre, got2 = _inproj_bwd(dlat, dz, dxbc, ddtr, w_in_p, h0, p["norm_mix_pre"], dh1, ride=pairs2)
    grads = dict(
        norm_mix_pre=g_mix_pre, norm_mix_post=g_mix_post, norm_ffn_pre=g_ffn_pre, norm_ffn_post=g_ffn_post, w_in=g_w_in,
        q_a_norm=g_qa, w_uq=g_w_uq, kv_a_norm=g_kva, w_ukv=g_w_ukv, attn_out_norm=g_ao, ssm_conv_w=g_scw[:SSM_K],
        ssm_conv_b=g_scb, ssm_dt_bias=g_dtb[:, :SSM_H], ssm_A_log=g_alog[:, :SSM_H], ssm_D=g_dd[:, :SSM_H],
        ssm_norm=g_ssm_norm, w_out=g_w_out, w_up=g_w_up, ffn_conv_w=g_fcw[:FFN_K], ffn_conv_b=g_fcb, w_down=g_w_down)
    return loss, dh0[FRONT:], dh0[PAD_ROWS:FRONT], grads, (list(pairs2) + list(pairs), list(got2) + list(got))


N_CHIPS = 4
BIG = (("w_in", (D, D_IN // N_CHIPS)), ("w_uq", (QR // N_CHIPS, MLA_H, DN + DR)), ("w_ukv", (KVR // N_CHIPS, MLA_H, DN + DV)),
       ("w_out", (2 * D // N_CHIPS, D)), ("w_up", (D, 2 * D_FF // N_CHIPS)), ("w_down", (D_FF // N_CHIPS, D)))
SMALL_SHARDED = (("meta_tokens", (N_META, D // N_CHIPS)), ("ssm_conv_w", (SSM_K, D_XBC // N_CHIPS)),
                 ("ffn_conv_w", (FFN_K, 2 * D_FF // N_CHIPS)))
SMALL_REPL = (("norm_mix_pre", D), ("norm_mix_post", D), ("norm_ffn_pre", D), ("norm_ffn_post", D), ("q_a_norm", QR),
              ("kv_a_norm", KVR), ("attn_out_norm", D), ("ssm_conv_b", D_XBC), ("ssm_dt_bias", SSM_H), ("ssm_A_log", SSM_H),
              ("ssm_D", SSM_H), ("ssm_norm", D_SSM), ("ffn_conv_b", 2 * D_FF))
ANY = pl.BlockSpec(memory_space=pl.ANY)


def _pad128(v):
    n = v.shape[0]
    return jnp.concatenate([v, jnp.zeros(((-n) % 128,), v.dtype)]) if n % 128 else v


def _pack_rows(vs, rows):
    flat = jnp.concatenate([_pad128(v.reshape(-1)) for v in vs])
    flat = jnp.concatenate([flat, jnp.zeros((rows * 128 - flat.shape[0],), flat.dtype)])
    return flat.reshape(rows, 128)


def _unpack_rows(pack, sizes):
    flat = pack.reshape(-1)
    out, off = [], 0
    for n in sizes:
        out.append(flat[off:off + n])
        off += n + (-n) % 128
    return out


def _my_place():
    return lax.axis_index("x"), lax.axis_index("y"), lax.axis_index("c")


def _other_chips(x, y):
    return [(1 - x, y), (x, 1 - y), (1 - x, 1 - y)]


def _remote(src, dst, send, recv, dev):
    return pltpu.make_async_remote_copy(src_ref=src, dst_ref=dst, send_sem=send, recv_sem=recv, device_id=dev,
                                        device_id_type=MESH)


SMALL_AG_ROWS = 80


def _gather_weights(shards, small, name):
    arrs = list(shards) + ([] if small is None else [small])
    n, nb = len(arrs), len(shards)

    def body(*refs):
        ins, outs = refs[:n], refs[n:2 * n]
        send, recv, lsem = refs[2 * n:]
        x, y, c = _my_place()
        me = 2 * x + y
        chips = _other_chips(x, y)
        slot = lambda w, chip, cc: outs[w].at[chip, cc] if w < nb else outs[w].at[chip]
        mine = lambda w: slot(w, me, c) if w < nb else ins[w]
        loc = [pltpu.make_async_copy(ins[w], outs[w].at[me], lsem.at[w - nb]) for w in range(nb, n)]
        for cp in loc:
            cp.start()
        sends = []
        for w in range(n):
            for kk, (cx, cy) in enumerate(chips):
                sends.append(_remote(mine(w), slot(w, me, c), send.at[3 * w + kk], recv.at[3 * w + kk], (cx, cy, c)))
        for cp in sends:
            cp.start()
        for w in range(nb):
            for kk, (cx, cy) in enumerate(chips):
                src = 2 * cx + cy
                _remote(mine(w), slot(w, src, c), send.at[3 * w + kk], recv.at[3 * w + kk], (cx, cy, c)).wait_recv()
                fwd = _remote(slot(w, src, c), slot(w, src, c), send.at[3 * (n + w) + kk], recv.at[3 * (n + w) + kk], (x, y, 1 - c))
                fwd.start()
                sends.append(fwd)
        for w in range(n):
            for kk, (cx, cy) in enumerate(chips):
                src = 2 * cx + cy
                if w < nb:
                    _remote(mine(w), slot(w, src, 1 - c), send.at[3 * (n + w) + kk], recv.at[3 * (n + w) + kk],
                            (x, y, 1 - c)).wait_recv()
                else:
                    _remote(ins[w], slot(w, src, c), send.at[3 * w + kk], recv.at[3 * w + kk], (cx, cy, c)).wait_recv()
        for cp in sends:
            cp.wait_send()
        for cp in loc:
            cp.wait()

    return pl.pallas_call(
        body, name=name, in_specs=[ANY] * n, out_specs=[ANY] * n,
        out_shape=[_sds(a.shape, a.dtype) for a in shards] + ([] if small is None else [_sds((N_CHIPS,) + small.shape, small.dtype)]),
        input_output_aliases={w: w for w in range(nb)},
        scratch_shapes=[pltpu.SemaphoreType.DMA((3 * (n + nb),)), pltpu.SemaphoreType.DMA((3 * (n + nb),)),
                        pltpu.SemaphoreType.DMA((max(n - nb, 1),))])(*arrs)


def _place_own(wt, chip, name):
    r, c = wt.shape
    tr = _row_tile(r, c)

    def body(c_ref, w_ref, o_ref):
        o_ref[...] = w_ref[...].astype(BF16)

    return pl.pallas_call(
        body, name=name, out_shape=_sds((N_CHIPS, r, c), BF16),
        grid_spec=pltpu.PrefetchScalarGridSpec(
            num_scalar_prefetch=1, grid=(r // tr,), in_specs=[pl.BlockSpec((tr, c), lambda i, cr: (i, 0))],
            out_specs=pl.BlockSpec((None, tr, c), lambda i, cr: (cr[0], i, 0))),
        compiler_params=_cp("parallel"))(chip, wt)


def _send_sibling_halves(gs, name):
    n = len(gs)

    def body(*refs):
        ins, outs, send, recv = refs[:n], refs[n:2 * n], refs[2 * n], refs[2 * n + 1]
        x, y, c = _my_place()
        cps = [_remote(ins[w].at[:, 1 - c], outs[w], send.at[w], recv.at[w], (x, y, 1 - c)) for w in range(n)]
        for cp in cps:
            cp.start()
        for cp in cps:
            cp.wait()

    return pl.pallas_call(
        body, name=name, in_specs=[ANY] * n, out_specs=[ANY] * n,
        out_shape=[_sds((g.shape[0],) + g.shape[2:], g.dtype) for g in gs],
        scratch_shapes=[pltpu.SemaphoreType.DMA((n,)), pltpu.SemaphoreType.DMA((n,))])(*gs)


def _ride_gather(bufs, send, recv, phase):
    n = len(bufs)
    x, y, c = _my_place()
    me = 2 * x + y
    for w in range(n):
        for kk, (cx, cy) in enumerate(_other_chips(x, y)):
            src = 2 * cx + cy
            out = lambda: _remote(bufs[w].at[me, c], bufs[w].at[me, c], send.at[3 * w + kk], recv.at[3 * w + kk], (cx, cy, c))
            fwd = lambda: _remote(bufs[w].at[src, c], bufs[w].at[src, c], send.at[3 * (n + w) + kk],
                                  recv.at[3 * (n + w) + kk], (x, y, 1 - c))
            if phase == 0:
                out().start()
            elif phase == 1:
                _remote(bufs[w].at[me, c], bufs[w].at[src, c], send.at[3 * w + kk], recv.at[3 * w + kk], (cx, cy, c)).wait_recv()
                fwd().start()
            else:
                _remote(bufs[w].at[me, c], bufs[w].at[src, 1 - c], send.at[3 * (n + w) + kk], recv.at[3 * (n + w) + kk],
                        (x, y, 1 - c)).wait_recv()
                out().wait_send()
                fwd().wait_send()


def _ride_sibling(gs, outs, send, recv, phase):
    x, y, c = _my_place()
    for w in range(len(gs)):
        cp = _remote(gs[w].at[:, 1 - c], outs[w], send.at[w], recv.at[w], (x, y, 1 - c))
        if phase == 0:
            cp.start()
        else:
            cp.wait()


def _ride_exchange(ps, outs, send, recv, phase):
    x, y, c = _my_place()
    for w in range(len(ps)):
        for kk, (cx, cy) in enumerate(_other_chips(x, y)):
            cp = _remote(ps[w].at[2 * cx + cy], outs[w].at[kk], send.at[3 * w + kk], recv.at[3 * w + kk], (cx, cy, c))
            if phase == 0:
                cp.start()
            else:
                cp.wait()


def _share_sibling(halves):
    n = len(halves)

    def body(*refs):
        outs, send, recv = refs[n:2 * n], refs[2 * n], refs[2 * n + 1]
        x, y, c = _my_place()
        cps = [_remote(outs[w].at[c], outs[w].at[c], send.at[w], recv.at[w], (x, y, 1 - c)) for w in range(n)]
        for cp in cps:
            cp.start()
        for w in range(n):
            cps[w].wait_send()
            _remote(outs[w].at[c], outs[w].at[1 - c], send.at[w], recv.at[w], (x, y, 1 - c)).wait_recv()

    return pl.pallas_call(
        body, name="share_sibling", in_specs=[ANY] * n, out_specs=[ANY] * n,
        out_shape=[_sds(h.shape, h.dtype) for h in halves], input_output_aliases={w: w for w in range(n)},
        scratch_shapes=[pltpu.SemaphoreType.DMA((n,)), pltpu.SemaphoreType.DMA((n,))])(*halves)


def _row_tile(r, c, cap=1 << 20):
    return next(t for t in range(r, 0, -1) if r % t == 0 and (t % 8 == 0 or t == r) and t * c * 4 <= cap)


def _add_pair(g, t, core, name):
    _, _, r, c = g.shape
    tr = _row_tile(r, c)

    def body(c_ref, g_ref, t_ref, o_ref):
        o_ref[...] = (g_ref[...] + t_ref[...]).astype(BF16)

    return pl.pallas_call(
        body, name=name, out_shape=_sds(t.shape, BF16),
        grid_spec=pltpu.PrefetchScalarGridSpec(
            num_scalar_prefetch=1, grid=(N_CHIPS, r // tr),
            in_specs=[pl.BlockSpec((None, None, tr, c), lambda j, i, cr: (j, cr[0], i, 0)),
                      pl.BlockSpec((None, tr, c), lambda j, i, cr: (j, i, 0))],
            out_specs=pl.BlockSpec((None, tr, c), lambda j, i, cr: (j, i, 0))),
        compiler_params=_cp("parallel", "parallel"))(core, g, t)


def _add_chips(p, got, chip, name):
    _, r, c = p.shape
    tr = _row_tile(r, c)

    def body(c_ref, p_ref, g_ref, o_ref):
        o_ref[...] = ((p_ref[...].astype(F32) + g_ref[0].astype(F32)) + g_ref[1].astype(F32)) + g_ref[2].astype(F32)

    return pl.pallas_call(
        body, name=name, out_shape=_sds((2, r, c), F32),
        grid_spec=pltpu.PrefetchScalarGridSpec(
            num_scalar_prefetch=1, grid=(r // tr,),
            in_specs=[pl.BlockSpec((None, tr, c), lambda i, cr: (cr[0], i, 0)), pl.BlockSpec((3, tr, c), lambda i, cr: (0, i, 0))],
            out_specs=pl.BlockSpec((None, tr, c), lambda i, cr: (cr[1], i, 0))),
        compiler_params=_cp("parallel"))(chip, p, got)


SMALL_AR_ROWS = 424


def _allreduce_small(v):
    def body(v_ref, o_ref, gath, send, recv):
        x, y, c = _my_place()
        me = 4 * x + 2 * y + c
        gath[me] = v_ref[...]
        cps = []
        for dd in range(1, 8):
            dx, dy, dc = dd >> 2, (dd >> 1) & 1, dd & 1
            peer = (1 - x if dx else x, 1 - y if dy else y, 1 - c if dc else c)
            cps.append(_remote(v_ref, gath.at[me], send.at[dd - 1], recv.at[dd - 1], peer))
        for cp in cps:
            cp.start()
        for cp in cps:
            cp.wait()
        acc = gath[0]
        for dev in range(1, 8):
            acc = acc + gath[dev]
        o_ref[...] = acc

    vm = pl.BlockSpec(memory_space=pltpu.VMEM)
    return pl.pallas_call(
        body, name="allreduce_small", in_specs=[vm], out_specs=vm, out_shape=_sds(v.shape, F32),
        scratch_shapes=[pltpu.VMEM((8,) + v.shape, F32), pltpu.SemaphoreType.DMA((7,)), pltpu.SemaphoreType.DMA((7,))])(v)


def _adamw(w, g, m, v, name):
    r, c = w.shape
    tr = _row_tile(r, c)

    def body(w_ref, g_ref, m_ref, v_ref, d_ref, m2_ref, v2_ref):
        gv = g_ref[...]
        m2 = ADAM_B1 * m_ref[...] + (1.0 - ADAM_B1) * gv
        v2 = ADAM_B2 * v_ref[...] + (1.0 - ADAM_B2) * jnp.square(gv)
        m_hat = m2 / (1.0 - ADAM_B1 ** ADAM_STEP)
        v_hat = v2 / (1.0 - ADAM_B2 ** ADAM_STEP)
        d_ref[...] = -ADAM_LR * (m_hat / (jnp.sqrt(v_hat) + ADAM_EPS) + ADAM_WD * w_ref[...])
        m2_ref[...] = m2
        v2_ref[...] = v2

    return pl.pallas_call(
        body, name=name, grid=(r // tr,), in_specs=[_rows(tr, c)] * 4, out_specs=[_rows(tr, c)] * 3,
        out_shape=[_sds((r, c), F32)] * 3, compiler_params=_cp("parallel"))(w, g, m, v)


WEIGHT_NAMES = ("meta_tokens", "norm_mix_pre", "norm_mix_post", "norm_ffn_pre", "norm_ffn_post", "w_in", "q_a_norm", "w_uq",
                "kv_a_norm", "w_ukv", "attn_out_norm", "ssm_conv_w", "ssm_conv_b", "ssm_dt_bias", "ssm_A_log", "ssm_D",
                "ssm_norm", "w_out", "w_up", "ffn_conv_w", "ffn_conv_b", "w_down")
SMALL_ADAM_ROWS = 192


def kernel(x, meta_tokens, norm_mix_pre, norm_mix_post, norm_ffn_pre, norm_ffn_post, w_in, q_a_norm, w_uq, kv_a_norm, w_ukv, attn_out_norm, ssm_conv_w, ssm_conv_b, ssm_dt_bias, ssm_A_log, ssm_D, ssm_norm, w_out, w_up, ffn_conv_w, ffn_conv_b, w_down, loss_target, m_meta_tokens, m_norm_mix_pre, m_norm_mix_post, m_norm_ffn_pre, m_norm_ffn_post, m_w_in, m_q_a_norm, m_w_uq, m_kv_a_norm, m_w_ukv, m_attn_out_norm, m_ssm_conv_w, m_ssm_conv_b, m_ssm_dt_bias, m_ssm_A_log, m_ssm_D, m_ssm_norm, m_w_out, m_w_up, m_ffn_conv_w, m_ffn_conv_b, m_w_down, v_meta_tokens, v_norm_mix_pre, v_norm_mix_post, v_norm_ffn_pre, v_norm_ffn_post, v_w_in, v_q_a_norm, v_w_uq, v_kv_a_norm, v_w_ukv, v_attn_out_norm, v_ssm_conv_w, v_ssm_conv_b, v_ssm_dt_bias, v_ssm_A_log, v_ssm_D, v_ssm_norm, v_w_out, v_w_up, v_ffn_conv_w, v_ffn_conv_b, v_w_down):
    args = locals()
    w = {n: args[n] for n in WEIGHT_NAMES}
    mom = {n: args["m_" + n] for n in WEIGHT_NAMES}
    var = {n: args["v_" + n] for n in WEIGHT_NAMES}
    cx, cy, cc = _my_place()
    chip = 2 * cx + cy

    two_d = {n: (shp[0], functools.reduce(lambda a, b: a * b, shp[1:])) for n, shp in BIG}
    names = [n for n, _ in BIG]
    core_i = cc.astype(jnp.int32).reshape(1)
    chip_i = chip.astype(jnp.int32).reshape(1)
    early, late = names[:3], names[3:]
    halves = lambda n, a: a.reshape(N_CHIPS, 2, two_d[n][0] // 2, two_d[n][1])
    bufs = {n: halves(n, _place_own(w[n].reshape(two_d[n]), chip_i, "place_" + n)) for n in names}
    small = _pack_rows([w[n] for n, _ in SMALL_SHARDED], SMALL_AG_ROWS)
    w_in_all, small_all = _gather_weights([bufs["w_in"]], small, "allgather_weights")
    p = dict(w_in=w_in_all.reshape((N_CHIPS,) + two_d["w_in"]).transpose(1, 0, 2).reshape(D, D_IN))
    sm_parts = [_unpack_rows(small_all[j], [a * b for _, (a, b) in SMALL_SHARDED]) for j in range(N_CHIPS)]
    for i, (n, shp) in enumerate(SMALL_SHARDED):
        p[n] = jnp.concatenate([sm_parts[j][i].reshape(shp) for j in range(N_CHIPS)], axis=1)
    for n, _ in SMALL_REPL:
        p[n] = w[n]
    meta_full = p.pop("meta_tokens")

    place_i = jnp.stack([chip, cc]).astype(jnp.int32)

    def pair_sums(gd, group):
        gd = dict(gd)
        if "w_in" in gd:
            gd["w_in"] = gd["w_in"].reshape(D, N_CHIPS, D_IN // N_CHIPS).transpose(1, 0, 2)
        gs = [halves(n, gd[n]) for n in group]
        from_sib = _send_sibling_halves(gs, "reduce_sibling_" + group[0])
        return [_add_pair(gg, tt, core_i, "reduce_pair_" + n) for n, gg, tt in zip(group, gs, from_sib)]

    class LateReduce:
        @staticmethod
        def halves(gd):
            return [halves(n, gd[n]) for n in late[1:]]

        @staticmethod
        def pairs(gd, ffn_gs, ffn_sib):
            gs = [halves(late[0], gd[late[0]])]
            sib = _send_sibling_halves(gs, "reduce_sibling_" + late[0])
            return [_add_pair(gg, tt, core_i, "reduce_pair_" + n)
                    for n, gg, tt in zip(late, gs + list(ffn_gs), list(sib) + list(ffn_sib))]

    loss_part, gx, gmeta, g, (pairs, got) = _device_step(
        x[0], loss_target[0], meta_full, p, late_bufs=[bufs[n] for n in late], mla_bufs=[bufs[n] for n in early[1:]],
        early_reduce=LateReduce,
        last_reduce=lambda gd: pair_sums(gd, early))

    small_names = [n for n, _ in SMALL_REPL] + ["ssm_conv_w", "ffn_conv_w"]
    small_sizes = [128] + [sz for _, sz in SMALL_REPL] + [N_META * D, SSM_K * D_XBC, FFN_K * 2 * D_FF]
    order = [n for n, _ in SMALL_REPL]
    sp = _pack_rows([loss_part[0]] + [g[n] for n in order] + [gmeta, g["ssm_conv_w"], g["ffn_conv_w"]], SMALL_AR_ROWS)
    red = _unpack_rows(_allreduce_small(sp), small_sizes)
    loss = red[0][0]
    gfull = {n: red[1 + i].reshape(1, -1) for i, n in enumerate(order)}
    n_r = len(order)
    gfull["meta_tokens"] = lax.dynamic_slice_in_dim(red[1 + n_r].reshape(N_META, D), chip * (D // N_CHIPS), D // N_CHIPS, axis=1)
    gfull["ssm_conv_w"] = lax.dynamic_slice_in_dim(red[2 + n_r].reshape(SSM_K, D_XBC), chip * (D_XBC // N_CHIPS),
                                                   D_XBC // N_CHIPS, axis=1)[None]
    gfull["ffn_conv_w"] = lax.dynamic_slice_in_dim(red[3 + n_r].reshape(FFN_K, 2 * D_FF), chip * (2 * D_FF // N_CHIPS),
                                                   2 * D_FF // N_CHIPS, axis=1)[None]

    mine = [_add_chips(pp, gg, place_i, "reduce_chips_" + n) for n, pp, gg in zip(names, pairs, got)]
    for n, both in zip(names, _share_sibling(mine)):
        gfull[n] = both.reshape(two_d[n])

    delta, new_m, new_v = {}, {}, {}
    for n, shp in BIG:
        outs = _adamw(w[n].reshape(two_d[n]), gfull[n], mom[n].reshape(two_d[n]), var[n].reshape(two_d[n]), "adamw_" + n)
        delta[n], new_m[n], new_v[n] = (o.reshape((1,) + shp) for o in outs)
    snames = order + ["meta_tokens", "ssm_conv_w", "ffn_conv_w"]
    ssizes = [functools.reduce(lambda a, b: a * b, w[n].shape) for n in snames]
    packs = [_pack_rows([d[n] for n in snames], SMALL_ADAM_ROWS) for d in (w, gfull, mom, var)]
    outs = _adamw(*packs, "adamw_small")
    for d, o in zip((delta, new_m, new_v), outs):
        for n, piece in zip(snames, _unpack_rows(o, ssizes)):
            d[n] = piece.reshape(w[n].shape)
    gout = {n: gfull[n].reshape(w[n].shape) for n in WEIGHT_NAMES}
    return (loss, gx[None], *[gout[n] for n in WEIGHT_NAMES], *[delta[n] for n in WEIGHT_NAMES],
            *[new_m[n] for n in WEIGHT_NAMES], *[new_v[n] for n in WEIGHT_NAMES])
```
